```python
import math
import jax
import jax.numpy as jnp
from jax import lax
import numpy as np

D_MODEL = 2048
BATCH = 8
SEQ = 4096
DEPTH = 1

HEAD_DIM = 64
N_Q_HEADS = 16
N_KV_HEADS = 4
GQA_GROUP = N_Q_HEADS // N_KV_HEADS
ATTN_WIDTH = N_Q_HEADS * HEAD_DIM
KV_WIDTH = N_KV_HEADS * HEAD_DIM
WINDOW = 128
BLOCK = 128
NUM_BUCKETS = 32
MAX_DISTANCE = 128
NEG_INF = -1e30
SSM_WIDTH = D_MODEL // 4
SSM_GROUP_CH = 16
SSM_GROUPS = SSM_WIDTH // SSM_GROUP_CH
SSM_STATE = 64
D_FF = 4 * D_MODEL
N_BRANCHES = 2
IN_WIDTH = ATTN_WIDTH + 2 * KV_WIDTH + SSM_WIDTH + N_BRANCHES * D_MODEL
N_MOD = 6
EPS = 1e-6

kernel_name = "hybrid_swa_sink_s5_gated_adaln_block"


def _rmsnorm(x, g):
    xf = x.astype(jnp.float32)
    y = xf * lax.rsqrt(jnp.mean(xf * xf, axis=-1, keepdims=True) + EPS)
    return (y * g.astype(jnp.float32)).astype(x.dtype)


def _modulate(h, shift, scale):
    return h * (1 + scale[:, None, :]) + shift[:, None, :]


def _t5_buckets_block():
    qi = np.arange(BLOCK)[:, None]
    ki = np.arange(2 * BLOCK)[None, :]
    n = np.maximum(qi + BLOCK - ki, 0)
    max_exact = NUM_BUCKETS // 2
    large = max_exact + (np.log(np.maximum(n, 1) / max_exact)
                         / np.log(MAX_DISTANCE / max_exact)
                         * (NUM_BUCKETS - max_exact)).astype(np.int32)
    large = np.minimum(large, NUM_BUCKETS - 1)
    return np.where(n < max_exact, n, large).astype(np.int32)


def _sliding_window_attention(q, k, v, sinks, bias):
    b, s, _ = q.shape
    nb = s // BLOCK
    q = q.reshape(b, nb, BLOCK, N_KV_HEADS, GQA_GROUP, HEAD_DIM)
    pad = ((0, 0), (BLOCK, 0), (0, 0))
    kp = jnp.pad(k, pad).reshape(b, nb + 1, BLOCK, N_KV_HEADS, HEAD_DIM)
    vp = jnp.pad(v, pad).reshape(b, nb + 1, BLOCK, N_KV_HEADS, HEAD_DIM)
    kk = jnp.concatenate([kp[:, :-1], kp[:, 1:]], axis=2)
    vv = jnp.concatenate([vp[:, :-1], vp[:, 1:]], axis=2)
    scores = jnp.einsum('bnqhgd,bnkhd->bnhgqk', q, kk).astype(jnp.float32)
    scores = scores * (HEAD_DIM ** -0.5) + bias
    qi = jnp.arange(BLOCK)[:, None]
    ki = jnp.arange(2 * BLOCK)[None, :]
    dist = qi + BLOCK - ki
    band = (dist >= 0) & (dist < WINDOW)
    blk = jnp.arange(nb)[:, None, None]
    valid = band[None] & (blk * BLOCK + ki[None] - BLOCK >= 0)
    scores = jnp.where(valid[None, :, None, None], scores, NEG_INF)
    sink = sinks.astype(jnp.float32).reshape(N_KV_HEADS, GQA_GROUP, 1)
    m = jnp.maximum(jnp.max(scores, axis=-1), sink)
    p = jnp.exp(scores - m[..., None])
    denom = jnp.sum(p, axis=-1) + jnp.exp(sink - m)
    p = (p / denom[..., None]).astype(vv.dtype)
    o = jnp.einsum('bnhgqk,bnkhd->bnqhgd', p, vv)
    return o.reshape(b, s, ATTN_WIDTH)


def _ssm_combine(e1, e2):
    (a1r, a1i), (b1r, b1i) = e1
    (a2r, a2i), (b2r, b2i) = e2
    a_new = (a1r * a2r - a1i * a2i, a1r * a2i + a1i * a2r)
    b_new = (a2r * b1r - a2i * b1i + b2r, a2r * b1i + a2i * b1r + b2i)
    return (a_new, b_new)


def _s5_ssm(u, lambda_re, lambda_im, log_step, b_re, b_im, c_re, c_im, d_skip):
    bsz, s, _ = u.shape
    f32 = jnp.float32
    uf = u.astype(f32).reshape(bsz, s, SSM_GROUPS, SSM_GROUP_CH)
    lam_re = jnp.minimum(lambda_re.astype(f32), -1e-4)
    lam_im = lambda_im.astype(f32)
    delta = jnp.exp(log_step.astype(f32))[:, None]
    mag = jnp.exp(lam_re * delta)
    ang = lam_im * delta
    abar_re, abar_im = mag * jnp.cos(ang), mag * jnp.sin(ang)
    num_re, num_im = abar_re - 1.0, abar_im
    den = lam_re * lam_re + lam_im * lam_im
    f_re = (num_re * lam_re + num_im * lam_im) / den
    f_im = (num_im * lam_re - num_re * lam_im) / den
    br, bi = b_re.astype(f32), b_im.astype(f32)
    bbar_re = f_re[..., None] * br - f_im[..., None] * bi
    bbar_im = f_re[..., None] * bi + f_im[..., None] * br
    bu_re = jnp.einsum('bsgp,gnp->bsgn', uf, bbar_re)
    bu_im = jnp.einsum('bsgp,gnp->bsgn', uf, bbar_im)
    shape_a = (1, s, SSM_GROUPS, SSM_STATE)
    a_re = jnp.broadcast_to(abar_re, shape_a)
    a_im = jnp.broadcast_to(abar_im, shape_a)
    _, (x_re, x_im) = lax.associative_scan(
        _ssm_combine, ((a_re, a_im), (bu_re, bu_im)), axis=1)
    y = (jnp.einsum('bsgn,gpn->bsgp', x_re, c_re.astype(f32))
         - jnp.einsum('bsgn,gpn->bsgp', x_im, c_im.astype(f32))
         + d_skip.astype(f32).reshape(SSM_GROUPS, SSM_GROUP_CH) * uf)
    return y.reshape(bsz, s, SSM_WIDTH).astype(u.dtype)


def _fwd_setup_inputs(seed: int = 0) -> dict:
    key = jax.random.key(seed)
    ks = jax.random.split(key, 32)

    def nrm(k, shape, scale):
        return jax.random.normal(k, shape, jnp.float32) * scale

    G, N, P = SSM_GROUPS, SSM_STATE, SSM_GROUP_CH
    lam_im0 = jnp.pi * jnp.arange(N, dtype=jnp.float32)
    return {
        "x": nrm(ks[0], (BATCH, SEQ, D_MODEL), 1.0),
        "c": nrm(ks[1], (BATCH, D_MODEL), 1.0),
        "w_ada": nrm(ks[2], (DEPTH, D_MODEL, N_MOD * D_MODEL), 0.5 * D_MODEL ** -0.5),
        "b_ada": nrm(ks[3], (DEPTH, N_MOD * D_MODEL), 0.02),
        "norm1_g": 1.0 + nrm(ks[4], (DEPTH, D_MODEL), 0.02),
        "w_in": nrm(ks[5], (DEPTH, D_MODEL, IN_WIDTH), D_MODEL ** -0.5),
        "b_in": nrm(ks[6], (DEPTH, IN_WIDTH), 0.02),
        "attn_sinks": nrm(ks[7], (DEPTH, N_Q_HEADS), 0.5),
        "rel_bias": nrm(ks[8], (NUM_BUCKETS, N_Q_HEADS), 0.1),
        "lambda_re": -0.5 + nrm(ks[9], (DEPTH, G, N), 0.01),
        "lambda_im": lam_im0 + nrm(ks[10], (DEPTH, G, N), 0.01),
        "log_step": jax.random.uniform(ks[11], (DEPTH, G), jnp.float32,
                                       minval=math.log(1e-3), maxval=math.log(1e-1)),
        "ssm_b_re": nrm(ks[12], (DEPTH, G, N, P), (2 * P) ** -0.5),
        "ssm_b_im": nrm(ks[13], (DEPTH, G, N, P), (2 * P) ** -0.5),
        "ssm_c_re": nrm(ks[14], (DEPTH, G, P, N), (2 * N) ** -0.5),
        "ssm_c_im": nrm(ks[15], (DEPTH, G, P, N), (2 * N) ** -0.5),
        "ssm_d": nrm(ks[16], (DEPTH, SSM_WIDTH), 1.0),
        "w_glu": nrm(ks[17], (DEPTH, SSM_WIDTH, SSM_WIDTH), SSM_WIDTH ** -0.5),
        "b_glu": nrm(ks[18], (DEPTH, SSM_WIDTH), 0.02),
        "w_attn_proj": nrm(ks[19], (DEPTH, ATTN_WIDTH, D_MODEL), ATTN_WIDTH ** -0.5),
        "w_ssm_proj": nrm(ks[20], (DEPTH, SSM_WIDTH, D_MODEL), SSM_WIDTH ** -0.5),
        "w_out": nrm(ks[21], (DEPTH, D_MODEL, D_MODEL), D_MODEL ** -0.5),
        "norm2_g": 1.0 + nrm(ks[22], (DEPTH, D_MODEL), 0.02),
        "w_ff1": nrm(ks[23], (DEPTH, D_MODEL, D_FF), D_MODEL ** -0.5),
        "w_ff2": nrm(ks[24], (DEPTH, D_FF, D_MODEL), D_FF ** -0.5),
        "final_g": 1.0 + nrm(ks[25], (D_MODEL,), 0.02),
    }


def _fwd_reference(x, c, w_ada, b_ada, norm1_g, w_in, b_in, attn_sinks, rel_bias,
              lambda_re, lambda_im, log_step, ssm_b_re, ssm_b_im, ssm_c_re,
              ssm_c_im, ssm_d, w_glu, b_glu, w_attn_proj, w_ssm_proj, w_out,
              norm2_g, w_ff1, w_ff2, final_g):
    bsz, s, _ = x.shape
    buckets = jnp.asarray(_t5_buckets_block())
    bias = rel_bias.astype(jnp.float32)[buckets]
    bias = jnp.transpose(bias, (2, 0, 1)).reshape(N_KV_HEADS, GQA_GROUP, BLOCK, 2 * BLOCK)
    splits = [ATTN_WIDTH, ATTN_WIDTH + KV_WIDTH, ATTN_WIDTH + 2 * KV_WIDTH,
              ATTN_WIDTH + 2 * KV_WIDTH + SSM_WIDTH,
              ATTN_WIDTH + 2 * KV_WIDTH + SSM_WIDTH + D_MODEL]
    cs = jax.nn.silu(c)
    for l in range(DEPTH):
        mod = cs @ w_ada[l] + b_ada[l]
        sh1, sc1, g1, sh2, sc2, g2 = jnp.split(mod, N_MOD, axis=-1)
        h = _modulate(_rmsnorm(x, norm1_g[l]), sh1, sc1)
        proj = h @ w_in[l] + b_in[l]
        q, k, v, u, gate_a, gate_s = jnp.split(proj, splits, axis=-1)
        attn = _sliding_window_attention(q, k, v, attn_sinks[l], bias)
        y_attn = attn @ w_attn_proj[l]
        y = _s5_ssm(u, lambda_re[l], lambda_im[l], log_step[l], ssm_b_re[l],
                    ssm_b_im[l], ssm_c_re[l], ssm_c_im[l], ssm_d[l])
        z = jax.nn.gelu(y)
        z = z * jax.nn.sigmoid(z @ w_glu[l] + b_glu[l])
        y_ssm = z @ w_ssm_proj[l]
        merged = jax.nn.sigmoid(gate_a) * y_attn + jax.nn.sigmoid(gate_s) * y_ssm
        x = x + g1[:, None, :] * (merged @ w_out[l])
        h2 = _modulate(_rmsnorm(x, norm2_g[l]), sh2, sc2)
        ff = jnp.square(jax.nn.relu(h2 @ w_ff1[l])) @ w_ff2[l]
        x = x + g2[:, None, :] * ff
    return _rmsnorm(x, final_g)


import jax as _jax
import jax.numpy as _jnp

TWIN_FORMAT = 'train_step'
FWD_PARAMS = ['x', 'c', 'w_ada', 'b_ada', 'norm1_g', 'w_in', 'b_in', 'attn_sinks', 'rel_bias', 'lambda_re', 'lambda_im', 'log_step', 'ssm_b_re', 'ssm_b_im', 'ssm_c_re', 'ssm_c_im', 'ssm_d', 'w_glu', 'b_glu', 'w_attn_proj', 'w_ssm_proj', 'w_out', 'norm2_g', 'w_ff1', 'w_ff2', 'final_g']
TWIN_WEIGHTS = ['w_ada', 'b_ada', 'norm1_g', 'w_in', 'b_in', 'attn_sinks', 'rel_bias', 'lambda_re', 'lambda_im', 'log_step', 'ssm_b_re', 'ssm_b_im', 'ssm_c_re', 'ssm_c_im', 'ssm_d', 'w_glu', 'b_glu', 'w_attn_proj', 'w_ssm_proj', 'w_out', 'norm2_g', 'w_ff1', 'w_ff2', 'final_g']
TWIN_DIFF_INPUT = 'x'
TWIN_INPUTS = ['x', 'c', 'w_ada', 'b_ada', 'norm1_g', 'w_in', 'b_in', 'attn_sinks', 'rel_bias', 'lambda_re', 'lambda_im', 'log_step', 'ssm_b_re', 'ssm_b_im', 'ssm_c_re', 'ssm_c_im', 'ssm_d', 'w_glu', 'b_glu', 'w_attn_proj', 'w_ssm_proj', 'w_out', 'norm2_g', 'w_ff1', 'w_ff2', 'final_g', 'loss_target', 'm_w_ada', 'm_b_ada', 'm_norm1_g', 'm_w_in', 'm_b_in', 'm_attn_sinks', 'm_rel_bias', 'm_lambda_re', 'm_lambda_im', 'm_log_step', 'm_ssm_b_re', 'm_ssm_b_im', 'm_ssm_c_re', 'm_ssm_c_im', 'm_ssm_d', 'm_w_glu', 'm_b_glu', 'm_w_attn_proj', 'm_w_ssm_proj', 'm_w_out', 'm_norm2_g', 'm_w_ff1', 'm_w_ff2', 'm_final_g', 'v_w_ada', 'v_b_ada', 'v_norm1_g', 'v_w_in', 'v_b_in', 'v_attn_sinks', 'v_rel_bias', 'v_lambda_re', 'v_lambda_im', 'v_log_step', 'v_ssm_b_re', 'v_ssm_b_im', 'v_ssm_c_re', 'v_ssm_c_im', 'v_ssm_d', 'v_w_glu', 'v_b_glu', 'v_w_attn_proj', 'v_w_ssm_proj', 'v_w_out', 'v_norm2_g', 'v_w_ff1', 'v_w_ff2', 'v_final_g']
TWIN_OUTPUTS = ['loss', 'grad_x', 'grad_w_ada', 'grad_b_ada', 'grad_norm1_g', 'grad_w_in', 'grad_b_in', 'grad_attn_sinks', 'grad_rel_bias', 'grad_lambda_re', 'grad_lambda_im', 'grad_log_step', 'grad_ssm_b_re', 'grad_ssm_b_im', 'grad_ssm_c_re', 'grad_ssm_c_im', 'grad_ssm_d', 'grad_w_glu', 'grad_b_glu', 'grad_w_attn_proj', 'grad_w_ssm_proj', 'grad_w_out', 'grad_norm2_g', 'grad_w_ff1', 'grad_w_ff2', 'grad_final_g', 'delta_w_ada', 'delta_b_ada', 'delta_norm1_g', 'delta_w_in', 'delta_b_in', 'delta_attn_sinks', 'delta_rel_bias', 'delta_lambda_re', 'delta_lambda_im', 'delta_log_step', 'delta_ssm_b_re', 'delta_ssm_b_im', 'delta_ssm_c_re', 'delta_ssm_c_im', 'delta_ssm_d', 'delta_w_glu', 'delta_b_glu', 'delta_w_attn_proj', 'delta_w_ssm_proj', 'delta_w_out', 'delta_norm2_g', 'delta_w_ff1', 'delta_w_ff2', 'delta_final_g', 'new_m_w_ada', 'new_m_b_ada', 'new_m_norm1_g', 'new_m_w_in', 'new_m_b_in', 'new_m_attn_sinks', 'new_m_rel_bias', 'new_m_lambda_re', 'new_m_lambda_im', 'new_m_log_step', 'new_m_ssm_b_re', 'new_m_ssm_b_im', 'new_m_ssm_c_re', 'new_m_ssm_c_im', 'new_m_ssm_d', 'new_m_w_glu', 'new_m_b_glu', 'new_m_w_attn_proj', 'new_m_w_ssm_proj', 'new_m_w_out', 'new_m_norm2_g', 'new_m_w_ff1', 'new_m_w_ff2', 'new_m_final_g', 'new_v_w_ada', 'new_v_b_ada', 'new_v_norm1_g', 'new_v_w_in', 'new_v_b_in', 'new_v_attn_sinks', 'new_v_rel_bias', 'new_v_lambda_re', 'new_v_lambda_im', 'new_v_log_step', 'new_v_ssm_b_re', 'new_v_ssm_b_im', 'new_v_ssm_c_re', 'new_v_ssm_c_im', 'new_v_ssm_d', 'new_v_w_glu', 'new_v_b_glu', 'new_v_w_attn_proj', 'new_v_w_ssm_proj', 'new_v_w_out', 'new_v_norm2_g', 'new_v_w_ff1', 'new_v_w_ff2', 'new_v_final_g']
TWIN_LEAF_KINDS = {'loss': 'loss', 'grad_x': 'grad_x', 'grad_w_ada': 'grad_w', 'grad_b_ada': 'grad_w', 'grad_norm1_g': 'grad_w', 'grad_w_in': 'grad_w', 'grad_b_in': 'grad_w', 'grad_attn_sinks': 'grad_w', 'grad_rel_bias': 'grad_w', 'grad_lambda_re': 'grad_w', 'grad_lambda_im': 'grad_w', 'grad_log_step': 'grad_w', 'grad_ssm_b_re': 'grad_w', 'grad_ssm_b_im': 'grad_w', 'grad_ssm_c_re': 'grad_w', 'grad_ssm_c_im': 'grad_w', 'grad_ssm_d': 'grad_w', 'grad_w_glu': 'grad_w', 'grad_b_glu': 'grad_w', 'grad_w_attn_proj': 'grad_w', 'grad_w_ssm_proj': 'grad_w', 'grad_w_out': 'grad_w', 'grad_norm2_g': 'grad_w', 'grad_w_ff1': 'grad_w', 'grad_w_ff2': 'grad_w', 'grad_final_g': 'grad_w', 'delta_w_ada': 'delta_w', 'delta_b_ada': 'delta_w', 'delta_norm1_g': 'delta_w', 'delta_w_in': 'delta_w', 'delta_b_in': 'delta_w', 'delta_attn_sinks': 'delta_w', 'delta_rel_bias': 'delta_w', 'delta_lambda_re': 'delta_w', 'delta_lambda_im': 'delta_w', 'delta_log_step': 'delta_w', 'delta_ssm_b_re': 'delta_w', 'delta_ssm_b_im': 'delta_w', 'delta_ssm_c_re': 'delta_w', 'delta_ssm_c_im': 'delta_w', 'delta_ssm_d': 'delta_w', 'delta_w_glu': 'delta_w', 'delta_b_glu': 'delta_w', 'delta_w_attn_proj': 'delta_w', 'delta_w_ssm_proj': 'delta_w', 'delta_w_out': 'delta_w', 'delta_norm2_g': 'delta_w', 'delta_w_ff1': 'delta_w', 'delta_w_ff2': 'delta_w', 'delta_final_g': 'delta_w', 'new_m_w_ada': 'new_m', 'new_m_b_ada': 'new_m', 'new_m_norm1_g': 'new_m', 'new_m_w_in': 'new_m', 'new_m_b_in': 'new_m', 'new_m_attn_sinks': 'new_m', 'new_m_rel_bias': 'new_m', 'new_m_lambda_re': 'new_m', 'new_m_lambda_im': 'new_m', 'new_m_log_step': 'new_m', 'new_m_ssm_b_re': 'new_m', 'new_m_ssm_b_im': 'new_m', 'new_m_ssm_c_re': 'new_m', 'new_m_ssm_c_im': 'new_m', 'new_m_ssm_d': 'new_m', 'new_m_w_glu': 'new_m', 'new_m_b_glu': 'new_m', 'new_m_w_attn_proj': 'new_m', 'new_m_w_ssm_proj': 'new_m', 'new_m_w_out': 'new_m', 'new_m_norm2_g': 'new_m', 'new_m_w_ff1': 'new_m', 'new_m_w_ff2': 'new_m', 'new_m_final_g': 'new_m', 'new_v_w_ada': 'new_v', 'new_v_b_ada': 'new_v', 'new_v_norm1_g': 'new_v', 'new_v_w_in': 'new_v', 'new_v_b_in': 'new_v', 'new_v_attn_sinks': 'new_v', 'new_v_rel_bias': 'new_v', 'new_v_lambda_re': 'new_v', 'new_v_lambda_im': 'new_v', 'new_v_log_step': 'new_v', 'new_v_ssm_b_re': 'new_v', 'new_v_ssm_b_im': 'new_v', 'new_v_ssm_c_re': 'new_v', 'new_v_ssm_c_im': 'new_v', 'new_v_ssm_d': 'new_v', 'new_v_w_glu': 'new_v', 'new_v_b_glu': 'new_v', 'new_v_w_attn_proj': 'new_v', 'new_v_w_ssm_proj': 'new_v', 'new_v_w_out': 'new_v', 'new_v_norm2_g': 'new_v', 'new_v_w_ff1': 'new_v', 'new_v_w_ff2': 'new_v', 'new_v_final_g': 'new_v'}


def _forward(args):
    return _fwd_reference(*[args[k] for k in FWD_PARAMS])


def _output_shape():
    def fwd():
        inp = _fwd_setup_inputs(0)
        return _fwd_reference(*[inp[k] for k in FWD_PARAMS])
    out = _jax.eval_shape(fwd)
    return out.shape, out.dtype

N_MICROBATCH = 1
ADAM_LR = 0.001
ADAM_B1 = 0.9
ADAM_B2 = 0.999
ADAM_EPS = 1e-08
ADAM_WD = 0.01
ADAM_STEP = 10
PER_EXAMPLE_BATCH_AXIS = {'x': 0, 'c': 0, 'loss_target': 0}
SHARED_INPUTS = []
_WEIGHT_DTYPES = {'w_ada': _jnp.float32, 'b_ada': _jnp.float32, 'norm1_g': _jnp.float32, 'w_in': _jnp.float32, 'b_in': _jnp.float32, 'attn_sinks': _jnp.float32, 'rel_bias': _jnp.float32, 'lambda_re': _jnp.float32, 'lambda_im': _jnp.float32, 'log_step': _jnp.float32, 'ssm_b_re': _jnp.float32, 'ssm_b_im': _jnp.float32, 'ssm_c_re': _jnp.float32, 'ssm_c_im': _jnp.float32, 'ssm_d': _jnp.float32, 'w_glu': _jnp.float32, 'b_glu': _jnp.float32, 'w_attn_proj': _jnp.float32, 'w_ssm_proj': _jnp.float32, 'w_out': _jnp.float32, 'norm2_g': _jnp.float32, 'w_ff1': _jnp.float32, 'w_ff2': _jnp.float32, 'final_g': _jnp.float32}
MOMENT_SCALE = {'w_ada': 3.979926e-02, 'b_ada': 7.524203e-02, 'norm1_g': 8.848333e-03, 'w_in': 5.788354e-03, 'b_in': 1.064791e-02, 'attn_sinks': 3.813384e-03, 'rel_bias': 5.726433e-03, 'lambda_re': 9.538007e-04, 'lambda_im': 9.647809e-04, 'log_step': 3.067835e-01, 'ssm_b_re': 4.734846e-04, 'ssm_b_im': 4.977967e-04, 'ssm_c_re': 9.680543e-04, 'ssm_c_im': 9.859709e-04, 'ssm_d': 1.274397e-02, 'w_glu': 3.585461e-03, 'b_glu': 5.269718e-03, 'w_attn_proj': 5.474126e-03, 'w_ssm_proj': 5.678033e-03, 'w_out': 7.831790e-03, 'norm2_g': 4.006288e-02, 'w_ff1': 2.057052e-02, 'w_ff2': 3.863387e-02, 'final_g': 1.605209e+01}


def _to_microbatches(a, axis):
    t = _jnp.moveaxis(a, axis, 0)
    t = t.reshape((N_MICROBATCH, t.shape[0] // N_MICROBATCH) + t.shape[1:])
    return _jnp.moveaxis(t, 1, axis + 1)


def setup_inputs(seed: int = 0) -> dict:
    inp = _fwd_setup_inputs(seed)
    key = _jax.random.fold_in(_jax.random.key(seed), 7919)
    shape, _ = _output_shape()
    out = dict(inp)
    out["loss_target"] = _jax.random.normal(_jax.random.fold_in(key, 0), shape, _jnp.float32)
    for i, name in enumerate(TWIN_WEIGHTS):
        w = inp[name].astype(_jnp.float32)
        if MOMENT_SCALE is None:
            s = _jnp.sqrt(_jnp.mean(_jnp.square(w)) + 1e-30)
        else:
            s = MOMENT_SCALE[name]
        km, kv = _jax.random.split(_jax.random.fold_in(key, i + 1))
        out[name] = w
        out["m_" + name] = s * _jax.random.normal(km, w.shape, _jnp.float32)
        out["v_" + name] = (s * s) * _jax.random.uniform(kv, w.shape, _jnp.float32, 0.5, 1.5)
    if N_MICROBATCH > 1:
        for name, axis in PER_EXAMPLE_BATCH_AXIS.items():
            out[name] = _to_microbatches(out[name], axis)
    return {'x': out['x'], 'c': out['c'], 'w_ada': out['w_ada'], 'b_ada': out['b_ada'], 'norm1_g': out['norm1_g'], 'w_in': out['w_in'], 'b_in': out['b_in'], 'attn_sinks': out['attn_sinks'], 'rel_bias': out['rel_bias'], 'lambda_re': out['lambda_re'], 'lambda_im': out['lambda_im'], 'log_step': out['log_step'], 'ssm_b_re': out['ssm_b_re'], 'ssm_b_im': out['ssm_b_im'], 'ssm_c_re': out['ssm_c_re'], 'ssm_c_im': out['ssm_c_im'], 'ssm_d': out['ssm_d'], 'w_glu': out['w_glu'], 'b_glu': out['b_glu'], 'w_attn_proj': out['w_attn_proj'], 'w_ssm_proj': out['w_ssm_proj'], 'w_out': out['w_out'], 'norm2_g': out['norm2_g'], 'w_ff1': out['w_ff1'], 'w_ff2': out['w_ff2'], 'final_g': out['final_g'], 'loss_target': out['loss_target'], 'm_w_ada': out['m_w_ada'], 'm_b_ada': out['m_b_ada'], 'm_norm1_g': out['m_norm1_g'], 'm_w_in': out['m_w_in'], 'm_b_in': out['m_b_in'], 'm_attn_sinks': out['m_attn_sinks'], 'm_rel_bias': out['m_rel_bias'], 'm_lambda_re': out['m_lambda_re'], 'm_lambda_im': out['m_lambda_im'], 'm_log_step': out['m_log_step'], 'm_ssm_b_re': out['m_ssm_b_re'], 'm_ssm_b_im': out['m_ssm_b_im'], 'm_ssm_c_re': out['m_ssm_c_re'], 'm_ssm_c_im': out['m_ssm_c_im'], 'm_ssm_d': out['m_ssm_d'], 'm_w_glu': out['m_w_glu'], 'm_b_glu': out['m_b_glu'], 'm_w_attn_proj': out['m_w_attn_proj'], 'm_w_ssm_proj': out['m_w_ssm_proj'], 'm_w_out': out['m_w_out'], 'm_norm2_g': out['m_norm2_g'], 'm_w_ff1': out['m_w_ff1'], 'm_w_ff2': out['m_w_ff2'], 'm_final_g': out['m_final_g'], 'v_w_ada': out['v_w_ada'], 'v_b_ada': out['v_b_ada'], 'v_norm1_g': out['v_norm1_g'], 'v_w_in': out['v_w_in'], 'v_b_in': out['v_b_in'], 'v_attn_sinks': out['v_attn_sinks'], 'v_rel_bias': out['v_rel_bias'], 'v_lambda_re': out['v_lambda_re'], 'v_lambda_im': out['v_lambda_im'], 'v_log_step': out['v_log_step'], 'v_ssm_b_re': out['v_ssm_b_re'], 'v_ssm_b_im': out['v_ssm_b_im'], 'v_ssm_c_re': out['v_ssm_c_re'], 'v_ssm_c_im': out['v_ssm_c_im'], 'v_ssm_d': out['v_ssm_d'], 'v_w_glu': out['v_w_glu'], 'v_b_glu': out['v_b_glu'], 'v_w_attn_proj': out['v_w_attn_proj'], 'v_w_ssm_proj': out['v_w_ssm_proj'], 'v_w_out': out['v_w_out'], 'v_norm2_g': out['v_norm2_g'], 'v_w_ff1': out['v_w_ff1'], 'v_w_ff2': out['v_w_ff2'], 'v_final_g': out['v_final_g']}


def _loss(weights, diff, rest, loss_target):
    with _jax.named_scope("forward"):
        args = {**rest, TWIN_DIFF_INPUT: diff, **{k: w.astype(_WEIGHT_DTYPES[k]) for k, w in weights.items()}}
        y = _forward(args)
    with _jax.named_scope("loss_head"):
        err = _jnp.square(y.astype(_jnp.float32) - loss_target)
        return 0.5 * _jnp.sum(_jnp.mean(err, axis=-1)) if err.ndim else 0.5 * err


def _adamw(w, g, m, v):
    m = ADAM_B1 * m + (1.0 - ADAM_B1) * g
    v = ADAM_B2 * v + (1.0 - ADAM_B2) * _jnp.square(g)
    m_hat = m / (1.0 - ADAM_B1 ** ADAM_STEP)
    v_hat = v / (1.0 - ADAM_B2 ** ADAM_STEP)
    delta = -ADAM_LR * (m_hat / (_jnp.sqrt(v_hat) + ADAM_EPS) + ADAM_WD * w)
    return delta, m, v


def reference(x, c, w_ada, b_ada, norm1_g, w_in, b_in, attn_sinks, rel_bias, lambda_re, lambda_im, log_step, ssm_b_re, ssm_b_im, ssm_c_re, ssm_c_im, ssm_d, w_glu, b_glu, w_attn_proj, w_ssm_proj, w_out, norm2_g, w_ff1, w_ff2, final_g, loss_target, m_w_ada, m_b_ada, m_norm1_g, m_w_in, m_b_in, m_attn_sinks, m_rel_bias, m_lambda_re, m_lambda_im, m_log_step, m_ssm_b_re, m_ssm_b_im, m_ssm_c_re, m_ssm_c_im, m_ssm_d, m_w_glu, m_b_glu, m_w_attn_proj, m_w_ssm_proj, m_w_out, m_norm2_g, m_w_ff1, m_w_ff2, m_final_g, v_w_ada, v_b_ada, v_norm1_g, v_w_in, v_b_in, v_attn_sinks, v_rel_bias, v_lambda_re, v_lambda_im, v_log_step, v_ssm_b_re, v_ssm_b_im, v_ssm_c_re, v_ssm_c_im, v_ssm_d, v_w_glu, v_b_glu, v_w_attn_proj, v_w_ssm_proj, v_w_out, v_norm2_g, v_w_ff1, v_w_ff2, v_final_g):
    given = dict(x=x, c=c, w_ada=w_ada, b_ada=b_ada, norm1_g=norm1_g, w_in=w_in, b_in=b_in, attn_sinks=attn_sinks, rel_bias=rel_bias, lambda_re=lambda_re, lambda_im=lambda_im, log_step=log_step, ssm_b_re=ssm_b_re, ssm_b_im=ssm_b_im, ssm_c_re=ssm_c_re, ssm_c_im=ssm_c_im, ssm_d=ssm_d, w_glu=w_glu, b_glu=b_glu, w_attn_proj=w_attn_proj, w_ssm_proj=w_ssm_proj, w_out=w_out, norm2_g=norm2_g, w_ff1=w_ff1, w_ff2=w_ff2, final_g=final_g, loss_target=loss_target, m_w_ada=m_w_ada, m_b_ada=m_b_ada, m_norm1_g=m_norm1_g, m_w_in=m_w_in, m_b_in=m_b_in, m_attn_sinks=m_attn_sinks, m_rel_bias=m_rel_bias, m_lambda_re=m_lambda_re, m_lambda_im=m_lambda_im, m_log_step=m_log_step, m_ssm_b_re=m_ssm_b_re, m_ssm_b_im=m_ssm_b_im, m_ssm_c_re=m_ssm_c_re, m_ssm_c_im=m_ssm_c_im, m_ssm_d=m_ssm_d, m_w_glu=m_w_glu, m_b_glu=m_b_glu, m_w_attn_proj=m_w_attn_proj, m_w_ssm_proj=m_w_ssm_proj, m_w_out=m_w_out, m_norm2_g=m_norm2_g, m_w_ff1=m_w_ff1, m_w_ff2=m_w_ff2, m_final_g=m_final_g, v_w_ada=v_w_ada, v_b_ada=v_b_ada, v_norm1_g=v_norm1_g, v_w_in=v_w_in, v_b_in=v_b_in, v_attn_sinks=v_attn_sinks, v_rel_bias=v_rel_bias, v_lambda_re=v_lambda_re, v_lambda_im=v_lambda_im, v_log_step=v_log_step, v_ssm_b_re=v_ssm_b_re, v_ssm_b_im=v_ssm_b_im, v_ssm_c_re=v_ssm_c_re, v_ssm_c_im=v_ssm_c_im, v_ssm_d=v_ssm_d, v_w_glu=v_w_glu, v_b_glu=v_b_glu, v_w_attn_proj=v_w_attn_proj, v_w_ssm_proj=v_w_ssm_proj, v_w_out=v_w_out, v_norm2_g=v_norm2_g, v_w_ff1=v_w_ff1, v_w_ff2=v_w_ff2, v_final_g=v_final_g)
    weights = {n: given[n] for n in TWIN_WEIGHTS}
    shared = {n: given[n] for n in SHARED_INPUTS}
    per_example = {n: given[n] for n in ['x', 'c']}
    grad_fn = _jax.value_and_grad(_loss, argnums=(0, 1))

    def one_microbatch(ex, loss_target):
        ex = dict(ex)
        diff = ex.pop(TWIN_DIFF_INPUT)
        return grad_fn(weights, diff, {**shared, **ex}, loss_target)

    if N_MICROBATCH == 1:
        loss, (grad_w, grad_x) = one_microbatch(per_example, given["loss_target"])
    else:
        def body(carry, xs):
            loss_sum, grad_sum = carry
            l_k, (gw_k, gx_k) = one_microbatch(xs[0], xs[1])
            with _jax.named_scope("update"):
                return (loss_sum + l_k, _jax.tree.map(_jnp.add, grad_sum, gw_k)), gx_k

        init = (_jnp.zeros((), _jnp.float32), _jax.tree.map(_jnp.zeros_like, weights))
        (loss, grad_w), grad_x = _jax.lax.scan(body, init, (per_example, given["loss_target"]))
    with _jax.named_scope("update"):
        delta_w, new_m, new_v = {}, {}, {}
        for n in TWIN_WEIGHTS:
            delta_w[n], new_m[n], new_v[n] = _adamw(weights[n], grad_w[n], given["m_" + n], given["v_" + n])
    return (loss, grad_x, *[grad_w[n] for n in TWIN_WEIGHTS], *[delta_w[n] for n in TWIN_WEIGHTS],
            *[new_m[n] for n in TWIN_WEIGHTS], *[new_v[n] for n in TWIN_WEIGHTS])
```

```python
import math

import numpy as np
import jax
import jax.numpy as jnp
from jax import lax
from jax.experimental import pallas as pl
from jax.experimental.pallas import tpu as pltpu

F32 = jnp.float32
BF16 = jnp.bfloat16
MESH = pl.DeviceIdType.MESH

HEAD_DIM = 64
N_Q_HEADS = 16
N_KV_HEADS = 4
GQA_GROUP = N_Q_HEADS // N_KV_HEADS
ATTN_WIDTH = N_Q_HEADS * HEAD_DIM
KV_WIDTH = N_KV_HEADS * HEAD_DIM
BLOCK = 128
NUM_BUCKETS = 32
MAX_DISTANCE = 128
NEG_INF = -1e30
SSM_GROUP_CH = 16
SSM_STATE = 64
EPS = 1e-6
ADAM_LR = 0.001
ADAM_B1 = 0.9
ADAM_B2 = 0.999
ADAM_EPS = 1e-08
ADAM_WD = 0.01
ADAM_STEP = 10

N_CHIPS = 4
N_DEV = 8
SCAN_CHUNKS = 8
VMEM_LIMIT_BYTES = 48 * 1024 * 1024


def _cparams(sem=None):
    return pltpu.CompilerParams(dimension_semantics=sem, vmem_limit_bytes=VMEM_LIMIT_BYTES)


class _Op:
    def __init__(self, arr, nsh=None, coff=0):
        self.arr, self.nsh, self.coff = arr, nsh, coff
        if nsh is None:
            self.rows, self.cols = arr.shape
        else:
            assert arr.shape[0] == nsh
            self.rows, self.cols = arr.shape[1], arr.shape[2] * nsh

    def spec(self, br, bc, idx):
        assert self.coff % bc == 0
        off = self.coff // bc
        if self.nsh is None:
            return pl.BlockSpec((br, bc), lambda *g: (idx(*g)[0], idx(*g)[1] + off))
        per = (self.cols // self.nsh) // bc
        assert per * bc * self.nsh == self.cols

        def imap(*g):
            r, c = idx(*g)
            c = c + off
            return (c // per, r, c % per)
        return pl.BlockSpec((None, br, bc), imap)


def _as_op(a):
    return a if isinstance(a, _Op) else _Op(a)


def _mm(a, b, mode, *, name, M, N, K, out_dtypes=(F32,), out_nsh=None, epilogue=None, extras=(),
        a_fn=None, ti=1024, tj=512, tk=2048):
    a, b = _as_op(a), _as_op(b)
    ti, tj, tk = min(ti, M), min(tj, N), min(tk, K)
    a_w = a.cols // a.nsh if a.nsh else None
    b_w = b.cols // b.nsh if b.nsh else None
    if a_w:
        ti, tk = (min(ti, a_w), tk) if mode == "TN" else (ti, min(tk, a_w))
    if b_w:
        tj, tk = (tj, min(tk, b_w)) if mode == "NT" else (min(tj, b_w), tk)
    if out_nsh:
        tj = min(tj, N // out_nsh)
    assert M % ti == 0 and N % tj == 0 and K % tk == 0, (name, M, N, K, ti, tj, tk)
    nk = K // tk
    if mode == "NN":
        a_spec = a.spec(ti, tk, lambda i, j, k: (i, k))
        b_spec = b.spec(tk, tj, lambda i, j, k: (k, j))
        dims = (((1,), (0,)), ((), ()))
    elif mode == "NT":
        a_spec = a.spec(ti, tk, lambda i, j, k: (i, k))
        b_spec = b.spec(tj, tk, lambda i, j, k: (j, k))
        dims = (((1,), (1,)), ((), ()))
    else:
        a_spec = a.spec(tk, ti, lambda i, j, k: (k, i))
        b_spec = b.spec(tk, tj, lambda i, j, k: (k, j))
        dims = (((0,), (0,)), ((), ()))
    ex_specs, ex_arrs = [], []
    for op, kind in extras:
        op = _as_op(op)
        if kind == "tile":
            ex_specs.append(op.spec(ti, tj, lambda i, j, k: (i, j)))
        else:
            ex_specs.append(op.spec(1, tj, lambda i, j, k: (0, j)))
        ex_arrs.append(op.arr)
    ne, no = len(ex_arrs), len(out_dtypes)
    if out_nsh is None:
        out_shapes = [jax.ShapeDtypeStruct((M, N), d) for d in out_dtypes]
        out_specs = [pl.BlockSpec((ti, tj), lambda i, j, k: (i, j)) for _ in out_dtypes]
    else:
        per = (N // out_nsh) // tj
        assert per * tj * out_nsh == N
        out_shapes = [jax.ShapeDtypeStruct((out_nsh, M, N // out_nsh), d) for d in out_dtypes]
        out_specs = [pl.BlockSpec((None, ti, tj), lambda i, j, k: (j // per, i, j % per)) for _ in out_dtypes]

    def body(a_ref, b_ref, *rest):
        ex_refs, out_refs, acc = rest[:ne], rest[ne:ne + no], rest[ne + no]
        k = pl.program_id(2)

        @pl.when(k == 0)
        def _():
            acc[...] = jnp.zeros_like(acc)

        av = a_ref[...]
        if a_fn is not None:
            av = a_fn(av)
        acc[...] += lax.dot_general(av.astype(BF16), b_ref[...].astype(BF16), dims,
                                    preferred_element_type=F32)

        @pl.when(k == nk - 1)
        def _():
            res = acc[...]
            outs = epilogue(res, *[r[...] for r in ex_refs]) if epilogue is not None else (res,)
            for o_ref, o in zip(out_refs, outs):
                o_ref[...] = o.astype(o_ref.dtype)

    outs = pl.pallas_call(
        body, name=name, grid=(M // ti, N // tj, nk),
        in_specs=[a_spec, b_spec] + ex_specs, out_specs=out_specs, out_shape=out_shapes,
        scratch_shapes=[pltpu.VMEM((ti, tj), F32)],
        compiler_params=_cparams(("parallel", "parallel", "arbitrary")),
    )(a.arr, b.arr, *ex_arrs)
    return outs[0] if no == 1 else outs


def _rowwise(fn, ins, outs, accs, *, name, rows, tr=256):
    tr = min(tr, rows)
    assert rows % tr == 0
    in_specs, arrs = [], []
    for op, kind, width in ins:
        op = _as_op(op)
        if kind == "tile":
            in_specs.append(op.spec(tr, width, lambda i: (i, 0)))
        else:
            in_specs.append(op.spec(op.rows, width, lambda i: (0, 0)))
        arrs.append(op.arr)
    ni, no, na = len(ins), len(outs), len(accs)
    out_shapes = [jax.ShapeDtypeStruct((rows, w), d) for w, d in outs]
    out_specs = [pl.BlockSpec((tr, w), lambda i: (i, 0)) for w, _ in outs]
    out_shapes += [jax.ShapeDtypeStruct((1, w), F32) for w in accs]
    out_specs += [pl.BlockSpec((1, w), lambda i: (0, 0)) for w in accs]

    def body(*refs):
        in_refs, out_refs, acc_refs = refs[:ni], refs[ni:ni + no], refs[ni + no:]
        res = fn(*[r[...] for r in in_refs])
        if not isinstance(res, (tuple, list)):
            res = (res,)
        for o_ref, r in zip(out_refs, res[:no]):
            o_ref[...] = r.astype(o_ref.dtype)
        if na:
            @pl.when(pl.program_id(0) == 0)
            def _():
                for a_ref in acc_refs:
                    a_ref[...] = jnp.zeros_like(a_ref)
            for a_ref, r in zip(acc_refs, res[no:]):
                a_ref[...] += r.astype(F32)

    res = pl.pallas_call(
        body, name=name, grid=(rows // tr,), in_specs=in_specs, out_specs=out_specs, out_shape=out_shapes,
        compiler_params=_cparams(("arbitrary",)),
    )(*arrs)
    return res


def _norm_mod(x, g, sh, sc):
    y = x * lax.rsqrt(jnp.mean(x * x, axis=-1, keepdims=True) + EPS) * g
    return y * (1.0 + sc) + sh


def _sigmoid(x):
    return 1.0 / (1.0 + jnp.exp(-x))


def _silu(x):
    return x * _sigmoid(x)


def _gelu(x):
    return 0.5 * x * (1.0 + jnp.tanh(math.sqrt(2.0 / math.pi) * (x + 0.044715 * (x * x * x))))


def _merge(ga, gs, ya, ys):
    return _sigmoid(ga) * ya + _sigmoid(gs) * ys


def _attn_head(q, kp, kc, vp, vc, sink, bias_p, bias_c, not_first):
    nt = (((1,), (1,)), ((), ()))
    nn = (((1,), (0,)), ((), ()))
    qb = q.astype(BF16)
    scale = HEAD_DIM ** -0.5
    sp = lax.dot_general(qb, kp.astype(BF16), nt, preferred_element_type=F32) * scale + bias_p
    sc = lax.dot_general(qb, kc.astype(BF16), nt, preferred_element_type=F32) * scale + bias_c
    qi = lax.broadcasted_iota(jnp.int32, (BLOCK, BLOCK), 0)
    ki = lax.broadcasted_iota(jnp.int32, (BLOCK, BLOCK), 1)
    sp = jnp.where(jnp.logical_and(ki > qi, not_first), sp, NEG_INF)
    sc = jnp.where(ki <= qi, sc, NEG_INF)
    m = jnp.maximum(jnp.maximum(jnp.max(sp, axis=-1, keepdims=True), jnp.max(sc, axis=-1, keepdims=True)), sink)
    m = lax.stop_gradient(m)
    pp = jnp.exp(sp - m)
    pc = jnp.exp(sc - m)
    denom = jnp.sum(pp, axis=-1, keepdims=True) + jnp.sum(pc, axis=-1, keepdims=True) + jnp.exp(sink - m)
    o = lax.dot_general((pp / denom).astype(BF16), vp.astype(BF16), nn, preferred_element_type=F32)
    o = o + lax.dot_general((pc / denom).astype(BF16), vc.astype(BF16), nn, preferred_element_type=F32)
    return o


def _attn_fwd(qh, kh, vh, sinks, bias, name):
    s = qh.shape[1]
    nb = s // BLOCK

    def body(q_ref, kp_ref, kc_ref, vp_ref, vc_ref, sink_ref, bias_ref, o_ref):
        not_first = pl.program_id(0) > 0
        for h in range(N_Q_HEADS):
            kv = h // GQA_GROUP
            o = _attn_head(q_ref[h], kp_ref[kv], kc_ref[kv], vp_ref[kv], vc_ref[kv], sink_ref[h:h + 1, 0:1],
                           bias_ref[h, :, 0:BLOCK], bias_ref[h, :, BLOCK:2 * BLOCK], not_first)
            o_ref[h] = o.astype(o_ref.dtype)

    cur = lambda i: (0, i, 0)
    prev = lambda i: (0, jnp.maximum(i - 1, 0), 0)
    return pl.pallas_call(
        body, name=name, grid=(nb,),
        in_specs=[pl.BlockSpec((N_Q_HEADS, BLOCK, HEAD_DIM), cur),
                  pl.BlockSpec((N_KV_HEADS, BLOCK, HEAD_DIM), prev), pl.BlockSpec((N_KV_HEADS, BLOCK, HEAD_DIM), cur),
                  pl.BlockSpec((N_KV_HEADS, BLOCK, HEAD_DIM), prev), pl.BlockSpec((N_KV_HEADS, BLOCK, HEAD_DIM), cur),
                  pl.BlockSpec((N_Q_HEADS, 128), lambda i: (0, 0)),
                  pl.BlockSpec((N_Q_HEADS, BLOCK, 2 * BLOCK), lambda i: (0, 0, 0))],
        out_specs=pl.BlockSpec((N_Q_HEADS, BLOCK, HEAD_DIM), cur),
        out_shape=jax.ShapeDtypeStruct((N_Q_HEADS, s, HEAD_DIM), BF16),
        compiler_params=_cparams(("arbitrary",)),
    )(qh, kh, kh, vh, vh, sinks, bias)


def _attn_bwd(qh, kh, vh, doh, sinks, bias, name):
    s = qh.shape[1]
    nb = s // BLOCK
    G = GQA_GROUP

    def body(q_ref, kp_ref, kc_ref, vp_ref, vc_ref, do_ref, sink_ref, bias_ref,
             dq_ref, dk_ref, dv_ref, dsink_ref, dbias_ref, ck, cv):
        i = pl.program_id(1)

        @pl.when(i == 0)
        def _():
            dsink_ref[...] = jnp.zeros_like(dsink_ref)
            dbias_ref[...] = jnp.zeros_like(dbias_ref)
            ck[...] = jnp.zeros_like(ck)
            cv[...] = jnp.zeros_like(cv)

        @pl.when(i < nb)
        def _():
            not_first = i > 0
            kp, kc, vp, vc = kp_ref[...], kc_ref[...], vp_ref[...], vc_ref[...]
            dkp = jnp.zeros((BLOCK, HEAD_DIM), F32)
            dkc = jnp.zeros((BLOCK, HEAD_DIM), F32)
            dvp = jnp.zeros((BLOCK, HEAD_DIM), F32)
            dvc = jnp.zeros((BLOCK, HEAD_DIM), F32)
            rows = lax.broadcasted_iota(jnp.int32, (8, 128), 0)
            dsink_blk = jnp.zeros((8, 128), F32)
            for g in range(G):
                sink = sink_ref[g:g + 1, 0:1]
                bp, bc = bias_ref[g, :, 0:BLOCK], bias_ref[g, :, BLOCK:2 * BLOCK]
                _, vjp = jax.vjp(lambda q, a, b, c, d, sk, e, f: _attn_head(q, a, b, c, d, sk, e, f, not_first),
                                 q_ref[g], kp, kc, vp, vc, sink, bp, bc)
                dq, a, b, c, d, dsk, dbp, dbc = vjp(do_ref[g].astype(F32))
                dq_ref[g] = dq
                dkp, dkc, dvp, dvc = dkp + a, dkc + b, dvp + c, dvc + d
                dsink_blk = dsink_blk + jnp.where(rows == g, jnp.broadcast_to(dsk, (8, 128)), 0.0)
                dbias_ref[g, :, 0:BLOCK] += dbp
                dbias_ref[g, :, BLOCK:2 * BLOCK] += dbc
            dsink_ref[...] += dsink_blk
            dk_ref[...] = ck[...] + dkp
            dv_ref[...] = cv[...] + dvp
            ck[...] = dkc
            cv[...] = dvc

        @pl.when(i == nb)
        def _():
            dk_ref[...] = ck[...]
            dv_ref[...] = cv[...]

    qcur = lambda kv, i: (kv, jnp.minimum(i, nb - 1), 0)
    kcur = lambda kv, i: (kv, jnp.minimum(i, nb - 1), 0)
    kprev = lambda kv, i: (kv, jnp.clip(i - 1, 0, nb - 1), 0)
    qspec = pl.BlockSpec((G, BLOCK, HEAD_DIM), qcur)
    kc_spec = pl.BlockSpec((None, BLOCK, HEAD_DIM), kcur)
    kp_spec = pl.BlockSpec((None, BLOCK, HEAD_DIM), kprev)
    return pl.pallas_call(
        body, name=name, grid=(N_KV_HEADS, nb + 1),
        in_specs=[qspec, kp_spec, kc_spec, kp_spec, kc_spec, qspec,
                  pl.BlockSpec((None, 8, 128), lambda kv, i: (kv, 0, 0)),
                  pl.BlockSpec((G, BLOCK, 2 * BLOCK), lambda kv, i: (kv, 0, 0))],
        out_specs=[qspec, kp_spec, kp_spec,
                   pl.BlockSpec((None, 8, 128), lambda kv, i: (kv, 0, 0)),
                   pl.BlockSpec((G, BLOCK, 2 * BLOCK), lambda kv, i: (kv, 0, 0))],
        out_shape=[jax.ShapeDtypeStruct((N_Q_HEADS, s, HEAD_DIM), F32),
                   jax.ShapeDtypeStruct((N_KV_HEADS, s, HEAD_DIM), F32),
                   jax.ShapeDtypeStruct((N_KV_HEADS, s, HEAD_DIM), F32),
                   jax.ShapeDtypeStruct((N_KV_HEADS, 8, 128), F32),
                   jax.ShapeDtypeStruct((N_Q_HEADS, BLOCK, 2 * BLOCK), F32)],
        scratch_shapes=[pltpu.VMEM((BLOCK, HEAD_DIM), F32), pltpu.VMEM((BLOCK, HEAD_DIM), F32)],
        compiler_params=_cparams(("arbitrary", "arbitrary")),
    )(qh, kh, kh, vh, vh, doh, sinks, bias)


def _cmul(ar, ai, br, bi):
    return ar * br - ai * bi, ar * bi + ai * br


def _scan(a, b, xs_prev, *, reverse, name, tc):
    _, s, c = b.shape
    nc = SCAN_CHUNKS
    steps = s // nc
    with_da = xs_prev is not None
    unroll = 8 if steps % 8 == 0 else 1

    def shift(v, d):
        row = lax.broadcasted_iota(jnp.int32, v.shape, 0)
        if reverse:
            return jnp.where(row < nc - d, pltpu.roll(v, nc - d, 0), 0.0)
        return jnp.where(row >= d, pltpu.roll(v, d, 0), 0.0)

    def body(*refs):
        if with_da:
            a_ref, b_ref, xp_ref, x_ref, da_ref = refs
        else:
            a_ref, b_ref, x_ref = refs
        ar = jnp.broadcast_to(a_ref[0], (nc, tc))
        ai = jnp.broadcast_to(a_ref[1], (nc, tc))

        def row_of(step):
            j = (steps - 1 - step) if reverse else step
            return pl.multiple_of(j * nc, nc)

        def p1(step, st):
            sr, si = st
            r0 = row_of(step)
            mr, mi = _cmul(ar, ai, sr, si)
            sr = mr + b_ref[0, pl.ds(r0, nc), :]
            si = mi + b_ref[1, pl.ds(r0, nc), :]
            x_ref[0, pl.ds(r0, nc), :] = sr
            x_ref[1, pl.ds(r0, nc), :] = si
            return sr, si
        zero = jnp.zeros((nc, tc), F32)
        er, ei = lax.fori_loop(0, steps, p1, (zero, zero), unroll=unroll)

        def pw(step, st):
            return _cmul(ar, ai, *st)
        pr, pi_ = lax.fori_loop(0, steps, pw, (jnp.ones((nc, tc), F32), zero), unroll=unroll)
        cr, ci = shift(er, 1), shift(ei, 1)
        d = 1
        while d < nc:
            mr, mi = _cmul(pr, pi_, shift(cr, d), shift(ci, d))
            cr, ci = cr + mr, ci + mi
            pr, pi_ = _cmul(pr, pi_, pr, pi_)
            d *= 2

        def p2(step, st):
            qr, qi, dar, dai = st
            r0 = row_of(step)
            qr, qi = _cmul(ar, ai, qr, qi)
            fr, fi = _cmul(qr, qi, cr, ci)
            xr = x_ref[0, pl.ds(r0, nc), :] + fr
            xi = x_ref[1, pl.ds(r0, nc), :] + fi
            x_ref[0, pl.ds(r0, nc), :] = xr
            x_ref[1, pl.ds(r0, nc), :] = xi
            if with_da:
                jm = jnp.where(step == steps - 1, steps - 1, steps - 2 - step)
                rp = pl.multiple_of(jm * nc, nc)
                vr, vi = xp_ref[0, pl.ds(rp, nc), :], xp_ref[1, pl.ds(rp, nc), :]
                row = lax.broadcasted_iota(jnp.int32, (nc, tc), 0)
                first = step == steps - 1
                sel = jnp.logical_and(first, row == 0)
                vr = jnp.where(sel, 0.0, jnp.where(first, pltpu.roll(vr, 1, 0), vr))
                vi = jnp.where(sel, 0.0, jnp.where(first, pltpu.roll(vi, 1, 0), vi))
                dar = dar + xr * vr + xi * vi
                dai = dai + xi * vr - xr * vi
            return qr, qi, dar, dai
        _, _, dar, dai = lax.fori_loop(0, steps, p2, (jnp.ones((nc, tc), F32), zero, zero, zero), unroll=unroll)
        if with_da:
            da_ref[0] = jnp.sum(dar, axis=0, keepdims=True)
            da_ref[1] = jnp.sum(dai, axis=0, keepdims=True)

    blk = pl.BlockSpec((2, s, tc), lambda i: (0, 0, i))
    vec = pl.BlockSpec((2, 1, tc), lambda i: (0, 0, i))
    in_specs, args = [vec, blk], [a, b]
    out_specs, out_shape = [blk], [jax.ShapeDtypeStruct((2, s, c), F32)]
    if with_da:
        in_specs.append(blk)
        args.append(xs_prev)
        out_specs.append(vec)
        out_shape.append(jax.ShapeDtypeStruct((2, 1, c), F32))
    res = pl.pallas_call(
        body, name=name, grid=(c // tc,), in_specs=in_specs, out_specs=out_specs, out_shape=out_shape,
        compiler_params=_cparams(("arbitrary",)),
    )(*args)
    return res if with_da else res[0]


def _adamw(w, g, m, v, name):
    r, c = w.shape
    tr = r
    for cand in (512, 256, 128, 64, 32, 16, 8):
        if r % cand == 0 and cand * c * 4 <= 2 * 1024 * 1024:
            tr = cand
            break

    def body(w_ref, g_ref, m_ref, v_ref, d_ref, nm_ref, nv_ref):
        gv = g_ref[...]
        nm = ADAM_B1 * m_ref[...] + (1.0 - ADAM_B1) * gv
        nv = ADAM_B2 * v_ref[...] + (1.0 - ADAM_B2) * (gv * gv)
        m_hat = nm / (1.0 - ADAM_B1 ** ADAM_STEP)
        v_hat = nv / (1.0 - ADAM_B2 ** ADAM_STEP)
        d_ref[...] = -ADAM_LR * (m_hat / (jnp.sqrt(v_hat) + ADAM_EPS) + ADAM_WD * w_ref[...])
        nm_ref[...] = nm
        nv_ref[...] = nv

    spec = pl.BlockSpec((tr, c), lambda i: (i, 0))
    sds = jax.ShapeDtypeStruct((r, c), F32)
    return pl.pallas_call(body, name=name, grid=(r // tr,), in_specs=[spec] * 4, out_specs=[spec] * 3,
                          out_shape=[sds] * 3, compiler_params=_cparams(("parallel",)))(w, g, m, v)


def _sum_lead(x, name, out_dtype=F32):
    n, r, c = x.shape
    tr = r
    for cand in (512, 256, 128, 64, 32, 16, 8):
        if r % cand == 0 and n * cand * c * 4 <= 4 * 1024 * 1024:
            tr = cand
            break

    def body(x_ref, o_ref):
        acc = x_ref[0].astype(F32)
        for k in range(1, n):
            acc = acc + x_ref[k].astype(F32)
        o_ref[...] = acc.astype(o_ref.dtype)

    return pl.pallas_call(body, name=name, grid=(r // tr,),
                          in_specs=[pl.BlockSpec((n, tr, c), lambda i: (0, i, 0))],
                          out_specs=pl.BlockSpec((tr, c), lambda i: (i, 0)),
                          out_shape=jax.ShapeDtypeStruct((r, c), out_dtype),
                          compiler_params=_cparams(("parallel",)))(x)


def _add_half(g, t, half, name):
    n, r, c = g.shape
    h = r // 2
    tr = h
    for cand in (512, 256, 128, 64, 32, 16):
        if h % cand == 0 and cand * c * 2 <= 2 * 1024 * 1024:
            tr = cand
            break
    nblk = h // tr

    def body(half_ref, g_ref, t_ref, o_ref):
        o_ref[...] = (g_ref[...].astype(F32) + t_ref[...].astype(F32)).astype(o_ref.dtype)

    gs = pltpu.PrefetchScalarGridSpec(
        num_scalar_prefetch=1, grid=(n, nblk),
        in_specs=[pl.BlockSpec((None, tr, c), lambda j, i, hr: (j, hr[0] * nblk + i, 0)),
                  pl.BlockSpec((None, tr, c), lambda j, i, hr: (j, i, 0))],
        out_specs=pl.BlockSpec((None, tr, c), lambda j, i, hr: (j, i, 0)))
    return pl.pallas_call(body, name=name, grid_spec=gs, out_shape=jax.ShapeDtypeStruct((n, h, c), BF16),
                          compiler_params=_cparams(("parallel", "parallel")))(half, g, t)


def _position():
    x, y, c = lax.axis_index("x"), lax.axis_index("y"), lax.axis_index("c")
    return x, y, c


def _allgather8(xs, name):
    m_per, n = xs.shape

    def body(x_ref, out_ref, send_sems, recv_sems, local_sem):
        x, y, c = _position()
        me, sibling = (x, y, c), (x, y, 1 - c)
        chips = [(1 - x, y), (x, 1 - y), (1 - x, 1 - y)]

        def rows(px, py, pc):
            return out_ref.at[pl.ds((4 * px + 2 * py + pc) * m_per, m_per), :]

        def copy(k, block, to, src=None):
            return pltpu.make_async_remote_copy(
                src_ref=rows(*block) if src is None else src, dst_ref=rows(*block),
                send_sem=send_sems.at[k], recv_sem=recv_sems.at[k], device_id=to, device_id_type=MESH)

        mine = pltpu.make_async_copy(x_ref, rows(*me), local_sem)
        mine.start()
        first = [copy(0, me, sibling, src=x_ref)]
        first += [copy(1 + j, me, (*chip, c), src=x_ref) for j, chip in enumerate(chips)]
        for cp in first:
            cp.start()
        passed = [copy(4 + j, (*chip, c), sibling) for j, chip in enumerate(chips)]
        for j, chip in enumerate(chips):
            copy(1 + j, (*chip, c), me).wait_recv()
            passed[j].start()
        copy(0, sibling, me).wait_recv()
        for j, chip in enumerate(chips):
            copy(4 + j, (*chip, 1 - c), me).wait_recv()
        for cp in first + passed:
            cp.wait_send()
        mine.wait()

    return pl.pallas_call(
        body, name=name, out_shape=jax.ShapeDtypeStruct((N_DEV * m_per, n), xs.dtype),
        in_specs=[pl.BlockSpec(memory_space=pltpu.VMEM)], out_specs=pl.BlockSpec(memory_space=pltpu.VMEM),
        scratch_shapes=[pltpu.SemaphoreType.DMA((7,)), pltpu.SemaphoreType.DMA((7,)), pltpu.SemaphoreType.DMA],
        compiler_params=pltpu.CompilerParams(vmem_limit_bytes=VMEM_LIMIT_BYTES),
    )(xs)


_HBM = pl.BlockSpec(memory_space=pltpu.HBM)


def _gather_weights(ws, name):
    n = len(ws)

    def body(*refs):
        in_refs, out_refs = refs[:n], refs[n:2 * n]
        send_sems, recv_sems, fsend, frecv, local_sems = refs[2 * n:]
        x, y, c = _position()
        me, sibling = (x, y, c), (x, y, 1 - c)
        mychip = 2 * x + y
        chips = [(1 - x, y), (x, 1 - y), (1 - x, 1 - y)]
        halves = [w.shape[0] // 2 for w in ws]

        def piece(i, chip_index, half):
            return out_refs[i].at[chip_index, pl.ds(half * halves[i], halves[i]), :]

        locals_, sends, forwards = [], [], []
        for i in range(n):
            loc = pltpu.make_async_copy(in_refs[i], out_refs[i].at[mychip], local_sems.at[i])
            loc.start()
            locals_.append(loc)
            for j, (px, py) in enumerate(chips):
                cp = pltpu.make_async_remote_copy(
                    src_ref=in_refs[i].at[pl.ds(c * halves[i], halves[i]), :], dst_ref=piece(i, mychip, c),
                    send_sem=send_sems.at[3 * i + j], recv_sem=recv_sems.at[3 * i + j],
                    device_id=(px, py, c), device_id_type=MESH)
                cp.start()
                sends.append(cp)
        for i in range(n):
            for j, (px, py) in enumerate(chips):
                got = piece(i, 2 * px + py, c)
                pltpu.make_async_remote_copy(
                    src_ref=got, dst_ref=got, send_sem=send_sems.at[3 * i + j], recv_sem=recv_sems.at[3 * i + j],
                    device_id=me, device_id_type=MESH).wait_recv()
                fw = pltpu.make_async_remote_copy(
                    src_ref=got, dst_ref=got, send_sem=fsend.at[3 * i + j], recv_sem=frecv.at[3 * i + j],
                    device_id=sibling, device_id_type=MESH)
                fw.start()
                forwards.append(fw)
        for i in range(n):
            for j, (px, py) in enumerate(chips):
                other = piece(i, 2 * px + py, 1 - c)
                pltpu.make_async_remote_copy(
                    src_ref=other, dst_ref=other, send_sem=fsend.at[3 * i + j], recv_sem=frecv.at[3 * i + j],
                    device_id=me, device_id_type=MESH).wait_recv()
        for cp in sends + forwards:
            cp.wait_send()
        for loc in locals_:
            loc.wait()

    return pl.pallas_call(
        body, name=name,
        out_shape=[jax.ShapeDtypeStruct((N_CHIPS,) + w.shape, w.dtype) for w in ws],
        in_specs=[_HBM] * n, out_specs=[_HBM] * n,
        scratch_shapes=[pltpu.SemaphoreType.DMA((3 * n,)), pltpu.SemaphoreType.DMA((3 * n,)),
                        pltpu.SemaphoreType.DMA((3 * n,)), pltpu.SemaphoreType.DMA((3 * n,)),
                        pltpu.SemaphoreType.DMA((n,))],
    )(*ws)


def _swap_halves(gs, name):
    n = len(gs)

    def body(*refs):
        in_refs, out_refs = refs[:n], refs[n:2 * n]
        send_sems, recv_sems = refs[2 * n:]
        x, y, c = _position()
        cps = []
        for i in range(n):
            h = gs[i].shape[1] // 2
            cp = pltpu.make_async_remote_copy(
                src_ref=in_refs[i].at[:, pl.ds((1 - c) * h, h), :], dst_ref=out_refs[i],
                send_sem=send_sems.at[i], recv_sem=recv_sems.at[i], device_id=(x, y, 1 - c), device_id_type=MESH)
            cp.start()
            cps.append(cp)
        for cp in cps:
            cp.wait()

    return pl.pallas_call(
        body, name=name,
        out_shape=[jax.ShapeDtypeStruct((g.shape[0], g.shape[1] // 2, g.shape[2]), g.dtype) for g in gs],
        in_specs=[_HBM] * n, out_specs=[_HBM] * n,
        scratch_shapes=[pltpu.SemaphoreType.DMA((n,)), pltpu.SemaphoreType.DMA((n,))],
    )(*gs)


def _scatter_chips(ps, name):
    n = len(ps)

    def body(*refs):
        in_refs, out_refs = refs[:n], refs[n:2 * n]
        send_sems, recv_sems, local_sems = refs[2 * n:]
        x, y, c = _position()
        me = (x, y, c)
        mychip = 2 * x + y
        chips = [(1 - x, y), (x, 1 - y), (1 - x, 1 - y)]
        cps, locs = [], []
        for i in range(n):
            loc = pltpu.make_async_copy(in_refs[i].at[mychip], out_refs[i].at[mychip], local_sems.at[i])
            loc.start()
            locs.append(loc)
            for j, (px, py) in enumerate(chips):
                cp = pltpu.make_async_remote_copy(
                    src_ref=in_refs[i].at[2 * px + py], dst_ref=out_refs[i].at[mychip],
                    send_sem=send_sems.at[3 * i + j], recv_sem=recv_sems.at[3 * i + j],
                    device_id=(px, py, c), device_id_type=MESH)
                cp.start()
                cps.append(cp)
        for i in range(n):
            for j, (px, py) in enumerate(chips):
                got = out_refs[i].at[2 * px + py]
                pltpu.make_async_remote_copy(
                    src_ref=got, dst_ref=got, send_sem=send_sems.at[3 * i + j], recv_sem=recv_sems.at[3 * i + j],
                    device_id=me, device_id_type=MESH).wait_recv()
        for cp in cps:
            cp.wait_send()
        for loc in locs:
            loc.wait()

    return pl.pallas_call(
        body, name=name,
        out_shape=[jax.ShapeDtypeStruct(p.shape, p.dtype) for p in ps],
        in_specs=[_HBM] * n, out_specs=[_HBM] * n,
        scratch_shapes=[pltpu.SemaphoreType.DMA((3 * n,)), pltpu.SemaphoreType.DMA((3 * n,)),
                        pltpu.SemaphoreType.DMA((n,))],
    )(*ps)


def _join_halves(rs, name):
    n = len(rs)

    def body(*refs):
        in_refs, out_refs = refs[:n], refs[n:2 * n]
        send_sems, recv_sems, local_sems = refs[2 * n:]
        x, y, c = _position()
        cps, locs = [], []
        for i in range(n):
            h = rs[i].shape[0]
            mine = out_refs[i].at[pl.ds(c * h, h), :]
            loc = pltpu.make_async_copy(in_refs[i], mine, local_sems.at[i])
            loc.start()
            locs.append(loc)
            cp = pltpu.make_async_remote_copy(
                src_ref=in_refs[i], dst_ref=mine, send_sem=send_sems.at[i], recv_sem=recv_sems.at[i],
                device_id=(x, y, 1 - c), device_id_type=MESH)
            cp.start()
            cps.append(cp)
        for i in range(n):
            h = rs[i].shape[0]
            other = out_refs[i].at[pl.ds((1 - c) * h, h), :]
            pltpu.make_async_remote_copy(
                src_ref=other, dst_ref=other, send_sem=send_sems.at[i], recv_sem=recv_sems.at[i],
                device_id=(x, y, c), device_id_type=MESH).wait_recv()
        for cp in cps:
            cp.wait_send()
        for loc in locs:
            loc.wait()

    return pl.pallas_call(
        body, name=name,
        out_shape=[jax.ShapeDtypeStruct((2 * r.shape[0], r.shape[1]), r.dtype) for r in rs],
        in_specs=[_HBM] * n, out_specs=[_HBM] * n,
        scratch_shapes=[pltpu.SemaphoreType.DMA((n,)), pltpu.SemaphoreType.DMA((n,)), pltpu.SemaphoreType.DMA((n,))],
    )(*rs)


def _t5_buckets_block():
    qi = np.arange(BLOCK)[:, None]
    ki = np.arange(2 * BLOCK)[None, :]
    n = np.maximum(qi + BLOCK - ki, 0)
    max_exact = NUM_BUCKETS // 2
    large = max_exact + (np.log(np.maximum(n, 1) / max_exact) / np.log(MAX_DISTANCE / max_exact)
                         * (NUM_BUCKETS - max_exact)).astype(np.int32)
    large = np.minimum(large, NUM_BUCKETS - 1)
    return np.where(n < max_exact, n, large).astype(np.int32)


def _discretise(lambda_re, lambda_im, log_step, b_re, b_im):
    lam_re = jnp.minimum(lambda_re, -1e-4)
    lam_im = lambda_im
    delta = jnp.exp(log_step)[:, None]
    mag = jnp.exp(lam_re * delta)
    ang = lam_im * delta
    abar_re, abar_im = mag * jnp.cos(ang), mag * jnp.sin(ang)
    num_re, num_im = abar_re - 1.0, abar_im
    den = lam_re * lam_re + lam_im * lam_im
    f_re = (num_re * lam_re + num_im * lam_im) / den
    f_im = (num_im * lam_re - num_re * lam_im) / den
    bbar_re = f_re[..., None] * b_re - f_im[..., None] * b_im
    bbar_im = f_re[..., None] * b_im + f_im[..., None] * b_re
    return abar_re, abar_im, bbar_re, bbar_im


def _interleave(v, nc):
    s, w = v.shape
    return v.reshape(nc, s // nc, w).transpose(1, 0, 2).reshape(s, w)


def _deinterleave(v, nc):
    s, w = v.shape
    return v.reshape(s // nc, nc, w).transpose(1, 0, 2).reshape(s, w)


_SMALL = ("norm1_g", "b_in", "attn_sinks", "rel_bias", "lambda_re", "lambda_im", "log_step", "ssm_b_re",
          "ssm_b_im", "ssm_c_re", "ssm_c_im", "ssm_d", "b_glu", "norm2_g", "final_g")


def _pack(parts):
    rows = []
    for p in parts:
        f = p.reshape(-1).astype(F32)
        pad = (-f.shape[0]) % 128
        rows.append(jnp.pad(f, (0, pad)).reshape(-1, 128))
    out = jnp.concatenate(rows, axis=0)
    pad = (-out.shape[0]) % 256
    return jnp.pad(out, ((0, pad), (0, 0)))


def _unpack(packed, shapes):
    res, r = [], 0
    for shp in shapes:
        size = int(np.prod(shp))
        nr = -(-size // 128)
        res.append(packed[r:r + nr].reshape(-1)[:size].reshape(shp))
        r += nr
    return res


def kernel(x, c, w_ada, b_ada, norm1_g, w_in, b_in, attn_sinks, rel_bias, lambda_re, lambda_im, log_step, ssm_b_re, ssm_b_im, ssm_c_re, ssm_c_im, ssm_d, w_glu, b_glu, w_attn_proj, w_ssm_proj, w_out, norm2_g, w_ff1, w_ff2, final_g, loss_target, m_w_ada, m_b_ada, m_norm1_g, m_w_in, m_b_in, m_attn_sinks, m_rel_bias, m_lambda_re, m_lambda_im, m_log_step, m_ssm_b_re, m_ssm_b_im, m_ssm_c_re, m_ssm_c_im, m_ssm_d, m_w_glu, m_b_glu, m_w_attn_proj, m_w_ssm_proj, m_w_out, m_norm2_g, m_w_ff1, m_w_ff2, m_final_g, v_w_ada, v_b_ada, v_norm1_g, v_w_in, v_b_in, v_attn_sinks, v_rel_bias, v_lambda_re, v_lambda_im, v_log_step, v_ssm_b_re, v_ssm_b_im, v_ssm_c_re, v_ssm_c_im, v_ssm_d, v_w_glu, v_b_glu, v_w_attn_proj, v_w_ssm_proj, v_w_out, v_norm2_g, v_w_ff1, v_w_ff2, v_final_g):
    given = dict(locals())
    S, D = x.shape[1], x.shape[2]
    SSM_W = w_glu.shape[2]
    G = SSM_W // SSM_GROUP_CH
    NST = G * SSM_STATE
    DFF = w_ff2.shape[1] * N_CHIPS
    INW = w_in.shape[2] * N_CHIPS
    o_q, o_k, o_v, o_u = 0, ATTN_WIDTH, ATTN_WIDTH + KV_WIDTH, ATTN_WIDTH + 2 * KV_WIDTH
    o_ga, o_gs = o_u + SSM_W, o_u + SSM_W + D
    mx, my, mc = _position()
    my_chip = 2 * mx + my
    my_b = 4 * mx + 2 * my + mc

    xv, tgt = x[0], loss_target[0]

    big = dict(w_in=w_in[0], w_glu=w_glu[0], w_attn_proj=w_attn_proj[0], w_ssm_proj=w_ssm_proj[0],
               w_out=w_out[0], w_ff1=w_ff1[0], w_ff2=w_ff2[0])
    big_names = list(big)
    colsharded = {"w_in", "w_attn_proj", "w_ssm_proj", "w_ff1"}
    gathered = dict(zip(big_names, _gather_weights([big[k].astype(BF16) for k in big_names], "gather_weights")))

    def wop(k):
        g = gathered[k]
        return _Op(g, N_CHIPS) if k in colsharded else _Op(g.reshape(g.shape[0] * g.shape[1], g.shape[2]))

    c_all = _allgather8(jnp.pad(c, ((0, 7), (0, 0))), "gather_c").reshape(N_DEV, 8, D)[:, 0]
    c16 = jnp.pad(c_all, ((0, 8), (0, 0)))
    b_ada_mine = lax.dynamic_slice(b_ada.reshape(N_CHIPS, -1), (my_chip, 0), (1, w_ada.shape[2]))
    mod_sh = _mm(c16, w_ada[0], "NN", name="mod", M=16, N=w_ada.shape[2], K=D, a_fn=_silu,
                 epilogue=lambda acc, b: (acc + b,), extras=[(b_ada_mine, "row")])
    mod_all = _allgather8(mod_sh[:8], "gather_mod").reshape(N_DEV, 8, -1)
    mod_row = jnp.concatenate(
        [lax.dynamic_slice(mod_all, (2 * j, my_b, 0), (1, 1, mod_all.shape[2]))[0] for j in range(N_CHIPS)], axis=1)
    sh1, sc1, g1, sh2, sc2, g2 = [mod_row[:, i * D:(i + 1) * D] for i in range(6)]

    disc_in = (lambda_re[0], lambda_im[0], log_step[0], ssm_b_re[0], ssm_b_im[0])
    (abar_re, abar_im, bbar_re, bbar_im), disc_vjp = jax.vjp(_discretise, *disc_in)
    eye = jnp.eye(G, dtype=F32)
    bd = jnp.concatenate([jnp.einsum("gnp,gh->gphn", bb, eye).reshape(SSM_W, NST) for bb in (bbar_re, bbar_im)], axis=1)
    cd = jnp.concatenate([jnp.einsum("gpn,gh->gnhp", cc, eye).reshape(NST, SSM_W)
                          for cc in (ssm_c_re[0], -ssm_c_im[0])], axis=0)
    a_fwd = jnp.stack([abar_re.reshape(1, NST), abar_im.reshape(1, NST)])
    a_bwd = jnp.stack([abar_re.reshape(1, NST), -abar_im.reshape(1, NST)])
    d_row = ssm_d

    buckets = _t5_buckets_block()
    bias = jnp.transpose(rel_bias[buckets], (2, 0, 1))
    sinks_b = jnp.broadcast_to(attn_sinks[0][:, None], (N_Q_HEADS, 128))
    sinks_kv = jnp.pad(sinks_b.reshape(N_KV_HEADS, GQA_GROUP, 128), ((0, 0), (0, 8 - GQA_GROUP), (0, 0)))

    h1 = _rowwise(_norm_mod, [(xv, "tile", D), (norm1_g, "row", D), (sh1, "row", D), (sc1, "row", D)],
                  [(D, BF16)], [], name="norm1", rows=S)[0]
    proj = _mm(h1, wop("w_in"), "NN", name="proj", M=S, N=INW, K=D,
               epilogue=lambda acc, b: (acc + b,), extras=[(b_in, "row")])

    def heads(v2d, nh):
        return v2d.reshape(S, nh, HEAD_DIM).transpose(1, 0, 2)

    def unheads(v3d):
        return v3d.transpose(1, 0, 2).reshape(S, -1)

    qh = heads(proj[:, o_q:o_k], N_Q_HEADS)
    kh = heads(proj[:, o_k:o_v], N_KV_HEADS)
    vh = heads(proj[:, o_v:o_u], N_KV_HEADS)
    attn = unheads(_attn_fwd(qh, kh, vh, sinks_b, bias, "attn_fwd"))
    y_attn = _mm(attn, wop("w_attn_proj"), "NN", name="attn_proj", M=S, N=D, K=ATTN_WIDTH)

    u = proj[:, o_u:o_ga]
    u_il = _interleave(u, SCAN_CHUNKS)
    bu = _mm(u_il, bd, "NN", name="ssm_bu", M=S, N=2 * NST, K=SSM_W, out_nsh=2)
    xs = _scan(a_fwd, bu, None, reverse=False, name="scan_fwd", tc=256)
    y_il = _mm(_Op(xs, 2), cd, "NN", name="ssm_y", M=S, N=SSM_W, K=2 * NST,
               epilogue=lambda acc, uu, dd: (acc + dd * uu,), extras=[(u_il, "tile"), (d_row, "row")])
    y = _deinterleave(y_il, SCAN_CHUNKS)
    z0b = _rowwise(_gelu, [(y, "tile", SSM_W)], [(SSM_W, BF16)], [], name="gelu", rows=S)[0]
    z, t_glu = _mm(z0b, wop("w_glu"), "NN", name="glu", M=S, N=SSM_W, K=SSM_W, out_dtypes=(BF16, F32),
                   epilogue=lambda acc, b, yy: (_gelu(yy) * _sigmoid(acc + b), acc + b),
                   extras=[(b_glu, "row"), (y, "tile")])
    y_ssm = _mm(z, wop("w_ssm_proj"), "NN", name="ssm_proj", M=S, N=D, K=SSM_W)

    merged = _rowwise(_merge, [(_Op(proj, coff=o_ga), "tile", D), (_Op(proj, coff=o_gs), "tile", D),
                               (y_attn, "tile", D), (y_ssm, "tile", D)], [(D, BF16)], [], name="merge", rows=S)[0]
    mo, x2 = _mm(merged, wop("w_out"), "NN", name="out_proj", M=S, N=D, K=D, out_dtypes=(F32, F32),
                 epilogue=lambda acc, xx, gg: (acc, xx + gg * acc), extras=[(xv, "tile"), (g1, "row")])
    h2 = _rowwise(_norm_mod, [(x2, "tile", D), (norm2_g, "row", D), (sh2, "row", D), (sc2, "row", D)],
                  [(D, BF16)], [], name="norm2", rows=S)[0]
    a_b, r_b = _mm(h2, wop("w_ff1"), "NN", name="ff1", M=S, N=DFF, K=D, out_dtypes=(BF16, BF16),
                   epilogue=lambda acc: (acc, jnp.square(jnp.maximum(acc, 0.0))))
    ff, x3 = _mm(r_b, wop("w_ff2"), "NN", name="ff2", M=S, N=D, K=DFF, out_dtypes=(F32, F32),
                 epilogue=lambda acc, xx, gg: (acc, xx + gg * acc), extras=[(x2, "tile"), (g2, "row")])

    def final_fn(x3b, gf, tb):
        def f(xx, gg):
            yv = xx * lax.rsqrt(jnp.mean(xx * xx, axis=-1, keepdims=True) + EPS) * gg
            err = jnp.square(yv - tb)
            return 0.5 * jnp.sum(jnp.mean(err, axis=-1, keepdims=True), axis=0, keepdims=True)
        lv, vjp = jax.vjp(f, x3b, gf)
        dx, dg = vjp(jnp.ones((1, 1), F32))
        return dx, dg, jnp.broadcast_to(lv, (1, 128))

    dx3, g_final, loss_acc = _rowwise(final_fn, [(x3, "tile", D), (final_g.reshape(1, D), "row", D), (tgt, "tile", D)],
                                      [(D, F32)], [D, 128], name="final", rows=S)

    def ff_out_bwd(dx3b, ffb, g2b):
        return dx3b * g2b, jnp.sum(dx3b * ffb, axis=0, keepdims=True)

    dff, d_g2 = _rowwise(ff_out_bwd, [(dx3, "tile", D), (ff, "tile", D), (g2, "row", D)], [(D, BF16)], [D],
                         name="ff_out_bwd", rows=S)
    da = _mm(dff, wop("w_ff2"), "NT", name="ff2_dx", M=S, N=DFF, K=D, out_dtypes=(BF16,),
             epilogue=lambda acc, ab: (acc * (2.0 * jnp.maximum(ab.astype(F32), 0.0)),), extras=[(a_b, "tile")])
    g_w_ff2 = _mm(r_b, dff, "TN", name="ff2_dw", M=DFF, N=D, K=S, out_dtypes=(BF16,), tk=1024)
    dh2 = _mm(da, wop("w_ff1"), "NT", name="ff1_dx", M=S, N=D, K=DFF)
    g_w_ff1 = _mm(h2, da, "TN", name="ff1_dw", M=D, N=DFF, K=S, out_dtypes=(BF16,), out_nsh=N_CHIPS, tk=1024)

    def norm2_bwd(x2b, dh2b, dx3b, mob, gn, shb, scb, g1b):
        _, vjp = jax.vjp(_norm_mod, x2b, gn, shb, scb)
        dx, dg, dsh, dsc = vjp(dh2b)
        dx2b = dx + dx3b
        return dx2b, dx2b * g1b, dg, dsh, dsc, jnp.sum(dx2b * mob, axis=0, keepdims=True)

    dx2, dmo, g_norm2, d_sh2, d_sc2, d_g1 = _rowwise(
        norm2_bwd, [(x2, "tile", D), (dh2, "tile", D), (dx3, "tile", D), (mo, "tile", D), (norm2_g, "row", D),
                    (sh2, "row", D), (sc2, "row", D), (g1, "row", D)],
        [(D, F32), (D, BF16)], [D, D, D, D], name="norm2_bwd", rows=S, tr=128)
    dmerged = _mm(dmo, wop("w_out"), "NT", name="out_dx", M=S, N=D, K=D)
    g_w_out = _mm(merged, dmo, "TN", name="out_dw", M=D, N=D, K=S, out_dtypes=(BF16,), tk=1024)

    def merge_bwd(gab, gsb, yab, ysb, dmb):
        _, vjp = jax.vjp(_merge, gab, gsb, yab, ysb)
        return vjp(dmb)

    d_ga, d_gs, dy_attn, dy_ssm = _rowwise(
        merge_bwd, [(_Op(proj, coff=o_ga), "tile", D), (_Op(proj, coff=o_gs), "tile", D), (y_attn, "tile", D),
                    (y_ssm, "tile", D), (dmerged, "tile", D)],
        [(D, BF16), (D, BF16), (D, BF16), (D, BF16)], [], name="merge_bwd", rows=S, tr=128)

    dattn = _mm(dy_attn, wop("w_attn_proj"), "NT", name="attn_proj_dx", M=S, N=ATTN_WIDTH, K=D)
    g_w_attn_proj = _mm(attn, dy_attn, "TN", name="attn_proj_dw", M=ATTN_WIDTH, N=D, K=S, out_dtypes=(BF16,),
                        out_nsh=N_CHIPS, tk=1024)
    dqh, dkh, dvh, dsink_blk, dbias = _attn_bwd(qh, kh, vh, heads(dattn, N_Q_HEADS), sinks_kv, bias, "attn_bwd")
    g_sinks = dsink_blk[:, :GQA_GROUP, 0].reshape(1, N_Q_HEADS)
    onehot = jnp.asarray(np.eye(128, dtype=np.float32)[buckets.reshape(-1)])
    g_rel = _mm(dbias.reshape(N_Q_HEADS, -1), onehot, "NN", name="rel_bias_dw", M=N_Q_HEADS, N=128,
                K=BLOCK * 2 * BLOCK, tk=4096)
    g_rel_bias = g_rel[:, :NUM_BUCKETS].T

    dz = _mm(dy_ssm, wop("w_ssm_proj"), "NT", name="ssm_proj_dx", M=S, N=SSM_W, K=D)
    g_w_ssm_proj = _mm(z, dy_ssm, "TN", name="ssm_proj_dw", M=SSM_W, N=D, K=S, out_dtypes=(BF16,),
                       out_nsh=N_CHIPS, tk=1024)

    def glu_bwd(dzb, yb, tb):
        z0 = _gelu(yb)
        sg = _sigmoid(tb)
        dt = dzb * z0 * sg * (1.0 - sg)
        return dt, dzb * sg, jnp.sum(dt, axis=0, keepdims=True)

    dt_b, dz0a, g_b_glu = _rowwise(glu_bwd, [(dz, "tile", SSM_W), (y, "tile", SSM_W), (t_glu, "tile", SSM_W)],
                                   [(SSM_W, BF16), (SSM_W, F32)], [SSM_W], name="glu_bwd", rows=S)

    def gelu_bwd(acc, dz0ab, yb):
        _, vjp = jax.vjp(_gelu, yb)
        return (vjp(acc + dz0ab)[0],)

    dy = _mm(dt_b, wop("w_glu"), "NT", name="glu_dx", M=S, N=SSM_W, K=SSM_W, epilogue=gelu_bwd,
             extras=[(dz0a, "tile"), (y, "tile")])
    g_w_glu = _mm(z0b, dt_b, "TN", name="glu_dw", M=SSM_W, N=SSM_W, K=S, out_dtypes=(BF16,), tk=1024)
    dy_il = _interleave(dy, SCAN_CHUNKS)
    dxs = _mm(dy_il, cd, "NT", name="ssm_dx", M=S, N=2 * NST, K=SSM_W, out_nsh=2)
    g_cd = _mm(_Op(xs, 2), dy_il, "TN", name="ssm_dc", M=2 * NST, N=SSM_W, K=S, tk=1024)
    lam, d_abar = _scan(a_bwd, dxs, xs, reverse=True, name="scan_bwd", tc=128)

    def du_fn(acc, dyb, dd):
        return (acc + dd * dyb,)

    du_il = _mm(_Op(lam, 2), bd, "NT", name="ssm_du", M=S, N=SSM_W, K=2 * NST, epilogue=du_fn,
                extras=[(dy_il, "tile"), (d_row, "row")])
    g_bd = _mm(u_il, _Op(lam, 2), "TN", name="ssm_db", M=SSM_W, N=2 * NST, K=S, tk=1024)
    g_ssm_d = _rowwise(lambda dyb, ub: (jnp.sum(dyb * ub, axis=0, keepdims=True),),
                       [(dy_il, "tile", SSM_W), (u_il, "tile", SSM_W)], [], [SSM_W], name="ssm_dd", rows=S)[0]
    du = _deinterleave(du_il, SCAN_CHUNKS)

    g_cd4 = g_cd.reshape(2, G, SSM_STATE, G, SSM_GROUP_CH)
    g_c_re = jnp.einsum("gnhp,gh->gpn", g_cd4[0], eye)
    g_c_im = -jnp.einsum("gnhp,gh->gpn", g_cd4[1], eye)
    g_bd4 = g_bd.reshape(G, SSM_GROUP_CH, 2, G, SSM_STATE)
    g_bbar_re = jnp.einsum("gphn,gh->gnp", g_bd4[:, :, 0], eye)
    g_bbar_im = jnp.einsum("gphn,gh->gnp", g_bd4[:, :, 1], eye)
    g_lre, g_lim, g_lstep, g_bre, g_bim = disc_vjp(
        (d_abar[0].reshape(G, SSM_STATE), d_abar[1].reshape(G, SSM_STATE), g_bbar_re, g_bbar_im))

    dproj = jnp.concatenate([unheads(dqh).astype(BF16), unheads(dkh).astype(BF16), unheads(dvh).astype(BF16),
                             du.astype(BF16), d_ga, d_gs], axis=1)
    dh1 = _mm(dproj, wop("w_in"), "NT", name="proj_dx", M=S, N=D, K=INW)
    g_w_in = _mm(h1, dproj, "TN", name="proj_dw", M=D, N=INW, K=S, out_dtypes=(BF16,), out_nsh=N_CHIPS, tk=1024)
    g_b_in = _rowwise(lambda d: (jnp.sum(d.astype(F32), axis=0, keepdims=True),), [(dproj, "tile", INW)], [], [INW],
                      name="proj_db", rows=S)[0]

    def norm1_bwd(xb, dhb, dresb, gn, shb, scb):
        _, vjp = jax.vjp(_norm_mod, xb, gn, shb, scb)
        dx, dg, dsh, dsc = vjp(dhb)
        return dx + dresb, dg, dsh, dsc

    grad_x, g_norm1, d_sh1, d_sc1 = _rowwise(
        norm1_bwd, [(xv, "tile", D), (dh1, "tile", D), (dx2, "tile", D), (norm1_g, "row", D), (sh1, "row", D),
                    (sc1, "row", D)], [(D, F32)], [D, D, D], name="norm1_bwd", rows=S)

    dmod_row = jnp.concatenate([d_sh1, d_sc1, d_g1, d_sh2, d_sc2, d_g2], axis=1)
    dmod_all = _allgather8(jnp.pad(dmod_row, ((0, 7), (0, 0))), "gather_dmod").reshape(N_DEV, 8, -1)[:, 0]
    g_b_ada = _sum_lead(dmod_all.reshape(N_DEV, -1, 128), "b_ada_dw").reshape(1, -1)
    dmod_mine = lax.dynamic_slice(dmod_all.reshape(N_DEV, N_CHIPS, -1), (0, my_chip, 0), (N_DEV, 1, w_ada.shape[2]))[:, 0]
    g_w_ada = _mm(c16, jnp.pad(dmod_mine, ((0, 8), (0, 0))), "TN", name="ada_dw", M=D, N=w_ada.shape[2], K=16,
                  a_fn=_silu)

    small_g = dict(norm1_g=g_norm1, b_in=g_b_in, attn_sinks=g_sinks, rel_bias=g_rel_bias, lambda_re=g_lre[None],
                   lambda_im=g_lim[None], log_step=g_lstep[None], ssm_b_re=g_bre[None], ssm_b_im=g_bim[None],
                   ssm_c_re=g_c_re[None], ssm_c_im=g_c_im[None], ssm_d=g_ssm_d, b_glu=g_b_glu, norm2_g=g_norm2,
                   final_g=g_final.reshape(D))
    packed = _pack([loss_acc[:, :1]] + [small_g[k] for k in _SMALL])
    rows = packed.shape[0]
    summed = _sum_lead(_allgather8(packed, "gather_small").reshape(N_DEV, rows, 128), "small_sum")
    small_shapes = [(1,)] + [given[k].shape for k in _SMALL]
    parts = _unpack(summed, small_shapes)
    loss = parts[0].reshape(())
    grads = dict(zip(_SMALL, parts[1:]))
    grads["b_ada"] = g_b_ada
    grads["w_ada"] = g_w_ada[None]

    big_g = dict(w_in=g_w_in, w_glu=g_w_glu, w_attn_proj=g_w_attn_proj, w_ssm_proj=g_w_ssm_proj, w_out=g_w_out,
                 w_ff1=g_w_ff1, w_ff2=g_w_ff2)
    g_list = []
    for k in big_names:
        gk = big_g[k]
        if k not in colsharded:
            gk = gk.reshape(N_CHIPS, gk.shape[0] // N_CHIPS, gk.shape[1])
        g_list.append(gk)
    half = mc.astype(jnp.int32).reshape(1)
    t1 = _swap_halves(g_list, "rs_swap")
    p_list = [_add_half(g, t, half, "rs_add_" + k) for g, t, k in zip(g_list, t1, big_names)]
    t2 = _scatter_chips(p_list, "rs_scatter")
    r_list = [_sum_lead(t, "rs_sum_" + k) for t, k in zip(t2, big_names)]
    full = _join_halves(r_list, "rs_join")
    for k, f in zip(big_names, full):
        grads[k] = f[None]

    deltas, new_m, new_v = {}, {}, {}
    for k in big_names + ["w_ada"]:
        d_, m_, v_ = _adamw(given[k][0], grads[k][0], given["m_" + k][0], given["v_" + k][0], "adamw_" + k)
        deltas[k], new_m[k], new_v[k] = d_[None], m_[None], v_[None]
    small_all = list(_SMALL) + ["b_ada"]
    shapes = [given[k].shape for k in small_all]
    pw, pg = _pack([given[k] for k in small_all]), _pack([grads[k] for k in small_all])
    pm, pv = _pack([given["m_" + k] for k in small_all]), _pack([given["v_" + k] for k in small_all])
    d_, m_, v_ = _adamw(pw, pg, pm, pv, "adamw_small")
    for k, dd, mm, vv in zip(small_all, _unpack(d_, shapes), _unpack(m_, shapes), _unpack(v_, shapes)):
        deltas[k], new_m[k], new_v[k] = dd, mm, vv
        grads[k] = grads[k].reshape(given[k].shape)

    names = ["w_ada", "b_ada", "norm1_g", "w_in", "b_in", "attn_sinks", "rel_bias", "lambda_re", "lambda_im",
             "log_step", "ssm_b_re", "ssm_b_im", "ssm_c_re", "ssm_c_im", "ssm_d", "w_glu", "b_glu", "w_attn_proj",
             "w_ssm_proj", "w_out", "norm2_g", "w_ff1", "w_ff2", "final_g"]
    return (loss, grad_x[None], *[grads[n] for n in names], *[deltas[n] for n in names],
            *[new_m[n] for n in names], *[new_v[n] for n in names])
```

```python
import math

import numpy as np
import jax
import jax.numpy as jnp
from jax import lax
from jax.experimental import pallas as pl
from jax.experimental.pallas import tpu as pltpu

F32 = jnp.float32
BF16 = jnp.bfloat16
MESH = pl.DeviceIdType.MESH

HEAD_DIM = 64
N_Q_HEADS = 16
N_KV_HEADS = 4
GQA_GROUP = N_Q_HEADS // N_KV_HEADS
ATTN_WIDTH = N_Q_HEADS * HEAD_DIM
KV_WIDTH = N_KV_HEADS * HEAD_DIM
BLOCK = 128
NUM_BUCKETS = 32
MAX_DISTANCE = 128
NEG_INF = -1e30
SSM_GROUP_CH = 16
SSM_STATE = 64
EPS = 1e-6
ADAM_LR = 0.001
ADAM_B1 = 0.9
ADAM_B2 = 0.999
ADAM_EPS = 1e-08
ADAM_WD = 0.01
ADAM_STEP = 10

N_CHIPS = 4
N_DEV = 8
SCAN_CHUNKS = 8
VMEM_LIMIT_BYTES = 48 * 1024 * 1024


def _cparams(sem=None):
    return pltpu.CompilerParams(dimension_semantics=sem, vmem_limit_bytes=VMEM_LIMIT_BYTES)


class _Op:
    def __init__(self, arr, nsh=None, coff=0):
        self.arr, self.nsh, self.coff = arr, nsh, coff
        if nsh is None:
            self.rows, self.cols = arr.shape
        else:
            assert arr.shape[0] == nsh
            self.rows, self.cols = arr.shape[1], arr.shape[2] * nsh

    def spec(self, br, bc, idx):
        assert self.coff % bc == 0
        off = self.coff // bc
        if self.nsh is None:
            return pl.BlockSpec((br, bc), lambda *g: (idx(*g)[0], idx(*g)[1] + off))
        per = (self.cols // self.nsh) // bc
        assert per * bc * self.nsh == self.cols

        def imap(*g):
            r, c = idx(*g)
            c = c + off
            return (c // per, r, c % per)
        return pl.BlockSpec((None, br, bc), imap)


def _as_op(a):
    return a if isinstance(a, _Op) else _Op(a)


def _mm(a, b, mode, *, name, M, N, K, out_dtypes=(F32,), out_nsh=None, epilogue=None, extras=(),
        a_fn=None, ti=1024, tj=512, tk=2048):
    a, b = _as_op(a), _as_op(b)
    ti, tj, tk = min(ti, M), min(tj, N), min(tk, K)
    a_w = a.cols // a.nsh if a.nsh else None
    b_w = b.cols // b.nsh if b.nsh else None
    if a_w:
        ti, tk = (min(ti, a_w), tk) if mode == "TN" else (ti, min(tk, a_w))
    if b_w:
        tj, tk = (tj, min(tk, b_w)) if mode == "NT" else (min(tj, b_w), tk)
    if out_nsh:
        tj = min(tj, N // out_nsh)
    assert M % ti == 0 and N % tj == 0 and K % tk == 0, (name, M, N, K, ti, tj, tk)
    nk = K // tk
    if mode == "NN":
        a_spec = a.spec(ti, tk, lambda i, j, k: (i, k))
        b_spec = b.spec(tk, tj, lambda i, j, k: (k, j))
        dims = (((1,), (0,)), ((), ()))
    elif mode == "NT":
        a_spec = a.spec(ti, tk, lambda i, j, k: (i, k))
        b_spec = b.spec(tj, tk, lambda i, j, k: (j, k))
        dims = (((1,), (1,)), ((), ()))
    else:
        a_spec = a.spec(tk, ti, lambda i, j, k: (k, i))
        b_spec = b.spec(tk, tj, lambda i, j, k: (k, j))
        dims = (((0,), (0,)), ((), ()))
    ex_specs, ex_arrs = [], []
    for op, kind in extras:
        op = _as_op(op)
        if kind == "tile":
            ex_specs.append(op.spec(ti, tj, lambda i, j, k: (i, j)))
        else:
            ex_specs.append(op.spec(1, tj, lambda i, j, k: (0, j)))
        ex_arrs.append(op.arr)
    ne, no = len(ex_arrs), len(out_dtypes)
    if out_nsh is None:
        out_shapes = [jax.ShapeDtypeStruct((M, N), d) for d in out_dtypes]
        out_specs = [pl.BlockSpec((ti, tj), lambda i, j, k: (i, j)) for _ in out_dtypes]
    else:
        per = (N // out_nsh) // tj
        assert per * tj * out_nsh == N
        out_shapes = [jax.ShapeDtypeStruct((out_nsh, M, N // out_nsh), d) for d in out_dtypes]
        out_specs = [pl.BlockSpec((None, ti, tj), lambda i, j, k: (j // per, i, j % per)) for _ in out_dtypes]

    def body(a_ref, b_ref, *rest):
        ex_refs, out_refs, acc = rest[:ne], rest[ne:ne + no], rest[ne + no]
        k = pl.program_id(2)

        @pl.when(k == 0)
        def _():
            acc[...] = jnp.zeros_like(acc)

        av = a_ref[...]
        if a_fn is not None:
            av = a_fn(av)
        acc[...] += lax.dot_general(av.astype(BF16), b_ref[...].astype(BF16), dims,
                                    preferred_element_type=F32)

        @pl.when(k == nk - 1)
        def _():
            res = acc[...]
            outs = epilogue(res, *[r[...] for r in ex_refs]) if epilogue is not None else (res,)
            for o_ref, o in zip(out_refs, outs):
                o_ref[...] = o.astype(o_ref.dtype)

    outs = pl.pallas_call(
        body, name=name, grid=(M // ti, N // tj, nk),
        in_specs=[a_spec, b_spec] + ex_specs, out_specs=out_specs, out_shape=out_shapes,
        scratch_shapes=[pltpu.VMEM((ti, tj), F32)],
        compiler_params=_cparams(("parallel", "parallel", "arbitrary")),
    )(a.arr, b.arr, *ex_arrs)
    return outs[0] if no == 1 else outs


def _rowwise(fn, ins, outs, accs, *, name, rows, tr=256):
    tr = min(tr, rows)
    assert rows % tr == 0
    in_specs, arrs = [], []
    for op, kind, width in ins:
        op = _as_op(op)
        if kind == "tile":
            in_specs.append(op.spec(tr, width, lambda i: (i, 0)))
        else:
            in_specs.append(op.spec(op.rows, width, lambda i: (0, 0)))
        arrs.append(op.arr)
    ni, no, na = len(ins), len(outs), len(accs)
    out_shapes = [jax.ShapeDtypeStruct((rows, w), d) for w, d in outs]
    out_specs = [pl.BlockSpec((tr, w), lambda i: (i, 0)) for w, _ in outs]
    out_shapes += [jax.ShapeDtypeStruct((1, w), F32) for w in accs]
    out_specs += [pl.BlockSpec((1, w), lambda i: (0, 0)) for w in accs]

    def body(*refs):
        in_refs, out_refs, acc_refs = refs[:ni], refs[ni:ni + no], refs[ni + no:]
        res = fn(*[r[...] for r in in_refs])
        if not isinstance(res, (tuple, list)):
            res = (res,)
        for o_ref, r in zip(out_refs, res[:no]):
            o_ref[...] = r.astype(o_ref.dtype)
        if na:
            @pl.when(pl.program_id(0) == 0)
            def _():
                for a_ref in acc_refs:
                    a_ref[...] = jnp.zeros_like(a_ref)
            for a_ref, r in zip(acc_refs, res[no:]):
                a_ref[...] += r.astype(F32)

    res = pl.pallas_call(
        body, name=name, grid=(rows // tr,), in_specs=in_specs, out_specs=out_specs, out_shape=out_shapes,
        compiler_params=_cparams(("arbitrary",)),
    )(*arrs)
    return res


def _norm_mod(x, g, sh, sc):
    y = x * lax.rsqrt(jnp.mean(x * x, axis=-1, keepdims=True) + EPS) * g
    return y * (1.0 + sc) + sh


def _sigmoid(x):
    return 1.0 / (1.0 + jnp.exp(-x))


def _silu(x):
    return x * _sigmoid(x)


def _gelu(x):
    return 0.5 * x * (1.0 + jnp.tanh(math.sqrt(2.0 / math.pi) * (x + 0.044715 * (x * x * x))))


def _merge(ga, gs, ya, ys):
    return _sigmoid(ga) * ya + _sigmoid(gs) * ys


def _attn_head(q, kp, kc, vp, vc, sink, bias_p, bias_c, not_first):
    nt = (((1,), (1,)), ((), ()))
    nn = (((1,), (0,)), ((), ()))
    qb = q.astype(BF16)
    scale = HEAD_DIM ** -0.5
    sp = lax.dot_general(qb, kp.astype(BF16), nt, preferred_element_type=F32) * scale + bias_p
    sc = lax.dot_general(qb, kc.astype(BF16), nt, preferred_element_type=F32) * scale + bias_c
    qi = lax.broadcasted_iota(jnp.int32, (BLOCK, BLOCK), 0)
    ki = lax.broadcasted_iota(jnp.int32, (BLOCK, BLOCK), 1)
    sp = jnp.where(jnp.logical_and(ki > qi, not_first), sp, NEG_INF)
    sc = jnp.where(ki <= qi, sc, NEG_INF)
    m = jnp.maximum(jnp.maximum(jnp.max(sp, axis=-1, keepdims=True), jnp.max(sc, axis=-1, keepdims=True)), sink)
    m = lax.stop_gradient(m)
    pp = jnp.exp(sp - m)
    pc = jnp.exp(sc - m)
    denom = jnp.sum(pp, axis=-1, keepdims=True) + jnp.sum(pc, axis=-1, keepdims=True) + jnp.exp(sink - m)
    o = lax.dot_general((pp / denom).astype(BF16), vp.astype(BF16), nn, preferred_element_type=F32)
    o = o + lax.dot_general((pc / denom).astype(BF16), vc.astype(BF16), nn, preferred_element_type=F32)
    return o


def _attn_fwd(qh, kh, vh, sinks, bias, name):
    s = qh.shape[1]
    nb = s // BLOCK

    def body(q_ref, kp_ref, kc_ref, vp_ref, vc_ref, sink_ref, bias_ref, o_ref):
        not_first = pl.program_id(0) > 0
        for h in range(N_Q_HEADS):
            kv = h // GQA_GROUP
            o = _attn_head(q_ref[h], kp_ref[kv], kc_ref[kv], vp_ref[kv], vc_ref[kv], sink_ref[h:h + 1, 0:1],
                           bias_ref[h, :, 0:BLOCK], bias_ref[h, :, BLOCK:2 * BLOCK], not_first)
            o_ref[h] = o.astype(o_ref.dtype)

    cur = lambda i: (0, i, 0)
    prev = lambda i: (0, jnp.maximum(i - 1, 0), 0)
    return pl.pallas_call(
        body, name=name, grid=(nb,),
        in_specs=[pl.BlockSpec((N_Q_HEADS, BLOCK, HEAD_DIM), cur),
                  pl.BlockSpec((N_KV_HEADS, BLOCK, HEAD_DIM), prev), pl.BlockSpec((N_KV_HEADS, BLOCK, HEAD_DIM), cur),
                  pl.BlockSpec((N_KV_HEADS, BLOCK, HEAD_DIM), prev), pl.BlockSpec((N_KV_HEADS, BLOCK, HEAD_DIM), cur),
                  pl.BlockSpec((N_Q_HEADS, 128), lambda i: (0, 0)),
                  pl.BlockSpec((N_Q_HEADS, BLOCK, 2 * BLOCK), lambda i: (0, 0, 0))],
        out_specs=pl.BlockSpec((N_Q_HEADS, BLOCK, HEAD_DIM), cur),
        out_shape=jax.ShapeDtypeStruct((N_Q_HEADS, s, HEAD_DIM), BF16),
        compiler_params=_cparams(("arbitrary",)),
    )(qh, kh, kh, vh, vh, sinks, bias)


def _attn_bwd(qh, kh, vh, doh, sinks, bias, name):
    s = qh.shape[1]
    nb = s // BLOCK
    G = GQA_GROUP

    def body(q_ref, kp_ref, kc_ref, vp_ref, vc_ref, do_ref, sink_ref, bias_ref,
             dq_ref, dk_ref, dv_ref, dsink_ref, dbias_ref, ck, cv):
        i = pl.program_id(1)

        @pl.when(i == 0)
        def _():
            dsink_ref[...] = jnp.zeros_like(dsink_ref)
            dbias_ref[...] = jnp.zeros_like(dbias_ref)
            ck[...] = jnp.zeros_like(ck)
            cv[...] = jnp.zeros_like(cv)

        @pl.when(i < nb)
        def _():
            not_first = i > 0
            kp, kc, vp, vc = kp_ref[...], kc_ref[...], vp_ref[...], vc_ref[...]
            dkp = jnp.zeros((BLOCK, HEAD_DIM), F32)
            dkc = jnp.zeros((BLOCK, HEAD_DIM), F32)
            dvp = jnp.zeros((BLOCK, HEAD_DIM), F32)
            dvc = jnp.zeros((BLOCK, HEAD_DIM), F32)
            rows = lax.broadcasted_iota(jnp.int32, (8, 128), 0)
            dsink_blk = jnp.zeros((8, 128), F32)
            for g in range(G):
                sink = sink_ref[g:g + 1, 0:1]
                bp, bc = bias_ref[g, :, 0:BLOCK], bias_ref[g, :, BLOCK:2 * BLOCK]
                _, vjp = jax.vjp(lambda q, a, b, c, d, sk, e, f: _attn_head(q, a, b, c, d, sk, e, f, not_first),
                                 q_ref[g], kp, kc, vp, vc, sink, bp, bc)
                dq, a, b, c, d, dsk, dbp, dbc = vjp(do_ref[g].astype(F32))
                dq_ref[g] = dq
                dkp, dkc, dvp, dvc = dkp + a, dkc + b, dvp + c, dvc + d
                dsink_blk = dsink_blk + jnp.where(rows == g, jnp.broadcast_to(dsk, (8, 128)), 0.0)
                dbias_ref[g, :, 0:BLOCK] += dbp
                dbias_ref[g, :, BLOCK:2 * BLOCK] += dbc
            dsink_ref[...] += dsink_blk
            dk_ref[...] = ck[...] + dkp
            dv_ref[...] = cv[...] + dvp
            ck[...] = dkc
            cv[...] = dvc

        @pl.when(i == nb)
        def _():
            dk_ref[...] = ck[...]
            dv_ref[...] = cv[...]

    qcur = lambda kv, i: (kv, jnp.minimum(i, nb - 1), 0)
    kcur = lambda kv, i: (kv, jnp.minimum(i, nb - 1), 0)
    kprev = lambda kv, i: (kv, jnp.clip(i - 1, 0, nb - 1), 0)
    qspec = pl.BlockSpec((G, BLOCK, HEAD_DIM), qcur)
    kc_spec = pl.BlockSpec((None, BLOCK, HEAD_DIM), kcur)
    kp_spec = pl.BlockSpec((None, BLOCK, HEAD_DIM), kprev)
    return pl.pallas_call(
        body, name=name, grid=(N_KV_HEADS, nb + 1),
        in_specs=[qspec, kp_spec, kc_spec, kp_spec, kc_spec, qspec,
                  pl.BlockSpec((None, 8, 128), lambda kv, i: (kv, 0, 0)),
                  pl.BlockSpec((G, BLOCK, 2 * BLOCK), lambda kv, i: (kv, 0, 0))],
        out_specs=[qspec, kp_spec, kp_spec,
                   pl.BlockSpec((None, 8, 128), lambda kv, i: (kv, 0, 0)),
                   pl.BlockSpec((G, BLOCK, 2 * BLOCK), lambda kv, i: (kv, 0, 0))],
        out_shape=[jax.ShapeDtypeStruct((N_Q_HEADS, s, HEAD_DIM), F32),
                   jax.ShapeDtypeStruct((N_KV_HEADS, s, HEAD_DIM), F32),
                   jax.ShapeDtypeStruct((N_KV_HEADS, s, HEAD_DIM), F32),
                   jax.ShapeDtypeStruct((N_KV_HEADS, 8, 128), F32),
                   jax.ShapeDtypeStruct((N_Q_HEADS, BLOCK, 2 * BLOCK), F32)],
        scratch_shapes=[pltpu.VMEM((BLOCK, HEAD_DIM), F32), pltpu.VMEM((BLOCK, HEAD_DIM), F32)],
        compiler_params=_cparams(("arbitrary", "arbitrary")),
    )(qh, kh, kh, vh, vh, doh, sinks, bias)


def _cmul(ar, ai, br, bi):
    return ar * br - ai * bi, ar * bi + ai * br


def _scan(a, b, xs_prev, *, reverse, name, tc):
    _, s, c = b.shape
    nc = SCAN_CHUNKS
    steps = s // nc
    with_da = xs_prev is not None
    unroll = 8 if steps % 8 == 0 else 1

    def shift(v, d):
        row = lax.broadcasted_iota(jnp.int32, v.shape, 0)
        if reverse:
            return jnp.where(row < nc - d, pltpu.roll(v, nc - d, 0), 0.0)
        return jnp.where(row >= d, pltpu.roll(v, d, 0), 0.0)

    def body(*refs):
        if with_da:
            a_ref, b_ref, xp_ref, x_ref, da_ref = refs
        else:
            a_ref, b_ref, x_ref = refs
        ar = jnp.broadcast_to(a_ref[0], (nc, tc))
        ai = jnp.broadcast_to(a_ref[1], (nc, tc))

        def row_of(step):
            j = (steps - 1 - step) if reverse else step
            return pl.multiple_of(j * nc, nc)

        def p1(step, st):
            sr, si = st
            r0 = row_of(step)
            mr, mi = _cmul(ar, ai, sr, si)
            sr = mr + b_ref[0, pl.ds(r0, nc), :]
            si = mi + b_ref[1, pl.ds(r0, nc), :]
            x_ref[0, pl.ds(r0, nc), :] = sr
            x_ref[1, pl.ds(r0, nc), :] = si
            return sr, si
        zero = jnp.zeros((nc, tc), F32)
        er, ei = lax.fori_loop(0, steps, p1, (zero, zero), unroll=unroll)

        def pw(step, st):
            return _cmul(ar, ai, *st)
        pr, pi_ = lax.fori_loop(0, steps, pw, (jnp.ones((nc, tc), F32), zero), unroll=unroll)
        cr, ci = shift(er, 1), shift(ei, 1)
        d = 1
        while d < nc:
            mr, mi = _cmul(pr, pi_, shift(cr, d), shift(ci, d))
            cr, ci = cr + mr, ci + mi
            pr, pi_ = _cmul(pr, pi_, pr, pi_)
            d *= 2

        def p2(step, st):
            qr, qi, dar, dai = st
            r0 = row_of(step)
            qr, qi = _cmul(ar, ai, qr, qi)
            fr, fi = _cmul(qr, qi, cr, ci)
            xr = x_ref[0, pl.ds(r0, nc), :] + fr
            xi = x_ref[1, pl.ds(r0, nc), :] + fi
            x_ref[0, pl.ds(r0, nc), :] = xr
            x_ref[1, pl.ds(r0, nc), :] = xi
            if with_da:
                jm = jnp.where(step == steps - 1, steps - 1, steps - 2 - step)
                rp = pl.multiple_of(jm * nc, nc)
                vr, vi = xp_ref[0, pl.ds(rp, nc), :], xp_ref[1, pl.ds(rp, nc), :]
                row = lax.broadcasted_iota(jnp.int32, (nc, tc), 0)
                first = step == steps - 1
                sel = jnp.logical_and(first, row == 0)
                vr = jnp.where(sel, 0.0, jnp.where(first, pltpu.roll(vr, 1, 0), vr))
                vi = jnp.where(sel, 0.0, jnp.where(first, pltpu.roll(vi, 1, 0), vi))
                dar = dar + xr * vr + xi * vi
                dai = dai + xi * vr - xr * vi
            return qr, qi, dar, dai
        _, _, dar, dai = lax.fori_loop(0, steps, p2, (jnp.ones((nc, tc), F32), zero, zero, zero), unroll=unroll)
        if with_da:
            da_ref[0] = jnp.sum(dar, axis=0, keepdims=True)
            da_ref[1] = jnp.sum(dai, axis=0, keepdims=True)

    blk = pl.BlockSpec((2, s, tc), lambda i: (0, 0, i))
    vec = pl.BlockSpec((2, 1, tc), lambda i: (0, 0, i))
    in_specs, args = [vec, blk], [a, b]
    out_specs, out_shape = [blk], [jax.ShapeDtypeStruct((2, s, c), F32)]
    if with_da:
        in_specs.append(blk)
        args.append(xs_prev)
        out_specs.append(vec)
        out_shape.append(jax.ShapeDtypeStruct((2, 1, c), F32))
    res = pl.pallas_call(
        body, name=name, grid=(c // tc,), in_specs=in_specs, out_specs=out_specs, out_shape=out_shape,
        compiler_params=_cparams(("arbitrary",)),
    )(*args)
    return res if with_da else res[0]


def _adamw(w, g, m, v, name):
    r, c = w.shape
    tr = r
    for cand in (512, 256, 128, 64, 32, 16, 8):
        if r % cand == 0 and cand * c * 4 <= 2 * 1024 * 1024:
            tr = cand
            break

    def body(w_ref, g_ref, m_ref, v_ref, d_ref, nm_ref, nv_ref):
        gv = g_ref[...]
        nm = ADAM_B1 * m_ref[...] + (1.0 - ADAM_B1) * gv
        nv = ADAM_B2 * v_ref[...] + (1.0 - ADAM_B2) * (gv * gv)
        m_hat = nm / (1.0 - ADAM_B1 ** ADAM_STEP)
        v_hat = nv / (1.0 - ADAM_B2 ** ADAM_STEP)
        d_ref[...] = -ADAM_LR * (m_hat / (jnp.sqrt(v_hat) + ADAM_EPS) + ADAM_WD * w_ref[...])
        nm_ref[...] = nm
        nv_ref[...] = nv

    spec = pl.BlockSpec((tr, c), lambda i: (i, 0))
    sds = jax.ShapeDtypeStruct((r, c), F32)
    return pl.pallas_call(body, name=name, grid=(r // tr,), in_specs=[spec] * 4, out_specs=[spec] * 3,
                          out_shape=[sds] * 3, compiler_params=_cparams(("parallel",)))(w, g, m, v)


def _sum_lead(x, name, out_dtype=F32):
    n, r, c = x.shape
    tr = r
    for cand in (512, 256, 128, 64, 32, 16, 8):
        if r % cand == 0 and n * cand * c * 4 <= 4 * 1024 * 1024:
            tr = cand
            break

    def body(x_ref, o_ref):
        acc = x_ref[0].astype(F32)
        for k in range(1, n):
            acc = acc + x_ref[k].astype(F32)
        o_ref[...] = acc.astype(o_ref.dtype)

    return pl.pallas_call(body, name=name, grid=(r // tr,),
                          in_specs=[pl.BlockSpec((n, tr, c), lambda i: (0, i, 0))],
                          out_specs=pl.BlockSpec((tr, c), lambda i: (i, 0)),
                          out_shape=jax.ShapeDtypeStruct((r, c), out_dtype),
                          compiler_params=_cparams(("parallel",)))(x)


def _row_tile(rows, row_bytes, budget, least=8):
    for cand in (1024, 512, 256, 128, 64, 32, 16, 8):
        if cand >= least and rows % cand == 0 and cand * row_bytes <= budget:
            return cand
    return rows


def _cast_into_slot(w, slot, name):
    r, c = w.shape
    tr = _row_tile(r, c * 4, 4 * 1024 * 1024, least=16)

    def body(slot_ref, w_ref, o_ref):
        o_ref[...] = w_ref[...].astype(o_ref.dtype)

    gs = pltpu.PrefetchScalarGridSpec(
        num_scalar_prefetch=1, grid=(r // tr,),
        in_specs=[pl.BlockSpec((tr, c), lambda i, s: (i, 0))],
        out_specs=pl.BlockSpec((None, tr, c), lambda i, s: (s[0], i, 0)))
    return pl.pallas_call(body, name=name, grid_spec=gs, out_shape=jax.ShapeDtypeStruct((N_CHIPS, r, c), BF16),
                          compiler_params=_cparams(("parallel",)))(slot, w)


def _sum_own(p, t, sel, name):
    _, h, c = p.shape
    tr = _row_tile(h, c * 4, 2 * 1024 * 1024, least=16)
    nblk = h // tr

    def body(sel_ref, p_ref, t_ref, o_ref):
        acc = p_ref[...].astype(F32)
        for k in range(3):
            acc = acc + t_ref[k].astype(F32)
        o_ref[...] = acc

    gs = pltpu.PrefetchScalarGridSpec(
        num_scalar_prefetch=1, grid=(nblk,),
        in_specs=[pl.BlockSpec((None, tr, c), lambda i, s: (s[0], i, 0)),
                  pl.BlockSpec((3, tr, c), lambda i, s: (0, i, 0))],
        out_specs=pl.BlockSpec((tr, c), lambda i, s: (s[1] * nblk + i, 0)))
    return pl.pallas_call(body, name=name, grid_spec=gs, out_shape=jax.ShapeDtypeStruct((2 * h, c), F32),
                          compiler_params=_cparams(("parallel",)))(sel, p, t)


def _add_half(g, t, half, name):
    n, r, c = g.shape
    h = r // 2
    tr = h
    for cand in (512, 256, 128, 64, 32, 16):
        if h % cand == 0 and cand * c * 2 <= 2 * 1024 * 1024:
            tr = cand
            break
    nblk = h // tr

    def body(half_ref, g_ref, t_ref, o_ref):
        o_ref[...] = (g_ref[...].astype(F32) + t_ref[...].astype(F32)).astype(o_ref.dtype)

    gs = pltpu.PrefetchScalarGridSpec(
        num_scalar_prefetch=1, grid=(n, nblk),
        in_specs=[pl.BlockSpec((None, tr, c), lambda j, i, hr: (j, hr[0] * nblk + i, 0)),
                  pl.BlockSpec((None, tr, c), lambda j, i, hr: (j, i, 0))],
        out_specs=pl.BlockSpec((None, tr, c), lambda j, i, hr: (j, i, 0)))
    return pl.pallas_call(body, name=name, grid_spec=gs, out_shape=jax.ShapeDtypeStruct((n, h, c), BF16),
                          compiler_params=_cparams(("parallel", "parallel")))(half, g, t)


def _position():
    x, y, c = lax.axis_index("x"), lax.axis_index("y"), lax.axis_index("c")
    return x, y, c


def _allgather8(xs, name):
    m_per, n = xs.shape

    def body(x_ref, out_ref, send_sems, recv_sems, local_sem):
        x, y, c = _position()
        me, sibling = (x, y, c), (x, y, 1 - c)
        chips = [(1 - x, y), (x, 1 - y), (1 - x, 1 - y)]

        def rows(px, py, pc):
            return out_ref.at[pl.ds((4 * px + 2 * py + pc) * m_per, m_per), :]

        def copy(k, block, to, src=None):
            return pltpu.make_async_remote_copy(
                src_ref=rows(*block) if src is None else src, dst_ref=rows(*block),
                send_sem=send_sems.at[k], recv_sem=recv_sems.at[k], device_id=to, device_id_type=MESH)

        mine = pltpu.make_async_copy(x_ref, rows(*me), local_sem)
        mine.start()
        first = [copy(0, me, sibling, src=x_ref)]
        first += [copy(1 + j, me, (*chip, c), src=x_ref) for j, chip in enumerate(chips)]
        for cp in first:
            cp.start()
        passed = [copy(4 + j, (*chip, c), sibling) for j, chip in enumerate(chips)]
        for j, chip in enumerate(chips):
            copy(1 + j, (*chip, c), me).wait_recv()
            passed[j].start()
        copy(0, sibling, me).wait_recv()
        for j, chip in enumerate(chips):
            copy(4 + j, (*chip, 1 - c), me).wait_recv()
        for cp in first + passed:
            cp.wait_send()
        mine.wait()

    return pl.pallas_call(
        body, name=name, out_shape=jax.ShapeDtypeStruct((N_DEV * m_per, n), xs.dtype),
        in_specs=[pl.BlockSpec(memory_space=pltpu.VMEM)], out_specs=pl.BlockSpec(memory_space=pltpu.VMEM),
        scratch_shapes=[pltpu.SemaphoreType.DMA((7,)), pltpu.SemaphoreType.DMA((7,)), pltpu.SemaphoreType.DMA],
        compiler_params=pltpu.CompilerParams(vmem_limit_bytes=VMEM_LIMIT_BYTES),
    )(xs)


_HBM = pl.BlockSpec(memory_space=pltpu.HBM)


def _gather_weights(ws, name):
    n = len(ws)

    def body(*refs):
        out_refs = refs[n:2 * n]
        send_sems, recv_sems, fsend, frecv = refs[2 * n:]
        x, y, c = _position()
        me, sibling = (x, y, c), (x, y, 1 - c)
        mychip = 2 * x + y
        chips = [(1 - x, y), (x, 1 - y), (1 - x, 1 - y)]
        halves = [w.shape[1] // 2 for w in ws]

        def piece(i, chip_index, half):
            return out_refs[i].at[chip_index, pl.ds(half * halves[i], halves[i]), :]

        sends, forwards = [], []
        for i in range(n):
            for j, (px, py) in enumerate(chips):
                cp = pltpu.make_async_remote_copy(
                    src_ref=piece(i, mychip, c), dst_ref=piece(i, mychip, c),
                    send_sem=send_sems.at[3 * i + j], recv_sem=recv_sems.at[3 * i + j],
                    device_id=(px, py, c), device_id_type=MESH)
                cp.start()
                sends.append(cp)
        for i in range(n):
            for j, (px, py) in enumerate(chips):
                got = piece(i, 2 * px + py, c)
                pltpu.make_async_remote_copy(
                    src_ref=got, dst_ref=got, send_sem=send_sems.at[3 * i + j], recv_sem=recv_sems.at[3 * i + j],
                    device_id=me, device_id_type=MESH).wait_recv()
                fw = pltpu.make_async_remote_copy(
                    src_ref=got, dst_ref=got, send_sem=fsend.at[3 * i + j], recv_sem=frecv.at[3 * i + j],
                    device_id=sibling, device_id_type=MESH)
                fw.start()
                forwards.append(fw)
        for i in range(n):
            for j, (px, py) in enumerate(chips):
                other = piece(i, 2 * px + py, 1 - c)
                pltpu.make_async_remote_copy(
                    src_ref=other, dst_ref=other, send_sem=fsend.at[3 * i + j], recv_sem=frecv.at[3 * i + j],
                    device_id=me, device_id_type=MESH).wait_recv()
        for cp in sends + forwards:
            cp.wait_send()

    return pl.pallas_call(
        body, name=name,
        out_shape=[jax.ShapeDtypeStruct(w.shape, w.dtype) for w in ws],
        in_specs=[_HBM] * n, out_specs=[_HBM] * n, input_output_aliases={i: i for i in range(n)},
        scratch_shapes=[pltpu.SemaphoreType.DMA((3 * n,)), pltpu.SemaphoreType.DMA((3 * n,)),
                        pltpu.SemaphoreType.DMA((3 * n,)), pltpu.SemaphoreType.DMA((3 * n,))],
    )(*ws)


def _swap_halves(gs, name):
    n = len(gs)

    def body(*refs):
        in_refs, out_refs = refs[:n], refs[n:2 * n]
        send_sems, recv_sems = refs[2 * n:]
        x, y, c = _position()
        cps = []
        for i in range(n):
            h = gs[i].shape[1] // 2
            cp = pltpu.make_async_remote_copy(
                src_ref=in_refs[i].at[:, pl.ds((1 - c) * h, h), :], dst_ref=out_refs[i],
                send_sem=send_sems.at[i], recv_sem=recv_sems.at[i], device_id=(x, y, 1 - c), device_id_type=MESH)
            cp.start()
            cps.append(cp)
        for cp in cps:
            cp.wait()

    return pl.pallas_call(
        body, name=name,
        out_shape=[jax.ShapeDtypeStruct((g.shape[0], g.shape[1] // 2, g.shape[2]), g.dtype) for g in gs],
        in_specs=[_HBM] * n, out_specs=[_HBM] * n,
        scratch_shapes=[pltpu.SemaphoreType.DMA((n,)), pltpu.SemaphoreType.DMA((n,))],
    )(*gs)


def _scatter_chips(ps, name):
    n = len(ps)

    def body(*refs):
        in_refs, out_refs = refs[:n], refs[n:2 * n]
        send_sems, recv_sems = refs[2 * n:]
        x, y, c = _position()
        me = (x, y, c)
        chips = [(1 - x, y), (x, 1 - y), (1 - x, 1 - y)]
        cps = []
        for i in range(n):
            for j, (px, py) in enumerate(chips):
                cp = pltpu.make_async_remote_copy(
                    src_ref=in_refs[i].at[2 * px + py], dst_ref=out_refs[i].at[j],
                    send_sem=send_sems.at[3 * i + j], recv_sem=recv_sems.at[3 * i + j],
                    device_id=(px, py, c), device_id_type=MESH)
                cp.start()
                cps.append(cp)
        for i in range(n):
            for j in range(3):
                got = out_refs[i].at[j]
                pltpu.make_async_remote_copy(
                    src_ref=got, dst_ref=got, send_sem=send_sems.at[3 * i + j], recv_sem=recv_sems.at[3 * i + j],
                    device_id=me, device_id_type=MESH).wait_recv()
        for cp in cps:
            cp.wait_send()

    return pl.pallas_call(
        body, name=name,
        out_shape=[jax.ShapeDtypeStruct((3,) + p.shape[1:], p.dtype) for p in ps],
        in_specs=[_HBM] * n, out_specs=[_HBM] * n,
        scratch_shapes=[pltpu.SemaphoreType.DMA((3 * n,)), pltpu.SemaphoreType.DMA((3 * n,))],
    )(*ps)


def _join_halves(rs, name):
    n = len(rs)

    def body(*refs):
        out_refs = refs[n:2 * n]
        send_sems, recv_sems = refs[2 * n:]
        x, y, c = _position()
        cps = []
        for i in range(n):
            h = rs[i].shape[0] // 2
            mine = out_refs[i].at[pl.ds(c * h, h), :]
            cp = pltpu.make_async_remote_copy(
                src_ref=mine, dst_ref=mine, send_sem=send_sems.at[i], recv_sem=recv_sems.at[i],
                device_id=(x, y, 1 - c), device_id_type=MESH)
            cp.start()
            cps.append(cp)
        for i in range(n):
            h = rs[i].shape[0] // 2
            other = out_refs[i].at[pl.ds((1 - c) * h, h), :]
            pltpu.make_async_remote_copy(
                src_ref=other, dst_ref=other, send_sem=send_sems.at[i], recv_sem=recv_sems.at[i],
                device_id=(x, y, c), device_id_type=MESH).wait_recv()
        for cp in cps:
            cp.wait_send()

    return pl.pallas_call(
        body, name=name,
        out_shape=[jax.ShapeDtypeStruct(r.shape, r.dtype) for r in rs],
        in_specs=[_HBM] * n, out_specs=[_HBM] * n, input_output_aliases={i: i for i in range(n)},
        scratch_shapes=[pltpu.SemaphoreType.DMA((n,)), pltpu.SemaphoreType.DMA((n,))],
    )(*rs)


def _t5_buckets_block():
    qi = np.arange(BLOCK)[:, None]
    ki = np.arange(2 * BLOCK)[None, :]
    n = np.maximum(qi + BLOCK - ki, 0)
    max_exact = NUM_BUCKETS // 2
    large = max_exact + (np.log(np.maximum(n, 1) / max_exact) / np.log(MAX_DISTANCE / max_exact)
                         * (NUM_BUCKETS - max_exact)).astype(np.int32)
    large = np.minimum(large, NUM_BUCKETS - 1)
    return np.where(n < max_exact, n, large).astype(np.int32)


def _discretise(lambda_re, lambda_im, log_step, b_re, b_im):
    lam_re = jnp.minimum(lambda_re, -1e-4)
    lam_im = lambda_im
    delta = jnp.exp(log_step)[:, None]
    mag = jnp.exp(lam_re * delta)
    ang = lam_im * delta
    abar_re, abar_im = mag * jnp.cos(ang), mag * jnp.sin(ang)
    num_re, num_im = abar_re - 1.0, abar_im
    den = lam_re * lam_re + lam_im * lam_im
    f_re = (num_re * lam_re + num_im * lam_im) / den
    f_im = (num_im * lam_re - num_re * lam_im) / den
    bbar_re = f_re[..., None] * b_re - f_im[..., None] * b_im
    bbar_im = f_re[..., None] * b_im + f_im[..., None] * b_re
    return abar_re, abar_im, bbar_re, bbar_im


def _interleave(v, nc):
    s, w = v.shape
    return v.reshape(nc, s // nc, w).transpose(1, 0, 2).reshape(s, w)


def _deinterleave(v, nc):
    s, w = v.shape
    return v.reshape(s // nc, nc, w).transpose(1, 0, 2).reshape(s, w)


_SMALL = ("norm1_g", "b_in", "attn_sinks", "rel_bias", "lambda_re", "lambda_im", "log_step", "ssm_b_re",
          "ssm_b_im", "ssm_c_re", "ssm_c_im", "ssm_d", "b_glu", "norm2_g", "final_g")


def _pack(parts):
    rows = []
    for p in parts:
        f = p.reshape(-1).astype(F32)
        pad = (-f.shape[0]) % 128
        rows.append(jnp.pad(f, (0, pad)).reshape(-1, 128))
    out = jnp.concatenate(rows, axis=0)
    pad = (-out.shape[0]) % 256
    return jnp.pad(out, ((0, pad), (0, 0)))


def _unpack(packed, shapes):
    res, r = [], 0
    for shp in shapes:
        size = int(np.prod(shp))
        nr = -(-size // 128)
        res.append(packed[r:r + nr].reshape(-1)[:size].reshape(shp))
        r += nr
    return res


def kernel(x, c, w_ada, b_ada, norm1_g, w_in, b_in, attn_sinks, rel_bias, lambda_re, lambda_im, log_step, ssm_b_re, ssm_b_im, ssm_c_re, ssm_c_im, ssm_d, w_glu, b_glu, w_attn_proj, w_ssm_proj, w_out, norm2_g, w_ff1, w_ff2, final_g, loss_target, m_w_ada, m_b_ada, m_norm1_g, m_w_in, m_b_in, m_attn_sinks, m_rel_bias, m_lambda_re, m_lambda_im, m_log_step, m_ssm_b_re, m_ssm_b_im, m_ssm_c_re, m_ssm_c_im, m_ssm_d, m_w_glu, m_b_glu, m_w_attn_proj, m_w_ssm_proj, m_w_out, m_norm2_g, m_w_ff1, m_w_ff2, m_final_g, v_w_ada, v_b_ada, v_norm1_g, v_w_in, v_b_in, v_attn_sinks, v_rel_bias, v_lambda_re, v_lambda_im, v_log_step, v_ssm_b_re, v_ssm_b_im, v_ssm_c_re, v_ssm_c_im, v_ssm_d, v_w_glu, v_b_glu, v_w_attn_proj, v_w_ssm_proj, v_w_out, v_norm2_g, v_w_ff1, v_w_ff2, v_final_g):
    given = dict(locals())
    S, D = x.shape[1], x.shape[2]
    SSM_W = w_glu.shape[2]
    G = SSM_W // SSM_GROUP_CH
    NST = G * SSM_STATE
    DFF = w_ff2.shape[1] * N_CHIPS
    INW = w_in.shape[2] * N_CHIPS
    o_q, o_k, o_v, o_u = 0, ATTN_WIDTH, ATTN_WIDTH + KV_WIDTH, ATTN_WIDTH + 2 * KV_WIDTH
    o_ga, o_gs = o_u + SSM_W, o_u + SSM_W + D
    mx, my, mc = _position()
    my_chip = 2 * mx + my
    my_b = 4 * mx + 2 * my + mc

    xv, tgt = x[0], loss_target[0]

    big = dict(w_in=w_in[0], w_glu=w_glu[0], w_attn_proj=w_attn_proj[0], w_ssm_proj=w_ssm_proj[0],
               w_out=w_out[0], w_ff1=w_ff1[0], w_ff2=w_ff2[0])
    big_names = list(big)
    colsharded = {"w_in", "w_attn_proj", "w_ssm_proj", "w_ff1"}
    chip_sel = my_chip.astype(jnp.int32).reshape(1)
    slots = [_cast_into_slot(big[k], chip_sel, "cast_" + k) for k in big_names]
    gathered = dict(zip(big_names, _gather_weights(slots, "gather_weights")))

    def wop(k):
        g = gathered[k]
        return _Op(g, N_CHIPS) if k in colsharded else _Op(g.reshape(g.shape[0] * g.shape[1], g.shape[2]))

    c_all = _allgather8(jnp.pad(c, ((0, 7), (0, 0))), "gather_c").reshape(N_DEV, 8, D)[:, 0]
    c16 = jnp.pad(c_all, ((0, 8), (0, 0)))
    b_ada_mine = lax.dynamic_slice(b_ada.reshape(N_CHIPS, -1), (my_chip, 0), (1, w_ada.shape[2]))
    mod_sh = _mm(c16, w_ada[0], "NN", name="mod", M=16, N=w_ada.shape[2], K=D, a_fn=_silu,
                 epilogue=lambda acc, b: (acc + b,), extras=[(b_ada_mine, "row")])
    mod_all = _allgather8(mod_sh[:8], "gather_mod").reshape(N_DEV, 8, -1)
    mod_row = jnp.concatenate(
        [lax.dynamic_slice(mod_all, (2 * j, my_b, 0), (1, 1, mod_all.shape[2]))[0] for j in range(N_CHIPS)], axis=1)
    sh1, sc1, g1, sh2, sc2, g2 = [mod_row[:, i * D:(i + 1) * D] for i in range(6)]

    disc_in = (lambda_re[0], lambda_im[0], log_step[0], ssm_b_re[0], ssm_b_im[0])
    (abar_re, abar_im, bbar_re, bbar_im), disc_vjp = jax.vjp(_discretise, *disc_in)
    eye = jnp.eye(G, dtype=F32)
    bd = jnp.concatenate([jnp.einsum("gnp,gh->gphn", bb, eye).reshape(SSM_W, NST) for bb in (bbar_re, bbar_im)], axis=1)
    cd = jnp.concatenate([jnp.einsum("gpn,gh->gnhp", cc, eye).reshape(NST, SSM_W)
                          for cc in (ssm_c_re[0], -ssm_c_im[0])], axis=0)
    a_fwd = jnp.stack([abar_re.reshape(1, NST), abar_im.reshape(1, NST)])
    a_bwd = jnp.stack([abar_re.reshape(1, NST), -abar_im.reshape(1, NST)])
    d_row = ssm_d

    buckets = _t5_buckets_block()
    onehot = (jnp.asarray(buckets.reshape(-1, 1)) == jnp.arange(128, dtype=jnp.int32)[None, :]).astype(BF16)
    rb_hi = rel_bias.astype(BF16)
    rb_lo = (rel_bias - rb_hi.astype(F32)).astype(BF16)
    rb_lo2 = (rel_bias - rb_hi.astype(F32) - rb_lo.astype(F32)).astype(BF16)
    rb3 = jnp.pad(jnp.concatenate([rb_hi, rb_lo, rb_lo2], axis=1),
                  ((0, 128 - NUM_BUCKETS), (0, 128 - 3 * N_Q_HEADS)))
    b3 = _mm(onehot, rb3, "NN", name="rel_bias_rows", M=BLOCK * 2 * BLOCK, N=128, K=128)
    bias = (b3[:, :N_Q_HEADS] + b3[:, N_Q_HEADS:2 * N_Q_HEADS]) + b3[:, 2 * N_Q_HEADS:3 * N_Q_HEADS]
    bias = jnp.transpose(bias.reshape(BLOCK, 2 * BLOCK, N_Q_HEADS), (2, 0, 1))
    sinks_b = jnp.broadcast_to(attn_sinks[0][:, None], (N_Q_HEADS, 128))
    sinks_kv = jnp.pad(sinks_b.reshape(N_KV_HEADS, GQA_GROUP, 128), ((0, 0), (0, 8 - GQA_GROUP), (0, 0)))

    h1 = _rowwise(_norm_mod, [(xv, "tile", D), (norm1_g, "row", D), (sh1, "row", D), (sc1, "row", D)],
                  [(D, BF16)], [], name="norm1", rows=S)[0]
    proj = _mm(h1, wop("w_in"), "NN", name="proj", M=S, N=INW, K=D,
               epilogue=lambda acc, b: (acc + b,), extras=[(b_in, "row")])

    def heads(v2d, nh):
        return v2d.reshape(S, nh, HEAD_DIM).transpose(1, 0, 2)

    def unheads(v3d):
        return v3d.transpose(1, 0, 2).reshape(S, -1)

    qh = heads(proj[:, o_q:o_k], N_Q_HEADS)
    kh = heads(proj[:, o_k:o_v], N_KV_HEADS)
    vh = heads(proj[:, o_v:o_u], N_KV_HEADS)
    attn = unheads(_attn_fwd(qh, kh, vh, sinks_b, bias, "attn_fwd"))
    y_attn = _mm(attn, wop("w_attn_proj"), "NN", name="attn_proj", M=S, N=D, K=ATTN_WIDTH)

    u = proj[:, o_u:o_ga]
    u_il = _interleave(u, SCAN_CHUNKS)
    bu = _mm(u_il, bd, "NN", name="ssm_bu", M=S, N=2 * NST, K=SSM_W, out_nsh=2)
    xs = _scan(a_fwd, bu, None, reverse=False, name="scan_fwd", tc=256)
    y_il = _mm(_Op(xs, 2), cd, "NN", name="ssm_y", M=S, N=SSM_W, K=2 * NST,
               epilogue=lambda acc, uu, dd: (acc + dd * uu,), extras=[(u_il, "tile"), (d_row, "row")])
    y = _deinterleave(y_il, SCAN_CHUNKS)
    z0b = _rowwise(_gelu, [(y, "tile", SSM_W)], [(SSM_W, BF16)], [], name="gelu", rows=S)[0]
    z, t_glu = _mm(z0b, wop("w_glu"), "NN", name="glu", M=S, N=SSM_W, K=SSM_W, out_dtypes=(BF16, F32),
                   epilogue=lambda acc, b, yy: (_gelu(yy) * _sigmoid(acc + b), acc + b),
                   extras=[(b_glu, "row"), (y, "tile")])
    y_ssm = _mm(z, wop("w_ssm_proj"), "NN", name="ssm_proj", M=S, N=D, K=SSM_W)

    merged = _rowwise(_merge, [(_Op(proj, coff=o_ga), "tile", D), (_Op(proj, coff=o_gs), "tile", D),
                               (y_attn, "tile", D), (y_ssm, "tile", D)], [(D, BF16)], [], name="merge", rows=S)[0]
    mo, x2 = _mm(merged, wop("w_out"), "NN", name="out_proj", M=S, N=D, K=D, out_dtypes=(F32, F32),
                 epilogue=lambda acc, xx, gg: (acc, xx + gg * acc), extras=[(xv, "tile"), (g1, "row")])
    h2 = _rowwise(_norm_mod, [(x2, "tile", D), (norm2_g, "row", D), (sh2, "row", D), (sc2, "row", D)],
                  [(D, BF16)], [], name="norm2", rows=S)[0]
    a_b, r_b = _mm(h2, wop("w_ff1"), "NN", name="ff1", M=S, N=DFF, K=D, out_dtypes=(BF16, BF16),
                   epilogue=lambda acc: (acc, jnp.square(jnp.maximum(acc, 0.0))))
    ff, x3 = _mm(r_b, wop("w_ff2"), "NN", name="ff2", M=S, N=D, K=DFF, out_dtypes=(F32, F32),
                 epilogue=lambda acc, xx, gg: (acc, xx + gg * acc), extras=[(x2, "tile"), (g2, "row")])

    def final_fn(x3b, gf, tb):
        def f(xx, gg):
            yv = xx * lax.rsqrt(jnp.mean(xx * xx, axis=-1, keepdims=True) + EPS) * gg
            err = jnp.square(yv - tb)
            return 0.5 * jnp.sum(jnp.mean(err, axis=-1, keepdims=True), axis=0, keepdims=True)
        lv, vjp = jax.vjp(f, x3b, gf)
        dx, dg = vjp(jnp.ones((1, 1), F32))
        return dx, dg, jnp.broadcast_to(lv, (1, 128))

    dx3, g_final, loss_acc = _rowwise(final_fn, [(x3, "tile", D), (final_g.reshape(1, D), "row", D), (tgt, "tile", D)],
                                      [(D, F32)], [D, 128], name="final", rows=S)

    def ff_out_bwd(dx3b, ffb, g2b):
        return dx3b * g2b, jnp.sum(dx3b * ffb, axis=0, keepdims=True)

    dff, d_g2 = _rowwise(ff_out_bwd, [(dx3, "tile", D), (ff, "tile", D), (g2, "row", D)], [(D, BF16)], [D],
                         name="ff_out_bwd", rows=S)
    da = _mm(dff, wop("w_ff2"), "NT", name="ff2_dx", M=S, N=DFF, K=D, out_dtypes=(BF16,),
             epilogue=lambda acc, ab: (acc * (2.0 * jnp.maximum(ab.astype(F32), 0.0)),), extras=[(a_b, "tile")])
    g_w_ff2 = _mm(r_b, dff, "TN", name="ff2_dw", M=DFF, N=D, K=S, out_dtypes=(BF16,), tk=1024)
    dh2 = _mm(da, wop("w_ff1"), "NT", name="ff1_dx", M=S, N=D, K=DFF)
    g_w_ff1 = _mm(h2, da, "TN", name="ff1_dw", M=D, N=DFF, K=S, out_dtypes=(BF16,), out_nsh=N_CHIPS, tk=1024)

    def norm2_bwd(x2b, dh2b, dx3b, mob, gn, shb, scb, g1b):
        _, vjp = jax.vjp(_norm_mod, x2b, gn, shb, scb)
        dx, dg, dsh, dsc = vjp(dh2b)
        dx2b = dx + dx3b
        return dx2b, dx2b * g1b, dg, dsh, dsc, jnp.sum(dx2b * mob, axis=0, keepdims=True)

    dx2, dmo, g_norm2, d_sh2, d_sc2, d_g1 = _rowwise(
        norm2_bwd, [(x2, "tile", D), (dh2, "tile", D), (dx3, "tile", D), (mo, "tile", D), (norm2_g, "row", D),
                    (sh2, "row", D), (sc2, "row", D), (g1, "row", D)],
        [(D, F32), (D, BF16)], [D, D, D, D], name="norm2_bwd", rows=S, tr=128)
    dmerged = _mm(dmo, wop("w_out"), "NT", name="out_dx", M=S, N=D, K=D)
    g_w_out = _mm(merged, dmo, "TN", name="out_dw", M=D, N=D, K=S, out_dtypes=(BF16,), tk=1024)

    def merge_bwd(gab, gsb, yab, ysb, dmb):
        _, vjp = jax.vjp(_merge, gab, gsb, yab, ysb)
        return vjp(dmb)

    d_ga, d_gs, dy_attn, dy_ssm = _rowwise(
        merge_bwd, [(_Op(proj, coff=o_ga), "tile", D), (_Op(proj, coff=o_gs), "tile", D), (y_attn, "tile", D),
                    (y_ssm, "tile", D), (dmerged, "tile", D)],
        [(D, BF16), (D, BF16), (D, BF16), (D, BF16)], [], name="merge_bwd", rows=S, tr=128)

    dattn = _mm(dy_attn, wop("w_attn_proj"), "NT", name="attn_proj_dx", M=S, N=ATTN_WIDTH, K=D)
    g_w_attn_proj = _mm(attn, dy_attn, "TN", name="attn_proj_dw", M=ATTN_WIDTH, N=D, K=S, out_dtypes=(BF16,),
                        out_nsh=N_CHIPS, tk=1024)
    dqh, dkh, dvh, dsink_blk, dbias = _attn_bwd(qh, kh, vh, heads(dattn, N_Q_HEADS), sinks_kv, bias, "attn_bwd")
    g_sinks = dsink_blk[:, :GQA_GROUP, 0].reshape(1, N_Q_HEADS)
    g_rel = _mm(dbias.reshape(N_Q_HEADS, -1), onehot, "NN", name="rel_bias_dw", M=N_Q_HEADS, N=128,
                K=BLOCK * 2 * BLOCK, tk=4096)
    g_rel_bias = g_rel[:, :NUM_BUCKETS].T

    dz = _mm(dy_ssm, wop("w_ssm_proj"), "NT", name="ssm_proj_dx", M=S, N=SSM_W, K=D)
    g_w_ssm_proj = _mm(z, dy_ssm, "TN", name="ssm_proj_dw", M=SSM_W, N=D, K=S, out_dtypes=(BF16,),
                       out_nsh=N_CHIPS, tk=1024)

    def glu_bwd(dzb, yb, tb):
        z0 = _gelu(yb)
        sg = _sigmoid(tb)
        dt = dzb * z0 * sg * (1.0 - sg)
        return dt, dzb * sg, jnp.sum(dt, axis=0, keepdims=True)

    dt_b, dz0a, g_b_glu = _rowwise(glu_bwd, [(dz, "tile", SSM_W), (y, "tile", SSM_W), (t_glu, "tile", SSM_W)],
                                   [(SSM_W, BF16), (SSM_W, F32)], [SSM_W], name="glu_bwd", rows=S)

    def gelu_bwd(acc, dz0ab, yb):
        _, vjp = jax.vjp(_gelu, yb)
        return (vjp(acc + dz0ab)[0],)

    dy = _mm(dt_b, wop("w_glu"), "NT", name="glu_dx", M=S, N=SSM_W, K=SSM_W, epilogue=gelu_bwd,
             extras=[(dz0a, "tile"), (y, "tile")])
    g_w_glu = _mm(z0b, dt_b, "TN", name="glu_dw", M=SSM_W, N=SSM_W, K=S, out_dtypes=(BF16,), tk=1024)
    dy_il = _interleave(dy, SCAN_CHUNKS)
    dxs = _mm(dy_il, cd, "NT", name="ssm_dx", M=S, N=2 * NST, K=SSM_W, out_nsh=2)
    g_cd = _mm(_Op(xs, 2), dy_il, "TN", name="ssm_dc", M=2 * NST, N=SSM_W, K=S, tk=1024)
    lam, d_abar = _scan(a_bwd, dxs, xs, reverse=True, name="scan_bwd", tc=128)

    def du_fn(acc, dyb, dd):
        return (acc + dd * dyb,)

    du_il = _mm(_Op(lam, 2), bd, "NT", name="ssm_du", M=S, N=SSM_W, K=2 * NST, epilogue=du_fn,
                extras=[(dy_il, "tile"), (d_row, "row")])
    g_bd = _mm(u_il, _Op(lam, 2), "TN", name="ssm_db", M=SSM_W, N=2 * NST, K=S, tk=1024)
    g_ssm_d = _rowwise(lambda dyb, ub: (jnp.sum(dyb * ub, axis=0, keepdims=True),),
                       [(dy_il, "tile", SSM_W), (u_il, "tile", SSM_W)], [], [SSM_W], name="ssm_dd", rows=S)[0]
    du = _deinterleave(du_il, SCAN_CHUNKS)

    g_cd4 = g_cd.reshape(2, G, SSM_STATE, G, SSM_GROUP_CH)
    g_c_re = jnp.einsum("gnhp,gh->gpn", g_cd4[0], eye)
    g_c_im = -jnp.einsum("gnhp,gh->gpn", g_cd4[1], eye)
    g_bd4 = g_bd.reshape(G, SSM_GROUP_CH, 2, G, SSM_STATE)
    g_bbar_re = jnp.einsum("gphn,gh->gnp", g_bd4[:, :, 0], eye)
    g_bbar_im = jnp.einsum("gphn,gh->gnp", g_bd4[:, :, 1], eye)
    g_lre, g_lim, g_lstep, g_bre, g_bim = disc_vjp(
        (d_abar[0].reshape(G, SSM_STATE), d_abar[1].reshape(G, SSM_STATE), g_bbar_re, g_bbar_im))

    dproj = jnp.concatenate([unheads(dqh).astype(BF16), unheads(dkh).astype(BF16), unheads(dvh).astype(BF16),
                             du.astype(BF16), d_ga, d_gs], axis=1)
    dh1 = _mm(dproj, wop("w_in"), "NT", name="proj_dx", M=S, N=D, K=INW)
    g_w_in = _mm(h1, dproj, "TN", name="proj_dw", M=D, N=INW, K=S, out_dtypes=(BF16,), out_nsh=N_CHIPS, tk=1024)
    g_b_in = _rowwise(lambda d: (jnp.sum(d.astype(F32), axis=0, keepdims=True),), [(dproj, "tile", INW)], [], [INW],
                      name="proj_db", rows=S)[0]

    def norm1_bwd(xb, dhb, dresb, gn, shb, scb):
        _, vjp = jax.vjp(_norm_mod, xb, gn, shb, scb)
        dx, dg, dsh, dsc = vjp(dhb)
        return dx + dresb, dg, dsh, dsc

    grad_x, g_norm1, d_sh1, d_sc1 = _rowwise(
        norm1_bwd, [(xv, "tile", D), (dh1, "tile", D), (dx2, "tile", D), (norm1_g, "row", D), (sh1, "row", D),
                    (sc1, "row", D)], [(D, F32)], [D, D, D], name="norm1_bwd", rows=S)

    dmod_row = jnp.concatenate([d_sh1, d_sc1, d_g1, d_sh2, d_sc2, d_g2], axis=1)
    dmod_all = _allgather8(jnp.pad(dmod_row, ((0, 7), (0, 0))), "gather_dmod").reshape(N_DEV, 8, -1)[:, 0]
    g_b_ada = _sum_lead(dmod_all.reshape(N_DEV, -1, 128), "b_ada_dw").reshape(1, -1)
    dmod_mine = lax.dynamic_slice(dmod_all.reshape(N_DEV, N_CHIPS, -1), (0, my_chip, 0), (N_DEV, 1, w_ada.shape[2]))[:, 0]
    g_w_ada = _mm(c16, jnp.pad(dmod_mine, ((0, 8), (0, 0))), "TN", name="ada_dw", M=D, N=w_ada.shape[2], K=16,
                  a_fn=_silu)

    small_g = dict(norm1_g=g_norm1, b_in=g_b_in, attn_sinks=g_sinks, rel_bias=g_rel_bias, lambda_re=g_lre[None],
                   lambda_im=g_lim[None], log_step=g_lstep[None], ssm_b_re=g_bre[None], ssm_b_im=g_bim[None],
                   ssm_c_re=g_c_re[None], ssm_c_im=g_c_im[None], ssm_d=g_ssm_d, b_glu=g_b_glu, norm2_g=g_norm2,
                   final_g=g_final.reshape(D))
    packed = _pack([loss_acc[:, :1]] + [small_g[k] for k in _SMALL])
    rows = packed.shape[0]
    summed = _sum_lead(_allgather8(packed, "gather_small").reshape(N_DEV, rows, 128), "small_sum")
    small_shapes = [(1,)] + [given[k].shape for k in _SMALL]
    parts = _unpack(summed, small_shapes)
    loss = parts[0].reshape(())
    grads = dict(zip(_SMALL, parts[1:]))
    grads["b_ada"] = g_b_ada
    grads["w_ada"] = g_w_ada[None]

    big_g = dict(w_in=g_w_in, w_glu=g_w_glu, w_attn_proj=g_w_attn_proj, w_ssm_proj=g_w_ssm_proj, w_out=g_w_out,
                 w_ff1=g_w_ff1, w_ff2=g_w_ff2)
    g_list = []
    for k in big_names:
        gk = big_g[k]
        if k not in colsharded:
            gk = gk.reshape(N_CHIPS, gk.shape[0] // N_CHIPS, gk.shape[1])
        g_list.append(gk)
    half = mc.astype(jnp.int32).reshape(1)
    t1 = _swap_halves(g_list, "rs_swap")
    p_list = [_add_half(g, t, half, "rs_add_" + k) for g, t, k in zip(g_list, t1, big_names)]
    t2 = _scatter_chips(p_list, "rs_scatter")
    sel = jnp.stack([my_chip, mc]).astype(jnp.int32)
    r_list = [_sum_own(p, t, sel, "rs_sum_" + k) for p, t, k in zip(p_list, t2, big_names)]
    full = _join_halves(r_list, "rs_join")
    for k, f in zip(big_names, full):
        grads[k] = f[None]

    deltas, new_m, new_v = {}, {}, {}
    for k in big_names + ["w_ada"]:
        d_, m_, v_ = _adamw(given[k][0], grads[k][0], given["m_" + k][0], given["v_" + k][0], "adamw_" + k)
        deltas[k], new_m[k], new_v[k] = d_[None], m_[None], v_[None]
    small_all = list(_SMALL) + ["b_ada"]
    shapes = [given[k].shape for k in small_all]
    pw, pg = _pack([given[k] for k in small_all]), _pack([grads[k] for k in small_all])
    pm, pv = _pack([given["m_" + k] for k in small_all]), _pack([given["v_" + k] for k in small_all])
    d_, m_, v_ = _adamw(pw, pg, pm, pv, "adamw_small")
    for k, dd, mm, vv in zip(small_all, _unpack(d_, shapes), _unpack(m_, shapes), _unpack(v_, shapes)):
        deltas[k], new_m[k], new_v[k] = dd, mm, vv
        grads[k] = grads[k].reshape(given[k].shape)

    names = ["w_ada", "b_ada", "norm1_g", "w_in", "b_in", "attn_sinks", "rel_bias", "lambda_re", "lambda_im",
             "log_step", "ssm_b_re", "ssm_b_im", "ssm_c_re", "ssm_c_im", "ssm_d", "w_glu", "b_glu", "w_attn_proj",
             "w_ssm_proj", "w_out", "norm2_g", "w_ff1", "w_ff2", "final_g"]
    return (loss, grad_x[None], *[grads[n] for n in names], *[deltas[n] for n in names],
            *[new_m[n] for n in names], *[new_v[n] for n in names])
```

```python
import math

import numpy as np
import jax
import jax.numpy as jnp
from jax import lax
from jax.experimental import pallas as pl
from jax.experimental.pallas import tpu as pltpu

F32 = jnp.float32
BF16 = jnp.bfloat16
MESH = pl.DeviceIdType.MESH

HEAD_DIM = 64
N_Q_HEADS = 16
N_KV_HEADS = 4
GQA_GROUP = N_Q_HEADS // N_KV_HEADS
ATTN_WIDTH = N_Q_HEADS * HEAD_DIM
KV_WIDTH = N_KV_HEADS * HEAD_DIM
BLOCK = 128
NUM_BUCKETS = 32
MAX_DISTANCE = 128
NEG_INF = -1e30
SSM_GROUP_CH = 16
SSM_STATE = 64
EPS = 1e-6
ADAM_LR = 0.001
ADAM_B1 = 0.9
ADAM_B2 = 0.999
ADAM_EPS = 1e-08
ADAM_WD = 0.01
ADAM_STEP = 10

N_CHIPS = 4
N_DEV = 8
SCAN_CHUNKS = 8
VMEM_LIMIT_BYTES = 48 * 1024 * 1024


def _cparams(sem=None):
    return pltpu.CompilerParams(dimension_semantics=sem, vmem_limit_bytes=VMEM_LIMIT_BYTES)


class _Op:
    def __init__(self, arr, nsh=None, coff=0):
        self.arr, self.nsh, self.coff = arr, nsh, coff
        if nsh is None:
            self.rows, self.cols = arr.shape
        else:
            assert arr.shape[0] == nsh
            self.rows, self.cols = arr.shape[1], arr.shape[2] * nsh

    def spec(self, br, bc, idx):
        assert self.coff % bc == 0
        off = self.coff // bc
        if self.nsh is None:
            return pl.BlockSpec((br, bc), lambda *g: (idx(*g)[0], idx(*g)[1] + off))
        per = (self.cols // self.nsh) // bc
        assert per * bc * self.nsh == self.cols

        def imap(*g):
            r, c = idx(*g)
            c = c + off
            return (c // per, r, c % per)
        return pl.BlockSpec((None, br, bc), imap)


def _as_op(a):
    return a if isinstance(a, _Op) else _Op(a)


def _mm(a, b, mode, *, name, M, N, K, out_dtypes=(F32,), out_nsh=None, epilogue=None, extras=(),
        a_fn=None, ti=1024, tj=512, tk=2048):
    a, b = _as_op(a), _as_op(b)
    ti, tj, tk = min(ti, M), min(tj, N), min(tk, K)
    a_w = a.cols // a.nsh if a.nsh else None
    b_w = b.cols // b.nsh if b.nsh else None
    if a_w:
        ti, tk = (min(ti, a_w), tk) if mode == "TN" else (ti, min(tk, a_w))
    if b_w:
        tj, tk = (tj, min(tk, b_w)) if mode == "NT" else (min(tj, b_w), tk)
    if out_nsh:
        tj = min(tj, N // out_nsh)
    assert M % ti == 0 and N % tj == 0 and K % tk == 0, (name, M, N, K, ti, tj, tk)
    nk = K // tk
    if mode == "NN":
        a_spec = a.spec(ti, tk, lambda i, j, k: (i, k))
        b_spec = b.spec(tk, tj, lambda i, j, k: (k, j))
        dims = (((1,), (0,)), ((), ()))
    elif mode == "NT":
        a_spec = a.spec(ti, tk, lambda i, j, k: (i, k))
        b_spec = b.spec(tj, tk, lambda i, j, k: (j, k))
        dims = (((1,), (1,)), ((), ()))
    else:
        a_spec = a.spec(tk, ti, lambda i, j, k: (k, i))
        b_spec = b.spec(tk, tj, lambda i, j, k: (k, j))
        dims = (((0,), (0,)), ((), ()))
    ex_specs, ex_arrs = [], []
    for op, kind in extras:
        op = _as_op(op)
        if kind == "tile":
            ex_specs.append(op.spec(ti, tj, lambda i, j, k: (i, j)))
        else:
            ex_specs.append(op.spec(1, tj, lambda i, j, k: (0, j)))
        ex_arrs.append(op.arr)
    ne, no = len(ex_arrs), len(out_dtypes)
    if out_nsh is None:
        out_shapes = [jax.ShapeDtypeStruct((M, N), d) for d in out_dtypes]
        out_specs = [pl.BlockSpec((ti, tj), lambda i, j, k: (i, j)) for _ in out_dtypes]
    else:
        per = (N // out_nsh) // tj
        assert per * tj * out_nsh == N
        out_shapes = [jax.ShapeDtypeStruct((out_nsh, M, N // out_nsh), d) for d in out_dtypes]
        out_specs = [pl.BlockSpec((None, ti, tj), lambda i, j, k: (j // per, i, j % per)) for _ in out_dtypes]

    def body(a_ref, b_ref, *rest):
        ex_refs, out_refs, acc = rest[:ne], rest[ne:ne + no], rest[ne + no]
        k = pl.program_id(2)

        @pl.when(k == 0)
        def _():
            acc[...] = jnp.zeros_like(acc)

        av = a_ref[...]
        if a_fn is not None:
            av = a_fn(av)
        acc[...] += lax.dot_general(av.astype(BF16), b_ref[...].astype(BF16), dims,
                                    preferred_element_type=F32)

        @pl.when(k == nk - 1)
        def _():
            res = acc[...]
            outs = epilogue(res, *[r[...] for r in ex_refs]) if epilogue is not None else (res,)
            for o_ref, o in zip(out_refs, outs):
                o_ref[...] = o.astype(o_ref.dtype)

    outs = pl.pallas_call(
        body, name=name, grid=(M // ti, N // tj, nk),
        in_specs=[a_spec, b_spec] + ex_specs, out_specs=out_specs, out_shape=out_shapes,
        scratch_shapes=[pltpu.VMEM((ti, tj), F32)],
        compiler_params=_cparams(("parallel", "parallel", "arbitrary")),
    )(a.arr, b.arr, *ex_arrs)
    return outs[0] if no == 1 else outs


def _rowwise(fn, ins, outs, accs, *, name, rows, tr=256):
    tr = min(tr, rows)
    assert rows % tr == 0
    in_specs, arrs = [], []
    for op, kind, width in ins:
        op = _as_op(op)
        if kind == "tile":
            in_specs.append(op.spec(tr, width, lambda i: (i, 0)))
        else:
            in_specs.append(op.spec(op.rows, width, lambda i: (0, 0)))
        arrs.append(op.arr)
    ni, no, na = len(ins), len(outs), len(accs)
    out_shapes = [jax.ShapeDtypeStruct((rows, w), d) for w, d in outs]
    out_specs = [pl.BlockSpec((tr, w), lambda i: (i, 0)) for w, _ in outs]
    out_shapes += [jax.ShapeDtypeStruct((1, w), F32) for w in accs]
    out_specs += [pl.BlockSpec((1, w), lambda i: (0, 0)) for w in accs]

    def body(*refs):
        in_refs, out_refs, acc_refs = refs[:ni], refs[ni:ni + no], refs[ni + no:]
        res = fn(*[r[...] for r in in_refs])
        if not isinstance(res, (tuple, list)):
            res = (res,)
        for o_ref, r in zip(out_refs, res[:no]):
            o_ref[...] = r.astype(o_ref.dtype)
        if na:
            @pl.when(pl.program_id(0) == 0)
            def _():
                for a_ref in acc_refs:
                    a_ref[...] = jnp.zeros_like(a_ref)
            for a_ref, r in zip(acc_refs, res[no:]):
                a_ref[...] += r.astype(F32)

    res = pl.pallas_call(
        body, name=name, grid=(rows // tr,), in_specs=in_specs, out_specs=out_specs, out_shape=out_shapes,
        compiler_params=_cparams(("arbitrary",)),
    )(*arrs)
    return res


def _norm_mod(x, g, sh, sc):
    y = x * lax.rsqrt(jnp.mean(x * x, axis=-1, keepdims=True) + EPS) * g
    return y * (1.0 + sc) + sh


def _sigmoid(x):
    return 1.0 / (1.0 + jnp.exp(-x))


def _silu(x):
    return x * _sigmoid(x)


def _gelu(x):
    return 0.5 * x * (1.0 + jnp.tanh(math.sqrt(2.0 / math.pi) * (x + 0.044715 * (x * x * x))))


def _merge(ga, gs, ya, ys):
    return _sigmoid(ga) * ya + _sigmoid(gs) * ys


def _attn_head(q, kp, kc, vp, vc, sink, bias_p, bias_c, not_first):
    nt = (((1,), (1,)), ((), ()))
    nn = (((1,), (0,)), ((), ()))
    qb = q.astype(BF16)
    scale = HEAD_DIM ** -0.5
    sp = lax.dot_general(qb, kp.astype(BF16), nt, preferred_element_type=F32) * scale + bias_p
    sc = lax.dot_general(qb, kc.astype(BF16), nt, preferred_element_type=F32) * scale + bias_c
    qi = lax.broadcasted_iota(jnp.int32, sp.shape, 0) & (BLOCK - 1)
    ki = lax.broadcasted_iota(jnp.int32, sp.shape, 1)
    sp = jnp.where(jnp.logical_and(ki > qi, not_first), sp, NEG_INF)
    sc = jnp.where(ki <= qi, sc, NEG_INF)
    m = jnp.maximum(jnp.maximum(jnp.max(sp, axis=-1, keepdims=True), jnp.max(sc, axis=-1, keepdims=True)), sink)
    m = lax.stop_gradient(m)
    pp = jnp.exp(sp - m)
    pc = jnp.exp(sc - m)
    denom = jnp.sum(pp, axis=-1, keepdims=True) + jnp.sum(pc, axis=-1, keepdims=True) + jnp.exp(sink - m)
    o = lax.dot_general((pp / denom).astype(BF16), vp.astype(BF16), nn, preferred_element_type=F32)
    o = o + lax.dot_general((pc / denom).astype(BF16), vc.astype(BF16), nn, preferred_element_type=F32)
    return o


def _attn_fwd(qh, kh, vh, sinks, bias, name):
    s = qh.shape[1]
    nb = s // BLOCK
    G = GQA_GROUP
    R = G * BLOCK

    def body(q_ref, kp_ref, kc_ref, vp_ref, vc_ref, sink_ref, bias_ref, o_ref):
        not_first = pl.program_id(0) > 0
        for kv in range(N_KV_HEADS):
            hs = slice(kv * G, (kv + 1) * G)
            o = _attn_head(q_ref[hs].reshape(R, HEAD_DIM), kp_ref[kv], kc_ref[kv], vp_ref[kv], vc_ref[kv],
                           sink_ref[kv * R:(kv + 1) * R, 0:1],
                           bias_ref[hs, :, 0:BLOCK].reshape(R, BLOCK), bias_ref[hs, :, BLOCK:2 * BLOCK].reshape(R, BLOCK),
                           not_first)
            o_ref[hs] = o.reshape(G, BLOCK, HEAD_DIM).astype(o_ref.dtype)

    cur = lambda i: (0, i, 0)
    prev = lambda i: (0, jnp.maximum(i - 1, 0), 0)
    return pl.pallas_call(
        body, name=name, grid=(nb,),
        in_specs=[pl.BlockSpec((N_Q_HEADS, BLOCK, HEAD_DIM), cur),
                  pl.BlockSpec((N_KV_HEADS, BLOCK, HEAD_DIM), prev), pl.BlockSpec((N_KV_HEADS, BLOCK, HEAD_DIM), cur),
                  pl.BlockSpec((N_KV_HEADS, BLOCK, HEAD_DIM), prev), pl.BlockSpec((N_KV_HEADS, BLOCK, HEAD_DIM), cur),
                  pl.BlockSpec((N_Q_HEADS * BLOCK, 128), lambda i: (0, 0)),
                  pl.BlockSpec((N_Q_HEADS, BLOCK, 2 * BLOCK), lambda i: (0, 0, 0))],
        out_specs=pl.BlockSpec((N_Q_HEADS, BLOCK, HEAD_DIM), cur),
        out_shape=jax.ShapeDtypeStruct((N_Q_HEADS, s, HEAD_DIM), BF16),
        compiler_params=_cparams(("arbitrary",)),
    )(qh, kh, kh, vh, vh, sinks, bias)


def _attn_bwd(qh, kh, vh, doh, sinks, bias, name):
    s = qh.shape[1]
    nb = s // BLOCK
    G = GQA_GROUP
    R = G * BLOCK

    def body(q_ref, kp_ref, kc_ref, vp_ref, vc_ref, do_ref, sink_ref, bias_ref,
             dq_ref, dk_ref, dv_ref, dsink_ref, dbias_ref, ck, cv):
        i = pl.program_id(1)

        @pl.when(i == 0)
        def _():
            dsink_ref[...] = jnp.zeros_like(dsink_ref)
            dbias_ref[...] = jnp.zeros_like(dbias_ref)
            ck[...] = jnp.zeros_like(ck)
            cv[...] = jnp.zeros_like(cv)

        @pl.when(i < nb)
        def _():
            not_first = i > 0
            _, vjp = jax.vjp(lambda q, a, b, c, d, sk, e, f: _attn_head(q, a, b, c, d, sk, e, f, not_first),
                             q_ref[...].reshape(R, HEAD_DIM), kp_ref[...], kc_ref[...], vp_ref[...], vc_ref[...],
                             sink_ref[:, 0:1], bias_ref[:, :, 0:BLOCK].reshape(R, BLOCK),
                             bias_ref[:, :, BLOCK:2 * BLOCK].reshape(R, BLOCK))
            dq, dkp, dkc, dvp, dvc, dsk, dbp, dbc = vjp(do_ref[...].reshape(R, HEAD_DIM).astype(F32))
            dq_ref[...] = dq.reshape(G, BLOCK, HEAD_DIM)
            dsink_ref[...] += jnp.broadcast_to(dsk, (R, 128))
            dbias_ref[:, :, 0:BLOCK] += dbp.reshape(G, BLOCK, BLOCK)
            dbias_ref[:, :, BLOCK:2 * BLOCK] += dbc.reshape(G, BLOCK, BLOCK)
            dk_ref[...] = ck[...] + dkp
            dv_ref[...] = cv[...] + dvp
            ck[...] = dkc
            cv[...] = dvc

        @pl.when(i == nb)
        def _():
            dk_ref[...] = ck[...]
            dv_ref[...] = cv[...]

    qcur = lambda kv, i: (kv, jnp.minimum(i, nb - 1), 0)
    kcur = lambda kv, i: (kv, jnp.minimum(i, nb - 1), 0)
    kprev = lambda kv, i: (kv, jnp.clip(i - 1, 0, nb - 1), 0)
    qspec = pl.BlockSpec((G, BLOCK, HEAD_DIM), qcur)
    kc_spec = pl.BlockSpec((None, BLOCK, HEAD_DIM), kcur)
    kp_spec = pl.BlockSpec((None, BLOCK, HEAD_DIM), kprev)
    return pl.pallas_call(
        body, name=name, grid=(N_KV_HEADS, nb + 1),
        in_specs=[qspec, kp_spec, kc_spec, kp_spec, kc_spec, qspec,
                  pl.BlockSpec((R, 128), lambda kv, i: (kv, 0)),
                  pl.BlockSpec((G, BLOCK, 2 * BLOCK), lambda kv, i: (kv, 0, 0))],
        out_specs=[qspec, kp_spec, kp_spec,
                   pl.BlockSpec((R, 128), lambda kv, i: (kv, 0)),
                   pl.BlockSpec((G, BLOCK, 2 * BLOCK), lambda kv, i: (kv, 0, 0))],
        out_shape=[jax.ShapeDtypeStruct((N_Q_HEADS, s, HEAD_DIM), F32),
                   jax.ShapeDtypeStruct((N_KV_HEADS, s, HEAD_DIM), F32),
                   jax.ShapeDtypeStruct((N_KV_HEADS, s, HEAD_DIM), F32),
                   jax.ShapeDtypeStruct((N_Q_HEADS * BLOCK, 128), F32),
                   jax.ShapeDtypeStruct((N_Q_HEADS, BLOCK, 2 * BLOCK), F32)],
        scratch_shapes=[pltpu.VMEM((BLOCK, HEAD_DIM), F32), pltpu.VMEM((BLOCK, HEAD_DIM), F32)],
        compiler_params=_cparams(("arbitrary", "arbitrary")),
    )(qh, kh, kh, vh, vh, doh, sinks, bias)


def _cmul(ar, ai, br, bi):
    return ar * br - ai * bi, ar * bi + ai * br


def _scan(a, b, xs_prev, *, reverse, name, tc):
    _, s, c = b.shape
    nc = SCAN_CHUNKS
    steps = s // nc
    with_da = xs_prev is not None
    unroll = 8 if steps % 8 == 0 else 1

    def shift(v, d):
        row = lax.broadcasted_iota(jnp.int32, v.shape, 0)
        if reverse:
            return jnp.where(row < nc - d, pltpu.roll(v, nc - d, 0), 0.0)
        return jnp.where(row >= d, pltpu.roll(v, d, 0), 0.0)

    def body(*refs):
        if with_da:
            a_ref, b_ref, xp_ref, x_ref, da_ref = refs
        else:
            a_ref, b_ref, x_ref = refs
        ar = jnp.broadcast_to(a_ref[0], (nc, tc))
        ai = jnp.broadcast_to(a_ref[1], (nc, tc))

        def row_of(step):
            j = (steps - 1 - step) if reverse else step
            return pl.multiple_of(j * nc, nc)

        def p1(step, st):
            sr, si = st
            r0 = row_of(step)
            mr, mi = _cmul(ar, ai, sr, si)
            sr = mr + b_ref[0, pl.ds(r0, nc), :]
            si = mi + b_ref[1, pl.ds(r0, nc), :]
            x_ref[0, pl.ds(r0, nc), :] = sr
            x_ref[1, pl.ds(r0, nc), :] = si
            return sr, si
        zero = jnp.zeros((nc, tc), F32)
        er, ei = lax.fori_loop(0, steps, p1, (zero, zero), unroll=unroll)

        def pw(step, st):
            return _cmul(ar, ai, *st)
        pr, pi_ = lax.fori_loop(0, steps, pw, (jnp.ones((nc, tc), F32), zero), unroll=unroll)
        cr, ci = shift(er, 1), shift(ei, 1)
        d = 1
        while d < nc:
            mr, mi = _cmul(pr, pi_, shift(cr, d), shift(ci, d))
            cr, ci = cr + mr, ci + mi
            pr, pi_ = _cmul(pr, pi_, pr, pi_)
            d *= 2

        def p2(step, st):
            qr, qi, dar, dai = st
            r0 = row_of(step)
            qr, qi = _cmul(ar, ai, qr, qi)
            fr, fi = _cmul(qr, qi, cr, ci)
            xr = x_ref[0, pl.ds(r0, nc), :] + fr
            xi = x_ref[1, pl.ds(r0, nc), :] + fi
            x_ref[0, pl.ds(r0, nc), :] = xr
            x_ref[1, pl.ds(r0, nc), :] = xi
            if with_da:
                jm = jnp.where(step == steps - 1, steps - 1, steps - 2 - step)
                rp = pl.multiple_of(jm * nc, nc)
                vr, vi = xp_ref[0, pl.ds(rp, nc), :], xp_ref[1, pl.ds(rp, nc), :]
                row = lax.broadcasted_iota(jnp.int32, (nc, tc), 0)
                first = step == steps - 1
                sel = jnp.logical_and(first, row == 0)
                vr = jnp.where(sel, 0.0, jnp.where(first, pltpu.roll(vr, 1, 0), vr))
                vi = jnp.where(sel, 0.0, jnp.where(first, pltpu.roll(vi, 1, 0), vi))
                dar = dar + xr * vr + xi * vi
                dai = dai + xi * vr - xr * vi
            return qr, qi, dar, dai
        _, _, dar, dai = lax.fori_loop(0, steps, p2, (jnp.ones((nc, tc), F32), zero, zero, zero), unroll=unroll)
        if with_da:
            da_ref[0] = jnp.sum(dar, axis=0, keepdims=True)
            da_ref[1] = jnp.sum(dai, axis=0, keepdims=True)

    blk = pl.BlockSpec((2, s, tc), lambda i: (0, 0, i))
    vec = pl.BlockSpec((2, 1, tc), lambda i: (0, 0, i))
    in_specs, args = [vec, blk], [a, b]
    out_specs, out_shape = [blk], [jax.ShapeDtypeStruct((2, s, c), F32)]
    if with_da:
        in_specs.append(blk)
        args.append(xs_prev)
        out_specs.append(vec)
        out_shape.append(jax.ShapeDtypeStruct((2, 1, c), F32))
    res = pl.pallas_call(
        body, name=name, grid=(c // tc,), in_specs=in_specs, out_specs=out_specs, out_shape=out_shape,
        compiler_params=_cparams(("arbitrary",)),
    )(*args)
    return res if with_da else res[0]


def _adamw(w, g, m, v, name):
    r, c = w.shape
    tr = r
    for cand in (512, 256, 128, 64, 32, 16, 8):
        if r % cand == 0 and cand * c * 4 <= 2 * 1024 * 1024:
            tr = cand
            break

    def body(w_ref, g_ref, m_ref, v_ref, d_ref, nm_ref, nv_ref):
        gv = g_ref[...]
        nm = ADAM_B1 * m_ref[...] + (1.0 - ADAM_B1) * gv
        nv = ADAM_B2 * v_ref[...] + (1.0 - ADAM_B2) * (gv * gv)
        m_hat = nm / (1.0 - ADAM_B1 ** ADAM_STEP)
        v_hat = nv / (1.0 - ADAM_B2 ** ADAM_STEP)
        d_ref[...] = -ADAM_LR * (m_hat / (jnp.sqrt(v_hat) + ADAM_EPS) + ADAM_WD * w_ref[...])
        nm_ref[...] = nm
        nv_ref[...] = nv

    spec = pl.BlockSpec((tr, c), lambda i: (i, 0))
    sds = jax.ShapeDtypeStruct((r, c), F32)
    return pl.pallas_call(body, name=name, grid=(r // tr,), in_specs=[spec] * 4, out_specs=[spec] * 3,
                          out_shape=[sds] * 3, compiler_params=_cparams(("parallel",)))(w, g, m, v)


def _sum_lead(x, name, out_dtype=F32):
    n, r, c = x.shape
    tr = r
    for cand in (512, 256, 128, 64, 32, 16, 8):
        if r % cand == 0 and n * cand * c * 4 <= 4 * 1024 * 1024:
            tr = cand
            break

    def body(x_ref, o_ref):
        acc = x_ref[0].astype(F32)
        for k in range(1, n):
            acc = acc + x_ref[k].astype(F32)
        o_ref[...] = acc.astype(o_ref.dtype)

    return pl.pallas_call(body, name=name, grid=(r // tr,),
                          in_specs=[pl.BlockSpec((n, tr, c), lambda i: (0, i, 0))],
                          out_specs=pl.BlockSpec((tr, c), lambda i: (i, 0)),
                          out_shape=jax.ShapeDtypeStruct((r, c), out_dtype),
                          compiler_params=_cparams(("parallel",)))(x)


def _row_tile(rows, row_bytes, budget, least=8):
    for cand in (1024, 512, 256, 128, 64, 32, 16, 8):
        if cand >= least and rows % cand == 0 and cand * row_bytes <= budget:
            return cand
    return rows


def _cast_into_slot(w, slot, name):
    r, c = w.shape
    tr = _row_tile(r, c * 4, 4 * 1024 * 1024, least=16)

    def body(slot_ref, w_ref, o_ref):
        o_ref[...] = w_ref[...].astype(o_ref.dtype)

    gs = pltpu.PrefetchScalarGridSpec(
        num_scalar_prefetch=1, grid=(r // tr,),
        in_specs=[pl.BlockSpec((tr, c), lambda i, s: (i, 0))],
        out_specs=pl.BlockSpec((None, tr, c), lambda i, s: (s[0], i, 0)))
    return pl.pallas_call(body, name=name, grid_spec=gs, out_shape=jax.ShapeDtypeStruct((N_CHIPS, r, c), BF16),
                          compiler_params=_cparams(("parallel",)))(slot, w)


def _sum_own(p, t, sel, name):
    _, h, c = p.shape
    tr = _row_tile(h, c * 4, 2 * 1024 * 1024, least=16)
    nblk = h // tr

    def body(sel_ref, p_ref, t_ref, o_ref):
        acc = p_ref[...].astype(F32)
        for k in range(3):
            acc = acc + t_ref[k].astype(F32)
        o_ref[...] = acc

    gs = pltpu.PrefetchScalarGridSpec(
        num_scalar_prefetch=1, grid=(nblk,),
        in_specs=[pl.BlockSpec((None, tr, c), lambda i, s: (s[0], i, 0)),
                  pl.BlockSpec((3, tr, c), lambda i, s: (0, i, 0))],
        out_specs=pl.BlockSpec((tr, c), lambda i, s: (s[1] * nblk + i, 0)))
    return pl.pallas_call(body, name=name, grid_spec=gs, out_shape=jax.ShapeDtypeStruct((2 * h, c), F32),
                          compiler_params=_cparams(("parallel",)))(sel, p, t)


def _add_half(g, t, half, name):
    n, r, c = g.shape
    h = r // 2
    tr = h
    for cand in (512, 256, 128, 64, 32, 16):
        if h % cand == 0 and cand * c * 2 <= 2 * 1024 * 1024:
            tr = cand
            break
    nblk = h // tr

    def body(half_ref, g_ref, t_ref, o_ref):
        o_ref[...] = (g_ref[...].astype(F32) + t_ref[...].astype(F32)).astype(o_ref.dtype)

    gs = pltpu.PrefetchScalarGridSpec(
        num_scalar_prefetch=1, grid=(n, nblk),
        in_specs=[pl.BlockSpec((None, tr, c), lambda j, i, hr: (j, hr[0] * nblk + i, 0)),
                  pl.BlockSpec((None, tr, c), lambda j, i, hr: (j, i, 0))],
        out_specs=pl.BlockSpec((None, tr, c), lambda j, i, hr: (j, i, 0)))
    return pl.pallas_call(body, name=name, grid_spec=gs, out_shape=jax.ShapeDtypeStruct((n, h, c), BF16),
                          compiler_params=_cparams(("parallel", "parallel")))(half, g, t)


def _position():
    x, y, c = lax.axis_index("x"), lax.axis_index("y"), lax.axis_index("c")
    return x, y, c


def _allgather8(xs, name):
    m_per, n = xs.shape

    def body(x_ref, out_ref, send_sems, recv_sems, local_sem):
        x, y, c = _position()
        me, sibling = (x, y, c), (x, y, 1 - c)
        chips = [(1 - x, y), (x, 1 - y), (1 - x, 1 - y)]

        def rows(px, py, pc):
            return out_ref.at[pl.ds((4 * px + 2 * py + pc) * m_per, m_per), :]

        def copy(k, block, to, src=None):
            return pltpu.make_async_remote_copy(
                src_ref=rows(*block) if src is None else src, dst_ref=rows(*block),
                send_sem=send_sems.at[k], recv_sem=recv_sems.at[k], device_id=to, device_id_type=MESH)

        mine = pltpu.make_async_copy(x_ref, rows(*me), local_sem)
        mine.start()
        first = [copy(0, me, sibling, src=x_ref)]
        first += [copy(1 + j, me, (*chip, c), src=x_ref) for j, chip in enumerate(chips)]
        for cp in first:
            cp.start()
        passed = [copy(4 + j, (*chip, c), sibling) for j, chip in enumerate(chips)]
        for j, chip in enumerate(chips):
            copy(1 + j, (*chip, c), me).wait_recv()
            passed[j].start()
        copy(0, sibling, me).wait_recv()
        for j, chip in enumerate(chips):
            copy(4 + j, (*chip, 1 - c), me).wait_recv()
        for cp in first + passed:
            cp.wait_send()
        mine.wait()

    return pl.pallas_call(
        body, name=name, out_shape=jax.ShapeDtypeStruct((N_DEV * m_per, n), xs.dtype),
        in_specs=[pl.BlockSpec(memory_space=pltpu.VMEM)], out_specs=pl.BlockSpec(memory_space=pltpu.VMEM),
        scratch_shapes=[pltpu.SemaphoreType.DMA((7,)), pltpu.SemaphoreType.DMA((7,)), pltpu.SemaphoreType.DMA],
        compiler_params=pltpu.CompilerParams(vmem_limit_bytes=VMEM_LIMIT_BYTES),
    )(xs)


_HBM = pl.BlockSpec(memory_space=pltpu.HBM)


def _gather_weights(ws, name):
    n = len(ws)

    def body(*refs):
        out_refs = refs[n:2 * n]
        send_sems, recv_sems, fsend, frecv = refs[2 * n:]
        x, y, c = _position()
        me, sibling = (x, y, c), (x, y, 1 - c)
        mychip = 2 * x + y
        chips = [(1 - x, y), (x, 1 - y), (1 - x, 1 - y)]
        halves = [w.shape[1] // 2 for w in ws]

        def piece(i, chip_index, half):
            return out_refs[i].at[chip_index, pl.ds(half * halves[i], halves[i]), :]

        sends, forwards = [], []
        for i in range(n):
            for j, (px, py) in enumerate(chips):
                cp = pltpu.make_async_remote_copy(
                    src_ref=piece(i, mychip, c), dst_ref=piece(i, mychip, c),
                    send_sem=send_sems.at[3 * i + j], recv_sem=recv_sems.at[3 * i + j],
                    device_id=(px, py, c), device_id_type=MESH)
                cp.start()
                sends.append(cp)
        for i in range(n):
            for j, (px, py) in enumerate(chips):
                got = piece(i, 2 * px + py, c)
                pltpu.make_async_remote_copy(
                    src_ref=got, dst_ref=got, send_sem=send_sems.at[3 * i + j], recv_sem=recv_sems.at[3 * i + j],
                    device_id=me, device_id_type=MESH).wait_recv()
                fw = pltpu.make_async_remote_copy(
                    src_ref=got, dst_ref=got, send_sem=fsend.at[3 * i + j], recv_sem=frecv.at[3 * i + j],
                    device_id=sibling, device_id_type=MESH)
                fw.start()
                forwards.append(fw)
        for i in range(n):
            for j, (px, py) in enumerate(chips):
                other = piece(i, 2 * px + py, 1 - c)
                pltpu.make_async_remote_copy(
                    src_ref=other, dst_ref=other, send_sem=fsend.at[3 * i + j], recv_sem=frecv.at[3 * i + j],
                    device_id=me, device_id_type=MESH).wait_recv()
        for cp in sends + forwards:
            cp.wait_send()

    return pl.pallas_call(
        body, name=name,
        out_shape=[jax.ShapeDtypeStruct(w.shape, w.dtype) for w in ws],
        in_specs=[_HBM] * n, out_specs=[_HBM] * n, input_output_aliases={i: i for i in range(n)},
        scratch_shapes=[pltpu.SemaphoreType.DMA((3 * n,)), pltpu.SemaphoreType.DMA((3 * n,)),
                        pltpu.SemaphoreType.DMA((3 * n,)), pltpu.SemaphoreType.DMA((3 * n,))],
    )(*ws)


def _swap_halves(gs, name):
    n = len(gs)

    def body(*refs):
        in_refs, out_refs = refs[:n], refs[n:2 * n]
        send_sems, recv_sems = refs[2 * n:]
        x, y, c = _position()
        cps = []
        for i in range(n):
            h = gs[i].shape[1] // 2
            cp = pltpu.make_async_remote_copy(
                src_ref=in_refs[i].at[:, pl.ds((1 - c) * h, h), :], dst_ref=out_refs[i],
                send_sem=send_sems.at[i], recv_sem=recv_sems.at[i], device_id=(x, y, 1 - c), device_id_type=MESH)
            cp.start()
            cps.append(cp)
        for cp in cps:
            cp.wait()

    return pl.pallas_call(
        body, name=name,
        out_shape=[jax.ShapeDtypeStruct((g.shape[0], g.shape[1] // 2, g.shape[2]), g.dtype) for g in gs],
        in_specs=[_HBM] * n, out_specs=[_HBM] * n,
        scratch_shapes=[pltpu.SemaphoreType.DMA((n,)), pltpu.SemaphoreType.DMA((n,))],
    )(*gs)


def _scatter_chips(ps, name):
    n = len(ps)

    def body(*refs):
        in_refs, out_refs = refs[:n], refs[n:2 * n]
        send_sems, recv_sems = refs[2 * n:]
        x, y, c = _position()
        me = (x, y, c)
        chips = [(1 - x, y), (x, 1 - y), (1 - x, 1 - y)]
        cps = []
        for i in range(n):
            for j, (px, py) in enumerate(chips):
                cp = pltpu.make_async_remote_copy(
                    src_ref=in_refs[i].at[2 * px + py], dst_ref=out_refs[i].at[j],
                    send_sem=send_sems.at[3 * i + j], recv_sem=recv_sems.at[3 * i + j],
                    device_id=(px, py, c), device_id_type=MESH)
                cp.start()
                cps.append(cp)
        for i in range(n):
            for j in range(3):
                got = out_refs[i].at[j]
                pltpu.make_async_remote_copy(
                    src_ref=got, dst_ref=got, send_sem=send_sems.at[3 * i + j], recv_sem=recv_sems.at[3 * i + j],
                    device_id=me, device_id_type=MESH).wait_recv()
        for cp in cps:
            cp.wait_send()

    return pl.pallas_call(
        body, name=name,
        out_shape=[jax.ShapeDtypeStruct((3,) + p.shape[1:], p.dtype) for p in ps],
        in_specs=[_HBM] * n, out_specs=[_HBM] * n,
        scratch_shapes=[pltpu.SemaphoreType.DMA((3 * n,)), pltpu.SemaphoreType.DMA((3 * n,))],
    )(*ps)


def _join_halves(rs, name):
    n = len(rs)

    def body(*refs):
        out_refs = refs[n:2 * n]
        send_sems, recv_sems = refs[2 * n:]
        x, y, c = _position()
        cps = []
        for i in range(n):
            h = rs[i].shape[0] // 2
            mine = out_refs[i].at[pl.ds(c * h, h), :]
            cp = pltpu.make_async_remote_copy(
                src_ref=mine, dst_ref=mine, send_sem=send_sems.at[i], recv_sem=recv_sems.at[i],
                device_id=(x, y, 1 - c), device_id_type=MESH)
            cp.start()
            cps.append(cp)
        for i in range(n):
            h = rs[i].shape[0] // 2
            other = out_refs[i].at[pl.ds((1 - c) * h, h), :]
            pltpu.make_async_remote_copy(
                src_ref=other, dst_ref=other, send_sem=send_sems.at[i], recv_sem=recv_sems.at[i],
                device_id=(x, y, c), device_id_type=MESH).wait_recv()
        for cp in cps:
            cp.wait_send()

    return pl.pallas_call(
        body, name=name,
        out_shape=[jax.ShapeDtypeStruct(r.shape, r.dtype) for r in rs],
        in_specs=[_HBM] * n, out_specs=[_HBM] * n, input_output_aliases={i: i for i in range(n)},
        scratch_shapes=[pltpu.SemaphoreType.DMA((n,)), pltpu.SemaphoreType.DMA((n,))],
    )(*rs)


def _t5_buckets_block():
    qi = np.arange(BLOCK)[:, None]
    ki = np.arange(2 * BLOCK)[None, :]
    n = np.maximum(qi + BLOCK - ki, 0)
    max_exact = NUM_BUCKETS // 2
    large = max_exact + (np.log(np.maximum(n, 1) / max_exact) / np.log(MAX_DISTANCE / max_exact)
                         * (NUM_BUCKETS - max_exact)).astype(np.int32)
    large = np.minimum(large, NUM_BUCKETS - 1)
    return np.where(n < max_exact, n, large).astype(np.int32)


def _discretise(lambda_re, lambda_im, log_step, b_re, b_im):
    lam_re = jnp.minimum(lambda_re, -1e-4)
    lam_im = lambda_im
    delta = jnp.exp(log_step)[:, None]
    mag = jnp.exp(lam_re * delta)
    ang = lam_im * delta
    abar_re, abar_im = mag * jnp.cos(ang), mag * jnp.sin(ang)
    num_re, num_im = abar_re - 1.0, abar_im
    den = lam_re * lam_re + lam_im * lam_im
    f_re = (num_re * lam_re + num_im * lam_im) / den
    f_im = (num_im * lam_re - num_re * lam_im) / den
    bbar_re = f_re[..., None] * b_re - f_im[..., None] * b_im
    bbar_im = f_re[..., None] * b_im + f_im[..., None] * b_re
    return abar_re, abar_im, bbar_re, bbar_im


def _interleave(v, nc):
    s, w = v.shape
    return v.reshape(nc, s // nc, w).transpose(1, 0, 2).reshape(s, w)


def _deinterleave(v, nc):
    s, w = v.shape
    return v.reshape(s // nc, nc, w).transpose(1, 0, 2).reshape(s, w)


_SMALL = ("norm1_g", "b_in", "attn_sinks", "rel_bias", "lambda_re", "lambda_im", "log_step", "ssm_b_re",
          "ssm_b_im", "ssm_c_re", "ssm_c_im", "ssm_d", "b_glu", "norm2_g", "final_g")


def _pack(parts):
    rows = []
    for p in parts:
        f = p.reshape(-1).astype(F32)
        pad = (-f.shape[0]) % 128
        rows.append(jnp.pad(f, (0, pad)).reshape(-1, 128))
    out = jnp.concatenate(rows, axis=0)
    pad = (-out.shape[0]) % 256
    return jnp.pad(out, ((0, pad), (0, 0)))


def _unpack(packed, shapes):
    res, r = [], 0
    for shp in shapes:
        size = int(np.prod(shp))
        nr = -(-size // 128)
        res.append(packed[r:r + nr].reshape(-1)[:size].reshape(shp))
        r += nr
    return res


def kernel(x, c, w_ada, b_ada, norm1_g, w_in, b_in, attn_sinks, rel_bias, lambda_re, lambda_im, log_step, ssm_b_re, ssm_b_im, ssm_c_re, ssm_c_im, ssm_d, w_glu, b_glu, w_attn_proj, w_ssm_proj, w_out, norm2_g, w_ff1, w_ff2, final_g, loss_target, m_w_ada, m_b_ada, m_norm1_g, m_w_in, m_b_in, m_attn_sinks, m_rel_bias, m_lambda_re, m_lambda_im, m_log_step, m_ssm_b_re, m_ssm_b_im, m_ssm_c_re, m_ssm_c_im, m_ssm_d, m_w_glu, m_b_glu, m_w_attn_proj, m_w_ssm_proj, m_w_out, m_norm2_g, m_w_ff1, m_w_ff2, m_final_g, v_w_ada, v_b_ada, v_norm1_g, v_w_in, v_b_in, v_attn_sinks, v_rel_bias, v_lambda_re, v_lambda_im, v_log_step, v_ssm_b_re, v_ssm_b_im, v_ssm_c_re, v_ssm_c_im, v_ssm_d, v_w_glu, v_b_glu, v_w_attn_proj, v_w_ssm_proj, v_w_out, v_norm2_g, v_w_ff1, v_w_ff2, v_final_g):
    given = dict(locals())
    S, D = x.shape[1], x.shape[2]
    SSM_W = w_glu.shape[2]
    G = SSM_W // SSM_GROUP_CH
    NST = G * SSM_STATE
    DFF = w_ff2.shape[1] * N_CHIPS
    INW = w_in.shape[2] * N_CHIPS
    o_q, o_k, o_v, o_u = 0, ATTN_WIDTH, ATTN_WIDTH + KV_WIDTH, ATTN_WIDTH + 2 * KV_WIDTH
    o_ga, o_gs = o_u + SSM_W, o_u + SSM_W + D
    mx, my, mc = _position()
    my_chip = 2 * mx + my
    my_b = 4 * mx + 2 * my + mc

    xv, tgt = x[0], loss_target[0]

    big = dict(w_in=w_in[0], w_glu=w_glu[0], w_attn_proj=w_attn_proj[0], w_ssm_proj=w_ssm_proj[0],
               w_out=w_out[0], w_ff1=w_ff1[0], w_ff2=w_ff2[0])
    big_names = list(big)
    colsharded = {"w_in", "w_attn_proj", "w_ssm_proj", "w_ff1"}
    chip_sel = my_chip.astype(jnp.int32).reshape(1)
    slots = [_cast_into_slot(big[k], chip_sel, "cast_" + k) for k in big_names]
    gathered = dict(zip(big_names, _gather_weights(slots, "gather_weights")))

    def wop(k):
        g = gathered[k]
        return _Op(g, N_CHIPS) if k in colsharded else _Op(g.reshape(g.shape[0] * g.shape[1], g.shape[2]))

    c_all = _allgather8(jnp.pad(c, ((0, 7), (0, 0))), "gather_c").reshape(N_DEV, 8, D)[:, 0]
    c16 = jnp.pad(c_all, ((0, 8), (0, 0)))
    b_ada_mine = lax.dynamic_slice(b_ada.reshape(N_CHIPS, -1), (my_chip, 0), (1, w_ada.shape[2]))
    mod_sh = _mm(c16, w_ada[0], "NN", name="mod", M=16, N=w_ada.shape[2], K=D, a_fn=_silu,
                 epilogue=lambda acc, b: (acc + b,), extras=[(b_ada_mine, "row")])
    mod_all = _allgather8(mod_sh[:8], "gather_mod").reshape(N_DEV, 8, -1)
    mod_row = jnp.concatenate(
        [lax.dynamic_slice(mod_all, (2 * j, my_b, 0), (1, 1, mod_all.shape[2]))[0] for j in range(N_CHIPS)], axis=1)
    sh1, sc1, g1, sh2, sc2, g2 = [mod_row[:, i * D:(i + 1) * D] for i in range(6)]

    disc_in = (lambda_re[0], lambda_im[0], log_step[0], ssm_b_re[0], ssm_b_im[0])
    (abar_re, abar_im, bbar_re, bbar_im), disc_vjp = jax.vjp(_discretise, *disc_in)
    eye = jnp.eye(G, dtype=F32)
    bd = jnp.concatenate([jnp.einsum("gnp,gh->gphn", bb, eye).reshape(SSM_W, NST) for bb in (bbar_re, bbar_im)], axis=1)
    cd = jnp.concatenate([jnp.einsum("gpn,gh->gnhp", cc, eye).reshape(NST, SSM_W)
                          for cc in (ssm_c_re[0], -ssm_c_im[0])], axis=0)
    a_fwd = jnp.stack([abar_re.reshape(1, NST), abar_im.reshape(1, NST)])
    a_bwd = jnp.stack([abar_re.reshape(1, NST), -abar_im.reshape(1, NST)])
    d_row = ssm_d

    buckets = _t5_buckets_block()
    onehot = (jnp.asarray(buckets.reshape(-1, 1)) == jnp.arange(128, dtype=jnp.int32)[None, :]).astype(BF16)
    rb_hi = rel_bias.astype(BF16)
    rb_lo = (rel_bias - rb_hi.astype(F32)).astype(BF16)
    rb_lo2 = (rel_bias - rb_hi.astype(F32) - rb_lo.astype(F32)).astype(BF16)
    rb3 = jnp.pad(jnp.concatenate([rb_hi, rb_lo, rb_lo2], axis=1),
                  ((0, 128 - NUM_BUCKETS), (0, 128 - 3 * N_Q_HEADS)))
    b3 = _mm(onehot, rb3, "NN", name="rel_bias_rows", M=BLOCK * 2 * BLOCK, N=128, K=128)
    bias = (b3[:, :N_Q_HEADS] + b3[:, N_Q_HEADS:2 * N_Q_HEADS]) + b3[:, 2 * N_Q_HEADS:3 * N_Q_HEADS]
    bias = jnp.transpose(bias.reshape(BLOCK, 2 * BLOCK, N_Q_HEADS), (2, 0, 1))
    sinks_b = jnp.broadcast_to(attn_sinks[0][:, None, None], (N_Q_HEADS, BLOCK, 128)).reshape(N_Q_HEADS * BLOCK, 128)

    h1 = _rowwise(_norm_mod, [(xv, "tile", D), (norm1_g, "row", D), (sh1, "row", D), (sc1, "row", D)],
                  [(D, BF16)], [], name="norm1", rows=S)[0]
    proj = _mm(h1, wop("w_in"), "NN", name="proj", M=S, N=INW, K=D,
               epilogue=lambda acc, b: (acc + b,), extras=[(b_in, "row")])

    def heads(v2d, nh):
        return v2d.reshape(S, nh, HEAD_DIM).transpose(1, 0, 2)

    def unheads(v3d):
        return v3d.transpose(1, 0, 2).reshape(S, -1)

    qh = heads(proj[:, o_q:o_k], N_Q_HEADS)
    kh = heads(proj[:, o_k:o_v], N_KV_HEADS)
    vh = heads(proj[:, o_v:o_u], N_KV_HEADS)
    attn = unheads(_attn_fwd(qh, kh, vh, sinks_b, bias, "attn_fwd"))
    y_attn = _mm(attn, wop("w_attn_proj"), "NN", name="attn_proj", M=S, N=D, K=ATTN_WIDTH)

    u = proj[:, o_u:o_ga]
    u_il = _interleave(u, SCAN_CHUNKS)
    bu = _mm(u_il, bd, "NN", name="ssm_bu", M=S, N=2 * NST, K=SSM_W, out_nsh=2)
    xs = _scan(a_fwd, bu, None, reverse=False, name="scan_fwd", tc=256)
    y_il = _mm(_Op(xs, 2), cd, "NN", name="ssm_y", M=S, N=SSM_W, K=2 * NST,
               epilogue=lambda acc, uu, dd: (acc + dd * uu,), extras=[(u_il, "tile"), (d_row, "row")])
    y = _deinterleave(y_il, SCAN_CHUNKS)
    z0b = _rowwise(_gelu, [(y, "tile", SSM_W)], [(SSM_W, BF16)], [], name="gelu", rows=S)[0]
    z, t_glu = _mm(z0b, wop("w_glu"), "NN", name="glu", M=S, N=SSM_W, K=SSM_W, out_dtypes=(BF16, F32),
                   epilogue=lambda acc, b, yy: (_gelu(yy) * _sigmoid(acc + b), acc + b),
                   extras=[(b_glu, "row"), (y, "tile")])
    y_ssm = _mm(z, wop("w_ssm_proj"), "NN", name="ssm_proj", M=S, N=D, K=SSM_W)

    merged = _rowwise(_merge, [(_Op(proj, coff=o_ga), "tile", D), (_Op(proj, coff=o_gs), "tile", D),
                               (y_attn, "tile", D), (y_ssm, "tile", D)], [(D, BF16)], [], name="merge", rows=S)[0]
    mo, x2 = _mm(merged, wop("w_out"), "NN", name="out_proj", M=S, N=D, K=D, out_dtypes=(F32, F32),
                 epilogue=lambda acc, xx, gg: (acc, xx + gg * acc), extras=[(xv, "tile"), (g1, "row")])
    h2 = _rowwise(_norm_mod, [(x2, "tile", D), (norm2_g, "row", D), (sh2, "row", D), (sc2, "row", D)],
                  [(D, BF16)], [], name="norm2", rows=S)[0]
    a_b, r_b = _mm(h2, wop("w_ff1"), "NN", name="ff1", M=S, N=DFF, K=D, out_dtypes=(BF16, BF16),
                   epilogue=lambda acc: (acc, jnp.square(jnp.maximum(acc, 0.0))))
    ff, x3 = _mm(r_b, wop("w_ff2"), "NN", name="ff2", M=S, N=D, K=DFF, out_dtypes=(F32, F32),
                 epilogue=lambda acc, xx, gg: (acc, xx + gg * acc), extras=[(x2, "tile"), (g2, "row")],
                 tj=1024, tk=1024)

    def final_fn(x3b, gf, tb):
        def f(xx, gg):
            yv = xx * lax.rsqrt(jnp.mean(xx * xx, axis=-1, keepdims=True) + EPS) * gg
            err = jnp.square(yv - tb)
            return 0.5 * jnp.sum(jnp.mean(err, axis=-1, keepdims=True), axis=0, keepdims=True)
        lv, vjp = jax.vjp(f, x3b, gf)
        dx, dg = vjp(jnp.ones((1, 1), F32))
        return dx, dg, jnp.broadcast_to(lv, (1, 128))

    dx3, g_final, loss_acc = _rowwise(final_fn, [(x3, "tile", D), (final_g.reshape(1, D), "row", D), (tgt, "tile", D)],
                                      [(D, F32)], [D, 128], name="final", rows=S)

    def ff_out_bwd(dx3b, ffb, g2b):
        return dx3b * g2b, jnp.sum(dx3b * ffb, axis=0, keepdims=True)

    dff, d_g2 = _rowwise(ff_out_bwd, [(dx3, "tile", D), (ff, "tile", D), (g2, "row", D)], [(D, BF16)], [D],
                         name="ff_out_bwd", rows=S)
    da = _mm(dff, wop("w_ff2"), "NT", name="ff2_dx", M=S, N=DFF, K=D, out_dtypes=(BF16,),
             epilogue=lambda acc, ab: (acc * (2.0 * jnp.maximum(ab.astype(F32), 0.0)),), extras=[(a_b, "tile")])
    g_w_ff2 = _mm(r_b, dff, "TN", name="ff2_dw", M=DFF, N=D, K=S, out_dtypes=(BF16,), tj=1024, tk=1024)
    dh2 = _mm(da, wop("w_ff1"), "NT", name="ff1_dx", M=S, N=D, K=DFF, tj=1024, tk=1024)
    g_w_ff1 = _mm(h2, da, "TN", name="ff1_dw", M=D, N=DFF, K=S, out_dtypes=(BF16,), out_nsh=N_CHIPS, tj=1024, tk=1024)

    def norm2_bwd(x2b, dh2b, dx3b, mob, gn, shb, scb, g1b):
        _, vjp = jax.vjp(_norm_mod, x2b, gn, shb, scb)
        dx, dg, dsh, dsc = vjp(dh2b)
        dx2b = dx + dx3b
        return dx2b, dx2b * g1b, dg, dsh, dsc, jnp.sum(dx2b * mob, axis=0, keepdims=True)

    dx2, dmo, g_norm2, d_sh2, d_sc2, d_g1 = _rowwise(
        norm2_bwd, [(x2, "tile", D), (dh2, "tile", D), (dx3, "tile", D), (mo, "tile", D), (norm2_g, "row", D),
                    (sh2, "row", D), (sc2, "row", D), (g1, "row", D)],
        [(D, F32), (D, BF16)], [D, D, D, D], name="norm2_bwd", rows=S, tr=128)
    dmerged = _mm(dmo, wop("w_out"), "NT", name="out_dx", M=S, N=D, K=D)
    g_w_out = _mm(merged, dmo, "TN", name="out_dw", M=D, N=D, K=S, out_dtypes=(BF16,), tk=1024)

    def merge_bwd(gab, gsb, yab, ysb, dmb):
        _, vjp = jax.vjp(_merge, gab, gsb, yab, ysb)
        return vjp(dmb)

    d_ga, d_gs, dy_attn, dy_ssm = _rowwise(
        merge_bwd, [(_Op(proj, coff=o_ga), "tile", D), (_Op(proj, coff=o_gs), "tile", D), (y_attn, "tile", D),
                    (y_ssm, "tile", D), (dmerged, "tile", D)],
        [(D, BF16), (D, BF16), (D, BF16), (D, BF16)], [], name="merge_bwd", rows=S, tr=128)

    dattn = _mm(dy_attn, wop("w_attn_proj"), "NT", name="attn_proj_dx", M=S, N=ATTN_WIDTH, K=D)
    g_w_attn_proj = _mm(attn, dy_attn, "TN", name="attn_proj_dw", M=ATTN_WIDTH, N=D, K=S, out_dtypes=(BF16,),
                        out_nsh=N_CHIPS, tk=1024)
    dqh, dkh, dvh, dsink_blk, dbias = _attn_bwd(qh, kh, vh, heads(dattn, N_Q_HEADS), sinks_b, bias, "attn_bwd")
    g_sinks = _sum_lead(dsink_blk.reshape(N_Q_HEADS, BLOCK, 128).transpose(1, 0, 2), "sinks_dw")[:, 0].reshape(1, N_Q_HEADS)
    g_rel = _mm(dbias.reshape(N_Q_HEADS, -1), onehot, "NN", name="rel_bias_dw", M=N_Q_HEADS, N=128,
                K=BLOCK * 2 * BLOCK, tk=4096)
    g_rel_bias = g_rel[:, :NUM_BUCKETS].T

    dz = _mm(dy_ssm, wop("w_ssm_proj"), "NT", name="ssm_proj_dx", M=S, N=SSM_W, K=D)
    g_w_ssm_proj = _mm(z, dy_ssm, "TN", name="ssm_proj_dw", M=SSM_W, N=D, K=S, out_dtypes=(BF16,),
                       out_nsh=N_CHIPS, tk=1024)

    def glu_bwd(dzb, yb, tb):
        z0 = _gelu(yb)
        sg = _sigmoid(tb)
        dt = dzb * z0 * sg * (1.0 - sg)
        return dt, dzb * sg, jnp.sum(dt, axis=0, keepdims=True)

    dt_b, dz0a, g_b_glu = _rowwise(glu_bwd, [(dz, "tile", SSM_W), (y, "tile", SSM_W), (t_glu, "tile", SSM_W)],
                                   [(SSM_W, BF16), (SSM_W, F32)], [SSM_W], name="glu_bwd", rows=S)

    def gelu_bwd(acc, dz0ab, yb):
        _, vjp = jax.vjp(_gelu, yb)
        return (vjp(acc + dz0ab)[0],)

    dy = _mm(dt_b, wop("w_glu"), "NT", name="glu_dx", M=S, N=SSM_W, K=SSM_W, epilogue=gelu_bwd,
             extras=[(dz0a, "tile"), (y, "tile")])
    g_w_glu = _mm(z0b, dt_b, "TN", name="glu_dw", M=SSM_W, N=SSM_W, K=S, out_dtypes=(BF16,), tk=1024)
    dy_il = _interleave(dy, SCAN_CHUNKS)
    dxs = _mm(dy_il, cd, "NT", name="ssm_dx", M=S, N=2 * NST, K=SSM_W, out_nsh=2)
    g_cd = _mm(_Op(xs, 2), dy_il, "TN", name="ssm_dc", M=2 * NST, N=SSM_W, K=S, tk=1024)
    lam, d_abar = _scan(a_bwd, dxs, xs, reverse=True, name="scan_bwd", tc=128)

    def du_fn(acc, dyb, dd):
        return (acc + dd * dyb,)

    du_il = _mm(_Op(lam, 2), bd, "NT", name="ssm_du", M=S, N=SSM_W, K=2 * NST, epilogue=du_fn,
                extras=[(dy_il, "tile"), (d_row, "row")])
    g_bd = _mm(u_il, _Op(lam, 2), "TN", name="ssm_db", M=SSM_W, N=2 * NST, K=S, tk=1024)
    g_ssm_d = _rowwise(lambda dyb, ub: (jnp.sum(dyb * ub, axis=0, keepdims=True),),
                       [(dy_il, "tile", SSM_W), (u_il, "tile", SSM_W)], [], [SSM_W], name="ssm_dd", rows=S)[0]
    du = _deinterleave(du_il, SCAN_CHUNKS)

    g_cd4 = g_cd.reshape(2, G, SSM_STATE, G, SSM_GROUP_CH)
    g_c_re = jnp.einsum("gnhp,gh->gpn", g_cd4[0], eye)
    g_c_im = -jnp.einsum("gnhp,gh->gpn", g_cd4[1], eye)
    g_bd4 = g_bd.reshape(G, SSM_GROUP_CH, 2, G, SSM_STATE)
    g_bbar_re = jnp.einsum("gphn,gh->gnp", g_bd4[:, :, 0], eye)
    g_bbar_im = jnp.einsum("gphn,gh->gnp", g_bd4[:, :, 1], eye)
    g_lre, g_lim, g_lstep, g_bre, g_bim = disc_vjp(
        (d_abar[0].reshape(G, SSM_STATE), d_abar[1].reshape(G, SSM_STATE), g_bbar_re, g_bbar_im))

    dproj = jnp.concatenate([unheads(dqh).astype(BF16), unheads(dkh).astype(BF16), unheads(dvh).astype(BF16),
                             du.astype(BF16), d_ga, d_gs], axis=1)
    dh1 = _mm(dproj, wop("w_in"), "NT", name="proj_dx", M=S, N=D, K=INW, tj=1024, tk=INW // N_CHIPS)
    g_w_in = _mm(h1, dproj, "TN", name="proj_dw", M=D, N=INW, K=S, out_dtypes=(BF16,), out_nsh=N_CHIPS,
                 tj=INW // (2 * N_CHIPS), tk=1024)
    g_b_in = _rowwise(lambda d: (jnp.sum(d.astype(F32), axis=0, keepdims=True),), [(dproj, "tile", INW)], [], [INW],
                      name="proj_db", rows=S)[0]

    def norm1_bwd(xb, dhb, dresb, gn, shb, scb):
        _, vjp = jax.vjp(_norm_mod, xb, gn, shb, scb)
        dx, dg, dsh, dsc = vjp(dhb)
        return dx + dresb, dg, dsh, dsc

    grad_x, g_norm1, d_sh1, d_sc1 = _rowwise(
        norm1_bwd, [(xv, "tile", D), (dh1, "tile", D), (dx2, "tile", D), (norm1_g, "row", D), (sh1, "row", D),
                    (sc1, "row", D)], [(D, F32)], [D, D, D], name="norm1_bwd", rows=S)

    dmod_row = jnp.concatenate([d_sh1, d_sc1, d_g1, d_sh2, d_sc2, d_g2], axis=1)
    dmod_all = _allgather8(jnp.pad(dmod_row, ((0, 7), (0, 0))), "gather_dmod").reshape(N_DEV, 8, -1)[:, 0]
    g_b_ada = _sum_lead(dmod_all.reshape(N_DEV, -1, 128), "b_ada_dw").reshape(1, -1)
    dmod_mine = lax.dynamic_slice(dmod_all.reshape(N_DEV, N_CHIPS, -1), (0, my_chip, 0), (N_DEV, 1, w_ada.shape[2]))[:, 0]
    g_w_ada = _mm(c16, jnp.pad(dmod_mine, ((0, 8), (0, 0))), "TN", name="ada_dw", M=D, N=w_ada.shape[2], K=16,
                  a_fn=_silu)

    small_g = dict(norm1_g=g_norm1, b_in=g_b_in, attn_sinks=g_sinks, rel_bias=g_rel_bias, lambda_re=g_lre[None],
                   lambda_im=g_lim[None], log_step=g_lstep[None], ssm_b_re=g_bre[None], ssm_b_im=g_bim[None],
                   ssm_c_re=g_c_re[None], ssm_c_im=g_c_im[None], ssm_d=g_ssm_d, b_glu=g_b_glu, norm2_g=g_norm2,
                   final_g=g_final.reshape(D))
    packed = _pack([loss_acc[:, :1]] + [small_g[k] for k in _SMALL])
    rows = packed.shape[0]
    summed = _sum_lead(_allgather8(packed, "gather_small").reshape(N_DEV, rows, 128), "small_sum")
    small_shapes = [(1,)] + [given[k].shape for k in _SMALL]
    parts = _unpack(summed, small_shapes)
    loss = parts[0].reshape(())
    grads = dict(zip(_SMALL, parts[1:]))
    grads["b_ada"] = g_b_ada
    grads["w_ada"] = g_w_ada[None]

    big_g = dict(w_in=g_w_in, w_glu=g_w_glu, w_attn_proj=g_w_attn_proj, w_ssm_proj=g_w_ssm_proj, w_out=g_w_out,
                 w_ff1=g_w_ff1, w_ff2=g_w_ff2)
    g_list = []
    for k in big_names:
        gk = big_g[k]
        if k not in colsharded:
            gk = gk.reshape(N_CHIPS, gk.shape[0] // N_CHIPS, gk.shape[1])
        g_list.append(gk)
    half = mc.astype(jnp.int32).reshape(1)
    t1 = _swap_halves(g_list, "rs_swap")
    p_list = [_add_half(g, t, half, "rs_add_" + k) for g, t, k in zip(g_list, t1, big_names)]
    t2 = _scatter_chips(p_list, "rs_scatter")
    sel = jnp.stack([my_chip, mc]).astype(jnp.int32)
    r_list = [_sum_own(p, t, sel, "rs_sum_" + k) for p, t, k in zip(p_list, t2, big_names)]
    full = _join_halves(r_list, "rs_join")
    for k, f in zip(big_names, full):
        grads[k] = f[None]

    deltas, new_m, new_v = {}, {}, {}
    for k in big_names + ["w_ada"]:
        d_, m_, v_ = _adamw(given[k][0], grads[k][0], given["m_" + k][0], given["v_" + k][0], "adamw_" + k)
        deltas[k], new_m[k], new_v[k] = d_[None], m_[None], v_[None]
    small_all = list(_SMALL) + ["b_ada"]
    shapes = [given[k].shape for k in small_all]
    pw, pg = _pack([given[k] for k in small_all]), _pack([grads[k] for k in small_all])
    pm, pv = _pack([given["m_" + k] for k in small_all]), _pack([given["v_" + k] for k in small_all])
    d_, m_, v_ = _adamw(pw, pg, pm, pv, "adamw_small")
    for k, dd, mm, vv in zip(small_all, _unpack(d_, shapes), _unpack(m_, shapes), _unpack(v_, shapes)):
        deltas[k], new_m[k], new_v[k] = dd, mm, vv
        grads[k] = grads[k].reshape(given[k].shape)

    names = ["w_ada", "b_ada", "norm1_g", "w_in", "b_in", "attn_sinks", "rel_bias", "lambda_re", "lambda_im",
             "log_step", "ssm_b_re", "ssm_b_im", "ssm_c_re", "ssm_c_im", "ssm_d", "w_glu", "b_glu", "w_attn_proj",
             "w_ssm_proj", "w_out", "norm2_g", "w_ff1", "w_ff2", "final_g"]
    return (loss, grad_x[None], *[grads[n] for n in names], *[deltas[n] for n in names],
            *[new_m[n] for n in names], *[new_v[n] for n in names])
```

```python
import math

import numpy as np
import jax
import jax.numpy as jnp
from jax import lax
from jax.experimental import pallas as pl
from jax.experimental.pallas import tpu as pltpu

F32 = jnp.float32
BF16 = jnp.bfloat16
MESH = pl.DeviceIdType.MESH

HEAD_DIM = 64
N_Q_HEADS = 16
N_KV_HEADS = 4
GQA_GROUP = N_Q_HEADS // N_KV_HEADS
ATTN_WIDTH = N_Q_HEADS * HEAD_DIM
KV_WIDTH = N_KV_HEADS * HEAD_DIM
BLOCK = 128
NUM_BUCKETS = 32
MAX_DISTANCE = 128
NEG_INF = -1e30
SSM_GROUP_CH = 16
SSM_STATE = 64
EPS = 1e-6
ADAM_LR = 0.001
ADAM_B1 = 0.9
ADAM_B2 = 0.999
ADAM_EPS = 1e-08
ADAM_WD = 0.01
ADAM_STEP = 10

N_CHIPS = 4
N_DEV = 8
SCAN_CHUNKS = 8
VMEM_LIMIT_BYTES = 48 * 1024 * 1024


def _cparams(sem=None):
    return pltpu.CompilerParams(dimension_semantics=sem, vmem_limit_bytes=VMEM_LIMIT_BYTES)


class _Op:
    def __init__(self, arr, nsh=None, coff=0):
        self.arr, self.nsh, self.coff = arr, nsh, coff
        if nsh is None:
            self.rows, self.cols = arr.shape
        else:
            assert arr.shape[0] == nsh
            self.rows, self.cols = arr.shape[1], arr.shape[2] * nsh

    def spec(self, br, bc, idx):
        assert self.coff % bc == 0
        off = self.coff // bc
        if self.nsh is None:
            return pl.BlockSpec((br, bc), lambda *g: (idx(*g)[0], idx(*g)[1] + off))
        per = (self.cols // self.nsh) // bc
        assert per * bc * self.nsh == self.cols

        def imap(*g):
            r, c = idx(*g)
            c = c + off
            return (c // per, r, c % per)
        return pl.BlockSpec((None, br, bc), imap)


def _as_op(a):
    return a if isinstance(a, _Op) else _Op(a)


def _mm(a, b, mode, *, name, M, N, K, out_dtypes=(F32,), out_nsh=None, epilogue=None, extras=(),
        a_fn=None, ti=1024, tj=512, tk=2048):
    a, b = _as_op(a), _as_op(b)
    ti, tj, tk = min(ti, M), min(tj, N), min(tk, K)
    a_w = a.cols // a.nsh if a.nsh else None
    b_w = b.cols // b.nsh if b.nsh else None
    if a_w:
        ti, tk = (min(ti, a_w), tk) if mode == "TN" else (ti, min(tk, a_w))
    if b_w:
        tj, tk = (tj, min(tk, b_w)) if mode == "NT" else (min(tj, b_w), tk)
    if out_nsh:
        tj = min(tj, N // out_nsh)
    assert M % ti == 0 and N % tj == 0 and K % tk == 0, (name, M, N, K, ti, tj, tk)
    nk = K // tk
    if mode == "NN":
        a_spec = a.spec(ti, tk, lambda i, j, k: (i, k))
        b_spec = b.spec(tk, tj, lambda i, j, k: (k, j))
        dims = (((1,), (0,)), ((), ()))
    elif mode == "NT":
        a_spec = a.spec(ti, tk, lambda i, j, k: (i, k))
        b_spec = b.spec(tj, tk, lambda i, j, k: (j, k))
        dims = (((1,), (1,)), ((), ()))
    else:
        a_spec = a.spec(tk, ti, lambda i, j, k: (k, i))
        b_spec = b.spec(tk, tj, lambda i, j, k: (k, j))
        dims = (((0,), (0,)), ((), ()))
    ex_specs, ex_arrs = [], []
    for op, kind in extras:
        op = _as_op(op)
        if kind == "tile":
            ex_specs.append(op.spec(ti, tj, lambda i, j, k: (i, j)))
        else:
            ex_specs.append(op.spec(1, tj, lambda i, j, k: (0, j)))
        ex_arrs.append(op.arr)
    ne, no = len(ex_arrs), len(out_dtypes)
    if out_nsh is None:
        out_shapes = [jax.ShapeDtypeStruct((M, N), d) for d in out_dtypes]
        out_specs = [pl.BlockSpec((ti, tj), lambda i, j, k: (i, j)) for _ in out_dtypes]
    else:
        per = (N // out_nsh) // tj
        assert per * tj * out_nsh == N
        out_shapes = [jax.ShapeDtypeStruct((out_nsh, M, N // out_nsh), d) for d in out_dtypes]
        out_specs = [pl.BlockSpec((None, ti, tj), lambda i, j, k: (j // per, i, j % per)) for _ in out_dtypes]

    def body(a_ref, b_ref, *rest):
        ex_refs, out_refs, acc = rest[:ne], rest[ne:ne + no], rest[ne + no]
        k = pl.program_id(2)

        @pl.when(k == 0)
        def _():
            acc[...] = jnp.zeros_like(acc)

        av = a_ref[...]
        if a_fn is not None:
            av = a_fn(av)
        acc[...] += lax.dot_general(av.astype(BF16), b_ref[...].astype(BF16), dims,
                                    preferred_element_type=F32)

        @pl.when(k == nk - 1)
        def _():
            res = acc[...]
            outs = epilogue(res, *[r[...] for r in ex_refs]) if epilogue is not None else (res,)
            for o_ref, o in zip(out_refs, outs):
                o_ref[...] = o.astype(o_ref.dtype)

    outs = pl.pallas_call(
        body, name=name, grid=(M // ti, N // tj, nk),
        in_specs=[a_spec, b_spec] + ex_specs, out_specs=out_specs, out_shape=out_shapes,
        scratch_shapes=[pltpu.VMEM((ti, tj), F32)],
        compiler_params=_cparams(("parallel", "parallel", "arbitrary")),
    )(a.arr, b.arr, *ex_arrs)
    return outs[0] if no == 1 else outs


def _rowwise(fn, ins, outs, accs, *, name, rows, tr=256):
    tr = min(tr, rows)
    assert rows % tr == 0
    in_specs, arrs = [], []
    for op, kind, width in ins:
        op = _as_op(op)
        if kind == "tile":
            in_specs.append(op.spec(tr, width, lambda i: (i, 0)))
        else:
            in_specs.append(op.spec(op.rows, width, lambda i: (0, 0)))
        arrs.append(op.arr)
    ni, no, na = len(ins), len(outs), len(accs)
    out_shapes = [jax.ShapeDtypeStruct((rows, w), d) for w, d in outs]
    out_specs = [pl.BlockSpec((tr, w), lambda i: (i, 0)) for w, _ in outs]
    out_shapes += [jax.ShapeDtypeStruct((1, w), F32) for w in accs]
    out_specs += [pl.BlockSpec((1, w), lambda i: (0, 0)) for w in accs]

    def body(*refs):
        in_refs, out_refs, acc_refs = refs[:ni], refs[ni:ni + no], refs[ni + no:]
        res = fn(*[r[...] for r in in_refs])
        if not isinstance(res, (tuple, list)):
            res = (res,)
        for o_ref, r in zip(out_refs, res[:no]):
            o_ref[...] = r.astype(o_ref.dtype)
        if na:
            @pl.when(pl.program_id(0) == 0)
            def _():
                for a_ref in acc_refs:
                    a_ref[...] = jnp.zeros_like(a_ref)
            for a_ref, r in zip(acc_refs, res[no:]):
                a_ref[...] += r.astype(F32)

    res = pl.pallas_call(
        body, name=name, grid=(rows // tr,), in_specs=in_specs, out_specs=out_specs, out_shape=out_shapes,
        compiler_params=_cparams(("arbitrary",)),
    )(*arrs)
    return res


def _norm_mod(x, g, sh, sc):
    y = x * lax.rsqrt(jnp.mean(x * x, axis=-1, keepdims=True) + EPS) * g
    return y * (1.0 + sc) + sh


def _sigmoid(x):
    return 1.0 / (1.0 + jnp.exp(-x))


def _silu(x):
    return x * _sigmoid(x)


def _gelu(x):
    return 0.5 * x * (1.0 + jnp.tanh(math.sqrt(2.0 / math.pi) * (x + 0.044715 * (x * x * x))))


def _merge(ga, gs, ya, ys):
    return _sigmoid(ga) * ya + _sigmoid(gs) * ys


def _attn_head(q, kp, kc, vp, vc, sink, bias_p, bias_c, not_first):
    nt = (((1,), (1,)), ((), ()))
    nn = (((1,), (0,)), ((), ()))
    qb = q.astype(BF16)
    scale = HEAD_DIM ** -0.5
    sp = lax.dot_general(qb, kp.astype(BF16), nt, preferred_element_type=F32) * scale + bias_p
    sc = lax.dot_general(qb, kc.astype(BF16), nt, preferred_element_type=F32) * scale + bias_c
    qi = lax.broadcasted_iota(jnp.int32, sp.shape, 0) & (BLOCK - 1)
    ki = lax.broadcasted_iota(jnp.int32, sp.shape, 1)
    sp = jnp.where(jnp.logical_and(ki > qi, not_first), sp, NEG_INF)
    sc = jnp.where(ki <= qi, sc, NEG_INF)
    m = jnp.maximum(jnp.maximum(jnp.max(sp, axis=-1, keepdims=True), jnp.max(sc, axis=-1, keepdims=True)), sink)
    m = lax.stop_gradient(m)
    pp = jnp.exp(sp - m)
    pc = jnp.exp(sc - m)
    denom = jnp.sum(pp, axis=-1, keepdims=True) + jnp.sum(pc, axis=-1, keepdims=True) + jnp.exp(sink - m)
    o = lax.dot_general((pp / denom).astype(BF16), vp.astype(BF16), nn, preferred_element_type=F32)
    o = o + lax.dot_general((pc / denom).astype(BF16), vc.astype(BF16), nn, preferred_element_type=F32)
    return o


def _attn_fwd(qh, kh, vh, sinks, bias, name):
    s = qh.shape[1]
    nb = s // BLOCK
    G = GQA_GROUP
    R = G * BLOCK

    def body(q_ref, kp_ref, kc_ref, vp_ref, vc_ref, sink_ref, bias_ref, o_ref):
        not_first = pl.program_id(0) > 0
        for kv in range(N_KV_HEADS):
            hs = slice(kv * G, (kv + 1) * G)
            o = _attn_head(q_ref[hs].reshape(R, HEAD_DIM), kp_ref[kv], kc_ref[kv], vp_ref[kv], vc_ref[kv],
                           sink_ref[kv * R:(kv + 1) * R, 0:1],
                           bias_ref[hs, :, 0:BLOCK].reshape(R, BLOCK), bias_ref[hs, :, BLOCK:2 * BLOCK].reshape(R, BLOCK),
                           not_first)
            o_ref[hs] = o.reshape(G, BLOCK, HEAD_DIM).astype(o_ref.dtype)

    cur = lambda i: (0, i, 0)
    prev = lambda i: (0, jnp.maximum(i - 1, 0), 0)
    return pl.pallas_call(
        body, name=name, grid=(nb,),
        in_specs=[pl.BlockSpec((N_Q_HEADS, BLOCK, HEAD_DIM), cur),
                  pl.BlockSpec((N_KV_HEADS, BLOCK, HEAD_DIM), prev), pl.BlockSpec((N_KV_HEADS, BLOCK, HEAD_DIM), cur),
                  pl.BlockSpec((N_KV_HEADS, BLOCK, HEAD_DIM), prev), pl.BlockSpec((N_KV_HEADS, BLOCK, HEAD_DIM), cur),
                  pl.BlockSpec((N_Q_HEADS * BLOCK, 128), lambda i: (0, 0)),
                  pl.BlockSpec((N_Q_HEADS, BLOCK, 2 * BLOCK), lambda i: (0, 0, 0))],
        out_specs=pl.BlockSpec((N_Q_HEADS, BLOCK, HEAD_DIM), cur),
        out_shape=jax.ShapeDtypeStruct((N_Q_HEADS, s, HEAD_DIM), BF16),
        compiler_params=_cparams(("arbitrary",)),
    )(qh, kh, kh, vh, vh, sinks, bias)


def _attn_bwd(qh, kh, vh, doh, sinks, bias, name):
    s = qh.shape[1]
    nb = s // BLOCK
    G = GQA_GROUP
    R = G * BLOCK

    def body(q_ref, kp_ref, kc_ref, vp_ref, vc_ref, do_ref, sink_ref, bias_ref,
             dq_ref, dk_ref, dv_ref, dsink_ref, dbias_ref, ck, cv):
        i = pl.program_id(1)

        @pl.when(i == 0)
        def _():
            dsink_ref[...] = jnp.zeros_like(dsink_ref)
            dbias_ref[...] = jnp.zeros_like(dbias_ref)
            ck[...] = jnp.zeros_like(ck)
            cv[...] = jnp.zeros_like(cv)

        @pl.when(i < nb)
        def _():
            not_first = i > 0
            _, vjp = jax.vjp(lambda q, a, b, c, d, sk, e, f: _attn_head(q, a, b, c, d, sk, e, f, not_first),
                             q_ref[...].reshape(R, HEAD_DIM), kp_ref[...], kc_ref[...], vp_ref[...], vc_ref[...],
                             sink_ref[:, 0:1], bias_ref[:, :, 0:BLOCK].reshape(R, BLOCK),
                             bias_ref[:, :, BLOCK:2 * BLOCK].reshape(R, BLOCK))
            dq, dkp, dkc, dvp, dvc, dsk, dbp, dbc = vjp(do_ref[...].reshape(R, HEAD_DIM).astype(F32))
            dq_ref[...] = dq.reshape(G, BLOCK, HEAD_DIM)
            dsink_ref[...] += jnp.broadcast_to(dsk, (R, 128))
            dbias_ref[:, :, 0:BLOCK] += dbp.reshape(G, BLOCK, BLOCK)
            dbias_ref[:, :, BLOCK:2 * BLOCK] += dbc.reshape(G, BLOCK, BLOCK)
            dk_ref[...] = ck[...] + dkp
            dv_ref[...] = cv[...] + dvp
            ck[...] = dkc
            cv[...] = dvc

        @pl.when(i == nb)
        def _():
            dk_ref[...] = ck[...]
            dv_ref[...] = cv[...]

    qcur = lambda kv, i: (kv, jnp.minimum(i, nb - 1), 0)
    kcur = lambda kv, i: (kv, jnp.minimum(i, nb - 1), 0)
    kprev = lambda kv, i: (kv, jnp.clip(i - 1, 0, nb - 1), 0)
    qspec = pl.BlockSpec((G, BLOCK, HEAD_DIM), qcur)
    kc_spec = pl.BlockSpec((None, BLOCK, HEAD_DIM), kcur)
    kp_spec = pl.BlockSpec((None, BLOCK, HEAD_DIM), kprev)
    return pl.pallas_call(
        body, name=name, grid=(N_KV_HEADS, nb + 1),
        in_specs=[qspec, kp_spec, kc_spec, kp_spec, kc_spec, qspec,
                  pl.BlockSpec((R, 128), lambda kv, i: (kv, 0)),
                  pl.BlockSpec((G, BLOCK, 2 * BLOCK), lambda kv, i: (kv, 0, 0))],
        out_specs=[qspec, kp_spec, kp_spec,
                   pl.BlockSpec((R, 128), lambda kv, i: (kv, 0)),
                   pl.BlockSpec((G, BLOCK, 2 * BLOCK), lambda kv, i: (kv, 0, 0))],
        out_shape=[jax.ShapeDtypeStruct((N_Q_HEADS, s, HEAD_DIM), F32),
                   jax.ShapeDtypeStruct((N_KV_HEADS, s, HEAD_DIM), F32),
                   jax.ShapeDtypeStruct((N_KV_HEADS, s, HEAD_DIM), F32),
                   jax.ShapeDtypeStruct((N_Q_HEADS * BLOCK, 128), F32),
                   jax.ShapeDtypeStruct((N_Q_HEADS, BLOCK, 2 * BLOCK), F32)],
        scratch_shapes=[pltpu.VMEM((BLOCK, HEAD_DIM), F32), pltpu.VMEM((BLOCK, HEAD_DIM), F32)],
        compiler_params=_cparams(("arbitrary", "arbitrary")),
    )(qh, kh, kh, vh, vh, doh, sinks, bias)


def _cmul(ar, ai, br, bi):
    return ar * br - ai * bi, ar * bi + ai * br


def _scan(a, b, xs_prev, *, reverse, name, tc):
    _, s, c = b.shape
    nc = SCAN_CHUNKS
    steps = s // nc
    with_da = xs_prev is not None
    unroll = 8 if steps % 8 == 0 else 1

    def shift(v, d):
        row = lax.broadcasted_iota(jnp.int32, v.shape, 0)
        if reverse:
            return jnp.where(row < nc - d, pltpu.roll(v, nc - d, 0), 0.0)
        return jnp.where(row >= d, pltpu.roll(v, d, 0), 0.0)

    def body(*refs):
        if with_da:
            a_ref, b_ref, xp_ref, x_ref, da_ref = refs
        else:
            a_ref, b_ref, x_ref = refs
        ar = jnp.broadcast_to(a_ref[0], (nc, tc))
        ai = jnp.broadcast_to(a_ref[1], (nc, tc))

        def row_of(step):
            j = (steps - 1 - step) if reverse else step
            return pl.multiple_of(j * nc, nc)

        def p1(step, st):
            sr, si = st
            r0 = row_of(step)
            mr, mi = _cmul(ar, ai, sr, si)
            sr = mr + b_ref[0, pl.ds(r0, nc), :]
            si = mi + b_ref[1, pl.ds(r0, nc), :]
            x_ref[0, pl.ds(r0, nc), :] = sr
            x_ref[1, pl.ds(r0, nc), :] = si
            return sr, si
        zero = jnp.zeros((nc, tc), F32)
        er, ei = lax.fori_loop(0, steps, p1, (zero, zero), unroll=unroll)

        def pw(step, st):
            return _cmul(ar, ai, *st)
        pr, pi_ = lax.fori_loop(0, steps, pw, (jnp.ones((nc, tc), F32), zero), unroll=unroll)
        cr, ci = shift(er, 1), shift(ei, 1)
        d = 1
        while d < nc:
            mr, mi = _cmul(pr, pi_, shift(cr, d), shift(ci, d))
            cr, ci = cr + mr, ci + mi
            pr, pi_ = _cmul(pr, pi_, pr, pi_)
            d *= 2

        def p2(step, st):
            qr, qi, dar, dai = st
            r0 = row_of(step)
            qr, qi = _cmul(ar, ai, qr, qi)
            fr, fi = _cmul(qr, qi, cr, ci)
            xr = x_ref[0, pl.ds(r0, nc), :] + fr
            xi = x_ref[1, pl.ds(r0, nc), :] + fi
            x_ref[0, pl.ds(r0, nc), :] = xr
            x_ref[1, pl.ds(r0, nc), :] = xi
            if with_da:
                jm = jnp.where(step == steps - 1, steps - 1, steps - 2 - step)
                rp = pl.multiple_of(jm * nc, nc)
                vr, vi = xp_ref[0, pl.ds(rp, nc), :], xp_ref[1, pl.ds(rp, nc), :]
                row = lax.broadcasted_iota(jnp.int32, (nc, tc), 0)
                first = step == steps - 1
                sel = jnp.logical_and(first, row == 0)
                vr = jnp.where(sel, 0.0, jnp.where(first, pltpu.roll(vr, 1, 0), vr))
                vi = jnp.where(sel, 0.0, jnp.where(first, pltpu.roll(vi, 1, 0), vi))
                dar = dar + xr * vr + xi * vi
                dai = dai + xi * vr - xr * vi
            return qr, qi, dar, dai
        _, _, dar, dai = lax.fori_loop(0, steps, p2, (jnp.ones((nc, tc), F32), zero, zero, zero), unroll=unroll)
        if with_da:
            da_ref[0] = jnp.sum(dar, axis=0, keepdims=True)
            da_ref[1] = jnp.sum(dai, axis=0, keepdims=True)

    blk = pl.BlockSpec((2, s, tc), lambda i: (0, 0, i))
    vec = pl.BlockSpec((2, 1, tc), lambda i: (0, 0, i))
    in_specs, args = [vec, blk], [a, b]
    out_specs, out_shape = [blk], [jax.ShapeDtypeStruct((2, s, c), F32)]
    if with_da:
        in_specs.append(blk)
        args.append(xs_prev)
        out_specs.append(vec)
        out_shape.append(jax.ShapeDtypeStruct((2, 1, c), F32))
    res = pl.pallas_call(
        body, name=name, grid=(c // tc,), in_specs=in_specs, out_specs=out_specs, out_shape=out_shape,
        compiler_params=_cparams(("arbitrary",)),
    )(*args)
    return res if with_da else res[0]


def _adamw(w, g, m, v, name):
    r, c = w.shape
    tr = r
    for cand in (512, 256, 128, 64, 32, 16, 8):
        if r % cand == 0 and cand * c * 4 <= 2 * 1024 * 1024:
            tr = cand
            break

    def body(w_ref, g_ref, m_ref, v_ref, d_ref, nm_ref, nv_ref):
        gv = g_ref[...]
        nm = ADAM_B1 * m_ref[...] + (1.0 - ADAM_B1) * gv
        nv = ADAM_B2 * v_ref[...] + (1.0 - ADAM_B2) * (gv * gv)
        m_hat = nm / (1.0 - ADAM_B1 ** ADAM_STEP)
        v_hat = nv / (1.0 - ADAM_B2 ** ADAM_STEP)
        d_ref[...] = -ADAM_LR * (m_hat / (jnp.sqrt(v_hat) + ADAM_EPS) + ADAM_WD * w_ref[...])
        nm_ref[...] = nm
        nv_ref[...] = nv

    spec = pl.BlockSpec((tr, c), lambda i: (i, 0))
    sds = jax.ShapeDtypeStruct((r, c), F32)
    return pl.pallas_call(body, name=name, grid=(r // tr,), in_specs=[spec] * 4, out_specs=[spec] * 3,
                          out_shape=[sds] * 3, compiler_params=_cparams(("parallel",)))(w, g, m, v)


def _sum_lead(x, name, out_dtype=F32):
    n, r, c = x.shape
    tr = r
    for cand in (512, 256, 128, 64, 32, 16, 8):
        if r % cand == 0 and n * cand * c * 4 <= 4 * 1024 * 1024:
            tr = cand
            break

    def body(x_ref, o_ref):
        acc = x_ref[0].astype(F32)
        for k in range(1, n):
            acc = acc + x_ref[k].astype(F32)
        o_ref[...] = acc.astype(o_ref.dtype)

    return pl.pallas_call(body, name=name, grid=(r // tr,),
                          in_specs=[pl.BlockSpec((n, tr, c), lambda i: (0, i, 0))],
                          out_specs=pl.BlockSpec((tr, c), lambda i: (i, 0)),
                          out_shape=jax.ShapeDtypeStruct((r, c), out_dtype),
                          compiler_params=_cparams(("parallel",)))(x)


def _row_tile(rows, row_bytes, budget, least=8):
    for cand in (1024, 512, 256, 128, 64, 32, 16, 8):
        if cand >= least and rows % cand == 0 and cand * row_bytes <= budget:
            return cand
    return rows


def _cast_into_slot(w, slot, name):
    r, c = w.shape
    tr = _row_tile(r, c * 4, 4 * 1024 * 1024, least=16)

    def body(slot_ref, w_ref, o_ref):
        o_ref[...] = w_ref[...].astype(o_ref.dtype)

    gs = pltpu.PrefetchScalarGridSpec(
        num_scalar_prefetch=1, grid=(r // tr,),
        in_specs=[pl.BlockSpec((tr, c), lambda i, s: (i, 0))],
        out_specs=pl.BlockSpec((None, tr, c), lambda i, s: (s[0], i, 0)))
    return pl.pallas_call(body, name=name, grid_spec=gs, out_shape=jax.ShapeDtypeStruct((N_CHIPS, r, c), BF16),
                          compiler_params=_cparams(("parallel",)))(slot, w)


def _sum_own(p, t, sel, name):
    _, h, c = p.shape
    tr = _row_tile(h, c * 4, 2 * 1024 * 1024, least=16)
    nblk = h // tr

    def body(sel_ref, p_ref, t_ref, o_ref):
        acc = p_ref[...].astype(F32)
        for k in range(3):
            acc = acc + t_ref[k].astype(F32)
        o_ref[...] = acc

    gs = pltpu.PrefetchScalarGridSpec(
        num_scalar_prefetch=1, grid=(nblk,),
        in_specs=[pl.BlockSpec((None, tr, c), lambda i, s: (s[0], i, 0)),
                  pl.BlockSpec((3, tr, c), lambda i, s: (0, i, 0))],
        out_specs=pl.BlockSpec((tr, c), lambda i, s: (s[1] * nblk + i, 0)))
    return pl.pallas_call(body, name=name, grid_spec=gs, out_shape=jax.ShapeDtypeStruct((2 * h, c), F32),
                          compiler_params=_cparams(("parallel",)))(sel, p, t)


def _add_half(g, t, half, name):
    n, r, c = g.shape
    h = r // 2
    tr = h
    for cand in (512, 256, 128, 64, 32, 16):
        if h % cand == 0 and cand * c * 2 <= 2 * 1024 * 1024:
            tr = cand
            break
    nblk = h // tr

    def body(half_ref, g_ref, t_ref, o_ref):
        o_ref[...] = (g_ref[...].astype(F32) + t_ref[...].astype(F32)).astype(o_ref.dtype)

    gs = pltpu.PrefetchScalarGridSpec(
        num_scalar_prefetch=1, grid=(n, nblk),
        in_specs=[pl.BlockSpec((None, tr, c), lambda j, i, hr: (j, hr[0] * nblk + i, 0)),
                  pl.BlockSpec((None, tr, c), lambda j, i, hr: (j, i, 0))],
        out_specs=pl.BlockSpec((None, tr, c), lambda j, i, hr: (j, i, 0)))
    return pl.pallas_call(body, name=name, grid_spec=gs, out_shape=jax.ShapeDtypeStruct((n, h, c), BF16),
                          compiler_params=_cparams(("parallel", "parallel")))(half, g, t)


def _position():
    x, y, c = lax.axis_index("x"), lax.axis_index("y"), lax.axis_index("c")
    return x, y, c


def _allgather8(xs, name):
    m_per, n = xs.shape

    def body(x_ref, out_ref, send_sems, recv_sems, local_sem):
        x, y, c = _position()
        me, sibling = (x, y, c), (x, y, 1 - c)
        chips = [(1 - x, y), (x, 1 - y), (1 - x, 1 - y)]

        def rows(px, py, pc):
            return out_ref.at[pl.ds((4 * px + 2 * py + pc) * m_per, m_per), :]

        def copy(k, block, to, src=None):
            return pltpu.make_async_remote_copy(
                src_ref=rows(*block) if src is None else src, dst_ref=rows(*block),
                send_sem=send_sems.at[k], recv_sem=recv_sems.at[k], device_id=to, device_id_type=MESH)

        mine = pltpu.make_async_copy(x_ref, rows(*me), local_sem)
        mine.start()
        first = [copy(0, me, sibling, src=x_ref)]
        first += [copy(1 + j, me, (*chip, c), src=x_ref) for j, chip in enumerate(chips)]
        for cp in first:
            cp.start()
        passed = [copy(4 + j, (*chip, c), sibling) for j, chip in enumerate(chips)]
        for j, chip in enumerate(chips):
            copy(1 + j, (*chip, c), me).wait_recv()
            passed[j].start()
        copy(0, sibling, me).wait_recv()
        for j, chip in enumerate(chips):
            copy(4 + j, (*chip, 1 - c), me).wait_recv()
        for cp in first + passed:
            cp.wait_send()
        mine.wait()

    return pl.pallas_call(
        body, name=name, out_shape=jax.ShapeDtypeStruct((N_DEV * m_per, n), xs.dtype),
        in_specs=[pl.BlockSpec(memory_space=pltpu.VMEM)], out_specs=pl.BlockSpec(memory_space=pltpu.VMEM),
        scratch_shapes=[pltpu.SemaphoreType.DMA((7,)), pltpu.SemaphoreType.DMA((7,)), pltpu.SemaphoreType.DMA],
        compiler_params=pltpu.CompilerParams(vmem_limit_bytes=VMEM_LIMIT_BYTES),
    )(xs)


_HBM = pl.BlockSpec(memory_space=pltpu.HBM)


_SEM = pl.BlockSpec(memory_space=pltpu.SEMAPHORE)
_ANY = pl.BlockSpec(memory_space=pl.ANY)
_EFFECT = pltpu.SideEffectType.DATAFLOW_SIDE_EFFECTING


def _in_hbm(a):
    return pltpu.with_memory_space_constraint(a, pltpu.HBM)


def _gather_start(ws, groups, name):
    n = len(ws)

    def body(*refs):
        in_refs = refs[:n]
        sems, token = refs[2 * n:-1], refs[-1]
        x, y, c = _position()
        mychip = 2 * x + y
        chips = [(1 - x, y), (x, 1 - y), (1 - x, 1 - y)]
        for g, members in enumerate(groups):
            for k, i in enumerate(members):
                h = ws[i].shape[1] // 2
                mine = in_refs[i].at[mychip, pl.ds(c * h, h), :]
                for j, (px, py) in enumerate(chips):
                    pltpu.make_async_remote_copy(
                        src_ref=mine, dst_ref=mine, send_sem=sems[2 * g].at[3 * k + j],
                        recv_sem=sems[2 * g + 1].at[3 * k + j], device_id=(px, py, c), device_id_type=MESH).start()
        token[...] = jnp.zeros_like(token)

    sem_shapes = [pltpu.SemaphoreType.DMA((3 * len(m),)) for m in groups for _ in range(2)]
    res = pl.pallas_call(
        body, name=name,
        out_shape=[pltpu.HBM(w.shape, w.dtype) for w in ws] + sem_shapes + [jax.ShapeDtypeStruct((8, 128), F32)],
        in_specs=[_HBM] * n,
        out_specs=[_HBM] * n + [_SEM] * len(sem_shapes) + [pl.BlockSpec(memory_space=pltpu.VMEM)],
        input_output_aliases={i: i for i in range(n)},
        compiler_params=pltpu.CompilerParams(has_side_effects=_EFFECT),
    )(*[_in_hbm(w) for w in ws])
    bufs, sems, token = res[:n], res[n:-1], res[-1]
    return list(bufs), [(sems[2 * g], sems[2 * g + 1]) for g in range(len(groups))], token


def _gather_wait(bufs, send_sems, recv_sems, after, name):
    m = len(bufs)

    def body(*refs):
        in_refs = refs[:m]
        send, recv = refs[m], refs[m + 1]
        x, y, c = _position()
        mychip = 2 * x + y
        chips = [(1 - x, y), (x, 1 - y), (1 - x, 1 - y)]
        for k in range(m):
            h = bufs[k].shape[1] // 2
            mine = in_refs[k].at[mychip, pl.ds(c * h, h), :]
            for j, (px, py) in enumerate(chips):
                cp = pltpu.make_async_remote_copy(
                    src_ref=mine, dst_ref=in_refs[k].at[2 * px + py, pl.ds(c * h, h), :],
                    send_sem=send.at[3 * k + j], recv_sem=recv.at[3 * k + j],
                    device_id=(px, py, c), device_id_type=MESH)
                cp.wait_send()
                cp.wait_recv()

    res = pl.pallas_call(
        body, name=name, out_shape=[pltpu.HBM(b.shape, b.dtype) for b in bufs],
        in_specs=[_HBM] * m + [_SEM, _SEM, _ANY], out_specs=[_HBM] * m,
        input_output_aliases={k: k for k in range(m)},
        compiler_params=pltpu.CompilerParams(has_side_effects=_EFFECT),
    )(*bufs, send_sems, recv_sems, after)
    return list(res)


def _forward_halves(ws, name):
    n = len(ws)

    def body(*refs):
        out_refs = refs[n:2 * n]
        send_sems, recv_sems = refs[2 * n:]
        x, y, c = _position()
        me, sibling = (x, y, c), (x, y, 1 - c)
        chips = [(1 - x, y), (x, 1 - y), (1 - x, 1 - y)]
        cps = []
        for i in range(n):
            h = ws[i].shape[1] // 2
            for j, (px, py) in enumerate(chips):
                got = out_refs[i].at[2 * px + py, pl.ds(c * h, h), :]
                cp = pltpu.make_async_remote_copy(
                    src_ref=got, dst_ref=got, send_sem=send_sems.at[3 * i + j], recv_sem=recv_sems.at[3 * i + j],
                    device_id=sibling, device_id_type=MESH)
                cp.start()
                cps.append(cp)
        for i in range(n):
            h = ws[i].shape[1] // 2
            for j, (px, py) in enumerate(chips):
                other = out_refs[i].at[2 * px + py, pl.ds((1 - c) * h, h), :]
                pltpu.make_async_remote_copy(
                    src_ref=other, dst_ref=other, send_sem=send_sems.at[3 * i + j], recv_sem=recv_sems.at[3 * i + j],
                    device_id=me, device_id_type=MESH).wait_recv()
        for cp in cps:
            cp.wait_send()

    return pl.pallas_call(
        body, name=name,
        out_shape=[jax.ShapeDtypeStruct(w.shape, w.dtype) for w in ws],
        in_specs=[_HBM] * n, out_specs=[_HBM] * n, input_output_aliases={i: i for i in range(n)},
        scratch_shapes=[pltpu.SemaphoreType.DMA((3 * n,)), pltpu.SemaphoreType.DMA((3 * n,))],
    )(*ws)


def _swap_halves(gs, name):
    n = len(gs)

    def body(*refs):
        in_refs, out_refs = refs[:n], refs[n:2 * n]
        send_sems, recv_sems = refs[2 * n:]
        x, y, c = _position()
        cps = []
        for i in range(n):
            h = gs[i].shape[1] // 2
            cp = pltpu.make_async_remote_copy(
                src_ref=in_refs[i].at[:, pl.ds((1 - c) * h, h), :], dst_ref=out_refs[i],
                send_sem=send_sems.at[i], recv_sem=recv_sems.at[i], device_id=(x, y, 1 - c), device_id_type=MESH)
            cp.start()
            cps.append(cp)
        for cp in cps:
            cp.wait()

    return pl.pallas_call(
        body, name=name,
        out_shape=[jax.ShapeDtypeStruct((g.shape[0], g.shape[1] // 2, g.shape[2]), g.dtype) for g in gs],
        in_specs=[_HBM] * n, out_specs=[_HBM] * n,
        scratch_shapes=[pltpu.SemaphoreType.DMA((n,)), pltpu.SemaphoreType.DMA((n,))],
    )(*gs)


def _scatter_copies(p_refs, land_refs, send, recv):
    x, y, c = _position()
    chips = [(1 - x, y), (x, 1 - y), (1 - x, 1 - y)]
    return [pltpu.make_async_remote_copy(
        src_ref=p_refs[i].at[2 * px + py], dst_ref=land_refs[i].at[j],
        send_sem=send.at[3 * i + j], recv_sem=recv.at[3 * i + j], device_id=(px, py, c), device_id_type=MESH)
        for i in range(len(p_refs)) for j, (px, py) in enumerate(chips)]


def _scatter_start(ps, name):
    n = len(ps)
    lands = [lax.empty((3,) + p.shape[1:], p.dtype) for p in ps]

    def body(*refs):
        for cp in _scatter_copies(refs[:n], refs[n:2 * n], refs[4 * n], refs[4 * n + 1]):
            cp.start()
        refs[4 * n + 2][...] = jnp.zeros_like(refs[4 * n + 2])

    res = pl.pallas_call(
        body, name=name,
        out_shape=[pltpu.HBM(a.shape, a.dtype) for a in list(ps) + lands]
        + [pltpu.SemaphoreType.DMA((3 * n,)), pltpu.SemaphoreType.DMA((3 * n,)), jax.ShapeDtypeStruct((8, 128), F32)],
        in_specs=[_HBM] * (2 * n),
        out_specs=[_HBM] * (2 * n) + [_SEM, _SEM, pl.BlockSpec(memory_space=pltpu.VMEM)],
        input_output_aliases={i: i for i in range(2 * n)},
        compiler_params=pltpu.CompilerParams(has_side_effects=_EFFECT),
    )(*[_in_hbm(a) for a in list(ps) + lands])
    return list(res[:n]), list(res[n:2 * n]), res[2 * n], res[2 * n + 1], res[2 * n + 2]


def _scatter_wait(ps, lands, send_sems, recv_sems, after, name):
    n = len(ps)

    def body(*refs):
        for cp in _scatter_copies(refs[:n], refs[n:2 * n], refs[2 * n], refs[2 * n + 1]):
            cp.wait_send()
            cp.wait_recv()

    res = pl.pallas_call(
        body, name=name, out_shape=[pltpu.HBM(a.shape, a.dtype) for a in list(ps) + list(lands)],
        in_specs=[_HBM] * (2 * n) + [_SEM, _SEM, _ANY], out_specs=[_HBM] * (2 * n),
        input_output_aliases={i: i for i in range(2 * n)},
        compiler_params=pltpu.CompilerParams(has_side_effects=_EFFECT),
    )(*ps, *lands, send_sems, recv_sems, after)
    return list(res[:n]), list(res[n:])


def _join_halves(rs, name):
    n = len(rs)

    def body(*refs):
        out_refs = refs[n:2 * n]
        send_sems, recv_sems = refs[2 * n:]
        x, y, c = _position()
        cps = []
        for i in range(n):
            h = rs[i].shape[0] // 2
            mine = out_refs[i].at[pl.ds(c * h, h), :]
            cp = pltpu.make_async_remote_copy(
                src_ref=mine, dst_ref=mine, send_sem=send_sems.at[i], recv_sem=recv_sems.at[i],
                device_id=(x, y, 1 - c), device_id_type=MESH)
            cp.start()
            cps.append(cp)
        for i in range(n):
            h = rs[i].shape[0] // 2
            other = out_refs[i].at[pl.ds((1 - c) * h, h), :]
            pltpu.make_async_remote_copy(
                src_ref=other, dst_ref=other, send_sem=send_sems.at[i], recv_sem=recv_sems.at[i],
                device_id=(x, y, c), device_id_type=MESH).wait_recv()
        for cp in cps:
            cp.wait_send()

    return pl.pallas_call(
        body, name=name,
        out_shape=[jax.ShapeDtypeStruct(r.shape, r.dtype) for r in rs],
        in_specs=[_HBM] * n, out_specs=[_HBM] * n, input_output_aliases={i: i for i in range(n)},
        scratch_shapes=[pltpu.SemaphoreType.DMA((n,)), pltpu.SemaphoreType.DMA((n,))],
    )(*rs)


def _t5_buckets_block():
    qi = np.arange(BLOCK)[:, None]
    ki = np.arange(2 * BLOCK)[None, :]
    n = np.maximum(qi + BLOCK - ki, 0)
    max_exact = NUM_BUCKETS // 2
    large = max_exact + (np.log(np.maximum(n, 1) / max_exact) / np.log(MAX_DISTANCE / max_exact)
                         * (NUM_BUCKETS - max_exact)).astype(np.int32)
    large = np.minimum(large, NUM_BUCKETS - 1)
    return np.where(n < max_exact, n, large).astype(np.int32)


def _discretise(lambda_re, lambda_im, log_step, b_re, b_im):
    lam_re = jnp.minimum(lambda_re, -1e-4)
    lam_im = lambda_im
    delta = jnp.exp(log_step)[:, None]
    mag = jnp.exp(lam_re * delta)
    ang = lam_im * delta
    abar_re, abar_im = mag * jnp.cos(ang), mag * jnp.sin(ang)
    num_re, num_im = abar_re - 1.0, abar_im
    den = lam_re * lam_re + lam_im * lam_im
    f_re = (num_re * lam_re + num_im * lam_im) / den
    f_im = (num_im * lam_re - num_re * lam_im) / den
    bbar_re = f_re[..., None] * b_re - f_im[..., None] * b_im
    bbar_im = f_re[..., None] * b_im + f_im[..., None] * b_re
    return abar_re, abar_im, bbar_re, bbar_im


def _interleave(v, nc):
    s, w = v.shape
    return v.reshape(nc, s // nc, w).transpose(1, 0, 2).reshape(s, w)


def _deinterleave(v, nc):
    s, w = v.shape
    return v.reshape(s // nc, nc, w).transpose(1, 0, 2).reshape(s, w)


_SMALL = ("norm1_g", "b_in", "attn_sinks", "rel_bias", "lambda_re", "lambda_im", "log_step", "ssm_b_re",
          "ssm_b_im", "ssm_c_re", "ssm_c_im", "ssm_d", "b_glu", "norm2_g", "final_g")


def _pack(parts):
    rows = []
    for p in parts:
        f = p.reshape(-1).astype(F32)
        pad = (-f.shape[0]) % 128
        rows.append(jnp.pad(f, (0, pad)).reshape(-1, 128))
    out = jnp.concatenate(rows, axis=0)
    pad = (-out.shape[0]) % 256
    return jnp.pad(out, ((0, pad), (0, 0)))


def _unpack(packed, shapes):
    res, r = [], 0
    for shp in shapes:
        size = int(np.prod(shp))
        nr = -(-size // 128)
        res.append(packed[r:r + nr].reshape(-1)[:size].reshape(shp))
        r += nr
    return res


def kernel(x, c, w_ada, b_ada, norm1_g, w_in, b_in, attn_sinks, rel_bias, lambda_re, lambda_im, log_step, ssm_b_re, ssm_b_im, ssm_c_re, ssm_c_im, ssm_d, w_glu, b_glu, w_attn_proj, w_ssm_proj, w_out, norm2_g, w_ff1, w_ff2, final_g, loss_target, m_w_ada, m_b_ada, m_norm1_g, m_w_in, m_b_in, m_attn_sinks, m_rel_bias, m_lambda_re, m_lambda_im, m_log_step, m_ssm_b_re, m_ssm_b_im, m_ssm_c_re, m_ssm_c_im, m_ssm_d, m_w_glu, m_b_glu, m_w_attn_proj, m_w_ssm_proj, m_w_out, m_norm2_g, m_w_ff1, m_w_ff2, m_final_g, v_w_ada, v_b_ada, v_norm1_g, v_w_in, v_b_in, v_attn_sinks, v_rel_bias, v_lambda_re, v_lambda_im, v_log_step, v_ssm_b_re, v_ssm_b_im, v_ssm_c_re, v_ssm_c_im, v_ssm_d, v_w_glu, v_b_glu, v_w_attn_proj, v_w_ssm_proj, v_w_out, v_norm2_g, v_w_ff1, v_w_ff2, v_final_g):
    given = dict(locals())
    S, D = x.shape[1], x.shape[2]
    SSM_W = w_glu.shape[2]
    G = SSM_W // SSM_GROUP_CH
    NST = G * SSM_STATE
    DFF = w_ff2.shape[1] * N_CHIPS
    INW = w_in.shape[2] * N_CHIPS
    o_q, o_k, o_v, o_u = 0, ATTN_WIDTH, ATTN_WIDTH + KV_WIDTH, ATTN_WIDTH + 2 * KV_WIDTH
    o_ga, o_gs = o_u + SSM_W, o_u + SSM_W + D
    mx, my, mc = _position()
    my_chip = 2 * mx + my
    my_b = 4 * mx + 2 * my + mc

    xv, tgt = x[0], loss_target[0]

    big = dict(w_in=w_in[0], w_glu=w_glu[0], w_attn_proj=w_attn_proj[0], w_ssm_proj=w_ssm_proj[0],
               w_out=w_out[0], w_ff1=w_ff1[0], w_ff2=w_ff2[0])
    big_names = list(big)
    colsharded = {"w_in", "w_attn_proj", "w_ssm_proj", "w_ff1"}
    chip_sel = my_chip.astype(jnp.int32).reshape(1)
    slots = [_cast_into_slot(big[k], chip_sel, "cast_" + k) for k in big_names]
    gather_groups = [["w_in"], ["w_attn_proj", "w_ssm_proj", "w_glu", "w_out"], ["w_ff1", "w_ff2"]]
    in_flight, gather_sems, gather_token = _gather_start(
        slots, [[big_names.index(k) for k in grp] for grp in gather_groups], "gather_start")
    gathered = {}

    def finish_gather(g, after):
        bufs = [in_flight[big_names.index(k)] for k in gather_groups[g]]
        bufs = _gather_wait(bufs, gather_sems[g][0], gather_sems[g][1], after, "gather_wait_%d" % g)
        gathered.update(zip(gather_groups[g], _forward_halves(bufs, "gather_forward_%d" % g)))

    def tied(v, token):
        return lax.optimization_barrier((v, token))[0]

    def wop(k):
        g = gathered[k]
        return _Op(g, N_CHIPS) if k in colsharded else _Op(g.reshape(g.shape[0] * g.shape[1], g.shape[2]))

    grads = {}
    half = mc.astype(jnp.int32).reshape(1)
    sel = jnp.stack([my_chip, mc]).astype(jnp.int32)

    def rs_begin(tag, named):
        keys, gl = list(named), []
        for k in keys:
            gk = named[k]
            if k not in colsharded:
                gk = gk.reshape(N_CHIPS, gk.shape[0] // N_CHIPS, gk.shape[1])
            gl.append(gk)
        t1 = _swap_halves(gl, "rs_swap_" + tag)
        ps = [_add_half(g, t, half, "rs_add_" + k) for g, t, k in zip(gl, t1, keys)]
        ps, lands, ssem, rsem, token = _scatter_start(ps, "rs_start_" + tag)
        return (keys, ps, lands, ssem, rsem), token

    def rs_end(tag, state, after):
        keys, ps, lands, ssem, rsem = state
        ps, lands = _scatter_wait(ps, lands, ssem, rsem, after, "rs_wait_" + tag)
        rs = [_sum_own(p, t, sel, "rs_sum_" + k) for p, t, k in zip(ps, lands, keys)]
        full = _join_halves(rs, "rs_join_" + tag)
        for k, f in zip(keys, full):
            grads[k] = f[None]
        return full[-1]

    c_all = _allgather8(jnp.pad(tied(c, gather_token), ((0, 7), (0, 0))), "gather_c").reshape(N_DEV, 8, D)[:, 0]
    c16 = jnp.pad(c_all, ((0, 8), (0, 0)))
    b_ada_mine = lax.dynamic_slice(b_ada.reshape(N_CHIPS, -1), (my_chip, 0), (1, w_ada.shape[2]))
    mod_sh = _mm(c16, w_ada[0], "NN", name="mod", M=16, N=w_ada.shape[2], K=D, a_fn=_silu,
                 epilogue=lambda acc, b: (acc + b,), extras=[(b_ada_mine, "row")])
    mod_all = _allgather8(mod_sh[:8], "gather_mod").reshape(N_DEV, 8, -1)
    mod_row = jnp.concatenate(
        [lax.dynamic_slice(mod_all, (2 * j, my_b, 0), (1, 1, mod_all.shape[2]))[0] for j in range(N_CHIPS)], axis=1)
    sh1, sc1, g1, sh2, sc2, g2 = [mod_row[:, i * D:(i + 1) * D] for i in range(6)]

    disc_in = (lambda_re[0], lambda_im[0], log_step[0], ssm_b_re[0], ssm_b_im[0])
    (abar_re, abar_im, bbar_re, bbar_im), disc_vjp = jax.vjp(_discretise, *disc_in)
    eye = jnp.eye(G, dtype=F32)
    bd = jnp.concatenate([jnp.einsum("gnp,gh->gphn", bb, eye).reshape(SSM_W, NST) for bb in (bbar_re, bbar_im)], axis=1)
    cd = jnp.concatenate([jnp.einsum("gpn,gh->gnhp", cc, eye).reshape(NST, SSM_W)
                          for cc in (ssm_c_re[0], -ssm_c_im[0])], axis=0)
    a_fwd = jnp.stack([abar_re.reshape(1, NST), abar_im.reshape(1, NST)])
    a_bwd = jnp.stack([abar_re.reshape(1, NST), -abar_im.reshape(1, NST)])
    d_row = ssm_d

    buckets = _t5_buckets_block()
    onehot = (jnp.asarray(buckets.reshape(-1, 1)) == jnp.arange(128, dtype=jnp.int32)[None, :]).astype(BF16)
    rb_hi = rel_bias.astype(BF16)
    rb_lo = (rel_bias - rb_hi.astype(F32)).astype(BF16)
    rb_lo2 = (rel_bias - rb_hi.astype(F32) - rb_lo.astype(F32)).astype(BF16)
    rb3 = jnp.pad(jnp.concatenate([rb_hi, rb_lo, rb_lo2], axis=1),
                  ((0, 128 - NUM_BUCKETS), (0, 128 - 3 * N_Q_HEADS)))
    b3 = _mm(onehot, rb3, "NN", name="rel_bias_rows", M=BLOCK * 2 * BLOCK, N=128, K=128)
    bias = (b3[:, :N_Q_HEADS] + b3[:, N_Q_HEADS:2 * N_Q_HEADS]) + b3[:, 2 * N_Q_HEADS:3 * N_Q_HEADS]
    bias = jnp.transpose(bias.reshape(BLOCK, 2 * BLOCK, N_Q_HEADS), (2, 0, 1))
    sinks_b = jnp.broadcast_to(attn_sinks[0][:, None, None], (N_Q_HEADS, BLOCK, 128)).reshape(N_Q_HEADS * BLOCK, 128)

    h1 = _rowwise(_norm_mod, [(xv, "tile", D), (norm1_g, "row", D), (sh1, "row", D), (sc1, "row", D)],
                  [(D, BF16)], [], name="norm1", rows=S)[0]
    finish_gather(0, h1)
    proj = _mm(h1, wop("w_in"), "NN", name="proj", M=S, N=INW, K=D,
               epilogue=lambda acc, b: (acc + b,), extras=[(b_in, "row")])

    def heads(v2d, nh):
        return v2d.reshape(S, nh, HEAD_DIM).transpose(1, 0, 2)

    def unheads(v3d):
        return v3d.transpose(1, 0, 2).reshape(S, -1)

    qh = heads(proj[:, o_q:o_k], N_Q_HEADS)
    kh = heads(proj[:, o_k:o_v], N_KV_HEADS)
    vh = heads(proj[:, o_v:o_u], N_KV_HEADS)
    attn = unheads(_attn_fwd(qh, kh, vh, sinks_b, bias, "attn_fwd"))
    finish_gather(1, attn)
    y_attn = _mm(attn, wop("w_attn_proj"), "NN", name="attn_proj", M=S, N=D, K=ATTN_WIDTH)

    u = proj[:, o_u:o_ga]
    u_il = _interleave(u, SCAN_CHUNKS)
    bu = _mm(u_il, bd, "NN", name="ssm_bu", M=S, N=2 * NST, K=SSM_W, out_nsh=2)
    xs = _scan(a_fwd, bu, None, reverse=False, name="scan_fwd", tc=256)
    y_il = _mm(_Op(xs, 2), cd, "NN", name="ssm_y", M=S, N=SSM_W, K=2 * NST,
               epilogue=lambda acc, uu, dd: (acc + dd * uu,), extras=[(u_il, "tile"), (d_row, "row")])
    y = _deinterleave(y_il, SCAN_CHUNKS)
    z0b = _rowwise(_gelu, [(y, "tile", SSM_W)], [(SSM_W, BF16)], [], name="gelu", rows=S)[0]
    z, t_glu = _mm(z0b, wop("w_glu"), "NN", name="glu", M=S, N=SSM_W, K=SSM_W, out_dtypes=(BF16, F32),
                   epilogue=lambda acc, b, yy: (_gelu(yy) * _sigmoid(acc + b), acc + b),
                   extras=[(b_glu, "row"), (y, "tile")])
    y_ssm = _mm(z, wop("w_ssm_proj"), "NN", name="ssm_proj", M=S, N=D, K=SSM_W)

    merged = _rowwise(_merge, [(_Op(proj, coff=o_ga), "tile", D), (_Op(proj, coff=o_gs), "tile", D),
                               (y_attn, "tile", D), (y_ssm, "tile", D)], [(D, BF16)], [], name="merge", rows=S)[0]
    mo, x2 = _mm(merged, wop("w_out"), "NN", name="out_proj", M=S, N=D, K=D, out_dtypes=(F32, F32),
                 epilogue=lambda acc, xx, gg: (acc, xx + gg * acc), extras=[(xv, "tile"), (g1, "row")])
    h2 = _rowwise(_norm_mod, [(x2, "tile", D), (norm2_g, "row", D), (sh2, "row", D), (sc2, "row", D)],
                  [(D, BF16)], [], name="norm2", rows=S)[0]
    finish_gather(2, h2)
    a_b, r_b = _mm(h2, wop("w_ff1"), "NN", name="ff1", M=S, N=DFF, K=D, out_dtypes=(BF16, BF16),
                   epilogue=lambda acc: (acc, jnp.square(jnp.maximum(acc, 0.0))))
    ff, x3 = _mm(r_b, wop("w_ff2"), "NN", name="ff2", M=S, N=D, K=DFF, out_dtypes=(F32, F32),
                 epilogue=lambda acc, xx, gg: (acc, xx + gg * acc), extras=[(x2, "tile"), (g2, "row")],
                 tj=1024, tk=1024)

    def final_fn(x3b, gf, tb):
        def f(xx, gg):
            yv = xx * lax.rsqrt(jnp.mean(xx * xx, axis=-1, keepdims=True) + EPS) * gg
            err = jnp.square(yv - tb)
            return 0.5 * jnp.sum(jnp.mean(err, axis=-1, keepdims=True), axis=0, keepdims=True)
        lv, vjp = jax.vjp(f, x3b, gf)
        dx, dg = vjp(jnp.ones((1, 1), F32))
        return dx, dg, jnp.broadcast_to(lv, (1, 128))

    dx3, g_final, loss_acc = _rowwise(final_fn, [(x3, "tile", D), (final_g.reshape(1, D), "row", D), (tgt, "tile", D)],
                                      [(D, F32)], [D, 128], name="final", rows=S)

    def ff_out_bwd(dx3b, ffb, g2b):
        return dx3b * g2b, jnp.sum(dx3b * ffb, axis=0, keepdims=True)

    dff, d_g2 = _rowwise(ff_out_bwd, [(dx3, "tile", D), (ff, "tile", D), (g2, "row", D)], [(D, BF16)], [D],
                         name="ff_out_bwd", rows=S)
    da = _mm(dff, wop("w_ff2"), "NT", name="ff2_dx", M=S, N=DFF, K=D, out_dtypes=(BF16,),
             epilogue=lambda acc, ab: (acc * (2.0 * jnp.maximum(ab.astype(F32), 0.0)),), extras=[(a_b, "tile")])
    g_w_ff2 = _mm(r_b, dff, "TN", name="ff2_dw", M=DFF, N=D, K=S, out_dtypes=(BF16,), tj=1024, tk=1024)
    dh2 = _mm(da, wop("w_ff1"), "NT", name="ff1_dx", M=S, N=D, K=DFF, tj=1024, tk=1024)
    g_w_ff1 = _mm(h2, da, "TN", name="ff1_dw", M=D, N=DFF, K=S, out_dtypes=(BF16,), out_nsh=N_CHIPS, tj=1024, tk=1024)
    rs_ff, token_ff = rs_begin("ff", dict(w_ff2=g_w_ff2, w_ff1=g_w_ff1))

    def norm2_bwd(x2b, dh2b, dx3b, mob, gn, shb, scb, g1b):
        _, vjp = jax.vjp(_norm_mod, x2b, gn, shb, scb)
        dx, dg, dsh, dsc = vjp(dh2b)
        dx2b = dx + dx3b
        return dx2b, dx2b * g1b, dg, dsh, dsc, jnp.sum(dx2b * mob, axis=0, keepdims=True)

    dx2, dmo, g_norm2, d_sh2, d_sc2, d_g1 = _rowwise(
        norm2_bwd, [(x2, "tile", D), (dh2, "tile", D), (dx3, "tile", D), (mo, "tile", D),
                    (tied(norm2_g, token_ff), "row", D), (sh2, "row", D), (sc2, "row", D), (g1, "row", D)],
        [(D, F32), (D, BF16)], [D, D, D, D], name="norm2_bwd", rows=S, tr=128)
    dmerged = _mm(dmo, wop("w_out"), "NT", name="out_dx", M=S, N=D, K=D)
    g_w_out = _mm(merged, dmo, "TN", name="out_dw", M=D, N=D, K=S, out_dtypes=(BF16,), tk=1024)

    def merge_bwd(gab, gsb, yab, ysb, dmb):
        _, vjp = jax.vjp(_merge, gab, gsb, yab, ysb)
        return vjp(dmb)

    d_ga, d_gs, dy_attn, dy_ssm = _rowwise(
        merge_bwd, [(_Op(proj, coff=o_ga), "tile", D), (_Op(proj, coff=o_gs), "tile", D), (y_attn, "tile", D),
                    (y_ssm, "tile", D), (dmerged, "tile", D)],
        [(D, BF16), (D, BF16), (D, BF16), (D, BF16)], [], name="merge_bwd", rows=S, tr=128)

    dattn = _mm(dy_attn, wop("w_attn_proj"), "NT", name="attn_proj_dx", M=S, N=ATTN_WIDTH, K=D)
    g_w_attn_proj = _mm(attn, dy_attn, "TN", name="attn_proj_dw", M=ATTN_WIDTH, N=D, K=S, out_dtypes=(BF16,),
                        out_nsh=N_CHIPS, tk=1024)
    dqh, dkh, dvh, dsink_blk, dbias = _attn_bwd(qh, kh, vh, heads(dattn, N_Q_HEADS), sinks_b, bias, "attn_bwd")
    g_sinks = _sum_lead(dsink_blk.reshape(N_Q_HEADS, BLOCK, 128).transpose(1, 0, 2), "sinks_dw")[:, 0].reshape(1, N_Q_HEADS)
    g_rel = _mm(dbias.reshape(N_Q_HEADS, -1), onehot, "NN", name="rel_bias_dw", M=N_Q_HEADS, N=128,
                K=BLOCK * 2 * BLOCK, tk=4096)
    g_rel_bias = g_rel[:, :NUM_BUCKETS].T

    dz = _mm(dy_ssm, wop("w_ssm_proj"), "NT", name="ssm_proj_dx", M=S, N=SSM_W, K=D)
    g_w_ssm_proj = _mm(z, dy_ssm, "TN", name="ssm_proj_dw", M=SSM_W, N=D, K=S, out_dtypes=(BF16,),
                       out_nsh=N_CHIPS, tk=1024)

    def glu_bwd(dzb, yb, tb):
        z0 = _gelu(yb)
        sg = _sigmoid(tb)
        dt = dzb * z0 * sg * (1.0 - sg)
        return dt, dzb * sg, jnp.sum(dt, axis=0, keepdims=True)

    dt_b, dz0a, g_b_glu = _rowwise(glu_bwd, [(dz, "tile", SSM_W), (y, "tile", SSM_W), (t_glu, "tile", SSM_W)],
                                   [(SSM_W, BF16), (SSM_W, F32)], [SSM_W], name="glu_bwd", rows=S)

    def gelu_bwd(acc, dz0ab, yb):
        _, vjp = jax.vjp(_gelu, yb)
        return (vjp(acc + dz0ab)[0],)

    dy = _mm(dt_b, wop("w_glu"), "NT", name="glu_dx", M=S, N=SSM_W, K=SSM_W, epilogue=gelu_bwd,
             extras=[(dz0a, "tile"), (y, "tile")])
    g_w_glu = _mm(z0b, dt_b, "TN", name="glu_dw", M=SSM_W, N=SSM_W, K=S, out_dtypes=(BF16,), tk=1024)
    rs_mix, token_mix = rs_begin("mix", dict(w_out=g_w_out, w_attn_proj=g_w_attn_proj, w_ssm_proj=g_w_ssm_proj,
                                             w_glu=g_w_glu))
    dy_il = _interleave(tied(dy, token_mix), SCAN_CHUNKS)
    dxs =_mm(dy_il, cd, "NT", name="ssm_dx", M=S, N=2 * NST, K=SSM_W, out_nsh=2)
    g_cd = _mm(_Op(xs, 2), dy_il, "TN", name="ssm_dc", M=2 * NST, N=SSM_W, K=S, tk=1024)
    lam, d_abar = _scan(a_bwd, dxs, xs, reverse=True, name="scan_bwd", tc=128)

    def du_fn(acc, dyb, dd):
        return (acc + dd * dyb,)

    du_il = _mm(_Op(lam, 2), bd, "NT", name="ssm_du", M=S, N=SSM_W, K=2 * NST, epilogue=du_fn,
                extras=[(dy_il, "tile"), (d_row, "row")])
    g_bd = _mm(u_il, _Op(lam, 2), "TN", name="ssm_db", M=SSM_W, N=2 * NST, K=S, tk=1024)
    g_ssm_d = _rowwise(lambda dyb, ub: (jnp.sum(dyb * ub, axis=0, keepdims=True),),
                       [(dy_il, "tile", SSM_W), (u_il, "tile", SSM_W)], [], [SSM_W], name="ssm_dd", rows=S)[0]
    du = _deinterleave(du_il, SCAN_CHUNKS)

    g_cd4 = g_cd.reshape(2, G, SSM_STATE, G, SSM_GROUP_CH)
    g_c_re = jnp.einsum("gnhp,gh->gpn", g_cd4[0], eye)
    g_c_im = -jnp.einsum("gnhp,gh->gpn", g_cd4[1], eye)
    g_bd4 = g_bd.reshape(G, SSM_GROUP_CH, 2, G, SSM_STATE)
    g_bbar_re = jnp.einsum("gphn,gh->gnp", g_bd4[:, :, 0], eye)
    g_bbar_im = jnp.einsum("gphn,gh->gnp", g_bd4[:, :, 1], eye)
    g_lre, g_lim, g_lstep, g_bre, g_bim = disc_vjp(
        (d_abar[0].reshape(G, SSM_STATE), d_abar[1].reshape(G, SSM_STATE), g_bbar_re, g_bbar_im))

    dproj = jnp.concatenate([unheads(dqh).astype(BF16), unheads(dkh).astype(BF16), unheads(dvh).astype(BF16),
                             du.astype(BF16), d_ga, d_gs], axis=1)
    g_w_in = _mm(h1, dproj, "TN", name="proj_dw", M=D, N=INW, K=S, out_dtypes=(BF16,), out_nsh=N_CHIPS,
                 tj=INW // (2 * N_CHIPS), tk=1024)
    rs_in, token_in = rs_begin("in", dict(w_in=g_w_in))
    dh1 = _mm(tied(dproj, token_in), wop("w_in"), "NT", name="proj_dx", M=S, N=D, K=INW, tj=1024, tk=INW // N_CHIPS)
    g_b_in = _rowwise(lambda d: (jnp.sum(d.astype(F32), axis=0, keepdims=True),), [(dproj, "tile", INW)], [], [INW],
                      name="proj_db", rows=S)[0]

    def norm1_bwd(xb, dhb, dresb, gn, shb, scb):
        _, vjp = jax.vjp(_norm_mod, xb, gn, shb, scb)
        dx, dg, dsh, dsc = vjp(dhb)
        return dx + dresb, dg, dsh, dsc

    grad_x, g_norm1, d_sh1, d_sc1 = _rowwise(
        norm1_bwd, [(xv, "tile", D), (dh1, "tile", D), (dx2, "tile", D), (norm1_g, "row", D), (sh1, "row", D),
                    (sc1, "row", D)], [(D, F32)], [D, D, D], name="norm1_bwd", rows=S)

    dmod_row = jnp.concatenate([d_sh1, d_sc1, d_g1, d_sh2, d_sc2, d_g2], axis=1)
    dmod_all = _allgather8(jnp.pad(dmod_row, ((0, 7), (0, 0))), "gather_dmod").reshape(N_DEV, 8, -1)[:, 0]
    g_b_ada = _sum_lead(dmod_all.reshape(N_DEV, -1, 128), "b_ada_dw").reshape(1, -1)
    dmod_mine = lax.dynamic_slice(dmod_all.reshape(N_DEV, N_CHIPS, -1), (0, my_chip, 0), (N_DEV, 1, w_ada.shape[2]))[:, 0]
    g_w_ada = _mm(c16, jnp.pad(dmod_mine, ((0, 8), (0, 0))), "TN", name="ada_dw", M=D, N=w_ada.shape[2], K=16,
                  a_fn=_silu)

    small_g = dict(norm1_g=g_norm1, b_in=g_b_in, attn_sinks=g_sinks, rel_bias=g_rel_bias, lambda_re=g_lre[None],
                   lambda_im=g_lim[None], log_step=g_lstep[None], ssm_b_re=g_bre[None], ssm_b_im=g_bim[None],
                   ssm_c_re=g_c_re[None], ssm_c_im=g_c_im[None], ssm_d=g_ssm_d, b_glu=g_b_glu, norm2_g=g_norm2,
                   final_g=g_final.reshape(D))
    packed = _pack([loss_acc[:, :1]] + [small_g[k] for k in _SMALL])
    rows = packed.shape[0]
    summed = _sum_lead(_allgather8(packed, "gather_small").reshape(N_DEV, rows, 128), "small_sum")
    small_shapes = [(1,)] + [given[k].shape for k in _SMALL]
    parts = _unpack(summed, small_shapes)
    loss = parts[0].reshape(())
    grads.update(zip(_SMALL, parts[1:]))
    grads["b_ada"] = g_b_ada
    grads["w_ada"] = g_w_ada[None]

    done = rs_end("ff", rs_ff, summed)
    done = rs_end("mix", rs_mix, done)
    rs_end("in", rs_in, done)

    deltas, new_m, new_v = {}, {}, {}
    for k in big_names + ["w_ada"]:
        d_, m_, v_ = _adamw(given[k][0], grads[k][0], given["m_" + k][0], given["v_" + k][0], "adamw_" + k)
        deltas[k], new_m[k], new_v[k] = d_[None], m_[None], v_[None]
    small_all = list(_SMALL) + ["b_ada"]
    shapes = [given[k].shape for k in small_all]
    pw, pg = _pack([given[k] for k in small_all]), _pack([grads[k] for k in small_all])
    pm, pv = _pack([given["m_" + k] for k in small_all]), _pack([given["v_" + k] for k in small_all])
    d_, m_, v_ = _adamw(pw, pg, pm, pv, "adamw_small")
    for k, dd, mm, vv in zip(small_all, _unpack(d_, shapes), _unpack(m_, shapes), _unpack(v_, shapes)):
        deltas[k], new_m[k], new_v[k] = dd, mm, vv
        grads[k] = grads[k].reshape(given[k].shape)

    names = ["w_ada", "b_ada", "norm1_g", "w_in", "b_in", "attn_sinks", "rel_bias", "lambda_re", "lambda_im",
             "log_step", "ssm_b_re", "ssm_b_im", "ssm_c_re", "ssm_c_im", "ssm_d", "w_glu", "b_glu", "w_attn_proj",
             "w_ssm_proj", "w_out", "norm2_g", "w_ff1", "w_ff2", "final_g"]
    return (loss, grad_x[None], *[grads[n] for n in names], *[deltas[n] for n in names],
            *[new_m[n] for n in names], *[new_v[n] for n in names])
```

```python
import math

import numpy as np
import jax
import jax.numpy as jnp
from jax import lax
from jax.experimental import pallas as pl
from jax.experimental.pallas import tpu as pltpu

F32 = jnp.float32
BF16 = jnp.bfloat16
MESH = pl.DeviceIdType.MESH

HEAD_DIM = 64
N_Q_HEADS = 16
N_KV_HEADS = 4
GQA_GROUP = N_Q_HEADS // N_KV_HEADS
ATTN_WIDTH = N_Q_HEADS * HEAD_DIM
KV_WIDTH = N_KV_HEADS * HEAD_DIM
BLOCK = 128
NUM_BUCKETS = 32
MAX_DISTANCE = 128
NEG_INF = -1e30
SSM_GROUP_CH = 16
SSM_STATE = 64
EPS = 1e-6
ADAM_LR = 0.001
ADAM_B1 = 0.9
ADAM_B2 = 0.999
ADAM_EPS = 1e-08
ADAM_WD = 0.01
ADAM_STEP = 10

N_CHIPS = 4
N_DEV = 8
SCAN_CHUNKS = 8
VMEM_LIMIT_BYTES = 48 * 1024 * 1024


def _cparams(sem=None):
    return pltpu.CompilerParams(dimension_semantics=sem, vmem_limit_bytes=VMEM_LIMIT_BYTES)


class _Op:
    def __init__(self, arr, nsh=None, coff=0):
        self.arr, self.nsh, self.coff = arr, nsh, coff
        if nsh is None:
            self.rows, self.cols = arr.shape
        else:
            assert arr.shape[0] == nsh
            self.rows, self.cols = arr.shape[1], arr.shape[2] * nsh

    def spec(self, br, bc, idx):
        assert self.coff % bc == 0
        off = self.coff // bc
        if self.nsh is None:
            return pl.BlockSpec((br, bc), lambda *g: (idx(*g)[0], idx(*g)[1] + off))
        per = (self.cols // self.nsh) // bc
        assert per * bc * self.nsh == self.cols

        def imap(*g):
            r, c = idx(*g)
            c = c + off
            return (c // per, r, c % per)
        return pl.BlockSpec((None, br, bc), imap)


def _as_op(a):
    return a if isinstance(a, _Op) else _Op(a)


def _mm(a, b, mode, *, name, M, N, K, out_dtypes=(F32,), out_nsh=None, epilogue=None, extras=(),
        a_fn=None, ti=1024, tj=512, tk=2048):
    a, b = _as_op(a), _as_op(b)
    ti, tj, tk = min(ti, M), min(tj, N), min(tk, K)
    a_w = a.cols // a.nsh if a.nsh else None
    b_w = b.cols // b.nsh if b.nsh else None
    if a_w:
        ti, tk = (min(ti, a_w), tk) if mode == "TN" else (ti, min(tk, a_w))
    if b_w:
        tj, tk = (tj, min(tk, b_w)) if mode == "NT" else (min(tj, b_w), tk)
    if out_nsh:
        tj = min(tj, N // out_nsh)
    assert M % ti == 0 and N % tj == 0 and K % tk == 0, (name, M, N, K, ti, tj, tk)
    nk = K // tk
    if mode == "NN":
        a_spec = a.spec(ti, tk, lambda i, j, k: (i, k))
        b_spec = b.spec(tk, tj, lambda i, j, k: (k, j))
        dims = (((1,), (0,)), ((), ()))
    elif mode == "NT":
        a_spec = a.spec(ti, tk, lambda i, j, k: (i, k))
        b_spec = b.spec(tj, tk, lambda i, j, k: (j, k))
        dims = (((1,), (1,)), ((), ()))
    else:
        a_spec = a.spec(tk, ti, lambda i, j, k: (k, i))
        b_spec = b.spec(tk, tj, lambda i, j, k: (k, j))
        dims = (((0,), (0,)), ((), ()))
    ex_specs, ex_arrs = [], []
    for op, kind in extras:
        op = _as_op(op)
        if kind == "tile":
            ex_specs.append(op.spec(ti, tj, lambda i, j, k: (i, j)))
        else:
            ex_specs.append(op.spec(1, tj, lambda i, j, k: (0, j)))
        ex_arrs.append(op.arr)
    ne, no = len(ex_arrs), len(out_dtypes)
    if out_nsh is None:
        out_shapes = [jax.ShapeDtypeStruct((M, N), d) for d in out_dtypes]
        out_specs = [pl.BlockSpec((ti, tj), lambda i, j, k: (i, j)) for _ in out_dtypes]
    else:
        per = (N // out_nsh) // tj
        assert per * tj * out_nsh == N
        out_shapes = [jax.ShapeDtypeStruct((out_nsh, M, N // out_nsh), d) for d in out_dtypes]
        out_specs = [pl.BlockSpec((None, ti, tj), lambda i, j, k: (j // per, i, j % per)) for _ in out_dtypes]

    def body(a_ref, b_ref, *rest):
        ex_refs, out_refs, acc = rest[:ne], rest[ne:ne + no], rest[ne + no]
        k = pl.program_id(2)

        @pl.when(k == 0)
        def _():
            acc[...] = jnp.zeros_like(acc)

        av = a_ref[...]
        if a_fn is not None:
            av = a_fn(av)
        acc[...] += lax.dot_general(av.astype(BF16), b_ref[...].astype(BF16), dims,
                                    preferred_element_type=F32)

        @pl.when(k == nk - 1)
        def _():
            res = acc[...]
            outs = epilogue(res, *[r[...] for r in ex_refs]) if epilogue is not None else (res,)
            for o_ref, o in zip(out_refs, outs):
                o_ref[...] = o.astype(o_ref.dtype)

    outs = pl.pallas_call(
        body, name=name, grid=(M // ti, N // tj, nk),
        in_specs=[a_spec, b_spec] + ex_specs, out_specs=out_specs, out_shape=out_shapes,
        scratch_shapes=[pltpu.VMEM((ti, tj), F32)],
        compiler_params=_cparams(("parallel", "parallel", "arbitrary")),
    )(a.arr, b.arr, *ex_arrs)
    return outs[0] if no == 1 else outs


def _rowwise(fn, ins, outs, accs, *, name, rows, tr=256):
    tr = min(tr, rows)
    assert rows % tr == 0
    in_specs, arrs = [], []
    for op, kind, width in ins:
        op = _as_op(op)
        if kind == "tile":
            in_specs.append(op.spec(tr, width, lambda i: (i, 0)))
        else:
            in_specs.append(op.spec(op.rows, width, lambda i: (0, 0)))
        arrs.append(op.arr)
    ni, no, na = len(ins), len(outs), len(accs)
    out_shapes = [jax.ShapeDtypeStruct((rows, w), d) for w, d in outs]
    out_specs = [pl.BlockSpec((tr, w), lambda i: (i, 0)) for w, _ in outs]
    out_shapes += [jax.ShapeDtypeStruct((1, w), F32) for w in accs]
    out_specs += [pl.BlockSpec((1, w), lambda i: (0, 0)) for w in accs]

    def body(*refs):
        in_refs, out_refs, acc_refs = refs[:ni], refs[ni:ni + no], refs[ni + no:]
        res = fn(*[r[...] for r in in_refs])
        if not isinstance(res, (tuple, list)):
            res = (res,)
        for o_ref, r in zip(out_refs, res[:no]):
            o_ref[...] = r.astype(o_ref.dtype)
        if na:
            @pl.when(pl.program_id(0) == 0)
            def _():
                for a_ref in acc_refs:
                    a_ref[...] = jnp.zeros_like(a_ref)
            for a_ref, r in zip(acc_refs, res[no:]):
                a_ref[...] += r.astype(F32)

    res = pl.pallas_call(
        body, name=name, grid=(rows // tr,), in_specs=in_specs, out_specs=out_specs, out_shape=out_shapes,
        compiler_params=_cparams(("arbitrary",)),
    )(*arrs)
    return res


def _norm_mod(x, g, sh, sc):
    y = x * lax.rsqrt(jnp.mean(x * x, axis=-1, keepdims=True) + EPS) * g
    return y * (1.0 + sc) + sh


def _sigmoid(x):
    return 1.0 / (1.0 + jnp.exp(-x))


def _silu(x):
    return x * _sigmoid(x)


def _gelu(x):
    return 0.5 * x * (1.0 + jnp.tanh(math.sqrt(2.0 / math.pi) * (x + 0.044715 * (x * x * x))))


def _merge(ga, gs, ya, ys):
    return _sigmoid(ga) * ya + _sigmoid(gs) * ys


def _attn_head(q, kp, kc, vp, vc, sink, bias_p, bias_c, not_first):
    nt = (((1,), (1,)), ((), ()))
    nn = (((1,), (0,)), ((), ()))
    qb = q.astype(BF16)
    scale = HEAD_DIM ** -0.5
    sp = lax.dot_general(qb, kp.astype(BF16), nt, preferred_element_type=F32) * scale + bias_p
    sc = lax.dot_general(qb, kc.astype(BF16), nt, preferred_element_type=F32) * scale + bias_c
    qi = lax.broadcasted_iota(jnp.int32, sp.shape, 0) & (BLOCK - 1)
    ki = lax.broadcasted_iota(jnp.int32, sp.shape, 1)
    sp = jnp.where(jnp.logical_and(ki > qi, not_first), sp, NEG_INF)
    sc = jnp.where(ki <= qi, sc, NEG_INF)
    m = jnp.maximum(jnp.maximum(jnp.max(sp, axis=-1, keepdims=True), jnp.max(sc, axis=-1, keepdims=True)), sink)
    m = lax.stop_gradient(m)
    pp = jnp.exp(sp - m)
    pc = jnp.exp(sc - m)
    denom = jnp.sum(pp, axis=-1, keepdims=True) + jnp.sum(pc, axis=-1, keepdims=True) + jnp.exp(sink - m)
    o = lax.dot_general((pp / denom).astype(BF16), vp.astype(BF16), nn, preferred_element_type=F32)
    o = o + lax.dot_general((pc / denom).astype(BF16), vc.astype(BF16), nn, preferred_element_type=F32)
    return o


def _attn_fwd(qh, kh, vh, sinks, bias, name):
    s = qh.shape[1]
    nb = s // BLOCK
    G = GQA_GROUP
    R = G * BLOCK

    def body(q_ref, kp_ref, kc_ref, vp_ref, vc_ref, sink_ref, bias_ref, o_ref):
        not_first = pl.program_id(0) > 0
        for kv in range(N_KV_HEADS):
            hs = slice(kv * G, (kv + 1) * G)
            o = _attn_head(q_ref[hs].reshape(R, HEAD_DIM), kp_ref[kv], kc_ref[kv], vp_ref[kv], vc_ref[kv],
                           sink_ref[kv * R:(kv + 1) * R, 0:1],
                           bias_ref[hs, :, 0:BLOCK].reshape(R, BLOCK), bias_ref[hs, :, BLOCK:2 * BLOCK].reshape(R, BLOCK),
                           not_first)
            o_ref[hs] = o.reshape(G, BLOCK, HEAD_DIM).astype(o_ref.dtype)

    cur = lambda i: (0, i, 0)
    prev = lambda i: (0, jnp.maximum(i - 1, 0), 0)
    return pl.pallas_call(
        body, name=name, grid=(nb,),
        in_specs=[pl.BlockSpec((N_Q_HEADS, BLOCK, HEAD_DIM), cur),
                  pl.BlockSpec((N_KV_HEADS, BLOCK, HEAD_DIM), prev), pl.BlockSpec((N_KV_HEADS, BLOCK, HEAD_DIM), cur),
                  pl.BlockSpec((N_KV_HEADS, BLOCK, HEAD_DIM), prev), pl.BlockSpec((N_KV_HEADS, BLOCK, HEAD_DIM), cur),
                  pl.BlockSpec((N_Q_HEADS * BLOCK, 128), lambda i: (0, 0)),
                  pl.BlockSpec((N_Q_HEADS, BLOCK, 2 * BLOCK), lambda i: (0, 0, 0))],
        out_specs=pl.BlockSpec((N_Q_HEADS, BLOCK, HEAD_DIM), cur),
        out_shape=jax.ShapeDtypeStruct((N_Q_HEADS, s, HEAD_DIM), BF16),
        compiler_params=_cparams(("arbitrary",)),
    )(qh, kh, kh, vh, vh, sinks, bias)


def _attn_bwd(qh, kh, vh, doh, sinks, bias, name):
    s = qh.shape[1]
    nb = s // BLOCK
    G = GQA_GROUP
    R = G * BLOCK

    def body(q_ref, kp_ref, kc_ref, vp_ref, vc_ref, do_ref, sink_ref, bias_ref,
             dq_ref, dk_ref, dv_ref, dsink_ref, dbias_ref, ck, cv):
        i = pl.program_id(1)

        @pl.when(i == 0)
        def _():
            dsink_ref[...] = jnp.zeros_like(dsink_ref)
            dbias_ref[...] = jnp.zeros_like(dbias_ref)
            ck[...] = jnp.zeros_like(ck)
            cv[...] = jnp.zeros_like(cv)

        @pl.when(i < nb)
        def _():
            not_first = i > 0
            _, vjp = jax.vjp(lambda q, a, b, c, d, sk, e, f: _attn_head(q, a, b, c, d, sk, e, f, not_first),
                             q_ref[...].reshape(R, HEAD_DIM), kp_ref[...], kc_ref[...], vp_ref[...], vc_ref[...],
                             sink_ref[:, 0:1], bias_ref[:, :, 0:BLOCK].reshape(R, BLOCK),
                             bias_ref[:, :, BLOCK:2 * BLOCK].reshape(R, BLOCK))
            dq, dkp, dkc, dvp, dvc, dsk, dbp, dbc = vjp(do_ref[...].reshape(R, HEAD_DIM).astype(F32))
            dq_ref[...] = dq.reshape(G, BLOCK, HEAD_DIM)
            dsink_ref[...] += jnp.broadcast_to(dsk, (R, 128))
            dbias_ref[:, :, 0:BLOCK] += dbp.reshape(G, BLOCK, BLOCK)
            dbias_ref[:, :, BLOCK:2 * BLOCK] += dbc.reshape(G, BLOCK, BLOCK)
            dk_ref[...] = ck[...] + dkp
            dv_ref[...] = cv[...] + dvp
            ck[...] = dkc
            cv[...] = dvc

        @pl.when(i == nb)
        def _():
            dk_ref[...] = ck[...]
            dv_ref[...] = cv[...]

    qcur = lambda kv, i: (kv, jnp.minimum(i, nb - 1), 0)
    kcur = lambda kv, i: (kv, jnp.minimum(i, nb - 1), 0)
    kprev = lambda kv, i: (kv, jnp.clip(i - 1, 0, nb - 1), 0)
    qspec = pl.BlockSpec((G, BLOCK, HEAD_DIM), qcur)
    kc_spec = pl.BlockSpec((None, BLOCK, HEAD_DIM), kcur)
    kp_spec = pl.BlockSpec((None, BLOCK, HEAD_DIM), kprev)
    return pl.pallas_call(
        body, name=name, grid=(N_KV_HEADS, nb + 1),
        in_specs=[qspec, kp_spec, kc_spec, kp_spec, kc_spec, qspec,
                  pl.BlockSpec((R, 128), lambda kv, i: (kv, 0)),
                  pl.BlockSpec((G, BLOCK, 2 * BLOCK), lambda kv, i: (kv, 0, 0))],
        out_specs=[qspec, kp_spec, kp_spec,
                   pl.BlockSpec((R, 128), lambda kv, i: (kv, 0)),
                   pl.BlockSpec((G, BLOCK, 2 * BLOCK), lambda kv, i: (kv, 0, 0))],
        out_shape=[jax.ShapeDtypeStruct((N_Q_HEADS, s, HEAD_DIM), F32),
                   jax.ShapeDtypeStruct((N_KV_HEADS, s, HEAD_DIM), F32),
                   jax.ShapeDtypeStruct((N_KV_HEADS, s, HEAD_DIM), F32),
                   jax.ShapeDtypeStruct((N_Q_HEADS * BLOCK, 128), F32),
                   jax.ShapeDtypeStruct((N_Q_HEADS, BLOCK, 2 * BLOCK), F32)],
        scratch_shapes=[pltpu.VMEM((BLOCK, HEAD_DIM), F32), pltpu.VMEM((BLOCK, HEAD_DIM), F32)],
        compiler_params=_cparams(("arbitrary", "arbitrary")),
    )(qh, kh, kh, vh, vh, doh, sinks, bias)


def _cmul(ar, ai, br, bi):
    return ar * br - ai * bi, ar * bi + ai * br


def _scan(a, b, xs_prev, *, reverse, name, tc):
    _, s, c = b.shape
    nc = SCAN_CHUNKS
    steps = s // nc
    with_da = xs_prev is not None
    unroll = 8 if steps % 8 == 0 else 1

    def shift(v, d):
        row = lax.broadcasted_iota(jnp.int32, v.shape, 0)
        if reverse:
            return jnp.where(row < nc - d, pltpu.roll(v, nc - d, 0), 0.0)
        return jnp.where(row >= d, pltpu.roll(v, d, 0), 0.0)

    def body(*refs):
        if with_da:
            a_ref, b_ref, xp_ref, x_ref, da_ref = refs
        else:
            a_ref, b_ref, x_ref = refs
        ar = jnp.broadcast_to(a_ref[0], (nc, tc))
        ai = jnp.broadcast_to(a_ref[1], (nc, tc))

        def row_of(step):
            j = (steps - 1 - step) if reverse else step
            return pl.multiple_of(j * nc, nc)

        def p1(step, st):
            sr, si = st
            r0 = row_of(step)
            mr, mi = _cmul(ar, ai, sr, si)
            sr = mr + b_ref[0, pl.ds(r0, nc), :]
            si = mi + b_ref[1, pl.ds(r0, nc), :]
            x_ref[0, pl.ds(r0, nc), :] = sr
            x_ref[1, pl.ds(r0, nc), :] = si
            return sr, si
        zero = jnp.zeros((nc, tc), F32)
        er, ei = lax.fori_loop(0, steps, p1, (zero, zero), unroll=unroll)

        def pw(step, st):
            return _cmul(ar, ai, *st)
        pr, pi_ = lax.fori_loop(0, steps, pw, (jnp.ones((nc, tc), F32), zero), unroll=unroll)
        cr, ci = shift(er, 1), shift(ei, 1)
        d = 1
        while d < nc:
            mr, mi = _cmul(pr, pi_, shift(cr, d), shift(ci, d))
            cr, ci = cr + mr, ci + mi
            pr, pi_ = _cmul(pr, pi_, pr, pi_)
            d *= 2

        def p2(step, st):
            qr, qi, dar, dai = st
            r0 = row_of(step)
            qr, qi = _cmul(ar, ai, qr, qi)
            fr, fi = _cmul(qr, qi, cr, ci)
            xr = x_ref[0, pl.ds(r0, nc), :] + fr
            xi = x_ref[1, pl.ds(r0, nc), :] + fi
            x_ref[0, pl.ds(r0, nc), :] = xr
            x_ref[1, pl.ds(r0, nc), :] = xi
            if with_da:
                jm = jnp.where(step == steps - 1, steps - 1, steps - 2 - step)
                rp = pl.multiple_of(jm * nc, nc)
                vr, vi = xp_ref[0, pl.ds(rp, nc), :], xp_ref[1, pl.ds(rp, nc), :]
                row = lax.broadcasted_iota(jnp.int32, (nc, tc), 0)
                first = step == steps - 1
                sel = jnp.logical_and(first, row == 0)
                vr = jnp.where(sel, 0.0, jnp.where(first, pltpu.roll(vr, 1, 0), vr))
                vi = jnp.where(sel, 0.0, jnp.where(first, pltpu.roll(vi, 1, 0), vi))
                dar = dar + xr * vr + xi * vi
                dai = dai + xi * vr - xr * vi
            return qr, qi, dar, dai
        _, _, dar, dai = lax.fori_loop(0, steps, p2, (jnp.ones((nc, tc), F32), zero, zero, zero), unroll=unroll)
        if with_da:
            da_ref[0] = jnp.sum(dar, axis=0, keepdims=True)
            da_ref[1] = jnp.sum(dai, axis=0, keepdims=True)

    blk = pl.BlockSpec((2, s, tc), lambda i: (0, 0, i))
    vec = pl.BlockSpec((2, 1, tc), lambda i: (0, 0, i))
    in_specs, args = [vec, blk], [a, b]
    out_specs, out_shape = [blk], [jax.ShapeDtypeStruct((2, s, c), F32)]
    if with_da:
        in_specs.append(blk)
        args.append(xs_prev)
        out_specs.append(vec)
        out_shape.append(jax.ShapeDtypeStruct((2, 1, c), F32))
    res = pl.pallas_call(
        body, name=name, grid=(c // tc,), in_specs=in_specs, out_specs=out_specs, out_shape=out_shape,
        compiler_params=_cparams(("arbitrary",)),
    )(*args)
    return res if with_da else res[0]


def _adamw(w, g, m, v, name):
    r, c = w.shape
    tr = r
    for cand in (512, 256, 128, 64, 32, 16, 8):
        if r % cand == 0 and cand * c * 4 <= 2 * 1024 * 1024:
            tr = cand
            break

    def body(w_ref, g_ref, m_ref, v_ref, d_ref, nm_ref, nv_ref):
        gv = g_ref[...]
        nm = ADAM_B1 * m_ref[...] + (1.0 - ADAM_B1) * gv
        nv = ADAM_B2 * v_ref[...] + (1.0 - ADAM_B2) * (gv * gv)
        m_hat = nm / (1.0 - ADAM_B1 ** ADAM_STEP)
        v_hat = nv / (1.0 - ADAM_B2 ** ADAM_STEP)
        d_ref[...] = -ADAM_LR * (m_hat / (jnp.sqrt(v_hat) + ADAM_EPS) + ADAM_WD * w_ref[...])
        nm_ref[...] = nm
        nv_ref[...] = nv

    spec = pl.BlockSpec((tr, c), lambda i: (i, 0))
    sds = jax.ShapeDtypeStruct((r, c), F32)
    return pl.pallas_call(body, name=name, grid=(r // tr,), in_specs=[spec] * 4, out_specs=[spec] * 3,
                          out_shape=[sds] * 3, compiler_params=_cparams(("parallel",)))(w, g, m, v)


def _sum_lead(x, name, out_dtype=F32):
    n, r, c = x.shape
    tr = r
    for cand in (512, 256, 128, 64, 32, 16, 8):
        if r % cand == 0 and n * cand * c * 4 <= 4 * 1024 * 1024:
            tr = cand
            break

    def body(x_ref, o_ref):
        acc = x_ref[0].astype(F32)
        for k in range(1, n):
            acc = acc + x_ref[k].astype(F32)
        o_ref[...] = acc.astype(o_ref.dtype)

    return pl.pallas_call(body, name=name, grid=(r // tr,),
                          in_specs=[pl.BlockSpec((n, tr, c), lambda i: (0, i, 0))],
                          out_specs=pl.BlockSpec((tr, c), lambda i: (i, 0)),
                          out_shape=jax.ShapeDtypeStruct((r, c), out_dtype),
                          compiler_params=_cparams(("parallel",)))(x)


def _row_tile(rows, row_bytes, budget, least=8):
    for cand in (1024, 512, 256, 128, 64, 32, 16, 8):
        if cand >= least and rows % cand == 0 and cand * row_bytes <= budget:
            return cand
    return rows


def _cast_into_slot(w, slot, name):
    r, c = w.shape
    tr = _row_tile(r, c * 4, 4 * 1024 * 1024, least=16)

    def body(slot_ref, w_ref, o_ref):
        o_ref[...] = w_ref[...].astype(o_ref.dtype)

    gs = pltpu.PrefetchScalarGridSpec(
        num_scalar_prefetch=1, grid=(r // tr,),
        in_specs=[pl.BlockSpec((tr, c), lambda i, s: (i, 0))],
        out_specs=pl.BlockSpec((None, tr, c), lambda i, s: (s[0], i, 0)))
    return pl.pallas_call(body, name=name, grid_spec=gs, out_shape=jax.ShapeDtypeStruct((N_CHIPS, r, c), BF16),
                          compiler_params=_cparams(("parallel",)))(slot, w)


def _sum_own(p, t, sel, name):
    _, h, c = p.shape
    tr = _row_tile(h, c * 4, 2 * 1024 * 1024, least=16)
    nblk = h // tr

    def body(sel_ref, p_ref, t_ref, o_ref):
        acc = p_ref[...].astype(F32)
        for k in range(3):
            acc = acc + t_ref[k].astype(F32)
        o_ref[...] = acc

    gs = pltpu.PrefetchScalarGridSpec(
        num_scalar_prefetch=1, grid=(nblk,),
        in_specs=[pl.BlockSpec((None, tr, c), lambda i, s: (s[0], i, 0)),
                  pl.BlockSpec((3, tr, c), lambda i, s: (0, i, 0))],
        out_specs=pl.BlockSpec((tr, c), lambda i, s: (s[1] * nblk + i, 0)))
    return pl.pallas_call(body, name=name, grid_spec=gs, out_shape=jax.ShapeDtypeStruct((2 * h, c), F32),
                          compiler_params=_cparams(("parallel",)))(sel, p, t)


def _add_half(g, t, half, name):
    n, r, c = g.shape
    h = r // 2
    tr = h
    for cand in (512, 256, 128, 64, 32, 16):
        if h % cand == 0 and cand * c * 2 <= 2 * 1024 * 1024:
            tr = cand
            break
    nblk = h // tr

    def body(half_ref, g_ref, t_ref, o_ref):
        o_ref[...] = (g_ref[...].astype(F32) + t_ref[...].astype(F32)).astype(o_ref.dtype)

    gs = pltpu.PrefetchScalarGridSpec(
        num_scalar_prefetch=1, grid=(n, nblk),
        in_specs=[pl.BlockSpec((None, tr, c), lambda j, i, hr: (j, hr[0] * nblk + i, 0)),
                  pl.BlockSpec((None, tr, c), lambda j, i, hr: (j, i, 0))],
        out_specs=pl.BlockSpec((None, tr, c), lambda j, i, hr: (j, i, 0)))
    return pl.pallas_call(body, name=name, grid_spec=gs, out_shape=jax.ShapeDtypeStruct((n, h, c), BF16),
                          compiler_params=_cparams(("parallel", "parallel")))(half, g, t)


def _position():
    x, y, c = lax.axis_index("x"), lax.axis_index("y"), lax.axis_index("c")
    return x, y, c


def _allgather8(xs, name):
    m_per, n = xs.shape

    def body(x_ref, out_ref, send_sems, recv_sems, local_sem):
        x, y, c = _position()
        me, sibling = (x, y, c), (x, y, 1 - c)
        chips = [(1 - x, y), (x, 1 - y), (1 - x, 1 - y)]

        def rows(px, py, pc):
            return out_ref.at[pl.ds((4 * px + 2 * py + pc) * m_per, m_per), :]

        def copy(k, block, to, src=None):
            return pltpu.make_async_remote_copy(
                src_ref=rows(*block) if src is None else src, dst_ref=rows(*block),
                send_sem=send_sems.at[k], recv_sem=recv_sems.at[k], device_id=to, device_id_type=MESH)

        mine = pltpu.make_async_copy(x_ref, rows(*me), local_sem)
        mine.start()
        first = [copy(0, me, sibling, src=x_ref)]
        first += [copy(1 + j, me, (*chip, c), src=x_ref) for j, chip in enumerate(chips)]
        for cp in first:
            cp.start()
        passed = [copy(4 + j, (*chip, c), sibling) for j, chip in enumerate(chips)]
        for j, chip in enumerate(chips):
            copy(1 + j, (*chip, c), me).wait_recv()
            passed[j].start()
        copy(0, sibling, me).wait_recv()
        for j, chip in enumerate(chips):
            copy(4 + j, (*chip, 1 - c), me).wait_recv()
        for cp in first + passed:
            cp.wait_send()
        mine.wait()

    return pl.pallas_call(
        body, name=name, out_shape=jax.ShapeDtypeStruct((N_DEV * m_per, n), xs.dtype),
        in_specs=[pl.BlockSpec(memory_space=pltpu.VMEM)], out_specs=pl.BlockSpec(memory_space=pltpu.VMEM),
        scratch_shapes=[pltpu.SemaphoreType.DMA((7,)), pltpu.SemaphoreType.DMA((7,)), pltpu.SemaphoreType.DMA],
        compiler_params=pltpu.CompilerParams(vmem_limit_bytes=VMEM_LIMIT_BYTES),
    )(xs)


_HBM = pl.BlockSpec(memory_space=pltpu.HBM)


_SEM = pl.BlockSpec(memory_space=pltpu.SEMAPHORE)
_ANY = pl.BlockSpec(memory_space=pl.ANY)
_EFFECT = pltpu.SideEffectType.DATAFLOW_SIDE_EFFECTING


def _in_hbm(a):
    return pltpu.with_memory_space_constraint(a, pltpu.HBM)


def _gather_start(ws, groups, name):
    n = len(ws)

    def body(*refs):
        in_refs = refs[:n]
        sems, token = refs[2 * n:-1], refs[-1]
        x, y, c = _position()
        mychip = 2 * x + y
        chips = [(1 - x, y), (x, 1 - y), (1 - x, 1 - y)]
        for g, members in enumerate(groups):
            for k, i in enumerate(members):
                h = ws[i].shape[1] // 2
                mine = in_refs[i].at[mychip, pl.ds(c * h, h), :]
                for j, (px, py) in enumerate(chips):
                    pltpu.make_async_remote_copy(
                        src_ref=mine, dst_ref=mine, send_sem=sems[2 * g].at[3 * k + j],
                        recv_sem=sems[2 * g + 1].at[3 * k + j], device_id=(px, py, c), device_id_type=MESH).start()
        token[...] = jnp.zeros_like(token)

    sem_shapes = [pltpu.SemaphoreType.DMA((3 * len(m),)) for m in groups for _ in range(2)]
    res = pl.pallas_call(
        body, name=name,
        out_shape=[pltpu.HBM(w.shape, w.dtype) for w in ws] + sem_shapes + [jax.ShapeDtypeStruct((8, 128), F32)],
        in_specs=[_HBM] * n,
        out_specs=[_HBM] * n + [_SEM] * len(sem_shapes) + [pl.BlockSpec(memory_space=pltpu.VMEM)],
        input_output_aliases={i: i for i in range(n)},
        compiler_params=pltpu.CompilerParams(has_side_effects=_EFFECT),
    )(*[_in_hbm(w) for w in ws])
    bufs, sems, token = res[:n], res[n:-1], res[-1]
    return list(bufs), [(sems[2 * g], sems[2 * g + 1]) for g in range(len(groups))], token


def _gather_wait(bufs, send_sems, recv_sems, after, name):
    m = len(bufs)

    def body(*refs):
        in_refs = refs[:m]
        send, recv = refs[m], refs[m + 1]
        x, y, c = _position()
        mychip = 2 * x + y
        chips = [(1 - x, y), (x, 1 - y), (1 - x, 1 - y)]
        for k in range(m):
            h = bufs[k].shape[1] // 2
            mine = in_refs[k].at[mychip, pl.ds(c * h, h), :]
            for j, (px, py) in enumerate(chips):
                cp = pltpu.make_async_remote_copy(
                    src_ref=mine, dst_ref=in_refs[k].at[2 * px + py, pl.ds(c * h, h), :],
                    send_sem=send.at[3 * k + j], recv_sem=recv.at[3 * k + j],
                    device_id=(px, py, c), device_id_type=MESH)
                cp.wait_send()
                cp.wait_recv()

    res = pl.pallas_call(
        body, name=name, out_shape=[pltpu.HBM(b.shape, b.dtype) for b in bufs],
        in_specs=[_HBM] * m + [_SEM, _SEM, _ANY], out_specs=[_HBM] * m,
        input_output_aliases={k: k for k in range(m)},
        compiler_params=pltpu.CompilerParams(has_side_effects=_EFFECT),
    )(*bufs, send_sems, recv_sems, after)
    return list(res)


def _forward_halves(ws, name):
    n = len(ws)

    def body(*refs):
        out_refs = refs[n:2 * n]
        send_sems, recv_sems = refs[2 * n:]
        x, y, c = _position()
        me, sibling = (x, y, c), (x, y, 1 - c)
        chips = [(1 - x, y), (x, 1 - y), (1 - x, 1 - y)]
        cps = []
        for i in range(n):
            h = ws[i].shape[1] // 2
            for j, (px, py) in enumerate(chips):
                got = out_refs[i].at[2 * px + py, pl.ds(c * h, h), :]
                cp = pltpu.make_async_remote_copy(
                    src_ref=got, dst_ref=got, send_sem=send_sems.at[3 * i + j], recv_sem=recv_sems.at[3 * i + j],
                    device_id=sibling, device_id_type=MESH)
                cp.start()
                cps.append(cp)
        for i in range(n):
            h = ws[i].shape[1] // 2
            for j, (px, py) in enumerate(chips):
                other = out_refs[i].at[2 * px + py, pl.ds((1 - c) * h, h), :]
                pltpu.make_async_remote_copy(
                    src_ref=other, dst_ref=other, send_sem=send_sems.at[3 * i + j], recv_sem=recv_sems.at[3 * i + j],
                    device_id=me, device_id_type=MESH).wait_recv()
        for cp in cps:
            cp.wait_send()

    return pl.pallas_call(
        body, name=name,
        out_shape=[jax.ShapeDtypeStruct(w.shape, w.dtype) for w in ws],
        in_specs=[_HBM] * n, out_specs=[_HBM] * n, input_output_aliases={i: i for i in range(n)},
        scratch_shapes=[pltpu.SemaphoreType.DMA((3 * n,)), pltpu.SemaphoreType.DMA((3 * n,))],
    )(*ws)


def _swap_halves(gs, name):
    n = len(gs)

    def body(*refs):
        in_refs, out_refs = refs[:n], refs[n:2 * n]
        send_sems, recv_sems = refs[2 * n:]
        x, y, c = _position()
        cps = []
        for i in range(n):
            h = gs[i].shape[1] // 2
            cp = pltpu.make_async_remote_copy(
                src_ref=in_refs[i].at[:, pl.ds((1 - c) * h, h), :], dst_ref=out_refs[i],
                send_sem=send_sems.at[i], recv_sem=recv_sems.at[i], device_id=(x, y, 1 - c), device_id_type=MESH)
            cp.start()
            cps.append(cp)
        for cp in cps:
            cp.wait()

    return pl.pallas_call(
        body, name=name,
        out_shape=[jax.ShapeDtypeStruct((g.shape[0], g.shape[1] // 2, g.shape[2]), g.dtype) for g in gs],
        in_specs=[_HBM] * n, out_specs=[_HBM] * n,
        scratch_shapes=[pltpu.SemaphoreType.DMA((n,)), pltpu.SemaphoreType.DMA((n,))],
    )(*gs)


def _scatter_copies(p_refs, land_refs, send, recv):
    x, y, c = _position()
    chips = [(1 - x, y), (x, 1 - y), (1 - x, 1 - y)]
    return [pltpu.make_async_remote_copy(
        src_ref=p_refs[i].at[2 * px + py], dst_ref=land_refs[i].at[j],
        send_sem=send.at[3 * i + j], recv_sem=recv.at[3 * i + j], device_id=(px, py, c), device_id_type=MESH)
        for i in range(len(p_refs)) for j, (px, py) in enumerate(chips)]


def _scatter_start(ps, name):
    n = len(ps)
    lands = [lax.empty((3,) + p.shape[1:], p.dtype) for p in ps]

    def body(*refs):
        for cp in _scatter_copies(refs[:n], refs[n:2 * n], refs[4 * n], refs[4 * n + 1]):
            cp.start()
        refs[4 * n + 2][...] = jnp.zeros_like(refs[4 * n + 2])

    res = pl.pallas_call(
        body, name=name,
        out_shape=[pltpu.HBM(a.shape, a.dtype) for a in list(ps) + lands]
        + [pltpu.SemaphoreType.DMA((3 * n,)), pltpu.SemaphoreType.DMA((3 * n,)), jax.ShapeDtypeStruct((8, 128), F32)],
        in_specs=[_HBM] * (2 * n),
        out_specs=[_HBM] * (2 * n) + [_SEM, _SEM, pl.BlockSpec(memory_space=pltpu.VMEM)],
        input_output_aliases={i: i for i in range(2 * n)},
        compiler_params=pltpu.CompilerParams(has_side_effects=_EFFECT),
    )(*[_in_hbm(a) for a in list(ps) + lands])
    return list(res[:n]), list(res[n:2 * n]), res[2 * n], res[2 * n + 1], res[2 * n + 2]


def _scatter_wait(ps, lands, send_sems, recv_sems, after, name):
    n = len(ps)

    def body(*refs):
        for cp in _scatter_copies(refs[:n], refs[n:2 * n], refs[2 * n], refs[2 * n + 1]):
            cp.wait_send()
            cp.wait_recv()

    res = pl.pallas_call(
        body, name=name, out_shape=[pltpu.HBM(a.shape, a.dtype) for a in list(ps) + list(lands)],
        in_specs=[_HBM] * (2 * n) + [_SEM, _SEM, _ANY], out_specs=[_HBM] * (2 * n),
        input_output_aliases={i: i for i in range(2 * n)},
        compiler_params=pltpu.CompilerParams(has_side_effects=_EFFECT),
    )(*ps, *lands, send_sems, recv_sems, after)
    return list(res[:n]), list(res[n:])


def _join_halves(rs, name):
    n = len(rs)

    def body(*refs):
        out_refs = refs[n:2 * n]
        send_sems, recv_sems = refs[2 * n:]
        x, y, c = _position()
        cps = []
        for i in range(n):
            h = rs[i].shape[0] // 2
            mine = out_refs[i].at[pl.ds(c * h, h), :]
            cp = pltpu.make_async_remote_copy(
                src_ref=mine, dst_ref=mine, send_sem=send_sems.at[i], recv_sem=recv_sems.at[i],
                device_id=(x, y, 1 - c), device_id_type=MESH)
            cp.start()
            cps.append(cp)
        for i in range(n):
            h = rs[i].shape[0] // 2
            other = out_refs[i].at[pl.ds((1 - c) * h, h), :]
            pltpu.make_async_remote_copy(
                src_ref=other, dst_ref=other, send_sem=send_sems.at[i], recv_sem=recv_sems.at[i],
                device_id=(x, y, c), device_id_type=MESH).wait_recv()
        for cp in cps:
            cp.wait_send()

    return pl.pallas_call(
        body, name=name,
        out_shape=[jax.ShapeDtypeStruct(r.shape, r.dtype) for r in rs],
        in_specs=[_HBM] * n, out_specs=[_HBM] * n, input_output_aliases={i: i for i in range(n)},
        scratch_shapes=[pltpu.SemaphoreType.DMA((n,)), pltpu.SemaphoreType.DMA((n,))],
    )(*rs)


def _t5_buckets_block():
    qi = np.arange(BLOCK)[:, None]
    ki = np.arange(2 * BLOCK)[None, :]
    n = np.maximum(qi + BLOCK - ki, 0)
    max_exact = NUM_BUCKETS // 2
    large = max_exact + (np.log(np.maximum(n, 1) / max_exact) / np.log(MAX_DISTANCE / max_exact)
                         * (NUM_BUCKETS - max_exact)).astype(np.int32)
    large = np.minimum(large, NUM_BUCKETS - 1)
    return np.where(n < max_exact, n, large).astype(np.int32)


def _discretise(lambda_re, lambda_im, log_step, b_re, b_im):
    lam_re = jnp.minimum(lambda_re, -1e-4)
    lam_im = lambda_im
    delta = jnp.exp(log_step)[:, None]
    mag = jnp.exp(lam_re * delta)
    ang = lam_im * delta
    abar_re, abar_im = mag * jnp.cos(ang), mag * jnp.sin(ang)
    num_re, num_im = abar_re - 1.0, abar_im
    den = lam_re * lam_re + lam_im * lam_im
    f_re = (num_re * lam_re + num_im * lam_im) / den
    f_im = (num_im * lam_re - num_re * lam_im) / den
    bbar_re = f_re[..., None] * b_re - f_im[..., None] * b_im
    bbar_im = f_re[..., None] * b_im + f_im[..., None] * b_re
    return abar_re, abar_im, bbar_re, bbar_im


def _interleave(v, nc):
    s, w = v.shape
    return v.reshape(nc, s // nc, w).transpose(1, 0, 2).reshape(s, w)


def _deinterleave(v, nc):
    s, w = v.shape
    return v.reshape(s // nc, nc, w).transpose(1, 0, 2).reshape(s, w)


_SMALL = ("norm1_g", "b_in", "attn_sinks", "rel_bias", "lambda_re", "lambda_im", "log_step", "ssm_b_re",
          "ssm_b_im", "ssm_c_re", "ssm_c_im", "ssm_d", "b_glu", "norm2_g", "final_g")


def _pack(parts):
    rows = []
    for p in parts:
        f = p.reshape(-1).astype(F32)
        pad = (-f.shape[0]) % 128
        rows.append(jnp.pad(f, (0, pad)).reshape(-1, 128))
    out = jnp.concatenate(rows, axis=0)
    pad = (-out.shape[0]) % 256
    return jnp.pad(out, ((0, pad), (0, 0)))


def _unpack(packed, shapes):
    res, r = [], 0
    for shp in shapes:
        size = int(np.prod(shp))
        nr = -(-size // 128)
        res.append(packed[r:r + nr].reshape(-1)[:size].reshape(shp))
        r += nr
    return res


def kernel(x, c, w_ada, b_ada, norm1_g, w_in, b_in, attn_sinks, rel_bias, lambda_re, lambda_im, log_step, ssm_b_re, ssm_b_im, ssm_c_re, ssm_c_im, ssm_d, w_glu, b_glu, w_attn_proj, w_ssm_proj, w_out, norm2_g, w_ff1, w_ff2, final_g, loss_target, m_w_ada, m_b_ada, m_norm1_g, m_w_in, m_b_in, m_attn_sinks, m_rel_bias, m_lambda_re, m_lambda_im, m_log_step, m_ssm_b_re, m_ssm_b_im, m_ssm_c_re, m_ssm_c_im, m_ssm_d, m_w_glu, m_b_glu, m_w_attn_proj, m_w_ssm_proj, m_w_out, m_norm2_g, m_w_ff1, m_w_ff2, m_final_g, v_w_ada, v_b_ada, v_norm1_g, v_w_in, v_b_in, v_attn_sinks, v_rel_bias, v_lambda_re, v_lambda_im, v_log_step, v_ssm_b_re, v_ssm_b_im, v_ssm_c_re, v_ssm_c_im, v_ssm_d, v_w_glu, v_b_glu, v_w_attn_proj, v_w_ssm_proj, v_w_out, v_norm2_g, v_w_ff1, v_w_ff2, v_final_g):
    given = dict(locals())
    S, D = x.shape[1], x.shape[2]
    SSM_W = w_glu.shape[2]
    G = SSM_W // SSM_GROUP_CH
    NST = G * SSM_STATE
    DFF = w_ff2.shape[1] * N_CHIPS
    INW = w_in.shape[2] * N_CHIPS
    o_q, o_k, o_v, o_u = 0, ATTN_WIDTH, ATTN_WIDTH + KV_WIDTH, ATTN_WIDTH + 2 * KV_WIDTH
    o_ga, o_gs = o_u + SSM_W, o_u + SSM_W + D
    mx, my, mc = _position()
    my_chip = 2 * mx + my
    my_b = 4 * mx + 2 * my + mc

    xv, tgt = x[0], loss_target[0]

    big = dict(w_in=w_in[0], w_glu=w_glu[0], w_attn_proj=w_attn_proj[0], w_ssm_proj=w_ssm_proj[0],
               w_out=w_out[0], w_ff1=w_ff1[0], w_ff2=w_ff2[0])
    big_names = list(big)
    colsharded = {"w_in", "w_attn_proj", "w_ssm_proj", "w_ff1"}
    chip_sel = my_chip.astype(jnp.int32).reshape(1)
    gather_groups = [["w_in"], ["w_attn_proj", "w_ssm_proj", "w_glu", "w_out"], ["w_ff1", "w_ff2"]]
    first = [_cast_into_slot(big["w_in"], chip_sel, "cast_w_in")]
    first, sems_first, token_first = _gather_start(first, [[0]], "gather_start_in")
    rest_names = gather_groups[1] + gather_groups[2]
    rest = [_cast_into_slot(big[k], chip_sel, "cast_" + k) for k in rest_names]
    rest, sems_rest, token_rest = _gather_start(
        rest, [[rest_names.index(k) for k in grp] for grp in gather_groups[1:]], "gather_start_rest")
    in_flight = dict(zip(["w_in"] + rest_names, first + rest))
    gather_sems = sems_first + sems_rest
    gathered = {}

    def finish_gather(g, after):
        bufs = [in_flight[k] for k in gather_groups[g]]
        bufs = _gather_wait(bufs, gather_sems[g][0], gather_sems[g][1], after, "gather_wait_%d" % g)
        gathered.update(zip(gather_groups[g], _forward_halves(bufs, "gather_forward_%d" % g)))

    def tied(v, token):
        return v + token[0:1, 0:1]

    def all_of(*arrays):
        return jnp.stack([a.reshape(-1)[0].astype(F32) for a in arrays])

    def wop(k):
        g = gathered[k]
        return _Op(g, N_CHIPS) if k in colsharded else _Op(g.reshape(g.shape[0] * g.shape[1], g.shape[2]))

    grads = {}
    half = mc.astype(jnp.int32).reshape(1)
    sel = jnp.stack([my_chip, mc]).astype(jnp.int32)

    def rs_begin(tag, named):
        keys, gl = list(named), []
        for k in keys:
            gk = named[k]
            if k not in colsharded:
                gk = gk.reshape(N_CHIPS, gk.shape[0] // N_CHIPS, gk.shape[1])
            gl.append(gk)
        t1 = _swap_halves(gl, "rs_swap_" + tag)
        ps = [_add_half(g, t, half, "rs_add_" + k) for g, t, k in zip(gl, t1, keys)]
        ps, lands, ssem, rsem, token = _scatter_start(ps, "rs_start_" + tag)
        return (keys, ps, lands, ssem, rsem), token

    def rs_end(tag, state, after):
        keys, ps, lands, ssem, rsem = state
        ps, lands = _scatter_wait(ps, lands, ssem, rsem, after, "rs_wait_" + tag)
        rs = [_sum_own(p, t, sel, "rs_sum_" + k) for p, t, k in zip(ps, lands, keys)]
        full = _join_halves(rs, "rs_join_" + tag)
        for k, f in zip(keys, full):
            grads[k] = f[None]
        return full[-1]

    c_all = _allgather8(jnp.pad(tied(c, token_first), ((0, 7), (0, 0))), "gather_c").reshape(N_DEV, 8, D)[:, 0]
    c16 = jnp.pad(c_all, ((0, 8), (0, 0)))
    b_ada_mine = lax.dynamic_slice(b_ada.reshape(N_CHIPS, -1), (my_chip, 0), (1, w_ada.shape[2]))
    mod_sh = _mm(c16, w_ada[0], "NN", name="mod", M=16, N=w_ada.shape[2], K=D, a_fn=_silu,
                 epilogue=lambda acc, b: (acc + b,), extras=[(b_ada_mine, "row")])
    mod_all = _allgather8(mod_sh[:8], "gather_mod").reshape(N_DEV, 8, -1)
    mod_row = jnp.concatenate(
        [lax.dynamic_slice(mod_all, (2 * j, my_b, 0), (1, 1, mod_all.shape[2]))[0] for j in range(N_CHIPS)], axis=1)
    sh1, sc1, g1, sh2, sc2, g2 = [mod_row[:, i * D:(i + 1) * D] for i in range(6)]

    disc_in = (lambda_re[0], lambda_im[0], log_step[0], ssm_b_re[0], ssm_b_im[0])
    (abar_re, abar_im, bbar_re, bbar_im), disc_vjp = jax.vjp(_discretise, *disc_in)
    eye = jnp.eye(G, dtype=F32)
    bd = jnp.concatenate([jnp.einsum("gnp,gh->gphn", bb, eye).reshape(SSM_W, NST) for bb in (bbar_re, bbar_im)], axis=1)
    cd = jnp.concatenate([jnp.einsum("gpn,gh->gnhp", cc, eye).reshape(NST, SSM_W)
                          for cc in (ssm_c_re[0], -ssm_c_im[0])], axis=0)
    a_fwd = jnp.stack([abar_re.reshape(1, NST), abar_im.reshape(1, NST)])
    a_bwd = jnp.stack([abar_re.reshape(1, NST), -abar_im.reshape(1, NST)])
    d_row = ssm_d

    buckets = _t5_buckets_block()
    onehot = (jnp.asarray(buckets.reshape(-1, 1)) == jnp.arange(128, dtype=jnp.int32)[None, :]).astype(BF16)
    rb_hi = rel_bias.astype(BF16)
    rb_lo = (rel_bias - rb_hi.astype(F32)).astype(BF16)
    rb_lo2 = (rel_bias - rb_hi.astype(F32) - rb_lo.astype(F32)).astype(BF16)
    rb3 = jnp.pad(jnp.concatenate([rb_hi, rb_lo, rb_lo2], axis=1),
                  ((0, 128 - NUM_BUCKETS), (0, 128 - 3 * N_Q_HEADS)))
    b3 = _mm(onehot, rb3, "NN", name="rel_bias_rows", M=BLOCK * 2 * BLOCK, N=128, K=128)
    bias = (b3[:, :N_Q_HEADS] + b3[:, N_Q_HEADS:2 * N_Q_HEADS]) + b3[:, 2 * N_Q_HEADS:3 * N_Q_HEADS]
    bias = jnp.transpose(bias.reshape(BLOCK, 2 * BLOCK, N_Q_HEADS), (2, 0, 1))
    sinks_b = jnp.broadcast_to(attn_sinks[0][:, None, None], (N_Q_HEADS, BLOCK, 128)).reshape(N_Q_HEADS * BLOCK, 128)

    h1 = _rowwise(_norm_mod, [(xv, "tile", D), (tied(norm1_g, token_rest), "row", D), (sh1, "row", D), (sc1, "row", D)],
                  [(D, BF16)], [], name="norm1", rows=S)[0]
    finish_gather(0, all_of(h1, bd, cd, a_fwd, a_bwd, bias, sinks_b))
    proj = _mm(h1, wop("w_in"), "NN", name="proj", M=S, N=INW, K=D,
               epilogue=lambda acc, b: (acc + b,), extras=[(b_in, "row")])

    def heads(v2d, nh):
        return v2d.reshape(S, nh, HEAD_DIM).transpose(1, 0, 2)

    def unheads(v3d):
        return v3d.transpose(1, 0, 2).reshape(S, -1)

    qh = heads(proj[:, o_q:o_k], N_Q_HEADS)
    kh = heads(proj[:, o_k:o_v], N_KV_HEADS)
    vh = heads(proj[:, o_v:o_u], N_KV_HEADS)
    attn = unheads(_attn_fwd(qh, kh, vh, sinks_b, bias, "attn_fwd"))
    finish_gather(1, attn)
    y_attn = _mm(attn, wop("w_attn_proj"), "NN", name="attn_proj", M=S, N=D, K=ATTN_WIDTH)

    u = proj[:, o_u:o_ga]
    u_il = _interleave(u, SCAN_CHUNKS)
    bu = _mm(u_il, bd, "NN", name="ssm_bu", M=S, N=2 * NST, K=SSM_W, out_nsh=2)
    xs = _scan(a_fwd, bu, None, reverse=False, name="scan_fwd", tc=256)
    y_il = _mm(_Op(xs, 2), cd, "NN", name="ssm_y", M=S, N=SSM_W, K=2 * NST,
               epilogue=lambda acc, uu, dd: (acc + dd * uu,), extras=[(u_il, "tile"), (d_row, "row")])
    y = _deinterleave(y_il, SCAN_CHUNKS)
    z0b = _rowwise(_gelu, [(y, "tile", SSM_W)], [(SSM_W, BF16)], [], name="gelu", rows=S)[0]
    z, t_glu = _mm(z0b, wop("w_glu"), "NN", name="glu", M=S, N=SSM_W, K=SSM_W, out_dtypes=(BF16, F32),
                   epilogue=lambda acc, b, yy: (_gelu(yy) * _sigmoid(acc + b), acc + b),
                   extras=[(b_glu, "row"), (y, "tile")])
    y_ssm = _mm(z, wop("w_ssm_proj"), "NN", name="ssm_proj", M=S, N=D, K=SSM_W)

    merged = _rowwise(_merge, [(_Op(proj, coff=o_ga), "tile", D), (_Op(proj, coff=o_gs), "tile", D),
                               (y_attn, "tile", D), (y_ssm, "tile", D)], [(D, BF16)], [], name="merge", rows=S)[0]
    mo, x2 = _mm(merged, wop("w_out"), "NN", name="out_proj", M=S, N=D, K=D, out_dtypes=(F32, F32),
                 epilogue=lambda acc, xx, gg: (acc, xx + gg * acc), extras=[(xv, "tile"), (g1, "row")])
    h2 = _rowwise(_norm_mod, [(x2, "tile", D), (norm2_g, "row", D), (sh2, "row", D), (sc2, "row", D)],
                  [(D, BF16)], [], name="norm2", rows=S)[0]
    finish_gather(2, h2)
    a_b, r_b = _mm(h2, wop("w_ff1"), "NN", name="ff1", M=S, N=DFF, K=D, out_dtypes=(BF16, BF16),
                   epilogue=lambda acc: (acc, jnp.square(jnp.maximum(acc, 0.0))))
    ff, x3 = _mm(r_b, wop("w_ff2"), "NN", name="ff2", M=S, N=D, K=DFF, out_dtypes=(F32, F32),
                 epilogue=lambda acc, xx, gg: (acc, xx + gg * acc), extras=[(x2, "tile"), (g2, "row")],
                 tj=1024, tk=1024)

    def final_fn(x3b, gf, tb):
        def f(xx, gg):
            yv = xx * lax.rsqrt(jnp.mean(xx * xx, axis=-1, keepdims=True) + EPS) * gg
            err = jnp.square(yv - tb)
            return 0.5 * jnp.sum(jnp.mean(err, axis=-1, keepdims=True), axis=0, keepdims=True)
        lv, vjp = jax.vjp(f, x3b, gf)
        dx, dg = vjp(jnp.ones((1, 1), F32))
        return dx, dg, jnp.broadcast_to(lv, (1, 128))

    dx3, g_final, loss_acc = _rowwise(final_fn, [(x3, "tile", D), (final_g.reshape(1, D), "row", D), (tgt, "tile", D)],
                                      [(D, F32)], [D, 128], name="final", rows=S)

    def ff_out_bwd(dx3b, ffb, g2b):
        return dx3b * g2b, jnp.sum(dx3b * ffb, axis=0, keepdims=True)

    dff, d_g2 = _rowwise(ff_out_bwd, [(dx3, "tile", D), (ff, "tile", D), (g2, "row", D)], [(D, BF16)], [D],
                         name="ff_out_bwd", rows=S)
    da = _mm(dff, wop("w_ff2"), "NT", name="ff2_dx", M=S, N=DFF, K=D, out_dtypes=(BF16,),
             epilogue=lambda acc, ab: (acc * (2.0 * jnp.maximum(ab.astype(F32), 0.0)),), extras=[(a_b, "tile")])
    g_w_ff2 = _mm(r_b, dff, "TN", name="ff2_dw", M=DFF, N=D, K=S, out_dtypes=(BF16,), tj=1024, tk=1024)
    dh2 = _mm(da, wop("w_ff1"), "NT", name="ff1_dx", M=S, N=D, K=DFF, tj=1024, tk=1024)
    g_w_ff1 = _mm(h2, da, "TN", name="ff1_dw", M=D, N=DFF, K=S, out_dtypes=(BF16,), out_nsh=N_CHIPS, tj=1024, tk=1024)
    rs_ff, token_ff = rs_begin("ff", dict(w_ff2=g_w_ff2, w_ff1=g_w_ff1))

    def norm2_bwd(x2b, dh2b, dx3b, mob, gn, shb, scb, g1b):
        _, vjp = jax.vjp(_norm_mod, x2b, gn, shb, scb)
        dx, dg, dsh, dsc = vjp(dh2b)
        dx2b = dx + dx3b
        return dx2b, dx2b * g1b, dg, dsh, dsc, jnp.sum(dx2b * mob, axis=0, keepdims=True)

    dx2, dmo, g_norm2, d_sh2, d_sc2, d_g1 = _rowwise(
        norm2_bwd, [(x2, "tile", D), (dh2, "tile", D), (dx3, "tile", D), (mo, "tile", D),
                    (tied(norm2_g, token_ff), "row", D), (sh2, "row", D), (sc2, "row", D), (g1, "row", D)],
        [(D, F32), (D, BF16)], [D, D, D, D], name="norm2_bwd", rows=S, tr=128)
    dmerged = _mm(dmo, wop("w_out"), "NT", name="out_dx", M=S, N=D, K=D)
    g_w_out = _mm(merged, dmo, "TN", name="out_dw", M=D, N=D, K=S, out_dtypes=(BF16,), tk=1024)

    def merge_bwd(gab, gsb, yab, ysb, dmb):
        _, vjp = jax.vjp(_merge, gab, gsb, yab, ysb)
        return vjp(dmb)

    d_ga, d_gs, dy_attn, dy_ssm = _rowwise(
        merge_bwd, [(_Op(proj, coff=o_ga), "tile", D), (_Op(proj, coff=o_gs), "tile", D), (y_attn, "tile", D),
                    (y_ssm, "tile", D), (dmerged, "tile", D)],
        [(D, BF16), (D, BF16), (D, BF16), (D, BF16)], [], name="merge_bwd", rows=S, tr=128)

    dattn = _mm(dy_attn, wop("w_attn_proj"), "NT", name="attn_proj_dx", M=S, N=ATTN_WIDTH, K=D)
    g_w_attn_proj = _mm(attn, dy_attn, "TN", name="attn_proj_dw", M=ATTN_WIDTH, N=D, K=S, out_dtypes=(BF16,),
                        out_nsh=N_CHIPS, tk=1024)
    dqh, dkh, dvh, dsink_blk, dbias = _attn_bwd(qh, kh, vh, heads(dattn, N_Q_HEADS), sinks_b, bias, "attn_bwd")
    g_sinks = _sum_lead(dsink_blk.reshape(N_Q_HEADS, BLOCK, 128).transpose(1, 0, 2), "sinks_dw")[:, 0].reshape(1, N_Q_HEADS)
    g_rel = _mm(dbias.reshape(N_Q_HEADS, -1), onehot, "NN", name="rel_bias_dw", M=N_Q_HEADS, N=128,
                K=BLOCK * 2 * BLOCK, tk=4096)
    g_rel_bias = g_rel[:, :NUM_BUCKETS].T

    dz = _mm(dy_ssm, wop("w_ssm_proj"), "NT", name="ssm_proj_dx", M=S, N=SSM_W, K=D)
    g_w_ssm_proj = _mm(z, dy_ssm, "TN", name="ssm_proj_dw", M=SSM_W, N=D, K=S, out_dtypes=(BF16,),
                       out_nsh=N_CHIPS, tk=1024)

    def glu_bwd(dzb, yb, tb):
        z0 = _gelu(yb)
        sg = _sigmoid(tb)
        dt = dzb * z0 * sg * (1.0 - sg)
        return dt, dzb * sg, jnp.sum(dt, axis=0, keepdims=True)

    dt_b, dz0a, g_b_glu = _rowwise(glu_bwd, [(dz, "tile", SSM_W), (y, "tile", SSM_W), (t_glu, "tile", SSM_W)],
                                   [(SSM_W, BF16), (SSM_W, F32)], [SSM_W], name="glu_bwd", rows=S)

    def gelu_bwd(acc, dz0ab, yb):
        _, vjp = jax.vjp(_gelu, yb)
        return (vjp(acc + dz0ab)[0],)

    dy = _mm(dt_b, wop("w_glu"), "NT", name="glu_dx", M=S, N=SSM_W, K=SSM_W, epilogue=gelu_bwd,
             extras=[(dz0a, "tile"), (y, "tile")])
    g_w_glu = _mm(z0b, dt_b, "TN", name="glu_dw", M=SSM_W, N=SSM_W, K=S, out_dtypes=(BF16,), tk=1024)
    rs_mix, token_mix = rs_begin("mix", dict(w_out=g_w_out, w_attn_proj=g_w_attn_proj, w_ssm_proj=g_w_ssm_proj,
                                             w_glu=g_w_glu))
    dy_il = _interleave(tied(dy, token_mix), SCAN_CHUNKS)
    dxs =_mm(dy_il, cd, "NT", name="ssm_dx", M=S, N=2 * NST, K=SSM_W, out_nsh=2)
    g_cd = _mm(_Op(xs, 2), dy_il, "TN", name="ssm_dc", M=2 * NST, N=SSM_W, K=S, tk=1024)
    lam, d_abar = _scan(a_bwd, dxs, xs, reverse=True, name="scan_bwd", tc=128)

    def du_fn(acc, dyb, dd):
        return (acc + dd * dyb,)

    du_il = _mm(_Op(lam, 2), bd, "NT", name="ssm_du", M=S, N=SSM_W, K=2 * NST, epilogue=du_fn,
                extras=[(dy_il, "tile"), (d_row, "row")])
    g_bd = _mm(u_il, _Op(lam, 2), "TN", name="ssm_db", M=SSM_W, N=2 * NST, K=S, tk=1024)
    g_ssm_d = _rowwise(lambda dyb, ub: (jnp.sum(dyb * ub, axis=0, keepdims=True),),
                       [(dy_il, "tile", SSM_W), (u_il, "tile", SSM_W)], [], [SSM_W], name="ssm_dd", rows=S)[0]
    du = _deinterleave(du_il, SCAN_CHUNKS)

    g_cd4 = g_cd.reshape(2, G, SSM_STATE, G, SSM_GROUP_CH)
    g_c_re = jnp.einsum("gnhp,gh->gpn", g_cd4[0], eye)
    g_c_im = -jnp.einsum("gnhp,gh->gpn", g_cd4[1], eye)
    g_bd4 = g_bd.reshape(G, SSM_GROUP_CH, 2, G, SSM_STATE)
    g_bbar_re = jnp.einsum("gphn,gh->gnp", g_bd4[:, :, 0], eye)
    g_bbar_im = jnp.einsum("gphn,gh->gnp", g_bd4[:, :, 1], eye)
    g_lre, g_lim, g_lstep, g_bre, g_bim = disc_vjp(
        (d_abar[0].reshape(G, SSM_STATE), d_abar[1].reshape(G, SSM_STATE), g_bbar_re, g_bbar_im))

    dproj = jnp.concatenate([unheads(dqh).astype(BF16), unheads(dkh).astype(BF16), unheads(dvh).astype(BF16),
                             du.astype(BF16), d_ga, d_gs], axis=1)
    g_w_in = _mm(h1, dproj, "TN", name="proj_dw", M=D, N=INW, K=S, out_dtypes=(BF16,), out_nsh=N_CHIPS,
                 tj=INW // (2 * N_CHIPS), tk=1024)
    rs_in, token_in = rs_begin("in", dict(w_in=g_w_in))
    dh1 = _mm(dproj, wop("w_in"), "NT", name="proj_dx", M=S, N=D, K=INW, tj=1024, tk=INW // N_CHIPS,
              epilogue=lambda acc, zero: (acc + zero,), extras=[(tied(jnp.zeros((1, D), F32), token_in), "row")])
    g_b_in = _rowwise(lambda d: (jnp.sum(d.astype(F32), axis=0, keepdims=True),), [(dproj, "tile", INW)], [], [INW],
                      name="proj_db", rows=S)[0]

    def norm1_bwd(xb, dhb, dresb, gn, shb, scb):
        _, vjp = jax.vjp(_norm_mod, xb, gn, shb, scb)
        dx, dg, dsh, dsc = vjp(dhb)
        return dx + dresb, dg, dsh, dsc

    grad_x, g_norm1, d_sh1, d_sc1 = _rowwise(
        norm1_bwd, [(xv, "tile", D), (dh1, "tile", D), (dx2, "tile", D), (norm1_g, "row", D), (sh1, "row", D),
                    (sc1, "row", D)], [(D, F32)], [D, D, D], name="norm1_bwd", rows=S)

    dmod_row = jnp.concatenate([d_sh1, d_sc1, d_g1, d_sh2, d_sc2, d_g2], axis=1)
    dmod_all = _allgather8(jnp.pad(dmod_row, ((0, 7), (0, 0))), "gather_dmod").reshape(N_DEV, 8, -1)[:, 0]
    g_b_ada = _sum_lead(dmod_all.reshape(N_DEV, -1, 128), "b_ada_dw").reshape(1, -1)
    dmod_mine = lax.dynamic_slice(dmod_all.reshape(N_DEV, N_CHIPS, -1), (0, my_chip, 0), (N_DEV, 1, w_ada.shape[2]))[:, 0]
    g_w_ada = _mm(c16, jnp.pad(dmod_mine, ((0, 8), (0, 0))), "TN", name="ada_dw", M=D, N=w_ada.shape[2], K=16,
                  a_fn=_silu)

    small_g = dict(norm1_g=g_norm1, b_in=g_b_in, attn_sinks=g_sinks, rel_bias=g_rel_bias, lambda_re=g_lre[None],
                   lambda_im=g_lim[None], log_step=g_lstep[None], ssm_b_re=g_bre[None], ssm_b_im=g_bim[None],
                   ssm_c_re=g_c_re[None], ssm_c_im=g_c_im[None], ssm_d=g_ssm_d, b_glu=g_b_glu, norm2_g=g_norm2,
                   final_g=g_final.reshape(D))
    packed = _pack([loss_acc[:, :1]] + [small_g[k] for k in _SMALL])
    rows = packed.shape[0]
    summed = _sum_lead(_allgather8(packed, "gather_small").reshape(N_DEV, rows, 128), "small_sum")
    small_shapes = [(1,)] + [given[k].shape for k in _SMALL]
    parts = _unpack(summed, small_shapes)
    loss = parts[0].reshape(())
    grads.update(zip(_SMALL, parts[1:]))
    grads["b_ada"] = g_b_ada
    grads["w_ada"] = g_w_ada[None]

    deltas, new_m, new_v = {}, {}, {}

    def adamw_big(k):
        d_, m_, v_ = _adamw(given[k][0], grads[k][0], given["m_" + k][0], given["v_" + k][0], "adamw_" + k)
        deltas[k], new_m[k], new_v[k] = d_[None], m_[None], v_[None]
        return v_

    rs_end("ff", rs_ff, summed)
    marks = [adamw_big(k) for k in ("w_ff2", "w_ff1")]
    rs_end("mix", rs_mix, all_of(*marks))
    marks = [adamw_big(k) for k in ("w_out", "w_attn_proj", "w_ssm_proj", "w_glu", "w_ada")]
    small_all = list(_SMALL) + ["b_ada"]
    shapes = [given[k].shape for k in small_all]
    pw, pg = _pack([given[k] for k in small_all]), _pack([grads[k] for k in small_all])
    pm, pv = _pack([given["m_" + k] for k in small_all]), _pack([given["v_" + k] for k in small_all])
    d_, m_, v_ = _adamw(pw, pg, pm, pv, "adamw_small")
    for k, dd, mm, vv in zip(small_all, _unpack(d_, shapes), _unpack(m_, shapes), _unpack(v_, shapes)):
        deltas[k], new_m[k], new_v[k] = dd, mm, vv
        grads[k] = grads[k].reshape(given[k].shape)
    rs_end("in", rs_in, all_of(v_, *marks))
    adamw_big("w_in")

    names = ["w_ada", "b_ada", "norm1_g", "w_in", "b_in", "attn_sinks", "rel_bias", "lambda_re", "lambda_im",
             "log_step", "ssm_b_re", "ssm_b_im", "ssm_c_re", "ssm_c_im", "ssm_d", "w_glu", "b_glu", "w_attn_proj",
             "w_ssm_proj", "w_out", "norm2_g", "w_ff1", "w_ff2", "final_g"]
    return (loss, grad_x[None], *[grads[n] for n in names], *[deltas[n] for n in names],
            *[new_m[n] for n in names], *[new_v[n] for n in names])
```

```python
import math

import numpy as np
import jax
import jax.numpy as jnp
from jax import lax
from jax.experimental import pallas as pl
from jax.experimental.pallas import tpu as pltpu

F32 = jnp.float32
BF16 = jnp.bfloat16
MESH = pl.DeviceIdType.MESH

HEAD_DIM = 64
N_Q_HEADS = 16
N_KV_HEADS = 4
GQA_GROUP = N_Q_HEADS // N_KV_HEADS
ATTN_WIDTH = N_Q_HEADS * HEAD_DIM
KV_WIDTH = N_KV_HEADS * HEAD_DIM
BLOCK = 128
NUM_BUCKETS = 32
MAX_DISTANCE = 128
NEG_INF = -1e30
SSM_GROUP_CH = 16
SSM_STATE = 64
EPS = 1e-6
ADAM_LR = 0.001
ADAM_B1 = 0.9
ADAM_B2 = 0.999
ADAM_EPS = 1e-08
ADAM_WD = 0.01
ADAM_STEP = 10

N_CHIPS = 4
N_DEV = 8
SCAN_CHUNKS = 8
VMEM_LIMIT_BYTES = 48 * 1024 * 1024


def _cparams(sem=None):
    return pltpu.CompilerParams(dimension_semantics=sem, vmem_limit_bytes=VMEM_LIMIT_BYTES)


class _Op:
    def __init__(self, arr, nsh=None, coff=0):
        self.arr, self.nsh, self.coff = arr, nsh, coff
        if nsh is None:
            self.rows, self.cols = arr.shape
        else:
            assert arr.shape[0] == nsh
            self.rows, self.cols = arr.shape[1], arr.shape[2] * nsh

    def spec(self, br, bc, idx):
        assert self.coff % bc == 0
        off = self.coff // bc
        if self.nsh is None:
            return pl.BlockSpec((br, bc), lambda *g: (idx(*g)[0], idx(*g)[1] + off))
        per = (self.cols // self.nsh) // bc
        assert per * bc * self.nsh == self.cols

        def imap(*g):
            r, c = idx(*g)
            c = c + off
            return (c // per, r, c % per)
        return pl.BlockSpec((None, br, bc), imap)


def _as_op(a):
    return a if isinstance(a, _Op) else _Op(a)


def _mm(a, b, mode, *, name, M, N, K, out_dtypes=(F32,), out_nsh=None, epilogue=None, extras=(),
        a_fn=None, ti=1024, tj=512, tk=2048, a_idx=None, b_idx=None):
    a, b = _as_op(a), _as_op(b)
    ti, tj, tk = min(ti, M), min(tj, N), min(tk, K)
    a_w = a.cols // a.nsh if a.nsh else None
    b_w = b.cols // b.nsh if b.nsh else None
    if a_w:
        ti, tk = (min(ti, a_w), tk) if mode == "TN" else (ti, min(tk, a_w))
    if b_w:
        tj, tk = (tj, min(tk, b_w)) if mode == "NT" else (min(tj, b_w), tk)
    if out_nsh:
        tj = min(tj, N // out_nsh)
    assert M % ti == 0 and N % tj == 0 and K % tk == 0, (name, M, N, K, ti, tj, tk)
    nk = K // tk
    if mode == "NN":
        a_spec = a.spec(ti, tk, a_idx or (lambda i, j, k: (i, k)))
        b_spec = b.spec(tk, tj, b_idx or (lambda i, j, k: (k, j)))
        dims = (((1,), (0,)), ((), ()))
    elif mode == "NT":
        a_spec = a.spec(ti, tk, a_idx or (lambda i, j, k: (i, k)))
        b_spec = b.spec(tj, tk, b_idx or (lambda i, j, k: (j, k)))
        dims = (((1,), (1,)), ((), ()))
    else:
        a_spec = a.spec(tk, ti, a_idx or (lambda i, j, k: (k, i)))
        b_spec = b.spec(tk, tj, b_idx or (lambda i, j, k: (k, j)))
        dims = (((0,), (0,)), ((), ()))
    ex_specs, ex_arrs = [], []
    for op, kind in extras:
        op = _as_op(op)
        if kind == "tile":
            ex_specs.append(op.spec(ti, tj, lambda i, j, k: (i, j)))
        else:
            ex_specs.append(op.spec(1, tj, lambda i, j, k: (0, j)))
        ex_arrs.append(op.arr)
    ne, no = len(ex_arrs), len(out_dtypes)
    if out_nsh is None:
        out_shapes = [jax.ShapeDtypeStruct((M, N), d) for d in out_dtypes]
        out_specs = [pl.BlockSpec((ti, tj), lambda i, j, k: (i, j)) for _ in out_dtypes]
    else:
        per = (N // out_nsh) // tj
        assert per * tj * out_nsh == N
        out_shapes = [jax.ShapeDtypeStruct((out_nsh, M, N // out_nsh), d) for d in out_dtypes]
        out_specs = [pl.BlockSpec((None, ti, tj), lambda i, j, k: (j // per, i, j % per)) for _ in out_dtypes]

    def body(a_ref, b_ref, *rest):
        ex_refs, out_refs, acc = rest[:ne], rest[ne:ne + no], rest[ne + no]
        k = pl.program_id(2)

        @pl.when(k == 0)
        def _():
            acc[...] = jnp.zeros_like(acc)

        av = a_ref[...]
        if a_fn is not None:
            av = a_fn(av)
        acc[...] += lax.dot_general(av.astype(BF16), b_ref[...].astype(BF16), dims,
                                    preferred_element_type=F32)

        @pl.when(k == nk - 1)
        def _():
            res = acc[...]
            outs = epilogue(res, *[r[...] for r in ex_refs]) if epilogue is not None else (res,)
            for o_ref, o in zip(out_refs, outs):
                o_ref[...] = o.astype(o_ref.dtype)

    outs = pl.pallas_call(
        body, name=name, grid=(M // ti, N // tj, nk),
        in_specs=[a_spec, b_spec] + ex_specs, out_specs=out_specs, out_shape=out_shapes,
        scratch_shapes=[pltpu.VMEM((ti, tj), F32)],
        compiler_params=_cparams(("parallel", "parallel", "arbitrary")),
    )(a.arr, b.arr, *ex_arrs)
    return outs[0] if no == 1 else outs


def _rowwise(fn, ins, outs, accs, *, name, rows, tr=256):
    tr = min(tr, rows)
    assert rows % tr == 0
    in_specs, arrs = [], []
    for op, kind, width in ins:
        op = _as_op(op)
        if kind == "tile":
            in_specs.append(op.spec(tr, width, lambda i: (i, 0)))
        else:
            in_specs.append(op.spec(op.rows, width, lambda i: (0, 0)))
        arrs.append(op.arr)
    ni, no, na = len(ins), len(outs), len(accs)
    out_shapes = [jax.ShapeDtypeStruct((rows, w), d) for w, d in outs]
    out_specs = [pl.BlockSpec((tr, w), lambda i: (i, 0)) for w, _ in outs]
    out_shapes += [jax.ShapeDtypeStruct((1, w), F32) for w in accs]
    out_specs += [pl.BlockSpec((1, w), lambda i: (0, 0)) for w in accs]

    def body(*refs):
        in_refs, out_refs, acc_refs = refs[:ni], refs[ni:ni + no], refs[ni + no:]
        res = fn(*[r[...] for r in in_refs])
        if not isinstance(res, (tuple, list)):
            res = (res,)
        for o_ref, r in zip(out_refs, res[:no]):
            o_ref[...] = r.astype(o_ref.dtype)
        if na:
            @pl.when(pl.program_id(0) == 0)
            def _():
                for a_ref in acc_refs:
                    a_ref[...] = jnp.zeros_like(a_ref)
            for a_ref, r in zip(acc_refs, res[no:]):
                a_ref[...] += r.astype(F32)

    res = pl.pallas_call(
        body, name=name, grid=(rows // tr,), in_specs=in_specs, out_specs=out_specs, out_shape=out_shapes,
        compiler_params=_cparams(("arbitrary",)),
    )(*arrs)
    return res


def _norm_mod(x, g, sh, sc):
    y = x * lax.rsqrt(jnp.mean(x * x, axis=-1, keepdims=True) + EPS) * g
    return y * (1.0 + sc) + sh


def _sigmoid(x):
    return 1.0 / (1.0 + jnp.exp(-x))


def _silu(x):
    return x * _sigmoid(x)


def _gelu(x):
    return 0.5 * x * (1.0 + jnp.tanh(math.sqrt(2.0 / math.pi) * (x + 0.044715 * (x * x * x))))


def _merge(ga, gs, ya, ys):
    return _sigmoid(ga) * ya + _sigmoid(gs) * ys


def _attn_head(q, kp, kc, vp, vc, sink, bias_p, bias_c, not_first):
    nt = (((1,), (1,)), ((), ()))
    nn = (((1,), (0,)), ((), ()))
    qb = q.astype(BF16)
    scale = HEAD_DIM ** -0.5
    sp = lax.dot_general(qb, kp.astype(BF16), nt, preferred_element_type=F32) * scale + bias_p
    sc = lax.dot_general(qb, kc.astype(BF16), nt, preferred_element_type=F32) * scale + bias_c
    qi = lax.broadcasted_iota(jnp.int32, sp.shape, 0) & (BLOCK - 1)
    ki = lax.broadcasted_iota(jnp.int32, sp.shape, 1)
    sp = jnp.where(jnp.logical_and(ki > qi, not_first), sp, NEG_INF)
    sc = jnp.where(ki <= qi, sc, NEG_INF)
    m = jnp.maximum(jnp.maximum(jnp.max(sp, axis=-1, keepdims=True), jnp.max(sc, axis=-1, keepdims=True)), sink)
    m = lax.stop_gradient(m)
    pp = jnp.exp(sp - m)
    pc = jnp.exp(sc - m)
    denom = jnp.sum(pp, axis=-1, keepdims=True) + jnp.sum(pc, axis=-1, keepdims=True) + jnp.exp(sink - m)
    o = lax.dot_general((pp / denom).astype(BF16), vp.astype(BF16), nn, preferred_element_type=F32)
    o = o + lax.dot_general((pc / denom).astype(BF16), vc.astype(BF16), nn, preferred_element_type=F32)
    return o


def _attn_fwd(qh, kh, vh, sinks, bias, name):
    s = qh.shape[1]
    nb = s // BLOCK
    G = GQA_GROUP
    R = G * BLOCK

    def body(q_ref, kp_ref, kc_ref, vp_ref, vc_ref, sink_ref, bias_ref, o_ref):
        not_first = pl.program_id(0) > 0
        for kv in range(N_KV_HEADS):
            hs = slice(kv * G, (kv + 1) * G)
            o = _attn_head(q_ref[hs].reshape(R, HEAD_DIM), kp_ref[kv], kc_ref[kv], vp_ref[kv], vc_ref[kv],
                           sink_ref[kv * R:(kv + 1) * R, 0:1],
                           bias_ref[hs, :, 0:BLOCK].reshape(R, BLOCK), bias_ref[hs, :, BLOCK:2 * BLOCK].reshape(R, BLOCK),
                           not_first)
            o_ref[hs] = o.reshape(G, BLOCK, HEAD_DIM).astype(o_ref.dtype)

    cur = lambda i: (0, i, 0)
    prev = lambda i: (0, jnp.maximum(i - 1, 0), 0)
    return pl.pallas_call(
        body, name=name, grid=(nb,),
        in_specs=[pl.BlockSpec((N_Q_HEADS, BLOCK, HEAD_DIM), cur),
                  pl.BlockSpec((N_KV_HEADS, BLOCK, HEAD_DIM), prev), pl.BlockSpec((N_KV_HEADS, BLOCK, HEAD_DIM), cur),
                  pl.BlockSpec((N_KV_HEADS, BLOCK, HEAD_DIM), prev), pl.BlockSpec((N_KV_HEADS, BLOCK, HEAD_DIM), cur),
                  pl.BlockSpec((N_Q_HEADS * BLOCK, 128), lambda i: (0, 0)),
                  pl.BlockSpec((N_Q_HEADS, BLOCK, 2 * BLOCK), lambda i: (0, 0, 0))],
        out_specs=pl.BlockSpec((N_Q_HEADS, BLOCK, HEAD_DIM), cur),
        out_shape=jax.ShapeDtypeStruct((N_Q_HEADS, s, HEAD_DIM), BF16),
        compiler_params=_cparams(("arbitrary",)),
    )(qh, kh, kh, vh, vh, sinks, bias)


def _attn_bwd(qh, kh, vh, doh, sinks, bias, name):
    s = qh.shape[1]
    nb = s // BLOCK
    G = GQA_GROUP
    R = G * BLOCK

    def body(q_ref, kp_ref, kc_ref, vp_ref, vc_ref, do_ref, sink_ref, bias_ref,
             dq_ref, dk_ref, dv_ref, dsink_ref, dbias_ref, ck, cv):
        i = pl.program_id(1)

        @pl.when(i == 0)
        def _():
            dsink_ref[...] = jnp.zeros_like(dsink_ref)
            dbias_ref[...] = jnp.zeros_like(dbias_ref)
            ck[...] = jnp.zeros_like(ck)
            cv[...] = jnp.zeros_like(cv)

        @pl.when(i < nb)
        def _():
            not_first = i > 0
            _, vjp = jax.vjp(lambda q, a, b, c, d, sk, e, f: _attn_head(q, a, b, c, d, sk, e, f, not_first),
                             q_ref[...].reshape(R, HEAD_DIM), kp_ref[...], kc_ref[...], vp_ref[...], vc_ref[...],
                             sink_ref[:, 0:1], bias_ref[:, :, 0:BLOCK].reshape(R, BLOCK),
                             bias_ref[:, :, BLOCK:2 * BLOCK].reshape(R, BLOCK))
            dq, dkp, dkc, dvp, dvc, dsk, dbp, dbc = vjp(do_ref[...].reshape(R, HEAD_DIM).astype(F32))
            dq_ref[...] = dq.reshape(G, BLOCK, HEAD_DIM)
            dsink_ref[...] += jnp.broadcast_to(dsk, (R, 128))
            dbias_ref[:, :, 0:BLOCK] += dbp.reshape(G, BLOCK, BLOCK)
            dbias_ref[:, :, BLOCK:2 * BLOCK] += dbc.reshape(G, BLOCK, BLOCK)
            dk_ref[...] = ck[...] + dkp
            dv_ref[...] = cv[...] + dvp
            ck[...] = dkc
            cv[...] = dvc

        @pl.when(i == nb)
        def _():
            dk_ref[...] = ck[...]
            dv_ref[...] = cv[...]

    qcur = lambda kv, i: (kv, jnp.minimum(i, nb - 1), 0)
    kcur = lambda kv, i: (kv, jnp.minimum(i, nb - 1), 0)
    kprev = lambda kv, i: (kv, jnp.clip(i - 1, 0, nb - 1), 0)
    qspec = pl.BlockSpec((G, BLOCK, HEAD_DIM), qcur)
    kc_spec = pl.BlockSpec((None, BLOCK, HEAD_DIM), kcur)
    kp_spec = pl.BlockSpec((None, BLOCK, HEAD_DIM), kprev)
    return pl.pallas_call(
        body, name=name, grid=(N_KV_HEADS, nb + 1),
        in_specs=[qspec, kp_spec, kc_spec, kp_spec, kc_spec, qspec,
                  pl.BlockSpec((R, 128), lambda kv, i: (kv, 0)),
                  pl.BlockSpec((G, BLOCK, 2 * BLOCK), lambda kv, i: (kv, 0, 0))],
        out_specs=[qspec, kp_spec, kp_spec,
                   pl.BlockSpec((R, 128), lambda kv, i: (kv, 0)),
                   pl.BlockSpec((G, BLOCK, 2 * BLOCK), lambda kv, i: (kv, 0, 0))],
        out_shape=[jax.ShapeDtypeStruct((N_Q_HEADS, s, HEAD_DIM), F32),
                   jax.ShapeDtypeStruct((N_KV_HEADS, s, HEAD_DIM), F32),
                   jax.ShapeDtypeStruct((N_KV_HEADS, s, HEAD_DIM), F32),
                   jax.ShapeDtypeStruct((N_Q_HEADS * BLOCK, 128), F32),
                   jax.ShapeDtypeStruct((N_Q_HEADS, BLOCK, 2 * BLOCK), F32)],
        scratch_shapes=[pltpu.VMEM((BLOCK, HEAD_DIM), F32), pltpu.VMEM((BLOCK, HEAD_DIM), F32)],
        compiler_params=_cparams(("arbitrary", "arbitrary")),
    )(qh, kh, kh, vh, vh, doh, sinks, bias)


def _cmul(ar, ai, br, bi):
    return ar * br - ai * bi, ar * bi + ai * br


def _scan(a, b, xs_prev, *, reverse, name, tc):
    _, s, c = b.shape
    nc = SCAN_CHUNKS
    steps = s // nc
    with_da = xs_prev is not None
    unroll = 8 if steps % 8 == 0 else 1

    def shift(v, d):
        row = lax.broadcasted_iota(jnp.int32, v.shape, 0)
        if reverse:
            return jnp.where(row < nc - d, pltpu.roll(v, nc - d, 0), 0.0)
        return jnp.where(row >= d, pltpu.roll(v, d, 0), 0.0)

    def body(*refs):
        if with_da:
            a_ref, b_ref, xp_ref, x_ref, da_ref = refs
        else:
            a_ref, b_ref, x_ref = refs
        ar = jnp.broadcast_to(a_ref[0], (nc, tc))
        ai = jnp.broadcast_to(a_ref[1], (nc, tc))

        def row_of(step):
            j = (steps - 1 - step) if reverse else step
            return pl.multiple_of(j * nc, nc)

        def p1(step, st):
            sr, si = st
            r0 = row_of(step)
            mr, mi = _cmul(ar, ai, sr, si)
            sr = mr + b_ref[0, pl.ds(r0, nc), :]
            si = mi + b_ref[1, pl.ds(r0, nc), :]
            x_ref[0, pl.ds(r0, nc), :] = sr
            x_ref[1, pl.ds(r0, nc), :] = si
            return sr, si
        zero = jnp.zeros((nc, tc), F32)
        er, ei = lax.fori_loop(0, steps, p1, (zero, zero), unroll=unroll)

        pr, pi_ = jnp.ones((nc, tc), F32), zero
        br, bi, left = ar, ai, steps
        while left:
            if left & 1:
                pr, pi_ = _cmul(pr, pi_, br, bi)
            br, bi = _cmul(br, bi, br, bi)
            left >>= 1
        cr, ci = shift(er, 1), shift(ei, 1)
        d = 1
        while d < nc:
            mr, mi = _cmul(pr, pi_, shift(cr, d), shift(ci, d))
            cr, ci = cr + mr, ci + mi
            pr, pi_ = _cmul(pr, pi_, pr, pi_)
            d *= 2

        def p2(step, st):
            qr, qi, dar, dai = st
            r0 = row_of(step)
            qr, qi = _cmul(ar, ai, qr, qi)
            fr, fi = _cmul(qr, qi, cr, ci)
            xr = x_ref[0, pl.ds(r0, nc), :] + fr
            xi = x_ref[1, pl.ds(r0, nc), :] + fi
            x_ref[0, pl.ds(r0, nc), :] = xr
            x_ref[1, pl.ds(r0, nc), :] = xi
            if with_da:
                jm = jnp.where(step == steps - 1, steps - 1, steps - 2 - step)
                rp = pl.multiple_of(jm * nc, nc)
                vr, vi = xp_ref[0, pl.ds(rp, nc), :], xp_ref[1, pl.ds(rp, nc), :]
                row = lax.broadcasted_iota(jnp.int32, (nc, tc), 0)
                first = step == steps - 1
                sel = jnp.logical_and(first, row == 0)
                vr = jnp.where(sel, 0.0, jnp.where(first, pltpu.roll(vr, 1, 0), vr))
                vi = jnp.where(sel, 0.0, jnp.where(first, pltpu.roll(vi, 1, 0), vi))
                dar = dar + xr * vr + xi * vi
                dai = dai + xi * vr - xr * vi
            return qr, qi, dar, dai
        _, _, dar, dai = lax.fori_loop(0, steps, p2, (jnp.ones((nc, tc), F32), zero, zero, zero), unroll=unroll)
        if with_da:
            da_ref[0] = jnp.sum(dar, axis=0, keepdims=True)
            da_ref[1] = jnp.sum(dai, axis=0, keepdims=True)

    blk = pl.BlockSpec((2, s, tc), lambda i: (0, 0, i))
    vec = pl.BlockSpec((2, 1, tc), lambda i: (0, 0, i))
    in_specs, args = [vec, blk], [a, b]
    out_specs, out_shape = [blk], [jax.ShapeDtypeStruct((2, s, c), F32)]
    if with_da:
        in_specs.append(blk)
        args.append(xs_prev)
        out_specs.append(vec)
        out_shape.append(jax.ShapeDtypeStruct((2, 1, c), F32))
    res = pl.pallas_call(
        body, name=name, grid=(c // tc,), in_specs=in_specs, out_specs=out_specs, out_shape=out_shape,
        compiler_params=_cparams(("arbitrary",)),
    )(*args)
    return res if with_da else res[0]


def _adamw(w, g, m, v, name):
    r, c = w.shape
    tr = r
    for cand in (512, 256, 128, 64, 32, 16, 8):
        if r % cand == 0 and cand * c * 4 <= 2 * 1024 * 1024:
            tr = cand
            break

    def body(w_ref, g_ref, m_ref, v_ref, d_ref, nm_ref, nv_ref):
        gv = g_ref[...]
        nm = ADAM_B1 * m_ref[...] + (1.0 - ADAM_B1) * gv
        nv = ADAM_B2 * v_ref[...] + (1.0 - ADAM_B2) * (gv * gv)
        m_hat = nm / (1.0 - ADAM_B1 ** ADAM_STEP)
        v_hat = nv / (1.0 - ADAM_B2 ** ADAM_STEP)
        d_ref[...] = -ADAM_LR * (m_hat / (jnp.sqrt(v_hat) + ADAM_EPS) + ADAM_WD * w_ref[...])
        nm_ref[...] = nm
        nv_ref[...] = nv

    spec = pl.BlockSpec((tr, c), lambda i: (i, 0))
    sds = jax.ShapeDtypeStruct((r, c), F32)
    return pl.pallas_call(body, name=name, grid=(r // tr,), in_specs=[spec] * 4, out_specs=[spec] * 3,
                          out_shape=[sds] * 3, compiler_params=_cparams(("parallel",)))(w, g, m, v)


def _sum_lead(x, name, out_dtype=F32):
    n, r, c = x.shape
    tr = r
    for cand in (512, 256, 128, 64, 32, 16, 8):
        if r % cand == 0 and n * cand * c * 4 <= 4 * 1024 * 1024:
            tr = cand
            break

    def body(x_ref, o_ref):
        acc = x_ref[0].astype(F32)
        for k in range(1, n):
            acc = acc + x_ref[k].astype(F32)
        o_ref[...] = acc.astype(o_ref.dtype)

    return pl.pallas_call(body, name=name, grid=(r // tr,),
                          in_specs=[pl.BlockSpec((n, tr, c), lambda i: (0, i, 0))],
                          out_specs=pl.BlockSpec((tr, c), lambda i: (i, 0)),
                          out_shape=jax.ShapeDtypeStruct((r, c), out_dtype),
                          compiler_params=_cparams(("parallel",)))(x)


def _row_tile(rows, row_bytes, budget, least=8):
    for cand in (1024, 512, 256, 128, 64, 32, 16, 8):
        if cand >= least and rows % cand == 0 and cand * row_bytes <= budget:
            return cand
    return rows


def _cast_into_slot(w, slot, name):
    r, c = w.shape
    tr = _row_tile(r, c * 4, 4 * 1024 * 1024, least=16)

    def body(slot_ref, w_ref, o_ref):
        o_ref[...] = w_ref[...].astype(o_ref.dtype)

    gs = pltpu.PrefetchScalarGridSpec(
        num_scalar_prefetch=1, grid=(r // tr,),
        in_specs=[pl.BlockSpec((tr, c), lambda i, s: (i, 0))],
        out_specs=pl.BlockSpec((None, tr, c), lambda i, s: (s[0], i, 0)))
    return pl.pallas_call(body, name=name, grid_spec=gs, out_shape=jax.ShapeDtypeStruct((N_CHIPS, r, c), BF16),
                          compiler_params=_cparams(("parallel",)))(slot, w)


def _sum_own(p, t, sel, name):
    _, h, c = p.shape
    tr = _row_tile(h, c * 4, 2 * 1024 * 1024, least=16)
    nblk = h // tr

    def body(sel_ref, p_ref, t_ref, o_ref):
        acc = p_ref[...].astype(F32)
        for k in range(3):
            acc = acc + t_ref[k].astype(F32)
        o_ref[...] = acc

    gs = pltpu.PrefetchScalarGridSpec(
        num_scalar_prefetch=1, grid=(nblk,),
        in_specs=[pl.BlockSpec((None, tr, c), lambda i, s: (s[0], i, 0)),
                  pl.BlockSpec((3, tr, c), lambda i, s: (0, i, 0))],
        out_specs=pl.BlockSpec((tr, c), lambda i, s: (s[1] * nblk + i, 0)))
    return pl.pallas_call(body, name=name, grid_spec=gs, out_shape=jax.ShapeDtypeStruct((2 * h, c), F32),
                          compiler_params=_cparams(("parallel",)))(sel, p, t)


def _add_half(g, t, half, name):
    n, r, c = g.shape
    h = r // 2
    tr = h
    for cand in (512, 256, 128, 64, 32, 16):
        if h % cand == 0 and cand * c * 2 <= 2 * 1024 * 1024:
            tr = cand
            break
    nblk = h // tr

    def body(half_ref, g_ref, t_ref, o_ref):
        o_ref[...] = (g_ref[...].astype(F32) + t_ref[...].astype(F32)).astype(o_ref.dtype)

    gs = pltpu.PrefetchScalarGridSpec(
        num_scalar_prefetch=1, grid=(n, nblk),
        in_specs=[pl.BlockSpec((None, tr, c), lambda j, i, hr: (j, hr[0] * nblk + i, 0)),
                  pl.BlockSpec((None, tr, c), lambda j, i, hr: (j, i, 0))],
        out_specs=pl.BlockSpec((None, tr, c), lambda j, i, hr: (j, i, 0)))
    return pl.pallas_call(body, name=name, grid_spec=gs, out_shape=jax.ShapeDtypeStruct((n, h, c), BF16),
                          compiler_params=_cparams(("parallel", "parallel")))(half, g, t)


def _position():
    x, y, c = lax.axis_index("x"), lax.axis_index("y"), lax.axis_index("c")
    return x, y, c


def _allgather8(xs, name):
    m_per, n = xs.shape

    def body(x_ref, out_ref, send_sems, recv_sems, local_sem):
        x, y, c = _position()
        me, sibling = (x, y, c), (x, y, 1 - c)
        chips = [(1 - x, y), (x, 1 - y), (1 - x, 1 - y)]

        def rows(px, py, pc):
            return out_ref.at[pl.ds((4 * px + 2 * py + pc) * m_per, m_per), :]

        def copy(k, block, to, src=None):
            return pltpu.make_async_remote_copy(
                src_ref=rows(*block) if src is None else src, dst_ref=rows(*block),
                send_sem=send_sems.at[k], recv_sem=recv_sems.at[k], device_id=to, device_id_type=MESH)

        mine = pltpu.make_async_copy(x_ref, rows(*me), local_sem)
        mine.start()
        first = [copy(0, me, sibling, src=x_ref)]
        first += [copy(1 + j, me, (*chip, c), src=x_ref) for j, chip in enumerate(chips)]
        for cp in first:
            cp.start()
        passed = [copy(4 + j, (*chip, c), sibling) for j, chip in enumerate(chips)]
        for j, chip in enumerate(chips):
            copy(1 + j, (*chip, c), me).wait_recv()
            passed[j].start()
        copy(0, sibling, me).wait_recv()
        for j, chip in enumerate(chips):
            copy(4 + j, (*chip, 1 - c), me).wait_recv()
        for cp in first + passed:
            cp.wait_send()
        mine.wait()

    return pl.pallas_call(
        body, name=name, out_shape=jax.ShapeDtypeStruct((N_DEV * m_per, n), xs.dtype),
        in_specs=[pl.BlockSpec(memory_space=pltpu.VMEM)], out_specs=pl.BlockSpec(memory_space=pltpu.VMEM),
        scratch_shapes=[pltpu.SemaphoreType.DMA((7,)), pltpu.SemaphoreType.DMA((7,)), pltpu.SemaphoreType.DMA],
        compiler_params=pltpu.CompilerParams(vmem_limit_bytes=VMEM_LIMIT_BYTES),
    )(xs)


_HBM = pl.BlockSpec(memory_space=pltpu.HBM)


_SEM = pl.BlockSpec(memory_space=pltpu.SEMAPHORE)
_ANY = pl.BlockSpec(memory_space=pl.ANY)
_EFFECT = pltpu.SideEffectType.DATAFLOW_SIDE_EFFECTING


def _in_hbm(a):
    return pltpu.with_memory_space_constraint(a, pltpu.HBM)


def _gather_start(ws, groups, after, name):
    n = len(ws)

    def body(*refs):
        in_refs = refs[:n]
        sems, token = refs[2 * n + 1:-1], refs[-1]
        x, y, c = _position()
        mychip = 2 * x + y
        chips = [(1 - x, y), (x, 1 - y), (1 - x, 1 - y)]
        for g, members in enumerate(groups):
            for k, i in enumerate(members):
                h = ws[i].shape[1] // 2
                mine = in_refs[i].at[mychip, pl.ds(c * h, h), :]
                for j, (px, py) in enumerate(chips):
                    pltpu.make_async_remote_copy(
                        src_ref=mine, dst_ref=mine, send_sem=sems[2 * g].at[3 * k + j],
                        recv_sem=sems[2 * g + 1].at[3 * k + j], device_id=(px, py, c), device_id_type=MESH).start()
        token[...] = jnp.zeros_like(token)

    sem_shapes = [pltpu.SemaphoreType.DMA((3 * len(m),)) for m in groups for _ in range(2)]
    res = pl.pallas_call(
        body, name=name,
        out_shape=[pltpu.HBM(w.shape, w.dtype) for w in ws] + sem_shapes + [jax.ShapeDtypeStruct((8, 128), F32)],
        in_specs=[_HBM] * n + [_ANY],
        out_specs=[_HBM] * n + [_SEM] * len(sem_shapes) + [pl.BlockSpec(memory_space=pltpu.VMEM)],
        input_output_aliases={i: i for i in range(n)},
        compiler_params=pltpu.CompilerParams(has_side_effects=_EFFECT),
    )(*[_in_hbm(w) for w in ws], after)
    bufs, sems, token = res[:n], res[n:-1], res[-1]
    return list(bufs), [(sems[2 * g], sems[2 * g + 1]) for g in range(len(groups))], token


def _gather_wait(bufs, send_sems, recv_sems, after, name):
    m = len(bufs)

    def body(*refs):
        in_refs = refs[:m]
        send, recv = refs[m], refs[m + 1]
        x, y, c = _position()
        mychip = 2 * x + y
        chips = [(1 - x, y), (x, 1 - y), (1 - x, 1 - y)]
        for k in range(m):
            h = bufs[k].shape[1] // 2
            mine = in_refs[k].at[mychip, pl.ds(c * h, h), :]
            for j, (px, py) in enumerate(chips):
                cp = pltpu.make_async_remote_copy(
                    src_ref=mine, dst_ref=in_refs[k].at[2 * px + py, pl.ds(c * h, h), :],
                    send_sem=send.at[3 * k + j], recv_sem=recv.at[3 * k + j],
                    device_id=(px, py, c), device_id_type=MESH)
                cp.wait_send()
                cp.wait_recv()

    res = pl.pallas_call(
        body, name=name, out_shape=[pltpu.HBM(b.shape, b.dtype) for b in bufs],
        in_specs=[_HBM] * m + [_SEM, _SEM, _ANY], out_specs=[_HBM] * m,
        input_output_aliases={k: k for k in range(m)},
        compiler_params=pltpu.CompilerParams(has_side_effects=_EFFECT),
    )(*bufs, send_sems, recv_sems, after)
    return list(res)


def _forward_halves(ws, name):
    n = len(ws)

    def body(*refs):
        out_refs = refs[n:2 * n]
        send_sems, recv_sems = refs[2 * n:]
        x, y, c = _position()
        me, sibling = (x, y, c), (x, y, 1 - c)
        chips = [(1 - x, y), (x, 1 - y), (1 - x, 1 - y)]
        cps = []
        for i in range(n):
            h = ws[i].shape[1] // 2
            for j, (px, py) in enumerate(chips):
                got = out_refs[i].at[2 * px + py, pl.ds(c * h, h), :]
                cp = pltpu.make_async_remote_copy(
                    src_ref=got, dst_ref=got, send_sem=send_sems.at[3 * i + j], recv_sem=recv_sems.at[3 * i + j],
                    device_id=sibling, device_id_type=MESH)
                cp.start()
                cps.append(cp)
        for i in range(n):
            h = ws[i].shape[1] // 2
            for j, (px, py) in enumerate(chips):
                other = out_refs[i].at[2 * px + py, pl.ds((1 - c) * h, h), :]
                pltpu.make_async_remote_copy(
                    src_ref=other, dst_ref=other, send_sem=send_sems.at[3 * i + j], recv_sem=recv_sems.at[3 * i + j],
                    device_id=me, device_id_type=MESH).wait_recv()
        for cp in cps:
            cp.wait_send()

    return pl.pallas_call(
        body, name=name,
        out_shape=[jax.ShapeDtypeStruct(w.shape, w.dtype) for w in ws],
        in_specs=[_HBM] * n, out_specs=[_HBM] * n, input_output_aliases={i: i for i in range(n)},
        scratch_shapes=[pltpu.SemaphoreType.DMA((3 * n,)), pltpu.SemaphoreType.DMA((3 * n,))],
    )(*ws)


def _swap_halves(gs, name):
    n = len(gs)

    def body(*refs):
        in_refs, out_refs = refs[:n], refs[n:2 * n]
        send_sems, recv_sems = refs[2 * n:]
        x, y, c = _position()
        cps = []
        for i in range(n):
            h = gs[i].shape[1] // 2
            cp = pltpu.make_async_remote_copy(
                src_ref=in_refs[i].at[:, pl.ds((1 - c) * h, h), :], dst_ref=out_refs[i],
                send_sem=send_sems.at[i], recv_sem=recv_sems.at[i], device_id=(x, y, 1 - c), device_id_type=MESH)
            cp.start()
            cps.append(cp)
        for cp in cps:
            cp.wait()

    return pl.pallas_call(
        body, name=name,
        out_shape=[jax.ShapeDtypeStruct((g.shape[0], g.shape[1] // 2, g.shape[2]), g.dtype) for g in gs],
        in_specs=[_HBM] * n, out_specs=[_HBM] * n,
        scratch_shapes=[pltpu.SemaphoreType.DMA((n,)), pltpu.SemaphoreType.DMA((n,))],
    )(*gs)


def _scatter_copies(p_refs, land_refs, send, recv):
    x, y, c = _position()
    chips = [(1 - x, y), (x, 1 - y), (1 - x, 1 - y)]
    return [pltpu.make_async_remote_copy(
        src_ref=p_refs[i].at[2 * px + py], dst_ref=land_refs[i].at[j],
        send_sem=send.at[3 * i + j], recv_sem=recv.at[3 * i + j], device_id=(px, py, c), device_id_type=MESH)
        for i in range(len(p_refs)) for j, (px, py) in enumerate(chips)]


def _scatter_start(ps, name):
    n = len(ps)
    lands = [lax.empty((3,) + p.shape[1:], p.dtype) for p in ps]

    def body(*refs):
        for cp in _scatter_copies(refs[:n], refs[n:2 * n], refs[4 * n], refs[4 * n + 1]):
            cp.start()
        refs[4 * n + 2][...] = jnp.zeros_like(refs[4 * n + 2])

    res = pl.pallas_call(
        body, name=name,
        out_shape=[pltpu.HBM(a.shape, a.dtype) for a in list(ps) + lands]
        + [pltpu.SemaphoreType.DMA((3 * n,)), pltpu.SemaphoreType.DMA((3 * n,)), jax.ShapeDtypeStruct((8, 128), F32)],
        in_specs=[_HBM] * (2 * n),
        out_specs=[_HBM] * (2 * n) + [_SEM, _SEM, pl.BlockSpec(memory_space=pltpu.VMEM)],
        input_output_aliases={i: i for i in range(2 * n)},
        compiler_params=pltpu.CompilerParams(has_side_effects=_EFFECT),
    )(*[_in_hbm(a) for a in list(ps) + lands])
    return list(res[:n]), list(res[n:2 * n]), res[2 * n], res[2 * n + 1], res[2 * n + 2]


def _scatter_wait(ps, lands, send_sems, recv_sems, after, name):
    n = len(ps)

    def body(*refs):
        for cp in _scatter_copies(refs[:n], refs[n:2 * n], refs[2 * n], refs[2 * n + 1]):
            cp.wait_send()
            cp.wait_recv()

    res = pl.pallas_call(
        body, name=name, out_shape=[pltpu.HBM(a.shape, a.dtype) for a in list(ps) + list(lands)],
        in_specs=[_HBM] * (2 * n) + [_SEM, _SEM, _ANY], out_specs=[_HBM] * (2 * n),
        input_output_aliases={i: i for i in range(2 * n)},
        compiler_params=pltpu.CompilerParams(has_side_effects=_EFFECT),
    )(*ps, *lands, send_sems, recv_sems, after)
    return list(res[:n]), list(res[n:])


def _join_halves(rs, name):
    n = len(rs)

    def body(*refs):
        out_refs = refs[n:2 * n]
        send_sems, recv_sems = refs[2 * n:]
        x, y, c = _position()
        cps = []
        for i in range(n):
            h = rs[i].shape[0] // 2
            mine = out_refs[i].at[pl.ds(c * h, h), :]
            cp = pltpu.make_async_remote_copy(
                src_ref=mine, dst_ref=mine, send_sem=send_sems.at[i], recv_sem=recv_sems.at[i],
                device_id=(x, y, 1 - c), device_id_type=MESH)
            cp.start()
            cps.append(cp)
        for i in range(n):
            h = rs[i].shape[0] // 2
            other = out_refs[i].at[pl.ds((1 - c) * h, h), :]
            pltpu.make_async_remote_copy(
                src_ref=other, dst_ref=other, send_sem=send_sems.at[i], recv_sem=recv_sems.at[i],
                device_id=(x, y, c), device_id_type=MESH).wait_recv()
        for cp in cps:
            cp.wait_send()

    return pl.pallas_call(
        body, name=name,
        out_shape=[jax.ShapeDtypeStruct(r.shape, r.dtype) for r in rs],
        in_specs=[_HBM] * n, out_specs=[_HBM] * n, input_output_aliases={i: i for i in range(n)},
        scratch_shapes=[pltpu.SemaphoreType.DMA((n,)), pltpu.SemaphoreType.DMA((n,))],
    )(*rs)


def _t5_buckets_block():
    qi = np.arange(BLOCK)[:, None]
    ki = np.arange(2 * BLOCK)[None, :]
    n = np.maximum(qi + BLOCK - ki, 0)
    max_exact = NUM_BUCKETS // 2
    large = max_exact + (np.log(np.maximum(n, 1) / max_exact) / np.log(MAX_DISTANCE / max_exact)
                         * (NUM_BUCKETS - max_exact)).astype(np.int32)
    large = np.minimum(large, NUM_BUCKETS - 1)
    return np.where(n < max_exact, n, large).astype(np.int32)


def _discretise(lambda_re, lambda_im, log_step, b_re, b_im):
    lam_re = jnp.minimum(lambda_re, -1e-4)
    lam_im = lambda_im
    delta = jnp.exp(log_step)[:, None]
    mag = jnp.exp(lam_re * delta)
    ang = lam_im * delta
    abar_re, abar_im = mag * jnp.cos(ang), mag * jnp.sin(ang)
    num_re, num_im = abar_re - 1.0, abar_im
    den = lam_re * lam_re + lam_im * lam_im
    f_re = (num_re * lam_re + num_im * lam_im) / den
    f_im = (num_im * lam_re - num_re * lam_im) / den
    bbar_re = f_re[..., None] * b_re - f_im[..., None] * b_im
    bbar_im = f_re[..., None] * b_im + f_im[..., None] * b_re
    return abar_re, abar_im, bbar_re, bbar_im


def _interleave(v, nc):
    s, w = v.shape
    return v.reshape(nc, s // nc, w).transpose(1, 0, 2).reshape(s, w)


def _deinterleave(v, nc):
    s, w = v.shape
    return v.reshape(s // nc, nc, w).transpose(1, 0, 2).reshape(s, w)


_SMALL = ("norm1_g", "b_in", "attn_sinks", "rel_bias", "lambda_re", "lambda_im", "log_step", "ssm_b_re",
          "ssm_b_im", "ssm_c_re", "ssm_c_im", "ssm_d", "b_glu", "norm2_g", "final_g")


def _pack(parts):
    rows = []
    for p in parts:
        f = p.reshape(-1).astype(F32)
        pad = (-f.shape[0]) % 128
        rows.append(jnp.pad(f, (0, pad)).reshape(-1, 128))
    out = jnp.concatenate(rows, axis=0)
    pad = (-out.shape[0]) % 256
    return jnp.pad(out, ((0, pad), (0, 0)))


def _unpack(packed, shapes):
    res, r = [], 0
    for shp in shapes:
        size = int(np.prod(shp))
        nr = -(-size // 128)
        res.append(packed[r:r + nr].reshape(-1)[:size].reshape(shp))
        r += nr
    return res


def kernel(x, c, w_ada, b_ada, norm1_g, w_in, b_in, attn_sinks, rel_bias, lambda_re, lambda_im, log_step, ssm_b_re, ssm_b_im, ssm_c_re, ssm_c_im, ssm_d, w_glu, b_glu, w_attn_proj, w_ssm_proj, w_out, norm2_g, w_ff1, w_ff2, final_g, loss_target, m_w_ada, m_b_ada, m_norm1_g, m_w_in, m_b_in, m_attn_sinks, m_rel_bias, m_lambda_re, m_lambda_im, m_log_step, m_ssm_b_re, m_ssm_b_im, m_ssm_c_re, m_ssm_c_im, m_ssm_d, m_w_glu, m_b_glu, m_w_attn_proj, m_w_ssm_proj, m_w_out, m_norm2_g, m_w_ff1, m_w_ff2, m_final_g, v_w_ada, v_b_ada, v_norm1_g, v_w_in, v_b_in, v_attn_sinks, v_rel_bias, v_lambda_re, v_lambda_im, v_log_step, v_ssm_b_re, v_ssm_b_im, v_ssm_c_re, v_ssm_c_im, v_ssm_d, v_w_glu, v_b_glu, v_w_attn_proj, v_w_ssm_proj, v_w_out, v_norm2_g, v_w_ff1, v_w_ff2, v_final_g):
    given = dict(locals())
    S, D = x.shape[1], x.shape[2]
    SSM_W = w_glu.shape[2]
    G = SSM_W // SSM_GROUP_CH
    NST = G * SSM_STATE
    DFF = w_ff2.shape[1] * N_CHIPS
    INW = w_in.shape[2] * N_CHIPS
    o_q, o_k, o_v, o_u = 0, ATTN_WIDTH, ATTN_WIDTH + KV_WIDTH, ATTN_WIDTH + 2 * KV_WIDTH
    o_ga, o_gs = o_u + SSM_W, o_u + SSM_W + D
    mx, my, mc = _position()
    my_chip = 2 * mx + my
    my_b = 4 * mx + 2 * my + mc

    xv, tgt = x[0], loss_target[0]

    big = dict(w_in=w_in[0], w_glu=w_glu[0], w_attn_proj=w_attn_proj[0], w_ssm_proj=w_ssm_proj[0],
               w_out=w_out[0], w_ff1=w_ff1[0], w_ff2=w_ff2[0])
    big_names = list(big)
    colsharded = {"w_in", "w_attn_proj", "w_ssm_proj", "w_ff1"}
    chip_sel = my_chip.astype(jnp.int32).reshape(1)
    gather_groups = [["w_in"], ["w_attn_proj", "w_ssm_proj", "w_glu", "w_out"], ["w_ff1", "w_ff2"]]
    in_flight, gather_sems, gathered = {}, [], {}

    def finish_gather(g, after):
        bufs = [in_flight[k] for k in gather_groups[g]]
        bufs = _gather_wait(bufs, gather_sems[g][0], gather_sems[g][1], after, "gather_wait_%d" % g)
        gathered.update(zip(gather_groups[g], _forward_halves(bufs, "gather_forward_%d" % g)))

    def tied(v, token):
        return v + token[0:1, 0:1]

    def all_of(*arrays):
        return jnp.stack([a.reshape(-1)[0].astype(F32) for a in arrays])

    def wop(k):
        g = gathered[k]
        return _Op(g, N_CHIPS) if k in colsharded else _Op(g.reshape(g.shape[0] * g.shape[1], g.shape[2]))

    grads = {}
    half = mc.astype(jnp.int32).reshape(1)
    sel = jnp.stack([my_chip, mc]).astype(jnp.int32)

    def rs_begin(tag, named):
        keys, gl = list(named), []
        for k in keys:
            gk = named[k]
            if k not in colsharded:
                gk = gk.reshape(N_CHIPS, gk.shape[0] // N_CHIPS, gk.shape[1])
            gl.append(gk)
        t1 = _swap_halves(gl, "rs_swap_" + tag)
        ps = [_add_half(g, t, half, "rs_add_" + k) for g, t, k in zip(gl, t1, keys)]
        ps, lands, ssem, rsem, token = _scatter_start(ps, "rs_start_" + tag)
        return (keys, ps, lands, ssem, rsem), token

    def rs_end(tag, state, after):
        keys, ps, lands, ssem, rsem = state
        ps, lands = _scatter_wait(ps, lands, ssem, rsem, after, "rs_wait_" + tag)
        rs = [_sum_own(p, t, sel, "rs_sum_" + k) for p, t, k in zip(ps, lands, keys)]
        full = _join_halves(rs, "rs_join_" + tag)
        for k, f in zip(keys, full):
            grads[k] = f[None]
        return full[-1]

    c_all = _allgather8(jnp.pad(c, ((0, 7), (0, 0))), "gather_c").reshape(N_DEV, 8, D)[:, 0]
    c16 = jnp.pad(c_all, ((0, 8), (0, 0)))
    b_ada_mine = lax.dynamic_slice(b_ada.reshape(N_CHIPS, -1), (my_chip, 0), (1, w_ada.shape[2]))
    mod_sh = _mm(c16, w_ada[0], "NN", name="mod", M=16, N=w_ada.shape[2], K=D, a_fn=_silu,
                 epilogue=lambda acc, b: (acc + b,), extras=[(b_ada_mine, "row")])
    mod_all = _allgather8(mod_sh[:8], "gather_mod").reshape(N_DEV, 8, -1)
    mod_row = jnp.concatenate(
        [lax.dynamic_slice(mod_all, (2 * j, my_b, 0), (1, 1, mod_all.shape[2]))[0] for j in range(N_CHIPS)], axis=1)
    sh1, sc1, g1, sh2, sc2, g2 = [mod_row[:, i * D:(i + 1) * D] for i in range(6)]

    first = [_cast_into_slot(big["w_in"], chip_sel, "cast_w_in")]
    first, sems_first, token_first = _gather_start(first, [[0]], mod_all, "gather_start_in")
    rest_names = gather_groups[1] + gather_groups[2]
    rest = [_cast_into_slot(big[k], chip_sel, "cast_" + k) for k in rest_names]
    rest, sems_rest, token_rest = _gather_start(
        rest, [[rest_names.index(k) for k in grp] for grp in gather_groups[1:]], token_first, "gather_start_rest")
    in_flight.update(zip(["w_in"] + rest_names, first + rest))
    gather_sems.extend(sems_first + sems_rest)

    disc_in = (lambda_re[0], lambda_im[0], log_step[0], ssm_b_re[0], ssm_b_im[0])
    (abar_re, abar_im, bbar_re, bbar_im), disc_vjp = jax.vjp(_discretise, *disc_in)
    eye = jnp.eye(G, dtype=F32)
    bd = jnp.concatenate([jnp.einsum("gnp,gh->gphn", bb, eye).reshape(SSM_W, NST) for bb in (bbar_re, bbar_im)], axis=1)
    cd = jnp.concatenate([jnp.einsum("gpn,gh->gnhp", cc, eye).reshape(NST, SSM_W)
                          for cc in (ssm_c_re[0], -ssm_c_im[0])], axis=0)
    a_fwd = jnp.stack([abar_re.reshape(1, NST), abar_im.reshape(1, NST)])
    a_bwd = jnp.stack([abar_re.reshape(1, NST), -abar_im.reshape(1, NST)])
    d_row = ssm_d

    buckets = _t5_buckets_block()
    onehot = (jnp.asarray(buckets.reshape(-1, 1)) == jnp.arange(128, dtype=jnp.int32)[None, :]).astype(BF16)
    rb_hi = rel_bias.astype(BF16)
    rb_lo = (rel_bias - rb_hi.astype(F32)).astype(BF16)
    rb_lo2 = (rel_bias - rb_hi.astype(F32) - rb_lo.astype(F32)).astype(BF16)
    rb3 = jnp.pad(jnp.concatenate([rb_hi, rb_lo, rb_lo2], axis=1),
                  ((0, 128 - NUM_BUCKETS), (0, 128 - 3 * N_Q_HEADS)))
    b3 = _mm(onehot, rb3, "NN", name="rel_bias_rows", M=BLOCK * 2 * BLOCK, N=128, K=128)
    bias = (b3[:, :N_Q_HEADS] + b3[:, N_Q_HEADS:2 * N_Q_HEADS]) + b3[:, 2 * N_Q_HEADS:3 * N_Q_HEADS]
    bias = jnp.transpose(bias.reshape(BLOCK, 2 * BLOCK, N_Q_HEADS), (2, 0, 1))
    sinks_b = jnp.broadcast_to(attn_sinks[0][:, None, None], (N_Q_HEADS, BLOCK, 128)).reshape(N_Q_HEADS * BLOCK, 128)

    h1 = _rowwise(_norm_mod, [(xv, "tile", D), (tied(tied(norm1_g, token_first), token_rest), "row", D), (sh1, "row", D),
                              (sc1, "row", D)],
                  [(D, BF16)], [], name="norm1", rows=S)[0]
    finish_gather(0, all_of(h1, bd, cd, a_fwd, a_bwd, bias, sinks_b))
    proj = _mm(h1, wop("w_in"), "NN", name="proj", M=S, N=INW, K=D,
               epilogue=lambda acc, b: (acc + b,), extras=[(b_in, "row")])

    def heads(v2d, nh):
        return v2d.reshape(S, nh, HEAD_DIM).transpose(1, 0, 2)

    def unheads(v3d):
        return v3d.transpose(1, 0, 2).reshape(S, -1)

    qh = heads(proj[:, o_q:o_k], N_Q_HEADS)
    kh = heads(proj[:, o_k:o_v], N_KV_HEADS)
    vh = heads(proj[:, o_v:o_u], N_KV_HEADS)
    attn = unheads(_attn_fwd(qh, kh, vh, sinks_b, bias, "attn_fwd"))
    finish_gather(1, attn)
    y_attn = _mm(attn, wop("w_attn_proj"), "NN", name="attn_proj", M=S, N=D, K=ATTN_WIDTH)

    u = proj[:, o_u:o_ga]
    u_il = _interleave(u, SCAN_CHUNKS)
    SB = 128
    nsb, gpb = SSM_W // SB, SB // SSM_GROUP_CH
    SBN = gpb * SSM_STATE
    bu = _mm(u_il, bd, "NN", name="ssm_bu", M=S, N=2 * NST, K=SB, out_nsh=2, tj=SBN, tk=SB,
             a_idx=lambda i, j, k: (i, j % nsb), b_idx=lambda i, j, k: (j % nsb, j))
    xs = _scan(a_fwd, bu, None, reverse=False, name="scan_fwd", tc=256)
    y_il = _mm(_Op(xs, 2), cd, "NN", name="ssm_y", M=S, N=SSM_W, K=2 * SBN, tj=SB, tk=SBN,
               a_idx=lambda i, j, k: (i, j + nsb * k), b_idx=lambda i, j, k: (j + nsb * k, j),
               epilogue=lambda acc, uu, dd: (acc + dd * uu,), extras=[(u_il, "tile"), (d_row, "row")])
    y = _deinterleave(y_il, SCAN_CHUNKS)
    z0b = _rowwise(_gelu, [(y, "tile", SSM_W)], [(SSM_W, BF16)], [], name="gelu", rows=S)[0]
    z, t_glu = _mm(z0b, wop("w_glu"), "NN", name="glu", M=S, N=SSM_W, K=SSM_W, out_dtypes=(BF16, F32),
                   epilogue=lambda acc, b, yy: (_gelu(yy) * _sigmoid(acc + b), acc + b),
                   extras=[(b_glu, "row"), (y, "tile")])
    y_ssm = _mm(z, wop("w_ssm_proj"), "NN", name="ssm_proj", M=S, N=D, K=SSM_W)

    merged = _rowwise(_merge, [(_Op(proj, coff=o_ga), "tile", D), (_Op(proj, coff=o_gs), "tile", D),
                               (y_attn, "tile", D), (y_ssm, "tile", D)], [(D, BF16)], [], name="merge", rows=S)[0]
    mo, x2 = _mm(merged, wop("w_out"), "NN", name="out_proj", M=S, N=D, K=D, out_dtypes=(F32, F32),
                 epilogue=lambda acc, xx, gg: (acc, xx + gg * acc), extras=[(xv, "tile"), (g1, "row")])
    h2 = _rowwise(_norm_mod, [(x2, "tile", D), (norm2_g, "row", D), (sh2, "row", D), (sc2, "row", D)],
                  [(D, BF16)], [], name="norm2", rows=S)[0]
    finish_gather(2, h2)
    a_b, r_b = _mm(h2, wop("w_ff1"), "NN", name="ff1", M=S, N=DFF, K=D, out_dtypes=(BF16, BF16),
                   epilogue=lambda acc: (acc, jnp.square(jnp.maximum(acc, 0.0))))
    ff, x3 = _mm(r_b, wop("w_ff2"), "NN", name="ff2", M=S, N=D, K=DFF, out_dtypes=(F32, F32),
                 epilogue=lambda acc, xx, gg: (acc, xx + gg * acc), extras=[(x2, "tile"), (g2, "row")],
                 tj=1024, tk=1024)

    def final_fn(x3b, gf, tb):
        def f(xx, gg):
            yv = xx * lax.rsqrt(jnp.mean(xx * xx, axis=-1, keepdims=True) + EPS) * gg
            err = jnp.square(yv - tb)
            return 0.5 * jnp.sum(jnp.mean(err, axis=-1, keepdims=True), axis=0, keepdims=True)
        lv, vjp = jax.vjp(f, x3b, gf)
        dx, dg = vjp(jnp.ones((1, 1), F32))
        return dx, dg, jnp.broadcast_to(lv, (1, 128))

    dx3, g_final, loss_acc = _rowwise(final_fn, [(x3, "tile", D), (final_g.reshape(1, D), "row", D), (tgt, "tile", D)],
                                      [(D, F32)], [D, 128], name="final", rows=S)

    def ff_out_bwd(dx3b, ffb, g2b):
        return dx3b * g2b, jnp.sum(dx3b * ffb, axis=0, keepdims=True)

    dff, d_g2 = _rowwise(ff_out_bwd, [(dx3, "tile", D), (ff, "tile", D), (g2, "row", D)], [(D, BF16)], [D],
                         name="ff_out_bwd", rows=S)
    da = _mm(dff, wop("w_ff2"), "NT", name="ff2_dx", M=S, N=DFF, K=D, out_dtypes=(BF16,),
             epilogue=lambda acc, ab: (acc * (2.0 * jnp.maximum(ab.astype(F32), 0.0)),), extras=[(a_b, "tile")])
    g_w_ff2 = _mm(r_b, dff, "TN", name="ff2_dw", M=DFF, N=D, K=S, out_dtypes=(BF16,), tj=1024, tk=1024)
    dh2 = _mm(da, wop("w_ff1"), "NT", name="ff1_dx", M=S, N=D, K=DFF, tj=1024, tk=1024)
    g_w_ff1 = _mm(h2, da, "TN", name="ff1_dw", M=D, N=DFF, K=S, out_dtypes=(BF16,), out_nsh=N_CHIPS, tj=1024, tk=1024)
    rs_ff, token_ff = rs_begin("ff", dict(w_ff2=g_w_ff2, w_ff1=g_w_ff1))

    def norm2_bwd(x2b, dh2b, dx3b, mob, gn, shb, scb, g1b):
        _, vjp = jax.vjp(_norm_mod, x2b, gn, shb, scb)
        dx, dg, dsh, dsc = vjp(dh2b)
        dx2b = dx + dx3b
        return dx2b, dx2b * g1b, dg, dsh, dsc, jnp.sum(dx2b * mob, axis=0, keepdims=True)

    dx2, dmo, g_norm2, d_sh2, d_sc2, d_g1 = _rowwise(
        norm2_bwd, [(x2, "tile", D), (dh2, "tile", D), (dx3, "tile", D), (mo, "tile", D),
                    (tied(norm2_g, token_ff), "row", D), (sh2, "row", D), (sc2, "row", D), (g1, "row", D)],
        [(D, F32), (D, BF16)], [D, D, D, D], name="norm2_bwd", rows=S, tr=128)
    dmerged = _mm(dmo, wop("w_out"), "NT", name="out_dx", M=S, N=D, K=D)
    g_w_out = _mm(merged, dmo, "TN", name="out_dw", M=D, N=D, K=S, out_dtypes=(BF16,), tk=1024)

    def merge_bwd(gab, gsb, yab, ysb, dmb):
        _, vjp = jax.vjp(_merge, gab, gsb, yab, ysb)
        return vjp(dmb)

    d_ga, d_gs, dy_attn, dy_ssm = _rowwise(
        merge_bwd, [(_Op(proj, coff=o_ga), "tile", D), (_Op(proj, coff=o_gs), "tile", D), (y_attn, "tile", D),
                    (y_ssm, "tile", D), (dmerged, "tile", D)],
        [(D, BF16), (D, BF16), (D, BF16), (D, BF16)], [], name="merge_bwd", rows=S, tr=128)

    dattn = _mm(dy_attn, wop("w_attn_proj"), "NT", name="attn_proj_dx", M=S, N=ATTN_WIDTH, K=D)
    g_w_attn_proj = _mm(attn, dy_attn, "TN", name="attn_proj_dw", M=ATTN_WIDTH, N=D, K=S, out_dtypes=(BF16,),
                        out_nsh=N_CHIPS, tk=1024)
    dqh, dkh, dvh, dsink_blk, dbias = _attn_bwd(qh, kh, vh, heads(dattn, N_Q_HEADS), sinks_b, bias, "attn_bwd")
    g_sinks = _sum_lead(dsink_blk.reshape(N_Q_HEADS, BLOCK, 128).transpose(1, 0, 2), "sinks_dw")[:, 0].reshape(1, N_Q_HEADS)
    g_rel = _mm(dbias.reshape(N_Q_HEADS, -1), onehot, "NN", name="rel_bias_dw", M=N_Q_HEADS, N=128,
                K=BLOCK * 2 * BLOCK, tk=4096)
    g_rel_bias = g_rel[:, :NUM_BUCKETS].T

    dz = _mm(dy_ssm, wop("w_ssm_proj"), "NT", name="ssm_proj_dx", M=S, N=SSM_W, K=D)
    g_w_ssm_proj = _mm(z, dy_ssm, "TN", name="ssm_proj_dw", M=SSM_W, N=D, K=S, out_dtypes=(BF16,),
                       out_nsh=N_CHIPS, tk=1024)

    def glu_bwd(dzb, yb, tb):
        z0 = _gelu(yb)
        sg = _sigmoid(tb)
        dt = dzb * z0 * sg * (1.0 - sg)
        return dt, dzb * sg, jnp.sum(dt, axis=0, keepdims=True)

    dt_b, dz0a, g_b_glu = _rowwise(glu_bwd, [(dz, "tile", SSM_W), (y, "tile", SSM_W), (t_glu, "tile", SSM_W)],
                                   [(SSM_W, BF16), (SSM_W, F32)], [SSM_W], name="glu_bwd", rows=S)

    def gelu_bwd(acc, dz0ab, yb):
        _, vjp = jax.vjp(_gelu, yb)
        return (vjp(acc + dz0ab)[0],)

    dy = _mm(dt_b, wop("w_glu"), "NT", name="glu_dx", M=S, N=SSM_W, K=SSM_W, epilogue=gelu_bwd,
             extras=[(dz0a, "tile"), (y, "tile")])
    g_w_glu = _mm(z0b, dt_b, "TN", name="glu_dw", M=SSM_W, N=SSM_W, K=S, out_dtypes=(BF16,), tk=1024)
    rs_mix, token_mix = rs_begin("mix", dict(w_out=g_w_out, w_attn_proj=g_w_attn_proj, w_ssm_proj=g_w_ssm_proj,
                                             w_glu=g_w_glu))
    dy_il = _interleave(tied(dy, token_mix), SCAN_CHUNKS)
    dxs = _mm(dy_il, cd, "NT", name="ssm_dx", M=S, N=2 * NST, K=SB, out_nsh=2, tj=SBN, tk=SB,
              a_idx=lambda i, j, k: (i, j % nsb), b_idx=lambda i, j, k: (j, j % nsb))
    g_cd = _mm(_Op(xs, 2), dy_il, "TN", name="ssm_dc", M=2 * NST, N=SB, K=S, ti=SBN, tj=SB, tk=1024,
               b_idx=lambda i, j, k: (k, i % nsb))
    lam, d_abar = _scan(a_bwd, dxs, xs, reverse=True, name="scan_bwd", tc=128)

    def du_fn(acc, dyb, dd):
        return (acc + dd * dyb,)

    du_il = _mm(_Op(lam, 2), bd, "NT", name="ssm_du", M=S, N=SSM_W, K=2 * SBN, tj=SB, tk=SBN,
                a_idx=lambda i, j, k: (i, j + nsb * k), b_idx=lambda i, j, k: (j, j + nsb * k),
                epilogue=du_fn, extras=[(dy_il, "tile"), (d_row, "row")])
    g_bd = _mm(u_il, _Op(lam, 2), "TN", name="ssm_db", M=SSM_W, N=2 * SBN, K=S, ti=SB, tj=SBN, tk=1024,
               b_idx=lambda i, j, k: (k, i + nsb * j))
    g_ssm_d = _rowwise(lambda dyb, ub: (jnp.sum(dyb * ub, axis=0, keepdims=True),),
                       [(dy_il, "tile", SSM_W), (u_il, "tile", SSM_W)], [], [SSM_W], name="ssm_dd", rows=S)[0]
    du = _deinterleave(du_il, SCAN_CHUNKS)

    eye_b = jnp.eye(gpb, dtype=F32)
    g_cd6 = g_cd.reshape(2, nsb, gpb, SSM_STATE, gpb, SSM_GROUP_CH)
    g_c_re = jnp.einsum("bgnhp,gh->bgpn", g_cd6[0], eye_b).reshape(G, SSM_GROUP_CH, SSM_STATE)
    g_c_im = -jnp.einsum("bgnhp,gh->bgpn", g_cd6[1], eye_b).reshape(G, SSM_GROUP_CH, SSM_STATE)
    g_bd6 = g_bd.reshape(nsb, gpb, SSM_GROUP_CH, 2, gpb, SSM_STATE)
    g_bbar = jnp.einsum("bhprgn,hg->rbhnp", g_bd6, eye_b).reshape(2, G, SSM_STATE, SSM_GROUP_CH)
    g_bbar_re, g_bbar_im = g_bbar[0], g_bbar[1]
    g_lre, g_lim, g_lstep, g_bre, g_bim = disc_vjp(
        (d_abar[0].reshape(G, SSM_STATE), d_abar[1].reshape(G, SSM_STATE), g_bbar_re, g_bbar_im))

    dproj = jnp.concatenate([unheads(dqh).astype(BF16), unheads(dkh).astype(BF16), unheads(dvh).astype(BF16),
                             du.astype(BF16), d_ga, d_gs], axis=1)
    g_w_in = _mm(h1, dproj, "TN", name="proj_dw", M=D, N=INW, K=S, out_dtypes=(BF16,), out_nsh=N_CHIPS,
                 tj=INW // (2 * N_CHIPS), tk=1024)
    rs_in, token_in = rs_begin("in", dict(w_in=g_w_in))
    dh1 = _mm(dproj, wop("w_in"), "NT", name="proj_dx", M=S, N=D, K=INW, tj=1024, tk=INW // N_CHIPS,
              epilogue=lambda acc, zero: (acc + zero,), extras=[(tied(jnp.zeros((1, D), F32), token_in), "row")])
    g_b_in = _rowwise(lambda d: (jnp.sum(d.astype(F32), axis=0, keepdims=True),), [(dproj, "tile", INW)], [], [INW],
                      name="proj_db", rows=S)[0]

    def norm1_bwd(xb, dhb, dresb, gn, shb, scb):
        _, vjp = jax.vjp(_norm_mod, xb, gn, shb, scb)
        dx, dg, dsh, dsc = vjp(dhb)
        return dx + dresb, dg, dsh, dsc

    grad_x, g_norm1, d_sh1, d_sc1 = _rowwise(
        norm1_bwd, [(xv, "tile", D), (dh1, "tile", D), (dx2, "tile", D), (norm1_g, "row", D), (sh1, "row", D),
                    (sc1, "row", D)], [(D, F32)], [D, D, D], name="norm1_bwd", rows=S)

    dmod_row = jnp.concatenate([d_sh1, d_sc1, d_g1, d_sh2, d_sc2, d_g2], axis=1)
    dmod_all = _allgather8(jnp.pad(dmod_row, ((0, 7), (0, 0))), "gather_dmod").reshape(N_DEV, 8, -1)[:, 0]
    g_b_ada = _sum_lead(dmod_all.reshape(N_DEV, -1, 128), "b_ada_dw").reshape(1, -1)
    dmod_mine = lax.dynamic_slice(dmod_all.reshape(N_DEV, N_CHIPS, -1), (0, my_chip, 0), (N_DEV, 1, w_ada.shape[2]))[:, 0]
    g_w_ada = _mm(c16, jnp.pad(dmod_mine, ((0, 8), (0, 0))), "TN", name="ada_dw", M=D, N=w_ada.shape[2], K=16,
                  a_fn=_silu)

    small_g = dict(norm1_g=g_norm1, b_in=g_b_in, attn_sinks=g_sinks, rel_bias=g_rel_bias, lambda_re=g_lre[None],
                   lambda_im=g_lim[None], log_step=g_lstep[None], ssm_b_re=g_bre[None], ssm_b_im=g_bim[None],
                   ssm_c_re=g_c_re[None], ssm_c_im=g_c_im[None], ssm_d=g_ssm_d, b_glu=g_b_glu, norm2_g=g_norm2,
                   final_g=g_final.reshape(D))
    packed = _pack([loss_acc[:, :1]] + [small_g[k] for k in _SMALL])
    rows = packed.shape[0]
    summed = _sum_lead(_allgather8(packed, "gather_small").reshape(N_DEV, rows, 128), "small_sum")
    small_shapes = [(1,)] + [given[k].shape for k in _SMALL]
    parts = _unpack(summed, small_shapes)
    loss = parts[0].reshape(())
    grads.update(zip(_SMALL, parts[1:]))
    grads["b_ada"] = g_b_ada
    grads["w_ada"] = g_w_ada[None]

    deltas, new_m, new_v = {}, {}, {}

    def adamw_big(k):
        d_, m_, v_ = _adamw(given[k][0], grads[k][0], given["m_" + k][0], given["v_" + k][0], "adamw_" + k)
        deltas[k], new_m[k], new_v[k] = d_[None], m_[None], v_[None]
        return v_

    rs_end("ff", rs_ff, summed)
    marks = [adamw_big(k) for k in ("w_ff2", "w_ff1")]
    rs_end("mix", rs_mix, all_of(*marks))
    marks = [adamw_big(k) for k in ("w_out", "w_attn_proj", "w_ssm_proj", "w_glu", "w_ada")]
    small_all = list(_SMALL) + ["b_ada"]
    shapes = [given[k].shape for k in small_all]
    pw, pg = _pack([given[k] for k in small_all]), _pack([grads[k] for k in small_all])
    pm, pv = _pack([given["m_" + k] for k in small_all]), _pack([given["v_" + k] for k in small_all])
    d_, m_, v_ = _adamw(pw, pg, pm, pv, "adamw_small")
    for k, dd, mm, vv in zip(small_all, _unpack(d_, shapes), _unpack(m_, shapes), _unpack(v_, shapes)):
        deltas[k], new_m[k], new_v[k] = dd, mm, vv
        grads[k] = grads[k].reshape(given[k].shape)
    rs_end("in", rs_in, all_of(v_, *marks))
    adamw_big("w_in")

    names = ["w_ada", "b_ada", "norm1_g", "w_in", "b_in", "attn_sinks", "rel_bias", "lambda_re", "lambda_im",
             "log_step", "ssm_b_re", "ssm_b_im", "ssm_c_re", "ssm_c_im", "ssm_d", "w_glu", "b_glu", "w_attn_proj",
             "w_ssm_proj", "w_out", "norm2_g", "w_ff1", "w_ff2", "final_g"]
    return (loss, grad_x[None], *[grads[n] for n in names], *[deltas[n] for n in names],
            *[new_m[n] for n in names], *[new_v[n] for n in names])
```

```python
import math

import numpy as np
import jax
import jax.numpy as jnp
from jax import lax
from jax.experimental import pallas as pl
from jax.experimental.pallas import tpu as pltpu

F32 = jnp.float32
BF16 = jnp.bfloat16
MESH = pl.DeviceIdType.MESH

HEAD_DIM = 64
N_Q_HEADS = 16
N_KV_HEADS = 4
GQA_GROUP = N_Q_HEADS // N_KV_HEADS
ATTN_WIDTH = N_Q_HEADS * HEAD_DIM
KV_WIDTH = N_KV_HEADS * HEAD_DIM
BLOCK = 128
NUM_BUCKETS = 32
MAX_DISTANCE = 128
NEG_INF = -1e30
SSM_GROUP_CH = 16
SSM_STATE = 64
EPS = 1e-6
ADAM_LR = 0.001
ADAM_B1 = 0.9
ADAM_B2 = 0.999
ADAM_EPS = 1e-08
ADAM_WD = 0.01
ADAM_STEP = 10

N_CHIPS = 4
N_DEV = 8
SCAN_CHUNKS = 8
VMEM_LIMIT_BYTES = 48 * 1024 * 1024


def _cparams(sem=None):
    return pltpu.CompilerParams(dimension_semantics=sem, vmem_limit_bytes=VMEM_LIMIT_BYTES)


class _Op:
    def __init__(self, arr, nsh=None, coff=0):
        self.arr, self.nsh, self.coff = arr, nsh, coff
        if nsh is None:
            self.rows, self.cols = arr.shape
        else:
            assert arr.shape[0] == nsh
            self.rows, self.cols = arr.shape[1], arr.shape[2] * nsh

    def spec(self, br, bc, idx):
        assert self.coff % bc == 0
        off = self.coff // bc
        if self.nsh is None:
            return pl.BlockSpec((br, bc), lambda *g: (idx(*g)[0], idx(*g)[1] + off))
        per = (self.cols // self.nsh) // bc
        assert per * bc * self.nsh == self.cols

        def imap(*g):
            r, c = idx(*g)
            c = c + off
            return (c // per, r, c % per)
        return pl.BlockSpec((None, br, bc), imap)


def _as_op(a):
    return a if isinstance(a, _Op) else _Op(a)


def _mm(a, b, mode, *, name, M, N, K, out_dtypes=(F32,), out_nsh=None, epilogue=None, extras=(),
        a_fn=None, ti=1024, tj=512, tk=2048, a_idx=None, b_idx=None):
    a, b = _as_op(a), _as_op(b)
    ti, tj, tk = min(ti, M), min(tj, N), min(tk, K)
    a_w = a.cols // a.nsh if a.nsh else None
    b_w = b.cols // b.nsh if b.nsh else None
    if a_w:
        ti, tk = (min(ti, a_w), tk) if mode == "TN" else (ti, min(tk, a_w))
    if b_w:
        tj, tk = (tj, min(tk, b_w)) if mode == "NT" else (min(tj, b_w), tk)
    if out_nsh:
        tj = min(tj, N // out_nsh)
    assert M % ti == 0 and N % tj == 0 and K % tk == 0, (name, M, N, K, ti, tj, tk)
    nk = K // tk
    if mode == "NN":
        a_spec = a.spec(ti, tk, a_idx or (lambda i, j, k: (i, k)))
        b_spec = b.spec(tk, tj, b_idx or (lambda i, j, k: (k, j)))
        dims = (((1,), (0,)), ((), ()))
    elif mode == "NT":
        a_spec = a.spec(ti, tk, a_idx or (lambda i, j, k: (i, k)))
        b_spec = b.spec(tj, tk, b_idx or (lambda i, j, k: (j, k)))
        dims = (((1,), (1,)), ((), ()))
    else:
        a_spec = a.spec(tk, ti, a_idx or (lambda i, j, k: (k, i)))
        b_spec = b.spec(tk, tj, b_idx or (lambda i, j, k: (k, j)))
        dims = (((0,), (0,)), ((), ()))
    ex_specs, ex_arrs = [], []
    for op, kind in extras:
        op = _as_op(op)
        if kind == "tile":
            ex_specs.append(op.spec(ti, tj, lambda i, j, k: (i, j)))
        else:
            ex_specs.append(op.spec(1, tj, lambda i, j, k: (0, j)))
        ex_arrs.append(op.arr)
    ne, no = len(ex_arrs), len(out_dtypes)
    if out_nsh is None:
        out_shapes = [jax.ShapeDtypeStruct((M, N), d) for d in out_dtypes]
        out_specs = [pl.BlockSpec((ti, tj), lambda i, j, k: (i, j)) for _ in out_dtypes]
    else:
        per = (N // out_nsh) // tj
        assert per * tj * out_nsh == N
        out_shapes = [jax.ShapeDtypeStruct((out_nsh, M, N // out_nsh), d) for d in out_dtypes]
        out_specs = [pl.BlockSpec((None, ti, tj), lambda i, j, k: (j // per, i, j % per)) for _ in out_dtypes]

    def body(a_ref, b_ref, *rest):
        ex_refs, out_refs, acc = rest[:ne], rest[ne:ne + no], rest[ne + no]
        k = pl.program_id(2)

        @pl.when(k == 0)
        def _():
            acc[...] = jnp.zeros_like(acc)

        av = a_ref[...]
        if a_fn is not None:
            av = a_fn(av)
        acc[...] += lax.dot_general(av.astype(BF16), b_ref[...].astype(BF16), dims,
                                    preferred_element_type=F32)

        @pl.when(k == nk - 1)
        def _():
            res = acc[...]
            outs = epilogue(res, *[r[...] for r in ex_refs]) if epilogue is not None else (res,)
            for o_ref, o in zip(out_refs, outs):
                o_ref[...] = o.astype(o_ref.dtype)

    outs = pl.pallas_call(
        body, name=name, grid=(M // ti, N // tj, nk),
        in_specs=[a_spec, b_spec] + ex_specs, out_specs=out_specs, out_shape=out_shapes,
        scratch_shapes=[pltpu.VMEM((ti, tj), F32)],
        compiler_params=_cparams(("parallel", "parallel", "arbitrary")),
    )(a.arr, b.arr, *ex_arrs)
    return outs[0] if no == 1 else outs


def _rowwise(fn, ins, outs, accs, *, name, rows, tr=256):
    tr = min(tr, rows)
    assert rows % tr == 0
    in_specs, arrs = [], []
    for op, kind, width in ins:
        op = _as_op(op)
        if kind == "tile":
            in_specs.append(op.spec(tr, width, lambda i: (i, 0)))
        else:
            in_specs.append(op.spec(op.rows, width, lambda i: (0, 0)))
        arrs.append(op.arr)
    ni, no, na = len(ins), len(outs), len(accs)
    flipped = [len(o) == 3 for o in outs]
    out_shapes = [jax.ShapeDtypeStruct((o[0], rows) if t else (rows, o[0]), o[1]) for o, t in zip(outs, flipped)]
    out_specs = [pl.BlockSpec((o[0], tr), lambda i: (0, i)) if t else pl.BlockSpec((tr, o[0]), lambda i: (i, 0))
                 for o, t in zip(outs, flipped)]
    out_shapes += [jax.ShapeDtypeStruct((1, w), F32) for w in accs]
    out_specs += [pl.BlockSpec((1, w), lambda i: (0, 0)) for w in accs]

    def body(*refs):
        in_refs, out_refs, acc_refs = refs[:ni], refs[ni:ni + no], refs[ni + no:]
        res = fn(*[r[...] for r in in_refs])
        if not isinstance(res, (tuple, list)):
            res = (res,)
        for o_ref, r, t in zip(out_refs, res[:no], flipped):
            o_ref[...] = (r.astype(F32).T if t else r).astype(o_ref.dtype)
        if na:
            @pl.when(pl.program_id(0) == 0)
            def _():
                for a_ref in acc_refs:
                    a_ref[...] = jnp.zeros_like(a_ref)
            for a_ref, r in zip(acc_refs, res[no:]):
                a_ref[...] += r.astype(F32)

    res = pl.pallas_call(
        body, name=name, grid=(rows // tr,), in_specs=in_specs, out_specs=out_specs, out_shape=out_shapes,
        compiler_params=_cparams(("arbitrary",)),
    )(*arrs)
    return res


def _norm_mod(x, g, sh, sc):
    y = x * lax.rsqrt(jnp.mean(x * x, axis=-1, keepdims=True) + EPS) * g
    return y * (1.0 + sc) + sh


def _sigmoid(x):
    return 1.0 / (1.0 + jnp.exp(-x))


def _silu(x):
    return x * _sigmoid(x)


def _gelu(x):
    return 0.5 * x * (1.0 + jnp.tanh(math.sqrt(2.0 / math.pi) * (x + 0.044715 * (x * x * x))))


def _merge(ga, gs, ya, ys):
    return _sigmoid(ga) * ya + _sigmoid(gs) * ys


def _attn_head(q, kp, kc, vp, vc, sink, bias_p, bias_c, not_first):
    nt = (((1,), (1,)), ((), ()))
    nn = (((1,), (0,)), ((), ()))
    qb = q.astype(BF16)
    scale = HEAD_DIM ** -0.5
    sp = lax.dot_general(qb, kp.astype(BF16), nt, preferred_element_type=F32) * scale + bias_p
    sc = lax.dot_general(qb, kc.astype(BF16), nt, preferred_element_type=F32) * scale + bias_c
    qi = lax.broadcasted_iota(jnp.int32, sp.shape, 0) & (BLOCK - 1)
    ki = lax.broadcasted_iota(jnp.int32, sp.shape, 1)
    sp = jnp.where(jnp.logical_and(ki > qi, not_first), sp, NEG_INF)
    sc = jnp.where(ki <= qi, sc, NEG_INF)
    m = jnp.maximum(jnp.maximum(jnp.max(sp, axis=-1, keepdims=True), jnp.max(sc, axis=-1, keepdims=True)), sink)
    m = lax.stop_gradient(m)
    pp = jnp.exp(sp - m)
    pc = jnp.exp(sc - m)
    denom = jnp.sum(pp, axis=-1, keepdims=True) + jnp.sum(pc, axis=-1, keepdims=True) + jnp.exp(sink - m)
    o = lax.dot_general((pp / denom).astype(BF16), vp.astype(BF16), nn, preferred_element_type=F32)
    o = o + lax.dot_general((pc / denom).astype(BF16), vc.astype(BF16), nn, preferred_element_type=F32)
    return o


def _attn_fwd(qh, kh, vh, sinks, bias, name):
    s = qh.shape[1]
    nb = s // BLOCK
    G = GQA_GROUP
    R = G * BLOCK

    def body(q_ref, kp_ref, kc_ref, vp_ref, vc_ref, sink_ref, bias_ref, o_ref):
        not_first = pl.program_id(0) > 0
        for kv in range(N_KV_HEADS):
            hs = slice(kv * G, (kv + 1) * G)
            o = _attn_head(q_ref[hs].reshape(R, HEAD_DIM), kp_ref[kv], kc_ref[kv], vp_ref[kv], vc_ref[kv],
                           sink_ref[kv * R:(kv + 1) * R, 0:1],
                           bias_ref[hs, :, 0:BLOCK].reshape(R, BLOCK), bias_ref[hs, :, BLOCK:2 * BLOCK].reshape(R, BLOCK),
                           not_first)
            o_ref[hs] = o.reshape(G, BLOCK, HEAD_DIM).astype(o_ref.dtype)

    cur = lambda i: (0, i, 0)
    prev = lambda i: (0, jnp.maximum(i - 1, 0), 0)
    return pl.pallas_call(
        body, name=name, grid=(nb,),
        in_specs=[pl.BlockSpec((N_Q_HEADS, BLOCK, HEAD_DIM), cur),
                  pl.BlockSpec((N_KV_HEADS, BLOCK, HEAD_DIM), prev), pl.BlockSpec((N_KV_HEADS, BLOCK, HEAD_DIM), cur),
                  pl.BlockSpec((N_KV_HEADS, BLOCK, HEAD_DIM), prev), pl.BlockSpec((N_KV_HEADS, BLOCK, HEAD_DIM), cur),
                  pl.BlockSpec((N_Q_HEADS * BLOCK, 128), lambda i: (0, 0)),
                  pl.BlockSpec((N_Q_HEADS, BLOCK, 2 * BLOCK), lambda i: (0, 0, 0))],
        out_specs=pl.BlockSpec((N_Q_HEADS, BLOCK, HEAD_DIM), cur),
        out_shape=jax.ShapeDtypeStruct((N_Q_HEADS, s, HEAD_DIM), BF16),
        compiler_params=_cparams(("arbitrary",)),
    )(qh, kh, kh, vh, vh, sinks, bias)


def _attn_bwd(qh, kh, vh, doh, sinks, bias, name):
    s = qh.shape[1]
    nb = s // BLOCK
    G = GQA_GROUP
    R = G * BLOCK

    def body(q_ref, kp_ref, kc_ref, vp_ref, vc_ref, do_ref, sink_ref, bias_ref,
             dq_ref, dk_ref, dv_ref, dsink_ref, dbias_ref, ck, cv):
        i = pl.program_id(1)

        @pl.when(i == 0)
        def _():
            dsink_ref[...] = jnp.zeros_like(dsink_ref)
            dbias_ref[...] = jnp.zeros_like(dbias_ref)
            ck[...] = jnp.zeros_like(ck)
            cv[...] = jnp.zeros_like(cv)

        @pl.when(i < nb)
        def _():
            not_first = i > 0
            _, vjp = jax.vjp(lambda q, a, b, c, d, sk, e, f: _attn_head(q, a, b, c, d, sk, e, f, not_first),
                             q_ref[...].reshape(R, HEAD_DIM), kp_ref[...], kc_ref[...], vp_ref[...], vc_ref[...],
                             sink_ref[:, 0:1], bias_ref[:, :, 0:BLOCK].reshape(R, BLOCK),
                             bias_ref[:, :, BLOCK:2 * BLOCK].reshape(R, BLOCK))
            dq, dkp, dkc, dvp, dvc, dsk, dbp, dbc = vjp(do_ref[...].reshape(R, HEAD_DIM).astype(F32))
            dq_ref[...] = dq.reshape(G, BLOCK, HEAD_DIM)
            dsink_ref[...] += jnp.broadcast_to(dsk, (R, 128))
            dbias_ref[:, :, 0:BLOCK] += dbp.reshape(G, BLOCK, BLOCK)
            dbias_ref[:, :, BLOCK:2 * BLOCK] += dbc.reshape(G, BLOCK, BLOCK)
            dk_ref[...] = ck[...] + dkp
            dv_ref[...] = cv[...] + dvp
            ck[...] = dkc
            cv[...] = dvc

        @pl.when(i == nb)
        def _():
            dk_ref[...] = ck[...]
            dv_ref[...] = cv[...]

    qcur = lambda kv, i: (kv, jnp.minimum(i, nb - 1), 0)
    kcur = lambda kv, i: (kv, jnp.minimum(i, nb - 1), 0)
    kprev = lambda kv, i: (kv, jnp.clip(i - 1, 0, nb - 1), 0)
    qspec = pl.BlockSpec((G, BLOCK, HEAD_DIM), qcur)
    kc_spec = pl.BlockSpec((None, BLOCK, HEAD_DIM), kcur)
    kp_spec = pl.BlockSpec((None, BLOCK, HEAD_DIM), kprev)
    return pl.pallas_call(
        body, name=name, grid=(N_KV_HEADS, nb + 1),
        in_specs=[qspec, kp_spec, kc_spec, kp_spec, kc_spec, qspec,
                  pl.BlockSpec((R, 128), lambda kv, i: (kv, 0)),
                  pl.BlockSpec((G, BLOCK, 2 * BLOCK), lambda kv, i: (kv, 0, 0))],
        out_specs=[qspec, kp_spec, kp_spec,
                   pl.BlockSpec((R, 128), lambda kv, i: (kv, 0)),
                   pl.BlockSpec((G, BLOCK, 2 * BLOCK), lambda kv, i: (kv, 0, 0))],
        out_shape=[jax.ShapeDtypeStruct((N_Q_HEADS, s, HEAD_DIM), F32),
                   jax.ShapeDtypeStruct((N_KV_HEADS, s, HEAD_DIM), F32),
                   jax.ShapeDtypeStruct((N_KV_HEADS, s, HEAD_DIM), F32),
                   jax.ShapeDtypeStruct((N_Q_HEADS * BLOCK, 128), F32),
                   jax.ShapeDtypeStruct((N_Q_HEADS, BLOCK, 2 * BLOCK), F32)],
        scratch_shapes=[pltpu.VMEM((BLOCK, HEAD_DIM), F32), pltpu.VMEM((BLOCK, HEAD_DIM), F32)],
        compiler_params=_cparams(("arbitrary", "arbitrary")),
    )(qh, kh, kh, vh, vh, doh, sinks, bias)


def _cmul(ar, ai, br, bi):
    return ar * br - ai * bi, ar * bi + ai * br


def _scan(a, b, xs_prev, *, reverse, name, tc):
    _, s, c = b.shape
    nc = SCAN_CHUNKS
    steps = s // nc
    with_da = xs_prev is not None
    unroll = 8 if steps % 8 == 0 else 1

    def shift(v, d):
        row = lax.broadcasted_iota(jnp.int32, v.shape, 0)
        if reverse:
            return jnp.where(row < nc - d, pltpu.roll(v, nc - d, 0), 0.0)
        return jnp.where(row >= d, pltpu.roll(v, d, 0), 0.0)

    def body(*refs):
        if with_da:
            a_ref, b_ref, xp_ref, x_ref, da_ref = refs
        else:
            a_ref, b_ref, x_ref = refs
        ar = jnp.broadcast_to(a_ref[0], (nc, tc))
        ai = jnp.broadcast_to(a_ref[1], (nc, tc))

        def row_of(step):
            j = (steps - 1 - step) if reverse else step
            return pl.multiple_of(j * nc, nc)

        def p1(step, st):
            sr, si = st
            r0 = row_of(step)
            mr, mi = _cmul(ar, ai, sr, si)
            sr = mr + b_ref[0, pl.ds(r0, nc), :]
            si = mi + b_ref[1, pl.ds(r0, nc), :]
            x_ref[0, pl.ds(r0, nc), :] = sr
            x_ref[1, pl.ds(r0, nc), :] = si
            return sr, si
        zero = jnp.zeros((nc, tc), F32)
        er, ei = lax.fori_loop(0, steps, p1, (zero, zero), unroll=unroll)

        pr, pi_ = jnp.ones((nc, tc), F32), zero
        br, bi, left = ar, ai, steps
        while left:
            if left & 1:
                pr, pi_ = _cmul(pr, pi_, br, bi)
            br, bi = _cmul(br, bi, br, bi)
            left >>= 1
        cr, ci = shift(er, 1), shift(ei, 1)
        d = 1
        while d < nc:
            mr, mi = _cmul(pr, pi_, shift(cr, d), shift(ci, d))
            cr, ci = cr + mr, ci + mi
            pr, pi_ = _cmul(pr, pi_, pr, pi_)
            d *= 2

        def p2(step, st):
            qr, qi, dar, dai = st
            r0 = row_of(step)
            qr, qi = _cmul(ar, ai, qr, qi)
            fr, fi = _cmul(qr, qi, cr, ci)
            xr = x_ref[0, pl.ds(r0, nc), :] + fr
            xi = x_ref[1, pl.ds(r0, nc), :] + fi
            x_ref[0, pl.ds(r0, nc), :] = xr
            x_ref[1, pl.ds(r0, nc), :] = xi
            if with_da:
                jm = jnp.where(step == steps - 1, steps - 1, steps - 2 - step)
                rp = pl.multiple_of(jm * nc, nc)
                vr, vi = xp_ref[0, pl.ds(rp, nc), :], xp_ref[1, pl.ds(rp, nc), :]
                row = lax.broadcasted_iota(jnp.int32, (nc, tc), 0)
                first = step == steps - 1
                sel = jnp.logical_and(first, row == 0)
                vr = jnp.where(sel, 0.0, jnp.where(first, pltpu.roll(vr, 1, 0), vr))
                vi = jnp.where(sel, 0.0, jnp.where(first, pltpu.roll(vi, 1, 0), vi))
                dar = dar + xr * vr + xi * vi
                dai = dai + xi * vr - xr * vi
            return qr, qi, dar, dai
        _, _, dar, dai = lax.fori_loop(0, steps, p2, (jnp.ones((nc, tc), F32), zero, zero, zero), unroll=unroll)
        if with_da:
            da_ref[0] = jnp.sum(dar, axis=0, keepdims=True)
            da_ref[1] = jnp.sum(dai, axis=0, keepdims=True)

    blk = pl.BlockSpec((2, s, tc), lambda i: (0, 0, i))
    vec = pl.BlockSpec((2, 1, tc), lambda i: (0, 0, i))
    in_specs, args = [vec, blk], [a, b]
    out_specs, out_shape = [blk], [jax.ShapeDtypeStruct((2, s, c), F32)]
    if with_da:
        in_specs.append(blk)
        args.append(xs_prev)
        out_specs.append(vec)
        out_shape.append(jax.ShapeDtypeStruct((2, 1, c), F32))
    res = pl.pallas_call(
        body, name=name, grid=(c // tc,), in_specs=in_specs, out_specs=out_specs, out_shape=out_shape,
        compiler_params=_cparams(("arbitrary",)),
    )(*args)
    return res if with_da else res[0]


def _adamw(w, g, m, v, name):
    r, c = w.shape
    tr = r
    for cand in (512, 256, 128, 64, 32, 16, 8):
        if r % cand == 0 and cand * c * 4 <= 2 * 1024 * 1024:
            tr = cand
            break

    def body(w_ref, g_ref, m_ref, v_ref, d_ref, nm_ref, nv_ref):
        gv = g_ref[...]
        nm = ADAM_B1 * m_ref[...] + (1.0 - ADAM_B1) * gv
        nv = ADAM_B2 * v_ref[...] + (1.0 - ADAM_B2) * (gv * gv)
        m_hat = nm / (1.0 - ADAM_B1 ** ADAM_STEP)
        v_hat = nv / (1.0 - ADAM_B2 ** ADAM_STEP)
        d_ref[...] = -ADAM_LR * (m_hat / (jnp.sqrt(v_hat) + ADAM_EPS) + ADAM_WD * w_ref[...])
        nm_ref[...] = nm
        nv_ref[...] = nv

    spec = pl.BlockSpec((tr, c), lambda i: (i, 0))
    sds = jax.ShapeDtypeStruct((r, c), F32)
    return pl.pallas_call(body, name=name, grid=(r // tr,), in_specs=[spec] * 4, out_specs=[spec] * 3,
                          out_shape=[sds] * 3, compiler_params=_cparams(("parallel",)))(w, g, m, v)


def _sum_lead(x, name, out_dtype=F32):
    n, r, c = x.shape
    tr = r
    for cand in (512, 256, 128, 64, 32, 16, 8):
        if r % cand == 0 and n * cand * c * 4 <= 4 * 1024 * 1024:
            tr = cand
            break

    def body(x_ref, o_ref):
        acc = x_ref[0].astype(F32)
        for k in range(1, n):
            acc = acc + x_ref[k].astype(F32)
        o_ref[...] = acc.astype(o_ref.dtype)

    return pl.pallas_call(body, name=name, grid=(r // tr,),
                          in_specs=[pl.BlockSpec((n, tr, c), lambda i: (0, i, 0))],
                          out_specs=pl.BlockSpec((tr, c), lambda i: (i, 0)),
                          out_shape=jax.ShapeDtypeStruct((r, c), out_dtype),
                          compiler_params=_cparams(("parallel",)))(x)


def _row_tile(rows, row_bytes, budget, least=8):
    for cand in (1024, 512, 256, 128, 64, 32, 16, 8):
        if cand >= least and rows % cand == 0 and cand * row_bytes <= budget:
            return cand
    return rows


def _cast_into_slot(w, slot, name):
    r, c = w.shape
    tr = _row_tile(r, c * 4, 4 * 1024 * 1024, least=16)

    def body(slot_ref, w_ref, o_ref):
        o_ref[...] = w_ref[...].astype(o_ref.dtype)

    gs = pltpu.PrefetchScalarGridSpec(
        num_scalar_prefetch=1, grid=(r // tr,),
        in_specs=[pl.BlockSpec((tr, c), lambda i, s: (i, 0))],
        out_specs=pl.BlockSpec((None, tr, c), lambda i, s: (s[0], i, 0)))
    return pl.pallas_call(body, name=name, grid_spec=gs, out_shape=jax.ShapeDtypeStruct((N_CHIPS, r, c), BF16),
                          compiler_params=_cparams(("parallel",)))(slot, w)


def _sum_own(p, t, sel, name):
    _, h, c = p.shape
    tr = _row_tile(h, c * 4, 2 * 1024 * 1024, least=16)
    nblk = h // tr

    def body(sel_ref, p_ref, t_ref, o_ref):
        acc = p_ref[...].astype(F32)
        for k in range(3):
            acc = acc + t_ref[k].astype(F32)
        o_ref[...] = acc

    gs = pltpu.PrefetchScalarGridSpec(
        num_scalar_prefetch=1, grid=(nblk,),
        in_specs=[pl.BlockSpec((None, tr, c), lambda i, s: (s[0], i, 0)),
                  pl.BlockSpec((3, tr, c), lambda i, s: (0, i, 0))],
        out_specs=pl.BlockSpec((tr, c), lambda i, s: (s[1] * nblk + i, 0)))
    return pl.pallas_call(body, name=name, grid_spec=gs, out_shape=jax.ShapeDtypeStruct((2 * h, c), F32),
                          compiler_params=_cparams(("parallel",)))(sel, p, t)


def _add_half(g, t, half, name):
    n, r, c = g.shape
    h = r // 2
    tr = h
    for cand in (512, 256, 128, 64, 32, 16):
        if h % cand == 0 and cand * c * 2 <= 2 * 1024 * 1024:
            tr = cand
            break
    nblk = h // tr

    def body(half_ref, g_ref, t_ref, o_ref):
        o_ref[...] = (g_ref[...].astype(F32) + t_ref[...].astype(F32)).astype(o_ref.dtype)

    gs = pltpu.PrefetchScalarGridSpec(
        num_scalar_prefetch=1, grid=(n, nblk),
        in_specs=[pl.BlockSpec((None, tr, c), lambda j, i, hr: (j, hr[0] * nblk + i, 0)),
                  pl.BlockSpec((None, tr, c), lambda j, i, hr: (j, i, 0))],
        out_specs=pl.BlockSpec((None, tr, c), lambda j, i, hr: (j, i, 0)))
    return pl.pallas_call(body, name=name, grid_spec=gs, out_shape=jax.ShapeDtypeStruct((n, h, c), BF16),
                          compiler_params=_cparams(("parallel", "parallel")))(half, g, t)


def _position():
    x, y, c = lax.axis_index("x"), lax.axis_index("y"), lax.axis_index("c")
    return x, y, c


def _allgather8(xs, name):
    m_per, n = xs.shape

    def body(x_ref, out_ref, send_sems, recv_sems, local_sem):
        x, y, c = _position()
        me, sibling = (x, y, c), (x, y, 1 - c)
        chips = [(1 - x, y), (x, 1 - y), (1 - x, 1 - y)]

        def rows(px, py, pc):
            return out_ref.at[pl.ds((4 * px + 2 * py + pc) * m_per, m_per), :]

        def copy(k, block, to, src=None):
            return pltpu.make_async_remote_copy(
                src_ref=rows(*block) if src is None else src, dst_ref=rows(*block),
                send_sem=send_sems.at[k], recv_sem=recv_sems.at[k], device_id=to, device_id_type=MESH)

        mine = pltpu.make_async_copy(x_ref, rows(*me), local_sem)
        mine.start()
        first = [copy(0, me, sibling, src=x_ref)]
        first += [copy(1 + j, me, (*chip, c), src=x_ref) for j, chip in enumerate(chips)]
        for cp in first:
            cp.start()
        passed = [copy(4 + j, (*chip, c), sibling) for j, chip in enumerate(chips)]
        for j, chip in enumerate(chips):
            copy(1 + j, (*chip, c), me).wait_recv()
            passed[j].start()
        copy(0, sibling, me).wait_recv()
        for j, chip in enumerate(chips):
            copy(4 + j, (*chip, 1 - c), me).wait_recv()
        for cp in first + passed:
            cp.wait_send()
        mine.wait()

    return pl.pallas_call(
        body, name=name, out_shape=jax.ShapeDtypeStruct((N_DEV * m_per, n), xs.dtype),
        in_specs=[pl.BlockSpec(memory_space=pltpu.VMEM)], out_specs=pl.BlockSpec(memory_space=pltpu.VMEM),
        scratch_shapes=[pltpu.SemaphoreType.DMA((7,)), pltpu.SemaphoreType.DMA((7,)), pltpu.SemaphoreType.DMA],
        compiler_params=pltpu.CompilerParams(vmem_limit_bytes=VMEM_LIMIT_BYTES),
    )(xs)


_HBM = pl.BlockSpec(memory_space=pltpu.HBM)


_SEM = pl.BlockSpec(memory_space=pltpu.SEMAPHORE)
_ANY = pl.BlockSpec(memory_space=pl.ANY)
_EFFECT = pltpu.SideEffectType.DATAFLOW_SIDE_EFFECTING


def _in_hbm(a):
    return pltpu.with_memory_space_constraint(a, pltpu.HBM)


def _gather_start(ws, groups, after, name):
    n = len(ws)

    def body(*refs):
        in_refs = refs[:n]
        sems, token = refs[2 * n + 1:-1], refs[-1]
        x, y, c = _position()
        mychip = 2 * x + y
        chips = [(1 - x, y), (x, 1 - y), (1 - x, 1 - y)]
        for g, members in enumerate(groups):
            for k, i in enumerate(members):
                h = ws[i].shape[1] // 2
                mine = in_refs[i].at[mychip, pl.ds(c * h, h), :]
                for j, (px, py) in enumerate(chips):
                    pltpu.make_async_remote_copy(
                        src_ref=mine, dst_ref=mine, send_sem=sems[2 * g].at[3 * k + j],
                        recv_sem=sems[2 * g + 1].at[3 * k + j], device_id=(px, py, c), device_id_type=MESH).start()
        token[...] = jnp.zeros_like(token)

    sem_shapes = [pltpu.SemaphoreType.DMA((3 * len(m),)) for m in groups for _ in range(2)]
    res = pl.pallas_call(
        body, name=name,
        out_shape=[pltpu.HBM(w.shape, w.dtype) for w in ws] + sem_shapes + [jax.ShapeDtypeStruct((8, 128), F32)],
        in_specs=[_HBM] * n + [_ANY],
        out_specs=[_HBM] * n + [_SEM] * len(sem_shapes) + [pl.BlockSpec(memory_space=pltpu.VMEM)],
        input_output_aliases={i: i for i in range(n)},
        compiler_params=pltpu.CompilerParams(has_side_effects=_EFFECT),
    )(*[_in_hbm(w) for w in ws], after)
    bufs, sems, token = res[:n], res[n:-1], res[-1]
    return list(bufs), [(sems[2 * g], sems[2 * g + 1]) for g in range(len(groups))], token


def _gather_wait(bufs, send_sems, recv_sems, after, name):
    m = len(bufs)

    def body(*refs):
        in_refs = refs[:m]
        send, recv = refs[m], refs[m + 1]
        x, y, c = _position()
        mychip = 2 * x + y
        chips = [(1 - x, y), (x, 1 - y), (1 - x, 1 - y)]
        for k in range(m):
            h = bufs[k].shape[1] // 2
            mine = in_refs[k].at[mychip, pl.ds(c * h, h), :]
            for j, (px, py) in enumerate(chips):
                cp = pltpu.make_async_remote_copy(
                    src_ref=mine, dst_ref=in_refs[k].at[2 * px + py, pl.ds(c * h, h), :],
                    send_sem=send.at[3 * k + j], recv_sem=recv.at[3 * k + j],
                    device_id=(px, py, c), device_id_type=MESH)
                cp.wait_send()
                cp.wait_recv()

    res = pl.pallas_call(
        body, name=name, out_shape=[pltpu.HBM(b.shape, b.dtype) for b in bufs],
        in_specs=[_HBM] * m + [_SEM, _SEM, _ANY], out_specs=[_HBM] * m,
        input_output_aliases={k: k for k in range(m)},
        compiler_params=pltpu.CompilerParams(has_side_effects=_EFFECT),
    )(*bufs, send_sems, recv_sems, after)
    return list(res)


def _forward_halves(ws, name):
    n = len(ws)

    def body(*refs):
        out_refs = refs[n:2 * n]
        send_sems, recv_sems = refs[2 * n:]
        x, y, c = _position()
        me, sibling = (x, y, c), (x, y, 1 - c)
        chips = [(1 - x, y), (x, 1 - y), (1 - x, 1 - y)]
        cps = []
        for i in range(n):
            h = ws[i].shape[1] // 2
            for j, (px, py) in enumerate(chips):
                got = out_refs[i].at[2 * px + py, pl.ds(c * h, h), :]
                cp = pltpu.make_async_remote_copy(
                    src_ref=got, dst_ref=got, send_sem=send_sems.at[3 * i + j], recv_sem=recv_sems.at[3 * i + j],
                    device_id=sibling, device_id_type=MESH)
                cp.start()
                cps.append(cp)
        for i in range(n):
            h = ws[i].shape[1] // 2
            for j, (px, py) in enumerate(chips):
                other = out_refs[i].at[2 * px + py, pl.ds((1 - c) * h, h), :]
                pltpu.make_async_remote_copy(
                    src_ref=other, dst_ref=other, send_sem=send_sems.at[3 * i + j], recv_sem=recv_sems.at[3 * i + j],
                    device_id=me, device_id_type=MESH).wait_recv()
        for cp in cps:
            cp.wait_send()

    return pl.pallas_call(
        body, name=name,
        out_shape=[jax.ShapeDtypeStruct(w.shape, w.dtype) for w in ws],
        in_specs=[_HBM] * n, out_specs=[_HBM] * n, input_output_aliases={i: i for i in range(n)},
        scratch_shapes=[pltpu.SemaphoreType.DMA((3 * n,)), pltpu.SemaphoreType.DMA((3 * n,))],
    )(*ws)


def _swap_halves(gs, name):
    n = len(gs)

    def body(*refs):
        in_refs, out_refs = refs[:n], refs[n:2 * n]
        send_sems, recv_sems = refs[2 * n:]
        x, y, c = _position()
        cps = []
        for i in range(n):
            h = gs[i].shape[1] // 2
            cp = pltpu.make_async_remote_copy(
                src_ref=in_refs[i].at[:, pl.ds((1 - c) * h, h), :], dst_ref=out_refs[i],
                send_sem=send_sems.at[i], recv_sem=recv_sems.at[i], device_id=(x, y, 1 - c), device_id_type=MESH)
            cp.start()
            cps.append(cp)
        for cp in cps:
            cp.wait()

    return pl.pallas_call(
        body, name=name,
        out_shape=[jax.ShapeDtypeStruct((g.shape[0], g.shape[1] // 2, g.shape[2]), g.dtype) for g in gs],
        in_specs=[_HBM] * n, out_specs=[_HBM] * n,
        scratch_shapes=[pltpu.SemaphoreType.DMA((n,)), pltpu.SemaphoreType.DMA((n,))],
    )(*gs)


def _scatter_copies(p_refs, land_refs, send, recv):
    x, y, c = _position()
    chips = [(1 - x, y), (x, 1 - y), (1 - x, 1 - y)]
    return [pltpu.make_async_remote_copy(
        src_ref=p_refs[i].at[2 * px + py], dst_ref=land_refs[i].at[j],
        send_sem=send.at[3 * i + j], recv_sem=recv.at[3 * i + j], device_id=(px, py, c), device_id_type=MESH)
        for i in range(len(p_refs)) for j, (px, py) in enumerate(chips)]


def _scatter_start(ps, name):
    n = len(ps)
    lands = [lax.empty((3,) + p.shape[1:], p.dtype) for p in ps]

    def body(*refs):
        for cp in _scatter_copies(refs[:n], refs[n:2 * n], refs[4 * n], refs[4 * n + 1]):
            cp.start()
        refs[4 * n + 2][...] = jnp.zeros_like(refs[4 * n + 2])

    res = pl.pallas_call(
        body, name=name,
        out_shape=[pltpu.HBM(a.shape, a.dtype) for a in list(ps) + lands]
        + [pltpu.SemaphoreType.DMA((3 * n,)), pltpu.SemaphoreType.DMA((3 * n,)), jax.ShapeDtypeStruct((8, 128), F32)],
        in_specs=[_HBM] * (2 * n),
        out_specs=[_HBM] * (2 * n) + [_SEM, _SEM, pl.BlockSpec(memory_space=pltpu.VMEM)],
        input_output_aliases={i: i for i in range(2 * n)},
        compiler_params=pltpu.CompilerParams(has_side_effects=_EFFECT),
    )(*[_in_hbm(a) for a in list(ps) + lands])
    return list(res[:n]), list(res[n:2 * n]), res[2 * n], res[2 * n + 1], res[2 * n + 2]


def _scatter_wait(ps, lands, send_sems, recv_sems, after, name):
    n = len(ps)

    def body(*refs):
        for cp in _scatter_copies(refs[:n], refs[n:2 * n], refs[2 * n], refs[2 * n + 1]):
            cp.wait_send()
            cp.wait_recv()

    res = pl.pallas_call(
        body, name=name, out_shape=[pltpu.HBM(a.shape, a.dtype) for a in list(ps) + list(lands)],
        in_specs=[_HBM] * (2 * n) + [_SEM, _SEM, _ANY], out_specs=[_HBM] * (2 * n),
        input_output_aliases={i: i for i in range(2 * n)},
        compiler_params=pltpu.CompilerParams(has_side_effects=_EFFECT),
    )(*ps, *lands, send_sems, recv_sems, after)
    return list(res[:n]), list(res[n:])


def _join_halves(rs, name):
    n = len(rs)

    def body(*refs):
        out_refs = refs[n:2 * n]
        send_sems, recv_sems = refs[2 * n:]
        x, y, c = _position()
        cps = []
        for i in range(n):
            h = rs[i].shape[0] // 2
            mine = out_refs[i].at[pl.ds(c * h, h), :]
            cp = pltpu.make_async_remote_copy(
                src_ref=mine, dst_ref=mine, send_sem=send_sems.at[i], recv_sem=recv_sems.at[i],
                device_id=(x, y, 1 - c), device_id_type=MESH)
            cp.start()
            cps.append(cp)
        for i in range(n):
            h = rs[i].shape[0] // 2
            other = out_refs[i].at[pl.ds((1 - c) * h, h), :]
            pltpu.make_async_remote_copy(
                src_ref=other, dst_ref=other, send_sem=send_sems.at[i], recv_sem=recv_sems.at[i],
                device_id=(x, y, c), device_id_type=MESH).wait_recv()
        for cp in cps:
            cp.wait_send()

    return pl.pallas_call(
        body, name=name,
        out_shape=[jax.ShapeDtypeStruct(r.shape, r.dtype) for r in rs],
        in_specs=[_HBM] * n, out_specs=[_HBM] * n, input_output_aliases={i: i for i in range(n)},
        scratch_shapes=[pltpu.SemaphoreType.DMA((n,)), pltpu.SemaphoreType.DMA((n,))],
    )(*rs)


def _t5_buckets_block():
    qi = np.arange(BLOCK)[:, None]
    ki = np.arange(2 * BLOCK)[None, :]
    n = np.maximum(qi + BLOCK - ki, 0)
    max_exact = NUM_BUCKETS // 2
    large = max_exact + (np.log(np.maximum(n, 1) / max_exact) / np.log(MAX_DISTANCE / max_exact)
                         * (NUM_BUCKETS - max_exact)).astype(np.int32)
    large = np.minimum(large, NUM_BUCKETS - 1)
    return np.where(n < max_exact, n, large).astype(np.int32)


def _discretise(lambda_re, lambda_im, log_step, b_re, b_im):
    lam_re = jnp.minimum(lambda_re, -1e-4)
    lam_im = lambda_im
    delta = jnp.exp(log_step)[:, None]
    mag = jnp.exp(lam_re * delta)
    ang = lam_im * delta
    abar_re, abar_im = mag * jnp.cos(ang), mag * jnp.sin(ang)
    num_re, num_im = abar_re - 1.0, abar_im
    den = lam_re * lam_re + lam_im * lam_im
    f_re = (num_re * lam_re + num_im * lam_im) / den
    f_im = (num_im * lam_re - num_re * lam_im) / den
    bbar_re = f_re[..., None] * b_re - f_im[..., None] * b_im
    bbar_im = f_re[..., None] * b_im + f_im[..., None] * b_re
    return abar_re, abar_im, bbar_re, bbar_im


def _interleave(v, nc):
    s, w = v.shape
    return v.reshape(nc, s // nc, w).transpose(1, 0, 2).reshape(s, w)


def _deinterleave(v, nc):
    s, w = v.shape
    return v.reshape(s // nc, nc, w).transpose(1, 0, 2).reshape(s, w)


_SMALL = ("norm1_g", "b_in", "attn_sinks", "rel_bias", "lambda_re", "lambda_im", "log_step", "ssm_b_re",
          "ssm_b_im", "ssm_c_re", "ssm_c_im", "ssm_d", "b_glu", "norm2_g", "final_g")


def _pack(parts):
    rows = []
    for p in parts:
        f = p.reshape(-1).astype(F32)
        pad = (-f.shape[0]) % 128
        rows.append(jnp.pad(f, (0, pad)).reshape(-1, 128))
    out = jnp.concatenate(rows, axis=0)
    pad = (-out.shape[0]) % 256
    return jnp.pad(out, ((0, pad), (0, 0)))


def _unpack(packed, shapes):
    res, r = [], 0
    for shp in shapes:
        size = int(np.prod(shp))
        nr = -(-size // 128)
        res.append(packed[r:r + nr].reshape(-1)[:size].reshape(shp))
        r += nr
    return res


def kernel(x, c, w_ada, b_ada, norm1_g, w_in, b_in, attn_sinks, rel_bias, lambda_re, lambda_im, log_step, ssm_b_re, ssm_b_im, ssm_c_re, ssm_c_im, ssm_d, w_glu, b_glu, w_attn_proj, w_ssm_proj, w_out, norm2_g, w_ff1, w_ff2, final_g, loss_target, m_w_ada, m_b_ada, m_norm1_g, m_w_in, m_b_in, m_attn_sinks, m_rel_bias, m_lambda_re, m_lambda_im, m_log_step, m_ssm_b_re, m_ssm_b_im, m_ssm_c_re, m_ssm_c_im, m_ssm_d, m_w_glu, m_b_glu, m_w_attn_proj, m_w_ssm_proj, m_w_out, m_norm2_g, m_w_ff1, m_w_ff2, m_final_g, v_w_ada, v_b_ada, v_norm1_g, v_w_in, v_b_in, v_attn_sinks, v_rel_bias, v_lambda_re, v_lambda_im, v_log_step, v_ssm_b_re, v_ssm_b_im, v_ssm_c_re, v_ssm_c_im, v_ssm_d, v_w_glu, v_b_glu, v_w_attn_proj, v_w_ssm_proj, v_w_out, v_norm2_g, v_w_ff1, v_w_ff2, v_final_g):
    given = dict(locals())
    S, D = x.shape[1], x.shape[2]
    SSM_W = w_glu.shape[2]
    G = SSM_W // SSM_GROUP_CH
    NST = G * SSM_STATE
    DFF = w_ff2.shape[1] * N_CHIPS
    INW = w_in.shape[2] * N_CHIPS
    o_q, o_k, o_v, o_u = 0, ATTN_WIDTH, ATTN_WIDTH + KV_WIDTH, ATTN_WIDTH + 2 * KV_WIDTH
    o_ga, o_gs = o_u + SSM_W, o_u + SSM_W + D
    mx, my, mc = _position()
    my_chip = 2 * mx + my
    my_b = 4 * mx + 2 * my + mc

    xv, tgt = x[0], loss_target[0]

    big = dict(w_in=w_in[0], w_glu=w_glu[0], w_attn_proj=w_attn_proj[0], w_ssm_proj=w_ssm_proj[0],
               w_out=w_out[0], w_ff1=w_ff1[0], w_ff2=w_ff2[0])
    big_names = list(big)
    colsharded = {"w_in", "w_attn_proj", "w_ssm_proj", "w_ff1"}
    chip_sel = my_chip.astype(jnp.int32).reshape(1)
    gather_groups = [["w_in"], ["w_attn_proj", "w_ssm_proj", "w_glu", "w_out"], ["w_ff1", "w_ff2"]]
    in_flight, gather_sems, gathered = {}, [], {}

    def finish_gather(g, after):
        bufs = [in_flight[k] for k in gather_groups[g]]
        bufs = _gather_wait(bufs, gather_sems[g][0], gather_sems[g][1], after, "gather_wait_%d" % g)
        gathered.update(zip(gather_groups[g], _forward_halves(bufs, "gather_forward_%d" % g)))

    def tied(v, token):
        return v + token[0:1, 0:1]

    def all_of(*arrays):
        return jnp.stack([a.reshape(-1)[0].astype(F32) for a in arrays])

    def wop(k):
        g = gathered[k]
        return _Op(g, N_CHIPS) if k in colsharded else _Op(g.reshape(g.shape[0] * g.shape[1], g.shape[2]))

    grads = {}
    half = mc.astype(jnp.int32).reshape(1)
    sel = jnp.stack([my_chip, mc]).astype(jnp.int32)

    def rs_begin(tag, named):
        keys, gl = list(named), []
        for k in keys:
            gk = named[k]
            if k not in colsharded:
                gk = gk.reshape(N_CHIPS, gk.shape[0] // N_CHIPS, gk.shape[1])
            gl.append(gk)
        t1 = _swap_halves(gl, "rs_swap_" + tag)
        ps = [_add_half(g, t, half, "rs_add_" + k) for g, t, k in zip(gl, t1, keys)]
        ps, lands, ssem, rsem, token = _scatter_start(ps, "rs_start_" + tag)
        return (keys, ps, lands, ssem, rsem), token

    def rs_end(tag, state, after):
        keys, ps, lands, ssem, rsem = state
        ps, lands = _scatter_wait(ps, lands, ssem, rsem, after, "rs_wait_" + tag)
        rs = [_sum_own(p, t, sel, "rs_sum_" + k) for p, t, k in zip(ps, lands, keys)]
        full = _join_halves(rs, "rs_join_" + tag)
        for k, f in zip(keys, full):
            grads[k] = f[None]
        return full[-1]

    c_all = _allgather8(jnp.pad(c, ((0, 7), (0, 0))), "gather_c").reshape(N_DEV, 8, D)[:, 0]
    c16 = jnp.pad(c_all, ((0, 8), (0, 0)))
    b_ada_mine = lax.dynamic_slice(b_ada.reshape(N_CHIPS, -1), (my_chip, 0), (1, w_ada.shape[2]))
    mod_sh = _mm(c16, w_ada[0], "NN", name="mod", M=16, N=w_ada.shape[2], K=D, a_fn=_silu,
                 epilogue=lambda acc, b: (acc + b,), extras=[(b_ada_mine, "row")])
    mod_all = _allgather8(mod_sh[:8], "gather_mod").reshape(N_DEV, 8, -1)
    mod_row = jnp.concatenate(
        [lax.dynamic_slice(mod_all, (2 * j, my_b, 0), (1, 1, mod_all.shape[2]))[0] for j in range(N_CHIPS)], axis=1)
    sh1, sc1, g1, sh2, sc2, g2 = [mod_row[:, i * D:(i + 1) * D] for i in range(6)]

    first = [_cast_into_slot(big["w_in"], chip_sel, "cast_w_in")]
    first, sems_first, token_first = _gather_start(first, [[0]], mod_all, "gather_start_in")
    rest_names = gather_groups[1] + gather_groups[2]
    rest = [_cast_into_slot(big[k], chip_sel, "cast_" + k) for k in rest_names]
    rest, sems_rest, token_rest = _gather_start(
        rest, [[rest_names.index(k) for k in grp] for grp in gather_groups[1:]], token_first, "gather_start_rest")
    in_flight.update(zip(["w_in"] + rest_names, first + rest))
    gather_sems.extend(sems_first + sems_rest)

    disc_in = (lambda_re[0], lambda_im[0], log_step[0], ssm_b_re[0], ssm_b_im[0])
    (abar_re, abar_im, bbar_re, bbar_im), disc_vjp = jax.vjp(_discretise, *disc_in)
    same_group = jnp.asarray(np.arange(SSM_W)[:, None] // SSM_GROUP_CH == np.arange(NST)[None, :] // SSM_STATE)

    def block_diag(t):
        return jnp.where(same_group, jnp.tile(t, (G, 1)), 0.0)

    bd = jnp.concatenate([block_diag(bb.transpose(2, 0, 1).reshape(SSM_GROUP_CH, NST)) for bb in (bbar_re, bbar_im)],
                         axis=1)
    cd = jnp.concatenate([block_diag(cc.transpose(1, 0, 2).reshape(SSM_GROUP_CH, NST)).T
                          for cc in (ssm_c_re[0], -ssm_c_im[0])], axis=0)
    a_fwd = jnp.stack([abar_re.reshape(1, NST), abar_im.reshape(1, NST)])
    a_bwd = jnp.stack([abar_re.reshape(1, NST), -abar_im.reshape(1, NST)])
    d_row = ssm_d

    buckets = _t5_buckets_block()
    onehot_t = (jnp.arange(128, dtype=jnp.int32)[:, None] == jnp.asarray(buckets.reshape(1, -1))).astype(BF16)
    rb_hi = rel_bias.astype(BF16)
    rb_lo = (rel_bias - rb_hi.astype(F32)).astype(BF16)
    rb_lo2 = (rel_bias - rb_hi.astype(F32) - rb_lo.astype(F32)).astype(BF16)
    rb3 = jnp.pad(jnp.concatenate([rb_hi.T, rb_lo.T, rb_lo2.T], axis=0), ((0, 0), (0, 128 - NUM_BUCKETS)))
    b3 = _mm(rb3, onehot_t, "NN", name="rel_bias_rows", M=3 * N_Q_HEADS, N=BLOCK * 2 * BLOCK, K=128, tj=4096)
    bias = (b3[:N_Q_HEADS] + b3[N_Q_HEADS:2 * N_Q_HEADS]) + b3[2 * N_Q_HEADS:]
    bias = bias.reshape(N_Q_HEADS, BLOCK, 2 * BLOCK)
    sinks_b = jnp.broadcast_to(attn_sinks[0][:, None, None], (N_Q_HEADS, BLOCK, 128)).reshape(N_Q_HEADS * BLOCK, 128)

    def two(fn):
        def both(*blocks):
            r = fn(*blocks)
            return r, r
        return both

    h1, h1_t = _rowwise(two(_norm_mod), [(xv, "tile", D), (tied(tied(norm1_g, token_first), token_rest), "row", D),
                                         (sh1, "row", D), (sc1, "row", D)],
                        [(D, BF16), (D, BF16, "T")], [], name="norm1", rows=S)
    finish_gather(0, all_of(h1, bd, cd, a_fwd, a_bwd, bias, sinks_b))
    proj = _mm(h1, wop("w_in"), "NN", name="proj", M=S, N=INW, K=D,
               epilogue=lambda acc, b: (acc + b,), extras=[(b_in, "row")])

    def heads(v2d, nh):
        return v2d.reshape(S, nh, HEAD_DIM).transpose(1, 0, 2)

    def unheads(v3d):
        return v3d.transpose(1, 0, 2).reshape(S, -1)

    qh = heads(proj[:, o_q:o_k], N_Q_HEADS)
    kh = heads(proj[:, o_k:o_v], N_KV_HEADS)
    vh = heads(proj[:, o_v:o_u], N_KV_HEADS)
    attn = unheads(_attn_fwd(qh, kh, vh, sinks_b, bias, "attn_fwd"))
    finish_gather(1, attn)
    y_attn = _mm(attn, wop("w_attn_proj"), "NN", name="attn_proj", M=S, N=D, K=ATTN_WIDTH, out_dtypes=(BF16,))

    u = proj[:, o_u:o_ga]
    u_il = _interleave(u, SCAN_CHUNKS)
    SB = 128
    nsb, gpb = SSM_W // SB, SB // SSM_GROUP_CH
    SBN = gpb * SSM_STATE
    bu = _mm(u_il, bd, "NN", name="ssm_bu", M=S, N=2 * NST, K=SB, out_nsh=2, tj=SBN, tk=SB,
             a_idx=lambda i, j, k: (i, j % nsb), b_idx=lambda i, j, k: (j % nsb, j))
    xs = _scan(a_fwd, bu, None, reverse=False, name="scan_fwd", tc=256)
    y_il = _mm(_Op(xs, 2), cd, "NN", name="ssm_y", M=S, N=SSM_W, K=2 * SBN, tj=SB, tk=SBN,
               a_idx=lambda i, j, k: (i, j + nsb * k), b_idx=lambda i, j, k: (j + nsb * k, j),
               epilogue=lambda acc, uu, dd: (acc + dd * uu,), extras=[(u_il, "tile"), (d_row, "row")])
    y = _deinterleave(y_il, SCAN_CHUNKS)
    z0b = _rowwise(_gelu, [(y, "tile", SSM_W)], [(SSM_W, BF16)], [], name="gelu", rows=S)[0]
    z, t_glu = _mm(z0b, wop("w_glu"), "NN", name="glu", M=S, N=SSM_W, K=SSM_W, out_dtypes=(BF16, F32),
                   epilogue=lambda acc, b, yy: (_gelu(yy) * _sigmoid(acc + b), acc + b),
                   extras=[(b_glu, "row"), (y, "tile")])
    y_ssm = _mm(z, wop("w_ssm_proj"), "NN", name="ssm_proj", M=S, N=D, K=SSM_W, out_dtypes=(BF16,))

    merged, merged_t = _rowwise(two(_merge), [(_Op(proj, coff=o_ga), "tile", D), (_Op(proj, coff=o_gs), "tile", D),
                                              (y_attn, "tile", D), (y_ssm, "tile", D)],
                                [(D, BF16), (D, BF16, "T")], [], name="merge", rows=S)
    mo, x2 = _mm(merged, wop("w_out"), "NN", name="out_proj", M=S, N=D, K=D, out_dtypes=(BF16, F32),
                 epilogue=lambda acc, xx, gg: (acc, xx + gg * acc), extras=[(xv, "tile"), (g1, "row")])
    h2, h2_t = _rowwise(two(_norm_mod), [(x2, "tile", D), (norm2_g, "row", D), (sh2, "row", D), (sc2, "row", D)],
                        [(D, BF16), (D, BF16, "T")], [], name="norm2", rows=S)
    finish_gather(2, h2)
    a_b, r_b = _mm(h2, wop("w_ff1"), "NN", name="ff1", M=S, N=DFF, K=D, out_dtypes=(BF16, BF16),
                   epilogue=lambda acc: (acc, jnp.square(jnp.maximum(acc, 0.0))))
    ff, x3 = _mm(r_b, wop("w_ff2"), "NN", name="ff2", M=S, N=D, K=DFF, out_dtypes=(BF16, F32),
                 epilogue=lambda acc, xx, gg: (acc, xx + gg * acc), extras=[(x2, "tile"), (g2, "row")],
                 tj=1024, tk=1024)

    def final_fn(x3b, gf, tb):
        def f(xx, gg):
            yv = xx * lax.rsqrt(jnp.mean(xx * xx, axis=-1, keepdims=True) + EPS) * gg
            err = jnp.square(yv - tb)
            return 0.5 * jnp.sum(jnp.mean(err, axis=-1, keepdims=True), axis=0, keepdims=True)
        lv, vjp = jax.vjp(f, x3b, gf)
        dx, dg = vjp(jnp.ones((1, 1), F32))
        return dx, dg, jnp.broadcast_to(lv, (1, 128))

    dx3, g_final, loss_acc = _rowwise(final_fn, [(x3, "tile", D), (final_g.reshape(1, D), "row", D), (tgt, "tile", D)],
                                      [(D, F32)], [D, 128], name="final", rows=S)

    def ff_out_bwd(dx3b, ffb, g2b):
        return dx3b * g2b, jnp.sum(dx3b * ffb, axis=0, keepdims=True)

    dff, d_g2 = _rowwise(ff_out_bwd, [(dx3, "tile", D), (ff, "tile", D), (g2, "row", D)], [(D, BF16)], [D],
                         name="ff_out_bwd", rows=S)
    da = _mm(dff, wop("w_ff2"), "NT", name="ff2_dx", M=S, N=DFF, K=D, out_dtypes=(BF16,),
             epilogue=lambda acc, ab: (acc * (2.0 * jnp.maximum(ab.astype(F32), 0.0)),), extras=[(a_b, "tile")])
    g_w_ff2 = _mm(r_b, dff, "TN", name="ff2_dw", M=DFF, N=D, K=S, out_dtypes=(BF16,), tj=1024, tk=1024)
    dh2 = _mm(da, wop("w_ff1"), "NT", name="ff1_dx", M=S, N=D, K=DFF, tj=1024, tk=1024)
    g_w_ff1 = _mm(h2_t, da, "NN", name="ff1_dw", M=D, N=DFF, K=S, out_dtypes=(BF16,), out_nsh=N_CHIPS, tj=1024, tk=1024)
    rs_ff, token_ff = rs_begin("ff", dict(w_ff2=g_w_ff2, w_ff1=g_w_ff1))

    def norm2_bwd(x2b, dh2b, dx3b, mob, gn, shb, scb, g1b):
        _, vjp = jax.vjp(_norm_mod, x2b, gn, shb, scb)
        dx, dg, dsh, dsc = vjp(dh2b)
        dx2b = dx + dx3b
        return dx2b, dx2b * g1b, dg, dsh, dsc, jnp.sum(dx2b * mob, axis=0, keepdims=True)

    dx2, dmo, g_norm2, d_sh2, d_sc2, d_g1 = _rowwise(
        norm2_bwd, [(x2, "tile", D), (dh2, "tile", D), (dx3, "tile", D), (mo, "tile", D),
                    (tied(norm2_g, token_ff), "row", D), (sh2, "row", D), (sc2, "row", D), (g1, "row", D)],
        [(D, F32), (D, BF16)], [D, D, D, D], name="norm2_bwd", rows=S, tr=128)
    dmerged = _mm(dmo, wop("w_out"), "NT", name="out_dx", M=S, N=D, K=D)
    g_w_out = _mm(merged_t, dmo, "NN", name="out_dw", M=D, N=D, K=S, out_dtypes=(BF16,), tk=1024)

    def merge_bwd(gab, gsb, yab, ysb, dmb):
        _, vjp = jax.vjp(_merge, gab, gsb, yab, ysb)
        return vjp(dmb)

    d_ga, d_gs, dy_attn, dy_ssm = _rowwise(
        merge_bwd, [(_Op(proj, coff=o_ga), "tile", D), (_Op(proj, coff=o_gs), "tile", D), (y_attn, "tile", D),
                    (y_ssm, "tile", D), (dmerged, "tile", D)],
        [(D, BF16), (D, BF16), (D, BF16), (D, BF16)], [], name="merge_bwd", rows=S, tr=128)

    dattn = _mm(dy_attn, wop("w_attn_proj"), "NT", name="attn_proj_dx", M=S, N=ATTN_WIDTH, K=D)
    g_w_attn_proj = _mm(attn, dy_attn, "TN", name="attn_proj_dw", M=ATTN_WIDTH, N=D, K=S, out_dtypes=(BF16,),
                        out_nsh=N_CHIPS, tk=1024)
    dqh, dkh, dvh, dsink_blk, dbias = _attn_bwd(qh, kh, vh, heads(dattn, N_Q_HEADS), sinks_b, bias, "attn_bwd")
    g_sinks = _sum_lead(dsink_blk.reshape(N_Q_HEADS, BLOCK, 128).transpose(1, 0, 2), "sinks_dw")[:, 0].reshape(1, N_Q_HEADS)
    g_rel = _mm(dbias.reshape(N_Q_HEADS, -1), onehot_t, "NT", name="rel_bias_dw", M=N_Q_HEADS, N=128,
                K=BLOCK * 2 * BLOCK, tk=4096)
    g_rel_bias = g_rel[:, :NUM_BUCKETS].T

    dz = _mm(dy_ssm, wop("w_ssm_proj"), "NT", name="ssm_proj_dx", M=S, N=SSM_W, K=D)
    g_w_ssm_proj = _mm(z, dy_ssm, "TN", name="ssm_proj_dw", M=SSM_W, N=D, K=S, out_dtypes=(BF16,),
                       out_nsh=N_CHIPS, tk=1024)

    def glu_bwd(dzb, yb, tb):
        z0 = _gelu(yb)
        sg = _sigmoid(tb)
        dt = dzb * z0 * sg * (1.0 - sg)
        return dt, dzb * sg, jnp.sum(dt, axis=0, keepdims=True)

    dt_b, dz0a, g_b_glu = _rowwise(glu_bwd, [(dz, "tile", SSM_W), (y, "tile", SSM_W), (t_glu, "tile", SSM_W)],
                                   [(SSM_W, BF16), (SSM_W, F32)], [SSM_W], name="glu_bwd", rows=S)

    def gelu_bwd(acc, dz0ab, yb):
        _, vjp = jax.vjp(_gelu, yb)
        return (vjp(acc + dz0ab)[0],)

    dy = _mm(dt_b, wop("w_glu"), "NT", name="glu_dx", M=S, N=SSM_W, K=SSM_W, epilogue=gelu_bwd,
             extras=[(dz0a, "tile"), (y, "tile")])
    g_w_glu = _mm(z0b, dt_b, "TN", name="glu_dw", M=SSM_W, N=SSM_W, K=S, out_dtypes=(BF16,), tk=1024)
    rs_mix, token_mix = rs_begin("mix", dict(w_out=g_w_out, w_attn_proj=g_w_attn_proj, w_ssm_proj=g_w_ssm_proj,
                                             w_glu=g_w_glu))
    dy_il = _interleave(tied(dy, token_mix), SCAN_CHUNKS)
    dxs = _mm(dy_il, cd, "NT", name="ssm_dx", M=S, N=2 * NST, K=SB, out_nsh=2, tj=SBN, tk=SB,
              a_idx=lambda i, j, k: (i, j % nsb), b_idx=lambda i, j, k: (j, j % nsb))
    g_cd = _mm(_Op(xs, 2), dy_il, "TN", name="ssm_dc", M=2 * NST, N=SB, K=S, ti=SBN, tj=SB, tk=1024,
               b_idx=lambda i, j, k: (k, i % nsb))
    lam, d_abar = _scan(a_bwd, dxs, xs, reverse=True, name="scan_bwd", tc=128)

    def du_fn(acc, dyb, dd):
        return (acc + dd * dyb,)

    du_il = _mm(_Op(lam, 2), bd, "NT", name="ssm_du", M=S, N=SSM_W, K=2 * SBN, tj=SB, tk=SBN,
                a_idx=lambda i, j, k: (i, j + nsb * k), b_idx=lambda i, j, k: (j, j + nsb * k),
                epilogue=du_fn, extras=[(dy_il, "tile"), (d_row, "row")])
    g_bd = _mm(u_il, _Op(lam, 2), "TN", name="ssm_db", M=SSM_W, N=2 * SBN, K=S, ti=SB, tj=SBN, tk=1024,
               b_idx=lambda i, j, k: (k, i + nsb * j))
    g_ssm_d = _rowwise(lambda dyb, ub: (jnp.sum(dyb * ub, axis=0, keepdims=True),),
                       [(dy_il, "tile", SSM_W), (u_il, "tile", SSM_W)], [], [SSM_W], name="ssm_dd", rows=S)[0]
    du = _deinterleave(du_il, SCAN_CHUNKS)

    eye_b = jnp.eye(gpb, dtype=F32)
    g_cd6 = g_cd.reshape(2, nsb, gpb, SSM_STATE, gpb, SSM_GROUP_CH)
    g_c_re = jnp.einsum("bgnhp,gh->bgpn", g_cd6[0], eye_b).reshape(G, SSM_GROUP_CH, SSM_STATE)
    g_c_im = -jnp.einsum("bgnhp,gh->bgpn", g_cd6[1], eye_b).reshape(G, SSM_GROUP_CH, SSM_STATE)
    g_bd6 = g_bd.reshape(nsb, gpb, SSM_GROUP_CH, 2, gpb, SSM_STATE)
    g_bbar = jnp.einsum("bhprgn,hg->rbhnp", g_bd6, eye_b).reshape(2, G, SSM_STATE, SSM_GROUP_CH)
    g_bbar_re, g_bbar_im = g_bbar[0], g_bbar[1]
    g_lre, g_lim, g_lstep, g_bre, g_bim = disc_vjp(
        (d_abar[0].reshape(G, SSM_STATE), d_abar[1].reshape(G, SSM_STATE), g_bbar_re, g_bbar_im))

    dproj = jnp.concatenate([unheads(dqh).astype(BF16), unheads(dkh).astype(BF16), unheads(dvh).astype(BF16),
                             du.astype(BF16), d_ga, d_gs], axis=1)
    g_w_in = _mm(h1_t, dproj, "NN", name="proj_dw", M=D, N=INW, K=S, out_dtypes=(BF16,), out_nsh=N_CHIPS,
                 tj=INW // (2 * N_CHIPS), tk=1024)
    rs_in, token_in = rs_begin("in", dict(w_in=g_w_in))
    dh1 = _mm(dproj, wop("w_in"), "NT", name="proj_dx", M=S, N=D, K=INW, tj=1024, tk=INW // N_CHIPS,
              epilogue=lambda acc, zero: (acc + zero,), extras=[(tied(jnp.zeros((1, D), F32), token_in), "row")])
    g_b_in = _rowwise(lambda d: (jnp.sum(d.astype(F32), axis=0, keepdims=True),), [(dproj, "tile", INW)], [], [INW],
                      name="proj_db", rows=S)[0]

    def norm1_bwd(xb, dhb, dresb, gn, shb, scb):
        _, vjp = jax.vjp(_norm_mod, xb, gn, shb, scb)
        dx, dg, dsh, dsc = vjp(dhb)
        return dx + dresb, dg, dsh, dsc

    grad_x, g_norm1, d_sh1, d_sc1 = _rowwise(
        norm1_bwd, [(xv, "tile", D), (dh1, "tile", D), (dx2, "tile", D), (norm1_g, "row", D), (sh1, "row", D),
                    (sc1, "row", D)], [(D, F32)], [D, D, D], name="norm1_bwd", rows=S)

    dmod_row = jnp.concatenate([d_sh1, d_sc1, d_g1, d_sh2, d_sc2, d_g2], axis=1)
    dmod_all = _allgather8(jnp.pad(dmod_row, ((0, 7), (0, 0))), "gather_dmod").reshape(N_DEV, 8, -1)[:, 0]
    g_b_ada = _sum_lead(dmod_all.reshape(N_DEV, -1, 128), "b_ada_dw").reshape(1, -1)
    dmod_mine = lax.dynamic_slice(dmod_all.reshape(N_DEV, N_CHIPS, -1), (0, my_chip, 0), (N_DEV, 1, w_ada.shape[2]))[:, 0]
    g_w_ada = _mm(c16, jnp.pad(dmod_mine, ((0, 8), (0, 0))), "TN", name="ada_dw", M=D, N=w_ada.shape[2], K=16,
                  a_fn=_silu)

    small_g = dict(norm1_g=g_norm1, b_in=g_b_in, attn_sinks=g_sinks, rel_bias=g_rel_bias, lambda_re=g_lre[None],
                   lambda_im=g_lim[None], log_step=g_lstep[None], ssm_b_re=g_bre[None], ssm_b_im=g_bim[None],
                   ssm_c_re=g_c_re[None], ssm_c_im=g_c_im[None], ssm_d=g_ssm_d, b_glu=g_b_glu, norm2_g=g_norm2,
                   final_g=g_final.reshape(D))
    packed = _pack([loss_acc[:, :1]] + [small_g[k] for k in _SMALL])
    rows = packed.shape[0]
    summed = _sum_lead(_allgather8(packed, "gather_small").reshape(N_DEV, rows, 128), "small_sum")
    small_shapes = [(1,)] + [given[k].shape for k in _SMALL]
    parts = _unpack(summed, small_shapes)
    loss = parts[0].reshape(())
    grads.update(zip(_SMALL, parts[1:]))
    grads["b_ada"] = g_b_ada
    grads["w_ada"] = g_w_ada[None]

    deltas, new_m, new_v = {}, {}, {}

    def adamw_big(k):
        d_, m_, v_ = _adamw(given[k][0], grads[k][0], given["m_" + k][0], given["v_" + k][0], "adamw_" + k)
        deltas[k], new_m[k], new_v[k] = d_[None], m_[None], v_[None]
        return v_

    rs_end("ff", rs_ff, summed)
    marks = [adamw_big(k) for k in ("w_ff2", "w_ff1")]
    rs_end("mix", rs_mix, all_of(*marks))
    marks = [adamw_big(k) for k in ("w_out", "w_attn_proj", "w_ssm_proj", "w_glu", "w_ada")]
    small_all = list(_SMALL) + ["b_ada"]
    shapes = [given[k].shape for k in small_all]
    pw, pg = _pack([given[k] for k in small_all]), _pack([grads[k] for k in small_all])
    pm, pv = _pack([given["m_" + k] for k in small_all]), _pack([given["v_" + k] for k in small_all])
    d_, m_, v_ = _adamw(pw, pg, pm, pv, "adamw_small")
    for k, dd, mm, vv in zip(small_all, _unpack(d_, shapes), _unpack(m_, shapes), _unpack(v_, shapes)):
        deltas[k], new_m[k], new_v[k] = dd, mm, vv
        grads[k] = grads[k].reshape(given[k].shape)
    rs_end("in", rs_in, all_of(v_, *marks))
    adamw_big("w_in")

    names = ["w_ada", "b_ada", "norm1_g", "w_in", "b_in", "attn_sinks", "rel_bias", "lambda_re", "lambda_im",
             "log_step", "ssm_b_re", "ssm_b_im", "ssm_c_re", "ssm_c_im", "ssm_d", "w_glu", "b_glu", "w_attn_proj",
             "w_ssm_proj", "w_out", "norm2_g", "w_ff1", "w_ff2", "final_g"]
    return (loss, grad_x[None], *[grads[n] for n in names], *[deltas[n] for n in names],
            *[new_m[n] for n in names], *[new_v[n] for n in names])
```

```python
import math

import numpy as np
import jax
import jax.numpy as jnp
from jax import lax
from jax.experimental import pallas as pl
from jax.experimental.pallas import tpu as pltpu

F32 = jnp.float32
BF16 = jnp.bfloat16
MESH = pl.DeviceIdType.MESH

HEAD_DIM = 64
N_Q_HEADS = 16
N_KV_HEADS = 4
GQA_GROUP = N_Q_HEADS // N_KV_HEADS
ATTN_WIDTH = N_Q_HEADS * HEAD_DIM
KV_WIDTH = N_KV_HEADS * HEAD_DIM
BLOCK = 128
NUM_BUCKETS = 32
MAX_DISTANCE = 128
NEG_INF = -1e30
SSM_GROUP_CH = 16
SSM_STATE = 64
EPS = 1e-6
ADAM_LR = 0.001
ADAM_B1 = 0.9
ADAM_B2 = 0.999
ADAM_EPS = 1e-08
ADAM_WD = 0.01
ADAM_STEP = 10

N_CHIPS = 4
N_DEV = 8
SCAN_CHUNKS = 8
VMEM_LIMIT_BYTES = 48 * 1024 * 1024
SSM_VMEM_LIMIT_BYTES = 56 * 1024 * 1024


def _cparams(sem=None):
    return pltpu.CompilerParams(dimension_semantics=sem, vmem_limit_bytes=VMEM_LIMIT_BYTES)


class _Op:
    def __init__(self, arr, nsh=None, coff=0):
        self.arr, self.nsh, self.coff = arr, nsh, coff
        if nsh is None:
            self.rows, self.cols = arr.shape
        else:
            assert arr.shape[0] == nsh
            self.rows, self.cols = arr.shape[1], arr.shape[2] * nsh

    def spec(self, br, bc, idx):
        assert self.coff % bc == 0
        off = self.coff // bc
        if self.nsh is None:
            return pl.BlockSpec((br, bc), lambda *g: (idx(*g)[0], idx(*g)[1] + off))
        per = (self.cols // self.nsh) // bc
        assert per * bc * self.nsh == self.cols

        def imap(*g):
            r, c = idx(*g)
            c = c + off
            return (c // per, r, c % per)
        return pl.BlockSpec((None, br, bc), imap)


def _as_op(a):
    return a if isinstance(a, _Op) else _Op(a)


def _mm(a, b, mode, *, name, M, N, K, out_dtypes=(F32,), out_nsh=None, epilogue=None, extras=(),
        a_fn=None, ti=1024, tj=512, tk=2048, a_idx=None, b_idx=None):
    a, b = _as_op(a), _as_op(b)
    ti, tj, tk = min(ti, M), min(tj, N), min(tk, K)
    a_w = a.cols // a.nsh if a.nsh else None
    b_w = b.cols // b.nsh if b.nsh else None
    if a_w:
        ti, tk = (min(ti, a_w), tk) if mode == "TN" else (ti, min(tk, a_w))
    if b_w:
        tj, tk = (tj, min(tk, b_w)) if mode == "NT" else (min(tj, b_w), tk)
    if out_nsh:
        tj = min(tj, N // out_nsh)
    assert M % ti == 0 and N % tj == 0 and K % tk == 0, (name, M, N, K, ti, tj, tk)
    nk = K // tk
    if mode == "NN":
        a_spec = a.spec(ti, tk, a_idx or (lambda i, j, k: (i, k)))
        b_spec = b.spec(tk, tj, b_idx or (lambda i, j, k: (k, j)))
        dims = (((1,), (0,)), ((), ()))
    elif mode == "NT":
        a_spec = a.spec(ti, tk, a_idx or (lambda i, j, k: (i, k)))
        b_spec = b.spec(tj, tk, b_idx or (lambda i, j, k: (j, k)))
        dims = (((1,), (1,)), ((), ()))
    else:
        a_spec = a.spec(tk, ti, a_idx or (lambda i, j, k: (k, i)))
        b_spec = b.spec(tk, tj, b_idx or (lambda i, j, k: (k, j)))
        dims = (((0,), (0,)), ((), ()))
    ex_specs, ex_arrs = [], []
    for op, kind in extras:
        op = _as_op(op)
        if kind == "tile":
            ex_specs.append(op.spec(ti, tj, lambda i, j, k: (i, j)))
        else:
            ex_specs.append(op.spec(1, tj, lambda i, j, k: (0, j)))
        ex_arrs.append(op.arr)
    ne, no = len(ex_arrs), len(out_dtypes)
    if out_nsh is None:
        out_shapes = [jax.ShapeDtypeStruct((M, N), d) for d in out_dtypes]
        out_specs = [pl.BlockSpec((ti, tj), lambda i, j, k: (i, j)) for _ in out_dtypes]
    else:
        per = (N // out_nsh) // tj
        assert per * tj * out_nsh == N
        out_shapes = [jax.ShapeDtypeStruct((out_nsh, M, N // out_nsh), d) for d in out_dtypes]
        out_specs = [pl.BlockSpec((None, ti, tj), lambda i, j, k: (j // per, i, j % per)) for _ in out_dtypes]

    def body(a_ref, b_ref, *rest):
        ex_refs, out_refs, acc = rest[:ne], rest[ne:ne + no], rest[ne + no]
        k = pl.program_id(2)

        @pl.when(k == 0)
        def _():
            acc[...] = jnp.zeros_like(acc)

        av = a_ref[...]
        if a_fn is not None:
            av = a_fn(av)
        acc[...] += lax.dot_general(av.astype(BF16), b_ref[...].astype(BF16), dims,
                                    preferred_element_type=F32)

        @pl.when(k == nk - 1)
        def _():
            res = acc[...]
            outs = epilogue(res, *[r[...] for r in ex_refs]) if epilogue is not None else (res,)
            for o_ref, o in zip(out_refs, outs):
                o_ref[...] = o.astype(o_ref.dtype)

    outs = pl.pallas_call(
        body, name=name, grid=(M // ti, N // tj, nk),
        in_specs=[a_spec, b_spec] + ex_specs, out_specs=out_specs, out_shape=out_shapes,
        scratch_shapes=[pltpu.VMEM((ti, tj), F32)],
        compiler_params=_cparams(("parallel", "parallel", "arbitrary")),
    )(a.arr, b.arr, *ex_arrs)
    return outs[0] if no == 1 else outs


def _rowwise(fn, ins, outs, accs, *, name, rows, tr=256):
    tr = min(tr, rows)
    assert rows % tr == 0
    in_specs, arrs = [], []
    for op, kind, width in ins:
        op = _as_op(op)
        if kind == "tile":
            in_specs.append(op.spec(tr, width, lambda i: (i, 0)))
        else:
            in_specs.append(op.spec(op.rows, width, lambda i: (0, 0)))
        arrs.append(op.arr)
    ni, no, na = len(ins), len(outs), len(accs)
    flipped = [len(o) == 3 for o in outs]
    out_shapes = [jax.ShapeDtypeStruct((o[0], rows) if t else (rows, o[0]), o[1]) for o, t in zip(outs, flipped)]
    out_specs = [pl.BlockSpec((o[0], tr), lambda i: (0, i)) if t else pl.BlockSpec((tr, o[0]), lambda i: (i, 0))
                 for o, t in zip(outs, flipped)]
    out_shapes += [jax.ShapeDtypeStruct((1, w), F32) for w in accs]
    out_specs += [pl.BlockSpec((1, w), lambda i: (0, 0)) for w in accs]

    def body(*refs):
        in_refs, out_refs, acc_refs = refs[:ni], refs[ni:ni + no], refs[ni + no:]
        res = fn(*[r[...] for r in in_refs])
        if not isinstance(res, (tuple, list)):
            res = (res,)
        for o_ref, r, t in zip(out_refs, res[:no], flipped):
            o_ref[...] = (r.astype(F32).T if t else r).astype(o_ref.dtype)
        if na:
            @pl.when(pl.program_id(0) == 0)
            def _():
                for a_ref in acc_refs:
                    a_ref[...] = jnp.zeros_like(a_ref)
            for a_ref, r in zip(acc_refs, res[no:]):
                a_ref[...] += r.astype(F32)

    res = pl.pallas_call(
        body, name=name, grid=(rows // tr,), in_specs=in_specs, out_specs=out_specs, out_shape=out_shapes,
        compiler_params=_cparams(("arbitrary",)),
    )(*arrs)
    return res


def _norm_mod(x, g, sh, sc):
    y = x * lax.rsqrt(jnp.mean(x * x, axis=-1, keepdims=True) + EPS) * g
    return y * (1.0 + sc) + sh


def _sigmoid(x):
    return 1.0 / (1.0 + jnp.exp(-x))


def _silu(x):
    return x * _sigmoid(x)


def _gelu(x):
    return 0.5 * x * (1.0 + jnp.tanh(math.sqrt(2.0 / math.pi) * (x + 0.044715 * (x * x * x))))


def _merge(ga, gs, ya, ys):
    return _sigmoid(ga) * ya + _sigmoid(gs) * ys


def _attn_head(q, kp, kc, vp, vc, sink, bias_p, bias_c, not_first):
    nt = (((1,), (1,)), ((), ()))
    nn = (((1,), (0,)), ((), ()))
    qb = q.astype(BF16)
    scale = HEAD_DIM ** -0.5
    sp = lax.dot_general(qb, kp.astype(BF16), nt, preferred_element_type=F32) * scale + bias_p
    sc = lax.dot_general(qb, kc.astype(BF16), nt, preferred_element_type=F32) * scale + bias_c
    qi = lax.broadcasted_iota(jnp.int32, sp.shape, 0) & (BLOCK - 1)
    ki = lax.broadcasted_iota(jnp.int32, sp.shape, 1)
    sp = jnp.where(jnp.logical_and(ki > qi, not_first), sp, NEG_INF)
    sc = jnp.where(ki <= qi, sc, NEG_INF)
    m = jnp.maximum(jnp.maximum(jnp.max(sp, axis=-1, keepdims=True), jnp.max(sc, axis=-1, keepdims=True)), sink)
    m = lax.stop_gradient(m)
    pp = jnp.exp(sp - m)
    pc = jnp.exp(sc - m)
    denom = jnp.sum(pp, axis=-1, keepdims=True) + jnp.sum(pc, axis=-1, keepdims=True) + jnp.exp(sink - m)
    o = lax.dot_general((pp / denom).astype(BF16), vp.astype(BF16), nn, preferred_element_type=F32)
    o = o + lax.dot_general((pc / denom).astype(BF16), vc.astype(BF16), nn, preferred_element_type=F32)
    return o


def _attn_fwd(qh, kh, vh, sinks, bias, name):
    s = qh.shape[1]
    nb = s // BLOCK
    G = GQA_GROUP
    R = G * BLOCK

    def body(q_ref, kp_ref, kc_ref, vp_ref, vc_ref, sink_ref, bias_ref, o_ref):
        not_first = pl.program_id(0) > 0
        for kv in range(N_KV_HEADS):
            hs = slice(kv * G, (kv + 1) * G)
            o = _attn_head(q_ref[hs].reshape(R, HEAD_DIM), kp_ref[kv], kc_ref[kv], vp_ref[kv], vc_ref[kv],
                           sink_ref[kv * R:(kv + 1) * R, 0:1],
                           bias_ref[hs, :, 0:BLOCK].reshape(R, BLOCK), bias_ref[hs, :, BLOCK:2 * BLOCK].reshape(R, BLOCK),
                           not_first)
            o_ref[hs] = o.reshape(G, BLOCK, HEAD_DIM).astype(o_ref.dtype)

    cur = lambda i: (0, i, 0)
    prev = lambda i: (0, jnp.maximum(i - 1, 0), 0)
    return pl.pallas_call(
        body, name=name, grid=(nb,),
        in_specs=[pl.BlockSpec((N_Q_HEADS, BLOCK, HEAD_DIM), cur),
                  pl.BlockSpec((N_KV_HEADS, BLOCK, HEAD_DIM), prev), pl.BlockSpec((N_KV_HEADS, BLOCK, HEAD_DIM), cur),
                  pl.BlockSpec((N_KV_HEADS, BLOCK, HEAD_DIM), prev), pl.BlockSpec((N_KV_HEADS, BLOCK, HEAD_DIM), cur),
                  pl.BlockSpec((N_Q_HEADS * BLOCK, 128), lambda i: (0, 0)),
                  pl.BlockSpec((N_Q_HEADS, BLOCK, 2 * BLOCK), lambda i: (0, 0, 0))],
        out_specs=pl.BlockSpec((N_Q_HEADS, BLOCK, HEAD_DIM), cur),
        out_shape=jax.ShapeDtypeStruct((N_Q_HEADS, s, HEAD_DIM), BF16),
        compiler_params=_cparams(("arbitrary",)),
    )(qh, kh, kh, vh, vh, sinks, bias)


def _attn_bwd(qh, kh, vh, doh, sinks, bias, name):
    s = qh.shape[1]
    nb = s // BLOCK
    G = GQA_GROUP
    R = G * BLOCK

    def body(q_ref, kp_ref, kc_ref, vp_ref, vc_ref, do_ref, sink_ref, bias_ref,
             dq_ref, dk_ref, dv_ref, dsink_ref, dbias_ref, ck, cv):
        i = pl.program_id(1)

        @pl.when(i == 0)
        def _():
            dsink_ref[...] = jnp.zeros_like(dsink_ref)
            dbias_ref[...] = jnp.zeros_like(dbias_ref)
            ck[...] = jnp.zeros_like(ck)
            cv[...] = jnp.zeros_like(cv)

        @pl.when(i < nb)
        def _():
            not_first = i > 0
            _, vjp = jax.vjp(lambda q, a, b, c, d, sk, e, f: _attn_head(q, a, b, c, d, sk, e, f, not_first),
                             q_ref[...].reshape(R, HEAD_DIM), kp_ref[...], kc_ref[...], vp_ref[...], vc_ref[...],
                             sink_ref[:, 0:1], bias_ref[:, :, 0:BLOCK].reshape(R, BLOCK),
                             bias_ref[:, :, BLOCK:2 * BLOCK].reshape(R, BLOCK))
            dq, dkp, dkc, dvp, dvc, dsk, dbp, dbc = vjp(do_ref[...].reshape(R, HEAD_DIM).astype(F32))
            dq_ref[...] = dq.reshape(G, BLOCK, HEAD_DIM)
            dsink_ref[...] += jnp.broadcast_to(dsk, (R, 128))
            dbias_ref[:, :, 0:BLOCK] += dbp.reshape(G, BLOCK, BLOCK)
            dbias_ref[:, :, BLOCK:2 * BLOCK] += dbc.reshape(G, BLOCK, BLOCK)
            dk_ref[...] = ck[...] + dkp
            dv_ref[...] = cv[...] + dvp
            ck[...] = dkc
            cv[...] = dvc

        @pl.when(i == nb)
        def _():
            dk_ref[...] = ck[...]
            dv_ref[...] = cv[...]

    qcur = lambda kv, i: (kv, jnp.minimum(i, nb - 1), 0)
    kcur = lambda kv, i: (kv, jnp.minimum(i, nb - 1), 0)
    kprev = lambda kv, i: (kv, jnp.clip(i - 1, 0, nb - 1), 0)
    qspec = pl.BlockSpec((G, BLOCK, HEAD_DIM), qcur)
    kc_spec = pl.BlockSpec((None, BLOCK, HEAD_DIM), kcur)
    kp_spec = pl.BlockSpec((None, BLOCK, HEAD_DIM), kprev)
    return pl.pallas_call(
        body, name=name, grid=(N_KV_HEADS, nb + 1),
        in_specs=[qspec, kp_spec, kc_spec, kp_spec, kc_spec, qspec,
                  pl.BlockSpec((R, 128), lambda kv, i: (kv, 0)),
                  pl.BlockSpec((G, BLOCK, 2 * BLOCK), lambda kv, i: (kv, 0, 0))],
        out_specs=[qspec, kp_spec, kp_spec,
                   pl.BlockSpec((R, 128), lambda kv, i: (kv, 0)),
                   pl.BlockSpec((G, BLOCK, 2 * BLOCK), lambda kv, i: (kv, 0, 0))],
        out_shape=[jax.ShapeDtypeStruct((N_Q_HEADS, s, HEAD_DIM), F32),
                   jax.ShapeDtypeStruct((N_KV_HEADS, s, HEAD_DIM), F32),
                   jax.ShapeDtypeStruct((N_KV_HEADS, s, HEAD_DIM), F32),
                   jax.ShapeDtypeStruct((N_Q_HEADS * BLOCK, 128), F32),
                   jax.ShapeDtypeStruct((N_Q_HEADS, BLOCK, 2 * BLOCK), F32)],
        scratch_shapes=[pltpu.VMEM((BLOCK, HEAD_DIM), F32), pltpu.VMEM((BLOCK, HEAD_DIM), F32)],
        compiler_params=_cparams(("arbitrary", "arbitrary")),
    )(qh, kh, kh, vh, vh, doh, sinks, bias)


def _cmul(ar, ai, br, bi):
    return ar * br - ai * bi, ar * bi + ai * br


def _scan_passes(a_ref, b_ref, x_ref, xp_ref, da_ref, *, s, tc, reverse):
    nc = SCAN_CHUNKS
    steps = s // nc
    with_da = xp_ref is not None
    unroll = 8 if steps % 8 == 0 else 1

    def shift(v, d):
        row = lax.broadcasted_iota(jnp.int32, v.shape, 0)
        if reverse:
            return jnp.where(row < nc - d, pltpu.roll(v, nc - d, 0), 0.0)
        return jnp.where(row >= d, pltpu.roll(v, d, 0), 0.0)

    def run():
        ar = jnp.broadcast_to(a_ref[0], (nc, tc))
        ai = jnp.broadcast_to(a_ref[1], (nc, tc))

        def row_of(step):
            j = (steps - 1 - step) if reverse else step
            return pl.multiple_of(j * nc, nc)

        def p1(step, st):
            sr, si = st
            r0 = row_of(step)
            mr, mi = _cmul(ar, ai, sr, si)
            sr = mr + b_ref[0, pl.ds(r0, nc), :]
            si = mi + b_ref[1, pl.ds(r0, nc), :]
            x_ref[0, pl.ds(r0, nc), :] = sr
            x_ref[1, pl.ds(r0, nc), :] = si
            return sr, si
        zero = jnp.zeros((nc, tc), F32)
        er, ei = lax.fori_loop(0, steps, p1, (zero, zero), unroll=unroll)

        pr, pi_ = jnp.ones((nc, tc), F32), zero
        br, bi, left = ar, ai, steps
        while left:
            if left & 1:
                pr, pi_ = _cmul(pr, pi_, br, bi)
            br, bi = _cmul(br, bi, br, bi)
            left >>= 1
        cr, ci = shift(er, 1), shift(ei, 1)
        d = 1
        while d < nc:
            mr, mi = _cmul(pr, pi_, shift(cr, d), shift(ci, d))
            cr, ci = cr + mr, ci + mi
            pr, pi_ = _cmul(pr, pi_, pr, pi_)
            d *= 2

        def p2(step, st):
            qr, qi, dar, dai = st
            r0 = row_of(step)
            qr, qi = _cmul(ar, ai, qr, qi)
            fr, fi = _cmul(qr, qi, cr, ci)
            xr = x_ref[0, pl.ds(r0, nc), :] + fr
            xi = x_ref[1, pl.ds(r0, nc), :] + fi
            x_ref[0, pl.ds(r0, nc), :] = xr
            x_ref[1, pl.ds(r0, nc), :] = xi
            if with_da:
                jm = jnp.where(step == steps - 1, steps - 1, steps - 2 - step)
                rp = pl.multiple_of(jm * nc, nc)
                vr, vi = xp_ref[0, pl.ds(rp, nc), :], xp_ref[1, pl.ds(rp, nc), :]
                row = lax.broadcasted_iota(jnp.int32, (nc, tc), 0)
                first = step == steps - 1
                sel = jnp.logical_and(first, row == 0)
                vr = jnp.where(sel, 0.0, jnp.where(first, pltpu.roll(vr, 1, 0), vr))
                vi = jnp.where(sel, 0.0, jnp.where(first, pltpu.roll(vi, 1, 0), vi))
                dar = dar + xr * vr + xi * vi
                dai = dai + xi * vr - xr * vi
            return qr, qi, dar, dai
        _, _, dar, dai = lax.fori_loop(0, steps, p2, (jnp.ones((nc, tc), F32), zero, zero, zero), unroll=unroll)
        if with_da:
            da_ref[0] = jnp.sum(dar, axis=0, keepdims=True)
            da_ref[1] = jnp.sum(dai, axis=0, keepdims=True)

    run()


def _ssm_fwd(u, bd, cd, a, d_row, *, name, sb, sbn):
    s, w = u.shape
    nst = a.shape[2]
    nblk = w // sb
    rows = min(512, s)
    nn = (((1,), (0,)), ((), ()))

    def body(u_ref, bre_ref, bim_ref, cre_ref, cim_ref, a_ref, d_ref, y_ref, x_ref):

        def fill(r, carry):
            r0 = pl.multiple_of(r * rows, rows)
            ub = u_ref[pl.ds(r0, rows), :].astype(BF16)
            x_ref[0, pl.ds(r0, rows), :] = lax.dot_general(ub, bre_ref[...].astype(BF16), nn, preferred_element_type=F32)
            x_ref[1, pl.ds(r0, rows), :] = lax.dot_general(ub, bim_ref[...].astype(BF16), nn, preferred_element_type=F32)
            return carry
        lax.fori_loop(0, s // rows, fill, 0)
        _scan_passes(a_ref, x_ref, x_ref, None, None, s=s, tc=sbn, reverse=False)

        def project(r, carry):
            r0 = pl.multiple_of(r * rows, rows)
            y = lax.dot_general(x_ref[0, pl.ds(r0, rows), :].astype(BF16), cre_ref[...].astype(BF16), nn, preferred_element_type=F32)
            y = y + lax.dot_general(x_ref[1, pl.ds(r0, rows), :].astype(BF16), cim_ref[...].astype(BF16), nn, preferred_element_type=F32)
            y_ref[pl.ds(r0, rows), :] = y + d_ref[...] * u_ref[pl.ds(r0, rows), :]
            return carry
        lax.fori_loop(0, s // rows, project, 0)

    return pl.pallas_call(
        body, name=name, grid=(nblk,),
        in_specs=[pl.BlockSpec((s, sb), lambda j: (0, j)),
                  pl.BlockSpec((sb, sbn), lambda j: (j, j)), pl.BlockSpec((sb, sbn), lambda j: (j, nblk + j)),
                  pl.BlockSpec((sbn, sb), lambda j: (j, j)), pl.BlockSpec((sbn, sb), lambda j: (nblk + j, j)),
                  pl.BlockSpec((2, 1, sbn), lambda j: (0, 0, j)), pl.BlockSpec((1, sb), lambda j: (0, j))],
        out_specs=[pl.BlockSpec((s, sb), lambda j: (0, j)), pl.BlockSpec((2, s, sbn), lambda j: (0, 0, j))],
        out_shape=[jax.ShapeDtypeStruct((s, w), F32), jax.ShapeDtypeStruct((2, s, nst), F32)],
        compiler_params=pltpu.CompilerParams(dimension_semantics=("arbitrary",), vmem_limit_bytes=SSM_VMEM_LIMIT_BYTES),
    )(u, bd, bd, cd, cd, a, d_row)


def _ssm_bwd(dy, u, xs, bd, cd, a, d_row, *, name, sb, sbn):
    s, w = u.shape
    nst = a.shape[2]
    nblk = w // sb
    rows = min(512, s)
    nt = (((1,), (1,)), ((), ()))
    tn = (((0,), (0,)), ((), ()))

    def body(dy_ref, u_ref, xs_hbm, bre_ref, bim_ref, cre_ref, cim_ref, a_ref, d_ref,
             du_ref, gb_ref, gc_ref, da_ref, gd_ref, lam, xs_ref, sem):
        j = pl.program_id(0)
        fetch = pltpu.make_async_copy(xs_hbm.at[:, :, pl.ds(pl.multiple_of(j * sbn, sbn), sbn)], xs_ref, sem)
        fetch.start()

        def fill(r, carry):
            r0 = pl.multiple_of(r * rows, rows)
            dyb = dy_ref[pl.ds(r0, rows), :].astype(BF16)
            lam[0, pl.ds(r0, rows), :] = lax.dot_general(dyb, cre_ref[...].astype(BF16), nt, preferred_element_type=F32)
            lam[1, pl.ds(r0, rows), :] = lax.dot_general(dyb, cim_ref[...].astype(BF16), nt, preferred_element_type=F32)
            return carry
        lax.fori_loop(0, s // rows, fill, 0)
        fetch.wait()
        _scan_passes(a_ref, lam, lam, xs_ref, da_ref, s=s, tc=sbn, reverse=True)
        gb_ref[...] = jnp.zeros_like(gb_ref)
        gc_ref[...] = jnp.zeros_like(gc_ref)
        gd_ref[...] = jnp.zeros_like(gd_ref)

        def project(r, carry):
            r0 = pl.multiple_of(r * rows, rows)
            dyv, uv = dy_ref[pl.ds(r0, rows), :], u_ref[pl.ds(r0, rows), :]
            dyb, ub = dyv.astype(BF16), uv.astype(BF16)
            lr, li = lam[0, pl.ds(r0, rows), :].astype(BF16), lam[1, pl.ds(r0, rows), :].astype(BF16)
            du = lax.dot_general(lr, bre_ref[...].astype(BF16), nt, preferred_element_type=F32)
            du = du + lax.dot_general(li, bim_ref[...].astype(BF16), nt, preferred_element_type=F32)
            du_ref[pl.ds(r0, rows), :] = du + d_ref[...] * dyv
            gb_ref[:, 0:sbn] += lax.dot_general(ub, lr, tn, preferred_element_type=F32)
            gb_ref[:, sbn:2 * sbn] += lax.dot_general(ub, li, tn, preferred_element_type=F32)
            gc_ref[0] += lax.dot_general(xs_ref[0, pl.ds(r0, rows), :].astype(BF16), dyb, tn, preferred_element_type=F32)
            gc_ref[1] += lax.dot_general(xs_ref[1, pl.ds(r0, rows), :].astype(BF16), dyb, tn, preferred_element_type=F32)
            gd_ref[...] += jnp.sum(dyv * uv, axis=0, keepdims=True)
            return carry
        lax.fori_loop(0, s // rows, project, 0)

    col = lambda j: (0, j)
    return pl.pallas_call(
        body, name=name, grid=(nblk,),
        in_specs=[pl.BlockSpec((s, sb), col), pl.BlockSpec((s, sb), col), pl.BlockSpec(memory_space=pl.ANY),
                  pl.BlockSpec((sb, sbn), lambda j: (j, j)), pl.BlockSpec((sb, sbn), lambda j: (j, nblk + j)),
                  pl.BlockSpec((sbn, sb), lambda j: (j, j)), pl.BlockSpec((sbn, sb), lambda j: (nblk + j, j)),
                  pl.BlockSpec((2, 1, sbn), lambda j: (0, 0, j)), pl.BlockSpec((1, sb), col)],
        out_specs=[pl.BlockSpec((s, sb), col), pl.BlockSpec((sb, 2 * sbn), lambda j: (j, 0)),
                   pl.BlockSpec((2, sbn, sb), lambda j: (0, j, 0)), pl.BlockSpec((2, 1, sbn), lambda j: (0, 0, j)),
                   pl.BlockSpec((1, sb), col)],
        out_shape=[jax.ShapeDtypeStruct((s, w), F32), jax.ShapeDtypeStruct((w, 2 * sbn), F32),
                   jax.ShapeDtypeStruct((2, nst, sb), F32), jax.ShapeDtypeStruct((2, 1, nst), F32),
                   jax.ShapeDtypeStruct((1, w), F32)],
        scratch_shapes=[pltpu.VMEM((2, s, sbn), F32), pltpu.VMEM((2, s, sbn), F32), pltpu.SemaphoreType.DMA],
        compiler_params=pltpu.CompilerParams(dimension_semantics=("arbitrary",), vmem_limit_bytes=SSM_VMEM_LIMIT_BYTES),
    )(dy, u, xs, bd, bd, cd, cd, a, d_row)


def _adamw(w, g, m, v, name):
    r, c = w.shape
    tr = r
    for cand in (512, 256, 128, 64, 32, 16, 8):
        if r % cand == 0 and cand * c * 4 <= 2 * 1024 * 1024:
            tr = cand
            break

    def body(w_ref, g_ref, m_ref, v_ref, d_ref, nm_ref, nv_ref):
        gv = g_ref[...]
        nm = ADAM_B1 * m_ref[...] + (1.0 - ADAM_B1) * gv
        nv = ADAM_B2 * v_ref[...] + (1.0 - ADAM_B2) * (gv * gv)
        m_hat = nm / (1.0 - ADAM_B1 ** ADAM_STEP)
        v_hat = nv / (1.0 - ADAM_B2 ** ADAM_STEP)
        d_ref[...] = -ADAM_LR * (m_hat / (jnp.sqrt(v_hat) + ADAM_EPS) + ADAM_WD * w_ref[...])
        nm_ref[...] = nm
        nv_ref[...] = nv

    spec = pl.BlockSpec((tr, c), lambda i: (i, 0))
    sds = jax.ShapeDtypeStruct((r, c), F32)
    return pl.pallas_call(body, name=name, grid=(r // tr,), in_specs=[spec] * 4, out_specs=[spec] * 3,
                          out_shape=[sds] * 3, compiler_params=_cparams(("parallel",)))(w, g, m, v)


def _sum_lead(x, name, out_dtype=F32):
    n, r, c = x.shape
    tr = r
    for cand in (512, 256, 128, 64, 32, 16, 8):
        if r % cand == 0 and n * cand * c * 4 <= 4 * 1024 * 1024:
            tr = cand
            break

    def body(x_ref, o_ref):
        acc = x_ref[0].astype(F32)
        for k in range(1, n):
            acc = acc + x_ref[k].astype(F32)
        o_ref[...] = acc.astype(o_ref.dtype)

    return pl.pallas_call(body, name=name, grid=(r // tr,),
                          in_specs=[pl.BlockSpec((n, tr, c), lambda i: (0, i, 0))],
                          out_specs=pl.BlockSpec((tr, c), lambda i: (i, 0)),
                          out_shape=jax.ShapeDtypeStruct((r, c), out_dtype),
                          compiler_params=_cparams(("parallel",)))(x)


def _row_tile(rows, row_bytes, budget, least=8):
    for cand in (1024, 512, 256, 128, 64, 32, 16, 8):
        if cand >= least and rows % cand == 0 and cand * row_bytes <= budget:
            return cand
    return rows


def _cast_into_slot(w, slot, name):
    r, c = w.shape
    tr = _row_tile(r, c * 4, 4 * 1024 * 1024, least=16)

    def body(slot_ref, w_ref, o_ref):
        o_ref[...] = w_ref[...].astype(o_ref.dtype)

    gs = pltpu.PrefetchScalarGridSpec(
        num_scalar_prefetch=1, grid=(r // tr,),
        in_specs=[pl.BlockSpec((tr, c), lambda i, s: (i, 0))],
        out_specs=pl.BlockSpec((None, tr, c), lambda i, s: (s[0], i, 0)))
    return pl.pallas_call(body, name=name, grid_spec=gs, out_shape=jax.ShapeDtypeStruct((N_CHIPS, r, c), BF16),
                          compiler_params=_cparams(("parallel",)))(slot, w)


def _sum_own(p, t, sel, name):
    _, h, c = p.shape
    tr = _row_tile(h, c * 4, 2 * 1024 * 1024, least=16)
    nblk = h // tr

    def body(sel_ref, p_ref, t_ref, o_ref):
        acc = p_ref[...].astype(F32)
        for k in range(3):
            acc = acc + t_ref[k].astype(F32)
        o_ref[...] = acc

    gs = pltpu.PrefetchScalarGridSpec(
        num_scalar_prefetch=1, grid=(nblk,),
        in_specs=[pl.BlockSpec((None, tr, c), lambda i, s: (s[0], i, 0)),
                  pl.BlockSpec((3, tr, c), lambda i, s: (0, i, 0))],
        out_specs=pl.BlockSpec((tr, c), lambda i, s: (s[1] * nblk + i, 0)))
    return pl.pallas_call(body, name=name, grid_spec=gs, out_shape=jax.ShapeDtypeStruct((2 * h, c), F32),
                          compiler_params=_cparams(("parallel",)))(sel, p, t)


def _add_half(g, t, half, name):
    n, r, c = g.shape
    h = r // 2
    tr = h
    for cand in (512, 256, 128, 64, 32, 16):
        if h % cand == 0 and cand * c * 2 <= 2 * 1024 * 1024:
            tr = cand
            break
    nblk = h // tr

    def body(half_ref, g_ref, t_ref, o_ref):
        o_ref[...] = (g_ref[...].astype(F32) + t_ref[...].astype(F32)).astype(o_ref.dtype)

    gs = pltpu.PrefetchScalarGridSpec(
        num_scalar_prefetch=1, grid=(n, nblk),
        in_specs=[pl.BlockSpec((None, tr, c), lambda j, i, hr: (j, hr[0] * nblk + i, 0)),
                  pl.BlockSpec((None, tr, c), lambda j, i, hr: (j, i, 0))],
        out_specs=pl.BlockSpec((None, tr, c), lambda j, i, hr: (j, i, 0)))
    return pl.pallas_call(body, name=name, grid_spec=gs, out_shape=jax.ShapeDtypeStruct((n, h, c), BF16),
                          compiler_params=_cparams(("parallel", "parallel")))(half, g, t)


def _position():
    x, y, c = lax.axis_index("x"), lax.axis_index("y"), lax.axis_index("c")
    return x, y, c


def _allgather8(xs, name):
    m_per, n = xs.shape

    def body(x_ref, out_ref, send_sems, recv_sems, local_sem):
        x, y, c = _position()
        me, sibling = (x, y, c), (x, y, 1 - c)
        chips = [(1 - x, y), (x, 1 - y), (1 - x, 1 - y)]

        def rows(px, py, pc):
            return out_ref.at[pl.ds((4 * px + 2 * py + pc) * m_per, m_per), :]

        def copy(k, block, to, src=None):
            return pltpu.make_async_remote_copy(
                src_ref=rows(*block) if src is None else src, dst_ref=rows(*block),
                send_sem=send_sems.at[k], recv_sem=recv_sems.at[k], device_id=to, device_id_type=MESH)

        mine = pltpu.make_async_copy(x_ref, rows(*me), local_sem)
        mine.start()
        first = [copy(0, me, sibling, src=x_ref)]
        first += [copy(1 + j, me, (*chip, c), src=x_ref) for j, chip in enumerate(chips)]
        for cp in first:
            cp.start()
        passed = [copy(4 + j, (*chip, c), sibling) for j, chip in enumerate(chips)]
        for j, chip in enumerate(chips):
            copy(1 + j, (*chip, c), me).wait_recv()
            passed[j].start()
        copy(0, sibling, me).wait_recv()
        for j, chip in enumerate(chips):
            copy(4 + j, (*chip, 1 - c), me).wait_recv()
        for cp in first + passed:
            cp.wait_send()
        mine.wait()

    return pl.pallas_call(
        body, name=name, out_shape=jax.ShapeDtypeStruct((N_DEV * m_per, n), xs.dtype),
        in_specs=[pl.BlockSpec(memory_space=pltpu.VMEM)], out_specs=pl.BlockSpec(memory_space=pltpu.VMEM),
        scratch_shapes=[pltpu.SemaphoreType.DMA((7,)), pltpu.SemaphoreType.DMA((7,)), pltpu.SemaphoreType.DMA],
        compiler_params=pltpu.CompilerParams(vmem_limit_bytes=VMEM_LIMIT_BYTES),
    )(xs)


_HBM = pl.BlockSpec(memory_space=pltpu.HBM)


_SEM = pl.BlockSpec(memory_space=pltpu.SEMAPHORE)
_ANY = pl.BlockSpec(memory_space=pl.ANY)
_EFFECT = pltpu.SideEffectType.DATAFLOW_SIDE_EFFECTING


def _in_hbm(a):
    return pltpu.with_memory_space_constraint(a, pltpu.HBM)


def _gather_start(ws, groups, after, name):
    n = len(ws)

    def body(*refs):
        in_refs = refs[:n]
        sems, token = refs[2 * n + 1:-1], refs[-1]
        x, y, c = _position()
        mychip = 2 * x + y
        chips = [(1 - x, y), (x, 1 - y), (1 - x, 1 - y)]
        for g, members in enumerate(groups):
            for k, i in enumerate(members):
                h = ws[i].shape[1] // 2
                mine = in_refs[i].at[mychip, pl.ds(c * h, h), :]
                for j, (px, py) in enumerate(chips):
                    pltpu.make_async_remote_copy(
                        src_ref=mine, dst_ref=mine, send_sem=sems[2 * g].at[3 * k + j],
                        recv_sem=sems[2 * g + 1].at[3 * k + j], device_id=(px, py, c), device_id_type=MESH).start()
        token[...] = jnp.zeros_like(token)

    sem_shapes = [pltpu.SemaphoreType.DMA((3 * len(m),)) for m in groups for _ in range(2)]
    res = pl.pallas_call(
        body, name=name,
        out_shape=[pltpu.HBM(w.shape, w.dtype) for w in ws] + sem_shapes + [jax.ShapeDtypeStruct((8, 128), F32)],
        in_specs=[_HBM] * n + [_ANY],
        out_specs=[_HBM] * n + [_SEM] * len(sem_shapes) + [pl.BlockSpec(memory_space=pltpu.VMEM)],
        input_output_aliases={i: i for i in range(n)},
        compiler_params=pltpu.CompilerParams(has_side_effects=_EFFECT),
    )(*[_in_hbm(w) for w in ws], after)
    bufs, sems, token = res[:n], res[n:-1], res[-1]
    return list(bufs), [(sems[2 * g], sems[2 * g + 1]) for g in range(len(groups))], token


def _gather_wait(bufs, send_sems, recv_sems, after, name):
    m = len(bufs)

    def body(*refs):
        in_refs = refs[:m]
        send, recv = refs[m], refs[m + 1]
        x, y, c = _position()
        mychip = 2 * x + y
        chips = [(1 - x, y), (x, 1 - y), (1 - x, 1 - y)]
        for k in range(m):
            h = bufs[k].shape[1] // 2
            mine = in_refs[k].at[mychip, pl.ds(c * h, h), :]
            for j, (px, py) in enumerate(chips):
                cp = pltpu.make_async_remote_copy(
                    src_ref=mine, dst_ref=in_refs[k].at[2 * px + py, pl.ds(c * h, h), :],
                    send_sem=send.at[3 * k + j], recv_sem=recv.at[3 * k + j],
                    device_id=(px, py, c), device_id_type=MESH)
                cp.wait_send()
                cp.wait_recv()

    res = pl.pallas_call(
        body, name=name, out_shape=[pltpu.HBM(b.shape, b.dtype) for b in bufs],
        in_specs=[_HBM] * m + [_SEM, _SEM, _ANY], out_specs=[_HBM] * m,
        input_output_aliases={k: k for k in range(m)},
        compiler_params=pltpu.CompilerParams(has_side_effects=_EFFECT),
    )(*bufs, send_sems, recv_sems, after)
    return list(res)


def _forward_halves(ws, name):
    n = len(ws)

    def body(*refs):
        out_refs = refs[n:2 * n]
        send_sems, recv_sems = refs[2 * n:]
        x, y, c = _position()
        me, sibling = (x, y, c), (x, y, 1 - c)
        chips = [(1 - x, y), (x, 1 - y), (1 - x, 1 - y)]
        cps = []
        for i in range(n):
            h = ws[i].shape[1] // 2
            for j, (px, py) in enumerate(chips):
                got = out_refs[i].at[2 * px + py, pl.ds(c * h, h), :]
                cp = pltpu.make_async_remote_copy(
                    src_ref=got, dst_ref=got, send_sem=send_sems.at[3 * i + j], recv_sem=recv_sems.at[3 * i + j],
                    device_id=sibling, device_id_type=MESH)
                cp.start()
                cps.append(cp)
        for i in range(n):
            h = ws[i].shape[1] // 2
            for j, (px, py) in enumerate(chips):
                other = out_refs[i].at[2 * px + py, pl.ds((1 - c) * h, h), :]
                pltpu.make_async_remote_copy(
                    src_ref=other, dst_ref=other, send_sem=send_sems.at[3 * i + j], recv_sem=recv_sems.at[3 * i + j],
                    device_id=me, device_id_type=MESH).wait_recv()
        for cp in cps:
            cp.wait_send()

    return pl.pallas_call(
        body, name=name,
        out_shape=[jax.ShapeDtypeStruct(w.shape, w.dtype) for w in ws],
        in_specs=[_HBM] * n, out_specs=[_HBM] * n, input_output_aliases={i: i for i in range(n)},
        scratch_shapes=[pltpu.SemaphoreType.DMA((3 * n,)), pltpu.SemaphoreType.DMA((3 * n,))],
    )(*ws)


def _swap_halves(gs, name):
    n = len(gs)

    def body(*refs):
        in_refs, out_refs = refs[:n], refs[n:2 * n]
        send_sems, recv_sems = refs[2 * n:]
        x, y, c = _position()
        cps = []
        for i in range(n):
            h = gs[i].shape[1] // 2
            cp = pltpu.make_async_remote_copy(
                src_ref=in_refs[i].at[:, pl.ds((1 - c) * h, h), :], dst_ref=out_refs[i],
                send_sem=send_sems.at[i], recv_sem=recv_sems.at[i], device_id=(x, y, 1 - c), device_id_type=MESH)
            cp.start()
            cps.append(cp)
        for cp in cps:
            cp.wait()

    return pl.pallas_call(
        body, name=name,
        out_shape=[jax.ShapeDtypeStruct((g.shape[0], g.shape[1] // 2, g.shape[2]), g.dtype) for g in gs],
        in_specs=[_HBM] * n, out_specs=[_HBM] * n,
        scratch_shapes=[pltpu.SemaphoreType.DMA((n,)), pltpu.SemaphoreType.DMA((n,))],
    )(*gs)


def _scatter_copies(p_refs, land_refs, send, recv):
    x, y, c = _position()
    chips = [(1 - x, y), (x, 1 - y), (1 - x, 1 - y)]
    return [pltpu.make_async_remote_copy(
        src_ref=p_refs[i].at[2 * px + py], dst_ref=land_refs[i].at[j],
        send_sem=send.at[3 * i + j], recv_sem=recv.at[3 * i + j], device_id=(px, py, c), device_id_type=MESH)
        for i in range(len(p_refs)) for j, (px, py) in enumerate(chips)]


def _scatter_start(ps, name):
    n = len(ps)
    lands = [lax.empty((3,) + p.shape[1:], p.dtype) for p in ps]

    def body(*refs):
        for cp in _scatter_copies(refs[:n], refs[n:2 * n], refs[4 * n], refs[4 * n + 1]):
            cp.start()
        refs[4 * n + 2][...] = jnp.zeros_like(refs[4 * n + 2])

    res = pl.pallas_call(
        body, name=name,
        out_shape=[pltpu.HBM(a.shape, a.dtype) for a in list(ps) + lands]
        + [pltpu.SemaphoreType.DMA((3 * n,)), pltpu.SemaphoreType.DMA((3 * n,)), jax.ShapeDtypeStruct((8, 128), F32)],
        in_specs=[_HBM] * (2 * n),
        out_specs=[_HBM] * (2 * n) + [_SEM, _SEM, pl.BlockSpec(memory_space=pltpu.VMEM)],
        input_output_aliases={i: i for i in range(2 * n)},
        compiler_params=pltpu.CompilerParams(has_side_effects=_EFFECT),
    )(*[_in_hbm(a) for a in list(ps) + lands])
    return list(res[:n]), list(res[n:2 * n]), res[2 * n], res[2 * n + 1], res[2 * n + 2]


def _scatter_wait(ps, lands, send_sems, recv_sems, after, name):
    n = len(ps)

    def body(*refs):
        for cp in _scatter_copies(refs[:n], refs[n:2 * n], refs[2 * n], refs[2 * n + 1]):
            cp.wait_send()
            cp.wait_recv()

    res = pl.pallas_call(
        body, name=name, out_shape=[pltpu.HBM(a.shape, a.dtype) for a in list(ps) + list(lands)],
        in_specs=[_HBM] * (2 * n) + [_SEM, _SEM, _ANY], out_specs=[_HBM] * (2 * n),
        input_output_aliases={i: i for i in range(2 * n)},
        compiler_params=pltpu.CompilerParams(has_side_effects=_EFFECT),
    )(*ps, *lands, send_sems, recv_sems, after)
    return list(res[:n]), list(res[n:])


def _join_halves(rs, name):
    n = len(rs)

    def body(*refs):
        out_refs = refs[n:2 * n]
        send_sems, recv_sems = refs[2 * n:]
        x, y, c = _position()
        cps = []
        for i in range(n):
            h = rs[i].shape[0] // 2
            mine = out_refs[i].at[pl.ds(c * h, h), :]
            cp = pltpu.make_async_remote_copy(
                src_ref=mine, dst_ref=mine, send_sem=send_sems.at[i], recv_sem=recv_sems.at[i],
                device_id=(x, y, 1 - c), device_id_type=MESH)
            cp.start()
            cps.append(cp)
        for i in range(n):
            h = rs[i].shape[0] // 2
            other = out_refs[i].at[pl.ds((1 - c) * h, h), :]
            pltpu.make_async_remote_copy(
                src_ref=other, dst_ref=other, send_sem=send_sems.at[i], recv_sem=recv_sems.at[i],
                device_id=(x, y, c), device_id_type=MESH).wait_recv()
        for cp in cps:
            cp.wait_send()

    return pl.pallas_call(
        body, name=name,
        out_shape=[jax.ShapeDtypeStruct(r.shape, r.dtype) for r in rs],
        in_specs=[_HBM] * n, out_specs=[_HBM] * n, input_output_aliases={i: i for i in range(n)},
        scratch_shapes=[pltpu.SemaphoreType.DMA((n,)), pltpu.SemaphoreType.DMA((n,))],
    )(*rs)


def _t5_buckets_block():
    qi = np.arange(BLOCK)[:, None]
    ki = np.arange(2 * BLOCK)[None, :]
    n = np.maximum(qi + BLOCK - ki, 0)
    max_exact = NUM_BUCKETS // 2
    large = max_exact + (np.log(np.maximum(n, 1) / max_exact) / np.log(MAX_DISTANCE / max_exact)
                         * (NUM_BUCKETS - max_exact)).astype(np.int32)
    large = np.minimum(large, NUM_BUCKETS - 1)
    return np.where(n < max_exact, n, large).astype(np.int32)


def _discretise(lambda_re, lambda_im, log_step, b_re, b_im):
    lam_re = jnp.minimum(lambda_re, -1e-4)
    lam_im = lambda_im
    delta = jnp.exp(log_step)[:, None]
    mag = jnp.exp(lam_re * delta)
    ang = lam_im * delta
    abar_re, abar_im = mag * jnp.cos(ang), mag * jnp.sin(ang)
    num_re, num_im = abar_re - 1.0, abar_im
    den = lam_re * lam_re + lam_im * lam_im
    f_re = (num_re * lam_re + num_im * lam_im) / den
    f_im = (num_im * lam_re - num_re * lam_im) / den
    bbar_re = f_re[..., None] * b_re - f_im[..., None] * b_im
    bbar_im = f_re[..., None] * b_im + f_im[..., None] * b_re
    return abar_re, abar_im, bbar_re, bbar_im


def _interleave(v, nc):
    s, w = v.shape
    return v.reshape(nc, s // nc, w).transpose(1, 0, 2).reshape(s, w)


def _deinterleave(v, nc):
    s, w = v.shape
    return v.reshape(s // nc, nc, w).transpose(1, 0, 2).reshape(s, w)


_SMALL = ("norm1_g", "b_in", "attn_sinks", "rel_bias", "lambda_re", "lambda_im", "log_step", "ssm_b_re",
          "ssm_b_im", "ssm_c_re", "ssm_c_im", "ssm_d", "b_glu", "norm2_g", "final_g")


def _pack(parts):
    rows = []
    for p in parts:
        f = p.reshape(-1).astype(F32)
        pad = (-f.shape[0]) % 128
        rows.append(jnp.pad(f, (0, pad)).reshape(-1, 128))
    out = jnp.concatenate(rows, axis=0)
    pad = (-out.shape[0]) % 256
    return jnp.pad(out, ((0, pad), (0, 0)))


def _unpack(packed, shapes):
    res, r = [], 0
    for shp in shapes:
        size = int(np.prod(shp))
        nr = -(-size // 128)
        res.append(packed[r:r + nr].reshape(-1)[:size].reshape(shp))
        r += nr
    return res


def kernel(x, c, w_ada, b_ada, norm1_g, w_in, b_in, attn_sinks, rel_bias, lambda_re, lambda_im, log_step, ssm_b_re, ssm_b_im, ssm_c_re, ssm_c_im, ssm_d, w_glu, b_glu, w_attn_proj, w_ssm_proj, w_out, norm2_g, w_ff1, w_ff2, final_g, loss_target, m_w_ada, m_b_ada, m_norm1_g, m_w_in, m_b_in, m_attn_sinks, m_rel_bias, m_lambda_re, m_lambda_im, m_log_step, m_ssm_b_re, m_ssm_b_im, m_ssm_c_re, m_ssm_c_im, m_ssm_d, m_w_glu, m_b_glu, m_w_attn_proj, m_w_ssm_proj, m_w_out, m_norm2_g, m_w_ff1, m_w_ff2, m_final_g, v_w_ada, v_b_ada, v_norm1_g, v_w_in, v_b_in, v_attn_sinks, v_rel_bias, v_lambda_re, v_lambda_im, v_log_step, v_ssm_b_re, v_ssm_b_im, v_ssm_c_re, v_ssm_c_im, v_ssm_d, v_w_glu, v_b_glu, v_w_attn_proj, v_w_ssm_proj, v_w_out, v_norm2_g, v_w_ff1, v_w_ff2, v_final_g):
    given = dict(locals())
    S, D = x.shape[1], x.shape[2]
    SSM_W = w_glu.shape[2]
    G = SSM_W // SSM_GROUP_CH
    NST = G * SSM_STATE
    DFF = w_ff2.shape[1] * N_CHIPS
    INW = w_in.shape[2] * N_CHIPS
    o_q, o_k, o_v, o_u = 0, ATTN_WIDTH, ATTN_WIDTH + KV_WIDTH, ATTN_WIDTH + 2 * KV_WIDTH
    o_ga, o_gs = o_u + SSM_W, o_u + SSM_W + D
    mx, my, mc = _position()
    my_chip = 2 * mx + my
    my_b = 4 * mx + 2 * my + mc

    xv, tgt = x[0], loss_target[0]

    big = dict(w_in=w_in[0], w_glu=w_glu[0], w_attn_proj=w_attn_proj[0], w_ssm_proj=w_ssm_proj[0],
               w_out=w_out[0], w_ff1=w_ff1[0], w_ff2=w_ff2[0])
    big_names = list(big)
    colsharded = {"w_in", "w_attn_proj", "w_ssm_proj", "w_ff1"}
    chip_sel = my_chip.astype(jnp.int32).reshape(1)
    gather_groups = [["w_in"], ["w_attn_proj", "w_ssm_proj", "w_glu", "w_out"], ["w_ff1", "w_ff2"]]
    in_flight, gather_sems, gathered = {}, [], {}

    def finish_gather(g, after):
        bufs = [in_flight[k] for k in gather_groups[g]]
        bufs = _gather_wait(bufs, gather_sems[g][0], gather_sems[g][1], after, "gather_wait_%d" % g)
        gathered.update(zip(gather_groups[g], _forward_halves(bufs, "gather_forward_%d" % g)))

    def tied(v, token):
        return v + token[0:1, 0:1]

    def all_of(*arrays):
        return jnp.stack([a.reshape(-1)[0].astype(F32) for a in arrays])

    def wop(k):
        g = gathered[k]
        return _Op(g, N_CHIPS) if k in colsharded else _Op(g.reshape(g.shape[0] * g.shape[1], g.shape[2]))

    grads = {}
    half = mc.astype(jnp.int32).reshape(1)
    sel = jnp.stack([my_chip, mc]).astype(jnp.int32)

    def rs_begin(tag, named):
        keys, gl = list(named), []
        for k in keys:
            gk = named[k]
            if k not in colsharded:
                gk = gk.reshape(N_CHIPS, gk.shape[0] // N_CHIPS, gk.shape[1])
            gl.append(gk)
        t1 = _swap_halves(gl, "rs_swap_" + tag)
        ps = [_add_half(g, t, half, "rs_add_" + k) for g, t, k in zip(gl, t1, keys)]
        ps, lands, ssem, rsem, token = _scatter_start(ps, "rs_start_" + tag)
        return (keys, ps, lands, ssem, rsem), token

    def rs_end(tag, state, after):
        keys, ps, lands, ssem, rsem = state
        ps, lands = _scatter_wait(ps, lands, ssem, rsem, after, "rs_wait_" + tag)
        rs = [_sum_own(p, t, sel, "rs_sum_" + k) for p, t, k in zip(ps, lands, keys)]
        full = _join_halves(rs, "rs_join_" + tag)
        for k, f in zip(keys, full):
            grads[k] = f[None]
        return full[-1]

    c_all = _allgather8(jnp.pad(c, ((0, 7), (0, 0))), "gather_c").reshape(N_DEV, 8, D)[:, 0]
    c16 = jnp.pad(c_all, ((0, 8), (0, 0)))
    b_ada_mine = lax.dynamic_slice(b_ada.reshape(N_CHIPS, -1), (my_chip, 0), (1, w_ada.shape[2]))
    mod_sh = _mm(c16, w_ada[0], "NN", name="mod", M=16, N=w_ada.shape[2], K=D, a_fn=_silu,
                 epilogue=lambda acc, b: (acc + b,), extras=[(b_ada_mine, "row")])
    mod_all = _allgather8(mod_sh[:8], "gather_mod").reshape(N_DEV, 8, -1)
    mod_row = jnp.concatenate(
        [lax.dynamic_slice(mod_all, (2 * j, my_b, 0), (1, 1, mod_all.shape[2]))[0] for j in range(N_CHIPS)], axis=1)
    sh1, sc1, g1, sh2, sc2, g2 = [mod_row[:, i * D:(i + 1) * D] for i in range(6)]

    first = [_cast_into_slot(big["w_in"], chip_sel, "cast_w_in")]
    first, sems_first, token_first = _gather_start(first, [[0]], mod_all, "gather_start_in")
    rest_names = gather_groups[1] + gather_groups[2]
    rest = [_cast_into_slot(big[k], chip_sel, "cast_" + k) for k in rest_names]
    rest, sems_rest, token_rest = _gather_start(
        rest, [[rest_names.index(k) for k in grp] for grp in gather_groups[1:]], token_first, "gather_start_rest")
    in_flight.update(zip(["w_in"] + rest_names, first + rest))
    gather_sems.extend(sems_first + sems_rest)

    disc_in = (lambda_re[0], lambda_im[0], log_step[0], ssm_b_re[0], ssm_b_im[0])
    (abar_re, abar_im, bbar_re, bbar_im), disc_vjp = jax.vjp(_discretise, *disc_in)
    same_group = jnp.asarray(np.arange(SSM_W)[:, None] // SSM_GROUP_CH == np.arange(NST)[None, :] // SSM_STATE)

    def block_diag(t):
        return jnp.where(same_group, jnp.tile(t, (G, 1)), 0.0)

    bd = jnp.concatenate([block_diag(bb.transpose(2, 0, 1).reshape(SSM_GROUP_CH, NST)) for bb in (bbar_re, bbar_im)],
                         axis=1)
    cd = jnp.concatenate([block_diag(cc.transpose(1, 0, 2).reshape(SSM_GROUP_CH, NST)).T
                          for cc in (ssm_c_re[0], -ssm_c_im[0])], axis=0)
    a_fwd = jnp.stack([abar_re.reshape(1, NST), abar_im.reshape(1, NST)])
    a_bwd = jnp.stack([abar_re.reshape(1, NST), -abar_im.reshape(1, NST)])
    d_row = ssm_d

    buckets = _t5_buckets_block()
    onehot_t = (jnp.arange(128, dtype=jnp.int32)[:, None] == jnp.asarray(buckets.reshape(1, -1))).astype(BF16)
    rb_hi = rel_bias.astype(BF16)
    rb_lo = (rel_bias - rb_hi.astype(F32)).astype(BF16)
    rb_lo2 = (rel_bias - rb_hi.astype(F32) - rb_lo.astype(F32)).astype(BF16)
    rb3 = jnp.pad(jnp.concatenate([rb_hi.T, rb_lo.T, rb_lo2.T], axis=0), ((0, 0), (0, 128 - NUM_BUCKETS)))
    b3 = _mm(rb3, onehot_t, "NN", name="rel_bias_rows", M=3 * N_Q_HEADS, N=BLOCK * 2 * BLOCK, K=128, tj=4096)
    bias = (b3[:N_Q_HEADS] + b3[N_Q_HEADS:2 * N_Q_HEADS]) + b3[2 * N_Q_HEADS:]
    bias = bias.reshape(N_Q_HEADS, BLOCK, 2 * BLOCK)
    sinks_b = jnp.broadcast_to(attn_sinks[0][:, None, None], (N_Q_HEADS, BLOCK, 128)).reshape(N_Q_HEADS * BLOCK, 128)

    def two(fn):
        def both(*blocks):
            r = fn(*blocks)
            return r, r
        return both

    h1, h1_t = _rowwise(two(_norm_mod), [(xv, "tile", D), (tied(tied(norm1_g, token_first), token_rest), "row", D),
                                         (sh1, "row", D), (sc1, "row", D)],
                        [(D, BF16), (D, BF16, "T")], [], name="norm1", rows=S)
    finish_gather(0, all_of(h1, bd, cd, a_fwd, a_bwd, bias, sinks_b))
    proj = _mm(h1, wop("w_in"), "NN", name="proj", M=S, N=INW, K=D,
               epilogue=lambda acc, b: (acc + b,), extras=[(b_in, "row")])

    def heads(v2d, nh):
        return v2d.reshape(S, nh, HEAD_DIM).transpose(1, 0, 2)

    def unheads(v3d):
        return v3d.transpose(1, 0, 2).reshape(S, -1)

    qh = heads(proj[:, o_q:o_k], N_Q_HEADS)
    kh = heads(proj[:, o_k:o_v], N_KV_HEADS)
    vh = heads(proj[:, o_v:o_u], N_KV_HEADS)
    attn = unheads(_attn_fwd(qh, kh, vh, sinks_b, bias, "attn_fwd"))
    finish_gather(1, attn)
    y_attn = _mm(attn, wop("w_attn_proj"), "NN", name="attn_proj", M=S, N=D, K=ATTN_WIDTH, out_dtypes=(BF16,))

    u = proj[:, o_u:o_ga]
    u_il = _interleave(u, SCAN_CHUNKS)
    SB = 128
    nsb, gpb = SSM_W // SB, SB // SSM_GROUP_CH
    SBN = gpb * SSM_STATE
    y_il, xs = _ssm_fwd(u_il, bd, cd, a_fwd, d_row, name="ssm_fwd", sb=SB, sbn=SBN)
    y = _deinterleave(y_il, SCAN_CHUNKS)
    z0b = _rowwise(_gelu, [(y, "tile", SSM_W)], [(SSM_W, BF16)], [], name="gelu", rows=S)[0]
    z, t_glu = _mm(z0b, wop("w_glu"), "NN", name="glu", M=S, N=SSM_W, K=SSM_W, out_dtypes=(BF16, F32),
                   epilogue=lambda acc, b, yy: (_gelu(yy) * _sigmoid(acc + b), acc + b),
                   extras=[(b_glu, "row"), (y, "tile")])
    y_ssm = _mm(z, wop("w_ssm_proj"), "NN", name="ssm_proj", M=S, N=D, K=SSM_W, out_dtypes=(BF16,))

    merged, merged_t = _rowwise(two(_merge), [(_Op(proj, coff=o_ga), "tile", D), (_Op(proj, coff=o_gs), "tile", D),
                                              (y_attn, "tile", D), (y_ssm, "tile", D)],
                                [(D, BF16), (D, BF16, "T")], [], name="merge", rows=S)
    mo, x2 = _mm(merged, wop("w_out"), "NN", name="out_proj", M=S, N=D, K=D, out_dtypes=(BF16, F32),
                 epilogue=lambda acc, xx, gg: (acc, xx + gg * acc), extras=[(xv, "tile"), (g1, "row")])
    h2, h2_t = _rowwise(two(_norm_mod), [(x2, "tile", D), (norm2_g, "row", D), (sh2, "row", D), (sc2, "row", D)],
                        [(D, BF16), (D, BF16, "T")], [], name="norm2", rows=S)
    finish_gather(2, h2)
    a_b, r_b = _mm(h2, wop("w_ff1"), "NN", name="ff1", M=S, N=DFF, K=D, out_dtypes=(BF16, BF16),
                   epilogue=lambda acc: (acc, jnp.square(jnp.maximum(acc, 0.0))))
    ff, x3 = _mm(r_b, wop("w_ff2"), "NN", name="ff2", M=S, N=D, K=DFF, out_dtypes=(BF16, F32),
                 epilogue=lambda acc, xx, gg: (acc, xx + gg * acc), extras=[(x2, "tile"), (g2, "row")],
                 tj=1024, tk=1024)

    def final_fn(x3b, gf, tb):
        def f(xx, gg):
            yv = xx * lax.rsqrt(jnp.mean(xx * xx, axis=-1, keepdims=True) + EPS) * gg
            err = jnp.square(yv - tb)
            return 0.5 * jnp.sum(jnp.mean(err, axis=-1, keepdims=True), axis=0, keepdims=True)
        lv, vjp = jax.vjp(f, x3b, gf)
        dx, dg = vjp(jnp.ones((1, 1), F32))
        return dx, dg, jnp.broadcast_to(lv, (1, 128))

    dx3, g_final, loss_acc = _rowwise(final_fn, [(x3, "tile", D), (final_g.reshape(1, D), "row", D), (tgt, "tile", D)],
                                      [(D, F32)], [D, 128], name="final", rows=S)

    def ff_out_bwd(dx3b, ffb, g2b):
        return dx3b * g2b, jnp.sum(dx3b * ffb, axis=0, keepdims=True)

    dff, d_g2 = _rowwise(ff_out_bwd, [(dx3, "tile", D), (ff, "tile", D), (g2, "row", D)], [(D, BF16)], [D],
                         name="ff_out_bwd", rows=S)
    da = _mm(dff, wop("w_ff2"), "NT", name="ff2_dx", M=S, N=DFF, K=D, out_dtypes=(BF16,),
             epilogue=lambda acc, ab: (acc * (2.0 * jnp.maximum(ab.astype(F32), 0.0)),), extras=[(a_b, "tile")])
    g_w_ff2 = _mm(r_b, dff, "TN", name="ff2_dw", M=DFF, N=D, K=S, out_dtypes=(BF16,), tj=1024, tk=1024)
    dh2 = _mm(da, wop("w_ff1"), "NT", name="ff1_dx", M=S, N=D, K=DFF, tj=1024, tk=1024)
    g_w_ff1 = _mm(h2_t, da, "NN", name="ff1_dw", M=D, N=DFF, K=S, out_dtypes=(BF16,), out_nsh=N_CHIPS, tj=1024, tk=1024)
    rs_ff, token_ff = rs_begin("ff", dict(w_ff2=g_w_ff2, w_ff1=g_w_ff1))

    def norm2_bwd(x2b, dh2b, dx3b, mob, gn, shb, scb, g1b):
        _, vjp = jax.vjp(_norm_mod, x2b, gn, shb, scb)
        dx, dg, dsh, dsc = vjp(dh2b)
        dx2b = dx + dx3b
        return dx2b, dx2b * g1b, dg, dsh, dsc, jnp.sum(dx2b * mob, axis=0, keepdims=True)

    dx2, dmo, g_norm2, d_sh2, d_sc2, d_g1 = _rowwise(
        norm2_bwd, [(x2, "tile", D), (dh2, "tile", D), (dx3, "tile", D), (mo, "tile", D),
                    (tied(norm2_g, token_ff), "row", D), (sh2, "row", D), (sc2, "row", D), (g1, "row", D)],
        [(D, F32), (D, BF16)], [D, D, D, D], name="norm2_bwd", rows=S, tr=128)
    dmerged = _mm(dmo, wop("w_out"), "NT", name="out_dx", M=S, N=D, K=D)
    g_w_out = _mm(merged_t, dmo, "NN", name="out_dw", M=D, N=D, K=S, out_dtypes=(BF16,), tk=1024)

    def merge_bwd(gab, gsb, yab, ysb, dmb):
        _, vjp = jax.vjp(_merge, gab, gsb, yab, ysb)
        return vjp(dmb)

    d_ga, d_gs, dy_attn, dy_ssm = _rowwise(
        merge_bwd, [(_Op(proj, coff=o_ga), "tile", D), (_Op(proj, coff=o_gs), "tile", D), (y_attn, "tile", D),
                    (y_ssm, "tile", D), (dmerged, "tile", D)],
        [(D, BF16), (D, BF16), (D, BF16), (D, BF16)], [], name="merge_bwd", rows=S, tr=128)

    dattn = _mm(dy_attn, wop("w_attn_proj"), "NT", name="attn_proj_dx", M=S, N=ATTN_WIDTH, K=D)
    g_w_attn_proj = _mm(attn, dy_attn, "TN", name="attn_proj_dw", M=ATTN_WIDTH, N=D, K=S, out_dtypes=(BF16,),
                        out_nsh=N_CHIPS, tk=1024)
    dqh, dkh, dvh, dsink_blk, dbias = _attn_bwd(qh, kh, vh, heads(dattn, N_Q_HEADS), sinks_b, bias, "attn_bwd")
    g_sinks = _sum_lead(dsink_blk.reshape(N_Q_HEADS, BLOCK, 128).transpose(1, 0, 2), "sinks_dw")[:, 0].reshape(1, N_Q_HEADS)
    g_rel = _mm(dbias.reshape(N_Q_HEADS, -1), onehot_t, "NT", name="rel_bias_dw", M=N_Q_HEADS, N=128,
                K=BLOCK * 2 * BLOCK, tk=4096)
    g_rel_bias = g_rel[:, :NUM_BUCKETS].T

    dz = _mm(dy_ssm, wop("w_ssm_proj"), "NT", name="ssm_proj_dx", M=S, N=SSM_W, K=D)
    g_w_ssm_proj = _mm(z, dy_ssm, "TN", name="ssm_proj_dw", M=SSM_W, N=D, K=S, out_dtypes=(BF16,),
                       out_nsh=N_CHIPS, tk=1024)

    def glu_bwd(dzb, yb, tb):
        z0 = _gelu(yb)
        sg = _sigmoid(tb)
        dt = dzb * z0 * sg * (1.0 - sg)
        return dt, dzb * sg, jnp.sum(dt, axis=0, keepdims=True)

    dt_b, dz0a, g_b_glu = _rowwise(glu_bwd, [(dz, "tile", SSM_W), (y, "tile", SSM_W), (t_glu, "tile", SSM_W)],
                                   [(SSM_W, BF16), (SSM_W, F32)], [SSM_W], name="glu_bwd", rows=S)

    def gelu_bwd(acc, dz0ab, yb):
        _, vjp = jax.vjp(_gelu, yb)
        return (vjp(acc + dz0ab)[0],)

    dy = _mm(dt_b, wop("w_glu"), "NT", name="glu_dx", M=S, N=SSM_W, K=SSM_W, epilogue=gelu_bwd,
             extras=[(dz0a, "tile"), (y, "tile")])
    g_w_glu = _mm(z0b, dt_b, "TN", name="glu_dw", M=SSM_W, N=SSM_W, K=S, out_dtypes=(BF16,), tk=1024)
    rs_mix, token_mix = rs_begin("mix", dict(w_out=g_w_out, w_attn_proj=g_w_attn_proj, w_ssm_proj=g_w_ssm_proj,
                                             w_glu=g_w_glu))
    dy_il = _interleave(tied(dy, token_mix), SCAN_CHUNKS)
    du_il, g_bd, g_cd, d_abar, g_ssm_d = _ssm_bwd(dy_il, u_il, xs, bd, cd, a_bwd, d_row, name="ssm_bwd", sb=SB, sbn=SBN)
    du = _deinterleave(du_il, SCAN_CHUNKS)

    eye_b = jnp.eye(gpb, dtype=F32)
    g_cd6 = g_cd.reshape(2, nsb, gpb, SSM_STATE, gpb, SSM_GROUP_CH)
    g_c_re = jnp.einsum("bgnhp,gh->bgpn", g_cd6[0], eye_b).reshape(G, SSM_GROUP_CH, SSM_STATE)
    g_c_im = -jnp.einsum("bgnhp,gh->bgpn", g_cd6[1], eye_b).reshape(G, SSM_GROUP_CH, SSM_STATE)
    g_bd6 = g_bd.reshape(nsb, gpb, SSM_GROUP_CH, 2, gpb, SSM_STATE)
    g_bbar = jnp.einsum("bhprgn,hg->rbhnp", g_bd6, eye_b).reshape(2, G, SSM_STATE, SSM_GROUP_CH)
    g_bbar_re, g_bbar_im = g_bbar[0], g_bbar[1]
    g_lre, g_lim, g_lstep, g_bre, g_bim = disc_vjp(
        (d_abar[0].reshape(G, SSM_STATE), d_abar[1].reshape(G, SSM_STATE), g_bbar_re, g_bbar_im))

    dproj = jnp.concatenate([unheads(dqh).astype(BF16), unheads(dkh).astype(BF16), unheads(dvh).astype(BF16),
                             du.astype(BF16), d_ga, d_gs], axis=1)
    g_w_in = _mm(h1_t, dproj, "NN", name="proj_dw", M=D, N=INW, K=S, out_dtypes=(BF16,), out_nsh=N_CHIPS,
                 tj=INW // (2 * N_CHIPS), tk=1024)
    rs_in, token_in = rs_begin("in", dict(w_in=g_w_in))
    dh1 = _mm(dproj, wop("w_in"), "NT", name="proj_dx", M=S, N=D, K=INW, tj=1024, tk=INW // N_CHIPS,
              epilogue=lambda acc, zero: (acc + zero,), extras=[(tied(jnp.zeros((1, D), F32), token_in), "row")])
    g_b_in = _rowwise(lambda d: (jnp.sum(d.astype(F32), axis=0, keepdims=True),), [(dproj, "tile", INW)], [], [INW],
                      name="proj_db", rows=S)[0]

    def norm1_bwd(xb, dhb, dresb, gn, shb, scb):
        _, vjp = jax.vjp(_norm_mod, xb, gn, shb, scb)
        dx, dg, dsh, dsc = vjp(dhb)
        return dx + dresb, dg, dsh, dsc

    grad_x, g_norm1, d_sh1, d_sc1 = _rowwise(
        norm1_bwd, [(xv, "tile", D), (dh1, "tile", D), (dx2, "tile", D), (norm1_g, "row", D), (sh1, "row", D),
                    (sc1, "row", D)], [(D, F32)], [D, D, D], name="norm1_bwd", rows=S)

    dmod_row = jnp.concatenate([d_sh1, d_sc1, d_g1, d_sh2, d_sc2, d_g2], axis=1)
    dmod_all = _allgather8(jnp.pad(dmod_row, ((0, 7), (0, 0))), "gather_dmod").reshape(N_DEV, 8, -1)[:, 0]
    g_b_ada = _sum_lead(dmod_all.reshape(N_DEV, -1, 128), "b_ada_dw").reshape(1, -1)
    dmod_mine = lax.dynamic_slice(dmod_all.reshape(N_DEV, N_CHIPS, -1), (0, my_chip, 0), (N_DEV, 1, w_ada.shape[2]))[:, 0]
    g_w_ada = _mm(c16, jnp.pad(dmod_mine, ((0, 8), (0, 0))), "TN", name="ada_dw", M=D, N=w_ada.shape[2], K=16,
                  a_fn=_silu)

    small_g = dict(norm1_g=g_norm1, b_in=g_b_in, attn_sinks=g_sinks, rel_bias=g_rel_bias, lambda_re=g_lre[None],
                   lambda_im=g_lim[None], log_step=g_lstep[None], ssm_b_re=g_bre[None], ssm_b_im=g_bim[None],
                   ssm_c_re=g_c_re[None], ssm_c_im=g_c_im[None], ssm_d=g_ssm_d, b_glu=g_b_glu, norm2_g=g_norm2,
                   final_g=g_final.reshape(D))
    packed = _pack([loss_acc[:, :1]] + [small_g[k] for k in _SMALL])
    rows = packed.shape[0]
    summed = _sum_lead(_allgather8(packed, "gather_small").reshape(N_DEV, rows, 128), "small_sum")
    small_shapes = [(1,)] + [given[k].shape for k in _SMALL]
    parts = _unpack(summed, small_shapes)
    loss = parts[0].reshape(())
    grads.update(zip(_SMALL, parts[1:]))
    grads["b_ada"] = g_b_ada
    grads["w_ada"] = g_w_ada[None]

    deltas, new_m, new_v = {}, {}, {}

    def adamw_big(k):
        d_, m_, v_ = _adamw(given[k][0], grads[k][0], given["m_" + k][0], given["v_" + k][0], "adamw_" + k)
        deltas[k], new_m[k], new_v[k] = d_[None], m_[None], v_[None]
        return v_

    rs_end("ff", rs_ff, summed)
    marks = [adamw_big(k) for k in ("w_ff2", "w_ff1")]
    rs_end("mix", rs_mix, all_of(*marks))
    marks = [adamw_big(k) for k in ("w_out", "w_attn_proj", "w_ssm_proj", "w_glu", "w_ada")]
    small_all = list(_SMALL) + ["b_ada"]
    shapes = [given[k].shape for k in small_all]
    pw, pg = _pack([given[k] for k in small_all]), _pack([grads[k] for k in small_all])
    pm, pv = _pack([given["m_" + k] for k in small_all]), _pack([given["v_" + k] for k in small_all])
    d_, m_, v_ = _adamw(pw, pg, pm, pv, "adamw_small")
    for k, dd, mm, vv in zip(small_all, _unpack(d_, shapes), _unpack(m_, shapes), _unpack(v_, shapes)):
        deltas[k], new_m[k], new_v[k] = dd, mm, vv
        grads[k] = grads[k].reshape(given[k].shape)
    rs_end("in", rs_in, all_of(v_, *marks))
    adamw_big("w_in")

    names = ["w_ada", "b_ada", "norm1_g", "w_in", "b_in", "attn_sinks", "rel_bias", "lambda_re", "lambda_im",
             "log_step", "ssm_b_re", "ssm_b_im", "ssm_c_re", "ssm_c_im", "ssm_d", "w_glu", "b_glu", "w_attn_proj",
             "w_ssm_proj", "w_out", "norm2_g", "w_ff1", "w_ff2", "final_g"]
    return (loss, grad_x[None], *[grads[n] for n in names], *[deltas[n] for n in names],
            *[new_m[n] for n in names], *[new_v[n] for n in names])
```

```python
import math

import numpy as np
import jax
import jax.numpy as jnp
from jax import lax
from jax.experimental import pallas as pl
from jax.experimental.pallas import tpu as pltpu

F32 = jnp.float32
BF16 = jnp.bfloat16
MESH = pl.DeviceIdType.MESH

HEAD_DIM = 64
N_Q_HEADS = 16
N_KV_HEADS = 4
GQA_GROUP = N_Q_HEADS // N_KV_HEADS
ATTN_WIDTH = N_Q_HEADS * HEAD_DIM
KV_WIDTH = N_KV_HEADS * HEAD_DIM
BLOCK = 128
NUM_BUCKETS = 32
MAX_DISTANCE = 128
NEG_INF = -1e30
SSM_GROUP_CH = 16
SSM_STATE = 64
EPS = 1e-6
ADAM_LR = 0.001
ADAM_B1 = 0.9
ADAM_B2 = 0.999
ADAM_EPS = 1e-08
ADAM_WD = 0.01
ADAM_STEP = 10

N_CHIPS = 4
N_DEV = 8
SCAN_CHUNKS = 8
VMEM_LIMIT_BYTES = 48 * 1024 * 1024
SSM_VMEM_LIMIT_BYTES = 56 * 1024 * 1024


def _cparams(sem=None):
    return pltpu.CompilerParams(dimension_semantics=sem, vmem_limit_bytes=VMEM_LIMIT_BYTES)


class _Op:
    def __init__(self, arr, nsh=None, coff=0):
        self.arr, self.nsh, self.coff = arr, nsh, coff
        if nsh is None:
            self.rows, self.cols = arr.shape
        else:
            assert arr.shape[0] == nsh
            self.rows, self.cols = arr.shape[1], arr.shape[2] * nsh

    def spec(self, br, bc, idx):
        assert self.coff % bc == 0
        off = self.coff // bc
        if self.nsh is None:
            return pl.BlockSpec((br, bc), lambda *g: (idx(*g)[0], idx(*g)[1] + off))
        per = (self.cols // self.nsh) // bc
        assert per * bc * self.nsh == self.cols

        def imap(*g):
            r, c = idx(*g)
            c = c + off
            return (c // per, r, c % per)
        return pl.BlockSpec((None, br, bc), imap)


def _as_op(a):
    return a if isinstance(a, _Op) else _Op(a)


def _mm(a, b, mode, *, name, M, N, K, out_dtypes=(F32,), out_nsh=None, epilogue=None, extras=(),
        a_fn=None, ti=1024, tj=512, tk=2048, deps=()):
    a_idx = b_idx = None
    nd = len(deps)
    a, b = _as_op(a), _as_op(b)
    ti, tj, tk = min(ti, M), min(tj, N), min(tk, K)
    a_w = a.cols // a.nsh if a.nsh else None
    b_w = b.cols // b.nsh if b.nsh else None
    if a_w:
        ti, tk = (min(ti, a_w), tk) if mode == "TN" else (ti, min(tk, a_w))
    if b_w:
        tj, tk = (tj, min(tk, b_w)) if mode == "NT" else (min(tj, b_w), tk)
    if out_nsh:
        tj = min(tj, N // out_nsh)
    assert M % ti == 0 and N % tj == 0 and K % tk == 0, (name, M, N, K, ti, tj, tk)
    nk = K // tk
    if mode == "NN":
        a_spec = a.spec(ti, tk, a_idx or (lambda i, j, k: (i, k)))
        b_spec = b.spec(tk, tj, b_idx or (lambda i, j, k: (k, j)))
        dims = (((1,), (0,)), ((), ()))
    elif mode == "NT":
        a_spec = a.spec(ti, tk, a_idx or (lambda i, j, k: (i, k)))
        b_spec = b.spec(tj, tk, b_idx or (lambda i, j, k: (j, k)))
        dims = (((1,), (1,)), ((), ()))
    else:
        a_spec = a.spec(tk, ti, a_idx or (lambda i, j, k: (k, i)))
        b_spec = b.spec(tk, tj, b_idx or (lambda i, j, k: (k, j)))
        dims = (((0,), (0,)), ((), ()))
    ex_specs, ex_arrs = [], []
    for op, kind in extras:
        op = _as_op(op)
        if kind == "tile":
            ex_specs.append(op.spec(ti, tj, lambda i, j, k: (i, j)))
        else:
            ex_specs.append(op.spec(1, tj, lambda i, j, k: (0, j)))
        ex_arrs.append(op.arr)
    ne, no = len(ex_arrs), len(out_dtypes)
    if out_nsh is None:
        out_shapes = [jax.ShapeDtypeStruct((M, N), d) for d in out_dtypes]
        out_specs = [pl.BlockSpec((ti, tj), lambda i, j, k: (i, j)) for _ in out_dtypes]
    else:
        per = (N // out_nsh) // tj
        assert per * tj * out_nsh == N
        out_shapes = [jax.ShapeDtypeStruct((out_nsh, M, N // out_nsh), d) for d in out_dtypes]
        out_specs = [pl.BlockSpec((None, ti, tj), lambda i, j, k: (j // per, i, j % per)) for _ in out_dtypes]

    def body(a_ref, b_ref, *rest):
        ex_refs, out_refs, acc = rest[:ne], rest[ne + nd:ne + nd + no], rest[ne + nd + no]
        k = pl.program_id(2)

        @pl.when(k == 0)
        def _():
            acc[...] = jnp.zeros_like(acc)

        av = a_ref[...]
        if a_fn is not None:
            av = a_fn(av)
        acc[...] += lax.dot_general(av.astype(BF16), b_ref[...].astype(BF16), dims,
                                    preferred_element_type=F32)

        @pl.when(k == nk - 1)
        def _():
            res = acc[...]
            outs = epilogue(res, *[r[...] for r in ex_refs]) if epilogue is not None else (res,)
            for o_ref, o in zip(out_refs, outs):
                o_ref[...] = o.astype(o_ref.dtype)

    outs = pl.pallas_call(
        body, name=name, grid=(M // ti, N // tj, nk),
        in_specs=[a_spec, b_spec] + ex_specs + [pl.BlockSpec(memory_space=pl.ANY)] * nd,
        out_specs=out_specs, out_shape=out_shapes,
        scratch_shapes=[pltpu.VMEM((ti, tj), F32)],
        compiler_params=_cparams(("parallel", "parallel", "arbitrary")),
    )(a.arr, b.arr, *ex_arrs, *deps)
    return outs[0] if no == 1 else outs


def _rowwise(fn, ins, outs, accs, *, name, rows, tr=256, deps=()):
    tr = min(tr, rows)
    assert rows % tr == 0
    in_specs, arrs = [], []
    for op, kind, width in ins:
        op = _as_op(op)
        if kind == "tile":
            in_specs.append(op.spec(tr, width, lambda i: (i, 0)))
        else:
            in_specs.append(op.spec(op.rows, width, lambda i: (0, 0)))
        arrs.append(op.arr)
    ni, no, na = len(ins), len(outs), len(accs)
    flipped = [len(o) == 3 for o in outs]
    out_shapes = [jax.ShapeDtypeStruct((o[0], rows) if t else (rows, o[0]), o[1]) for o, t in zip(outs, flipped)]
    out_specs = [pl.BlockSpec((o[0], tr), lambda i: (0, i)) if t else pl.BlockSpec((tr, o[0]), lambda i: (i, 0))
                 for o, t in zip(outs, flipped)]
    out_shapes += [jax.ShapeDtypeStruct((1, w), F32) for w in accs]
    out_specs += [pl.BlockSpec((1, w), lambda i: (0, 0)) for w in accs]

    def body(*refs):
        nd = len(deps)
        in_refs, out_refs, acc_refs = refs[:ni], refs[ni + nd:ni + nd + no], refs[ni + nd + no:]
        res = fn(*[r[...] for r in in_refs])
        if not isinstance(res, (tuple, list)):
            res = (res,)
        for o_ref, r, t in zip(out_refs, res[:no], flipped):
            o_ref[...] = (r.astype(F32).T if t else r).astype(o_ref.dtype)
        if na:
            @pl.when(pl.program_id(0) == 0)
            def _():
                for a_ref in acc_refs:
                    a_ref[...] = jnp.zeros_like(a_ref)
            for a_ref, r in zip(acc_refs, res[no:]):
                a_ref[...] += r.astype(F32)

    res = pl.pallas_call(
        body, name=name, grid=(rows // tr,), in_specs=in_specs + [pl.BlockSpec(memory_space=pl.ANY)] * len(deps),
        out_specs=out_specs, out_shape=out_shapes, compiler_params=_cparams(("arbitrary",)),
    )(*arrs, *deps)
    return res


def _norm_mod(x, g, sh, sc):
    y = x * lax.rsqrt(jnp.mean(x * x, axis=-1, keepdims=True) + EPS) * g
    return y * (1.0 + sc) + sh


def _sigmoid(x):
    return 1.0 / (1.0 + jnp.exp(-x))


def _silu(x):
    return x * _sigmoid(x)


def _gelu(x):
    return 0.5 * x * (1.0 + jnp.tanh(math.sqrt(2.0 / math.pi) * (x + 0.044715 * (x * x * x))))


def _merge(ga, gs, ya, ys):
    return _sigmoid(ga) * ya + _sigmoid(gs) * ys


def _attn_head(q, kp, kc, vp, vc, sink, bias_p, bias_c, not_first):
    nt = (((1,), (1,)), ((), ()))
    nn = (((1,), (0,)), ((), ()))
    qb = q.astype(BF16)
    scale = HEAD_DIM ** -0.5
    sp = lax.dot_general(qb, kp.astype(BF16), nt, preferred_element_type=F32) * scale + bias_p
    sc = lax.dot_general(qb, kc.astype(BF16), nt, preferred_element_type=F32) * scale + bias_c
    qi = lax.broadcasted_iota(jnp.int32, sp.shape, 0) & (BLOCK - 1)
    ki = lax.broadcasted_iota(jnp.int32, sp.shape, 1)
    sp = jnp.where(jnp.logical_and(ki > qi, not_first), sp, NEG_INF)
    sc = jnp.where(ki <= qi, sc, NEG_INF)
    m = jnp.maximum(jnp.maximum(jnp.max(sp, axis=-1, keepdims=True), jnp.max(sc, axis=-1, keepdims=True)), sink)
    m = lax.stop_gradient(m)
    pp = jnp.exp(sp - m)
    pc = jnp.exp(sc - m)
    denom = jnp.sum(pp, axis=-1, keepdims=True) + jnp.sum(pc, axis=-1, keepdims=True) + jnp.exp(sink - m)
    o = lax.dot_general((pp / denom).astype(BF16), vp.astype(BF16), nn, preferred_element_type=F32)
    o = o + lax.dot_general((pc / denom).astype(BF16), vc.astype(BF16), nn, preferred_element_type=F32)
    return o


def _attn_fwd(qh, kh, vh, sinks, bias, name):
    s = qh.shape[1]
    nb = s // BLOCK
    G = GQA_GROUP
    R = G * BLOCK

    def body(q_ref, kp_ref, kc_ref, vp_ref, vc_ref, sink_ref, bias_ref, o_ref):
        not_first = pl.program_id(0) > 0
        for kv in range(N_KV_HEADS):
            hs = slice(kv * G, (kv + 1) * G)
            o = _attn_head(q_ref[hs].reshape(R, HEAD_DIM), kp_ref[kv], kc_ref[kv], vp_ref[kv], vc_ref[kv],
                           sink_ref[kv * R:(kv + 1) * R, 0:1],
                           bias_ref[hs, :, 0:BLOCK].reshape(R, BLOCK), bias_ref[hs, :, BLOCK:2 * BLOCK].reshape(R, BLOCK),
                           not_first)
            o_ref[hs] = o.reshape(G, BLOCK, HEAD_DIM).astype(o_ref.dtype)

    cur = lambda i: (0, i, 0)
    prev = lambda i: (0, jnp.maximum(i - 1, 0), 0)
    return pl.pallas_call(
        body, name=name, grid=(nb,),
        in_specs=[pl.BlockSpec((N_Q_HEADS, BLOCK, HEAD_DIM), cur),
                  pl.BlockSpec((N_KV_HEADS, BLOCK, HEAD_DIM), prev), pl.BlockSpec((N_KV_HEADS, BLOCK, HEAD_DIM), cur),
                  pl.BlockSpec((N_KV_HEADS, BLOCK, HEAD_DIM), prev), pl.BlockSpec((N_KV_HEADS, BLOCK, HEAD_DIM), cur),
                  pl.BlockSpec((N_Q_HEADS * BLOCK, 128), lambda i: (0, 0)),
                  pl.BlockSpec((N_Q_HEADS, BLOCK, 2 * BLOCK), lambda i: (0, 0, 0))],
        out_specs=pl.BlockSpec((N_Q_HEADS, BLOCK, HEAD_DIM), cur),
        out_shape=jax.ShapeDtypeStruct((N_Q_HEADS, s, HEAD_DIM), BF16),
        compiler_params=_cparams(("arbitrary",)),
    )(qh, kh, kh, vh, vh, sinks, bias)


def _attn_bwd(qh, kh, vh, doh, sinks, bias, name):
    s = qh.shape[1]
    nb = s // BLOCK
    G = GQA_GROUP
    R = G * BLOCK

    def body(q_ref, kp_ref, kc_ref, vp_ref, vc_ref, do_ref, sink_ref, bias_ref,
             dq_ref, dk_ref, dv_ref, dsink_ref, dbias_ref, ck, cv):
        i = pl.program_id(1)

        @pl.when(i == 0)
        def _():
            dsink_ref[...] = jnp.zeros_like(dsink_ref)
            dbias_ref[...] = jnp.zeros_like(dbias_ref)
            ck[...] = jnp.zeros_like(ck)
            cv[...] = jnp.zeros_like(cv)

        @pl.when(i < nb)
        def _():
            not_first = i > 0
            _, vjp = jax.vjp(lambda q, a, b, c, d, sk, e, f: _attn_head(q, a, b, c, d, sk, e, f, not_first),
                             q_ref[...].reshape(R, HEAD_DIM), kp_ref[...], kc_ref[...], vp_ref[...], vc_ref[...],
                             sink_ref[:, 0:1], bias_ref[:, :, 0:BLOCK].reshape(R, BLOCK),
                             bias_ref[:, :, BLOCK:2 * BLOCK].reshape(R, BLOCK))
            dq, dkp, dkc, dvp, dvc, dsk, dbp, dbc = vjp(do_ref[...].reshape(R, HEAD_DIM).astype(F32))
            dq_ref[...] = dq.reshape(G, BLOCK, HEAD_DIM)
            dsink_ref[...] += jnp.broadcast_to(dsk, (R, 128))
            dbias_ref[:, :, 0:BLOCK] += dbp.reshape(G, BLOCK, BLOCK)
            dbias_ref[:, :, BLOCK:2 * BLOCK] += dbc.reshape(G, BLOCK, BLOCK)
            dk_ref[...] = ck[...] + dkp
            dv_ref[...] = cv[...] + dvp
            ck[...] = dkc
            cv[...] = dvc

        @pl.when(i == nb)
        def _():
            dk_ref[...] = ck[...]
            dv_ref[...] = cv[...]

    qcur = lambda kv, i: (kv, jnp.minimum(i, nb - 1), 0)
    kcur = lambda kv, i: (kv, jnp.minimum(i, nb - 1), 0)
    kprev = lambda kv, i: (kv, jnp.clip(i - 1, 0, nb - 1), 0)
    qspec = pl.BlockSpec((G, BLOCK, HEAD_DIM), qcur)
    kc_spec = pl.BlockSpec((None, BLOCK, HEAD_DIM), kcur)
    kp_spec = pl.BlockSpec((None, BLOCK, HEAD_DIM), kprev)
    return pl.pallas_call(
        body, name=name, grid=(N_KV_HEADS, nb + 1),
        in_specs=[qspec, kp_spec, kc_spec, kp_spec, kc_spec, qspec,
                  pl.BlockSpec((R, 128), lambda kv, i: (kv, 0)),
                  pl.BlockSpec((G, BLOCK, 2 * BLOCK), lambda kv, i: (kv, 0, 0))],
        out_specs=[qspec, kp_spec, kp_spec,
                   pl.BlockSpec((R, 128), lambda kv, i: (kv, 0)),
                   pl.BlockSpec((G, BLOCK, 2 * BLOCK), lambda kv, i: (kv, 0, 0))],
        out_shape=[jax.ShapeDtypeStruct((N_Q_HEADS, s, HEAD_DIM), F32),
                   jax.ShapeDtypeStruct((N_KV_HEADS, s, HEAD_DIM), F32),
                   jax.ShapeDtypeStruct((N_KV_HEADS, s, HEAD_DIM), F32),
                   jax.ShapeDtypeStruct((N_Q_HEADS * BLOCK, 128), F32),
                   jax.ShapeDtypeStruct((N_Q_HEADS, BLOCK, 2 * BLOCK), F32)],
        scratch_shapes=[pltpu.VMEM((BLOCK, HEAD_DIM), F32), pltpu.VMEM((BLOCK, HEAD_DIM), F32)],
        compiler_params=_cparams(("arbitrary", "arbitrary")),
    )(qh, kh, kh, vh, vh, doh, sinks, bias)


def _cmul(ar, ai, br, bi):
    return ar * br - ai * bi, ar * bi + ai * br


def _scan_passes(a_ref, b_ref, x_ref, xp_ref, da_ref, *, s, tc, reverse):
    nc = SCAN_CHUNKS
    steps = s // nc
    with_da = xp_ref is not None
    unroll = 8 if steps % 8 == 0 else 1

    def shift(v, d):
        row = lax.broadcasted_iota(jnp.int32, v.shape, 0)
        if reverse:
            return jnp.where(row < nc - d, pltpu.roll(v, nc - d, 0), 0.0)
        return jnp.where(row >= d, pltpu.roll(v, d, 0), 0.0)

    def run():
        ar = jnp.broadcast_to(a_ref[0], (nc, tc))
        ai = jnp.broadcast_to(a_ref[1], (nc, tc))

        def row_of(step):
            j = (steps - 1 - step) if reverse else step
            return pl.multiple_of(j * nc, nc)

        def p1(step, st):
            sr, si = st
            r0 = row_of(step)
            mr, mi = _cmul(ar, ai, sr, si)
            sr = mr + b_ref[0, pl.ds(r0, nc), :]
            si = mi + b_ref[1, pl.ds(r0, nc), :]
            x_ref[0, pl.ds(r0, nc), :] = sr
            x_ref[1, pl.ds(r0, nc), :] = si
            return sr, si
        zero = jnp.zeros((nc, tc), F32)
        er, ei = lax.fori_loop(0, steps, p1, (zero, zero), unroll=unroll)

        pr, pi_ = jnp.ones((nc, tc), F32), zero
        br, bi, left = ar, ai, steps
        while left:
            if left & 1:
                pr, pi_ = _cmul(pr, pi_, br, bi)
            br, bi = _cmul(br, bi, br, bi)
            left >>= 1
        cr, ci = shift(er, 1), shift(ei, 1)
        d = 1
        while d < nc:
            mr, mi = _cmul(pr, pi_, shift(cr, d), shift(ci, d))
            cr, ci = cr + mr, ci + mi
            pr, pi_ = _cmul(pr, pi_, pr, pi_)
            d *= 2

        def p2(step, st):
            qr, qi, dar, dai = st
            r0 = row_of(step)
            qr, qi = _cmul(ar, ai, qr, qi)
            fr, fi = _cmul(qr, qi, cr, ci)
            xr = x_ref[0, pl.ds(r0, nc), :] + fr
            xi = x_ref[1, pl.ds(r0, nc), :] + fi
            x_ref[0, pl.ds(r0, nc), :] = xr
            x_ref[1, pl.ds(r0, nc), :] = xi
            if with_da:
                jm = jnp.where(step == steps - 1, steps - 1, steps - 2 - step)
                rp = pl.multiple_of(jm * nc, nc)
                vr, vi = xp_ref[0, pl.ds(rp, nc), :], xp_ref[1, pl.ds(rp, nc), :]
                row = lax.broadcasted_iota(jnp.int32, (nc, tc), 0)
                first = step == steps - 1
                sel = jnp.logical_and(first, row == 0)
                vr = jnp.where(sel, 0.0, jnp.where(first, pltpu.roll(vr, 1, 0), vr))
                vi = jnp.where(sel, 0.0, jnp.where(first, pltpu.roll(vi, 1, 0), vi))
                dar = dar + xr * vr + xi * vi
                dai = dai + xi * vr - xr * vi
            return qr, qi, dar, dai
        _, _, dar, dai = lax.fori_loop(0, steps, p2, (jnp.ones((nc, tc), F32), zero, zero, zero), unroll=unroll)
        if with_da:
            da_ref[0] = jnp.sum(dar, axis=0, keepdims=True)
            da_ref[1] = jnp.sum(dai, axis=0, keepdims=True)

    run()


def _ssm_fwd(u, bd, cd, a, d_row, *, name, sb, sbn):
    s, w = u.shape
    nst = a.shape[2]
    nblk = w // sb
    rows = min(512, s)
    nn = (((1,), (0,)), ((), ()))

    def body(u_ref, bre_ref, bim_ref, cre_ref, cim_ref, a_ref, d_ref, y_ref, x_ref):

        def fill(r, carry):
            r0 = pl.multiple_of(r * rows, rows)
            ub = u_ref[pl.ds(r0, rows), :].astype(BF16)
            x_ref[0, pl.ds(r0, rows), :] = lax.dot_general(ub, bre_ref[...].astype(BF16), nn, preferred_element_type=F32)
            x_ref[1, pl.ds(r0, rows), :] = lax.dot_general(ub, bim_ref[...].astype(BF16), nn, preferred_element_type=F32)
            return carry
        lax.fori_loop(0, s // rows, fill, 0)
        _scan_passes(a_ref, x_ref, x_ref, None, None, s=s, tc=sbn, reverse=False)

        def project(r, carry):
            r0 = pl.multiple_of(r * rows, rows)
            y = lax.dot_general(x_ref[0, pl.ds(r0, rows), :].astype(BF16), cre_ref[...].astype(BF16), nn, preferred_element_type=F32)
            y = y + lax.dot_general(x_ref[1, pl.ds(r0, rows), :].astype(BF16), cim_ref[...].astype(BF16), nn, preferred_element_type=F32)
            y_ref[pl.ds(r0, rows), :] = y + d_ref[...] * u_ref[pl.ds(r0, rows), :]
            return carry
        lax.fori_loop(0, s // rows, project, 0)

    return pl.pallas_call(
        body, name=name, grid=(nblk,),
        in_specs=[pl.BlockSpec((s, sb), lambda j: (0, j)),
                  pl.BlockSpec((sb, sbn), lambda j: (j, j)), pl.BlockSpec((sb, sbn), lambda j: (j, nblk + j)),
                  pl.BlockSpec((sbn, sb), lambda j: (j, j)), pl.BlockSpec((sbn, sb), lambda j: (nblk + j, j)),
                  pl.BlockSpec((2, 1, sbn), lambda j: (0, 0, j)), pl.BlockSpec((1, sb), lambda j: (0, j))],
        out_specs=[pl.BlockSpec((s, sb), lambda j: (0, j)), pl.BlockSpec((2, s, sbn), lambda j: (0, 0, j))],
        out_shape=[jax.ShapeDtypeStruct((s, w), F32), jax.ShapeDtypeStruct((2, s, nst), F32)],
        compiler_params=pltpu.CompilerParams(dimension_semantics=("arbitrary",), vmem_limit_bytes=SSM_VMEM_LIMIT_BYTES),
    )(u, bd, bd, cd, cd, a, d_row)


def _ssm_bwd(dy, u, xs, bd, cd, a, d_row, *, name, sb, sbn):
    s, w = u.shape
    nst = a.shape[2]
    nblk = w // sb
    rows = min(512, s)
    nt = (((1,), (1,)), ((), ()))
    tn = (((0,), (0,)), ((), ()))

    def body(dy_ref, u_ref, xs_hbm, bre_ref, bim_ref, cre_ref, cim_ref, a_ref, d_ref,
             du_ref, gb_ref, gc_ref, da_ref, gd_ref, lam, xs_ref, sem):
        j = pl.program_id(0)
        fetch = pltpu.make_async_copy(xs_hbm.at[:, :, pl.ds(pl.multiple_of(j * sbn, sbn), sbn)], xs_ref, sem)
        fetch.start()

        def fill(r, carry):
            r0 = pl.multiple_of(r * rows, rows)
            dyb = dy_ref[pl.ds(r0, rows), :].astype(BF16)
            lam[0, pl.ds(r0, rows), :] = lax.dot_general(dyb, cre_ref[...].astype(BF16), nt, preferred_element_type=F32)
            lam[1, pl.ds(r0, rows), :] = lax.dot_general(dyb, cim_ref[...].astype(BF16), nt, preferred_element_type=F32)
            return carry
        lax.fori_loop(0, s // rows, fill, 0)
        fetch.wait()
        _scan_passes(a_ref, lam, lam, xs_ref, da_ref, s=s, tc=sbn, reverse=True)
        gb_ref[...] = jnp.zeros_like(gb_ref)
        gc_ref[...] = jnp.zeros_like(gc_ref)
        gd_ref[...] = jnp.zeros_like(gd_ref)

        def project(r, carry):
            r0 = pl.multiple_of(r * rows, rows)
            dyv, uv = dy_ref[pl.ds(r0, rows), :], u_ref[pl.ds(r0, rows), :]
            dyb, ub = dyv.astype(BF16), uv.astype(BF16)
            lr, li = lam[0, pl.ds(r0, rows), :].astype(BF16), lam[1, pl.ds(r0, rows), :].astype(BF16)
            du = lax.dot_general(lr, bre_ref[...].astype(BF16), nt, preferred_element_type=F32)
            du = du + lax.dot_general(li, bim_ref[...].astype(BF16), nt, preferred_element_type=F32)
            du_ref[pl.ds(r0, rows), :] = du + d_ref[...] * dyv
            gb_ref[:, 0:sbn] += lax.dot_general(ub, lr, tn, preferred_element_type=F32)
            gb_ref[:, sbn:2 * sbn] += lax.dot_general(ub, li, tn, preferred_element_type=F32)
            gc_ref[0] += lax.dot_general(xs_ref[0, pl.ds(r0, rows), :].astype(BF16), dyb, tn, preferred_element_type=F32)
            gc_ref[1] += lax.dot_general(xs_ref[1, pl.ds(r0, rows), :].astype(BF16), dyb, tn, preferred_element_type=F32)
            gd_ref[...] += jnp.sum(dyv * uv, axis=0, keepdims=True)
            return carry
        lax.fori_loop(0, s // rows, project, 0)

    col = lambda j: (0, j)
    return pl.pallas_call(
        body, name=name, grid=(nblk,),
        in_specs=[pl.BlockSpec((s, sb), col), pl.BlockSpec((s, sb), col), pl.BlockSpec(memory_space=pl.ANY),
                  pl.BlockSpec((sb, sbn), lambda j: (j, j)), pl.BlockSpec((sb, sbn), lambda j: (j, nblk + j)),
                  pl.BlockSpec((sbn, sb), lambda j: (j, j)), pl.BlockSpec((sbn, sb), lambda j: (nblk + j, j)),
                  pl.BlockSpec((2, 1, sbn), lambda j: (0, 0, j)), pl.BlockSpec((1, sb), col)],
        out_specs=[pl.BlockSpec((s, sb), col), pl.BlockSpec((sb, 2 * sbn), lambda j: (j, 0)),
                   pl.BlockSpec((2, sbn, sb), lambda j: (0, j, 0)), pl.BlockSpec((2, 1, sbn), lambda j: (0, 0, j)),
                   pl.BlockSpec((1, sb), col)],
        out_shape=[jax.ShapeDtypeStruct((s, w), F32), jax.ShapeDtypeStruct((w, 2 * sbn), F32),
                   jax.ShapeDtypeStruct((2, nst, sb), F32), jax.ShapeDtypeStruct((2, 1, nst), F32),
                   jax.ShapeDtypeStruct((1, w), F32)],
        scratch_shapes=[pltpu.VMEM((2, s, sbn), F32), pltpu.VMEM((2, s, sbn), F32), pltpu.SemaphoreType.DMA],
        compiler_params=pltpu.CompilerParams(dimension_semantics=("arbitrary",), vmem_limit_bytes=SSM_VMEM_LIMIT_BYTES),
    )(dy, u, xs, bd, bd, cd, cd, a, d_row)


def _adamw(w, g, m, v, name, deps=()):
    nd = len(deps)
    r, c = w.shape
    tr = r
    for cand in (512, 256, 128, 64, 32, 16, 8):
        if r % cand == 0 and cand * c * 4 <= 2 * 1024 * 1024:
            tr = cand
            break

    def body(w_ref, g_ref, m_ref, v_ref, *rest):
        d_ref, nm_ref, nv_ref = rest[nd:]
        gv = g_ref[...]
        nm = ADAM_B1 * m_ref[...] + (1.0 - ADAM_B1) * gv
        nv = ADAM_B2 * v_ref[...] + (1.0 - ADAM_B2) * (gv * gv)
        m_hat = nm / (1.0 - ADAM_B1 ** ADAM_STEP)
        v_hat = nv / (1.0 - ADAM_B2 ** ADAM_STEP)
        d_ref[...] = -ADAM_LR * (m_hat / (jnp.sqrt(v_hat) + ADAM_EPS) + ADAM_WD * w_ref[...])
        nm_ref[...] = nm
        nv_ref[...] = nv

    spec = pl.BlockSpec((tr, c), lambda i: (i, 0))
    sds = jax.ShapeDtypeStruct((r, c), F32)
    return pl.pallas_call(body, name=name, grid=(r // tr,),
                          in_specs=[spec] * 4 + [pl.BlockSpec(memory_space=pl.ANY)] * nd, out_specs=[spec] * 3,
                          out_shape=[sds] * 3, compiler_params=_cparams(("parallel",)))(w, g, m, v, *deps)


def _sum_lead(x, name, out_dtype=F32):
    n, r, c = x.shape
    tr = r
    for cand in (512, 256, 128, 64, 32, 16, 8):
        if r % cand == 0 and n * cand * c * 4 <= 4 * 1024 * 1024:
            tr = cand
            break

    def body(x_ref, o_ref):
        acc = x_ref[0].astype(F32)
        for k in range(1, n):
            acc = acc + x_ref[k].astype(F32)
        o_ref[...] = acc.astype(o_ref.dtype)

    return pl.pallas_call(body, name=name, grid=(r // tr,),
                          in_specs=[pl.BlockSpec((n, tr, c), lambda i: (0, i, 0))],
                          out_specs=pl.BlockSpec((tr, c), lambda i: (i, 0)),
                          out_shape=jax.ShapeDtypeStruct((r, c), out_dtype),
                          compiler_params=_cparams(("parallel",)))(x)


def _row_tile(rows, row_bytes, budget, least=8):
    for cand in (1024, 512, 256, 128, 64, 32, 16, 8):
        if cand >= least and rows % cand == 0 and cand * row_bytes <= budget:
            return cand
    return rows


def _cast_into_slot(w, slot, name):
    r, c = w.shape
    tr = _row_tile(r, c * 4, 4 * 1024 * 1024, least=16)

    def body(slot_ref, w_ref, o_ref):
        o_ref[...] = w_ref[...].astype(o_ref.dtype)

    gs = pltpu.PrefetchScalarGridSpec(
        num_scalar_prefetch=1, grid=(r // tr,),
        in_specs=[pl.BlockSpec((tr, c), lambda i, s: (i, 0))],
        out_specs=pl.BlockSpec((None, tr, c), lambda i, s: (s[0], i, 0)))
    return pl.pallas_call(body, name=name, grid_spec=gs, out_shape=jax.ShapeDtypeStruct((N_CHIPS, r, c), BF16),
                          compiler_params=_cparams(("parallel",)))(slot, w)


def _sum_own(p, t, sel, name):
    _, h, c = p.shape
    tr = _row_tile(h, c * 4, 2 * 1024 * 1024, least=16)
    nblk = h // tr

    def body(sel_ref, p_ref, t_ref, o_ref):
        acc = p_ref[...].astype(F32)
        for k in range(3):
            acc = acc + t_ref[k].astype(F32)
        o_ref[...] = acc

    gs = pltpu.PrefetchScalarGridSpec(
        num_scalar_prefetch=1, grid=(nblk,),
        in_specs=[pl.BlockSpec((None, tr, c), lambda i, s: (s[0], i, 0)),
                  pl.BlockSpec((3, tr, c), lambda i, s: (0, i, 0))],
        out_specs=pl.BlockSpec((tr, c), lambda i, s: (s[1] * nblk + i, 0)))
    return pl.pallas_call(body, name=name, grid_spec=gs, out_shape=jax.ShapeDtypeStruct((2 * h, c), F32),
                          compiler_params=_cparams(("parallel",)))(sel, p, t)


def _add_half(g, t, half, name):
    n, r, c = g.shape
    h = r // 2
    tr = h
    for cand in (512, 256, 128, 64, 32, 16):
        if h % cand == 0 and cand * c * 2 <= 2 * 1024 * 1024:
            tr = cand
            break
    nblk = h // tr

    def body(half_ref, g_ref, t_ref, o_ref):
        o_ref[...] = (g_ref[...].astype(F32) + t_ref[...].astype(F32)).astype(o_ref.dtype)

    gs = pltpu.PrefetchScalarGridSpec(
        num_scalar_prefetch=1, grid=(n, nblk),
        in_specs=[pl.BlockSpec((None, tr, c), lambda j, i, hr: (j, hr[0] * nblk + i, 0)),
                  pl.BlockSpec((None, tr, c), lambda j, i, hr: (j, i, 0))],
        out_specs=pl.BlockSpec((None, tr, c), lambda j, i, hr: (j, i, 0)))
    return pl.pallas_call(body, name=name, grid_spec=gs, out_shape=jax.ShapeDtypeStruct((n, h, c), BF16),
                          compiler_params=_cparams(("parallel", "parallel")))(half, g, t)


def _position():
    x, y, c = lax.axis_index("x"), lax.axis_index("y"), lax.axis_index("c")
    return x, y, c


def _allgather8(xs, name):
    m_per, n = xs.shape

    def body(x_ref, out_ref, send_sems, recv_sems, local_sem):
        x, y, c = _position()
        me, sibling = (x, y, c), (x, y, 1 - c)
        chips = [(1 - x, y), (x, 1 - y), (1 - x, 1 - y)]

        def rows(px, py, pc):
            return out_ref.at[pl.ds((4 * px + 2 * py + pc) * m_per, m_per), :]

        def copy(k, block, to, src=None):
            return pltpu.make_async_remote_copy(
                src_ref=rows(*block) if src is None else src, dst_ref=rows(*block),
                send_sem=send_sems.at[k], recv_sem=recv_sems.at[k], device_id=to, device_id_type=MESH)

        mine = pltpu.make_async_copy(x_ref, rows(*me), local_sem)
        mine.start()
        first = [copy(0, me, sibling, src=x_ref)]
        first += [copy(1 + j, me, (*chip, c), src=x_ref) for j, chip in enumerate(chips)]
        for cp in first:
            cp.start()
        passed = [copy(4 + j, (*chip, c), sibling) for j, chip in enumerate(chips)]
        for j, chip in enumerate(chips):
            copy(1 + j, (*chip, c), me).wait_recv()
            passed[j].start()
        copy(0, sibling, me).wait_recv()
        for j, chip in enumerate(chips):
            copy(4 + j, (*chip, 1 - c), me).wait_recv()
        for cp in first + passed:
            cp.wait_send()
        mine.wait()

    return pl.pallas_call(
        body, name=name, out_shape=jax.ShapeDtypeStruct((N_DEV * m_per, n), xs.dtype),
        in_specs=[pl.BlockSpec(memory_space=pltpu.VMEM)], out_specs=pl.BlockSpec(memory_space=pltpu.VMEM),
        scratch_shapes=[pltpu.SemaphoreType.DMA((7,)), pltpu.SemaphoreType.DMA((7,)), pltpu.SemaphoreType.DMA],
        compiler_params=pltpu.CompilerParams(vmem_limit_bytes=VMEM_LIMIT_BYTES),
    )(xs)


_HBM = pl.BlockSpec(memory_space=pltpu.HBM)


_SEM = pl.BlockSpec(memory_space=pltpu.SEMAPHORE)
_ANY = pl.BlockSpec(memory_space=pl.ANY)
_EFFECT = pltpu.SideEffectType.DATAFLOW_SIDE_EFFECTING


def _in_hbm(a):
    return pltpu.with_memory_space_constraint(a, pltpu.HBM)


def _gather_start(ws, groups, after, name):
    n = len(ws)

    def body(*refs):
        in_refs = refs[:n]
        sems, token = refs[2 * n + 1:-1], refs[-1]
        x, y, c = _position()
        mychip = 2 * x + y
        chips = [(1 - x, y), (x, 1 - y), (1 - x, 1 - y)]
        for g, members in enumerate(groups):
            for k, i in enumerate(members):
                h = ws[i].shape[1] // 2
                mine = in_refs[i].at[mychip, pl.ds(c * h, h), :]
                for j, (px, py) in enumerate(chips):
                    pltpu.make_async_remote_copy(
                        src_ref=mine, dst_ref=mine, send_sem=sems[2 * g].at[3 * k + j],
                        recv_sem=sems[2 * g + 1].at[3 * k + j], device_id=(px, py, c), device_id_type=MESH).start()
        token[...] = jnp.zeros_like(token)

    sem_shapes = [pltpu.SemaphoreType.DMA((3 * len(m),)) for m in groups for _ in range(2)]
    res = pl.pallas_call(
        body, name=name,
        out_shape=[pltpu.HBM(w.shape, w.dtype) for w in ws] + sem_shapes + [jax.ShapeDtypeStruct((8, 128), F32)],
        in_specs=[_HBM] * n + [_ANY],
        out_specs=[_HBM] * n + [_SEM] * len(sem_shapes) + [pl.BlockSpec(memory_space=pltpu.VMEM)],
        input_output_aliases={i: i for i in range(n)},
        compiler_params=pltpu.CompilerParams(has_side_effects=_EFFECT),
    )(*[_in_hbm(w) for w in ws], after)
    bufs, sems, token = res[:n], res[n:-1], res[-1]
    return list(bufs), [(sems[2 * g], sems[2 * g + 1]) for g in range(len(groups))], token


def _gather_wait(bufs, send_sems, recv_sems, after, name):
    m = len(bufs)

    def body(*refs):
        in_refs = refs[:m]
        send, recv = refs[m], refs[m + 1]
        x, y, c = _position()
        mychip = 2 * x + y
        chips = [(1 - x, y), (x, 1 - y), (1 - x, 1 - y)]
        for k in range(m):
            h = bufs[k].shape[1] // 2
            mine = in_refs[k].at[mychip, pl.ds(c * h, h), :]
            for j, (px, py) in enumerate(chips):
                cp = pltpu.make_async_remote_copy(
                    src_ref=mine, dst_ref=in_refs[k].at[2 * px + py, pl.ds(c * h, h), :],
                    send_sem=send.at[3 * k + j], recv_sem=recv.at[3 * k + j],
                    device_id=(px, py, c), device_id_type=MESH)
                cp.wait_send()
                cp.wait_recv()

    res = pl.pallas_call(
        body, name=name, out_shape=[pltpu.HBM(b.shape, b.dtype) for b in bufs],
        in_specs=[_HBM] * m + [_SEM, _SEM, _ANY], out_specs=[_HBM] * m,
        input_output_aliases={k: k for k in range(m)},
        compiler_params=pltpu.CompilerParams(has_side_effects=_EFFECT),
    )(*bufs, send_sems, recv_sems, after)
    return list(res)


def _forward_halves(ws, name):
    n = len(ws)

    def body(*refs):
        out_refs = refs[n:2 * n]
        send_sems, recv_sems = refs[2 * n:]
        x, y, c = _position()
        me, sibling = (x, y, c), (x, y, 1 - c)
        chips = [(1 - x, y), (x, 1 - y), (1 - x, 1 - y)]
        cps = []
        for i in range(n):
            h = ws[i].shape[1] // 2
            for j, (px, py) in enumerate(chips):
                got = out_refs[i].at[2 * px + py, pl.ds(c * h, h), :]
                cp = pltpu.make_async_remote_copy(
                    src_ref=got, dst_ref=got, send_sem=send_sems.at[3 * i + j], recv_sem=recv_sems.at[3 * i + j],
                    device_id=sibling, device_id_type=MESH)
                cp.start()
                cps.append(cp)
        for i in range(n):
            h = ws[i].shape[1] // 2
            for j, (px, py) in enumerate(chips):
                other = out_refs[i].at[2 * px + py, pl.ds((1 - c) * h, h), :]
                pltpu.make_async_remote_copy(
                    src_ref=other, dst_ref=other, send_sem=send_sems.at[3 * i + j], recv_sem=recv_sems.at[3 * i + j],
                    device_id=me, device_id_type=MESH).wait_recv()
        for cp in cps:
            cp.wait_send()

    return pl.pallas_call(
        body, name=name,
        out_shape=[jax.ShapeDtypeStruct(w.shape, w.dtype) for w in ws],
        in_specs=[_HBM] * n, out_specs=[_HBM] * n, input_output_aliases={i: i for i in range(n)},
        scratch_shapes=[pltpu.SemaphoreType.DMA((3 * n,)), pltpu.SemaphoreType.DMA((3 * n,))],
    )(*ws)


def _swap_halves(gs, name):
    n = len(gs)

    def body(*refs):
        in_refs, out_refs = refs[:n], refs[n:2 * n]
        send_sems, recv_sems = refs[2 * n:]
        x, y, c = _position()
        cps = []
        for i in range(n):
            h = gs[i].shape[1] // 2
            cp = pltpu.make_async_remote_copy(
                src_ref=in_refs[i].at[:, pl.ds((1 - c) * h, h), :], dst_ref=out_refs[i],
                send_sem=send_sems.at[i], recv_sem=recv_sems.at[i], device_id=(x, y, 1 - c), device_id_type=MESH)
            cp.start()
            cps.append(cp)
        for cp in cps:
            cp.wait()

    return pl.pallas_call(
        body, name=name,
        out_shape=[jax.ShapeDtypeStruct((g.shape[0], g.shape[1] // 2, g.shape[2]), g.dtype) for g in gs],
        in_specs=[_HBM] * n, out_specs=[_HBM] * n,
        scratch_shapes=[pltpu.SemaphoreType.DMA((n,)), pltpu.SemaphoreType.DMA((n,))],
    )(*gs)


def _copies_start(arrays, copies, nsem, after, name):
    n = len(arrays)

    def body(*refs):
        for cp in copies(refs[:n], refs[2 * n + 1], refs[2 * n + 2]):
            cp.start()
        refs[2 * n + 3][...] = jnp.zeros_like(refs[2 * n + 3])

    res = pl.pallas_call(
        body, name=name,
        out_shape=[pltpu.HBM(a.shape, a.dtype) for a in arrays]
        + [pltpu.SemaphoreType.DMA((nsem,)), pltpu.SemaphoreType.DMA((nsem,)), jax.ShapeDtypeStruct((8, 128), F32)],
        in_specs=[_HBM] * n + [_ANY],
        out_specs=[_HBM] * n + [_SEM, _SEM, pl.BlockSpec(memory_space=pltpu.VMEM)],
        input_output_aliases={i: i for i in range(n)},
        compiler_params=pltpu.CompilerParams(has_side_effects=_EFFECT),
    )(*[_in_hbm(a) for a in arrays], after)
    return list(res[:n]), res[n], res[n + 1], res[n + 2]


def _copies_wait(arrays, copies, send_sems, recv_sems, after, name):
    n = len(arrays)

    def body(*refs):
        for cp in copies(refs[:n], refs[n], refs[n + 1]):
            cp.wait_send()
            cp.wait_recv()

    res = pl.pallas_call(
        body, name=name, out_shape=[pltpu.HBM(a.shape, a.dtype) for a in arrays],
        in_specs=[_HBM] * n + [_SEM, _SEM, _ANY], out_specs=[_HBM] * n,
        input_output_aliases={i: i for i in range(n)},
        compiler_params=pltpu.CompilerParams(has_side_effects=_EFFECT),
    )(*arrays, send_sems, recv_sems, after)
    return list(res)


def _scatter_copies(refs, send, recv):
    n = len(refs) // 2
    x, y, c = _position()
    chips = [(1 - x, y), (x, 1 - y), (1 - x, 1 - y)]
    return [pltpu.make_async_remote_copy(
        src_ref=refs[i].at[2 * px + py], dst_ref=refs[n + i].at[j],
        send_sem=send.at[3 * i + j], recv_sem=recv.at[3 * i + j], device_id=(px, py, c), device_id_type=MESH)
        for i in range(n) for j, (px, py) in enumerate(chips)]


def _swap_copies(refs, send, recv):
    n = len(refs) // 2
    x, y, c = _position()
    cps = []
    for i in range(n):
        h = refs[i].shape[1] // 2
        cps.append(pltpu.make_async_remote_copy(
            src_ref=refs[i].at[:, pl.ds((1 - c) * h, h), :], dst_ref=refs[n + i],
            send_sem=send.at[i], recv_sem=recv.at[i], device_id=(x, y, 1 - c), device_id_type=MESH))
    return cps


def _join_copies(refs, send, recv):
    x, y, c = _position()
    cps = []
    for i, r in enumerate(refs):
        h = r.shape[0] // 2
        mine = r.at[pl.ds(c * h, h), :]
        cps.append(pltpu.make_async_remote_copy(
            src_ref=mine, dst_ref=mine, send_sem=send.at[i], recv_sem=recv.at[i],
            device_id=(x, y, 1 - c), device_id_type=MESH))
    return cps


def _forward_copies(refs, send, recv):
    x, y, c = _position()
    chips = [(1 - x, y), (x, 1 - y), (1 - x, 1 - y)]
    cps = []
    for i, r in enumerate(refs):
        h = r.shape[1] // 2
        for j, (px, py) in enumerate(chips):
            got = r.at[2 * px + py, pl.ds(c * h, h), :]
            cps.append(pltpu.make_async_remote_copy(
                src_ref=got, dst_ref=got, send_sem=send.at[3 * i + j], recv_sem=recv.at[3 * i + j],
                device_id=(x, y, 1 - c), device_id_type=MESH))
    return cps


def _join_halves(rs, name):
    n = len(rs)

    def body(*refs):
        out_refs = refs[n:2 * n]
        send_sems, recv_sems = refs[2 * n:]
        x, y, c = _position()
        cps = []
        for i in range(n):
            h = rs[i].shape[0] // 2
            mine = out_refs[i].at[pl.ds(c * h, h), :]
            cp = pltpu.make_async_remote_copy(
                src_ref=mine, dst_ref=mine, send_sem=send_sems.at[i], recv_sem=recv_sems.at[i],
                device_id=(x, y, 1 - c), device_id_type=MESH)
            cp.start()
            cps.append(cp)
        for i in range(n):
            h = rs[i].shape[0] // 2
            other = out_refs[i].at[pl.ds((1 - c) * h, h), :]
            pltpu.make_async_remote_copy(
                src_ref=other, dst_ref=other, send_sem=send_sems.at[i], recv_sem=recv_sems.at[i],
                device_id=(x, y, c), device_id_type=MESH).wait_recv()
        for cp in cps:
            cp.wait_send()

    return pl.pallas_call(
        body, name=name,
        out_shape=[jax.ShapeDtypeStruct(r.shape, r.dtype) for r in rs],
        in_specs=[_HBM] * n, out_specs=[_HBM] * n, input_output_aliases={i: i for i in range(n)},
        scratch_shapes=[pltpu.SemaphoreType.DMA((n,)), pltpu.SemaphoreType.DMA((n,))],
    )(*rs)


def _t5_buckets_block():
    qi = np.arange(BLOCK)[:, None]
    ki = np.arange(2 * BLOCK)[None, :]
    n = np.maximum(qi + BLOCK - ki, 0)
    max_exact = NUM_BUCKETS // 2
    large = max_exact + (np.log(np.maximum(n, 1) / max_exact) / np.log(MAX_DISTANCE / max_exact)
                         * (NUM_BUCKETS - max_exact)).astype(np.int32)
    large = np.minimum(large, NUM_BUCKETS - 1)
    return np.where(n < max_exact, n, large).astype(np.int32)


def _discretise(lambda_re, lambda_im, log_step, b_re, b_im):
    lam_re = jnp.minimum(lambda_re, -1e-4)
    lam_im = lambda_im
    delta = jnp.exp(log_step)[:, None]
    mag = jnp.exp(lam_re * delta)
    ang = lam_im * delta
    abar_re, abar_im = mag * jnp.cos(ang), mag * jnp.sin(ang)
    num_re, num_im = abar_re - 1.0, abar_im
    den = lam_re * lam_re + lam_im * lam_im
    f_re = (num_re * lam_re + num_im * lam_im) / den
    f_im = (num_im * lam_re - num_re * lam_im) / den
    bbar_re = f_re[..., None] * b_re - f_im[..., None] * b_im
    bbar_im = f_re[..., None] * b_im + f_im[..., None] * b_re
    return abar_re, abar_im, bbar_re, bbar_im


def _interleave(v, nc):
    s, w = v.shape
    return v.reshape(nc, s // nc, w).transpose(1, 0, 2).reshape(s, w)


def _deinterleave(v, nc):
    s, w = v.shape
    return v.reshape(s // nc, nc, w).transpose(1, 0, 2).reshape(s, w)


_SMALL = ("norm1_g", "b_in", "attn_sinks", "rel_bias", "lambda_re", "lambda_im", "log_step", "ssm_b_re",
          "ssm_b_im", "ssm_c_re", "ssm_c_im", "ssm_d", "b_glu", "norm2_g", "final_g")


def _pack(parts):
    rows = []
    for p in parts:
        f = p.reshape(-1).astype(F32)
        pad = (-f.shape[0]) % 128
        rows.append(jnp.pad(f, (0, pad)).reshape(-1, 128))
    out = jnp.concatenate(rows, axis=0)
    pad = (-out.shape[0]) % 256
    return jnp.pad(out, ((0, pad), (0, 0)))


def _unpack(packed, shapes):
    res, r = [], 0
    for shp in shapes:
        size = int(np.prod(shp))
        nr = -(-size // 128)
        res.append(packed[r:r + nr].reshape(-1)[:size].reshape(shp))
        r += nr
    return res


def kernel(x, c, w_ada, b_ada, norm1_g, w_in, b_in, attn_sinks, rel_bias, lambda_re, lambda_im, log_step, ssm_b_re, ssm_b_im, ssm_c_re, ssm_c_im, ssm_d, w_glu, b_glu, w_attn_proj, w_ssm_proj, w_out, norm2_g, w_ff1, w_ff2, final_g, loss_target, m_w_ada, m_b_ada, m_norm1_g, m_w_in, m_b_in, m_attn_sinks, m_rel_bias, m_lambda_re, m_lambda_im, m_log_step, m_ssm_b_re, m_ssm_b_im, m_ssm_c_re, m_ssm_c_im, m_ssm_d, m_w_glu, m_b_glu, m_w_attn_proj, m_w_ssm_proj, m_w_out, m_norm2_g, m_w_ff1, m_w_ff2, m_final_g, v_w_ada, v_b_ada, v_norm1_g, v_w_in, v_b_in, v_attn_sinks, v_rel_bias, v_lambda_re, v_lambda_im, v_log_step, v_ssm_b_re, v_ssm_b_im, v_ssm_c_re, v_ssm_c_im, v_ssm_d, v_w_glu, v_b_glu, v_w_attn_proj, v_w_ssm_proj, v_w_out, v_norm2_g, v_w_ff1, v_w_ff2, v_final_g):
    given = dict(locals())
    S, D = x.shape[1], x.shape[2]
    SSM_W = w_glu.shape[2]
    G = SSM_W // SSM_GROUP_CH
    NST = G * SSM_STATE
    DFF = w_ff2.shape[1] * N_CHIPS
    INW = w_in.shape[2] * N_CHIPS
    o_q, o_k, o_v, o_u = 0, ATTN_WIDTH, ATTN_WIDTH + KV_WIDTH, ATTN_WIDTH + 2 * KV_WIDTH
    o_ga, o_gs = o_u + SSM_W, o_u + SSM_W + D
    mx, my, mc = _position()
    my_chip = 2 * mx + my
    my_b = 4 * mx + 2 * my + mc

    xv, tgt = x[0], loss_target[0]

    big = dict(w_in=w_in[0], w_glu=w_glu[0], w_attn_proj=w_attn_proj[0], w_ssm_proj=w_ssm_proj[0],
               w_out=w_out[0], w_ff1=w_ff1[0], w_ff2=w_ff2[0])
    big_names = list(big)
    colsharded = {"w_in", "w_attn_proj", "w_ssm_proj", "w_ff1"}
    chip_sel = my_chip.astype(jnp.int32).reshape(1)
    gather_groups = [["w_in"], ["w_attn_proj", "w_ssm_proj", "w_glu", "w_out"], ["w_ff1", "w_ff2"]]
    in_flight, gather_sems, gathered = {}, [], {}

    def finish_gather(g, after):
        bufs = [in_flight[k] for k in gather_groups[g]]
        bufs = _gather_wait(bufs, gather_sems[g][0], gather_sems[g][1], after, "gather_wait_%d" % g)
        gathered.update(zip(gather_groups[g], _forward_halves(bufs, "gather_forward_%d" % g)))

    def tied(v, token):
        return v + token[0:1, 0:1]

    def all_of(*arrays):
        return jnp.stack([a.reshape(-1)[0].astype(F32) for a in arrays])

    def wop(k):
        g = gathered[k]
        return _Op(g, N_CHIPS) if k in colsharded else _Op(g.reshape(g.shape[0] * g.shape[1], g.shape[2]))

    grads = {}
    nothing = jnp.zeros((8, 128), F32)
    half = mc.astype(jnp.int32).reshape(1)
    sel = jnp.stack([my_chip, mc]).astype(jnp.int32)

    def rs_swap(tag, named):
        keys, gl = list(named), []
        for k in keys:
            gk = named[k]
            if k not in colsharded:
                gk = gk.reshape(N_CHIPS, gk.shape[0] // N_CHIPS, gk.shape[1])
            gl.append(gk)
        lands = [lax.empty((g.shape[0], g.shape[1] // 2, g.shape[2]), g.dtype) for g in gl]
        arrays, ssem, rsem, token = _copies_start(gl + lands, _swap_copies, len(gl), nothing, "rs_swap_start_" + tag)
        return (keys, arrays, ssem, rsem), token

    def rs_scatter(tag, state, after):
        keys, arrays, ssem, rsem = state
        arrays = _copies_wait(arrays, _swap_copies, ssem, rsem, after, "rs_swap_wait_" + tag)
        n = len(keys)
        ps = [_add_half(g, t, half, "rs_add_" + k) for g, t, k in zip(arrays[:n], arrays[n:], keys)]
        lands = [lax.empty((3,) + p.shape[1:], p.dtype) for p in ps]
        arrays, ssem, rsem, token = _copies_start(ps + lands, _scatter_copies, 3 * n, nothing, "rs_start_" + tag)
        return (keys, arrays, ssem, rsem), token

    def rs_sum(tag, state, after):
        keys, arrays, ssem, rsem = state
        arrays = _copies_wait(arrays, _scatter_copies, ssem, rsem, after, "rs_wait_" + tag)
        n = len(keys)
        rs = [_sum_own(p, t, sel, "rs_sum_" + k) for p, t, k in zip(arrays[:n], arrays[n:], keys)]
        rs, ssem, rsem, token = _copies_start(rs, _join_copies, n, nothing, "rs_join_start_" + tag)
        return (keys, rs, ssem, rsem), token

    def rs_finish(tag, state, after):
        keys, rs, ssem, rsem = state
        for k, f in zip(keys, _copies_wait(rs, _join_copies, ssem, rsem, after, "rs_join_wait_" + tag)):
            grads[k] = f[None]

    c_all = _allgather8(jnp.pad(c, ((0, 7), (0, 0))), "gather_c").reshape(N_DEV, 8, D)[:, 0]
    c16 = jnp.pad(c_all, ((0, 8), (0, 0)))
    b_ada_mine = lax.dynamic_slice(b_ada.reshape(N_CHIPS, -1), (my_chip, 0), (1, w_ada.shape[2]))
    mod_sh = _mm(c16, w_ada[0], "NN", name="mod", M=16, N=w_ada.shape[2], K=D, a_fn=_silu,
                 epilogue=lambda acc, b: (acc + b,), extras=[(b_ada_mine, "row")])
    mod_all = _allgather8(mod_sh[:8], "gather_mod").reshape(N_DEV, 8, -1)
    mod_row = jnp.concatenate(
        [lax.dynamic_slice(mod_all, (2 * j, my_b, 0), (1, 1, mod_all.shape[2]))[0] for j in range(N_CHIPS)], axis=1)
    sh1, sc1, g1, sh2, sc2, g2 = [mod_row[:, i * D:(i + 1) * D] for i in range(6)]

    first = [_cast_into_slot(big["w_in"], chip_sel, "cast_w_in")]
    first, sems_first, token_first = _gather_start(first, [[0]], mod_all, "gather_start_in")
    rest_names = gather_groups[1] + gather_groups[2]
    rest = [_cast_into_slot(big[k], chip_sel, "cast_" + k) for k in rest_names]
    rest, sems_rest, token_rest = _gather_start(
        rest, [[rest_names.index(k) for k in grp] for grp in gather_groups[1:]], token_first, "gather_start_rest")
    in_flight.update(zip(["w_in"] + rest_names, first + rest))
    gather_sems.extend(sems_first + sems_rest)

    disc_in = (lambda_re[0], lambda_im[0], log_step[0], ssm_b_re[0], ssm_b_im[0])
    (abar_re, abar_im, bbar_re, bbar_im), disc_vjp = jax.vjp(_discretise, *disc_in)
    same_group = jnp.asarray(np.arange(SSM_W)[:, None] // SSM_GROUP_CH == np.arange(NST)[None, :] // SSM_STATE)

    def block_diag(t):
        return jnp.where(same_group, jnp.tile(t, (G, 1)), 0.0)

    bd = jnp.concatenate([block_diag(bb.transpose(2, 0, 1).reshape(SSM_GROUP_CH, NST)) for bb in (bbar_re, bbar_im)],
                         axis=1)
    cd = jnp.concatenate([block_diag(cc.transpose(1, 0, 2).reshape(SSM_GROUP_CH, NST)).T
                          for cc in (ssm_c_re[0], -ssm_c_im[0])], axis=0)
    a_fwd = jnp.stack([abar_re.reshape(1, NST), abar_im.reshape(1, NST)])
    a_bwd = jnp.stack([abar_re.reshape(1, NST), -abar_im.reshape(1, NST)])
    d_row = ssm_d

    buckets = _t5_buckets_block()
    onehot_t = (jnp.arange(128, dtype=jnp.int32)[:, None] == jnp.asarray(buckets.reshape(1, -1))).astype(BF16)
    rb_hi = rel_bias.astype(BF16)
    rb_lo = (rel_bias - rb_hi.astype(F32)).astype(BF16)
    rb_lo2 = (rel_bias - rb_hi.astype(F32) - rb_lo.astype(F32)).astype(BF16)
    rb3 = jnp.pad(jnp.concatenate([rb_hi.T, rb_lo.T, rb_lo2.T], axis=0), ((0, 0), (0, 128 - NUM_BUCKETS)))
    b3 = _mm(rb3, onehot_t, "NN", name="rel_bias_rows", M=3 * N_Q_HEADS, N=BLOCK * 2 * BLOCK, K=128, tj=4096)
    bias = (b3[:N_Q_HEADS] + b3[N_Q_HEADS:2 * N_Q_HEADS]) + b3[2 * N_Q_HEADS:]
    bias = bias.reshape(N_Q_HEADS, BLOCK, 2 * BLOCK)
    sinks_b = jnp.broadcast_to(attn_sinks[0][:, None, None], (N_Q_HEADS, BLOCK, 128)).reshape(N_Q_HEADS * BLOCK, 128)

    def two(fn):
        def both(*blocks):
            r = fn(*blocks)
            return r, r
        return both

    h1, h1_t = _rowwise(two(_norm_mod), [(xv, "tile", D), (tied(tied(norm1_g, token_first), token_rest), "row", D),
                                         (sh1, "row", D), (sc1, "row", D)],
                        [(D, BF16), (D, BF16, "T")], [], name="norm1", rows=S)
    finish_gather(0, all_of(h1, bd, cd, a_fwd, a_bwd, bias, sinks_b))
    proj = _mm(h1, wop("w_in"), "NN", name="proj", M=S, N=INW, K=D,
               epilogue=lambda acc, b: (acc + b,), extras=[(b_in, "row")])

    def heads(v2d, nh):
        return v2d.reshape(S, nh, HEAD_DIM).transpose(1, 0, 2)

    def unheads(v3d):
        return v3d.transpose(1, 0, 2).reshape(S, -1)

    qh = heads(proj[:, o_q:o_k], N_Q_HEADS)
    kh = heads(proj[:, o_k:o_v], N_KV_HEADS)
    vh = heads(proj[:, o_v:o_u], N_KV_HEADS)
    attn = unheads(_attn_fwd(qh, kh, vh, sinks_b, bias, "attn_fwd"))
    finish_gather(1, attn)
    y_attn = _mm(attn, wop("w_attn_proj"), "NN", name="attn_proj", M=S, N=D, K=ATTN_WIDTH, out_dtypes=(BF16,))

    u = proj[:, o_u:o_ga]
    u_il = _interleave(u, SCAN_CHUNKS)
    SB = 128
    nsb, gpb = SSM_W // SB, SB // SSM_GROUP_CH
    SBN = gpb * SSM_STATE
    y_il, xs = _ssm_fwd(u_il, bd, cd, a_fwd, d_row, name="ssm_fwd", sb=SB, sbn=SBN)
    y = _deinterleave(y_il, SCAN_CHUNKS)
    z0b = _rowwise(_gelu, [(y, "tile", SSM_W)], [(SSM_W, BF16)], [], name="gelu", rows=S)[0]
    z, t_glu = _mm(z0b, wop("w_glu"), "NN", name="glu", M=S, N=SSM_W, K=SSM_W, out_dtypes=(BF16, F32),
                   epilogue=lambda acc, b, yy: (_gelu(yy) * _sigmoid(acc + b), acc + b),
                   extras=[(b_glu, "row"), (y, "tile")])
    y_ssm = _mm(z, wop("w_ssm_proj"), "NN", name="ssm_proj", M=S, N=D, K=SSM_W, out_dtypes=(BF16,))

    ff_bufs = _gather_wait([in_flight[k] for k in gather_groups[2]], gather_sems[2][0], gather_sems[2][1], all_of(y_ssm),
                           "gather_wait_2")
    ff_bufs, ff_send, ff_recv, token = _copies_start(ff_bufs, _forward_copies, 3 * len(ff_bufs), nothing,
                                                    "gather_forward_2_start")
    merged, merged_t = _rowwise(two(_merge), [(_Op(proj, coff=o_ga), "tile", D), (_Op(proj, coff=o_gs), "tile", D),
                                              (y_attn, "tile", D), (y_ssm, "tile", D)],
                                [(D, BF16), (D, BF16, "T")], [], name="merge", rows=S, deps=[token])
    mo, x2 = _mm(merged, wop("w_out"), "NN", name="out_proj", M=S, N=D, K=D, out_dtypes=(BF16, F32),
                 epilogue=lambda acc, xx, gg: (acc, xx + gg * acc), extras=[(xv, "tile"), (g1, "row")])
    h2, h2_t = _rowwise(two(_norm_mod), [(x2, "tile", D), (norm2_g, "row", D), (sh2, "row", D), (sc2, "row", D)],
                        [(D, BF16), (D, BF16, "T")], [], name="norm2", rows=S)
    gathered.update(zip(gather_groups[2], _copies_wait(ff_bufs, _forward_copies, ff_send, ff_recv, h2,
                                                       "gather_forward_2_wait")))
    a_b, r_b = _mm(h2, wop("w_ff1"), "NN", name="ff1", M=S, N=DFF, K=D, out_dtypes=(BF16, BF16),
                   epilogue=lambda acc: (acc, jnp.square(jnp.maximum(acc, 0.0))))
    ff, x3 = _mm(r_b, wop("w_ff2"), "NN", name="ff2", M=S, N=D, K=DFF, out_dtypes=(BF16, F32),
                 epilogue=lambda acc, xx, gg: (acc, xx + gg * acc), extras=[(x2, "tile"), (g2, "row")],
                 tj=1024, tk=1024)

    def final_fn(x3b, gf, tb):
        def f(xx, gg):
            yv = xx * lax.rsqrt(jnp.mean(xx * xx, axis=-1, keepdims=True) + EPS) * gg
            err = jnp.square(yv - tb)
            return 0.5 * jnp.sum(jnp.mean(err, axis=-1, keepdims=True), axis=0, keepdims=True)
        lv, vjp = jax.vjp(f, x3b, gf)
        dx, dg = vjp(jnp.ones((1, 1), F32))
        return dx, dg, jnp.broadcast_to(lv, (1, 128))

    dx3, g_final, loss_acc = _rowwise(final_fn, [(x3, "tile", D), (final_g.reshape(1, D), "row", D), (tgt, "tile", D)],
                                      [(D, F32)], [D, 128], name="final", rows=S)

    def ff_out_bwd(dx3b, ffb, g2b):
        return dx3b * g2b, jnp.sum(dx3b * ffb, axis=0, keepdims=True)

    dff, d_g2 = _rowwise(ff_out_bwd, [(dx3, "tile", D), (ff, "tile", D), (g2, "row", D)], [(D, BF16)], [D],
                         name="ff_out_bwd", rows=S)
    da = _mm(dff, wop("w_ff2"), "NT", name="ff2_dx", M=S, N=DFF, K=D, out_dtypes=(BF16,),
             epilogue=lambda acc, ab: (acc * (2.0 * jnp.maximum(ab.astype(F32), 0.0)),), extras=[(a_b, "tile")])
    g_w_ff2 = _mm(r_b, dff, "TN", name="ff2_dw", M=DFF, N=D, K=S, out_dtypes=(BF16,), tj=1024, tk=1024)
    g_w_ff1 = _mm(h2_t, da, "NN", name="ff1_dw", M=D, N=DFF, K=S, out_dtypes=(BF16,), out_nsh=N_CHIPS, tj=1024, tk=1024)
    rs_ff, token = rs_swap("ff", dict(w_ff2=g_w_ff2, w_ff1=g_w_ff1))
    dh2 = _mm(da, wop("w_ff1"), "NT", name="ff1_dx", M=S, N=D, K=DFF, tj=1024, tk=1024, deps=[token])
    rs_ff, token_ff = rs_scatter("ff", rs_ff, dh2)

    def norm2_bwd(x2b, dh2b, dx3b, mob, gn, shb, scb, g1b):
        _, vjp = jax.vjp(_norm_mod, x2b, gn, shb, scb)
        dx, dg, dsh, dsc = vjp(dh2b)
        dx2b = dx + dx3b
        return dx2b, dx2b * g1b, dg, dsh, dsc, jnp.sum(dx2b * mob, axis=0, keepdims=True)

    dx2, dmo, g_norm2, d_sh2, d_sc2, d_g1 = _rowwise(
        norm2_bwd, [(x2, "tile", D), (dh2, "tile", D), (dx3, "tile", D), (mo, "tile", D),
                    (tied(norm2_g, token_ff), "row", D), (sh2, "row", D), (sc2, "row", D), (g1, "row", D)],
        [(D, F32), (D, BF16)], [D, D, D, D], name="norm2_bwd", rows=S, tr=128)
    dmerged = _mm(dmo, wop("w_out"), "NT", name="out_dx", M=S, N=D, K=D)
    g_w_out = _mm(merged_t, dmo, "NN", name="out_dw", M=D, N=D, K=S, out_dtypes=(BF16,), tk=1024)

    def merge_bwd(gab, gsb, yab, ysb, dmb):
        _, vjp = jax.vjp(_merge, gab, gsb, yab, ysb)
        return vjp(dmb)

    d_ga, d_gs, dy_attn, dy_ssm = _rowwise(
        merge_bwd, [(_Op(proj, coff=o_ga), "tile", D), (_Op(proj, coff=o_gs), "tile", D), (y_attn, "tile", D),
                    (y_ssm, "tile", D), (dmerged, "tile", D)],
        [(D, BF16), (D, BF16), (D, BF16), (D, BF16)], [], name="merge_bwd", rows=S, tr=128)

    dattn = _mm(dy_attn, wop("w_attn_proj"), "NT", name="attn_proj_dx", M=S, N=ATTN_WIDTH, K=D)
    g_w_attn_proj = _mm(attn, dy_attn, "TN", name="attn_proj_dw", M=ATTN_WIDTH, N=D, K=S, out_dtypes=(BF16,),
                        out_nsh=N_CHIPS, tk=1024)

    dz = _mm(dy_ssm, wop("w_ssm_proj"), "NT", name="ssm_proj_dx", M=S, N=SSM_W, K=D)
    g_w_ssm_proj = _mm(z, dy_ssm, "TN", name="ssm_proj_dw", M=SSM_W, N=D, K=S, out_dtypes=(BF16,),
                       out_nsh=N_CHIPS, tk=1024)

    def glu_bwd(dzb, yb, tb):
        z0 = _gelu(yb)
        sg = _sigmoid(tb)
        dt = dzb * z0 * sg * (1.0 - sg)
        return dt, dzb * sg, jnp.sum(dt, axis=0, keepdims=True)

    dt_b, dz0a, g_b_glu = _rowwise(glu_bwd, [(dz, "tile", SSM_W), (y, "tile", SSM_W), (t_glu, "tile", SSM_W)],
                                   [(SSM_W, BF16), (SSM_W, F32)], [SSM_W], name="glu_bwd", rows=S)

    def gelu_bwd(acc, dz0ab, yb):
        _, vjp = jax.vjp(_gelu, yb)
        return (vjp(acc + dz0ab)[0],)

    dy = _mm(dt_b, wop("w_glu"), "NT", name="glu_dx", M=S, N=SSM_W, K=SSM_W, epilogue=gelu_bwd,
             extras=[(dz0a, "tile"), (y, "tile")])
    g_w_glu = _mm(z0b, dt_b, "TN", name="glu_dw", M=SSM_W, N=SSM_W, K=S, out_dtypes=(BF16,), tk=1024)
    rs_mix, token = rs_swap("mix", dict(w_out=g_w_out, w_attn_proj=g_w_attn_proj, w_ssm_proj=g_w_ssm_proj,
                                        w_glu=g_w_glu))
    dy_il = _interleave(tied(dy, token), SCAN_CHUNKS)
    du_il, g_bd, g_cd, d_abar, g_ssm_d = _ssm_bwd(dy_il, u_il, xs, bd, cd, a_bwd, d_row, name="ssm_bwd", sb=SB, sbn=SBN)
    du = _deinterleave(du_il, SCAN_CHUNKS)
    rs_mix, token_mix = rs_scatter("mix", rs_mix, du_il)

    dqh, dkh, dvh, dsink_blk, dbias = _attn_bwd(qh, kh, vh, heads(dattn, N_Q_HEADS), tied(sinks_b, token_mix), bias,
                                                "attn_bwd")
    g_sinks = _sum_lead(dsink_blk.reshape(N_Q_HEADS, BLOCK, 128).transpose(1, 0, 2), "sinks_dw")[:, 0].reshape(1, N_Q_HEADS)
    g_rel = _mm(dbias.reshape(N_Q_HEADS, -1), onehot_t, "NT", name="rel_bias_dw", M=N_Q_HEADS, N=128,
                K=BLOCK * 2 * BLOCK, tk=4096)
    g_rel_bias = g_rel[:, :NUM_BUCKETS].T

    eye_b = jnp.eye(gpb, dtype=F32)
    g_cd6 = g_cd.reshape(2, nsb, gpb, SSM_STATE, gpb, SSM_GROUP_CH)
    g_c_re = jnp.einsum("bgnhp,gh->bgpn", g_cd6[0], eye_b).reshape(G, SSM_GROUP_CH, SSM_STATE)
    g_c_im = -jnp.einsum("bgnhp,gh->bgpn", g_cd6[1], eye_b).reshape(G, SSM_GROUP_CH, SSM_STATE)
    g_bd6 = g_bd.reshape(nsb, gpb, SSM_GROUP_CH, 2, gpb, SSM_STATE)
    g_bbar = jnp.einsum("bhprgn,hg->rbhnp", g_bd6, eye_b).reshape(2, G, SSM_STATE, SSM_GROUP_CH)
    g_bbar_re, g_bbar_im = g_bbar[0], g_bbar[1]
    g_lre, g_lim, g_lstep, g_bre, g_bim = disc_vjp(
        (d_abar[0].reshape(G, SSM_STATE), d_abar[1].reshape(G, SSM_STATE), g_bbar_re, g_bbar_im))

    dproj = jnp.concatenate([unheads(dqh).astype(BF16), unheads(dkh).astype(BF16), unheads(dvh).astype(BF16),
                             du.astype(BF16), d_ga, d_gs], axis=1)
    g_w_in = _mm(h1_t, dproj, "NN", name="proj_dw", M=D, N=INW, K=S, out_dtypes=(BF16,), out_nsh=N_CHIPS,
                 tj=INW // (2 * N_CHIPS), tk=1024)
    rs_in, token = rs_swap("in", dict(w_in=g_w_in))
    dh1 = _mm(dproj, wop("w_in"), "NT", name="proj_dx", M=S, N=D, K=INW, tj=1024, tk=INW // N_CHIPS, deps=[token])
    rs_in, token_in = rs_scatter("in", rs_in, dh1)
    g_b_in = _rowwise(lambda d: (jnp.sum(d.astype(F32), axis=0, keepdims=True),), [(dproj, "tile", INW)], [], [INW],
                      name="proj_db", rows=S)[0]

    def norm1_bwd(xb, dhb, dresb, gn, shb, scb):
        _, vjp = jax.vjp(_norm_mod, xb, gn, shb, scb)
        dx, dg, dsh, dsc = vjp(dhb)
        return dx + dresb, dg, dsh, dsc

    grad_x, g_norm1, d_sh1, d_sc1 = _rowwise(
        norm1_bwd, [(xv, "tile", D), (dh1, "tile", D), (dx2, "tile", D), (tied(norm1_g, token_in), "row", D),
                    (sh1, "row", D),
                    (sc1, "row", D)], [(D, F32)], [D, D, D], name="norm1_bwd", rows=S)

    dmod_row = jnp.concatenate([d_sh1, d_sc1, d_g1, d_sh2, d_sc2, d_g2], axis=1)
    dmod_all = _allgather8(jnp.pad(dmod_row, ((0, 7), (0, 0))), "gather_dmod").reshape(N_DEV, 8, -1)[:, 0]
    g_b_ada = _sum_lead(dmod_all.reshape(N_DEV, -1, 128), "b_ada_dw").reshape(1, -1)
    dmod_mine = lax.dynamic_slice(dmod_all.reshape(N_DEV, N_CHIPS, -1), (0, my_chip, 0), (N_DEV, 1, w_ada.shape[2]))[:, 0]
    g_w_ada = _mm(c16, jnp.pad(dmod_mine, ((0, 8), (0, 0))), "TN", name="ada_dw", M=D, N=w_ada.shape[2], K=16,
                  a_fn=_silu)

    small_g = dict(norm1_g=g_norm1, b_in=g_b_in, attn_sinks=g_sinks, rel_bias=g_rel_bias, lambda_re=g_lre[None],
                   lambda_im=g_lim[None], log_step=g_lstep[None], ssm_b_re=g_bre[None], ssm_b_im=g_bim[None],
                   ssm_c_re=g_c_re[None], ssm_c_im=g_c_im[None], ssm_d=g_ssm_d, b_glu=g_b_glu, norm2_g=g_norm2,
                   final_g=g_final.reshape(D))
    packed = _pack([loss_acc[:, :1]] + [small_g[k] for k in _SMALL])
    rows = packed.shape[0]
    summed = _sum_lead(_allgather8(packed, "gather_small").reshape(N_DEV, rows, 128), "small_sum")
    small_shapes = [(1,)] + [given[k].shape for k in _SMALL]
    parts = _unpack(summed, small_shapes)
    loss = parts[0].reshape(())
    grads.update(zip(_SMALL, parts[1:]))
    grads["b_ada"] = g_b_ada
    grads["w_ada"] = g_w_ada[None]

    deltas, new_m, new_v = {}, {}, {}

    def adamw_big(k, deps=()):
        d_, m_, v_ = _adamw(given[k][0], grads[k][0], given["m_" + k][0], given["v_" + k][0], "adamw_" + k, deps)
        deltas[k], new_m[k], new_v[k] = d_[None], m_[None], v_[None]
        return v_

    rs_ff, token = rs_sum("ff", rs_ff, summed)
    mark = adamw_big("w_ada", [token])
    rs_mix, token = rs_sum("mix", rs_mix, mark)
    small_all = list(_SMALL) + ["b_ada"]
    shapes = [given[k].shape for k in small_all]
    pw, pg = _pack([given[k] for k in small_all]), _pack([grads[k] for k in small_all])
    pm, pv = _pack([given["m_" + k] for k in small_all]), _pack([given["v_" + k] for k in small_all])
    d_, m_, v_ = _adamw(pw, pg, pm, pv, "adamw_small", [token])
    for k, dd, mm, vv in zip(small_all, _unpack(d_, shapes), _unpack(m_, shapes), _unpack(v_, shapes)):
        deltas[k], new_m[k], new_v[k] = dd, mm, vv
        grads[k] = grads[k].reshape(given[k].shape)
    rs_finish("ff", rs_ff, v_)
    marks = [adamw_big(k) for k in ("w_ff2", "w_ff1")]
    rs_finish("mix", rs_mix, all_of(*marks))
    marks = [adamw_big(k) for k in ("w_out", "w_attn_proj", "w_ssm_proj", "w_glu")]
    rs_in, token = rs_sum("in", rs_in, all_of(*marks))
    rs_finish("in", rs_in, token)
    adamw_big("w_in")

    names = ["w_ada", "b_ada", "norm1_g", "w_in", "b_in", "attn_sinks", "rel_bias", "lambda_re", "lambda_im",
             "log_step", "ssm_b_re", "ssm_b_im", "ssm_c_re", "ssm_c_im", "ssm_d", "w_glu", "b_glu", "w_attn_proj",
             "w_ssm_proj", "w_out", "norm2_g", "w_ff1", "w_ff2", "final_g"]
    return (loss, grad_x[None], *[grads[n] for n in names], *[deltas[n] for n in names],
            *[new_m[n] for n in names], *[new_v[n] for n in names])
```

```python
import math

import numpy as np
import jax
import jax.numpy as jnp
from jax import lax
from jax.experimental import pallas as pl
from jax.experimental.pallas import tpu as pltpu

F32 = jnp.float32
BF16 = jnp.bfloat16
MESH = pl.DeviceIdType.MESH

HEAD_DIM = 64
N_Q_HEADS = 16
N_KV_HEADS = 4
GQA_GROUP = N_Q_HEADS // N_KV_HEADS
ATTN_WIDTH = N_Q_HEADS * HEAD_DIM
KV_WIDTH = N_KV_HEADS * HEAD_DIM
BLOCK = 128
NUM_BUCKETS = 32
MAX_DISTANCE = 128
NEG_INF = -1e30
SSM_GROUP_CH = 16
SSM_STATE = 64
EPS = 1e-6
ADAM_LR = 0.001
ADAM_B1 = 0.9
ADAM_B2 = 0.999
ADAM_EPS = 1e-08
ADAM_WD = 0.01
ADAM_STEP = 10

N_CHIPS = 4
N_DEV = 8
SCAN_CHUNKS = 8
VMEM_LIMIT_BYTES = 48 * 1024 * 1024
SSM_VMEM_LIMIT_BYTES = 56 * 1024 * 1024


def _cparams(sem=None):
    return pltpu.CompilerParams(dimension_semantics=sem, vmem_limit_bytes=VMEM_LIMIT_BYTES)


class _Op:
    def __init__(self, arr, nsh=None, coff=0):
        self.arr, self.nsh, self.coff = arr, nsh, coff
        if nsh is None:
            self.rows, self.cols = arr.shape
        else:
            assert arr.shape[0] == nsh
            self.rows, self.cols = arr.shape[1], arr.shape[2] * nsh

    def spec(self, br, bc, idx):
        assert self.coff % bc == 0
        off = self.coff // bc
        if self.nsh is None:
            return pl.BlockSpec((br, bc), lambda *g: (idx(*g)[0], idx(*g)[1] + off))
        per = (self.cols // self.nsh) // bc
        assert per * bc * self.nsh == self.cols

        def imap(*g):
            r, c = idx(*g)
            c = c + off
            return (c // per, r, c % per)
        return pl.BlockSpec((None, br, bc), imap)


def _as_op(a):
    return a if isinstance(a, _Op) else _Op(a)


def _mm(a, b, mode, *, name, M, N, K, out_dtypes=(F32,), out_nsh=None, epilogue=None, extras=(),
        a_fn=None, ti=1024, tj=512, tk=2048, deps=()):
    a_idx = b_idx = None
    nd = len(deps)
    a, b = _as_op(a), _as_op(b)
    ti, tj, tk = min(ti, M), min(tj, N), min(tk, K)
    a_w = a.cols // a.nsh if a.nsh else None
    b_w = b.cols // b.nsh if b.nsh else None
    if a_w:
        ti, tk = (min(ti, a_w), tk) if mode == "TN" else (ti, min(tk, a_w))
    if b_w:
        tj, tk = (tj, min(tk, b_w)) if mode == "NT" else (min(tj, b_w), tk)
    if out_nsh:
        tj = min(tj, N // out_nsh)
    assert M % ti == 0 and N % tj == 0 and K % tk == 0, (name, M, N, K, ti, tj, tk)
    nk = K // tk
    if mode == "NN":
        a_spec = a.spec(ti, tk, a_idx or (lambda i, j, k: (i, k)))
        b_spec = b.spec(tk, tj, b_idx or (lambda i, j, k: (k, j)))
        dims = (((1,), (0,)), ((), ()))
    elif mode == "NT":
        a_spec = a.spec(ti, tk, a_idx or (lambda i, j, k: (i, k)))
        b_spec = b.spec(tj, tk, b_idx or (lambda i, j, k: (j, k)))
        dims = (((1,), (1,)), ((), ()))
    else:
        a_spec = a.spec(tk, ti, a_idx or (lambda i, j, k: (k, i)))
        b_spec = b.spec(tk, tj, b_idx or (lambda i, j, k: (k, j)))
        dims = (((0,), (0,)), ((), ()))
    ex_specs, ex_arrs = [], []
    for op, kind in extras:
        op = _as_op(op)
        if kind == "tile":
            ex_specs.append(op.spec(ti, tj, lambda i, j, k: (i, j)))
        else:
            ex_specs.append(op.spec(1, tj, lambda i, j, k: (0, j)))
        ex_arrs.append(op.arr)
    ne, no = len(ex_arrs), len(out_dtypes)
    if out_nsh is None:
        out_shapes = [jax.ShapeDtypeStruct((M, N), d) for d in out_dtypes]
        out_specs = [pl.BlockSpec((ti, tj), lambda i, j, k: (i, j)) for _ in out_dtypes]
    else:
        per = (N // out_nsh) // tj
        assert per * tj * out_nsh == N
        out_shapes = [jax.ShapeDtypeStruct((out_nsh, M, N // out_nsh), d) for d in out_dtypes]
        out_specs = [pl.BlockSpec((None, ti, tj), lambda i, j, k: (j // per, i, j % per)) for _ in out_dtypes]

    def body(a_ref, b_ref, *rest):
        ex_refs, out_refs, acc = rest[:ne], rest[ne + nd:ne + nd + no], rest[ne + nd + no]
        k = pl.program_id(2)

        @pl.when(k == 0)
        def _():
            acc[...] = jnp.zeros_like(acc)

        av = a_ref[...]
        if a_fn is not None:
            av = a_fn(av)
        acc[...] += lax.dot_general(av.astype(BF16), b_ref[...].astype(BF16), dims,
                                    preferred_element_type=F32)

        @pl.when(k == nk - 1)
        def _():
            res = acc[...]
            outs = epilogue(res, *[r[...] for r in ex_refs]) if epilogue is not None else (res,)
            for o_ref, o in zip(out_refs, outs):
                o_ref[...] = o.astype(o_ref.dtype)

    outs = pl.pallas_call(
        body, name=name, grid=(M // ti, N // tj, nk),
        in_specs=[a_spec, b_spec] + ex_specs + [pl.BlockSpec(memory_space=pl.ANY)] * nd,
        out_specs=out_specs, out_shape=out_shapes,
        scratch_shapes=[pltpu.VMEM((ti, tj), F32)],
        compiler_params=_cparams(("parallel", "parallel", "arbitrary")),
    )(a.arr, b.arr, *ex_arrs, *deps)
    return outs[0] if no == 1 else outs


def _rowwise(fn, ins, outs, accs, *, name, rows, tr=256, deps=()):
    tr = min(tr, rows)
    assert rows % tr == 0
    in_specs, arrs = [], []
    for op, kind, width in ins:
        op = _as_op(op)
        if kind == "tile":
            in_specs.append(op.spec(tr, width, lambda i: (i, 0)))
        else:
            in_specs.append(op.spec(op.rows, width, lambda i: (0, 0)))
        arrs.append(op.arr)
    ni, no, na = len(ins), len(outs), len(accs)
    flipped = [len(o) == 3 for o in outs]
    out_shapes = [jax.ShapeDtypeStruct((o[0], rows) if t else (rows, o[0]), o[1]) for o, t in zip(outs, flipped)]
    out_specs = [pl.BlockSpec((o[0], tr), lambda i: (0, i)) if t else pl.BlockSpec((tr, o[0]), lambda i: (i, 0))
                 for o, t in zip(outs, flipped)]
    out_shapes += [jax.ShapeDtypeStruct((1, w), F32) for w in accs]
    out_specs += [pl.BlockSpec((1, w), lambda i: (0, 0)) for w in accs]

    def body(*refs):
        nd = len(deps)
        in_refs, out_refs, acc_refs = refs[:ni], refs[ni + nd:ni + nd + no], refs[ni + nd + no:]
        res = fn(*[r[...] for r in in_refs])
        if not isinstance(res, (tuple, list)):
            res = (res,)
        for o_ref, r, t in zip(out_refs, res[:no], flipped):
            o_ref[...] = (r.astype(F32).T if t else r).astype(o_ref.dtype)
        if na:
            @pl.when(pl.program_id(0) == 0)
            def _():
                for a_ref in acc_refs:
                    a_ref[...] = jnp.zeros_like(a_ref)
            for a_ref, r in zip(acc_refs, res[no:]):
                a_ref[...] += r.astype(F32)

    res = pl.pallas_call(
        body, name=name, grid=(rows // tr,), in_specs=in_specs + [pl.BlockSpec(memory_space=pl.ANY)] * len(deps),
        out_specs=out_specs, out_shape=out_shapes, compiler_params=_cparams(("arbitrary",)),
    )(*arrs, *deps)
    return res


def _norm_mod(x, g, sh, sc):
    y = x * lax.rsqrt(jnp.mean(x * x, axis=-1, keepdims=True) + EPS) * g
    return y * (1.0 + sc) + sh


def _sigmoid(x):
    return 1.0 / (1.0 + jnp.exp(-x))


def _silu(x):
    return x * _sigmoid(x)


def _gelu(x):
    return 0.5 * x * (1.0 + jnp.tanh(math.sqrt(2.0 / math.pi) * (x + 0.044715 * (x * x * x))))


def _merge(ga, gs, ya, ys):
    return _sigmoid(ga) * ya + _sigmoid(gs) * ys


def _attn_head(q, kp, kc, vp, vc, sink, bias_p, bias_c, not_first):
    nt = (((1,), (1,)), ((), ()))
    nn = (((1,), (0,)), ((), ()))
    qb = q.astype(BF16)
    scale = HEAD_DIM ** -0.5
    sp = lax.dot_general(qb, kp.astype(BF16), nt, preferred_element_type=F32) * scale + bias_p
    sc = lax.dot_general(qb, kc.astype(BF16), nt, preferred_element_type=F32) * scale + bias_c
    qi = lax.broadcasted_iota(jnp.int32, sp.shape, 0) & (BLOCK - 1)
    ki = lax.broadcasted_iota(jnp.int32, sp.shape, 1)
    sp = jnp.where(jnp.logical_and(ki > qi, not_first), sp, NEG_INF)
    sc = jnp.where(ki <= qi, sc, NEG_INF)
    m = jnp.maximum(jnp.maximum(jnp.max(sp, axis=-1, keepdims=True), jnp.max(sc, axis=-1, keepdims=True)), sink)
    m = lax.stop_gradient(m)
    pp = jnp.exp(sp - m)
    pc = jnp.exp(sc - m)
    denom = jnp.sum(pp, axis=-1, keepdims=True) + jnp.sum(pc, axis=-1, keepdims=True) + jnp.exp(sink - m)
    o = lax.dot_general((pp / denom).astype(BF16), vp.astype(BF16), nn, preferred_element_type=F32)
    o = o + lax.dot_general((pc / denom).astype(BF16), vc.astype(BF16), nn, preferred_element_type=F32)
    return o


def _attn_fwd(qh, kh, vh, sinks, bias, name):
    s = qh.shape[1]
    nb = s // BLOCK
    G = GQA_GROUP
    R = G * BLOCK

    def body(q_ref, kp_ref, kc_ref, vp_ref, vc_ref, sink_ref, bias_ref, o_ref):
        not_first = pl.program_id(0) > 0
        for kv in range(N_KV_HEADS):
            hs = slice(kv * G, (kv + 1) * G)
            o = _attn_head(q_ref[hs].reshape(R, HEAD_DIM), kp_ref[kv], kc_ref[kv], vp_ref[kv], vc_ref[kv],
                           sink_ref[kv * R:(kv + 1) * R, 0:1],
                           bias_ref[hs, :, 0:BLOCK].reshape(R, BLOCK), bias_ref[hs, :, BLOCK:2 * BLOCK].reshape(R, BLOCK),
                           not_first)
            o_ref[hs] = o.reshape(G, BLOCK, HEAD_DIM).astype(o_ref.dtype)

    cur = lambda i: (0, i, 0)
    prev = lambda i: (0, jnp.maximum(i - 1, 0), 0)
    return pl.pallas_call(
        body, name=name, grid=(nb,),
        in_specs=[pl.BlockSpec((N_Q_HEADS, BLOCK, HEAD_DIM), cur),
                  pl.BlockSpec((N_KV_HEADS, BLOCK, HEAD_DIM), prev), pl.BlockSpec((N_KV_HEADS, BLOCK, HEAD_DIM), cur),
                  pl.BlockSpec((N_KV_HEADS, BLOCK, HEAD_DIM), prev), pl.BlockSpec((N_KV_HEADS, BLOCK, HEAD_DIM), cur),
                  pl.BlockSpec((N_Q_HEADS * BLOCK, 128), lambda i: (0, 0)),
                  pl.BlockSpec((N_Q_HEADS, BLOCK, 2 * BLOCK), lambda i: (0, 0, 0))],
        out_specs=pl.BlockSpec((N_Q_HEADS, BLOCK, HEAD_DIM), cur),
        out_shape=jax.ShapeDtypeStruct((N_Q_HEADS, s, HEAD_DIM), BF16),
        compiler_params=_cparams(("arbitrary",)),
    )(qh, kh, kh, vh, vh, sinks, bias)


def _attn_bwd(qh, kh, vh, doh, sinks, bias, name):
    s = qh.shape[1]
    nb = s // BLOCK
    G = GQA_GROUP
    R = G * BLOCK

    def body(q_ref, kp_ref, kc_ref, vp_ref, vc_ref, do_ref, sink_ref, bias_ref,
             dq_ref, dk_ref, dv_ref, dsink_ref, dbias_ref, ck, cv):
        i = pl.program_id(1)

        @pl.when(i == 0)
        def _():
            dsink_ref[...] = jnp.zeros_like(dsink_ref)
            dbias_ref[...] = jnp.zeros_like(dbias_ref)
            ck[...] = jnp.zeros_like(ck)
            cv[...] = jnp.zeros_like(cv)

        @pl.when(i < nb)
        def _():
            not_first = i > 0
            _, vjp = jax.vjp(lambda q, a, b, c, d, sk, e, f: _attn_head(q, a, b, c, d, sk, e, f, not_first),
                             q_ref[...].reshape(R, HEAD_DIM), kp_ref[...], kc_ref[...], vp_ref[...], vc_ref[...],
                             sink_ref[:, 0:1], bias_ref[:, :, 0:BLOCK].reshape(R, BLOCK),
                             bias_ref[:, :, BLOCK:2 * BLOCK].reshape(R, BLOCK))
            dq, dkp, dkc, dvp, dvc, dsk, dbp, dbc = vjp(do_ref[...].reshape(R, HEAD_DIM).astype(F32))
            dq_ref[...] = dq.reshape(G, BLOCK, HEAD_DIM)
            dsink_ref[...] += jnp.broadcast_to(dsk, (R, 128))
            dbias_ref[:, :, 0:BLOCK] += dbp.reshape(G, BLOCK, BLOCK)
            dbias_ref[:, :, BLOCK:2 * BLOCK] += dbc.reshape(G, BLOCK, BLOCK)
            dk_ref[...] = ck[...] + dkp
            dv_ref[...] = cv[...] + dvp
            ck[...] = dkc
            cv[...] = dvc

        @pl.when(i == nb)
        def _():
            dk_ref[...] = ck[...]
            dv_ref[...] = cv[...]

    qcur = lambda kv, i: (kv, jnp.minimum(i, nb - 1), 0)
    kcur = lambda kv, i: (kv, jnp.minimum(i, nb - 1), 0)
    kprev = lambda kv, i: (kv, jnp.clip(i - 1, 0, nb - 1), 0)
    qspec = pl.BlockSpec((G, BLOCK, HEAD_DIM), qcur)
    kc_spec = pl.BlockSpec((None, BLOCK, HEAD_DIM), kcur)
    kp_spec = pl.BlockSpec((None, BLOCK, HEAD_DIM), kprev)
    return pl.pallas_call(
        body, name=name, grid=(N_KV_HEADS, nb + 1),
        in_specs=[qspec, kp_spec, kc_spec, kp_spec, kc_spec, qspec,
                  pl.BlockSpec((R, 128), lambda kv, i: (kv, 0)),
                  pl.BlockSpec((G, BLOCK, 2 * BLOCK), lambda kv, i: (kv, 0, 0))],
        out_specs=[qspec, kp_spec, kp_spec,
                   pl.BlockSpec((R, 128), lambda kv, i: (kv, 0)),
                   pl.BlockSpec((G, BLOCK, 2 * BLOCK), lambda kv, i: (kv, 0, 0))],
        out_shape=[jax.ShapeDtypeStruct((N_Q_HEADS, s, HEAD_DIM), F32),
                   jax.ShapeDtypeStruct((N_KV_HEADS, s, HEAD_DIM), F32),
                   jax.ShapeDtypeStruct((N_KV_HEADS, s, HEAD_DIM), F32),
                   jax.ShapeDtypeStruct((N_Q_HEADS * BLOCK, 128), F32),
                   jax.ShapeDtypeStruct((N_Q_HEADS, BLOCK, 2 * BLOCK), F32)],
        scratch_shapes=[pltpu.VMEM((BLOCK, HEAD_DIM), F32), pltpu.VMEM((BLOCK, HEAD_DIM), F32)],
        compiler_params=_cparams(("arbitrary", "arbitrary")),
    )(qh, kh, kh, vh, vh, doh, sinks, bias)


def _cmul(ar, ai, br, bi):
    return ar * br - ai * bi, ar * bi + ai * br


def _scan_passes(a_ref, b_ref, x_ref, xp_ref, da_ref, *, s, tc, reverse):
    nc = SCAN_CHUNKS
    steps = s // nc
    with_da = xp_ref is not None
    unroll = 8 if steps % 8 == 0 else 1

    def shift(v, d):
        row = lax.broadcasted_iota(jnp.int32, v.shape, 0)
        if reverse:
            return jnp.where(row < nc - d, pltpu.roll(v, nc - d, 0), 0.0)
        return jnp.where(row >= d, pltpu.roll(v, d, 0), 0.0)

    def run():
        ar = jnp.broadcast_to(a_ref[0], (nc, tc))
        ai = jnp.broadcast_to(a_ref[1], (nc, tc))

        def row_of(step):
            j = (steps - 1 - step) if reverse else step
            return pl.multiple_of(j * nc, nc)

        def p1(step, st):
            sr, si = st
            r0 = row_of(step)
            mr, mi = _cmul(ar, ai, sr, si)
            sr = mr + b_ref[0, pl.ds(r0, nc), :]
            si = mi + b_ref[1, pl.ds(r0, nc), :]
            x_ref[0, pl.ds(r0, nc), :] = sr
            x_ref[1, pl.ds(r0, nc), :] = si
            return sr, si
        zero = jnp.zeros((nc, tc), F32)
        er, ei = lax.fori_loop(0, steps, p1, (zero, zero), unroll=unroll)

        pr, pi_ = jnp.ones((nc, tc), F32), zero
        br, bi, left = ar, ai, steps
        while left:
            if left & 1:
                pr, pi_ = _cmul(pr, pi_, br, bi)
            br, bi = _cmul(br, bi, br, bi)
            left >>= 1
        cr, ci = shift(er, 1), shift(ei, 1)
        d = 1
        while d < nc:
            mr, mi = _cmul(pr, pi_, shift(cr, d), shift(ci, d))
            cr, ci = cr + mr, ci + mi
            pr, pi_ = _cmul(pr, pi_, pr, pi_)
            d *= 2

        def p2(step, st):
            qr, qi, dar, dai = st
            r0 = row_of(step)
            qr, qi = _cmul(ar, ai, qr, qi)
            fr, fi = _cmul(qr, qi, cr, ci)
            xr = x_ref[0, pl.ds(r0, nc), :] + fr
            xi = x_ref[1, pl.ds(r0, nc), :] + fi
            x_ref[0, pl.ds(r0, nc), :] = xr
            x_ref[1, pl.ds(r0, nc), :] = xi
            if with_da:
                jm = jnp.where(step == steps - 1, steps - 1, steps - 2 - step)
                rp = pl.multiple_of(jm * nc, nc)
                vr, vi = xp_ref[0, pl.ds(rp, nc), :], xp_ref[1, pl.ds(rp, nc), :]
                row = lax.broadcasted_iota(jnp.int32, (nc, tc), 0)
                first = step == steps - 1
                sel = jnp.logical_and(first, row == 0)
                vr = jnp.where(sel, 0.0, jnp.where(first, pltpu.roll(vr, 1, 0), vr))
                vi = jnp.where(sel, 0.0, jnp.where(first, pltpu.roll(vi, 1, 0), vi))
                dar = dar + xr * vr + xi * vi
                dai = dai + xi * vr - xr * vi
            return qr, qi, dar, dai
        _, _, dar, dai = lax.fori_loop(0, steps, p2, (jnp.ones((nc, tc), F32), zero, zero, zero), unroll=unroll)
        if with_da:
            da_ref[0] = jnp.sum(dar, axis=0, keepdims=True)
            da_ref[1] = jnp.sum(dai, axis=0, keepdims=True)

    run()


def _ssm_fwd(u, bd, cd, a, d_row, *, name, sb, sbn):
    s, w = u.shape
    nst = a.shape[2]
    nblk = w // sb
    rows = min(512, s)
    nn = (((1,), (0,)), ((), ()))

    def body(u_ref, bre_ref, bim_ref, cre_ref, cim_ref, a_ref, d_ref, y_ref, x_ref):

        def fill(r, carry):
            r0 = pl.multiple_of(r * rows, rows)
            ub = u_ref[pl.ds(r0, rows), :].astype(BF16)
            x_ref[0, pl.ds(r0, rows), :] = lax.dot_general(ub, bre_ref[...].astype(BF16), nn, preferred_element_type=F32)
            x_ref[1, pl.ds(r0, rows), :] = lax.dot_general(ub, bim_ref[...].astype(BF16), nn, preferred_element_type=F32)
            return carry
        lax.fori_loop(0, s // rows, fill, 0)
        _scan_passes(a_ref, x_ref, x_ref, None, None, s=s, tc=sbn, reverse=False)

        def project(r, carry):
            r0 = pl.multiple_of(r * rows, rows)
            y = lax.dot_general(x_ref[0, pl.ds(r0, rows), :].astype(BF16), cre_ref[...].astype(BF16), nn, preferred_element_type=F32)
            y = y + lax.dot_general(x_ref[1, pl.ds(r0, rows), :].astype(BF16), cim_ref[...].astype(BF16), nn, preferred_element_type=F32)
            y_ref[pl.ds(r0, rows), :] = y + d_ref[...] * u_ref[pl.ds(r0, rows), :]
            return carry
        lax.fori_loop(0, s // rows, project, 0)

    return pl.pallas_call(
        body, name=name, grid=(nblk,),
        in_specs=[pl.BlockSpec((s, sb), lambda j: (0, j)),
                  pl.BlockSpec((sb, sbn), lambda j: (j, j)), pl.BlockSpec((sb, sbn), lambda j: (j, nblk + j)),
                  pl.BlockSpec((sbn, sb), lambda j: (j, j)), pl.BlockSpec((sbn, sb), lambda j: (nblk + j, j)),
                  pl.BlockSpec((2, 1, sbn), lambda j: (0, 0, j)), pl.BlockSpec((1, sb), lambda j: (0, j))],
        out_specs=[pl.BlockSpec((s, sb), lambda j: (0, j)), pl.BlockSpec((2, s, sbn), lambda j: (0, 0, j))],
        out_shape=[jax.ShapeDtypeStruct((s, w), F32), jax.ShapeDtypeStruct((2, s, nst), F32)],
        compiler_params=pltpu.CompilerParams(dimension_semantics=("arbitrary",), vmem_limit_bytes=SSM_VMEM_LIMIT_BYTES),
    )(u, bd, bd, cd, cd, a, d_row)


def _ssm_bwd(dy, u, xs, bd, cd, a, d_row, *, name, sb, sbn):
    s, w = u.shape
    nst = a.shape[2]
    nblk = w // sb
    rows = min(512, s)
    nt = (((1,), (1,)), ((), ()))
    tn = (((0,), (0,)), ((), ()))

    def body(dy_ref, u_ref, xs_hbm, bre_ref, bim_ref, cre_ref, cim_ref, a_ref, d_ref,
             du_ref, gb_ref, gc_ref, da_ref, gd_ref, lam, xs_ref, sem):
        j = pl.program_id(0)
        fetch = pltpu.make_async_copy(xs_hbm.at[:, :, pl.ds(pl.multiple_of(j * sbn, sbn), sbn)], xs_ref, sem)
        fetch.start()

        def fill(r, carry):
            r0 = pl.multiple_of(r * rows, rows)
            dyb = dy_ref[pl.ds(r0, rows), :].astype(BF16)
            lam[0, pl.ds(r0, rows), :] = lax.dot_general(dyb, cre_ref[...].astype(BF16), nt, preferred_element_type=F32)
            lam[1, pl.ds(r0, rows), :] = lax.dot_general(dyb, cim_ref[...].astype(BF16), nt, preferred_element_type=F32)
            return carry
        lax.fori_loop(0, s // rows, fill, 0)
        fetch.wait()
        _scan_passes(a_ref, lam, lam, xs_ref, da_ref, s=s, tc=sbn, reverse=True)
        gb_ref[...] = jnp.zeros_like(gb_ref)
        gc_ref[...] = jnp.zeros_like(gc_ref)
        gd_ref[...] = jnp.zeros_like(gd_ref)

        def project(r, carry):
            r0 = pl.multiple_of(r * rows, rows)
            dyv, uv = dy_ref[pl.ds(r0, rows), :], u_ref[pl.ds(r0, rows), :]
            dyb, ub = dyv.astype(BF16), uv.astype(BF16)
            lr, li = lam[0, pl.ds(r0, rows), :].astype(BF16), lam[1, pl.ds(r0, rows), :].astype(BF16)
            du = lax.dot_general(lr, bre_ref[...].astype(BF16), nt, preferred_element_type=F32)
            du = du + lax.dot_general(li, bim_ref[...].astype(BF16), nt, preferred_element_type=F32)
            du_ref[pl.ds(r0, rows), :] = du + d_ref[...] * dyv
            gb_ref[:, 0:sbn] += lax.dot_general(ub, lr, tn, preferred_element_type=F32)
            gb_ref[:, sbn:2 * sbn] += lax.dot_general(ub, li, tn, preferred_element_type=F32)
            gc_ref[0] += lax.dot_general(xs_ref[0, pl.ds(r0, rows), :].astype(BF16), dyb, tn, preferred_element_type=F32)
            gc_ref[1] += lax.dot_general(xs_ref[1, pl.ds(r0, rows), :].astype(BF16), dyb, tn, preferred_element_type=F32)
            gd_ref[...] += jnp.sum(dyv * uv, axis=0, keepdims=True)
            return carry
        lax.fori_loop(0, s // rows, project, 0)

    col = lambda j: (0, j)
    return pl.pallas_call(
        body, name=name, grid=(nblk,),
        in_specs=[pl.BlockSpec((s, sb), col), pl.BlockSpec((s, sb), col), pl.BlockSpec(memory_space=pl.ANY),
                  pl.BlockSpec((sb, sbn), lambda j: (j, j)), pl.BlockSpec((sb, sbn), lambda j: (j, nblk + j)),
                  pl.BlockSpec((sbn, sb), lambda j: (j, j)), pl.BlockSpec((sbn, sb), lambda j: (nblk + j, j)),
                  pl.BlockSpec((2, 1, sbn), lambda j: (0, 0, j)), pl.BlockSpec((1, sb), col)],
        out_specs=[pl.BlockSpec((s, sb), col), pl.BlockSpec((sb, 2 * sbn), lambda j: (j, 0)),
                   pl.BlockSpec((2, sbn, sb), lambda j: (0, j, 0)), pl.BlockSpec((2, 1, sbn), lambda j: (0, 0, j)),
                   pl.BlockSpec((1, sb), col)],
        out_shape=[jax.ShapeDtypeStruct((s, w), F32), jax.ShapeDtypeStruct((w, 2 * sbn), F32),
                   jax.ShapeDtypeStruct((2, nst, sb), F32), jax.ShapeDtypeStruct((2, 1, nst), F32),
                   jax.ShapeDtypeStruct((1, w), F32)],
        scratch_shapes=[pltpu.VMEM((2, s, sbn), F32), pltpu.VMEM((2, s, sbn), F32), pltpu.SemaphoreType.DMA],
        compiler_params=pltpu.CompilerParams(dimension_semantics=("arbitrary",), vmem_limit_bytes=SSM_VMEM_LIMIT_BYTES),
    )(dy, u, xs, bd, bd, cd, cd, a, d_row)


def _adamw(w, g, m, v, name, deps=()):
    nd = len(deps)
    r, c = w.shape
    tr = r
    for cand in (512, 256, 128, 64, 32, 16, 8):
        if r % cand == 0 and cand * c * 4 <= 2 * 1024 * 1024:
            tr = cand
            break

    def body(w_ref, g_ref, m_ref, v_ref, *rest):
        d_ref, nm_ref, nv_ref = rest[nd:]
        gv = g_ref[...]
        nm = ADAM_B1 * m_ref[...] + (1.0 - ADAM_B1) * gv
        nv = ADAM_B2 * v_ref[...] + (1.0 - ADAM_B2) * (gv * gv)
        m_hat = nm / (1.0 - ADAM_B1 ** ADAM_STEP)
        v_hat = nv / (1.0 - ADAM_B2 ** ADAM_STEP)
        d_ref[...] = -ADAM_LR * (m_hat / (jnp.sqrt(v_hat) + ADAM_EPS) + ADAM_WD * w_ref[...])
        nm_ref[...] = nm
        nv_ref[...] = nv

    spec = pl.BlockSpec((tr, c), lambda i: (i, 0))
    sds = jax.ShapeDtypeStruct((r, c), F32)
    return pl.pallas_call(body, name=name, grid=(r // tr,),
                          in_specs=[spec] * 4 + [pl.BlockSpec(memory_space=pl.ANY)] * nd, out_specs=[spec] * 3,
                          out_shape=[sds] * 3, compiler_params=_cparams(("parallel",)))(w, g, m, v, *deps)


def _sum_lead(x, name, out_dtype=F32):
    n, r, c = x.shape
    tr = r
    for cand in (512, 256, 128, 64, 32, 16, 8):
        if r % cand == 0 and n * cand * c * 4 <= 4 * 1024 * 1024:
            tr = cand
            break

    def body(x_ref, o_ref):
        acc = x_ref[0].astype(F32)
        for k in range(1, n):
            acc = acc + x_ref[k].astype(F32)
        o_ref[...] = acc.astype(o_ref.dtype)

    return pl.pallas_call(body, name=name, grid=(r // tr,),
                          in_specs=[pl.BlockSpec((n, tr, c), lambda i: (0, i, 0))],
                          out_specs=pl.BlockSpec((tr, c), lambda i: (i, 0)),
                          out_shape=jax.ShapeDtypeStruct((r, c), out_dtype),
                          compiler_params=_cparams(("parallel",)))(x)


def _row_tile(rows, row_bytes, budget, least=8):
    for cand in (1024, 512, 256, 128, 64, 32, 16, 8):
        if cand >= least and rows % cand == 0 and cand * row_bytes <= budget:
            return cand
    return rows


def _cast_into_slot(w, slot, name):
    r, c = w.shape
    tr = _row_tile(r, c * 4, 4 * 1024 * 1024, least=16)

    def body(slot_ref, w_ref, o_ref):
        o_ref[...] = w_ref[...].astype(o_ref.dtype)

    gs = pltpu.PrefetchScalarGridSpec(
        num_scalar_prefetch=1, grid=(r // tr,),
        in_specs=[pl.BlockSpec((tr, c), lambda i, s: (i, 0))],
        out_specs=pl.BlockSpec((None, tr, c), lambda i, s: (s[0], i, 0)))
    return pl.pallas_call(body, name=name, grid_spec=gs, out_shape=jax.ShapeDtypeStruct((N_CHIPS, r, c), BF16),
                          compiler_params=_cparams(("parallel",)))(slot, w)


def _sum_own(p, t, sel, name):
    _, h, c = p.shape
    tr = _row_tile(h, c * 4, 2 * 1024 * 1024, least=16)
    nblk = h // tr

    def body(sel_ref, p_ref, t_ref, o_ref):
        acc = p_ref[...].astype(F32)
        for k in range(3):
            acc = acc + t_ref[k].astype(F32)
        o_ref[...] = acc

    gs = pltpu.PrefetchScalarGridSpec(
        num_scalar_prefetch=1, grid=(nblk,),
        in_specs=[pl.BlockSpec((None, tr, c), lambda i, s: (s[0], i, 0)),
                  pl.BlockSpec((3, tr, c), lambda i, s: (0, i, 0))],
        out_specs=pl.BlockSpec((tr, c), lambda i, s: (s[1] * nblk + i, 0)))
    return pl.pallas_call(body, name=name, grid_spec=gs, out_shape=jax.ShapeDtypeStruct((2 * h, c), F32),
                          compiler_params=_cparams(("parallel",)))(sel, p, t)


def _add_half(g, t, half, name):
    n, r, c = g.shape
    h = r // 2
    tr = h
    for cand in (512, 256, 128, 64, 32, 16):
        if h % cand == 0 and cand * c * 2 <= 2 * 1024 * 1024:
            tr = cand
            break
    nblk = h // tr

    def body(half_ref, g_ref, t_ref, o_ref):
        o_ref[...] = (g_ref[...].astype(F32) + t_ref[...].astype(F32)).astype(o_ref.dtype)

    gs = pltpu.PrefetchScalarGridSpec(
        num_scalar_prefetch=1, grid=(n, nblk),
        in_specs=[pl.BlockSpec((None, tr, c), lambda j, i, hr: (j, hr[0] * nblk + i, 0)),
                  pl.BlockSpec((None, tr, c), lambda j, i, hr: (j, i, 0))],
        out_specs=pl.BlockSpec((None, tr, c), lambda j, i, hr: (j, i, 0)))
    return pl.pallas_call(body, name=name, grid_spec=gs, out_shape=jax.ShapeDtypeStruct((n, h, c), BF16),
                          compiler_params=_cparams(("parallel", "parallel")))(half, g, t)


def _position():
    x, y, c = lax.axis_index("x"), lax.axis_index("y"), lax.axis_index("c")
    return x, y, c


def _allgather8(xs, name):
    m_per, n = xs.shape

    def body(x_ref, out_ref, send_sems, recv_sems, local_sem):
        x, y, c = _position()
        me, sibling = (x, y, c), (x, y, 1 - c)
        chips = [(1 - x, y), (x, 1 - y), (1 - x, 1 - y)]

        def rows(px, py, pc):
            return out_ref.at[pl.ds((4 * px + 2 * py + pc) * m_per, m_per), :]

        def copy(k, block, to, src=None):
            return pltpu.make_async_remote_copy(
                src_ref=rows(*block) if src is None else src, dst_ref=rows(*block),
                send_sem=send_sems.at[k], recv_sem=recv_sems.at[k], device_id=to, device_id_type=MESH)

        mine = pltpu.make_async_copy(x_ref, rows(*me), local_sem)
        mine.start()
        first = [copy(0, me, sibling, src=x_ref)]
        first += [copy(1 + j, me, (*chip, c), src=x_ref) for j, chip in enumerate(chips)]
        for cp in first:
            cp.start()
        passed = [copy(4 + j, (*chip, c), sibling) for j, chip in enumerate(chips)]
        for j, chip in enumerate(chips):
            copy(1 + j, (*chip, c), me).wait_recv()
            passed[j].start()
        copy(0, sibling, me).wait_recv()
        for j, chip in enumerate(chips):
            copy(4 + j, (*chip, 1 - c), me).wait_recv()
        for cp in first + passed:
            cp.wait_send()
        mine.wait()

    return pl.pallas_call(
        body, name=name, out_shape=jax.ShapeDtypeStruct((N_DEV * m_per, n), xs.dtype),
        in_specs=[pl.BlockSpec(memory_space=pltpu.VMEM)], out_specs=pl.BlockSpec(memory_space=pltpu.VMEM),
        scratch_shapes=[pltpu.SemaphoreType.DMA((7,)), pltpu.SemaphoreType.DMA((7,)), pltpu.SemaphoreType.DMA],
        compiler_params=pltpu.CompilerParams(vmem_limit_bytes=VMEM_LIMIT_BYTES),
    )(xs)


_HBM = pl.BlockSpec(memory_space=pltpu.HBM)


_SEM = pl.BlockSpec(memory_space=pltpu.SEMAPHORE)
_ANY = pl.BlockSpec(memory_space=pl.ANY)
_EFFECT = pltpu.SideEffectType.DATAFLOW_SIDE_EFFECTING


def _in_hbm(a):
    return pltpu.with_memory_space_constraint(a, pltpu.HBM)


def _gather_start(ws, groups, after, name):
    n = len(ws)

    def body(*refs):
        in_refs = refs[:n]
        sems, token = refs[2 * n + 1:-1], refs[-1]
        x, y, c = _position()
        mychip = 2 * x + y
        chips = [(1 - x, y), (x, 1 - y), (1 - x, 1 - y)]
        for g, members in enumerate(groups):
            for k, i in enumerate(members):
                h = ws[i].shape[1] // 2
                mine = in_refs[i].at[mychip, pl.ds(c * h, h), :]
                for j, (px, py) in enumerate(chips):
                    pltpu.make_async_remote_copy(
                        src_ref=mine, dst_ref=mine, send_sem=sems[2 * g].at[3 * k + j],
                        recv_sem=sems[2 * g + 1].at[3 * k + j], device_id=(px, py, c), device_id_type=MESH).start()
        token[...] = jnp.zeros_like(token)

    sem_shapes = [pltpu.SemaphoreType.DMA((3 * len(m),)) for m in groups for _ in range(2)]
    res = pl.pallas_call(
        body, name=name,
        out_shape=[pltpu.HBM(w.shape, w.dtype) for w in ws] + sem_shapes + [jax.ShapeDtypeStruct((8, 128), F32)],
        in_specs=[_HBM] * n + [_ANY],
        out_specs=[_HBM] * n + [_SEM] * len(sem_shapes) + [pl.BlockSpec(memory_space=pltpu.VMEM)],
        input_output_aliases={i: i for i in range(n)},
        compiler_params=pltpu.CompilerParams(has_side_effects=_EFFECT),
    )(*[_in_hbm(w) for w in ws], after)
    bufs, sems, token = res[:n], res[n:-1], res[-1]
    return list(bufs), [(sems[2 * g], sems[2 * g + 1]) for g in range(len(groups))], token


def _gather_wait(bufs, send_sems, recv_sems, after, name):
    m = len(bufs)

    def body(*refs):
        in_refs = refs[:m]
        send, recv = refs[m], refs[m + 1]
        x, y, c = _position()
        mychip = 2 * x + y
        chips = [(1 - x, y), (x, 1 - y), (1 - x, 1 - y)]
        for k in range(m):
            h = bufs[k].shape[1] // 2
            mine = in_refs[k].at[mychip, pl.ds(c * h, h), :]
            for j, (px, py) in enumerate(chips):
                cp = pltpu.make_async_remote_copy(
                    src_ref=mine, dst_ref=in_refs[k].at[2 * px + py, pl.ds(c * h, h), :],
                    send_sem=send.at[3 * k + j], recv_sem=recv.at[3 * k + j],
                    device_id=(px, py, c), device_id_type=MESH)
                cp.wait_send()
                cp.wait_recv()

    res = pl.pallas_call(
        body, name=name, out_shape=[pltpu.HBM(b.shape, b.dtype) for b in bufs],
        in_specs=[_HBM] * m + [_SEM, _SEM, _ANY], out_specs=[_HBM] * m,
        input_output_aliases={k: k for k in range(m)},
        compiler_params=pltpu.CompilerParams(has_side_effects=_EFFECT),
    )(*bufs, send_sems, recv_sems, after)
    return list(res)


def _forward_halves(ws, name):
    n = len(ws)

    def body(*refs):
        out_refs = refs[n:2 * n]
        send_sems, recv_sems = refs[2 * n:]
        x, y, c = _position()
        me, sibling = (x, y, c), (x, y, 1 - c)
        chips = [(1 - x, y), (x, 1 - y), (1 - x, 1 - y)]
        cps = []
        for i in range(n):
            h = ws[i].shape[1] // 2
            for j, (px, py) in enumerate(chips):
                got = out_refs[i].at[2 * px + py, pl.ds(c * h, h), :]
                cp = pltpu.make_async_remote_copy(
                    src_ref=got, dst_ref=got, send_sem=send_sems.at[3 * i + j], recv_sem=recv_sems.at[3 * i + j],
                    device_id=sibling, device_id_type=MESH)
                cp.start()
                cps.append(cp)
        for i in range(n):
            h = ws[i].shape[1] // 2
            for j, (px, py) in enumerate(chips):
                other = out_refs[i].at[2 * px + py, pl.ds((1 - c) * h, h), :]
                pltpu.make_async_remote_copy(
                    src_ref=other, dst_ref=other, send_sem=send_sems.at[3 * i + j], recv_sem=recv_sems.at[3 * i + j],
                    device_id=me, device_id_type=MESH).wait_recv()
        for cp in cps:
            cp.wait_send()

    return pl.pallas_call(
        body, name=name,
        out_shape=[jax.ShapeDtypeStruct(w.shape, w.dtype) for w in ws],
        in_specs=[_HBM] * n, out_specs=[_HBM] * n, input_output_aliases={i: i for i in range(n)},
        scratch_shapes=[pltpu.SemaphoreType.DMA((3 * n,)), pltpu.SemaphoreType.DMA((3 * n,))],
    )(*ws)


def _swap_halves(gs, name):
    n = len(gs)

    def body(*refs):
        in_refs, out_refs = refs[:n], refs[n:2 * n]
        send_sems, recv_sems = refs[2 * n:]
        x, y, c = _position()
        cps = []
        for i in range(n):
            h = gs[i].shape[1] // 2
            cp = pltpu.make_async_remote_copy(
                src_ref=in_refs[i].at[:, pl.ds((1 - c) * h, h), :], dst_ref=out_refs[i],
                send_sem=send_sems.at[i], recv_sem=recv_sems.at[i], device_id=(x, y, 1 - c), device_id_type=MESH)
            cp.start()
            cps.append(cp)
        for cp in cps:
            cp.wait()

    return pl.pallas_call(
        body, name=name,
        out_shape=[jax.ShapeDtypeStruct((g.shape[0], g.shape[1] // 2, g.shape[2]), g.dtype) for g in gs],
        in_specs=[_HBM] * n, out_specs=[_HBM] * n,
        scratch_shapes=[pltpu.SemaphoreType.DMA((n,)), pltpu.SemaphoreType.DMA((n,))],
    )(*gs)


def _copies_start(arrays, copies, nsem, after, name):
    n = len(arrays)

    def body(*refs):
        for cp in copies(refs[:n], refs[2 * n + 1], refs[2 * n + 2]):
            cp.start()
        refs[2 * n + 3][...] = jnp.zeros_like(refs[2 * n + 3])

    res = pl.pallas_call(
        body, name=name,
        out_shape=[pltpu.HBM(a.shape, a.dtype) for a in arrays]
        + [pltpu.SemaphoreType.DMA((nsem,)), pltpu.SemaphoreType.DMA((nsem,)), jax.ShapeDtypeStruct((8, 128), F32)],
        in_specs=[_HBM] * n + [_ANY],
        out_specs=[_HBM] * n + [_SEM, _SEM, pl.BlockSpec(memory_space=pltpu.VMEM)],
        input_output_aliases={i: i for i in range(n)},
        compiler_params=pltpu.CompilerParams(has_side_effects=_EFFECT),
    )(*[_in_hbm(a) for a in arrays], after)
    return list(res[:n]), res[n], res[n + 1], res[n + 2]


def _copies_wait(arrays, copies, send_sems, recv_sems, after, name):
    n = len(arrays)

    def body(*refs):
        for cp in copies(refs[:n], refs[n], refs[n + 1]):
            cp.wait_send()
            cp.wait_recv()

    res = pl.pallas_call(
        body, name=name, out_shape=[pltpu.HBM(a.shape, a.dtype) for a in arrays],
        in_specs=[_HBM] * n + [_SEM, _SEM, _ANY], out_specs=[_HBM] * n,
        input_output_aliases={i: i for i in range(n)},
        compiler_params=pltpu.CompilerParams(has_side_effects=_EFFECT),
    )(*arrays, send_sems, recv_sems, after)
    return list(res)


def _scatter_copies(refs, send, recv):
    n = len(refs) // 2
    x, y, c = _position()
    chips = [(1 - x, y), (x, 1 - y), (1 - x, 1 - y)]
    return [pltpu.make_async_remote_copy(
        src_ref=refs[i].at[2 * px + py], dst_ref=refs[n + i].at[j],
        send_sem=send.at[3 * i + j], recv_sem=recv.at[3 * i + j], device_id=(px, py, c), device_id_type=MESH)
        for i in range(n) for j, (px, py) in enumerate(chips)]


def _swap_copies(refs, send, recv):
    n = len(refs) // 2
    x, y, c = _position()
    cps = []
    for i in range(n):
        h = refs[i].shape[1] // 2
        cps.append(pltpu.make_async_remote_copy(
            src_ref=refs[i].at[:, pl.ds((1 - c) * h, h), :], dst_ref=refs[n + i],
            send_sem=send.at[i], recv_sem=recv.at[i], device_id=(x, y, 1 - c), device_id_type=MESH))
    return cps


def _join_copies(refs, send, recv):
    x, y, c = _position()
    cps = []
    for i, r in enumerate(refs):
        h = r.shape[0] // 2
        mine = r.at[pl.ds(c * h, h), :]
        cps.append(pltpu.make_async_remote_copy(
            src_ref=mine, dst_ref=mine, send_sem=send.at[i], recv_sem=recv.at[i],
            device_id=(x, y, 1 - c), device_id_type=MESH))
    return cps


def _forward_copies(refs, send, recv):
    x, y, c = _position()
    chips = [(1 - x, y), (x, 1 - y), (1 - x, 1 - y)]
    cps = []
    for i, r in enumerate(refs):
        h = r.shape[1] // 2
        for j, (px, py) in enumerate(chips):
            got = r.at[2 * px + py, pl.ds(c * h, h), :]
            cps.append(pltpu.make_async_remote_copy(
                src_ref=got, dst_ref=got, send_sem=send.at[3 * i + j], recv_sem=recv.at[3 * i + j],
                device_id=(x, y, 1 - c), device_id_type=MESH))
    return cps


def _join_halves(rs, name):
    n = len(rs)

    def body(*refs):
        out_refs = refs[n:2 * n]
        send_sems, recv_sems = refs[2 * n:]
        x, y, c = _position()
        cps = []
        for i in range(n):
            h = rs[i].shape[0] // 2
            mine = out_refs[i].at[pl.ds(c * h, h), :]
            cp = pltpu.make_async_remote_copy(
                src_ref=mine, dst_ref=mine, send_sem=send_sems.at[i], recv_sem=recv_sems.at[i],
                device_id=(x, y, 1 - c), device_id_type=MESH)
            cp.start()
            cps.append(cp)
        for i in range(n):
            h = rs[i].shape[0] // 2
            other = out_refs[i].at[pl.ds((1 - c) * h, h), :]
            pltpu.make_async_remote_copy(
                src_ref=other, dst_ref=other, send_sem=send_sems.at[i], recv_sem=recv_sems.at[i],
                device_id=(x, y, c), device_id_type=MESH).wait_recv()
        for cp in cps:
            cp.wait_send()

    return pl.pallas_call(
        body, name=name,
        out_shape=[jax.ShapeDtypeStruct(r.shape, r.dtype) for r in rs],
        in_specs=[_HBM] * n, out_specs=[_HBM] * n, input_output_aliases={i: i for i in range(n)},
        scratch_shapes=[pltpu.SemaphoreType.DMA((n,)), pltpu.SemaphoreType.DMA((n,))],
    )(*rs)


def _t5_buckets_block():
    qi = np.arange(BLOCK)[:, None]
    ki = np.arange(2 * BLOCK)[None, :]
    n = np.maximum(qi + BLOCK - ki, 0)
    max_exact = NUM_BUCKETS // 2
    large = max_exact + (np.log(np.maximum(n, 1) / max_exact) / np.log(MAX_DISTANCE / max_exact)
                         * (NUM_BUCKETS - max_exact)).astype(np.int32)
    large = np.minimum(large, NUM_BUCKETS - 1)
    return np.where(n < max_exact, n, large).astype(np.int32)


def _discretise(lambda_re, lambda_im, log_step, b_re, b_im):
    lam_re = jnp.minimum(lambda_re, -1e-4)
    lam_im = lambda_im
    delta = jnp.exp(log_step)[:, None]
    mag = jnp.exp(lam_re * delta)
    ang = lam_im * delta
    abar_re, abar_im = mag * jnp.cos(ang), mag * jnp.sin(ang)
    num_re, num_im = abar_re - 1.0, abar_im
    den = lam_re * lam_re + lam_im * lam_im
    f_re = (num_re * lam_re + num_im * lam_im) / den
    f_im = (num_im * lam_re - num_re * lam_im) / den
    bbar_re = f_re[..., None] * b_re - f_im[..., None] * b_im
    bbar_im = f_re[..., None] * b_im + f_im[..., None] * b_re
    return abar_re, abar_im, bbar_re, bbar_im


def _interleave(v, nc):
    s, w = v.shape
    return v.reshape(nc, s // nc, w).transpose(1, 0, 2).reshape(s, w)


def _deinterleave(v, nc):
    s, w = v.shape
    return v.reshape(s // nc, nc, w).transpose(1, 0, 2).reshape(s, w)


_SMALL = ("norm1_g", "b_in", "attn_sinks", "rel_bias", "lambda_re", "lambda_im", "log_step", "ssm_b_re",
          "ssm_b_im", "ssm_c_re", "ssm_c_im", "ssm_d", "b_glu", "norm2_g", "final_g")


def _pack(parts):
    rows = []
    for p in parts:
        f = p.reshape(-1).astype(F32)
        pad = (-f.shape[0]) % 128
        rows.append(jnp.pad(f, (0, pad)).reshape(-1, 128))
    out = jnp.concatenate(rows, axis=0)
    pad = (-out.shape[0]) % 256
    return jnp.pad(out, ((0, pad), (0, 0)))


def _unpack(packed, shapes):
    res, r = [], 0
    for shp in shapes:
        size = int(np.prod(shp))
        nr = -(-size // 128)
        res.append(packed[r:r + nr].reshape(-1)[:size].reshape(shp))
        r += nr
    return res


def kernel(x, c, w_ada, b_ada, norm1_g, w_in, b_in, attn_sinks, rel_bias, lambda_re, lambda_im, log_step, ssm_b_re, ssm_b_im, ssm_c_re, ssm_c_im, ssm_d, w_glu, b_glu, w_attn_proj, w_ssm_proj, w_out, norm2_g, w_ff1, w_ff2, final_g, loss_target, m_w_ada, m_b_ada, m_norm1_g, m_w_in, m_b_in, m_attn_sinks, m_rel_bias, m_lambda_re, m_lambda_im, m_log_step, m_ssm_b_re, m_ssm_b_im, m_ssm_c_re, m_ssm_c_im, m_ssm_d, m_w_glu, m_b_glu, m_w_attn_proj, m_w_ssm_proj, m_w_out, m_norm2_g, m_w_ff1, m_w_ff2, m_final_g, v_w_ada, v_b_ada, v_norm1_g, v_w_in, v_b_in, v_attn_sinks, v_rel_bias, v_lambda_re, v_lambda_im, v_log_step, v_ssm_b_re, v_ssm_b_im, v_ssm_c_re, v_ssm_c_im, v_ssm_d, v_w_glu, v_b_glu, v_w_attn_proj, v_w_ssm_proj, v_w_out, v_norm2_g, v_w_ff1, v_w_ff2, v_final_g):
    given = dict(locals())
    S, D = x.shape[1], x.shape[2]
    SSM_W = w_glu.shape[2]
    G = SSM_W // SSM_GROUP_CH
    NST = G * SSM_STATE
    DFF = w_ff2.shape[1] * N_CHIPS
    INW = w_in.shape[2] * N_CHIPS
    o_q, o_k, o_v, o_u = 0, ATTN_WIDTH, ATTN_WIDTH + KV_WIDTH, ATTN_WIDTH + 2 * KV_WIDTH
    o_ga, o_gs = o_u + SSM_W, o_u + SSM_W + D
    mx, my, mc = _position()
    my_chip = 2 * mx + my
    my_b = 4 * mx + 2 * my + mc

    xv, tgt = x[0], loss_target[0]

    big = dict(w_in=w_in[0], w_glu=w_glu[0], w_attn_proj=w_attn_proj[0], w_ssm_proj=w_ssm_proj[0],
               w_out=w_out[0], w_ff1=w_ff1[0], w_ff2=w_ff2[0])
    big_names = list(big)
    colsharded = {"w_in", "w_attn_proj", "w_ssm_proj", "w_ff1"}
    chip_sel = my_chip.astype(jnp.int32).reshape(1)
    gather_groups = [["w_in"], ["w_attn_proj", "w_ssm_proj", "w_glu", "w_out"], ["w_ff1", "w_ff2"]]
    in_flight, gather_sems, gathered = {}, [], {}

    def finish_gather(g, after):
        bufs = [in_flight[k] for k in gather_groups[g]]
        bufs = _gather_wait(bufs, gather_sems[g][0], gather_sems[g][1], after, "gather_wait_%d" % g)
        gathered.update(zip(gather_groups[g], _forward_halves(bufs, "gather_forward_%d" % g)))

    def tied(v, token):
        return v + token[0:1, 0:1]

    def all_of(*arrays):
        return jnp.stack([a.reshape(-1)[0].astype(F32) for a in arrays])

    def wop(k):
        g = gathered[k]
        return _Op(g, N_CHIPS) if k in colsharded else _Op(g.reshape(g.shape[0] * g.shape[1], g.shape[2]))

    grads = {}
    nothing = jnp.zeros((8, 128), F32)
    half = mc.astype(jnp.int32).reshape(1)
    sel = jnp.stack([my_chip, mc]).astype(jnp.int32)

    def rs_swap(tag, named):
        keys, gl = list(named), []
        for k in keys:
            gk = named[k]
            if k not in colsharded:
                gk = gk.reshape(N_CHIPS, gk.shape[0] // N_CHIPS, gk.shape[1])
            gl.append(gk)
        lands = [lax.empty((g.shape[0], g.shape[1] // 2, g.shape[2]), g.dtype) for g in gl]
        arrays, ssem, rsem, token = _copies_start(gl + lands, _swap_copies, len(gl), nothing, "rs_swap_start_" + tag)
        return (keys, arrays, ssem, rsem), token

    def rs_scatter(tag, state, after):
        keys, arrays, ssem, rsem = state
        arrays = _copies_wait(arrays, _swap_copies, ssem, rsem, after, "rs_swap_wait_" + tag)
        n = len(keys)
        ps = [_add_half(g, t, half, "rs_add_" + k) for g, t, k in zip(arrays[:n], arrays[n:], keys)]
        lands = [lax.empty((3,) + p.shape[1:], p.dtype) for p in ps]
        arrays, ssem, rsem, token = _copies_start(ps + lands, _scatter_copies, 3 * n, nothing, "rs_start_" + tag)
        return (keys, arrays, ssem, rsem), token

    def rs_sum(tag, state, after):
        keys, arrays, ssem, rsem = state
        arrays = _copies_wait(arrays, _scatter_copies, ssem, rsem, after, "rs_wait_" + tag)
        n = len(keys)
        rs = [_sum_own(p, t, sel, "rs_sum_" + k) for p, t, k in zip(arrays[:n], arrays[n:], keys)]
        rs, ssem, rsem, token = _copies_start(rs, _join_copies, n, nothing, "rs_join_start_" + tag)
        return (keys, rs, ssem, rsem), token

    def rs_finish(tag, state, after):
        keys, rs, ssem, rsem = state
        for k, f in zip(keys, _copies_wait(rs, _join_copies, ssem, rsem, after, "rs_join_wait_" + tag)):
            grads[k] = f[None]

    c_all = _allgather8(jnp.pad(c, ((0, 7), (0, 0))), "gather_c").reshape(N_DEV, 8, D)[:, 0]
    c16 = jnp.pad(c_all, ((0, 8), (0, 0)))
    b_ada_mine = lax.dynamic_slice(b_ada.reshape(N_CHIPS, -1), (my_chip, 0), (1, w_ada.shape[2]))
    mod_sh = _mm(c16, w_ada[0], "NN", name="mod", M=16, N=w_ada.shape[2], K=D, a_fn=_silu,
                 epilogue=lambda acc, b: (acc + b,), extras=[(b_ada_mine, "row")])
    mod_all = _allgather8(mod_sh[:8], "gather_mod").reshape(N_DEV, 8, -1)
    mod_row = jnp.concatenate(
        [lax.dynamic_slice(mod_all, (2 * j, my_b, 0), (1, 1, mod_all.shape[2]))[0] for j in range(N_CHIPS)], axis=1)
    sh1, sc1, g1, sh2, sc2, g2 = [mod_row[:, i * D:(i + 1) * D] for i in range(6)]

    first = [_cast_into_slot(big["w_in"], chip_sel, "cast_w_in")]
    first, sems_first, token_first = _gather_start(first, [[0]], mod_all, "gather_start_in")
    rest_names = gather_groups[1] + gather_groups[2]
    rest = [_cast_into_slot(big[k], chip_sel, "cast_" + k) for k in rest_names]
    rest, sems_rest, token_rest = _gather_start(
        rest, [[rest_names.index(k) for k in grp] for grp in gather_groups[1:]], token_first, "gather_start_rest")
    in_flight.update(zip(["w_in"] + rest_names, first + rest))
    gather_sems.extend(sems_first + sems_rest)

    disc_in = (lambda_re[0], lambda_im[0], log_step[0], ssm_b_re[0], ssm_b_im[0])
    (abar_re, abar_im, bbar_re, bbar_im), disc_vjp = jax.vjp(_discretise, *disc_in)
    same_group = jnp.asarray(np.arange(SSM_W)[:, None] // SSM_GROUP_CH == np.arange(NST)[None, :] // SSM_STATE)

    def block_diag(t):
        return jnp.where(same_group, jnp.tile(t, (G, 1)), 0.0)

    bd = jnp.concatenate([block_diag(bb.transpose(2, 0, 1).reshape(SSM_GROUP_CH, NST)) for bb in (bbar_re, bbar_im)],
                         axis=1)
    cd = jnp.concatenate([block_diag(cc.transpose(1, 0, 2).reshape(SSM_GROUP_CH, NST)).T
                          for cc in (ssm_c_re[0], -ssm_c_im[0])], axis=0)
    a_fwd = jnp.stack([abar_re.reshape(1, NST), abar_im.reshape(1, NST)])
    a_bwd = jnp.stack([abar_re.reshape(1, NST), -abar_im.reshape(1, NST)])
    d_row = ssm_d

    buckets = _t5_buckets_block()
    onehot_t = (jnp.arange(128, dtype=jnp.int32)[:, None] == jnp.asarray(buckets.reshape(1, -1))).astype(BF16)
    rb_hi = rel_bias.astype(BF16)
    rb_lo = (rel_bias - rb_hi.astype(F32)).astype(BF16)
    rb_lo2 = (rel_bias - rb_hi.astype(F32) - rb_lo.astype(F32)).astype(BF16)
    rb3 = jnp.pad(jnp.concatenate([rb_hi.T, rb_lo.T, rb_lo2.T], axis=0), ((0, 0), (0, 128 - NUM_BUCKETS)))
    b3 = _mm(rb3, onehot_t, "NN", name="rel_bias_rows", M=3 * N_Q_HEADS, N=BLOCK * 2 * BLOCK, K=128, tj=4096)
    bias = (b3[:N_Q_HEADS] + b3[N_Q_HEADS:2 * N_Q_HEADS]) + b3[2 * N_Q_HEADS:]
    bias = bias.reshape(N_Q_HEADS, BLOCK, 2 * BLOCK)
    sinks_b = jnp.broadcast_to(attn_sinks[0][:, None, None], (N_Q_HEADS, BLOCK, 128)).reshape(N_Q_HEADS * BLOCK, 128)

    def two(fn):
        def both(*blocks):
            r = fn(*blocks)
            return r, r
        return both

    h1, h1_t = _rowwise(two(_norm_mod), [(xv, "tile", D), (tied(tied(norm1_g, token_first), token_rest), "row", D),
                                         (sh1, "row", D), (sc1, "row", D)],
                        [(D, BF16), (D, BF16, "T")], [], name="norm1", rows=S)
    finish_gather(0, all_of(h1, bd, cd, a_fwd, a_bwd, bias, sinks_b))
    proj = _mm(h1, wop("w_in"), "NN", name="proj", M=S, N=INW, K=D,
               epilogue=lambda acc, b: (acc + b,), extras=[(b_in, "row")])

    def heads(v2d, nh):
        return v2d.reshape(S, nh, HEAD_DIM).transpose(1, 0, 2)

    def unheads(v3d):
        return v3d.transpose(1, 0, 2).reshape(S, -1)

    qh = heads(proj[:, o_q:o_k], N_Q_HEADS)
    kh = heads(proj[:, o_k:o_v], N_KV_HEADS)
    vh = heads(proj[:, o_v:o_u], N_KV_HEADS)
    attn = unheads(_attn_fwd(qh, kh, vh, sinks_b, bias, "attn_fwd"))
    finish_gather(1, attn)
    y_attn = _mm(attn, wop("w_attn_proj"), "NN", name="attn_proj", M=S, N=D, K=ATTN_WIDTH, out_dtypes=(BF16,))

    u = proj[:, o_u:o_ga]
    u_il = _interleave(u, SCAN_CHUNKS)
    SB = 128
    nsb, gpb = SSM_W // SB, SB // SSM_GROUP_CH
    SBN = gpb * SSM_STATE
    y_il, xs = _ssm_fwd(u_il, bd, cd, a_fwd, d_row, name="ssm_fwd", sb=SB, sbn=SBN)
    y = _deinterleave(y_il, SCAN_CHUNKS)
    z, t_glu = _mm(y, wop("w_glu"), "NN", name="glu", M=S, N=SSM_W, K=SSM_W, out_dtypes=(BF16, F32), a_fn=_gelu,
                   epilogue=lambda acc, b, yy: (_gelu(yy) * _sigmoid(acc + b), acc + b),
                   extras=[(b_glu, "row"), (y, "tile")])
    y_ssm = _mm(z, wop("w_ssm_proj"), "NN", name="ssm_proj", M=S, N=D, K=SSM_W, out_dtypes=(BF16,))

    ff_bufs = _gather_wait([in_flight[k] for k in gather_groups[2]], gather_sems[2][0], gather_sems[2][1], all_of(y_ssm),
                           "gather_wait_2")
    ff_bufs, ff_send, ff_recv, token = _copies_start(ff_bufs, _forward_copies, 3 * len(ff_bufs), nothing,
                                                    "gather_forward_2_start")
    merged, merged_t = _rowwise(two(_merge), [(_Op(proj, coff=o_ga), "tile", D), (_Op(proj, coff=o_gs), "tile", D),
                                              (y_attn, "tile", D), (y_ssm, "tile", D)],
                                [(D, BF16), (D, BF16, "T")], [], name="merge", rows=S, deps=[token])
    mo, x2 = _mm(merged, wop("w_out"), "NN", name="out_proj", M=S, N=D, K=D, out_dtypes=(BF16, F32),
                 epilogue=lambda acc, xx, gg: (acc, xx + gg * acc), extras=[(xv, "tile"), (g1, "row")])
    h2, h2_t = _rowwise(two(_norm_mod), [(x2, "tile", D), (norm2_g, "row", D), (sh2, "row", D), (sc2, "row", D)],
                        [(D, BF16), (D, BF16, "T")], [], name="norm2", rows=S)
    gathered.update(zip(gather_groups[2], _copies_wait(ff_bufs, _forward_copies, ff_send, ff_recv, h2,
                                                       "gather_forward_2_wait")))
    a_b, r_b = _mm(h2, wop("w_ff1"), "NN", name="ff1", M=S, N=DFF, K=D, out_dtypes=(BF16, BF16),
                   epilogue=lambda acc: (acc, jnp.square(jnp.maximum(acc, 0.0))))
    ff, x3 = _mm(r_b, wop("w_ff2"), "NN", name="ff2", M=S, N=D, K=DFF, out_dtypes=(BF16, F32),
                 epilogue=lambda acc, xx, gg: (acc, xx + gg * acc), extras=[(x2, "tile"), (g2, "row")],
                 tj=1024, tk=1024)

    def final_fn(x3b, gf, tb, ffb, g2b):
        def f(xx, gg):
            yv = xx * lax.rsqrt(jnp.mean(xx * xx, axis=-1, keepdims=True) + EPS) * gg
            err = jnp.square(yv - tb)
            return 0.5 * jnp.sum(jnp.mean(err, axis=-1, keepdims=True), axis=0, keepdims=True)
        lv, vjp = jax.vjp(f, x3b, gf)
        dx, dg = vjp(jnp.ones((1, 1), F32))
        return dx, dx * g2b, dg, jnp.broadcast_to(lv, (1, 128)), jnp.sum(dx * ffb, axis=0, keepdims=True)

    dx3, dff, g_final, loss_acc, d_g2 = _rowwise(
        final_fn, [(x3, "tile", D), (final_g.reshape(1, D), "row", D), (tgt, "tile", D), (ff, "tile", D), (g2, "row", D)],
        [(D, F32), (D, BF16)], [D, 128, D], name="final", rows=S)
    da = _mm(dff, wop("w_ff2"), "NT", name="ff2_dx", M=S, N=DFF, K=D, out_dtypes=(BF16,),
             epilogue=lambda acc, ab: (acc * (2.0 * jnp.maximum(ab.astype(F32), 0.0)),), extras=[(a_b, "tile")])
    g_w_ff2 = _mm(r_b, dff, "TN", name="ff2_dw", M=DFF, N=D, K=S, out_dtypes=(BF16,), tj=1024, tk=1024)
    g_w_ff1 = _mm(h2_t, da, "NN", name="ff1_dw", M=D, N=DFF, K=S, out_dtypes=(BF16,), out_nsh=N_CHIPS, tj=1024, tk=1024)
    rs_ff, token = rs_swap("ff", dict(w_ff2=g_w_ff2, w_ff1=g_w_ff1))
    dh2 = _mm(da, wop("w_ff1"), "NT", name="ff1_dx", M=S, N=D, K=DFF, tj=1024, tk=1024, deps=[token])
    rs_ff, token_ff = rs_scatter("ff", rs_ff, dh2)

    def norm2_bwd(x2b, dh2b, dx3b, mob, gn, shb, scb, g1b):
        _, vjp = jax.vjp(_norm_mod, x2b, gn, shb, scb)
        dx, dg, dsh, dsc = vjp(dh2b)
        dx2b = dx + dx3b
        return dx2b, dx2b * g1b, dg, dsh, dsc, jnp.sum(dx2b * mob, axis=0, keepdims=True)

    dx2, dmo, g_norm2, d_sh2, d_sc2, d_g1 = _rowwise(
        norm2_bwd, [(x2, "tile", D), (dh2, "tile", D), (dx3, "tile", D), (mo, "tile", D),
                    (tied(norm2_g, token_ff), "row", D), (sh2, "row", D), (sc2, "row", D), (g1, "row", D)],
        [(D, F32), (D, BF16)], [D, D, D, D], name="norm2_bwd", rows=S, tr=128)
    dmerged = _mm(dmo, wop("w_out"), "NT", name="out_dx", M=S, N=D, K=D)
    g_w_out = _mm(merged_t, dmo, "NN", name="out_dw", M=D, N=D, K=S, out_dtypes=(BF16,), tk=1024)

    def merge_bwd(gab, gsb, yab, ysb, dmb):
        _, vjp = jax.vjp(_merge, gab, gsb, yab, ysb)
        return vjp(dmb)

    d_ga, d_gs, dy_attn, dy_ssm = _rowwise(
        merge_bwd, [(_Op(proj, coff=o_ga), "tile", D), (_Op(proj, coff=o_gs), "tile", D), (y_attn, "tile", D),
                    (y_ssm, "tile", D), (dmerged, "tile", D)],
        [(D, BF16), (D, BF16), (D, BF16), (D, BF16)], [], name="merge_bwd", rows=S, tr=128)

    dattn = _mm(dy_attn, wop("w_attn_proj"), "NT", name="attn_proj_dx", M=S, N=ATTN_WIDTH, K=D)
    g_w_attn_proj = _mm(attn, dy_attn, "TN", name="attn_proj_dw", M=ATTN_WIDTH, N=D, K=S, out_dtypes=(BF16,),
                        out_nsh=N_CHIPS, tk=1024)

    dz = _mm(dy_ssm, wop("w_ssm_proj"), "NT", name="ssm_proj_dx", M=S, N=SSM_W, K=D)
    g_w_ssm_proj = _mm(z, dy_ssm, "TN", name="ssm_proj_dw", M=SSM_W, N=D, K=S, out_dtypes=(BF16,),
                       out_nsh=N_CHIPS, tk=1024)

    def glu_bwd(dzb, yb, tb):
        z0 = _gelu(yb)
        sg = _sigmoid(tb)
        dt = dzb * z0 * sg * (1.0 - sg)
        return dt, dzb * sg, jnp.sum(dt, axis=0, keepdims=True)

    dt_b, dz0a, g_b_glu = _rowwise(glu_bwd, [(dz, "tile", SSM_W), (y, "tile", SSM_W), (t_glu, "tile", SSM_W)],
                                   [(SSM_W, BF16), (SSM_W, F32)], [SSM_W], name="glu_bwd", rows=S)

    def gelu_bwd(acc, dz0ab, yb):
        _, vjp = jax.vjp(_gelu, yb)
        return (vjp(acc + dz0ab)[0],)

    dy = _mm(dt_b, wop("w_glu"), "NT", name="glu_dx", M=S, N=SSM_W, K=SSM_W, epilogue=gelu_bwd,
             extras=[(dz0a, "tile"), (y, "tile")])
    g_w_glu = _mm(y, dt_b, "TN", name="glu_dw", M=SSM_W, N=SSM_W, K=S, out_dtypes=(BF16,), tk=1024, a_fn=_gelu)
    rs_mix, token = rs_swap("mix", dict(w_out=g_w_out, w_attn_proj=g_w_attn_proj, w_ssm_proj=g_w_ssm_proj,
                                        w_glu=g_w_glu))
    dy_il = _interleave(tied(dy, token), SCAN_CHUNKS)
    du_il, g_bd, g_cd, d_abar, g_ssm_d = _ssm_bwd(dy_il, u_il, xs, bd, cd, a_bwd, d_row, name="ssm_bwd", sb=SB, sbn=SBN)
    du = _deinterleave(du_il, SCAN_CHUNKS)
    rs_mix, token_mix = rs_scatter("mix", rs_mix, du_il)

    dqh, dkh, dvh, dsink_blk, dbias = _attn_bwd(qh, kh, vh, heads(dattn, N_Q_HEADS), tied(sinks_b, token_mix), bias,
                                                "attn_bwd")
    g_sinks = _sum_lead(dsink_blk.reshape(N_Q_HEADS, BLOCK, 128).transpose(1, 0, 2), "sinks_dw")[:, 0].reshape(1, N_Q_HEADS)
    g_rel = _mm(dbias.reshape(N_Q_HEADS, -1), onehot_t, "NT", name="rel_bias_dw", M=N_Q_HEADS, N=128,
                K=BLOCK * 2 * BLOCK, tk=4096)
    g_rel_bias = g_rel[:, :NUM_BUCKETS].T

    eye_b = jnp.eye(gpb, dtype=F32)
    g_cd6 = g_cd.reshape(2, nsb, gpb, SSM_STATE, gpb, SSM_GROUP_CH)
    g_c_re = jnp.einsum("bgnhp,gh->bgpn", g_cd6[0], eye_b).reshape(G, SSM_GROUP_CH, SSM_STATE)
    g_c_im = -jnp.einsum("bgnhp,gh->bgpn", g_cd6[1], eye_b).reshape(G, SSM_GROUP_CH, SSM_STATE)
    g_bd6 = g_bd.reshape(nsb, gpb, SSM_GROUP_CH, 2, gpb, SSM_STATE)
    g_bbar = jnp.einsum("bhprgn,hg->rbhnp", g_bd6, eye_b).reshape(2, G, SSM_STATE, SSM_GROUP_CH)
    g_bbar_re, g_bbar_im = g_bbar[0], g_bbar[1]
    g_lre, g_lim, g_lstep, g_bre, g_bim = disc_vjp(
        (d_abar[0].reshape(G, SSM_STATE), d_abar[1].reshape(G, SSM_STATE), g_bbar_re, g_bbar_im))

    dproj = jnp.concatenate([unheads(dqh).astype(BF16), unheads(dkh).astype(BF16), unheads(dvh).astype(BF16),
                             du.astype(BF16), d_ga, d_gs], axis=1)
    g_w_in = _mm(h1_t, dproj, "NN", name="proj_dw", M=D, N=INW, K=S, out_dtypes=(BF16,), out_nsh=N_CHIPS,
                 tj=INW // (2 * N_CHIPS), tk=1024)
    rs_in, token = rs_swap("in", dict(w_in=g_w_in))
    dh1 = _mm(dproj, wop("w_in"), "NT", name="proj_dx", M=S, N=D, K=INW, tj=1024, tk=INW // N_CHIPS, deps=[token])
    g_b_in = _rowwise(lambda d: (jnp.sum(d.astype(F32), axis=0, keepdims=True),), [(dproj, "tile", INW)], [], [INW],
                      name="proj_db", rows=S)[0]

    def norm1_bwd(xb, dhb, dresb, gn, shb, scb):
        _, vjp = jax.vjp(_norm_mod, xb, gn, shb, scb)
        dx, dg, dsh, dsc = vjp(dhb)
        return dx + dresb, dg, dsh, dsc

    grad_x, g_norm1, d_sh1, d_sc1 = _rowwise(
        norm1_bwd, [(xv, "tile", D), (dh1, "tile", D), (dx2, "tile", D), (norm1_g, "row", D),
                    (sh1, "row", D),
                    (sc1, "row", D)], [(D, F32)], [D, D, D], name="norm1_bwd", rows=S)

    dmod_row = jnp.concatenate([d_sh1, d_sc1, d_g1, d_sh2, d_sc2, d_g2], axis=1)
    dmod_all = _allgather8(jnp.pad(dmod_row, ((0, 7), (0, 0))), "gather_dmod").reshape(N_DEV, 8, -1)[:, 0]
    g_b_ada = _sum_lead(dmod_all.reshape(N_DEV, -1, 128), "b_ada_dw").reshape(1, -1)
    dmod_mine = lax.dynamic_slice(dmod_all.reshape(N_DEV, N_CHIPS, -1), (0, my_chip, 0), (N_DEV, 1, w_ada.shape[2]))[:, 0]
    g_w_ada = _mm(c16, jnp.pad(dmod_mine, ((0, 8), (0, 0))), "TN", name="ada_dw", M=D, N=w_ada.shape[2], K=16,
                  a_fn=_silu)

    small_g = dict(norm1_g=g_norm1, b_in=g_b_in, attn_sinks=g_sinks, rel_bias=g_rel_bias, lambda_re=g_lre[None],
                   lambda_im=g_lim[None], log_step=g_lstep[None], ssm_b_re=g_bre[None], ssm_b_im=g_bim[None],
                   ssm_c_re=g_c_re[None], ssm_c_im=g_c_im[None], ssm_d=g_ssm_d, b_glu=g_b_glu, norm2_g=g_norm2,
                   final_g=g_final.reshape(D))
    packed = _pack([loss_acc[:, :1]] + [small_g[k] for k in _SMALL])
    rows = packed.shape[0]
    summed = _sum_lead(_allgather8(packed, "gather_small").reshape(N_DEV, rows, 128), "small_sum")
    small_shapes = [(1,)] + [given[k].shape for k in _SMALL]
    parts = _unpack(summed, small_shapes)
    loss = parts[0].reshape(())
    grads.update(zip(_SMALL, parts[1:]))
    grads["b_ada"] = g_b_ada
    grads["w_ada"] = g_w_ada[None]

    deltas, new_m, new_v = {}, {}, {}

    def adamw_big(k, deps=()):
        d_, m_, v_ = _adamw(given[k][0], grads[k][0], given["m_" + k][0], given["v_" + k][0], "adamw_" + k, deps)
        deltas[k], new_m[k], new_v[k] = d_[None], m_[None], v_[None]
        return v_

    rs_in, token_in = rs_scatter("in", rs_in, all_of(summed, dmod_all))
    rs_ff, token = rs_sum("ff", rs_ff, all_of(summed, token_in))
    mark = adamw_big("w_ada", [token])
    rs_mix, token = rs_sum("mix", rs_mix, mark)
    small_all = list(_SMALL) + ["b_ada"]
    shapes = [given[k].shape for k in small_all]
    pw, pg = _pack([given[k] for k in small_all]), _pack([grads[k] for k in small_all])
    pm, pv = _pack([given["m_" + k] for k in small_all]), _pack([given["v_" + k] for k in small_all])
    d_, m_, v_ = _adamw(pw, pg, pm, pv, "adamw_small", [token])
    for k, dd, mm, vv in zip(small_all, _unpack(d_, shapes), _unpack(m_, shapes), _unpack(v_, shapes)):
        deltas[k], new_m[k], new_v[k] = dd, mm, vv
        grads[k] = grads[k].reshape(given[k].shape)
    rs_finish("ff", rs_ff, v_)
    marks = [adamw_big(k) for k in ("w_ff2", "w_ff1")]
    rs_finish("mix", rs_mix, all_of(*marks))
    marks = [adamw_big(k) for k in ("w_out", "w_attn_proj", "w_ssm_proj", "w_glu")]
    rs_in, token = rs_sum("in", rs_in, all_of(*marks))
    rs_finish("in", rs_in, token)
    adamw_big("w_in")

    names = ["w_ada", "b_ada", "norm1_g", "w_in", "b_in", "attn_sinks", "rel_bias", "lambda_re", "lambda_im",
             "log_step", "ssm_b_re", "ssm_b_im", "ssm_c_re", "ssm_c_im", "ssm_d", "w_glu", "b_glu", "w_attn_proj",
             "w_ssm_proj", "w_out", "norm2_g", "w_ff1", "w_ff2", "final_g"]
    return (loss, grad_x[None], *[grads[n] for n in names], *[deltas[n] for n in names],
            *[new_m[n] for n in names], *[new_v[n] for n in names])
```

```python
import math

import numpy as np
import jax
import jax.numpy as jnp
from jax import lax
from jax.experimental import pallas as pl
from jax.experimental.pallas import tpu as pltpu

F32 = jnp.float32
BF16 = jnp.bfloat16
MESH = pl.DeviceIdType.MESH

HEAD_DIM = 64
N_Q_HEADS = 16
N_KV_HEADS = 4
GQA_GROUP = N_Q_HEADS // N_KV_HEADS
ATTN_WIDTH = N_Q_HEADS * HEAD_DIM
KV_WIDTH = N_KV_HEADS * HEAD_DIM
BLOCK = 128
NUM_BUCKETS = 32
MAX_DISTANCE = 128
NEG_INF = -1e30
SSM_GROUP_CH = 16
SSM_STATE = 64
EPS = 1e-6
ADAM_LR = 0.001
ADAM_B1 = 0.9
ADAM_B2 = 0.999
ADAM_EPS = 1e-08
ADAM_WD = 0.01
ADAM_STEP = 10

N_CHIPS = 4
N_DEV = 8
SCAN_CHUNKS = 8
VMEM_LIMIT_BYTES = 48 * 1024 * 1024
SSM_VMEM_LIMIT_BYTES = 56 * 1024 * 1024


def _cparams(sem=None):
    return pltpu.CompilerParams(dimension_semantics=sem, vmem_limit_bytes=VMEM_LIMIT_BYTES)


class _Op:
    def __init__(self, arr, nsh=None, coff=0):
        self.arr, self.nsh, self.coff = arr, nsh, coff
        if nsh is None:
            self.rows, self.cols = arr.shape
        else:
            assert arr.shape[0] == nsh
            self.rows, self.cols = arr.shape[1], arr.shape[2] * nsh

    def spec(self, br, bc, idx):
        assert self.coff % bc == 0
        off = self.coff // bc
        if self.nsh is None:
            return pl.BlockSpec((br, bc), lambda *g: (idx(*g)[0], idx(*g)[1] + off))
        per = (self.cols // self.nsh) // bc
        assert per * bc * self.nsh == self.cols

        def imap(*g):
            r, c = idx(*g)
            c = c + off
            return (c // per, r, c % per)
        return pl.BlockSpec((None, br, bc), imap)


def _as_op(a):
    return a if isinstance(a, _Op) else _Op(a)


def _mm(a, b, mode, *, name, M, N, K, out_dtypes=(F32,), out_nsh=None, epilogue=None, extras=(),
        a_fn=None, ti=1024, tj=512, tk=2048, deps=()):
    a_idx = b_idx = None
    nd = len(deps)
    a, b = _as_op(a), _as_op(b)
    ti, tj, tk = min(ti, M), min(tj, N), min(tk, K)
    a_w = a.cols // a.nsh if a.nsh else None
    b_w = b.cols // b.nsh if b.nsh else None
    if a_w:
        ti, tk = (min(ti, a_w), tk) if mode == "TN" else (ti, min(tk, a_w))
    if b_w:
        tj, tk = (tj, min(tk, b_w)) if mode == "NT" else (min(tj, b_w), tk)
    if out_nsh:
        tj = min(tj, N // out_nsh)
    assert M % ti == 0 and N % tj == 0 and K % tk == 0, (name, M, N, K, ti, tj, tk)
    nk = K // tk
    if mode == "NN":
        a_spec = a.spec(ti, tk, a_idx or (lambda i, j, k: (i, k)))
        b_spec = b.spec(tk, tj, b_idx or (lambda i, j, k: (k, j)))
        dims = (((1,), (0,)), ((), ()))
    elif mode == "NT":
        a_spec = a.spec(ti, tk, a_idx or (lambda i, j, k: (i, k)))
        b_spec = b.spec(tj, tk, b_idx or (lambda i, j, k: (j, k)))
        dims = (((1,), (1,)), ((), ()))
    else:
        a_spec = a.spec(tk, ti, a_idx or (lambda i, j, k: (k, i)))
        b_spec = b.spec(tk, tj, b_idx or (lambda i, j, k: (k, j)))
        dims = (((0,), (0,)), ((), ()))
    ex_specs, ex_arrs = [], []
    for op, kind in extras:
        op = _as_op(op)
        if kind == "tile":
            ex_specs.append(op.spec(ti, tj, lambda i, j, k: (i, j)))
        else:
            ex_specs.append(op.spec(1, tj, lambda i, j, k: (0, j)))
        ex_arrs.append(op.arr)
    ne, no = len(ex_arrs), len(out_dtypes)
    if out_nsh is None:
        out_shapes = [jax.ShapeDtypeStruct((M, N), d) for d in out_dtypes]
        out_specs = [pl.BlockSpec((ti, tj), lambda i, j, k: (i, j)) for _ in out_dtypes]
    else:
        per = (N // out_nsh) // tj
        assert per * tj * out_nsh == N
        out_shapes = [jax.ShapeDtypeStruct((out_nsh, M, N // out_nsh), d) for d in out_dtypes]
        out_specs = [pl.BlockSpec((None, ti, tj), lambda i, j, k: (j // per, i, j % per)) for _ in out_dtypes]

    def body(a_ref, b_ref, *rest):
        ex_refs, out_refs, acc = rest[:ne], rest[ne + nd:ne + nd + no], rest[ne + nd + no]
        k = pl.program_id(2)

        @pl.when(k == 0)
        def _():
            acc[...] = jnp.zeros_like(acc)

        av = a_ref[...]
        if a_fn is not None:
            av = a_fn(av)
        acc[...] += lax.dot_general(av.astype(BF16), b_ref[...].astype(BF16), dims,
                                    preferred_element_type=F32)

        @pl.when(k == nk - 1)
        def _():
            res = acc[...]
            outs = epilogue(res, *[r[...] for r in ex_refs]) if epilogue is not None else (res,)
            for o_ref, o in zip(out_refs, outs):
                o_ref[...] = o.astype(o_ref.dtype)

    outs = pl.pallas_call(
        body, name=name, grid=(M // ti, N // tj, nk),
        in_specs=[a_spec, b_spec] + ex_specs + [pl.BlockSpec(memory_space=pl.ANY)] * nd,
        out_specs=out_specs, out_shape=out_shapes,
        scratch_shapes=[pltpu.VMEM((ti, tj), F32)],
        compiler_params=_cparams(("parallel", "parallel", "arbitrary")),
    )(a.arr, b.arr, *ex_arrs, *deps)
    return outs[0] if no == 1 else outs


def _rowwise(fn, ins, outs, accs, *, name, rows, tr=256, deps=()):
    tr = min(tr, rows)
    assert rows % tr == 0
    in_specs, arrs = [], []
    for op, kind, width in ins:
        op = _as_op(op)
        if kind == "tile":
            in_specs.append(op.spec(tr, width, lambda i: (i, 0)))
        else:
            in_specs.append(op.spec(op.rows, width, lambda i: (0, 0)))
        arrs.append(op.arr)
    ni, no, na = len(ins), len(outs), len(accs)
    flipped = [len(o) == 3 for o in outs]
    out_shapes = [jax.ShapeDtypeStruct((o[0], rows) if t else (rows, o[0]), o[1]) for o, t in zip(outs, flipped)]
    out_specs = [pl.BlockSpec((o[0], tr), lambda i: (0, i)) if t else pl.BlockSpec((tr, o[0]), lambda i: (i, 0))
                 for o, t in zip(outs, flipped)]
    out_shapes += [jax.ShapeDtypeStruct((1, w), F32) for w in accs]
    out_specs += [pl.BlockSpec((1, w), lambda i: (0, 0)) for w in accs]

    def body(*refs):
        nd = len(deps)
        in_refs, out_refs, acc_refs = refs[:ni], refs[ni + nd:ni + nd + no], refs[ni + nd + no:]
        res = fn(*[r[...] for r in in_refs])
        if not isinstance(res, (tuple, list)):
            res = (res,)
        for o_ref, r, t in zip(out_refs, res[:no], flipped):
            o_ref[...] = (r.astype(F32).T if t else r).astype(o_ref.dtype)
        if na:
            @pl.when(pl.program_id(0) == 0)
            def _():
                for a_ref in acc_refs:
                    a_ref[...] = jnp.zeros_like(a_ref)
            for a_ref, r in zip(acc_refs, res[no:]):
                a_ref[...] += r.astype(F32)

    res = pl.pallas_call(
        body, name=name, grid=(rows // tr,), in_specs=in_specs + [pl.BlockSpec(memory_space=pl.ANY)] * len(deps),
        out_specs=out_specs, out_shape=out_shapes, compiler_params=_cparams(("arbitrary",)),
    )(*arrs, *deps)
    return res


def _norm_mod(x, g, sh, sc):
    y = x * lax.rsqrt(jnp.mean(x * x, axis=-1, keepdims=True) + EPS) * g
    return y * (1.0 + sc) + sh


def _sigmoid(x):
    return 1.0 / (1.0 + jnp.exp(-x))


def _silu(x):
    return x * _sigmoid(x)


def _gelu(x):
    return 0.5 * x * (1.0 + jnp.tanh(math.sqrt(2.0 / math.pi) * (x + 0.044715 * (x * x * x))))


def _merge(ga, gs, ya, ys):
    ga, gs, ya, ys = (v.astype(F32) for v in (ga, gs, ya, ys))
    return _sigmoid(ga) * ya + _sigmoid(gs) * ys


def _attn_head(q, kp, kc, vp, vc, sink, bias_p, bias_c, not_first):
    nt = (((1,), (1,)), ((), ()))
    nn = (((1,), (0,)), ((), ()))
    qb = q.astype(BF16)
    scale = HEAD_DIM ** -0.5
    sp = lax.dot_general(qb, kp.astype(BF16), nt, preferred_element_type=F32) * scale + bias_p
    sc = lax.dot_general(qb, kc.astype(BF16), nt, preferred_element_type=F32) * scale + bias_c
    qi = lax.broadcasted_iota(jnp.int32, sp.shape, 0) & (BLOCK - 1)
    ki = lax.broadcasted_iota(jnp.int32, sp.shape, 1)
    sp = jnp.where(jnp.logical_and(ki > qi, not_first), sp, NEG_INF)
    sc = jnp.where(ki <= qi, sc, NEG_INF)
    m = jnp.maximum(jnp.maximum(jnp.max(sp, axis=-1, keepdims=True), jnp.max(sc, axis=-1, keepdims=True)), sink)
    m = lax.stop_gradient(m)
    pp = jnp.exp(sp - m)
    pc = jnp.exp(sc - m)
    denom = jnp.sum(pp, axis=-1, keepdims=True) + jnp.sum(pc, axis=-1, keepdims=True) + jnp.exp(sink - m)
    o = lax.dot_general((pp / denom).astype(BF16), vp.astype(BF16), nn, preferred_element_type=F32)
    o = o + lax.dot_general((pc / denom).astype(BF16), vc.astype(BF16), nn, preferred_element_type=F32)
    return o


def _attn_fwd(qh, kh, vh, sinks, bias, name):
    s = qh.shape[1]
    nb = s // BLOCK
    G = GQA_GROUP
    R = G * BLOCK

    def body(q_ref, kp_ref, kc_ref, vp_ref, vc_ref, sink_ref, bias_ref, o_ref):
        not_first = pl.program_id(0) > 0
        for kv in range(N_KV_HEADS):
            hs = slice(kv * G, (kv + 1) * G)
            o = _attn_head(q_ref[hs].reshape(R, HEAD_DIM), kp_ref[kv], kc_ref[kv], vp_ref[kv], vc_ref[kv],
                           sink_ref[kv * R:(kv + 1) * R, 0:1],
                           bias_ref[hs, :, 0:BLOCK].reshape(R, BLOCK), bias_ref[hs, :, BLOCK:2 * BLOCK].reshape(R, BLOCK),
                           not_first)
            o_ref[hs] = o.reshape(G, BLOCK, HEAD_DIM).astype(o_ref.dtype)

    cur = lambda i: (0, i, 0)
    prev = lambda i: (0, jnp.maximum(i - 1, 0), 0)
    return pl.pallas_call(
        body, name=name, grid=(nb,),
        in_specs=[pl.BlockSpec((N_Q_HEADS, BLOCK, HEAD_DIM), cur),
                  pl.BlockSpec((N_KV_HEADS, BLOCK, HEAD_DIM), prev), pl.BlockSpec((N_KV_HEADS, BLOCK, HEAD_DIM), cur),
                  pl.BlockSpec((N_KV_HEADS, BLOCK, HEAD_DIM), prev), pl.BlockSpec((N_KV_HEADS, BLOCK, HEAD_DIM), cur),
                  pl.BlockSpec((N_Q_HEADS * BLOCK, 128), lambda i: (0, 0)),
                  pl.BlockSpec((N_Q_HEADS, BLOCK, 2 * BLOCK), lambda i: (0, 0, 0))],
        out_specs=pl.BlockSpec((N_Q_HEADS, BLOCK, HEAD_DIM), cur),
        out_shape=jax.ShapeDtypeStruct((N_Q_HEADS, s, HEAD_DIM), BF16),
        compiler_params=_cparams(("arbitrary",)),
    )(qh, kh, kh, vh, vh, sinks, bias)


def _attn_bwd(qh, kh, vh, doh, sinks, bias, name):
    s = qh.shape[1]
    nb = s // BLOCK
    G = GQA_GROUP
    R = G * BLOCK

    def body(q_ref, kp_ref, kc_ref, vp_ref, vc_ref, do_ref, sink_ref, bias_ref,
             dq_ref, dk_ref, dv_ref, dsink_ref, dbias_ref, ck, cv):
        i = pl.program_id(1)

        @pl.when(i == 0)
        def _():
            dsink_ref[...] = jnp.zeros_like(dsink_ref)
            dbias_ref[...] = jnp.zeros_like(dbias_ref)
            ck[...] = jnp.zeros_like(ck)
            cv[...] = jnp.zeros_like(cv)

        @pl.when(i < nb)
        def _():
            not_first = i > 0
            _, vjp = jax.vjp(lambda q, a, b, c, d, sk, e, f: _attn_head(q, a, b, c, d, sk, e, f, not_first),
                             q_ref[...].astype(F32).reshape(R, HEAD_DIM), kp_ref[...].astype(F32),
                             kc_ref[...].astype(F32), vp_ref[...].astype(F32), vc_ref[...].astype(F32),
                             sink_ref[:, 0:1], bias_ref[:, :, 0:BLOCK].reshape(R, BLOCK),
                             bias_ref[:, :, BLOCK:2 * BLOCK].reshape(R, BLOCK))
            dq, dkp, dkc, dvp, dvc, dsk, dbp, dbc = vjp(do_ref[...].reshape(R, HEAD_DIM).astype(F32))
            dq_ref[...] = dq.reshape(G, BLOCK, HEAD_DIM)
            dsink_ref[...] += jnp.broadcast_to(dsk, (R, 128))
            dbias_ref[:, :, 0:BLOCK] += dbp.reshape(G, BLOCK, BLOCK)
            dbias_ref[:, :, BLOCK:2 * BLOCK] += dbc.reshape(G, BLOCK, BLOCK)
            dk_ref[...] = ck[...] + dkp
            dv_ref[...] = cv[...] + dvp
            ck[...] = dkc
            cv[...] = dvc

        @pl.when(i == nb)
        def _():
            dk_ref[...] = ck[...]
            dv_ref[...] = cv[...]

    qcur = lambda kv, i: (kv, jnp.minimum(i, nb - 1), 0)
    kcur = lambda kv, i: (kv, jnp.minimum(i, nb - 1), 0)
    kprev = lambda kv, i: (kv, jnp.clip(i - 1, 0, nb - 1), 0)
    qspec = pl.BlockSpec((G, BLOCK, HEAD_DIM), qcur)
    kc_spec = pl.BlockSpec((None, BLOCK, HEAD_DIM), kcur)
    kp_spec = pl.BlockSpec((None, BLOCK, HEAD_DIM), kprev)
    return pl.pallas_call(
        body, name=name, grid=(N_KV_HEADS, nb + 1),
        in_specs=[qspec, kp_spec, kc_spec, kp_spec, kc_spec, qspec,
                  pl.BlockSpec((R, 128), lambda kv, i: (kv, 0)),
                  pl.BlockSpec((G, BLOCK, 2 * BLOCK), lambda kv, i: (kv, 0, 0))],
        out_specs=[qspec, kp_spec, kp_spec,
                   pl.BlockSpec((R, 128), lambda kv, i: (kv, 0)),
                   pl.BlockSpec((G, BLOCK, 2 * BLOCK), lambda kv, i: (kv, 0, 0))],
        out_shape=[jax.ShapeDtypeStruct((N_Q_HEADS, s, HEAD_DIM), F32),
                   jax.ShapeDtypeStruct((N_KV_HEADS, s, HEAD_DIM), F32),
                   jax.ShapeDtypeStruct((N_KV_HEADS, s, HEAD_DIM), F32),
                   jax.ShapeDtypeStruct((N_Q_HEADS * BLOCK, 128), F32),
                   jax.ShapeDtypeStruct((N_Q_HEADS, BLOCK, 2 * BLOCK), F32)],
        scratch_shapes=[pltpu.VMEM((BLOCK, HEAD_DIM), F32), pltpu.VMEM((BLOCK, HEAD_DIM), F32)],
        compiler_params=_cparams(("arbitrary", "arbitrary")),
    )(qh, kh, kh, vh, vh, doh, sinks, bias)


def _cmul(ar, ai, br, bi):
    return ar * br - ai * bi, ar * bi + ai * br


def _scan_passes(a_ref, b_ref, x_ref, xp_ref, da_ref, *, s, tc, reverse):
    nc = SCAN_CHUNKS
    steps = s // nc
    with_da = xp_ref is not None
    unroll = 8 if steps % 8 == 0 else 1

    def shift(v, d):
        row = lax.broadcasted_iota(jnp.int32, v.shape, 0)
        if reverse:
            return jnp.where(row < nc - d, pltpu.roll(v, nc - d, 0), 0.0)
        return jnp.where(row >= d, pltpu.roll(v, d, 0), 0.0)

    def run():
        ar = jnp.broadcast_to(a_ref[0], (nc, tc))
        ai = jnp.broadcast_to(a_ref[1], (nc, tc))

        def row_of(step):
            j = (steps - 1 - step) if reverse else step
            return pl.multiple_of(j * nc, nc)

        def p1(step, st):
            sr, si = st
            r0 = row_of(step)
            mr, mi = _cmul(ar, ai, sr, si)
            sr = mr + b_ref[0, pl.ds(r0, nc), :]
            si = mi + b_ref[1, pl.ds(r0, nc), :]
            x_ref[0, pl.ds(r0, nc), :] = sr
            x_ref[1, pl.ds(r0, nc), :] = si
            return sr, si
        zero = jnp.zeros((nc, tc), F32)
        er, ei = lax.fori_loop(0, steps, p1, (zero, zero), unroll=unroll)

        pr, pi_ = jnp.ones((nc, tc), F32), zero
        br, bi, left = ar, ai, steps
        while left:
            if left & 1:
                pr, pi_ = _cmul(pr, pi_, br, bi)
            br, bi = _cmul(br, bi, br, bi)
            left >>= 1
        cr, ci = shift(er, 1), shift(ei, 1)
        d = 1
        while d < nc:
            mr, mi = _cmul(pr, pi_, shift(cr, d), shift(ci, d))
            cr, ci = cr + mr, ci + mi
            pr, pi_ = _cmul(pr, pi_, pr, pi_)
            d *= 2

        def p2(step, st):
            qr, qi, dar, dai = st
            r0 = row_of(step)
            qr, qi = _cmul(ar, ai, qr, qi)
            fr, fi = _cmul(qr, qi, cr, ci)
            xr = x_ref[0, pl.ds(r0, nc), :] + fr
            xi = x_ref[1, pl.ds(r0, nc), :] + fi
            x_ref[0, pl.ds(r0, nc), :] = xr
            x_ref[1, pl.ds(r0, nc), :] = xi
            if with_da:
                jm = jnp.where(step == steps - 1, steps - 1, steps - 2 - step)
                rp = pl.multiple_of(jm * nc, nc)
                vr, vi = xp_ref[0, pl.ds(rp, nc), :], xp_ref[1, pl.ds(rp, nc), :]
                row = lax.broadcasted_iota(jnp.int32, (nc, tc), 0)
                first = step == steps - 1
                sel = jnp.logical_and(first, row == 0)
                vr = jnp.where(sel, 0.0, jnp.where(first, pltpu.roll(vr, 1, 0), vr))
                vi = jnp.where(sel, 0.0, jnp.where(first, pltpu.roll(vi, 1, 0), vi))
                dar = dar + xr * vr + xi * vi
                dai = dai + xi * vr - xr * vi
            return qr, qi, dar, dai
        _, _, dar, dai = lax.fori_loop(0, steps, p2, (jnp.ones((nc, tc), F32), zero, zero, zero), unroll=unroll)
        if with_da:
            da_ref[0] = jnp.sum(dar, axis=0, keepdims=True)
            da_ref[1] = jnp.sum(dai, axis=0, keepdims=True)

    run()


def _ssm_fwd(u, bd, cd, a, d_row, *, name, sb, sbn):
    s, w = u.shape
    nst = a.shape[2]
    nblk = w // sb
    rows = min(512, s)
    nn = (((1,), (0,)), ((), ()))

    def body(u_ref, bre_ref, bim_ref, cre_ref, cim_ref, a_ref, d_ref, y_ref, x_ref):

        def fill(r, carry):
            r0 = pl.multiple_of(r * rows, rows)
            ub = u_ref[pl.ds(r0, rows), :].astype(BF16)
            x_ref[0, pl.ds(r0, rows), :] = lax.dot_general(ub, bre_ref[...].astype(BF16), nn, preferred_element_type=F32)
            x_ref[1, pl.ds(r0, rows), :] = lax.dot_general(ub, bim_ref[...].astype(BF16), nn, preferred_element_type=F32)
            return carry
        lax.fori_loop(0, s // rows, fill, 0)
        _scan_passes(a_ref, x_ref, x_ref, None, None, s=s, tc=sbn, reverse=False)

        def project(r, carry):
            r0 = pl.multiple_of(r * rows, rows)
            y = lax.dot_general(x_ref[0, pl.ds(r0, rows), :].astype(BF16), cre_ref[...].astype(BF16), nn, preferred_element_type=F32)
            y = y + lax.dot_general(x_ref[1, pl.ds(r0, rows), :].astype(BF16), cim_ref[...].astype(BF16), nn, preferred_element_type=F32)
            y_ref[pl.ds(r0, rows), :] = y + d_ref[...] * u_ref[pl.ds(r0, rows), :]
            return carry
        lax.fori_loop(0, s // rows, project, 0)

    return pl.pallas_call(
        body, name=name, grid=(nblk,),
        in_specs=[pl.BlockSpec((s, sb), lambda j: (0, j)),
                  pl.BlockSpec((sb, sbn), lambda j: (j, j)), pl.BlockSpec((sb, sbn), lambda j: (j, nblk + j)),
                  pl.BlockSpec((sbn, sb), lambda j: (j, j)), pl.BlockSpec((sbn, sb), lambda j: (nblk + j, j)),
                  pl.BlockSpec((2, 1, sbn), lambda j: (0, 0, j)), pl.BlockSpec((1, sb), lambda j: (0, j))],
        out_specs=[pl.BlockSpec((s, sb), lambda j: (0, j)), pl.BlockSpec((2, s, sbn), lambda j: (0, 0, j))],
        out_shape=[jax.ShapeDtypeStruct((s, w), F32), jax.ShapeDtypeStruct((2, s, nst), F32)],
        compiler_params=pltpu.CompilerParams(dimension_semantics=("arbitrary",), vmem_limit_bytes=SSM_VMEM_LIMIT_BYTES),
    )(u, bd, bd, cd, cd, a, d_row)


def _ssm_bwd(dy, u, xs, bd, cd, a, d_row, *, name, sb, sbn):
    s, w = u.shape
    nst = a.shape[2]
    nblk = w // sb
    rows = min(512, s)
    nt = (((1,), (1,)), ((), ()))
    tn = (((0,), (0,)), ((), ()))

    def body(dy_ref, u_ref, xs_hbm, bre_ref, bim_ref, cre_ref, cim_ref, a_ref, d_ref,
             du_ref, gb_ref, gc_ref, da_ref, gd_ref, lam, xs_ref, sem):
        j = pl.program_id(0)
        fetch = pltpu.make_async_copy(xs_hbm.at[:, :, pl.ds(pl.multiple_of(j * sbn, sbn), sbn)], xs_ref, sem)
        fetch.start()

        def fill(r, carry):
            r0 = pl.multiple_of(r * rows, rows)
            dyb = dy_ref[pl.ds(r0, rows), :].astype(BF16)
            lam[0, pl.ds(r0, rows), :] = lax.dot_general(dyb, cre_ref[...].astype(BF16), nt, preferred_element_type=F32)
            lam[1, pl.ds(r0, rows), :] = lax.dot_general(dyb, cim_ref[...].astype(BF16), nt, preferred_element_type=F32)
            return carry
        lax.fori_loop(0, s // rows, fill, 0)
        fetch.wait()
        _scan_passes(a_ref, lam, lam, xs_ref, da_ref, s=s, tc=sbn, reverse=True)
        gb_ref[...] = jnp.zeros_like(gb_ref)
        gc_ref[...] = jnp.zeros_like(gc_ref)
        gd_ref[...] = jnp.zeros_like(gd_ref)

        def project(r, carry):
            r0 = pl.multiple_of(r * rows, rows)
            dyv, uv = dy_ref[pl.ds(r0, rows), :], u_ref[pl.ds(r0, rows), :]
            dyb, ub = dyv.astype(BF16), uv.astype(BF16)
            lr, li = lam[0, pl.ds(r0, rows), :].astype(BF16), lam[1, pl.ds(r0, rows), :].astype(BF16)
            du = lax.dot_general(lr, bre_ref[...].astype(BF16), nt, preferred_element_type=F32)
            du = du + lax.dot_general(li, bim_ref[...].astype(BF16), nt, preferred_element_type=F32)
            du_ref[pl.ds(r0, rows), :] = du + d_ref[...] * dyv
            gb_ref[:, 0:sbn] += lax.dot_general(ub, lr, tn, preferred_element_type=F32)
            gb_ref[:, sbn:2 * sbn] += lax.dot_general(ub, li, tn, preferred_element_type=F32)
            gc_ref[0] += lax.dot_general(xs_ref[0, pl.ds(r0, rows), :].astype(BF16), dyb, tn, preferred_element_type=F32)
            gc_ref[1] += lax.dot_general(xs_ref[1, pl.ds(r0, rows), :].astype(BF16), dyb, tn, preferred_element_type=F32)
            gd_ref[...] += jnp.sum(dyv * uv, axis=0, keepdims=True)
            return carry
        lax.fori_loop(0, s // rows, project, 0)

    col = lambda j: (0, j)
    return pl.pallas_call(
        body, name=name, grid=(nblk,),
        in_specs=[pl.BlockSpec((s, sb), col), pl.BlockSpec((s, sb), col), pl.BlockSpec(memory_space=pl.ANY),
                  pl.BlockSpec((sb, sbn), lambda j: (j, j)), pl.BlockSpec((sb, sbn), lambda j: (j, nblk + j)),
                  pl.BlockSpec((sbn, sb), lambda j: (j, j)), pl.BlockSpec((sbn, sb), lambda j: (nblk + j, j)),
                  pl.BlockSpec((2, 1, sbn), lambda j: (0, 0, j)), pl.BlockSpec((1, sb), col)],
        out_specs=[pl.BlockSpec((s, sb), col), pl.BlockSpec((sb, 2 * sbn), lambda j: (j, 0)),
                   pl.BlockSpec((2, sbn, sb), lambda j: (0, j, 0)), pl.BlockSpec((2, 1, sbn), lambda j: (0, 0, j)),
                   pl.BlockSpec((1, sb), col)],
        out_shape=[jax.ShapeDtypeStruct((s, w), F32), jax.ShapeDtypeStruct((w, 2 * sbn), F32),
                   jax.ShapeDtypeStruct((2, nst, sb), F32), jax.ShapeDtypeStruct((2, 1, nst), F32),
                   jax.ShapeDtypeStruct((1, w), F32)],
        scratch_shapes=[pltpu.VMEM((2, s, sbn), F32), pltpu.VMEM((2, s, sbn), F32), pltpu.SemaphoreType.DMA],
        compiler_params=pltpu.CompilerParams(dimension_semantics=("arbitrary",), vmem_limit_bytes=SSM_VMEM_LIMIT_BYTES),
    )(dy, u, xs, bd, bd, cd, cd, a, d_row)


def _adamw(w, g, m, v, name, deps=()):
    nd = len(deps)
    r, c = w.shape
    tr = r
    for cand in (512, 256, 128, 64, 32, 16, 8):
        if r % cand == 0 and cand * c * 4 <= 2 * 1024 * 1024:
            tr = cand
            break

    def body(w_ref, g_ref, m_ref, v_ref, *rest):
        d_ref, nm_ref, nv_ref = rest[nd:]
        gv = g_ref[...]
        nm = ADAM_B1 * m_ref[...] + (1.0 - ADAM_B1) * gv
        nv = ADAM_B2 * v_ref[...] + (1.0 - ADAM_B2) * (gv * gv)
        m_hat = nm / (1.0 - ADAM_B1 ** ADAM_STEP)
        v_hat = nv / (1.0 - ADAM_B2 ** ADAM_STEP)
        d_ref[...] = -ADAM_LR * (m_hat / (jnp.sqrt(v_hat) + ADAM_EPS) + ADAM_WD * w_ref[...])
        nm_ref[...] = nm
        nv_ref[...] = nv

    spec = pl.BlockSpec((tr, c), lambda i: (i, 0))
    sds = jax.ShapeDtypeStruct((r, c), F32)
    return pl.pallas_call(body, name=name, grid=(r // tr,),
                          in_specs=[spec] * 4 + [pl.BlockSpec(memory_space=pl.ANY)] * nd, out_specs=[spec] * 3,
                          out_shape=[sds] * 3, compiler_params=_cparams(("parallel",)))(w, g, m, v, *deps)


def _sum_lead(x, name, out_dtype=F32):
    n, r, c = x.shape
    tr = r
    for cand in (512, 256, 128, 64, 32, 16, 8):
        if r % cand == 0 and n * cand * c * 4 <= 4 * 1024 * 1024:
            tr = cand
            break

    def body(x_ref, o_ref):
        acc = x_ref[0].astype(F32)
        for k in range(1, n):
            acc = acc + x_ref[k].astype(F32)
        o_ref[...] = acc.astype(o_ref.dtype)

    return pl.pallas_call(body, name=name, grid=(r // tr,),
                          in_specs=[pl.BlockSpec((n, tr, c), lambda i: (0, i, 0))],
                          out_specs=pl.BlockSpec((tr, c), lambda i: (i, 0)),
                          out_shape=jax.ShapeDtypeStruct((r, c), out_dtype),
                          compiler_params=_cparams(("parallel",)))(x)


def _row_tile(rows, row_bytes, budget, least=8):
    for cand in (1024, 512, 256, 128, 64, 32, 16, 8):
        if cand >= least and rows % cand == 0 and cand * row_bytes <= budget:
            return cand
    return rows


def _cast_into_slot(w, slot, name):
    r, c = w.shape
    tr = _row_tile(r, c * 4, 4 * 1024 * 1024, least=16)

    def body(slot_ref, w_ref, o_ref):
        o_ref[...] = w_ref[...].astype(o_ref.dtype)

    gs = pltpu.PrefetchScalarGridSpec(
        num_scalar_prefetch=1, grid=(r // tr,),
        in_specs=[pl.BlockSpec((tr, c), lambda i, s: (i, 0))],
        out_specs=pl.BlockSpec((None, tr, c), lambda i, s: (s[0], i, 0)))
    return pl.pallas_call(body, name=name, grid_spec=gs, out_shape=jax.ShapeDtypeStruct((N_CHIPS, r, c), BF16),
                          compiler_params=_cparams(("parallel",)))(slot, w)


def _sum_own(p, t, sel, name):
    _, h, c = p.shape
    tr = _row_tile(h, c * 4, 2 * 1024 * 1024, least=16)
    nblk = h // tr

    def body(sel_ref, p_ref, t_ref, o_ref):
        acc = p_ref[...].astype(F32)
        for k in range(3):
            acc = acc + t_ref[k].astype(F32)
        o_ref[...] = acc

    gs = pltpu.PrefetchScalarGridSpec(
        num_scalar_prefetch=1, grid=(nblk,),
        in_specs=[pl.BlockSpec((None, tr, c), lambda i, s: (s[0], i, 0)),
                  pl.BlockSpec((3, tr, c), lambda i, s: (0, i, 0))],
        out_specs=pl.BlockSpec((tr, c), lambda i, s: (s[1] * nblk + i, 0)))
    return pl.pallas_call(body, name=name, grid_spec=gs, out_shape=jax.ShapeDtypeStruct((2 * h, c), F32),
                          compiler_params=_cparams(("parallel",)))(sel, p, t)


def _add_half(g, t, half, name):
    n, r, c = g.shape
    h = r // 2
    tr = h
    for cand in (512, 256, 128, 64, 32, 16):
        if h % cand == 0 and cand * c * 2 <= 2 * 1024 * 1024:
            tr = cand
            break
    nblk = h // tr

    def body(half_ref, g_ref, t_ref, o_ref):
        o_ref[...] = (g_ref[...].astype(F32) + t_ref[...].astype(F32)).astype(o_ref.dtype)

    gs = pltpu.PrefetchScalarGridSpec(
        num_scalar_prefetch=1, grid=(n, nblk),
        in_specs=[pl.BlockSpec((None, tr, c), lambda j, i, hr: (j, hr[0] * nblk + i, 0)),
                  pl.BlockSpec((None, tr, c), lambda j, i, hr: (j, i, 0))],
        out_specs=pl.BlockSpec((None, tr, c), lambda j, i, hr: (j, i, 0)))
    return pl.pallas_call(body, name=name, grid_spec=gs, out_shape=jax.ShapeDtypeStruct((n, h, c), BF16),
                          compiler_params=_cparams(("parallel", "parallel")))(half, g, t)


def _position():
    x, y, c = lax.axis_index("x"), lax.axis_index("y"), lax.axis_index("c")
    return x, y, c


def _allgather8(xs, name):
    m_per, n = xs.shape

    def body(x_ref, out_ref, send_sems, recv_sems, local_sem):
        x, y, c = _position()
        me, sibling = (x, y, c), (x, y, 1 - c)
        chips = [(1 - x, y), (x, 1 - y), (1 - x, 1 - y)]

        def rows(px, py, pc):
            return out_ref.at[pl.ds((4 * px + 2 * py + pc) * m_per, m_per), :]

        def copy(k, block, to, src=None):
            return pltpu.make_async_remote_copy(
                src_ref=rows(*block) if src is None else src, dst_ref=rows(*block),
                send_sem=send_sems.at[k], recv_sem=recv_sems.at[k], device_id=to, device_id_type=MESH)

        mine = pltpu.make_async_copy(x_ref, rows(*me), local_sem)
        mine.start()
        first = [copy(0, me, sibling, src=x_ref)]
        first += [copy(1 + j, me, (*chip, c), src=x_ref) for j, chip in enumerate(chips)]
        for cp in first:
            cp.start()
        passed = [copy(4 + j, (*chip, c), sibling) for j, chip in enumerate(chips)]
        for j, chip in enumerate(chips):
            copy(1 + j, (*chip, c), me).wait_recv()
            passed[j].start()
        copy(0, sibling, me).wait_recv()
        for j, chip in enumerate(chips):
            copy(4 + j, (*chip, 1 - c), me).wait_recv()
        for cp in first + passed:
            cp.wait_send()
        mine.wait()

    return pl.pallas_call(
        body, name=name, out_shape=jax.ShapeDtypeStruct((N_DEV * m_per, n), xs.dtype),
        in_specs=[pl.BlockSpec(memory_space=pltpu.VMEM)], out_specs=pl.BlockSpec(memory_space=pltpu.VMEM),
        scratch_shapes=[pltpu.SemaphoreType.DMA((7,)), pltpu.SemaphoreType.DMA((7,)), pltpu.SemaphoreType.DMA],
        compiler_params=pltpu.CompilerParams(vmem_limit_bytes=VMEM_LIMIT_BYTES),
    )(xs)


_HBM = pl.BlockSpec(memory_space=pltpu.HBM)


_SEM = pl.BlockSpec(memory_space=pltpu.SEMAPHORE)
_ANY = pl.BlockSpec(memory_space=pl.ANY)
_EFFECT = pltpu.SideEffectType.DATAFLOW_SIDE_EFFECTING


def _in_hbm(a):
    return pltpu.with_memory_space_constraint(a, pltpu.HBM)


def _several(after):
    return list(after) if isinstance(after, (list, tuple)) else [after]


def _gather_start(ws, groups, after, name):
    n = len(ws)
    after = _several(after)

    def body(*refs):
        in_refs = refs[:n]
        sems, token = refs[2 * n + len(after):-1], refs[-1]
        x, y, c = _position()
        mychip = 2 * x + y
        chips = [(1 - x, y), (x, 1 - y), (1 - x, 1 - y)]
        for g, members in enumerate(groups):
            for k, i in enumerate(members):
                h = ws[i].shape[1] // 2
                mine = in_refs[i].at[mychip, pl.ds(c * h, h), :]
                for j, (px, py) in enumerate(chips):
                    pltpu.make_async_remote_copy(
                        src_ref=mine, dst_ref=mine, send_sem=sems[2 * g].at[3 * k + j],
                        recv_sem=sems[2 * g + 1].at[3 * k + j], device_id=(px, py, c), device_id_type=MESH).start()
        token[...] = jnp.zeros_like(token)

    sem_shapes = [pltpu.SemaphoreType.DMA((3 * len(m),)) for m in groups for _ in range(2)]
    res = pl.pallas_call(
        body, name=name,
        out_shape=[pltpu.HBM(w.shape, w.dtype) for w in ws] + sem_shapes + [jax.ShapeDtypeStruct((8, 128), F32)],
        in_specs=[_HBM] * n + [_ANY] * len(after),
        out_specs=[_HBM] * n + [_SEM] * len(sem_shapes) + [pl.BlockSpec(memory_space=pltpu.VMEM)],
        input_output_aliases={i: i for i in range(n)},
        compiler_params=pltpu.CompilerParams(has_side_effects=_EFFECT),
    )(*[_in_hbm(w) for w in ws], *after)
    bufs, sems, token = res[:n], res[n:-1], res[-1]
    return list(bufs), [(sems[2 * g], sems[2 * g + 1]) for g in range(len(groups))], token


def _gather_wait(bufs, send_sems, recv_sems, after, name):
    m = len(bufs)

    def body(*refs):
        in_refs = refs[:m]
        send, recv = refs[m], refs[m + 1]
        x, y, c = _position()
        mychip = 2 * x + y
        chips = [(1 - x, y), (x, 1 - y), (1 - x, 1 - y)]
        for k in range(m):
            h = bufs[k].shape[1] // 2
            mine = in_refs[k].at[mychip, pl.ds(c * h, h), :]
            for j, (px, py) in enumerate(chips):
                cp = pltpu.make_async_remote_copy(
                    src_ref=mine, dst_ref=in_refs[k].at[2 * px + py, pl.ds(c * h, h), :],
                    send_sem=send.at[3 * k + j], recv_sem=recv.at[3 * k + j],
                    device_id=(px, py, c), device_id_type=MESH)
                cp.wait_send()
                cp.wait_recv()

    res = pl.pallas_call(
        body, name=name, out_shape=[pltpu.HBM(b.shape, b.dtype) for b in bufs],
        in_specs=[_HBM] * m + [_SEM, _SEM] + [_ANY] * len(_several(after)), out_specs=[_HBM] * m,
        input_output_aliases={k: k for k in range(m)},
        compiler_params=pltpu.CompilerParams(has_side_effects=_EFFECT),
    )(*bufs, send_sems, recv_sems, *_several(after))
    return list(res)


def _forward_halves(ws, name):
    n = len(ws)

    def body(*refs):
        out_refs = refs[n:2 * n]
        send_sems, recv_sems = refs[2 * n:]
        x, y, c = _position()
        me, sibling = (x, y, c), (x, y, 1 - c)
        chips = [(1 - x, y), (x, 1 - y), (1 - x, 1 - y)]
        cps = []
        for i in range(n):
            h = ws[i].shape[1] // 2
            for j, (px, py) in enumerate(chips):
                got = out_refs[i].at[2 * px + py, pl.ds(c * h, h), :]
                cp = pltpu.make_async_remote_copy(
                    src_ref=got, dst_ref=got, send_sem=send_sems.at[3 * i + j], recv_sem=recv_sems.at[3 * i + j],
                    device_id=sibling, device_id_type=MESH)
                cp.start()
                cps.append(cp)
        for i in range(n):
            h = ws[i].shape[1] // 2
            for j, (px, py) in enumerate(chips):
                other = out_refs[i].at[2 * px + py, pl.ds((1 - c) * h, h), :]
                pltpu.make_async_remote_copy(
                    src_ref=other, dst_ref=other, send_sem=send_sems.at[3 * i + j], recv_sem=recv_sems.at[3 * i + j],
                    device_id=me, device_id_type=MESH).wait_recv()
        for cp in cps:
            cp.wait_send()

    return pl.pallas_call(
        body, name=name,
        out_shape=[jax.ShapeDtypeStruct(w.shape, w.dtype) for w in ws],
        in_specs=[_HBM] * n, out_specs=[_HBM] * n, input_output_aliases={i: i for i in range(n)},
        scratch_shapes=[pltpu.SemaphoreType.DMA((3 * n,)), pltpu.SemaphoreType.DMA((3 * n,))],
    )(*ws)


def _swap_halves(gs, name):
    n = len(gs)

    def body(*refs):
        in_refs, out_refs = refs[:n], refs[n:2 * n]
        send_sems, recv_sems = refs[2 * n:]
        x, y, c = _position()
        cps = []
        for i in range(n):
            h = gs[i].shape[1] // 2
            cp = pltpu.make_async_remote_copy(
                src_ref=in_refs[i].at[:, pl.ds((1 - c) * h, h), :], dst_ref=out_refs[i],
                send_sem=send_sems.at[i], recv_sem=recv_sems.at[i], device_id=(x, y, 1 - c), device_id_type=MESH)
            cp.start()
            cps.append(cp)
        for cp in cps:
            cp.wait()

    return pl.pallas_call(
        body, name=name,
        out_shape=[jax.ShapeDtypeStruct((g.shape[0], g.shape[1] // 2, g.shape[2]), g.dtype) for g in gs],
        in_specs=[_HBM] * n, out_specs=[_HBM] * n,
        scratch_shapes=[pltpu.SemaphoreType.DMA((n,)), pltpu.SemaphoreType.DMA((n,))],
    )(*gs)


def _copies_start(arrays, copies, nsem, after, name):
    n = len(arrays)
    after = _several(after)
    first = 2 * n + len(after)

    def body(*refs):
        for cp in copies(refs[:n], refs[first], refs[first + 1]):
            cp.start()
        refs[first + 2][...] = jnp.zeros_like(refs[first + 2])

    res = pl.pallas_call(
        body, name=name,
        out_shape=[pltpu.HBM(a.shape, a.dtype) for a in arrays]
        + [pltpu.SemaphoreType.DMA((nsem,)), pltpu.SemaphoreType.DMA((nsem,)), jax.ShapeDtypeStruct((8, 128), F32)],
        in_specs=[_HBM] * n + [_ANY] * len(after),
        out_specs=[_HBM] * n + [_SEM, _SEM, pl.BlockSpec(memory_space=pltpu.VMEM)],
        input_output_aliases={i: i for i in range(n)},
        compiler_params=pltpu.CompilerParams(has_side_effects=_EFFECT),
    )(*[_in_hbm(a) for a in arrays], *after)
    return list(res[:n]), res[n], res[n + 1], res[n + 2]


def _copies_wait(arrays, copies, send_sems, recv_sems, after, name):
    n = len(arrays)

    def body(*refs):
        for cp in copies(refs[:n], refs[n], refs[n + 1]):
            cp.wait_send()
            cp.wait_recv()

    res = pl.pallas_call(
        body, name=name, out_shape=[pltpu.HBM(a.shape, a.dtype) for a in arrays],
        in_specs=[_HBM] * n + [_SEM, _SEM] + [_ANY] * len(_several(after)), out_specs=[_HBM] * n,
        input_output_aliases={i: i for i in range(n)},
        compiler_params=pltpu.CompilerParams(has_side_effects=_EFFECT),
    )(*arrays, send_sems, recv_sems, *_several(after))
    return list(res)


def _scatter_copies(refs, send, recv):
    n = len(refs) // 2
    x, y, c = _position()
    chips = [(1 - x, y), (x, 1 - y), (1 - x, 1 - y)]
    return [pltpu.make_async_remote_copy(
        src_ref=refs[i].at[2 * px + py], dst_ref=refs[n + i].at[j],
        send_sem=send.at[3 * i + j], recv_sem=recv.at[3 * i + j], device_id=(px, py, c), device_id_type=MESH)
        for i in range(n) for j, (px, py) in enumerate(chips)]


def _swap_copies(refs, send, recv):
    n = len(refs) // 2
    x, y, c = _position()
    cps = []
    for i in range(n):
        h = refs[i].shape[1] // 2
        cps.append(pltpu.make_async_remote_copy(
            src_ref=refs[i].at[:, pl.ds((1 - c) * h, h), :], dst_ref=refs[n + i],
            send_sem=send.at[i], recv_sem=recv.at[i], device_id=(x, y, 1 - c), device_id_type=MESH))
    return cps


def _join_copies(refs, send, recv):
    x, y, c = _position()
    cps = []
    for i, r in enumerate(refs):
        h = r.shape[0] // 2
        mine = r.at[pl.ds(c * h, h), :]
        cps.append(pltpu.make_async_remote_copy(
            src_ref=mine, dst_ref=mine, send_sem=send.at[i], recv_sem=recv.at[i],
            device_id=(x, y, 1 - c), device_id_type=MESH))
    return cps


def _forward_copies(refs, send, recv):
    x, y, c = _position()
    chips = [(1 - x, y), (x, 1 - y), (1 - x, 1 - y)]
    cps = []
    for i, r in enumerate(refs):
        h = r.shape[1] // 2
        for j, (px, py) in enumerate(chips):
            got = r.at[2 * px + py, pl.ds(c * h, h), :]
            cps.append(pltpu.make_async_remote_copy(
                src_ref=got, dst_ref=got, send_sem=send.at[3 * i + j], recv_sem=recv.at[3 * i + j],
                device_id=(x, y, 1 - c), device_id_type=MESH))
    return cps


def _join_halves(rs, name):
    n = len(rs)

    def body(*refs):
        out_refs = refs[n:2 * n]
        send_sems, recv_sems = refs[2 * n:]
        x, y, c = _position()
        cps = []
        for i in range(n):
            h = rs[i].shape[0] // 2
            mine = out_refs[i].at[pl.ds(c * h, h), :]
            cp = pltpu.make_async_remote_copy(
                src_ref=mine, dst_ref=mine, send_sem=send_sems.at[i], recv_sem=recv_sems.at[i],
                device_id=(x, y, 1 - c), device_id_type=MESH)
            cp.start()
            cps.append(cp)
        for i in range(n):
            h = rs[i].shape[0] // 2
            other = out_refs[i].at[pl.ds((1 - c) * h, h), :]
            pltpu.make_async_remote_copy(
                src_ref=other, dst_ref=other, send_sem=send_sems.at[i], recv_sem=recv_sems.at[i],
                device_id=(x, y, c), device_id_type=MESH).wait_recv()
        for cp in cps:
            cp.wait_send()

    return pl.pallas_call(
        body, name=name,
        out_shape=[jax.ShapeDtypeStruct(r.shape, r.dtype) for r in rs],
        in_specs=[_HBM] * n, out_specs=[_HBM] * n, input_output_aliases={i: i for i in range(n)},
        scratch_shapes=[pltpu.SemaphoreType.DMA((n,)), pltpu.SemaphoreType.DMA((n,))],
    )(*rs)


def _t5_buckets_block():
    qi = np.arange(BLOCK)[:, None]
    ki = np.arange(2 * BLOCK)[None, :]
    n = np.maximum(qi + BLOCK - ki, 0)
    max_exact = NUM_BUCKETS // 2
    large = max_exact + (np.log(np.maximum(n, 1) / max_exact) / np.log(MAX_DISTANCE / max_exact)
                         * (NUM_BUCKETS - max_exact)).astype(np.int32)
    large = np.minimum(large, NUM_BUCKETS - 1)
    return np.where(n < max_exact, n, large).astype(np.int32)


def _discretise(lambda_re, lambda_im, log_step, b_re, b_im):
    lam_re = jnp.minimum(lambda_re, -1e-4)
    lam_im = lambda_im
    delta = jnp.exp(log_step)[:, None]
    mag = jnp.exp(lam_re * delta)
    ang = lam_im * delta
    abar_re, abar_im = mag * jnp.cos(ang), mag * jnp.sin(ang)
    num_re, num_im = abar_re - 1.0, abar_im
    den = lam_re * lam_re + lam_im * lam_im
    f_re = (num_re * lam_re + num_im * lam_im) / den
    f_im = (num_im * lam_re - num_re * lam_im) / den
    bbar_re = f_re[..., None] * b_re - f_im[..., None] * b_im
    bbar_im = f_re[..., None] * b_im + f_im[..., None] * b_re
    return abar_re, abar_im, bbar_re, bbar_im


def _interleave(v, nc):
    s, w = v.shape
    return v.reshape(nc, s // nc, w).transpose(1, 0, 2).reshape(s, w)


def _deinterleave(v, nc):
    s, w = v.shape
    return v.reshape(s // nc, nc, w).transpose(1, 0, 2).reshape(s, w)


_SMALL = ("norm1_g", "b_in", "attn_sinks", "rel_bias", "lambda_re", "lambda_im", "log_step", "ssm_b_re",
          "ssm_b_im", "ssm_c_re", "ssm_c_im", "ssm_d", "b_glu", "norm2_g", "final_g")


def _pack(parts):
    rows = []
    for p in parts:
        f = p.reshape(-1).astype(F32)
        pad = (-f.shape[0]) % 128
        rows.append(jnp.pad(f, (0, pad)).reshape(-1, 128))
    out = jnp.concatenate(rows, axis=0)
    pad = (-out.shape[0]) % 256
    return jnp.pad(out, ((0, pad), (0, 0)))


def _unpack(packed, shapes):
    res, r = [], 0
    for shp in shapes:
        size = int(np.prod(shp))
        nr = -(-size // 128)
        res.append(packed[r:r + nr].reshape(-1)[:size].reshape(shp))
        r += nr
    return res


def kernel(x, c, w_ada, b_ada, norm1_g, w_in, b_in, attn_sinks, rel_bias, lambda_re, lambda_im, log_step, ssm_b_re, ssm_b_im, ssm_c_re, ssm_c_im, ssm_d, w_glu, b_glu, w_attn_proj, w_ssm_proj, w_out, norm2_g, w_ff1, w_ff2, final_g, loss_target, m_w_ada, m_b_ada, m_norm1_g, m_w_in, m_b_in, m_attn_sinks, m_rel_bias, m_lambda_re, m_lambda_im, m_log_step, m_ssm_b_re, m_ssm_b_im, m_ssm_c_re, m_ssm_c_im, m_ssm_d, m_w_glu, m_b_glu, m_w_attn_proj, m_w_ssm_proj, m_w_out, m_norm2_g, m_w_ff1, m_w_ff2, m_final_g, v_w_ada, v_b_ada, v_norm1_g, v_w_in, v_b_in, v_attn_sinks, v_rel_bias, v_lambda_re, v_lambda_im, v_log_step, v_ssm_b_re, v_ssm_b_im, v_ssm_c_re, v_ssm_c_im, v_ssm_d, v_w_glu, v_b_glu, v_w_attn_proj, v_w_ssm_proj, v_w_out, v_norm2_g, v_w_ff1, v_w_ff2, v_final_g):
    given = dict(locals())
    S, D = x.shape[1], x.shape[2]
    SSM_W = w_glu.shape[2]
    G = SSM_W // SSM_GROUP_CH
    NST = G * SSM_STATE
    DFF = w_ff2.shape[1] * N_CHIPS
    INW = w_in.shape[2] * N_CHIPS
    o_q, o_k, o_v, o_u = 0, ATTN_WIDTH, ATTN_WIDTH + KV_WIDTH, ATTN_WIDTH + 2 * KV_WIDTH
    o_ga, o_gs = o_u + SSM_W, o_u + SSM_W + D
    mx, my, mc = _position()
    my_chip = 2 * mx + my
    my_b = 4 * mx + 2 * my + mc

    xv, tgt = x[0], loss_target[0]

    big = dict(w_in=w_in[0], w_glu=w_glu[0], w_attn_proj=w_attn_proj[0], w_ssm_proj=w_ssm_proj[0],
               w_out=w_out[0], w_ff1=w_ff1[0], w_ff2=w_ff2[0])
    big_names = list(big)
    colsharded = {"w_in", "w_attn_proj", "w_ssm_proj", "w_ff1"}
    chip_sel = my_chip.astype(jnp.int32).reshape(1)
    gather_groups = [["w_in"], ["w_attn_proj", "w_ssm_proj", "w_glu", "w_out"], ["w_ff1", "w_ff2"]]
    in_flight, gather_sems, gathered = {}, [], {}

    def finish_gather(g, after):
        bufs = [in_flight[k] for k in gather_groups[g]]
        bufs = _gather_wait(bufs, gather_sems[g][0], gather_sems[g][1], after, "gather_wait_%d" % g)
        gathered.update(zip(gather_groups[g], _forward_halves(bufs, "gather_forward_%d" % g)))

    def tied(v, token):
        return v + token[0:1, 0:1]

    def all_of(*arrays):
        return list(arrays)

    def wop(k):
        g = gathered[k]
        return _Op(g, N_CHIPS) if k in colsharded else _Op(g.reshape(g.shape[0] * g.shape[1], g.shape[2]))

    grads = {}
    nothing = jnp.zeros((8, 128), F32)
    half = mc.astype(jnp.int32).reshape(1)
    sel = jnp.stack([my_chip, mc]).astype(jnp.int32)

    def rs_swap(tag, named):
        keys, gl = list(named), []
        for k in keys:
            gk = named[k]
            if k not in colsharded:
                gk = gk.reshape(N_CHIPS, gk.shape[0] // N_CHIPS, gk.shape[1])
            gl.append(gk)
        lands = [lax.empty((g.shape[0], g.shape[1] // 2, g.shape[2]), g.dtype) for g in gl]
        arrays, ssem, rsem, token = _copies_start(gl + lands, _swap_copies, len(gl), nothing, "rs_swap_start_" + tag)
        return (keys, arrays, ssem, rsem), token

    def rs_scatter(tag, state, after):
        keys, arrays, ssem, rsem = state
        arrays = _copies_wait(arrays, _swap_copies, ssem, rsem, after, "rs_swap_wait_" + tag)
        n = len(keys)
        ps = [_add_half(g, t, half, "rs_add_" + k) for g, t, k in zip(arrays[:n], arrays[n:], keys)]
        lands = [lax.empty((3,) + p.shape[1:], p.dtype) for p in ps]
        arrays, ssem, rsem, token = _copies_start(ps + lands, _scatter_copies, 3 * n, nothing, "rs_start_" + tag)
        return (keys, arrays, ssem, rsem), token

    def rs_sum(tag, state, after):
        keys, arrays, ssem, rsem = state
        arrays = _copies_wait(arrays, _scatter_copies, ssem, rsem, after, "rs_wait_" + tag)
        n = len(keys)
        rs = [_sum_own(p, t, sel, "rs_sum_" + k) for p, t, k in zip(arrays[:n], arrays[n:], keys)]
        rs, ssem, rsem, token = _copies_start(rs, _join_copies, n, nothing, "rs_join_start_" + tag)
        return (keys, rs, ssem, rsem), token

    def rs_finish(tag, state, after):
        keys, rs, ssem, rsem = state
        for k, f in zip(keys, _copies_wait(rs, _join_copies, ssem, rsem, after, "rs_join_wait_" + tag)):
            grads[k] = f[None]

    c_all = _allgather8(jnp.pad(c, ((0, 7), (0, 0))), "gather_c").reshape(N_DEV, 8, D)[:, 0]
    c16 = jnp.pad(c_all, ((0, 8), (0, 0)))
    b_ada_mine = lax.dynamic_slice(b_ada.reshape(N_CHIPS, -1), (my_chip, 0), (1, w_ada.shape[2]))
    mod_sh = _mm(c16, w_ada[0], "NN", name="mod", M=16, N=w_ada.shape[2], K=D, a_fn=_silu,
                 epilogue=lambda acc, b: (acc + b,), extras=[(b_ada_mine, "row")])
    mod_all = _allgather8(mod_sh[:8], "gather_mod").reshape(N_DEV, 8, -1)
    mod_row = jnp.concatenate(
        [lax.dynamic_slice(mod_all, (2 * j, my_b, 0), (1, 1, mod_all.shape[2]))[0] for j in range(N_CHIPS)], axis=1)
    sh1, sc1, g1, sh2, sc2, g2 = [mod_row[:, i * D:(i + 1) * D] for i in range(6)]

    first = [_cast_into_slot(big["w_in"], chip_sel, "cast_w_in")]
    first, sems_first, token_first = _gather_start(first, [[0]], mod_all, "gather_start_in")
    rest_names = gather_groups[1] + gather_groups[2]
    rest = [_cast_into_slot(big[k], chip_sel, "cast_" + k) for k in rest_names]
    rest, sems_rest, token_rest = _gather_start(
        rest, [[rest_names.index(k) for k in grp] for grp in gather_groups[1:]], token_first, "gather_start_rest")
    in_flight.update(zip(["w_in"] + rest_names, first + rest))
    gather_sems.extend(sems_first + sems_rest)

    disc_in = (lambda_re[0], lambda_im[0], log_step[0], ssm_b_re[0], ssm_b_im[0])
    (abar_re, abar_im, bbar_re, bbar_im), disc_vjp = jax.vjp(_discretise, *disc_in)
    same_group = jnp.asarray(np.arange(SSM_W)[:, None] // SSM_GROUP_CH == np.arange(NST)[None, :] // SSM_STATE)

    def block_diag(t):
        return jnp.where(same_group, jnp.tile(t, (G, 1)), 0.0)

    bd = jnp.concatenate([block_diag(bb.transpose(2, 0, 1).reshape(SSM_GROUP_CH, NST)) for bb in (bbar_re, bbar_im)],
                         axis=1)
    cd = jnp.concatenate([block_diag(cc.transpose(1, 0, 2).reshape(SSM_GROUP_CH, NST)).T
                          for cc in (ssm_c_re[0], -ssm_c_im[0])], axis=0)
    a_fwd = jnp.stack([abar_re.reshape(1, NST), abar_im.reshape(1, NST)])
    a_bwd = jnp.stack([abar_re.reshape(1, NST), -abar_im.reshape(1, NST)])
    d_row = ssm_d

    buckets = _t5_buckets_block()
    onehot_t = (jnp.arange(128, dtype=jnp.int32)[:, None] == jnp.asarray(buckets.reshape(1, -1))).astype(BF16)
    rb_hi = rel_bias.astype(BF16)
    rb_lo = (rel_bias - rb_hi.astype(F32)).astype(BF16)
    rb_lo2 = (rel_bias - rb_hi.astype(F32) - rb_lo.astype(F32)).astype(BF16)
    rb3 = jnp.pad(jnp.concatenate([rb_hi.T, rb_lo.T, rb_lo2.T], axis=0), ((0, 0), (0, 128 - NUM_BUCKETS)))
    b3 = _mm(rb3, onehot_t, "NN", name="rel_bias_rows", M=3 * N_Q_HEADS, N=BLOCK * 2 * BLOCK, K=128, tj=4096)
    bias = (b3[:N_Q_HEADS] + b3[N_Q_HEADS:2 * N_Q_HEADS]) + b3[2 * N_Q_HEADS:]
    bias = bias.reshape(N_Q_HEADS, BLOCK, 2 * BLOCK)
    sinks_b = jnp.broadcast_to(attn_sinks[0][:, None, None], (N_Q_HEADS, BLOCK, 128)).reshape(N_Q_HEADS * BLOCK, 128)

    def two(fn):
        def both(*blocks):
            r = fn(*blocks)
            return r, r
        return both

    h1, h1_t = _rowwise(two(_norm_mod), [(xv, "tile", D), (tied(tied(norm1_g, token_first), token_rest), "row", D),
                                         (sh1, "row", D), (sc1, "row", D)],
                        [(D, BF16), (D, BF16, "T")], [], name="norm1", rows=S)
    finish_gather(0, all_of(h1, bd, cd, a_fwd, a_bwd, bias, sinks_b))
    proj = _mm(h1, wop("w_in"), "NN", name="proj", M=S, N=INW, K=D, out_dtypes=(BF16,),
               epilogue=lambda acc, b: (acc + b,), extras=[(b_in, "row")])

    def heads(v2d, nh):
        return v2d.reshape(S, nh, HEAD_DIM).transpose(1, 0, 2)

    def unheads(v3d):
        return v3d.transpose(1, 0, 2).reshape(S, -1)

    qh = heads(proj[:, o_q:o_k], N_Q_HEADS)
    kh = heads(proj[:, o_k:o_v], N_KV_HEADS)
    vh = heads(proj[:, o_v:o_u], N_KV_HEADS)
    attn = unheads(_attn_fwd(qh, kh, vh, sinks_b, bias, "attn_fwd"))
    finish_gather(1, attn)
    y_attn = _mm(attn, wop("w_attn_proj"), "NN", name="attn_proj", M=S, N=D, K=ATTN_WIDTH, out_dtypes=(BF16,))

    u = proj[:, o_u:o_ga]
    u_il = _interleave(u, SCAN_CHUNKS)
    SB = 128
    nsb, gpb = SSM_W // SB, SB // SSM_GROUP_CH
    SBN = gpb * SSM_STATE
    y_il, xs = _ssm_fwd(u_il, bd, cd, a_fwd, d_row, name="ssm_fwd", sb=SB, sbn=SBN)
    y = _deinterleave(y_il, SCAN_CHUNKS)
    z, t_glu = _mm(y, wop("w_glu"), "NN", name="glu", M=S, N=SSM_W, K=SSM_W, out_dtypes=(BF16, F32), a_fn=_gelu,
                   epilogue=lambda acc, b, yy: (_gelu(yy) * _sigmoid(acc + b), acc + b),
                   extras=[(b_glu, "row"), (y, "tile")])
    y_ssm = _mm(z, wop("w_ssm_proj"), "NN", name="ssm_proj", M=S, N=D, K=SSM_W, out_dtypes=(BF16,))

    ff_bufs = _gather_wait([in_flight[k] for k in gather_groups[2]], gather_sems[2][0], gather_sems[2][1], all_of(y_ssm),
                           "gather_wait_2")
    ff_bufs, ff_send, ff_recv, token = _copies_start(ff_bufs, _forward_copies, 3 * len(ff_bufs), nothing,
                                                    "gather_forward_2_start")
    merged, merged_t = _rowwise(two(_merge), [(_Op(proj, coff=o_ga), "tile", D), (_Op(proj, coff=o_gs), "tile", D),
                                              (y_attn, "tile", D), (y_ssm, "tile", D)],
                                [(D, BF16), (D, BF16, "T")], [], name="merge", rows=S, deps=[token])
    mo, x2 = _mm(merged, wop("w_out"), "NN", name="out_proj", M=S, N=D, K=D, out_dtypes=(BF16, F32),
                 epilogue=lambda acc, xx, gg: (acc, xx + gg * acc), extras=[(xv, "tile"), (g1, "row")])
    h2, h2_t = _rowwise(two(_norm_mod), [(x2, "tile", D), (norm2_g, "row", D), (sh2, "row", D), (sc2, "row", D)],
                        [(D, BF16), (D, BF16, "T")], [], name="norm2", rows=S)
    gathered.update(zip(gather_groups[2], _copies_wait(ff_bufs, _forward_copies, ff_send, ff_recv, h2,
                                                       "gather_forward_2_wait")))
    a_b, r_b = _mm(h2, wop("w_ff1"), "NN", name="ff1", M=S, N=DFF, K=D, out_dtypes=(BF16, BF16),
                   epilogue=lambda acc: (acc, jnp.square(jnp.maximum(acc, 0.0))))
    ff, x3 = _mm(r_b, wop("w_ff2"), "NN", name="ff2", M=S, N=D, K=DFF, out_dtypes=(BF16, F32),
                 epilogue=lambda acc, xx, gg: (acc, xx + gg * acc), extras=[(x2, "tile"), (g2, "row")],
                 tj=1024, tk=1024)

    def final_fn(x3b, gf, tb, ffb, g2b):
        def f(xx, gg):
            yv = xx * lax.rsqrt(jnp.mean(xx * xx, axis=-1, keepdims=True) + EPS) * gg
            err = jnp.square(yv - tb)
            return 0.5 * jnp.sum(jnp.mean(err, axis=-1, keepdims=True), axis=0, keepdims=True)
        lv, vjp = jax.vjp(f, x3b, gf)
        dx, dg = vjp(jnp.ones((1, 1), F32))
        return dx, dx * g2b, dg, jnp.broadcast_to(lv, (1, 128)), jnp.sum(dx * ffb, axis=0, keepdims=True)

    dx3, dff, g_final, loss_acc, d_g2 = _rowwise(
        final_fn, [(x3, "tile", D), (final_g.reshape(1, D), "row", D), (tgt, "tile", D), (ff, "tile", D), (g2, "row", D)],
        [(D, F32), (D, BF16)], [D, 128, D], name="final", rows=S)
    da = _mm(dff, wop("w_ff2"), "NT", name="ff2_dx", M=S, N=DFF, K=D, out_dtypes=(BF16,),
             epilogue=lambda acc, ab: (acc * (2.0 * jnp.maximum(ab.astype(F32), 0.0)),), extras=[(a_b, "tile")])
    g_w_ff2 = _mm(r_b, dff, "TN", name="ff2_dw", M=DFF, N=D, K=S, out_dtypes=(BF16,), tj=1024, tk=1024)
    g_w_ff1 = _mm(h2_t, da, "NN", name="ff1_dw", M=D, N=DFF, K=S, out_dtypes=(BF16,), out_nsh=N_CHIPS, tj=1024, tk=1024)
    rs_ff, token = rs_swap("ff", dict(w_ff2=g_w_ff2, w_ff1=g_w_ff1))
    dh2 = _mm(da, wop("w_ff1"), "NT", name="ff1_dx", M=S, N=D, K=DFF, tj=1024, tk=1024, deps=[token])
    rs_ff, token_ff = rs_scatter("ff", rs_ff, dh2)

    def norm2_bwd(x2b, dh2b, dx3b, mob, gn, shb, scb, g1b):
        _, vjp = jax.vjp(_norm_mod, x2b, gn, shb, scb)
        dx, dg, dsh, dsc = vjp(dh2b)
        dx2b = dx + dx3b
        return dx2b, dx2b * g1b, dg, dsh, dsc, jnp.sum(dx2b * mob, axis=0, keepdims=True)

    dx2, dmo, g_norm2, d_sh2, d_sc2, d_g1 = _rowwise(
        norm2_bwd, [(x2, "tile", D), (dh2, "tile", D), (dx3, "tile", D), (mo, "tile", D),
                    (tied(norm2_g, token_ff), "row", D), (sh2, "row", D), (sc2, "row", D), (g1, "row", D)],
        [(D, F32), (D, BF16)], [D, D, D, D], name="norm2_bwd", rows=S)
    dmerged = _mm(dmo, wop("w_out"), "NT", name="out_dx", M=S, N=D, K=D)
    g_w_out = _mm(merged_t, dmo, "NN", name="out_dw", M=D, N=D, K=S, out_dtypes=(BF16,), tj=1024, tk=1024)

    def merge_bwd(gab, gsb, yab, ysb, dmb):
        _, vjp = jax.vjp(_merge, gab, gsb, yab, ysb)
        return vjp(dmb)

    d_ga, d_gs, dy_attn, dy_ssm = _rowwise(
        merge_bwd, [(_Op(proj, coff=o_ga), "tile", D), (_Op(proj, coff=o_gs), "tile", D), (y_attn, "tile", D),
                    (y_ssm, "tile", D), (dmerged, "tile", D)],
        [(D, BF16), (D, BF16), (D, BF16), (D, BF16)], [], name="merge_bwd", rows=S)

    dattn = _mm(dy_attn, wop("w_attn_proj"), "NT", name="attn_proj_dx", M=S, N=ATTN_WIDTH, K=D, tj=1024)
    g_w_attn_proj = _mm(attn, dy_attn, "TN", name="attn_proj_dw", M=ATTN_WIDTH, N=D, K=S, out_dtypes=(BF16,),
                        out_nsh=N_CHIPS, tk=1024)

    dz = _mm(dy_ssm, wop("w_ssm_proj"), "NT", name="ssm_proj_dx", M=S, N=SSM_W, K=D)
    g_w_ssm_proj = _mm(z, dy_ssm, "TN", name="ssm_proj_dw", M=SSM_W, N=D, K=S, out_dtypes=(BF16,),
                       out_nsh=N_CHIPS, tk=1024)

    def glu_bwd(dzb, yb, tb):
        z0 = _gelu(yb)
        sg = _sigmoid(tb)
        dt = dzb * z0 * sg * (1.0 - sg)
        return dt, dzb * sg, jnp.sum(dt, axis=0, keepdims=True)

    dt_b, dz0a, g_b_glu = _rowwise(glu_bwd, [(dz, "tile", SSM_W), (y, "tile", SSM_W), (t_glu, "tile", SSM_W)],
                                   [(SSM_W, BF16), (SSM_W, F32)], [SSM_W], name="glu_bwd", rows=S)

    def gelu_bwd(acc, dz0ab, yb):
        _, vjp = jax.vjp(_gelu, yb)
        return (vjp(acc + dz0ab)[0],)

    dy = _mm(dt_b, wop("w_glu"), "NT", name="glu_dx", M=S, N=SSM_W, K=SSM_W, epilogue=gelu_bwd,
             extras=[(dz0a, "tile"), (y, "tile")])
    g_w_glu = _mm(y, dt_b, "TN", name="glu_dw", M=SSM_W, N=SSM_W, K=S, out_dtypes=(BF16,), tk=1024, a_fn=_gelu)
    rs_mix, token = rs_swap("mix", dict(w_out=g_w_out, w_attn_proj=g_w_attn_proj, w_ssm_proj=g_w_ssm_proj,
                                        w_glu=g_w_glu))
    dy_il = _interleave(tied(dy, token), SCAN_CHUNKS)
    du_il, g_bd, g_cd, d_abar, g_ssm_d = _ssm_bwd(dy_il, u_il, xs, bd, cd, a_bwd, d_row, name="ssm_bwd", sb=SB, sbn=SBN)
    du = _deinterleave(du_il, SCAN_CHUNKS)
    rs_mix, token_mix = rs_scatter("mix", rs_mix, du_il)

    dqh, dkh, dvh, dsink_blk, dbias = _attn_bwd(qh, kh, vh, heads(dattn, N_Q_HEADS), tied(sinks_b, token_mix), bias,
                                                "attn_bwd")
    g_sinks = _sum_lead(dsink_blk.reshape(N_Q_HEADS, BLOCK, 128).transpose(1, 0, 2), "sinks_dw")[:, 0].reshape(1, N_Q_HEADS)
    g_rel = _mm(dbias.reshape(N_Q_HEADS, -1), onehot_t, "NT", name="rel_bias_dw", M=N_Q_HEADS, N=128,
                K=BLOCK * 2 * BLOCK, tk=4096)
    g_rel_bias = g_rel[:, :NUM_BUCKETS].T

    eye_b = jnp.eye(gpb, dtype=F32)
    g_cd6 = g_cd.reshape(2, nsb, gpb, SSM_STATE, gpb, SSM_GROUP_CH)
    g_c_re = jnp.einsum("bgnhp,gh->bgpn", g_cd6[0], eye_b).reshape(G, SSM_GROUP_CH, SSM_STATE)
    g_c_im = -jnp.einsum("bgnhp,gh->bgpn", g_cd6[1], eye_b).reshape(G, SSM_GROUP_CH, SSM_STATE)
    g_bd6 = g_bd.reshape(nsb, gpb, SSM_GROUP_CH, 2, gpb, SSM_STATE)
    g_bbar = jnp.einsum("bhprgn,hg->rbhnp", g_bd6, eye_b).reshape(2, G, SSM_STATE, SSM_GROUP_CH)
    g_bbar_re, g_bbar_im = g_bbar[0], g_bbar[1]
    g_lre, g_lim, g_lstep, g_bre, g_bim = disc_vjp(
        (d_abar[0].reshape(G, SSM_STATE), d_abar[1].reshape(G, SSM_STATE), g_bbar_re, g_bbar_im))

    dproj = jnp.concatenate([unheads(dqh).astype(BF16), unheads(dkh).astype(BF16), unheads(dvh).astype(BF16),
                             du.astype(BF16), d_ga, d_gs], axis=1)
    g_w_in = _mm(h1_t, dproj, "NN", name="proj_dw", M=D, N=INW, K=S, out_dtypes=(BF16,), out_nsh=N_CHIPS,
                 tj=INW // (2 * N_CHIPS), tk=1024)
    rs_in, token = rs_swap("in", dict(w_in=g_w_in))
    dh1 = _mm(dproj, wop("w_in"), "NT", name="proj_dx", M=S, N=D, K=INW, tj=1024, tk=INW // N_CHIPS, deps=[token])
    g_b_in = _rowwise(lambda d: (jnp.sum(d.astype(F32), axis=0, keepdims=True),), [(dproj, "tile", INW)], [], [INW],
                      name="proj_db", rows=S)[0]

    def norm1_bwd(xb, dhb, dresb, gn, shb, scb):
        _, vjp = jax.vjp(_norm_mod, xb, gn, shb, scb)
        dx, dg, dsh, dsc = vjp(dhb)
        return dx + dresb, dg, dsh, dsc

    grad_x, g_norm1, d_sh1, d_sc1 = _rowwise(
        norm1_bwd, [(xv, "tile", D), (dh1, "tile", D), (dx2, "tile", D), (norm1_g, "row", D),
                    (sh1, "row", D),
                    (sc1, "row", D)], [(D, F32)], [D, D, D], name="norm1_bwd", rows=S)

    dmod_row = jnp.concatenate([d_sh1, d_sc1, d_g1, d_sh2, d_sc2, d_g2], axis=1)
    dmod_all = _allgather8(jnp.pad(dmod_row, ((0, 7), (0, 0))), "gather_dmod").reshape(N_DEV, 8, -1)[:, 0]
    g_b_ada = _sum_lead(dmod_all.reshape(N_DEV, -1, 128), "b_ada_dw").reshape(1, -1)
    dmod_mine = lax.dynamic_slice(dmod_all.reshape(N_DEV, N_CHIPS, -1), (0, my_chip, 0), (N_DEV, 1, w_ada.shape[2]))[:, 0]
    g_w_ada = _mm(c16, jnp.pad(dmod_mine, ((0, 8), (0, 0))), "TN", name="ada_dw", M=D, N=w_ada.shape[2], K=16,
                  a_fn=_silu)

    small_g = dict(norm1_g=g_norm1, b_in=g_b_in, attn_sinks=g_sinks, rel_bias=g_rel_bias, lambda_re=g_lre[None],
                   lambda_im=g_lim[None], log_step=g_lstep[None], ssm_b_re=g_bre[None], ssm_b_im=g_bim[None],
                   ssm_c_re=g_c_re[None], ssm_c_im=g_c_im[None], ssm_d=g_ssm_d, b_glu=g_b_glu, norm2_g=g_norm2,
                   final_g=g_final.reshape(D))
    packed = _pack([loss_acc[:, :1]] + [small_g[k] for k in _SMALL])
    rows = packed.shape[0]
    summed = _sum_lead(_allgather8(packed, "gather_small").reshape(N_DEV, rows, 128), "small_sum")
    small_shapes = [(1,)] + [given[k].shape for k in _SMALL]
    parts = _unpack(summed, small_shapes)
    loss = parts[0].reshape(())
    grads.update(zip(_SMALL, parts[1:]))
    grads["b_ada"] = g_b_ada
    grads["w_ada"] = g_w_ada[None]

    deltas, new_m, new_v = {}, {}, {}

    def adamw_big(k, deps=()):
        d_, m_, v_ = _adamw(given[k][0], grads[k][0], given["m_" + k][0], given["v_" + k][0], "adamw_" + k, deps)
        deltas[k], new_m[k], new_v[k] = d_[None], m_[None], v_[None]
        return v_

    rs_in, token_in = rs_scatter("in", rs_in, all_of(summed, dmod_all))
    rs_ff, token = rs_sum("ff", rs_ff, all_of(summed, token_in))
    mark = adamw_big("w_ada", [token])
    rs_mix, token = rs_sum("mix", rs_mix, mark)
    small_all = list(_SMALL) + ["b_ada"]
    shapes = [given[k].shape for k in small_all]
    pw, pg = _pack([given[k] for k in small_all]), _pack([grads[k] for k in small_all])
    pm, pv = _pack([given["m_" + k] for k in small_all]), _pack([given["v_" + k] for k in small_all])
    d_, m_, v_ = _adamw(pw, pg, pm, pv, "adamw_small", [token])
    for k, dd, mm, vv in zip(small_all, _unpack(d_, shapes), _unpack(m_, shapes), _unpack(v_, shapes)):
        deltas[k], new_m[k], new_v[k] = dd, mm, vv
        grads[k] = grads[k].reshape(given[k].shape)
    rs_finish("ff", rs_ff, v_)
    marks = [adamw_big(k) for k in ("w_ff2", "w_ff1")]
    rs_finish("mix", rs_mix, all_of(*marks))
    marks = [adamw_big(k) for k in ("w_out", "w_attn_proj", "w_ssm_proj", "w_glu")]
    rs_in, token = rs_sum("in", rs_in, all_of(*marks))
    rs_finish("in", rs_in, token)
    adamw_big("w_in")

    names = ["w_ada", "b_ada", "norm1_g", "w_in", "b_in", "attn_sinks", "rel_bias", "lambda_re", "lambda_im",
             "log_step", "ssm_b_re", "ssm_b_im", "ssm_c_re", "ssm_c_im", "ssm_d", "w_glu", "b_glu", "w_attn_proj",
             "w_ssm_proj", "w_out", "norm2_g", "w_ff1", "w_ff2", "final_g"]
    return (loss, grad_x[None], *[grads[n] for n in names], *[deltas[n] for n in names],
            *[new_m[n] for n in names], *[new_v[n] for n in names])
```

```python
import math

import numpy as np
import jax
import jax.numpy as jnp
from jax import lax
from jax.experimental import pallas as pl
from jax.experimental.pallas import tpu as pltpu

F32 = jnp.float32
BF16 = jnp.bfloat16
MESH = pl.DeviceIdType.MESH

HEAD_DIM = 64
N_Q_HEADS = 16
N_KV_HEADS = 4
GQA_GROUP = N_Q_HEADS // N_KV_HEADS
ATTN_WIDTH = N_Q_HEADS * HEAD_DIM
KV_WIDTH = N_KV_HEADS * HEAD_DIM
BLOCK = 128
NUM_BUCKETS = 32
MAX_DISTANCE = 128
NEG_INF = -1e30
SSM_GROUP_CH = 16
SSM_STATE = 64
EPS = 1e-6
ADAM_LR = 0.001
ADAM_B1 = 0.9
ADAM_B2 = 0.999
ADAM_EPS = 1e-08
ADAM_WD = 0.01
ADAM_STEP = 10

N_CHIPS = 4
N_DEV = 8
SCAN_CHUNKS = 8
VMEM_LIMIT_BYTES = 48 * 1024 * 1024
SSM_VMEM_LIMIT_BYTES = 56 * 1024 * 1024


def _cparams(sem=None):
    return pltpu.CompilerParams(dimension_semantics=sem, vmem_limit_bytes=VMEM_LIMIT_BYTES)


class _Op:
    def __init__(self, arr, nsh=None, coff=0):
        self.arr, self.nsh, self.coff = arr, nsh, coff
        if nsh is None:
            self.rows, self.cols = arr.shape
        else:
            assert arr.shape[0] == nsh
            self.rows, self.cols = arr.shape[1], arr.shape[2] * nsh

    def spec(self, br, bc, idx):
        assert self.coff % bc == 0
        off = self.coff // bc
        if self.nsh is None:
            return pl.BlockSpec((br, bc), lambda *g: (idx(*g)[0], idx(*g)[1] + off))
        per = (self.cols // self.nsh) // bc
        assert per * bc * self.nsh == self.cols

        def imap(*g):
            r, c = idx(*g)
            c = c + off
            return (c // per, r, c % per)
        return pl.BlockSpec((None, br, bc), imap)


def _as_op(a):
    return a if isinstance(a, _Op) else _Op(a)


def _mm(a, b, mode, *, name, M, N, K, out_dtypes=(F32,), out_nsh=None, epilogue=None, extras=(),
        a_fn=None, ti=1024, tj=512, tk=2048, deps=()):
    a_idx = b_idx = None
    nd = len(deps)
    a, b = _as_op(a), _as_op(b)
    ti, tj, tk = min(ti, M), min(tj, N), min(tk, K)
    a_w = a.cols // a.nsh if a.nsh else None
    b_w = b.cols // b.nsh if b.nsh else None
    if a_w:
        ti, tk = (min(ti, a_w), tk) if mode == "TN" else (ti, min(tk, a_w))
    if b_w:
        tj, tk = (tj, min(tk, b_w)) if mode == "NT" else (min(tj, b_w), tk)
    if out_nsh:
        tj = min(tj, N // out_nsh)
    assert M % ti == 0 and N % tj == 0 and K % tk == 0, (name, M, N, K, ti, tj, tk)
    nk = K // tk
    if mode == "NN":
        a_spec = a.spec(ti, tk, a_idx or (lambda i, j, k: (i, k)))
        b_spec = b.spec(tk, tj, b_idx or (lambda i, j, k: (k, j)))
        dims = (((1,), (0,)), ((), ()))
    elif mode == "NT":
        a_spec = a.spec(ti, tk, a_idx or (lambda i, j, k: (i, k)))
        b_spec = b.spec(tj, tk, b_idx or (lambda i, j, k: (j, k)))
        dims = (((1,), (1,)), ((), ()))
    else:
        a_spec = a.spec(tk, ti, a_idx or (lambda i, j, k: (k, i)))
        b_spec = b.spec(tk, tj, b_idx or (lambda i, j, k: (k, j)))
        dims = (((0,), (0,)), ((), ()))
    ex_specs, ex_arrs = [], []
    for op, kind in extras:
        op = _as_op(op)
        if kind == "tile":
            ex_specs.append(op.spec(ti, tj, lambda i, j, k: (i, j)))
        else:
            ex_specs.append(op.spec(1, tj, lambda i, j, k: (0, j)))
        ex_arrs.append(op.arr)
    ne, no = len(ex_arrs), len(out_dtypes)
    if out_nsh is None:
        out_shapes = [jax.ShapeDtypeStruct((M, N), d) for d in out_dtypes]
        out_specs = [pl.BlockSpec((ti, tj), lambda i, j, k: (i, j)) for _ in out_dtypes]
    else:
        per = (N // out_nsh) // tj
        assert per * tj * out_nsh == N
        out_shapes = [jax.ShapeDtypeStruct((out_nsh, M, N // out_nsh), d) for d in out_dtypes]
        out_specs = [pl.BlockSpec((None, ti, tj), lambda i, j, k: (j // per, i, j % per)) for _ in out_dtypes]

    def body(a_ref, b_ref, *rest):
        ex_refs, out_refs, acc = rest[:ne], rest[ne + nd:ne + nd + no], rest[ne + nd + no]
        k = pl.program_id(2)

        @pl.when(k == 0)
        def _():
            acc[...] = jnp.zeros_like(acc)

        av = a_ref[...]
        if a_fn is not None:
            av = a_fn(av)
        acc[...] += lax.dot_general(av.astype(BF16), b_ref[...].astype(BF16), dims,
                                    preferred_element_type=F32)

        @pl.when(k == nk - 1)
        def _():
            res = acc[...]
            outs = epilogue(res, *[r[...] for r in ex_refs]) if epilogue is not None else (res,)
            for o_ref, o in zip(out_refs, outs):
                o_ref[...] = o.astype(o_ref.dtype)

    outs = pl.pallas_call(
        body, name=name, grid=(M // ti, N // tj, nk),
        in_specs=[a_spec, b_spec] + ex_specs + [pl.BlockSpec(memory_space=pl.ANY)] * nd,
        out_specs=out_specs, out_shape=out_shapes,
        scratch_shapes=[pltpu.VMEM((ti, tj), F32)],
        compiler_params=_cparams(("parallel", "parallel", "arbitrary")),
    )(a.arr, b.arr, *ex_arrs, *deps)
    return outs[0] if no == 1 else outs


def _rowwise(fn, ins, outs, accs, *, name, rows, tr=256, deps=()):
    tr = min(tr, rows)
    assert rows % tr == 0
    in_specs, arrs = [], []
    for op, kind, width in ins:
        op = _as_op(op)
        if kind == "tile":
            in_specs.append(op.spec(tr, width, lambda i: (i, 0)))
        else:
            in_specs.append(op.spec(op.rows, width, lambda i: (0, 0)))
        arrs.append(op.arr)
    ni, no, na = len(ins), len(outs), len(accs)
    flipped = [len(o) == 3 for o in outs]
    out_shapes = [jax.ShapeDtypeStruct((o[0], rows) if t else (rows, o[0]), o[1]) for o, t in zip(outs, flipped)]
    out_specs = [pl.BlockSpec((o[0], tr), lambda i: (0, i)) if t else pl.BlockSpec((tr, o[0]), lambda i: (i, 0))
                 for o, t in zip(outs, flipped)]
    out_shapes += [jax.ShapeDtypeStruct((1, w), F32) for w in accs]
    out_specs += [pl.BlockSpec((1, w), lambda i: (0, 0)) for w in accs]

    def body(*refs):
        nd = len(deps)
        in_refs, out_refs, acc_refs = refs[:ni], refs[ni + nd:ni + nd + no], refs[ni + nd + no:]
        res = fn(*[r[...] for r in in_refs])
        if not isinstance(res, (tuple, list)):
            res = (res,)
        for o_ref, r, t in zip(out_refs, res[:no], flipped):
            o_ref[...] = (r.astype(F32).T if t else r).astype(o_ref.dtype)
        if na:
            @pl.when(pl.program_id(0) == 0)
            def _():
                for a_ref in acc_refs:
                    a_ref[...] = jnp.zeros_like(a_ref)
            for a_ref, r in zip(acc_refs, res[no:]):
                a_ref[...] += r.astype(F32)

    res = pl.pallas_call(
        body, name=name, grid=(rows // tr,), in_specs=in_specs + [pl.BlockSpec(memory_space=pl.ANY)] * len(deps),
        out_specs=out_specs, out_shape=out_shapes, compiler_params=_cparams(("arbitrary",)),
    )(*arrs, *deps)
    return res


def _norm_mod(x, g, sh, sc):
    y = x * lax.rsqrt(jnp.mean(x * x, axis=-1, keepdims=True) + EPS) * g
    return y * (1.0 + sc) + sh


def _sigmoid(x):
    return 1.0 / (1.0 + jnp.exp(-x))


def _silu(x):
    return x * _sigmoid(x)


def _gelu(x):
    return 0.5 * x * (1.0 + jnp.tanh(math.sqrt(2.0 / math.pi) * (x + 0.044715 * (x * x * x))))


def _merge(ga, gs, ya, ys):
    ga, gs, ya, ys = (v.astype(F32) for v in (ga, gs, ya, ys))
    return _sigmoid(ga) * ya + _sigmoid(gs) * ys


def _attn_head(q, kp, kc, vp, vc, sink, bias_p, bias_c, not_first):
    nt = (((1,), (1,)), ((), ()))
    nn = (((1,), (0,)), ((), ()))
    qb = q.astype(BF16)
    scale = HEAD_DIM ** -0.5
    sp = lax.dot_general(qb, kp.astype(BF16), nt, preferred_element_type=F32) * scale + bias_p
    sc = lax.dot_general(qb, kc.astype(BF16), nt, preferred_element_type=F32) * scale + bias_c
    qi = lax.broadcasted_iota(jnp.int32, sp.shape, 0) & (BLOCK - 1)
    ki = lax.broadcasted_iota(jnp.int32, sp.shape, 1)
    sp = jnp.where(jnp.logical_and(ki > qi, not_first), sp, NEG_INF)
    sc = jnp.where(ki <= qi, sc, NEG_INF)
    m = jnp.maximum(jnp.maximum(jnp.max(sp, axis=-1, keepdims=True), jnp.max(sc, axis=-1, keepdims=True)), sink)
    m = lax.stop_gradient(m)
    pp = jnp.exp(sp - m)
    pc = jnp.exp(sc - m)
    denom = jnp.sum(pp, axis=-1, keepdims=True) + jnp.sum(pc, axis=-1, keepdims=True) + jnp.exp(sink - m)
    o = lax.dot_general((pp / denom).astype(BF16), vp.astype(BF16), nn, preferred_element_type=F32)
    o = o + lax.dot_general((pc / denom).astype(BF16), vc.astype(BF16), nn, preferred_element_type=F32)
    return o


def _attn_fwd(qh, kh, vh, sinks, bias, name):
    s = qh.shape[1]
    nb = s // BLOCK
    G = GQA_GROUP
    R = G * BLOCK

    def body(q_ref, kp_ref, kc_ref, vp_ref, vc_ref, sink_ref, bias_ref, o_ref):
        not_first = pl.program_id(0) > 0
        for kv in range(N_KV_HEADS):
            hs = slice(kv * G, (kv + 1) * G)
            o = _attn_head(q_ref[hs].reshape(R, HEAD_DIM), kp_ref[kv], kc_ref[kv], vp_ref[kv], vc_ref[kv],
                           sink_ref[kv * R:(kv + 1) * R, 0:1],
                           bias_ref[hs, :, 0:BLOCK].reshape(R, BLOCK), bias_ref[hs, :, BLOCK:2 * BLOCK].reshape(R, BLOCK),
                           not_first)
            o_ref[hs] = o.reshape(G, BLOCK, HEAD_DIM).astype(o_ref.dtype)

    cur = lambda i: (0, i, 0)
    prev = lambda i: (0, jnp.maximum(i - 1, 0), 0)
    return pl.pallas_call(
        body, name=name, grid=(nb,),
        in_specs=[pl.BlockSpec((N_Q_HEADS, BLOCK, HEAD_DIM), cur),
                  pl.BlockSpec((N_KV_HEADS, BLOCK, HEAD_DIM), prev), pl.BlockSpec((N_KV_HEADS, BLOCK, HEAD_DIM), cur),
                  pl.BlockSpec((N_KV_HEADS, BLOCK, HEAD_DIM), prev), pl.BlockSpec((N_KV_HEADS, BLOCK, HEAD_DIM), cur),
                  pl.BlockSpec((N_Q_HEADS * BLOCK, 128), lambda i: (0, 0)),
                  pl.BlockSpec((N_Q_HEADS, BLOCK, 2 * BLOCK), lambda i: (0, 0, 0))],
        out_specs=pl.BlockSpec((N_Q_HEADS, BLOCK, HEAD_DIM), cur),
        out_shape=jax.ShapeDtypeStruct((N_Q_HEADS, s, HEAD_DIM), BF16),
        compiler_params=_cparams(("arbitrary",)),
    )(qh, kh, kh, vh, vh, sinks, bias)


def _attn_bwd(qh, kh, vh, doh, sinks, bias, name):
    s = qh.shape[1]
    nb = s // BLOCK
    G = GQA_GROUP
    R = G * BLOCK

    def body(q_ref, kp_ref, kc_ref, vp_ref, vc_ref, do_ref, sink_ref, bias_ref,
             dq_ref, dk_ref, dv_ref, dsink_ref, dbias_ref, ck, cv):
        i = pl.program_id(1)

        @pl.when(i == 0)
        def _():
            dsink_ref[...] = jnp.zeros_like(dsink_ref)
            dbias_ref[...] = jnp.zeros_like(dbias_ref)
            ck[...] = jnp.zeros_like(ck)
            cv[...] = jnp.zeros_like(cv)

        @pl.when(i < nb)
        def _():
            not_first = i > 0
            _, vjp = jax.vjp(lambda q, a, b, c, d, sk, e, f: _attn_head(q, a, b, c, d, sk, e, f, not_first),
                             q_ref[...].astype(F32).reshape(R, HEAD_DIM), kp_ref[...].astype(F32),
                             kc_ref[...].astype(F32), vp_ref[...].astype(F32), vc_ref[...].astype(F32),
                             sink_ref[:, 0:1], bias_ref[:, :, 0:BLOCK].reshape(R, BLOCK),
                             bias_ref[:, :, BLOCK:2 * BLOCK].reshape(R, BLOCK))
            dq, dkp, dkc, dvp, dvc, dsk, dbp, dbc = vjp(do_ref[...].reshape(R, HEAD_DIM).astype(F32))
            dq_ref[...] = dq.reshape(G, BLOCK, HEAD_DIM)
            dsink_ref[...] += jnp.broadcast_to(dsk, (R, 128))
            dbias_ref[:, :, 0:BLOCK] += dbp.reshape(G, BLOCK, BLOCK)
            dbias_ref[:, :, BLOCK:2 * BLOCK] += dbc.reshape(G, BLOCK, BLOCK)
            dk_ref[...] = ck[...] + dkp
            dv_ref[...] = cv[...] + dvp
            ck[...] = dkc
            cv[...] = dvc

        @pl.when(i == nb)
        def _():
            dk_ref[...] = ck[...]
            dv_ref[...] = cv[...]

    qcur = lambda kv, i: (kv, jnp.minimum(i, nb - 1), 0)
    kcur = lambda kv, i: (kv, jnp.minimum(i, nb - 1), 0)
    kprev = lambda kv, i: (kv, jnp.clip(i - 1, 0, nb - 1), 0)
    qspec = pl.BlockSpec((G, BLOCK, HEAD_DIM), qcur)
    kc_spec = pl.BlockSpec((None, BLOCK, HEAD_DIM), kcur)
    kp_spec = pl.BlockSpec((None, BLOCK, HEAD_DIM), kprev)
    return pl.pallas_call(
        body, name=name, grid=(N_KV_HEADS, nb + 1),
        in_specs=[qspec, kp_spec, kc_spec, kp_spec, kc_spec, qspec,
                  pl.BlockSpec((R, 128), lambda kv, i: (kv, 0)),
                  pl.BlockSpec((G, BLOCK, 2 * BLOCK), lambda kv, i: (kv, 0, 0))],
        out_specs=[qspec, kp_spec, kp_spec,
                   pl.BlockSpec((R, 128), lambda kv, i: (kv, 0)),
                   pl.BlockSpec((G, BLOCK, 2 * BLOCK), lambda kv, i: (kv, 0, 0))],
        out_shape=[jax.ShapeDtypeStruct((N_Q_HEADS, s, HEAD_DIM), F32),
                   jax.ShapeDtypeStruct((N_KV_HEADS, s, HEAD_DIM), F32),
                   jax.ShapeDtypeStruct((N_KV_HEADS, s, HEAD_DIM), F32),
                   jax.ShapeDtypeStruct((N_Q_HEADS * BLOCK, 128), F32),
                   jax.ShapeDtypeStruct((N_Q_HEADS, BLOCK, 2 * BLOCK), F32)],
        scratch_shapes=[pltpu.VMEM((BLOCK, HEAD_DIM), F32), pltpu.VMEM((BLOCK, HEAD_DIM), F32)],
        compiler_params=_cparams(("arbitrary", "arbitrary")),
    )(qh, kh, kh, vh, vh, doh, sinks, bias)


def _cmul(ar, ai, br, bi):
    return ar * br - ai * bi, ar * bi + ai * br


def _scan_passes(a_ref, b_ref, x_ref, xp_ref, da_ref, *, s, tc, reverse):
    nc = SCAN_CHUNKS
    steps = s // nc
    with_da = xp_ref is not None
    unroll = 8 if steps % 8 == 0 else 1

    def shift(v, d):
        row = lax.broadcasted_iota(jnp.int32, v.shape, 0)
        if reverse:
            return jnp.where(row < nc - d, pltpu.roll(v, nc - d, 0), 0.0)
        return jnp.where(row >= d, pltpu.roll(v, d, 0), 0.0)

    def run():
        ar = jnp.broadcast_to(a_ref[0], (nc, tc))
        ai = jnp.broadcast_to(a_ref[1], (nc, tc))

        def row_of(step):
            j = (steps - 1 - step) if reverse else step
            return pl.multiple_of(j * nc, nc)

        def p1(step, st):
            sr, si = st
            r0 = row_of(step)
            mr, mi = _cmul(ar, ai, sr, si)
            sr = mr + b_ref[0, pl.ds(r0, nc), :]
            si = mi + b_ref[1, pl.ds(r0, nc), :]
            x_ref[0, pl.ds(r0, nc), :] = sr
            x_ref[1, pl.ds(r0, nc), :] = si
            return sr, si
        zero = jnp.zeros((nc, tc), F32)
        er, ei = lax.fori_loop(0, steps, p1, (zero, zero), unroll=unroll)

        pr, pi_ = jnp.ones((nc, tc), F32), zero
        br, bi, left = ar, ai, steps
        while left:
            if left & 1:
                pr, pi_ = _cmul(pr, pi_, br, bi)
            br, bi = _cmul(br, bi, br, bi)
            left >>= 1
        cr, ci = shift(er, 1), shift(ei, 1)
        d = 1
        while d < nc:
            mr, mi = _cmul(pr, pi_, shift(cr, d), shift(ci, d))
            cr, ci = cr + mr, ci + mi
            pr, pi_ = _cmul(pr, pi_, pr, pi_)
            d *= 2

        def p2(step, st):
            qr, qi, dar, dai = st
            r0 = row_of(step)
            qr, qi = _cmul(ar, ai, qr, qi)
            fr, fi = _cmul(qr, qi, cr, ci)
            xr = x_ref[0, pl.ds(r0, nc), :] + fr
            xi = x_ref[1, pl.ds(r0, nc), :] + fi
            x_ref[0, pl.ds(r0, nc), :] = xr
            x_ref[1, pl.ds(r0, nc), :] = xi
            if with_da:
                jm = jnp.where(step == steps - 1, steps - 1, steps - 2 - step)
                rp = pl.multiple_of(jm * nc, nc)
                vr, vi = xp_ref[0, pl.ds(rp, nc), :], xp_ref[1, pl.ds(rp, nc), :]
                row = lax.broadcasted_iota(jnp.int32, (nc, tc), 0)
                first = step == steps - 1
                sel = jnp.logical_and(first, row == 0)
                vr = jnp.where(sel, 0.0, jnp.where(first, pltpu.roll(vr, 1, 0), vr))
                vi = jnp.where(sel, 0.0, jnp.where(first, pltpu.roll(vi, 1, 0), vi))
                dar = dar + xr * vr + xi * vi
                dai = dai + xi * vr - xr * vi
            return qr, qi, dar, dai
        _, _, dar, dai = lax.fori_loop(0, steps, p2, (jnp.ones((nc, tc), F32), zero, zero, zero), unroll=unroll)
        if with_da:
            da_ref[0] = jnp.sum(dar, axis=0, keepdims=True)
            da_ref[1] = jnp.sum(dai, axis=0, keepdims=True)

    run()


def _ssm_fwd(u, bd, cd, a, d_row, *, name, sb, sbn):
    s, w = u.shape
    nst = a.shape[2]
    nblk = w // sb
    rows = min(512, s)
    nn = (((1,), (0,)), ((), ()))

    def body(u_ref, bre_ref, bim_ref, cre_ref, cim_ref, a_ref, d_ref, y_ref, x_ref):

        def fill(r, carry):
            r0 = pl.multiple_of(r * rows, rows)
            ub = u_ref[pl.ds(r0, rows), :].astype(BF16)
            x_ref[0, pl.ds(r0, rows), :] = lax.dot_general(ub, bre_ref[...].astype(BF16), nn, preferred_element_type=F32)
            x_ref[1, pl.ds(r0, rows), :] = lax.dot_general(ub, bim_ref[...].astype(BF16), nn, preferred_element_type=F32)
            return carry
        lax.fori_loop(0, s // rows, fill, 0)
        _scan_passes(a_ref, x_ref, x_ref, None, None, s=s, tc=sbn, reverse=False)

        def project(r, carry):
            r0 = pl.multiple_of(r * rows, rows)
            y = lax.dot_general(x_ref[0, pl.ds(r0, rows), :].astype(BF16), cre_ref[...].astype(BF16), nn, preferred_element_type=F32)
            y = y + lax.dot_general(x_ref[1, pl.ds(r0, rows), :].astype(BF16), cim_ref[...].astype(BF16), nn, preferred_element_type=F32)
            y_ref[pl.ds(r0, rows), :] = y + d_ref[...] * u_ref[pl.ds(r0, rows), :]
            return carry
        lax.fori_loop(0, s // rows, project, 0)

    return pl.pallas_call(
        body, name=name, grid=(nblk,),
        in_specs=[pl.BlockSpec((s, sb), lambda j: (0, j)),
                  pl.BlockSpec((sb, sbn), lambda j: (j, j)), pl.BlockSpec((sb, sbn), lambda j: (j, nblk + j)),
                  pl.BlockSpec((sbn, sb), lambda j: (j, j)), pl.BlockSpec((sbn, sb), lambda j: (nblk + j, j)),
                  pl.BlockSpec((2, 1, sbn), lambda j: (0, 0, j)), pl.BlockSpec((1, sb), lambda j: (0, j))],
        out_specs=[pl.BlockSpec((s, sb), lambda j: (0, j)), pl.BlockSpec((2, s, sbn), lambda j: (0, 0, j))],
        out_shape=[jax.ShapeDtypeStruct((s, w), F32), jax.ShapeDtypeStruct((2, s, nst), F32)],
        compiler_params=pltpu.CompilerParams(dimension_semantics=("arbitrary",), vmem_limit_bytes=SSM_VMEM_LIMIT_BYTES),
    )(u, bd, bd, cd, cd, a, d_row)


def _ssm_bwd(dy, u, xs, bd, cd, a, d_row, *, name, sb, sbn):
    s, w = u.shape
    nst = a.shape[2]
    nblk = w // sb
    rows = min(512, s)
    nt = (((1,), (1,)), ((), ()))
    tn = (((0,), (0,)), ((), ()))

    def body(dy_ref, u_ref, xs_hbm, bre_ref, bim_ref, cre_ref, cim_ref, a_ref, d_ref,
             du_ref, gb_ref, gc_ref, da_ref, gd_ref, lam, xs_ref, sem):
        j = pl.program_id(0)
        fetch = pltpu.make_async_copy(xs_hbm.at[:, :, pl.ds(pl.multiple_of(j * sbn, sbn), sbn)], xs_ref, sem)
        fetch.start()

        def fill(r, carry):
            r0 = pl.multiple_of(r * rows, rows)
            dyb = dy_ref[pl.ds(r0, rows), :].astype(BF16)
            lam[0, pl.ds(r0, rows), :] = lax.dot_general(dyb, cre_ref[...].astype(BF16), nt, preferred_element_type=F32)
            lam[1, pl.ds(r0, rows), :] = lax.dot_general(dyb, cim_ref[...].astype(BF16), nt, preferred_element_type=F32)
            return carry
        lax.fori_loop(0, s // rows, fill, 0)
        fetch.wait()
        _scan_passes(a_ref, lam, lam, xs_ref, da_ref, s=s, tc=sbn, reverse=True)
        gb_ref[...] = jnp.zeros_like(gb_ref)
        gc_ref[...] = jnp.zeros_like(gc_ref)
        gd_ref[...] = jnp.zeros_like(gd_ref)

        def project(r, carry):
            r0 = pl.multiple_of(r * rows, rows)
            dyv, uv = dy_ref[pl.ds(r0, rows), :], u_ref[pl.ds(r0, rows), :]
            dyb, ub = dyv.astype(BF16), uv.astype(BF16)
            lr, li = lam[0, pl.ds(r0, rows), :].astype(BF16), lam[1, pl.ds(r0, rows), :].astype(BF16)
            du = lax.dot_general(lr, bre_ref[...].astype(BF16), nt, preferred_element_type=F32)
            du = du + lax.dot_general(li, bim_ref[...].astype(BF16), nt, preferred_element_type=F32)
            du_ref[pl.ds(r0, rows), :] = du + d_ref[...] * dyv
            gb_ref[:, 0:sbn] += lax.dot_general(ub, lr, tn, preferred_element_type=F32)
            gb_ref[:, sbn:2 * sbn] += lax.dot_general(ub, li, tn, preferred_element_type=F32)
            gc_ref[0] += lax.dot_general(xs_ref[0, pl.ds(r0, rows), :].astype(BF16), dyb, tn, preferred_element_type=F32)
            gc_ref[1] += lax.dot_general(xs_ref[1, pl.ds(r0, rows), :].astype(BF16), dyb, tn, preferred_element_type=F32)
            gd_ref[...] += jnp.sum(dyv * uv, axis=0, keepdims=True)
            return carry
        lax.fori_loop(0, s // rows, project, 0)

    col = lambda j: (0, j)
    return pl.pallas_call(
        body, name=name, grid=(nblk,),
        in_specs=[pl.BlockSpec((s, sb), col), pl.BlockSpec((s, sb), col), pl.BlockSpec(memory_space=pl.ANY),
                  pl.BlockSpec((sb, sbn), lambda j: (j, j)), pl.BlockSpec((sb, sbn), lambda j: (j, nblk + j)),
                  pl.BlockSpec((sbn, sb), lambda j: (j, j)), pl.BlockSpec((sbn, sb), lambda j: (nblk + j, j)),
                  pl.BlockSpec((2, 1, sbn), lambda j: (0, 0, j)), pl.BlockSpec((1, sb), col)],
        out_specs=[pl.BlockSpec((s, sb), col), pl.BlockSpec((sb, 2 * sbn), lambda j: (j, 0)),
                   pl.BlockSpec((2, sbn, sb), lambda j: (0, j, 0)), pl.BlockSpec((2, 1, sbn), lambda j: (0, 0, j)),
                   pl.BlockSpec((1, sb), col)],
        out_shape=[jax.ShapeDtypeStruct((s, w), F32), jax.ShapeDtypeStruct((w, 2 * sbn), F32),
                   jax.ShapeDtypeStruct((2, nst, sb), F32), jax.ShapeDtypeStruct((2, 1, nst), F32),
                   jax.ShapeDtypeStruct((1, w), F32)],
        scratch_shapes=[pltpu.VMEM((2, s, sbn), F32), pltpu.VMEM((2, s, sbn), F32), pltpu.SemaphoreType.DMA],
        compiler_params=pltpu.CompilerParams(dimension_semantics=("arbitrary",), vmem_limit_bytes=SSM_VMEM_LIMIT_BYTES),
    )(dy, u, xs, bd, bd, cd, cd, a, d_row)


def _adamw_math(w, g, m, v):
    nm = ADAM_B1 * m + (1.0 - ADAM_B1) * g
    nv = ADAM_B2 * v + (1.0 - ADAM_B2) * (g * g)
    m_hat = nm / (1.0 - ADAM_B1 ** ADAM_STEP)
    v_hat = nv / (1.0 - ADAM_B2 ** ADAM_STEP)
    return -ADAM_LR * (m_hat / (jnp.sqrt(v_hat) + ADAM_EPS) + ADAM_WD * w), nm, nv


def _adamw_many(ws, gs, ms, vs, name, deps=()):
    n, nd = len(ws), len(deps)

    def body(*refs):
        outs = refs[4 * n + nd:]
        for i in range(n):
            d, nm, nv = _adamw_math(refs[i][...], refs[n + i][...], refs[2 * n + i][...], refs[3 * n + i][...])
            outs[i][...], outs[n + i][...], outs[2 * n + i][...] = d, nm, nv

    whole = pl.BlockSpec(memory_space=pltpu.VMEM)
    res = pl.pallas_call(
        body, name=name, in_specs=[whole] * (4 * n) + [pl.BlockSpec(memory_space=pl.ANY)] * nd,
        out_specs=[whole] * (3 * n), out_shape=[jax.ShapeDtypeStruct(w.shape, F32) for w in ws] * 3,
        compiler_params=pltpu.CompilerParams(vmem_limit_bytes=VMEM_LIMIT_BYTES),
    )(*ws, *gs, *ms, *vs, *deps)
    return res[:n], res[n:2 * n], res[2 * n:]


def _adamw(w, g, m, v, name, deps=()):
    nd = len(deps)
    r, c = w.shape
    tr = r
    for cand in (512, 256, 128, 64, 32, 16, 8):
        if r % cand == 0 and cand * c * 4 <= 2 * 1024 * 1024:
            tr = cand
            break

    def body(w_ref, g_ref, m_ref, v_ref, *rest):
        d_ref, nm_ref, nv_ref = rest[nd:]
        d_ref[...], nm_ref[...], nv_ref[...] = _adamw_math(w_ref[...], g_ref[...], m_ref[...], v_ref[...])

    spec = pl.BlockSpec((tr, c), lambda i: (i, 0))
    sds = jax.ShapeDtypeStruct((r, c), F32)
    return pl.pallas_call(body, name=name, grid=(r // tr,),
                          in_specs=[spec] * 4 + [pl.BlockSpec(memory_space=pl.ANY)] * nd, out_specs=[spec] * 3,
                          out_shape=[sds] * 3, compiler_params=_cparams(("parallel",)))(w, g, m, v, *deps)


def _sum_lead(x, name, out_dtype=F32):
    n, r, c = x.shape
    tr = r
    for cand in (512, 256, 128, 64, 32, 16, 8):
        if r % cand == 0 and n * cand * c * 4 <= 4 * 1024 * 1024:
            tr = cand
            break

    def body(x_ref, o_ref):
        acc = x_ref[0].astype(F32)
        for k in range(1, n):
            acc = acc + x_ref[k].astype(F32)
        o_ref[...] = acc.astype(o_ref.dtype)

    return pl.pallas_call(body, name=name, grid=(r // tr,),
                          in_specs=[pl.BlockSpec((n, tr, c), lambda i: (0, i, 0))],
                          out_specs=pl.BlockSpec((tr, c), lambda i: (i, 0)),
                          out_shape=jax.ShapeDtypeStruct((r, c), out_dtype),
                          compiler_params=_cparams(("parallel",)))(x)


def _row_tile(rows, row_bytes, budget, least=8):
    for cand in (1024, 512, 256, 128, 64, 32, 16, 8):
        if cand >= least and rows % cand == 0 and cand * row_bytes <= budget:
            return cand
    return rows


def _cast_into_slot(w, slot, name):
    r, c = w.shape
    tr = _row_tile(r, c * 4, 4 * 1024 * 1024, least=16)

    def body(slot_ref, w_ref, o_ref):
        o_ref[...] = w_ref[...].astype(o_ref.dtype)

    gs = pltpu.PrefetchScalarGridSpec(
        num_scalar_prefetch=1, grid=(r // tr,),
        in_specs=[pl.BlockSpec((tr, c), lambda i, s: (i, 0))],
        out_specs=pl.BlockSpec((None, tr, c), lambda i, s: (s[0], i, 0)))
    return pl.pallas_call(body, name=name, grid_spec=gs, out_shape=jax.ShapeDtypeStruct((N_CHIPS, r, c), BF16),
                          compiler_params=_cparams(("parallel",)))(slot, w)


def _sum_own(p, t, sel, name):
    _, h, c = p.shape
    tr = _row_tile(h, c * 4, 2 * 1024 * 1024, least=16)
    nblk = h // tr

    def body(sel_ref, p_ref, t_ref, o_ref):
        acc = p_ref[...].astype(F32)
        for k in range(3):
            acc = acc + t_ref[k].astype(F32)
        o_ref[...] = acc

    gs = pltpu.PrefetchScalarGridSpec(
        num_scalar_prefetch=1, grid=(nblk,),
        in_specs=[pl.BlockSpec((None, tr, c), lambda i, s: (s[0], i, 0)),
                  pl.BlockSpec((3, tr, c), lambda i, s: (0, i, 0))],
        out_specs=pl.BlockSpec((tr, c), lambda i, s: (s[1] * nblk + i, 0)))
    return pl.pallas_call(body, name=name, grid_spec=gs, out_shape=jax.ShapeDtypeStruct((2 * h, c), F32),
                          compiler_params=_cparams(("parallel",)))(sel, p, t)


def _add_half(g, t, half, name):
    n, r, c = g.shape
    h = r // 2
    tr = h
    for cand in (512, 256, 128, 64, 32, 16):
        if h % cand == 0 and cand * c * 2 <= 2 * 1024 * 1024:
            tr = cand
            break
    nblk = h // tr

    def body(half_ref, g_ref, t_ref, o_ref):
        o_ref[...] = (g_ref[...].astype(F32) + t_ref[...].astype(F32)).astype(o_ref.dtype)

    gs = pltpu.PrefetchScalarGridSpec(
        num_scalar_prefetch=1, grid=(n, nblk),
        in_specs=[pl.BlockSpec((None, tr, c), lambda j, i, hr: (j, hr[0] * nblk + i, 0)),
                  pl.BlockSpec((None, tr, c), lambda j, i, hr: (j, i, 0))],
        out_specs=pl.BlockSpec((None, tr, c), lambda j, i, hr: (j, i, 0)))
    return pl.pallas_call(body, name=name, grid_spec=gs, out_shape=jax.ShapeDtypeStruct((n, h, c), BF16),
                          compiler_params=_cparams(("parallel", "parallel")))(half, g, t)


def _position():
    x, y, c = lax.axis_index("x"), lax.axis_index("y"), lax.axis_index("c")
    return x, y, c


def _allgather8(xs, name):
    m_per, n = xs.shape

    def body(x_ref, out_ref, send_sems, recv_sems, local_sem):
        x, y, c = _position()
        me, sibling = (x, y, c), (x, y, 1 - c)
        chips = [(1 - x, y), (x, 1 - y), (1 - x, 1 - y)]

        def rows(px, py, pc):
            return out_ref.at[pl.ds((4 * px + 2 * py + pc) * m_per, m_per), :]

        def copy(k, block, to, src=None):
            return pltpu.make_async_remote_copy(
                src_ref=rows(*block) if src is None else src, dst_ref=rows(*block),
                send_sem=send_sems.at[k], recv_sem=recv_sems.at[k], device_id=to, device_id_type=MESH)

        mine = pltpu.make_async_copy(x_ref, rows(*me), local_sem)
        mine.start()
        first = [copy(0, me, sibling, src=x_ref)]
        first += [copy(1 + j, me, (*chip, c), src=x_ref) for j, chip in enumerate(chips)]
        for cp in first:
            cp.start()
        passed = [copy(4 + j, (*chip, c), sibling) for j, chip in enumerate(chips)]
        for j, chip in enumerate(chips):
            copy(1 + j, (*chip, c), me).wait_recv()
            passed[j].start()
        copy(0, sibling, me).wait_recv()
        for j, chip in enumerate(chips):
            copy(4 + j, (*chip, 1 - c), me).wait_recv()
        for cp in first + passed:
            cp.wait_send()
        mine.wait()

    return pl.pallas_call(
        body, name=name, out_shape=jax.ShapeDtypeStruct((N_DEV * m_per, n), xs.dtype),
        in_specs=[pl.BlockSpec(memory_space=pltpu.VMEM)], out_specs=pl.BlockSpec(memory_space=pltpu.VMEM),
        scratch_shapes=[pltpu.SemaphoreType.DMA((7,)), pltpu.SemaphoreType.DMA((7,)), pltpu.SemaphoreType.DMA],
        compiler_params=pltpu.CompilerParams(vmem_limit_bytes=VMEM_LIMIT_BYTES),
    )(xs)


_HBM = pl.BlockSpec(memory_space=pltpu.HBM)


_SEM = pl.BlockSpec(memory_space=pltpu.SEMAPHORE)
_ANY = pl.BlockSpec(memory_space=pl.ANY)
_EFFECT = pltpu.SideEffectType.DATAFLOW_SIDE_EFFECTING


def _in_hbm(a):
    return pltpu.with_memory_space_constraint(a, pltpu.HBM)


def _several(after):
    return list(after) if isinstance(after, (list, tuple)) else [after]


def _gather_start(ws, groups, after, name):
    n = len(ws)
    after = _several(after)

    def body(*refs):
        in_refs = refs[:n]
        sems, token = refs[2 * n + len(after):-1], refs[-1]
        x, y, c = _position()
        mychip = 2 * x + y
        chips = [(1 - x, y), (x, 1 - y), (1 - x, 1 - y)]
        for g, members in enumerate(groups):
            for k, i in enumerate(members):
                h = ws[i].shape[1] // 2
                mine = in_refs[i].at[mychip, pl.ds(c * h, h), :]
                for j, (px, py) in enumerate(chips):
                    pltpu.make_async_remote_copy(
                        src_ref=mine, dst_ref=mine, send_sem=sems[2 * g].at[3 * k + j],
                        recv_sem=sems[2 * g + 1].at[3 * k + j], device_id=(px, py, c), device_id_type=MESH).start()
        token[...] = jnp.zeros_like(token)

    sem_shapes = [pltpu.SemaphoreType.DMA((3 * len(m),)) for m in groups for _ in range(2)]
    res = pl.pallas_call(
        body, name=name,
        out_shape=[pltpu.HBM(w.shape, w.dtype) for w in ws] + sem_shapes + [jax.ShapeDtypeStruct((8, 128), F32)],
        in_specs=[_HBM] * n + [_ANY] * len(after),
        out_specs=[_HBM] * n + [_SEM] * len(sem_shapes) + [pl.BlockSpec(memory_space=pltpu.VMEM)],
        input_output_aliases={i: i for i in range(n)},
        compiler_params=pltpu.CompilerParams(has_side_effects=_EFFECT),
    )(*[_in_hbm(w) for w in ws], *after)
    bufs, sems, token = res[:n], res[n:-1], res[-1]
    return list(bufs), [(sems[2 * g], sems[2 * g + 1]) for g in range(len(groups))], token


def _gather_wait(bufs, send_sems, recv_sems, after, name):
    m = len(bufs)

    def body(*refs):
        in_refs = refs[:m]
        send, recv = refs[m], refs[m + 1]
        x, y, c = _position()
        mychip = 2 * x + y
        chips = [(1 - x, y), (x, 1 - y), (1 - x, 1 - y)]
        for k in range(m):
            h = bufs[k].shape[1] // 2
            mine = in_refs[k].at[mychip, pl.ds(c * h, h), :]
            for j, (px, py) in enumerate(chips):
                cp = pltpu.make_async_remote_copy(
                    src_ref=mine, dst_ref=in_refs[k].at[2 * px + py, pl.ds(c * h, h), :],
                    send_sem=send.at[3 * k + j], recv_sem=recv.at[3 * k + j],
                    device_id=(px, py, c), device_id_type=MESH)
                cp.wait_send()
                cp.wait_recv()

    res = pl.pallas_call(
        body, name=name, out_shape=[pltpu.HBM(b.shape, b.dtype) for b in bufs],
        in_specs=[_HBM] * m + [_SEM, _SEM] + [_ANY] * len(_several(after)), out_specs=[_HBM] * m,
        input_output_aliases={k: k for k in range(m)},
        compiler_params=pltpu.CompilerParams(has_side_effects=_EFFECT),
    )(*bufs, send_sems, recv_sems, *_several(after))
    return list(res)


def _forward_halves(ws, name):
    n = len(ws)

    def body(*refs):
        out_refs = refs[n:2 * n]
        send_sems, recv_sems = refs[2 * n:]
        x, y, c = _position()
        me, sibling = (x, y, c), (x, y, 1 - c)
        chips = [(1 - x, y), (x, 1 - y), (1 - x, 1 - y)]
        cps = []
        for i in range(n):
            h = ws[i].shape[1] // 2
            for j, (px, py) in enumerate(chips):
                got = out_refs[i].at[2 * px + py, pl.ds(c * h, h), :]
                cp = pltpu.make_async_remote_copy(
                    src_ref=got, dst_ref=got, send_sem=send_sems.at[3 * i + j], recv_sem=recv_sems.at[3 * i + j],
                    device_id=sibling, device_id_type=MESH)
                cp.start()
                cps.append(cp)
        for i in range(n):
            h = ws[i].shape[1] // 2
            for j, (px, py) in enumerate(chips):
                other = out_refs[i].at[2 * px + py, pl.ds((1 - c) * h, h), :]
                pltpu.make_async_remote_copy(
                    src_ref=other, dst_ref=other, send_sem=send_sems.at[3 * i + j], recv_sem=recv_sems.at[3 * i + j],
                    device_id=me, device_id_type=MESH).wait_recv()
        for cp in cps:
            cp.wait_send()

    return pl.pallas_call(
        body, name=name,
        out_shape=[jax.ShapeDtypeStruct(w.shape, w.dtype) for w in ws],
        in_specs=[_HBM] * n, out_specs=[_HBM] * n, input_output_aliases={i: i for i in range(n)},
        scratch_shapes=[pltpu.SemaphoreType.DMA((3 * n,)), pltpu.SemaphoreType.DMA((3 * n,))],
    )(*ws)


def _swap_halves(gs, name):
    n = len(gs)

    def body(*refs):
        in_refs, out_refs = refs[:n], refs[n:2 * n]
        send_sems, recv_sems = refs[2 * n:]
        x, y, c = _position()
        cps = []
        for i in range(n):
            h = gs[i].shape[1] // 2
            cp = pltpu.make_async_remote_copy(
                src_ref=in_refs[i].at[:, pl.ds((1 - c) * h, h), :], dst_ref=out_refs[i],
                send_sem=send_sems.at[i], recv_sem=recv_sems.at[i], device_id=(x, y, 1 - c), device_id_type=MESH)
            cp.start()
            cps.append(cp)
        for cp in cps:
            cp.wait()

    return pl.pallas_call(
        body, name=name,
        out_shape=[jax.ShapeDtypeStruct((g.shape[0], g.shape[1] // 2, g.shape[2]), g.dtype) for g in gs],
        in_specs=[_HBM] * n, out_specs=[_HBM] * n,
        scratch_shapes=[pltpu.SemaphoreType.DMA((n,)), pltpu.SemaphoreType.DMA((n,))],
    )(*gs)


def _copies_start(arrays, copies, nsem, after, name):
    n = len(arrays)
    after = _several(after)
    first = 2 * n + len(after)

    def body(*refs):
        for cp in copies(refs[:n], refs[first], refs[first + 1]):
            cp.start()
        refs[first + 2][...] = jnp.zeros_like(refs[first + 2])

    res = pl.pallas_call(
        body, name=name,
        out_shape=[pltpu.HBM(a.shape, a.dtype) for a in arrays]
        + [pltpu.SemaphoreType.DMA((nsem,)), pltpu.SemaphoreType.DMA((nsem,)), jax.ShapeDtypeStruct((8, 128), F32)],
        in_specs=[_HBM] * n + [_ANY] * len(after),
        out_specs=[_HBM] * n + [_SEM, _SEM, pl.BlockSpec(memory_space=pltpu.VMEM)],
        input_output_aliases={i: i for i in range(n)},
        compiler_params=pltpu.CompilerParams(has_side_effects=_EFFECT),
    )(*[_in_hbm(a) for a in arrays], *after)
    return list(res[:n]), res[n], res[n + 1], res[n + 2]


def _copies_wait(arrays, copies, send_sems, recv_sems, after, name):
    n = len(arrays)

    def body(*refs):
        for cp in copies(refs[:n], refs[n], refs[n + 1]):
            cp.wait_send()
            cp.wait_recv()

    res = pl.pallas_call(
        body, name=name, out_shape=[pltpu.HBM(a.shape, a.dtype) for a in arrays],
        in_specs=[_HBM] * n + [_SEM, _SEM] + [_ANY] * len(_several(after)), out_specs=[_HBM] * n,
        input_output_aliases={i: i for i in range(n)},
        compiler_params=pltpu.CompilerParams(has_side_effects=_EFFECT),
    )(*arrays, send_sems, recv_sems, *_several(after))
    return list(res)


def _scatter_copies(refs, send, recv):
    n = len(refs) // 2
    x, y, c = _position()
    chips = [(1 - x, y), (x, 1 - y), (1 - x, 1 - y)]
    return [pltpu.make_async_remote_copy(
        src_ref=refs[i].at[2 * px + py], dst_ref=refs[n + i].at[j],
        send_sem=send.at[3 * i + j], recv_sem=recv.at[3 * i + j], device_id=(px, py, c), device_id_type=MESH)
        for i in range(n) for j, (px, py) in enumerate(chips)]


def _swap_copies(refs, send, recv):
    n = len(refs) // 2
    x, y, c = _position()
    cps = []
    for i in range(n):
        h = refs[i].shape[1] // 2
        cps.append(pltpu.make_async_remote_copy(
            src_ref=refs[i].at[:, pl.ds((1 - c) * h, h), :], dst_ref=refs[n + i],
            send_sem=send.at[i], recv_sem=recv.at[i], device_id=(x, y, 1 - c), device_id_type=MESH))
    return cps


def _join_copies(refs, send, recv):
    x, y, c = _position()
    cps = []
    for i, r in enumerate(refs):
        h = r.shape[0] // 2
        mine = r.at[pl.ds(c * h, h), :]
        cps.append(pltpu.make_async_remote_copy(
            src_ref=mine, dst_ref=mine, send_sem=send.at[i], recv_sem=recv.at[i],
            device_id=(x, y, 1 - c), device_id_type=MESH))
    return cps


def _forward_copies(refs, send, recv):
    x, y, c = _position()
    chips = [(1 - x, y), (x, 1 - y), (1 - x, 1 - y)]
    cps = []
    for i, r in enumerate(refs):
        h = r.shape[1] // 2
        for j, (px, py) in enumerate(chips):
            got = r.at[2 * px + py, pl.ds(c * h, h), :]
            cps.append(pltpu.make_async_remote_copy(
                src_ref=got, dst_ref=got, send_sem=send.at[3 * i + j], recv_sem=recv.at[3 * i + j],
                device_id=(x, y, 1 - c), device_id_type=MESH))
    return cps


def _join_halves(rs, name):
    n = len(rs)

    def body(*refs):
        out_refs = refs[n:2 * n]
        send_sems, recv_sems = refs[2 * n:]
        x, y, c = _position()
        cps = []
        for i in range(n):
            h = rs[i].shape[0] // 2
            mine = out_refs[i].at[pl.ds(c * h, h), :]
            cp = pltpu.make_async_remote_copy(
                src_ref=mine, dst_ref=mine, send_sem=send_sems.at[i], recv_sem=recv_sems.at[i],
                device_id=(x, y, 1 - c), device_id_type=MESH)
            cp.start()
            cps.append(cp)
        for i in range(n):
            h = rs[i].shape[0] // 2
            other = out_refs[i].at[pl.ds((1 - c) * h, h), :]
            pltpu.make_async_remote_copy(
                src_ref=other, dst_ref=other, send_sem=send_sems.at[i], recv_sem=recv_sems.at[i],
                device_id=(x, y, c), device_id_type=MESH).wait_recv()
        for cp in cps:
            cp.wait_send()

    return pl.pallas_call(
        body, name=name,
        out_shape=[jax.ShapeDtypeStruct(r.shape, r.dtype) for r in rs],
        in_specs=[_HBM] * n, out_specs=[_HBM] * n, input_output_aliases={i: i for i in range(n)},
        scratch_shapes=[pltpu.SemaphoreType.DMA((n,)), pltpu.SemaphoreType.DMA((n,))],
    )(*rs)


def _t5_buckets_block():
    qi = np.arange(BLOCK)[:, None]
    ki = np.arange(2 * BLOCK)[None, :]
    n = np.maximum(qi + BLOCK - ki, 0)
    max_exact = NUM_BUCKETS // 2
    large = max_exact + (np.log(np.maximum(n, 1) / max_exact) / np.log(MAX_DISTANCE / max_exact)
                         * (NUM_BUCKETS - max_exact)).astype(np.int32)
    large = np.minimum(large, NUM_BUCKETS - 1)
    return np.where(n < max_exact, n, large).astype(np.int32)


def _discretise(lambda_re, lambda_im, log_step, b_re, b_im):
    lam_re = jnp.minimum(lambda_re, -1e-4)
    lam_im = lambda_im
    delta = jnp.exp(log_step)[:, None]
    mag = jnp.exp(lam_re * delta)
    ang = lam_im * delta
    abar_re, abar_im = mag * jnp.cos(ang), mag * jnp.sin(ang)
    num_re, num_im = abar_re - 1.0, abar_im
    den = lam_re * lam_re + lam_im * lam_im
    f_re = (num_re * lam_re + num_im * lam_im) / den
    f_im = (num_im * lam_re - num_re * lam_im) / den
    bbar_re = f_re[..., None] * b_re - f_im[..., None] * b_im
    bbar_im = f_re[..., None] * b_im + f_im[..., None] * b_re
    return abar_re, abar_im, bbar_re, bbar_im


def _interleave(v, nc):
    s, w = v.shape
    return v.reshape(nc, s // nc, w).transpose(1, 0, 2).reshape(s, w)


def _deinterleave(v, nc):
    s, w = v.shape
    return v.reshape(s // nc, nc, w).transpose(1, 0, 2).reshape(s, w)


_SMALL = ("norm1_g", "b_in", "attn_sinks", "rel_bias", "lambda_re", "lambda_im", "log_step", "ssm_b_re",
          "ssm_b_im", "ssm_c_re", "ssm_c_im", "ssm_d", "b_glu", "norm2_g", "final_g")


def _pack(parts):
    rows = []
    for p in parts:
        f = p.reshape(-1).astype(F32)
        pad = (-f.shape[0]) % 128
        rows.append(jnp.pad(f, (0, pad)).reshape(-1, 128))
    out = jnp.concatenate(rows, axis=0)
    pad = (-out.shape[0]) % 256
    return jnp.pad(out, ((0, pad), (0, 0)))


def _unpack(packed, shapes):
    res, r = [], 0
    for shp in shapes:
        size = int(np.prod(shp))
        nr = -(-size // 128)
        res.append(packed[r:r + nr].reshape(-1)[:size].reshape(shp))
        r += nr
    return res


def kernel(x, c, w_ada, b_ada, norm1_g, w_in, b_in, attn_sinks, rel_bias, lambda_re, lambda_im, log_step, ssm_b_re, ssm_b_im, ssm_c_re, ssm_c_im, ssm_d, w_glu, b_glu, w_attn_proj, w_ssm_proj, w_out, norm2_g, w_ff1, w_ff2, final_g, loss_target, m_w_ada, m_b_ada, m_norm1_g, m_w_in, m_b_in, m_attn_sinks, m_rel_bias, m_lambda_re, m_lambda_im, m_log_step, m_ssm_b_re, m_ssm_b_im, m_ssm_c_re, m_ssm_c_im, m_ssm_d, m_w_glu, m_b_glu, m_w_attn_proj, m_w_ssm_proj, m_w_out, m_norm2_g, m_w_ff1, m_w_ff2, m_final_g, v_w_ada, v_b_ada, v_norm1_g, v_w_in, v_b_in, v_attn_sinks, v_rel_bias, v_lambda_re, v_lambda_im, v_log_step, v_ssm_b_re, v_ssm_b_im, v_ssm_c_re, v_ssm_c_im, v_ssm_d, v_w_glu, v_b_glu, v_w_attn_proj, v_w_ssm_proj, v_w_out, v_norm2_g, v_w_ff1, v_w_ff2, v_final_g):
    given = dict(locals())
    S, D = x.shape[1], x.shape[2]
    SSM_W = w_glu.shape[2]
    G = SSM_W // SSM_GROUP_CH
    NST = G * SSM_STATE
    DFF = w_ff2.shape[1] * N_CHIPS
    INW = w_in.shape[2] * N_CHIPS
    o_q, o_k, o_v, o_u = 0, ATTN_WIDTH, ATTN_WIDTH + KV_WIDTH, ATTN_WIDTH + 2 * KV_WIDTH
    o_ga, o_gs = o_u + SSM_W, o_u + SSM_W + D
    mx, my, mc = _position()
    my_chip = 2 * mx + my
    my_b = 4 * mx + 2 * my + mc

    xv, tgt = x[0], loss_target[0]

    big = dict(w_in=w_in[0], w_glu=w_glu[0], w_attn_proj=w_attn_proj[0], w_ssm_proj=w_ssm_proj[0],
               w_out=w_out[0], w_ff1=w_ff1[0], w_ff2=w_ff2[0])
    big_names = list(big)
    colsharded = {"w_in", "w_attn_proj", "w_ssm_proj", "w_ff1"}
    chip_sel = my_chip.astype(jnp.int32).reshape(1)
    gather_groups = [["w_in"], ["w_attn_proj", "w_ssm_proj", "w_glu", "w_out"], ["w_ff1", "w_ff2"]]
    in_flight, gather_sems, gathered = {}, [], {}

    def finish_gather(g, after):
        bufs = [in_flight[k] for k in gather_groups[g]]
        bufs = _gather_wait(bufs, gather_sems[g][0], gather_sems[g][1], after, "gather_wait_%d" % g)
        gathered.update(zip(gather_groups[g], _forward_halves(bufs, "gather_forward_%d" % g)))

    def tied(v, token):
        return v + token[0:1, 0:1]

    def all_of(*arrays):
        return list(arrays)

    def wop(k):
        g = gathered[k]
        return _Op(g, N_CHIPS) if k in colsharded else _Op(g.reshape(g.shape[0] * g.shape[1], g.shape[2]))

    grads = {}
    nothing = jnp.zeros((8, 128), F32)
    half = mc.astype(jnp.int32).reshape(1)
    sel = jnp.stack([my_chip, mc]).astype(jnp.int32)

    def rs_swap(tag, named):
        keys, gl = list(named), []
        for k in keys:
            gk = named[k]
            if k not in colsharded:
                gk = gk.reshape(N_CHIPS, gk.shape[0] // N_CHIPS, gk.shape[1])
            gl.append(gk)
        lands = [lax.empty((g.shape[0], g.shape[1] // 2, g.shape[2]), g.dtype) for g in gl]
        arrays, ssem, rsem, token = _copies_start(gl + lands, _swap_copies, len(gl), nothing, "rs_swap_start_" + tag)
        return (keys, arrays, ssem, rsem), token

    def rs_scatter(tag, state, after):
        keys, arrays, ssem, rsem = state
        arrays = _copies_wait(arrays, _swap_copies, ssem, rsem, after, "rs_swap_wait_" + tag)
        n = len(keys)
        ps = [_add_half(g, t, half, "rs_add_" + k) for g, t, k in zip(arrays[:n], arrays[n:], keys)]
        lands = [lax.empty((3,) + p.shape[1:], p.dtype) for p in ps]
        arrays, ssem, rsem, token = _copies_start(ps + lands, _scatter_copies, 3 * n, nothing, "rs_start_" + tag)
        return (keys, arrays, ssem, rsem), token

    def rs_sum(tag, state, after):
        keys, arrays, ssem, rsem = state
        arrays = _copies_wait(arrays, _scatter_copies, ssem, rsem, after, "rs_wait_" + tag)
        n = len(keys)
        rs = [_sum_own(p, t, sel, "rs_sum_" + k) for p, t, k in zip(arrays[:n], arrays[n:], keys)]
        rs, ssem, rsem, token = _copies_start(rs, _join_copies, n, nothing, "rs_join_start_" + tag)
        return (keys, rs, ssem, rsem), token

    def rs_finish(tag, state, after):
        keys, rs, ssem, rsem = state
        for k, f in zip(keys, _copies_wait(rs, _join_copies, ssem, rsem, after, "rs_join_wait_" + tag)):
            grads[k] = f[None]

    c_all = _allgather8(jnp.pad(c, ((0, 7), (0, 0))), "gather_c").reshape(N_DEV, 8, D)[:, 0]
    c16 = jnp.pad(c_all, ((0, 8), (0, 0)))
    b_ada_mine = lax.dynamic_slice(b_ada.reshape(N_CHIPS, -1), (my_chip, 0), (1, w_ada.shape[2]))
    mod_sh = _mm(c16, w_ada[0], "NN", name="mod", M=16, N=w_ada.shape[2], K=D, a_fn=_silu,
                 epilogue=lambda acc, b: (acc + b,), extras=[(b_ada_mine, "row")])
    mod_all = _allgather8(mod_sh[:8], "gather_mod").reshape(N_DEV, 8, -1)
    mod_row = jnp.concatenate(
        [lax.dynamic_slice(mod_all, (2 * j, my_b, 0), (1, 1, mod_all.shape[2]))[0] for j in range(N_CHIPS)], axis=1)
    sh1, sc1, g1, sh2, sc2, g2 = [mod_row[:, i * D:(i + 1) * D] for i in range(6)]

    first = [_cast_into_slot(big["w_in"], chip_sel, "cast_w_in")]
    first, sems_first, token_first = _gather_start(first, [[0]], mod_all, "gather_start_in")
    rest_names = gather_groups[1] + gather_groups[2]
    rest = [_cast_into_slot(big[k], chip_sel, "cast_" + k) for k in rest_names]
    rest, sems_rest, token_rest = _gather_start(
        rest, [[rest_names.index(k) for k in grp] for grp in gather_groups[1:]], token_first, "gather_start_rest")
    in_flight.update(zip(["w_in"] + rest_names, first + rest))
    gather_sems.extend(sems_first + sems_rest)

    disc_in = (lambda_re[0], lambda_im[0], log_step[0], ssm_b_re[0], ssm_b_im[0])
    (abar_re, abar_im, bbar_re, bbar_im), disc_vjp = jax.vjp(_discretise, *disc_in)
    same_group = jnp.asarray(np.arange(SSM_W)[:, None] // SSM_GROUP_CH == np.arange(NST)[None, :] // SSM_STATE)

    def block_diag(t):
        return jnp.where(same_group, jnp.tile(t, (G, 1)), 0.0)

    bd = jnp.concatenate([block_diag(bb.transpose(2, 0, 1).reshape(SSM_GROUP_CH, NST)) for bb in (bbar_re, bbar_im)],
                         axis=1)
    cd = jnp.concatenate([block_diag(cc.transpose(1, 0, 2).reshape(SSM_GROUP_CH, NST)).T
                          for cc in (ssm_c_re[0], -ssm_c_im[0])], axis=0)
    a_fwd = jnp.stack([abar_re.reshape(1, NST), abar_im.reshape(1, NST)])
    a_bwd = jnp.stack([abar_re.reshape(1, NST), -abar_im.reshape(1, NST)])
    d_row = ssm_d

    buckets = _t5_buckets_block()
    onehot_t = (jnp.arange(128, dtype=jnp.int32)[:, None] == jnp.asarray(buckets.reshape(1, -1))).astype(BF16)
    rb_hi = rel_bias.astype(BF16)
    rb_lo = (rel_bias - rb_hi.astype(F32)).astype(BF16)
    rb_lo2 = (rel_bias - rb_hi.astype(F32) - rb_lo.astype(F32)).astype(BF16)
    rb3 = jnp.pad(jnp.concatenate([rb_hi.T, rb_lo.T, rb_lo2.T], axis=0), ((0, 0), (0, 128 - NUM_BUCKETS)))
    b3 = _mm(rb3, onehot_t, "NN", name="rel_bias_rows", M=3 * N_Q_HEADS, N=BLOCK * 2 * BLOCK, K=128, tj=4096)
    bias = (b3[:N_Q_HEADS] + b3[N_Q_HEADS:2 * N_Q_HEADS]) + b3[2 * N_Q_HEADS:]
    bias = bias.reshape(N_Q_HEADS, BLOCK, 2 * BLOCK)
    sinks_b = jnp.broadcast_to(attn_sinks[0][:, None, None], (N_Q_HEADS, BLOCK, 128)).reshape(N_Q_HEADS * BLOCK, 128)

    def two(fn):
        def both(*blocks):
            r = fn(*blocks)
            return r, r
        return both

    h1, h1_t = _rowwise(two(_norm_mod), [(xv, "tile", D), (tied(tied(norm1_g, token_first), token_rest), "row", D),
                                         (sh1, "row", D), (sc1, "row", D)],
                        [(D, BF16), (D, BF16, "T")], [], name="norm1", rows=S)
    finish_gather(0, all_of(h1, bd, cd, a_fwd, a_bwd, bias, sinks_b))
    proj = _mm(h1, wop("w_in"), "NN", name="proj", M=S, N=INW, K=D, out_dtypes=(BF16,),
               epilogue=lambda acc, b: (acc + b,), extras=[(b_in, "row")])

    def heads(v2d, nh):
        return v2d.reshape(S, nh, HEAD_DIM).transpose(1, 0, 2)

    def unheads(v3d):
        return v3d.transpose(1, 0, 2).reshape(S, -1)

    qh = heads(proj[:, o_q:o_k], N_Q_HEADS)
    kh = heads(proj[:, o_k:o_v], N_KV_HEADS)
    vh = heads(proj[:, o_v:o_u], N_KV_HEADS)
    attn = unheads(_attn_fwd(qh, kh, vh, sinks_b, bias, "attn_fwd"))
    finish_gather(1, attn)
    y_attn = _mm(attn, wop("w_attn_proj"), "NN", name="attn_proj", M=S, N=D, K=ATTN_WIDTH, out_dtypes=(BF16,))

    u = proj[:, o_u:o_ga]
    u_il = _interleave(u, SCAN_CHUNKS)
    SB = 128
    nsb, gpb = SSM_W // SB, SB // SSM_GROUP_CH
    SBN = gpb * SSM_STATE
    y_il, xs = _ssm_fwd(u_il, bd, cd, a_fwd, d_row, name="ssm_fwd", sb=SB, sbn=SBN)
    y = _deinterleave(y_il, SCAN_CHUNKS)
    z, t_glu = _mm(y, wop("w_glu"), "NN", name="glu", M=S, N=SSM_W, K=SSM_W, out_dtypes=(BF16, F32), a_fn=_gelu,
                   epilogue=lambda acc, b, yy: (_gelu(yy) * _sigmoid(acc + b), acc + b),
                   extras=[(b_glu, "row"), (y, "tile")])
    y_ssm = _mm(z, wop("w_ssm_proj"), "NN", name="ssm_proj", M=S, N=D, K=SSM_W, out_dtypes=(BF16,))

    ff_bufs = _gather_wait([in_flight[k] for k in gather_groups[2]], gather_sems[2][0], gather_sems[2][1], all_of(y_ssm),
                           "gather_wait_2")
    ff_bufs, ff_send, ff_recv, token = _copies_start(ff_bufs, _forward_copies, 3 * len(ff_bufs), nothing,
                                                    "gather_forward_2_start")
    merged, merged_t = _rowwise(two(_merge), [(_Op(proj, coff=o_ga), "tile", D), (_Op(proj, coff=o_gs), "tile", D),
                                              (y_attn, "tile", D), (y_ssm, "tile", D)],
                                [(D, BF16), (D, BF16, "T")], [], name="merge", rows=S, deps=[token])
    mo, x2 = _mm(merged, wop("w_out"), "NN", name="out_proj", M=S, N=D, K=D, out_dtypes=(BF16, F32),
                 epilogue=lambda acc, xx, gg: (acc, xx + gg * acc), extras=[(xv, "tile"), (g1, "row")])
    h2, h2_t = _rowwise(two(_norm_mod), [(x2, "tile", D), (norm2_g, "row", D), (sh2, "row", D), (sc2, "row", D)],
                        [(D, BF16), (D, BF16, "T")], [], name="norm2", rows=S)
    gathered.update(zip(gather_groups[2], _copies_wait(ff_bufs, _forward_copies, ff_send, ff_recv, h2,
                                                       "gather_forward_2_wait")))
    a_b, r_b = _mm(h2, wop("w_ff1"), "NN", name="ff1", M=S, N=DFF, K=D, out_dtypes=(BF16, BF16),
                   epilogue=lambda acc: (acc, jnp.square(jnp.maximum(acc, 0.0))))
    ff, x3 = _mm(r_b, wop("w_ff2"), "NN", name="ff2", M=S, N=D, K=DFF, out_dtypes=(BF16, F32),
                 epilogue=lambda acc, xx, gg: (acc, xx + gg * acc), extras=[(x2, "tile"), (g2, "row")],
                 tj=1024, tk=1024)

    def final_fn(x3b, gf, tb, ffb, g2b):
        def f(xx, gg):
            yv = xx * lax.rsqrt(jnp.mean(xx * xx, axis=-1, keepdims=True) + EPS) * gg
            err = jnp.square(yv - tb)
            return 0.5 * jnp.sum(jnp.mean(err, axis=-1, keepdims=True), axis=0, keepdims=True)
        lv, vjp = jax.vjp(f, x3b, gf)
        dx, dg = vjp(jnp.ones((1, 1), F32))
        return dx, dx * g2b, dg, jnp.broadcast_to(lv, (1, 128)), jnp.sum(dx * ffb, axis=0, keepdims=True)

    dx3, dff, g_final, loss_acc, d_g2 = _rowwise(
        final_fn, [(x3, "tile", D), (final_g.reshape(1, D), "row", D), (tgt, "tile", D), (ff, "tile", D), (g2, "row", D)],
        [(D, F32), (D, BF16)], [D, 128, D], name="final", rows=S)
    da = _mm(dff, wop("w_ff2"), "NT", name="ff2_dx", M=S, N=DFF, K=D, out_dtypes=(BF16,),
             epilogue=lambda acc, ab: (acc * (2.0 * jnp.maximum(ab.astype(F32), 0.0)),), extras=[(a_b, "tile")])
    g_w_ff2 = _mm(r_b, dff, "TN", name="ff2_dw", M=DFF, N=D, K=S, out_dtypes=(BF16,), tj=1024, tk=1024)
    g_w_ff1 = _mm(h2_t, da, "NN", name="ff1_dw", M=D, N=DFF, K=S, out_dtypes=(BF16,), out_nsh=N_CHIPS, tj=1024, tk=1024)
    rs_ff, token = rs_swap("ff", dict(w_ff2=g_w_ff2, w_ff1=g_w_ff1))
    dh2 = _mm(da, wop("w_ff1"), "NT", name="ff1_dx", M=S, N=D, K=DFF, tj=1024, tk=1024, deps=[token])
    rs_ff, token_ff = rs_scatter("ff", rs_ff, dh2)

    def norm2_bwd(x2b, dh2b, dx3b, mob, gn, shb, scb, g1b):
        _, vjp = jax.vjp(_norm_mod, x2b, gn, shb, scb)
        dx, dg, dsh, dsc = vjp(dh2b)
        dx2b = dx + dx3b
        return dx2b, dx2b * g1b, dg, dsh, dsc, jnp.sum(dx2b * mob, axis=0, keepdims=True)

    dx2, dmo, g_norm2, d_sh2, d_sc2, d_g1 = _rowwise(
        norm2_bwd, [(x2, "tile", D), (dh2, "tile", D), (dx3, "tile", D), (mo, "tile", D),
                    (tied(norm2_g, token_ff), "row", D), (sh2, "row", D), (sc2, "row", D), (g1, "row", D)],
        [(D, F32), (D, BF16)], [D, D, D, D], name="norm2_bwd", rows=S)
    dmerged = _mm(dmo, wop("w_out"), "NT", name="out_dx", M=S, N=D, K=D)
    g_w_out = _mm(merged_t, dmo, "NN", name="out_dw", M=D, N=D, K=S, out_dtypes=(BF16,), tj=1024, tk=1024)

    def merge_bwd(gab, gsb, yab, ysb, dmb):
        _, vjp = jax.vjp(_merge, gab, gsb, yab, ysb)
        return vjp(dmb)

    d_ga, d_gs, dy_attn, dy_ssm = _rowwise(
        merge_bwd, [(_Op(proj, coff=o_ga), "tile", D), (_Op(proj, coff=o_gs), "tile", D), (y_attn, "tile", D),
                    (y_ssm, "tile", D), (dmerged, "tile", D)],
        [(D, BF16), (D, BF16), (D, BF16), (D, BF16)], [], name="merge_bwd", rows=S)

    dattn = _mm(dy_attn, wop("w_attn_proj"), "NT", name="attn_proj_dx", M=S, N=ATTN_WIDTH, K=D, tj=1024)
    g_w_attn_proj = _mm(attn, dy_attn, "TN", name="attn_proj_dw", M=ATTN_WIDTH, N=D, K=S, out_dtypes=(BF16,),
                        out_nsh=N_CHIPS, tk=1024)

    dz = _mm(dy_ssm, wop("w_ssm_proj"), "NT", name="ssm_proj_dx", M=S, N=SSM_W, K=D)
    g_w_ssm_proj = _mm(z, dy_ssm, "TN", name="ssm_proj_dw", M=SSM_W, N=D, K=S, out_dtypes=(BF16,),
                       out_nsh=N_CHIPS, tk=1024)

    def glu_bwd(dzb, yb, tb):
        z0 = _gelu(yb)
        sg = _sigmoid(tb)
        dt = dzb * z0 * sg * (1.0 - sg)
        return dt, dzb * sg, jnp.sum(dt, axis=0, keepdims=True)

    dt_b, dz0a, g_b_glu = _rowwise(glu_bwd, [(dz, "tile", SSM_W), (y, "tile", SSM_W), (t_glu, "tile", SSM_W)],
                                   [(SSM_W, BF16), (SSM_W, F32)], [SSM_W], name="glu_bwd", rows=S)

    def gelu_bwd(acc, dz0ab, yb):
        _, vjp = jax.vjp(_gelu, yb)
        return (vjp(acc + dz0ab)[0],)

    dy = _mm(dt_b, wop("w_glu"), "NT", name="glu_dx", M=S, N=SSM_W, K=SSM_W, epilogue=gelu_bwd,
             extras=[(dz0a, "tile"), (y, "tile")])
    g_w_glu = _mm(y, dt_b, "TN", name="glu_dw", M=SSM_W, N=SSM_W, K=S, out_dtypes=(BF16,), tk=1024, a_fn=_gelu)
    rs_mix, token = rs_swap("mix", dict(w_out=g_w_out, w_attn_proj=g_w_attn_proj, w_ssm_proj=g_w_ssm_proj,
                                        w_glu=g_w_glu))
    dy_il = _interleave(tied(dy, token), SCAN_CHUNKS)
    du_il, g_bd, g_cd, d_abar, g_ssm_d = _ssm_bwd(dy_il, u_il, xs, bd, cd, a_bwd, d_row, name="ssm_bwd", sb=SB, sbn=SBN)
    du = _deinterleave(du_il, SCAN_CHUNKS)
    rs_mix, token_mix = rs_scatter("mix", rs_mix, du_il)

    dqh, dkh, dvh, dsink_blk, dbias = _attn_bwd(qh, kh, vh, heads(dattn, N_Q_HEADS), tied(sinks_b, token_mix), bias,
                                                "attn_bwd")
    g_sinks = _sum_lead(dsink_blk.reshape(N_Q_HEADS, BLOCK, 128).transpose(1, 0, 2), "sinks_dw")[:, 0].reshape(1, N_Q_HEADS)
    g_rel = _mm(dbias.reshape(N_Q_HEADS, -1), onehot_t, "NT", name="rel_bias_dw", M=N_Q_HEADS, N=128,
                K=BLOCK * 2 * BLOCK, tk=4096)
    g_rel_bias = g_rel[:, :NUM_BUCKETS].T

    eye_b = jnp.eye(gpb, dtype=F32)
    g_cd6 = g_cd.reshape(2, nsb, gpb, SSM_STATE, gpb, SSM_GROUP_CH)
    g_c_re = jnp.einsum("bgnhp,gh->bgpn", g_cd6[0], eye_b).reshape(G, SSM_GROUP_CH, SSM_STATE)
    g_c_im = -jnp.einsum("bgnhp,gh->bgpn", g_cd6[1], eye_b).reshape(G, SSM_GROUP_CH, SSM_STATE)
    g_bd6 = g_bd.reshape(nsb, gpb, SSM_GROUP_CH, 2, gpb, SSM_STATE)
    g_bbar = jnp.einsum("bhprgn,hg->rbhnp", g_bd6, eye_b).reshape(2, G, SSM_STATE, SSM_GROUP_CH)
    g_bbar_re, g_bbar_im = g_bbar[0], g_bbar[1]
    g_lre, g_lim, g_lstep, g_bre, g_bim = disc_vjp(
        (d_abar[0].reshape(G, SSM_STATE), d_abar[1].reshape(G, SSM_STATE), g_bbar_re, g_bbar_im))

    dproj = jnp.concatenate([unheads(dqh).astype(BF16), unheads(dkh).astype(BF16), unheads(dvh).astype(BF16),
                             du.astype(BF16), d_ga, d_gs], axis=1)
    g_w_in = _mm(h1_t, dproj, "NN", name="proj_dw", M=D, N=INW, K=S, out_dtypes=(BF16,), out_nsh=N_CHIPS,
                 tj=INW // (2 * N_CHIPS), tk=1024)
    rs_in, token = rs_swap("in", dict(w_in=g_w_in))
    dh1 = _mm(dproj, wop("w_in"), "NT", name="proj_dx", M=S, N=D, K=INW, tj=1024, tk=INW // N_CHIPS, deps=[token])
    g_b_in = _rowwise(lambda d: (jnp.sum(d.astype(F32), axis=0, keepdims=True),), [(dproj, "tile", INW)], [], [INW],
                      name="proj_db", rows=S)[0]

    def norm1_bwd(xb, dhb, dresb, gn, shb, scb):
        _, vjp = jax.vjp(_norm_mod, xb, gn, shb, scb)
        dx, dg, dsh, dsc = vjp(dhb)
        return dx + dresb, dg, dsh, dsc

    grad_x, g_norm1, d_sh1, d_sc1 = _rowwise(
        norm1_bwd, [(xv, "tile", D), (dh1, "tile", D), (dx2, "tile", D), (norm1_g, "row", D),
                    (sh1, "row", D),
                    (sc1, "row", D)], [(D, F32)], [D, D, D], name="norm1_bwd", rows=S)

    dmod_row = jnp.concatenate([d_sh1, d_sc1, d_g1, d_sh2, d_sc2, d_g2], axis=1)
    dmod_all = _allgather8(jnp.pad(dmod_row, ((0, 7), (0, 0))), "gather_dmod").reshape(N_DEV, 8, -1)[:, 0]
    g_b_ada = _sum_lead(dmod_all.reshape(N_DEV, -1, 128), "b_ada_dw").reshape(1, -1)
    dmod_mine = lax.dynamic_slice(dmod_all.reshape(N_DEV, N_CHIPS, -1), (0, my_chip, 0), (N_DEV, 1, w_ada.shape[2]))[:, 0]
    g_w_ada = _mm(c16, jnp.pad(dmod_mine, ((0, 8), (0, 0))), "TN", name="ada_dw", M=D, N=w_ada.shape[2], K=16,
                  a_fn=_silu)

    small_g = dict(norm1_g=g_norm1, b_in=g_b_in, attn_sinks=g_sinks, rel_bias=g_rel_bias, lambda_re=g_lre[None],
                   lambda_im=g_lim[None], log_step=g_lstep[None], ssm_b_re=g_bre[None], ssm_b_im=g_bim[None],
                   ssm_c_re=g_c_re[None], ssm_c_im=g_c_im[None], ssm_d=g_ssm_d, b_glu=g_b_glu, norm2_g=g_norm2,
                   final_g=g_final.reshape(D))
    packed = _pack([loss_acc[:, :1]] + [small_g[k] for k in _SMALL])
    rows = packed.shape[0]
    summed = _sum_lead(_allgather8(packed, "gather_small").reshape(N_DEV, rows, 128), "small_sum")
    small_shapes = [(1,)] + [given[k].shape for k in _SMALL]
    parts = _unpack(summed, small_shapes)
    loss = parts[0].reshape(())
    grads.update(zip(_SMALL, parts[1:]))
    grads["b_ada"] = g_b_ada
    grads["w_ada"] = g_w_ada[None]

    deltas, new_m, new_v = {}, {}, {}

    def adamw_big(k, deps=()):
        d_, m_, v_ = _adamw(given[k][0], grads[k][0], given["m_" + k][0], given["v_" + k][0], "adamw_" + k, deps)
        deltas[k], new_m[k], new_v[k] = d_[None], m_[None], v_[None]
        return v_

    rs_in, token_in = rs_scatter("in", rs_in, all_of(summed, dmod_all))
    rs_ff, token = rs_sum("ff", rs_ff, all_of(summed, token_in))
    mark = adamw_big("w_ada", [token])
    rs_mix, token = rs_sum("mix", rs_mix, mark)
    small_all = list(_SMALL) + ["b_ada"]
    for k in small_all:
        grads[k] = grads[k].reshape(given[k].shape)

    def rows_of(a):
        return a.reshape(1, -1) if a.ndim == 1 else a

    d_, m_, v_ = _adamw_many(*[[rows_of(src[k]) for k in small_all] for src in (
        given, grads, {k: given["m_" + k] for k in small_all}, {k: given["v_" + k] for k in small_all})],
        "adamw_small", [token])
    for k, dd, mm, vv in zip(small_all, d_, m_, v_):
        deltas[k], new_m[k], new_v[k] = (t.reshape(given[k].shape) for t in (dd, mm, vv))
    v_ = v_[0]
    rs_finish("ff", rs_ff, v_)
    marks = [adamw_big(k) for k in ("w_ff2", "w_ff1")]
    rs_finish("mix", rs_mix, all_of(*marks))
    marks = [adamw_big(k) for k in ("w_out", "w_attn_proj", "w_ssm_proj", "w_glu")]
    rs_in, token = rs_sum("in", rs_in, all_of(*marks))
    rs_finish("in", rs_in, token)
    adamw_big("w_in")

    names = ["w_ada", "b_ada", "norm1_g", "w_in", "b_in", "attn_sinks", "rel_bias", "lambda_re", "lambda_im",
             "log_step", "ssm_b_re", "ssm_b_im", "ssm_c_re", "ssm_c_im", "ssm_d", "w_glu", "b_glu", "w_attn_proj",
             "w_ssm_proj", "w_out", "norm2_g", "w_ff1", "w_ff2", "final_g"]
    return (loss, grad_x[None], *[grads[n] for n in names], *[deltas[n] for n in names],
            *[new_m[n] for n in names], *[new_v[n] for n in names])
```

```python
import math

import numpy as np
import jax
import jax.numpy as jnp
from jax import lax
from jax.experimental import pallas as pl
from jax.experimental.pallas import tpu as pltpu

F32 = jnp.float32
BF16 = jnp.bfloat16
MESH = pl.DeviceIdType.MESH

HEAD_DIM = 64
N_Q_HEADS = 16
N_KV_HEADS = 4
GQA_GROUP = N_Q_HEADS // N_KV_HEADS
ATTN_WIDTH = N_Q_HEADS * HEAD_DIM
KV_WIDTH = N_KV_HEADS * HEAD_DIM
BLOCK = 128
NUM_BUCKETS = 32
MAX_DISTANCE = 128
NEG_INF = -1e30
SSM_GROUP_CH = 16
SSM_STATE = 64
EPS = 1e-6
ADAM_LR = 0.001
ADAM_B1 = 0.9
ADAM_B2 = 0.999
ADAM_EPS = 1e-08
ADAM_WD = 0.01
ADAM_STEP = 10

N_CHIPS = 4
N_DEV = 8
SCAN_CHUNKS = 8
VMEM_LIMIT_BYTES = 48 * 1024 * 1024
SSM_VMEM_LIMIT_BYTES = 56 * 1024 * 1024


def _cparams(sem=None):
    return pltpu.CompilerParams(dimension_semantics=sem, vmem_limit_bytes=VMEM_LIMIT_BYTES)


class _Op:
    def __init__(self, arr, nsh=None, coff=0):
        self.arr, self.nsh, self.coff = arr, nsh, coff
        if nsh is None:
            self.rows, self.cols = arr.shape
        else:
            assert arr.shape[0] == nsh
            self.rows, self.cols = arr.shape[1], arr.shape[2] * nsh

    def spec(self, br, bc, idx):
        assert self.coff % bc == 0
        off = self.coff // bc
        if self.nsh is None:
            return pl.BlockSpec((br, bc), lambda *g: (idx(*g)[0], idx(*g)[1] + off))
        per = (self.cols // self.nsh) // bc
        assert per * bc * self.nsh == self.cols

        def imap(*g):
            r, c = idx(*g)
            c = c + off
            return (c // per, r, c % per)
        return pl.BlockSpec((None, br, bc), imap)


def _as_op(a):
    return a if isinstance(a, _Op) else _Op(a)


def _mm(a, b, mode, *, name, M, N, K, out_dtypes=(F32,), out_nsh=None, epilogue=None, extras=(),
        a_fn=None, ti=1024, tj=512, tk=2048, deps=()):
    nd = len(deps)
    a, b = _as_op(a), _as_op(b)
    ti, tj, tk = min(ti, M), min(tj, N), min(tk, K)
    a_w = a.cols // a.nsh if a.nsh else None
    b_w = b.cols // b.nsh if b.nsh else None
    if a_w:
        ti, tk = (min(ti, a_w), tk) if mode == "TN" else (ti, min(tk, a_w))
    if b_w:
        tj, tk = (tj, min(tk, b_w)) if mode == "NT" else (min(tj, b_w), tk)
    if out_nsh:
        tj = min(tj, N // out_nsh)
    assert M % ti == 0 and N % tj == 0 and K % tk == 0, (name, M, N, K, ti, tj, tk)
    nk = K // tk
    if mode == "NN":
        a_spec = a.spec(ti, tk, lambda i, j, k: (i, k))
        b_spec = b.spec(tk, tj, lambda i, j, k: (k, j))
        dims = (((1,), (0,)), ((), ()))
    elif mode == "NT":
        a_spec = a.spec(ti, tk, lambda i, j, k: (i, k))
        b_spec = b.spec(tj, tk, lambda i, j, k: (j, k))
        dims = (((1,), (1,)), ((), ()))
    else:
        a_spec = a.spec(tk, ti, lambda i, j, k: (k, i))
        b_spec = b.spec(tk, tj, lambda i, j, k: (k, j))
        dims = (((0,), (0,)), ((), ()))
    ex_specs, ex_arrs = [], []
    for op, kind in extras:
        op = _as_op(op)
        if kind == "tile":
            ex_specs.append(op.spec(ti, tj, lambda i, j, k: (i, j)))
        else:
            ex_specs.append(op.spec(1, tj, lambda i, j, k: (0, j)))
        ex_arrs.append(op.arr)
    ne, no = len(ex_arrs), len(out_dtypes)
    if out_nsh is None:
        out_shapes = [jax.ShapeDtypeStruct((M, N), d) for d in out_dtypes]
        out_specs = [pl.BlockSpec((ti, tj), lambda i, j, k: (i, j)) for _ in out_dtypes]
    else:
        per = (N // out_nsh) // tj
        assert per * tj * out_nsh == N
        out_shapes = [jax.ShapeDtypeStruct((out_nsh, M, N // out_nsh), d) for d in out_dtypes]
        out_specs = [pl.BlockSpec((None, ti, tj), lambda i, j, k: (j // per, i, j % per)) for _ in out_dtypes]

    def body(a_ref, b_ref, *rest):
        ex_refs, out_refs, acc = rest[:ne], rest[ne + nd:ne + nd + no], rest[ne + nd + no]
        k = pl.program_id(2)

        @pl.when(k == 0)
        def _():
            acc[...] = jnp.zeros_like(acc)

        av = a_ref[...]
        if a_fn is not None:
            av = a_fn(av)
        acc[...] += lax.dot_general(av.astype(BF16), b_ref[...].astype(BF16), dims,
                                    preferred_element_type=F32)

        @pl.when(k == nk - 1)
        def _():
            res = acc[...]
            outs = epilogue(res, *[r[...] for r in ex_refs]) if epilogue is not None else (res,)
            for o_ref, o in zip(out_refs, outs):
                o_ref[...] = o.astype(o_ref.dtype)

    outs = pl.pallas_call(
        body, name=name, grid=(M // ti, N // tj, nk),
        in_specs=[a_spec, b_spec] + ex_specs + [pl.BlockSpec(memory_space=pl.ANY)] * nd,
        out_specs=out_specs, out_shape=out_shapes,
        scratch_shapes=[pltpu.VMEM((ti, tj), F32)],
        compiler_params=_cparams(("parallel", "parallel", "arbitrary")),
    )(a.arr, b.arr, *ex_arrs, *deps)
    return outs[0] if no == 1 else outs


def _rowwise(fn, ins, outs, accs, *, name, rows, tr=256, deps=()):
    tr = min(tr, rows)
    assert rows % tr == 0
    in_specs, arrs = [], []
    for op, kind, width in ins:
        op = _as_op(op)
        if kind == "tile":
            in_specs.append(op.spec(tr, width, lambda i: (i, 0)))
        else:
            in_specs.append(op.spec(op.rows, width, lambda i: (0, 0)))
        arrs.append(op.arr)
    ni, no, na = len(ins), len(outs), len(accs)
    flipped = [len(o) == 3 for o in outs]
    out_shapes = [jax.ShapeDtypeStruct((o[0], rows) if t else (rows, o[0]), o[1]) for o, t in zip(outs, flipped)]
    out_specs = [pl.BlockSpec((o[0], tr), lambda i: (0, i)) if t else pl.BlockSpec((tr, o[0]), lambda i: (i, 0))
                 for o, t in zip(outs, flipped)]
    out_shapes += [jax.ShapeDtypeStruct((1, w), F32) for w in accs]
    out_specs += [pl.BlockSpec((1, w), lambda i: (0, 0)) for w in accs]

    def body(*refs):
        nd = len(deps)
        in_refs, out_refs, acc_refs = refs[:ni], refs[ni + nd:ni + nd + no], refs[ni + nd + no:]
        res = fn(*[r[...] for r in in_refs])
        if not isinstance(res, (tuple, list)):
            res = (res,)
        for o_ref, r, t in zip(out_refs, res[:no], flipped):
            o_ref[...] = (r.astype(F32).T if t else r).astype(o_ref.dtype)
        if na:
            @pl.when(pl.program_id(0) == 0)
            def _():
                for a_ref in acc_refs:
                    a_ref[...] = jnp.zeros_like(a_ref)
            for a_ref, r in zip(acc_refs, res[no:]):
                a_ref[...] += r.astype(F32)

    res = pl.pallas_call(
        body, name=name, grid=(rows // tr,), in_specs=in_specs + [pl.BlockSpec(memory_space=pl.ANY)] * len(deps),
        out_specs=out_specs, out_shape=out_shapes, compiler_params=_cparams(("arbitrary",)),
    )(*arrs, *deps)
    return res


def _norm_mod(x, g, sh, sc):
    y = x * lax.rsqrt(jnp.mean(x * x, axis=-1, keepdims=True) + EPS) * g
    return y * (1.0 + sc) + sh


def _sigmoid(x):
    return 1.0 / (1.0 + jnp.exp(-x))


def _silu(x):
    return x * _sigmoid(x)


def _gelu(x):
    return 0.5 * x * (1.0 + jnp.tanh(math.sqrt(2.0 / math.pi) * (x + 0.044715 * (x * x * x))))


def _merge(ga, gs, ya, ys):
    ga, gs, ya, ys = (v.astype(F32) for v in (ga, gs, ya, ys))
    return _sigmoid(ga) * ya + _sigmoid(gs) * ys


def _attn_head(q, kp, kc, vp, vc, sink, bias_p, bias_c, not_first):
    nt = (((1,), (1,)), ((), ()))
    nn = (((1,), (0,)), ((), ()))
    qb = q.astype(BF16)
    scale = HEAD_DIM ** -0.5
    sp = lax.dot_general(qb, kp.astype(BF16), nt, preferred_element_type=F32) * scale + bias_p
    sc = lax.dot_general(qb, kc.astype(BF16), nt, preferred_element_type=F32) * scale + bias_c
    qi = lax.broadcasted_iota(jnp.int32, sp.shape, 0) & (BLOCK - 1)
    ki = lax.broadcasted_iota(jnp.int32, sp.shape, 1)
    sp = jnp.where(jnp.logical_and(ki > qi, not_first), sp, NEG_INF)
    sc = jnp.where(ki <= qi, sc, NEG_INF)
    m = jnp.maximum(jnp.maximum(jnp.max(sp, axis=-1, keepdims=True), jnp.max(sc, axis=-1, keepdims=True)), sink)
    m = lax.stop_gradient(m)
    pp = jnp.exp(sp - m)
    pc = jnp.exp(sc - m)
    denom = jnp.sum(pp, axis=-1, keepdims=True) + jnp.sum(pc, axis=-1, keepdims=True) + jnp.exp(sink - m)
    o = lax.dot_general((pp / denom).astype(BF16), vp.astype(BF16), nn, preferred_element_type=F32)
    o = o + lax.dot_general((pc / denom).astype(BF16), vc.astype(BF16), nn, preferred_element_type=F32)
    return o


def _attn_fwd(qh, kh, vh, sinks, bias, name):
    s = qh.shape[1]
    nb = s // BLOCK
    G = GQA_GROUP
    R = G * BLOCK

    def body(q_ref, kp_ref, kc_ref, vp_ref, vc_ref, sink_ref, bias_ref, o_ref):
        not_first = pl.program_id(0) > 0
        for kv in range(N_KV_HEADS):
            hs = slice(kv * G, (kv + 1) * G)
            o = _attn_head(q_ref[hs].reshape(R, HEAD_DIM), kp_ref[kv], kc_ref[kv], vp_ref[kv], vc_ref[kv],
                           sink_ref[kv * R:(kv + 1) * R, 0:1],
                           bias_ref[hs, :, 0:BLOCK].reshape(R, BLOCK), bias_ref[hs, :, BLOCK:2 * BLOCK].reshape(R, BLOCK),
                           not_first)
            o_ref[hs] = o.reshape(G, BLOCK, HEAD_DIM).astype(o_ref.dtype)

    cur = lambda i: (0, i, 0)
    prev = lambda i: (0, jnp.maximum(i - 1, 0), 0)
    return pl.pallas_call(
        body, name=name, grid=(nb,),
        in_specs=[pl.BlockSpec((N_Q_HEADS, BLOCK, HEAD_DIM), cur),
                  pl.BlockSpec((N_KV_HEADS, BLOCK, HEAD_DIM), prev), pl.BlockSpec((N_KV_HEADS, BLOCK, HEAD_DIM), cur),
                  pl.BlockSpec((N_KV_HEADS, BLOCK, HEAD_DIM), prev), pl.BlockSpec((N_KV_HEADS, BLOCK, HEAD_DIM), cur),
                  pl.BlockSpec((N_Q_HEADS * BLOCK, 128), lambda i: (0, 0)),
                  pl.BlockSpec((N_Q_HEADS, BLOCK, 2 * BLOCK), lambda i: (0, 0, 0))],
        out_specs=pl.BlockSpec((N_Q_HEADS, BLOCK, HEAD_DIM), cur),
        out_shape=jax.ShapeDtypeStruct((N_Q_HEADS, s, HEAD_DIM), BF16),
        compiler_params=_cparams(("arbitrary",)),
    )(qh, kh, kh, vh, vh, sinks, bias)


def _attn_bwd(qh, kh, vh, doh, sinks, bias, name):
    s = qh.shape[1]
    nb = s // BLOCK
    G = GQA_GROUP
    R = G * BLOCK

    def body(q_ref, kp_ref, kc_ref, vp_ref, vc_ref, do_ref, sink_ref, bias_ref,
             dq_ref, dk_ref, dv_ref, dsink_ref, dbias_ref, ck, cv):
        i = pl.program_id(1)

        @pl.when(i == 0)
        def _():
            dsink_ref[...] = jnp.zeros_like(dsink_ref)
            dbias_ref[...] = jnp.zeros_like(dbias_ref)
            ck[...] = jnp.zeros_like(ck)
            cv[...] = jnp.zeros_like(cv)

        @pl.when(i < nb)
        def _():
            not_first = i > 0
            _, vjp = jax.vjp(lambda q, a, b, c, d, sk, e, f: _attn_head(q, a, b, c, d, sk, e, f, not_first),
                             q_ref[...].astype(F32).reshape(R, HEAD_DIM), kp_ref[...].astype(F32),
                             kc_ref[...].astype(F32), vp_ref[...].astype(F32), vc_ref[...].astype(F32),
                             sink_ref[:, 0:1], bias_ref[:, :, 0:BLOCK].reshape(R, BLOCK),
                             bias_ref[:, :, BLOCK:2 * BLOCK].reshape(R, BLOCK))
            dq, dkp, dkc, dvp, dvc, dsk, dbp, dbc = vjp(do_ref[...].reshape(R, HEAD_DIM).astype(F32))
            dq_ref[...] = dq.reshape(G, BLOCK, HEAD_DIM).astype(dq_ref.dtype)
            dsink_ref[...] += jnp.broadcast_to(dsk, (R, 128))
            dbias_ref[:, :, 0:BLOCK] += dbp.reshape(G, BLOCK, BLOCK)
            dbias_ref[:, :, BLOCK:2 * BLOCK] += dbc.reshape(G, BLOCK, BLOCK)
            dk_ref[...] = (ck[...] + dkp).astype(dk_ref.dtype)
            dv_ref[...] = (cv[...] + dvp).astype(dv_ref.dtype)
            ck[...] = dkc
            cv[...] = dvc

        @pl.when(i == nb)
        def _():
            dk_ref[...] = ck[...].astype(dk_ref.dtype)
            dv_ref[...] = cv[...].astype(dv_ref.dtype)

    qcur = lambda kv, i: (kv, jnp.minimum(i, nb - 1), 0)
    kcur = lambda kv, i: (kv, jnp.minimum(i, nb - 1), 0)
    kprev = lambda kv, i: (kv, jnp.clip(i - 1, 0, nb - 1), 0)
    qspec = pl.BlockSpec((G, BLOCK, HEAD_DIM), qcur)
    kc_spec = pl.BlockSpec((None, BLOCK, HEAD_DIM), kcur)
    kp_spec = pl.BlockSpec((None, BLOCK, HEAD_DIM), kprev)
    return pl.pallas_call(
        body, name=name, grid=(N_KV_HEADS, nb + 1),
        in_specs=[qspec, kp_spec, kc_spec, kp_spec, kc_spec, qspec,
                  pl.BlockSpec((R, 128), lambda kv, i: (kv, 0)),
                  pl.BlockSpec((G, BLOCK, 2 * BLOCK), lambda kv, i: (kv, 0, 0))],
        out_specs=[qspec, kp_spec, kp_spec,
                   pl.BlockSpec((R, 128), lambda kv, i: (kv, 0)),
                   pl.BlockSpec((G, BLOCK, 2 * BLOCK), lambda kv, i: (kv, 0, 0))],
        out_shape=[jax.ShapeDtypeStruct((N_Q_HEADS, s, HEAD_DIM), BF16),
                   jax.ShapeDtypeStruct((N_KV_HEADS, s, HEAD_DIM), BF16),
                   jax.ShapeDtypeStruct((N_KV_HEADS, s, HEAD_DIM), BF16),
                   jax.ShapeDtypeStruct((N_Q_HEADS * BLOCK, 128), F32),
                   jax.ShapeDtypeStruct((N_Q_HEADS, BLOCK, 2 * BLOCK), F32)],
        scratch_shapes=[pltpu.VMEM((BLOCK, HEAD_DIM), F32), pltpu.VMEM((BLOCK, HEAD_DIM), F32)],
        compiler_params=_cparams(("arbitrary", "arbitrary")),
    )(qh, kh, kh, vh, vh, doh, sinks, bias)


def _cmul(ar, ai, br, bi):
    return ar * br - ai * bi, ar * bi + ai * br


def _scan_passes(a_ref, b_ref, x_ref, xp_ref, da_ref, *, s, tc, reverse):
    nc = SCAN_CHUNKS
    steps = s // nc
    with_da = xp_ref is not None
    unroll = 8 if steps % 8 == 0 else 1

    def shift(v, d):
        row = lax.broadcasted_iota(jnp.int32, v.shape, 0)
        if reverse:
            return jnp.where(row < nc - d, pltpu.roll(v, nc - d, 0), 0.0)
        return jnp.where(row >= d, pltpu.roll(v, d, 0), 0.0)

    def run():
        ar = jnp.broadcast_to(a_ref[0], (nc, tc))
        ai = jnp.broadcast_to(a_ref[1], (nc, tc))

        def row_of(step):
            j = (steps - 1 - step) if reverse else step
            return pl.multiple_of(j * nc, nc)

        def p1(step, st):
            sr, si = st
            r0 = row_of(step)
            mr, mi = _cmul(ar, ai, sr, si)
            sr = mr + b_ref[0, pl.ds(r0, nc), :]
            si = mi + b_ref[1, pl.ds(r0, nc), :]
            x_ref[0, pl.ds(r0, nc), :] = sr
            x_ref[1, pl.ds(r0, nc), :] = si
            return sr, si
        zero = jnp.zeros((nc, tc), F32)
        er, ei = lax.fori_loop(0, steps, p1, (zero, zero), unroll=unroll)

        pr, pi_ = jnp.ones((nc, tc), F32), zero
        br, bi, left = ar, ai, steps
        while left:
            if left & 1:
                pr, pi_ = _cmul(pr, pi_, br, bi)
            br, bi = _cmul(br, bi, br, bi)
            left >>= 1
        cr, ci = shift(er, 1), shift(ei, 1)
        d = 1
        while d < nc:
            mr, mi = _cmul(pr, pi_, shift(cr, d), shift(ci, d))
            cr, ci = cr + mr, ci + mi
            pr, pi_ = _cmul(pr, pi_, pr, pi_)
            d *= 2

        def p2(step, st):
            qr, qi, dar, dai = st
            r0 = row_of(step)
            qr, qi = _cmul(ar, ai, qr, qi)
            fr, fi = _cmul(qr, qi, cr, ci)
            xr = x_ref[0, pl.ds(r0, nc), :] + fr
            xi = x_ref[1, pl.ds(r0, nc), :] + fi
            x_ref[0, pl.ds(r0, nc), :] = xr
            x_ref[1, pl.ds(r0, nc), :] = xi
            if with_da:
                jm = jnp.where(step == steps - 1, steps - 1, steps - 2 - step)
                rp = pl.multiple_of(jm * nc, nc)
                vr, vi = xp_ref[0, pl.ds(rp, nc), :], xp_ref[1, pl.ds(rp, nc), :]
                row = lax.broadcasted_iota(jnp.int32, (nc, tc), 0)
                first = step == steps - 1
                sel = jnp.logical_and(first, row == 0)
                vr = jnp.where(sel, 0.0, jnp.where(first, pltpu.roll(vr, 1, 0), vr))
                vi = jnp.where(sel, 0.0, jnp.where(first, pltpu.roll(vi, 1, 0), vi))
                dar = dar + xr * vr + xi * vi
                dai = dai + xi * vr - xr * vi
            return qr, qi, dar, dai
        _, _, dar, dai = lax.fori_loop(0, steps, p2, (jnp.ones((nc, tc), F32), zero, zero, zero), unroll=unroll)
        if with_da:
            da_ref[0] = jnp.sum(dar, axis=0, keepdims=True)
            da_ref[1] = jnp.sum(dai, axis=0, keepdims=True)

    run()


def _ssm_fwd(u, bd, cd, a, d_row, *, name, sb, sbn):
    s, w = u.shape
    nst = a.shape[2]
    nblk = w // sb
    rows = min(512, s)
    nn = (((1,), (0,)), ((), ()))

    def body(u_ref, bre_ref, bim_ref, cre_ref, cim_ref, a_ref, d_ref, y_ref, x_ref):

        def fill(r, carry):
            r0 = pl.multiple_of(r * rows, rows)
            ub = u_ref[pl.ds(r0, rows), :].astype(BF16)
            x_ref[0, pl.ds(r0, rows), :] = lax.dot_general(ub, bre_ref[...].astype(BF16), nn, preferred_element_type=F32)
            x_ref[1, pl.ds(r0, rows), :] = lax.dot_general(ub, bim_ref[...].astype(BF16), nn, preferred_element_type=F32)
            return carry
        lax.fori_loop(0, s // rows, fill, 0)
        _scan_passes(a_ref, x_ref, x_ref, None, None, s=s, tc=sbn, reverse=False)

        def project(r, carry):
            r0 = pl.multiple_of(r * rows, rows)
            y = lax.dot_general(x_ref[0, pl.ds(r0, rows), :].astype(BF16), cre_ref[...].astype(BF16), nn, preferred_element_type=F32)
            y = y + lax.dot_general(x_ref[1, pl.ds(r0, rows), :].astype(BF16), cim_ref[...].astype(BF16), nn, preferred_element_type=F32)
            y_ref[pl.ds(r0, rows), :] = y + d_ref[...] * u_ref[pl.ds(r0, rows), :]
            return carry
        lax.fori_loop(0, s // rows, project, 0)

    return pl.pallas_call(
        body, name=name, grid=(nblk,),
        in_specs=[pl.BlockSpec((s, sb), lambda j: (0, j)),
                  pl.BlockSpec((sb, sbn), lambda j: (j, j)), pl.BlockSpec((sb, sbn), lambda j: (j, nblk + j)),
                  pl.BlockSpec((sbn, sb), lambda j: (j, j)), pl.BlockSpec((sbn, sb), lambda j: (nblk + j, j)),
                  pl.BlockSpec((2, 1, sbn), lambda j: (0, 0, j)), pl.BlockSpec((1, sb), lambda j: (0, j))],
        out_specs=[pl.BlockSpec((s, sb), lambda j: (0, j)), pl.BlockSpec((2, s, sbn), lambda j: (0, 0, j))],
        out_shape=[jax.ShapeDtypeStruct((s, w), F32), jax.ShapeDtypeStruct((2, s, nst), F32)],
        compiler_params=pltpu.CompilerParams(dimension_semantics=("arbitrary",), vmem_limit_bytes=SSM_VMEM_LIMIT_BYTES),
    )(u, bd, bd, cd, cd, a, d_row)


def _ssm_bwd(dy, u, xs, bd, cd, a, d_row, *, name, sb, sbn, deps=()):
    s, w = u.shape
    nst = a.shape[2]
    nblk = w // sb
    rows = min(512, s)
    nt = (((1,), (1,)), ((), ()))
    tn = (((0,), (0,)), ((), ()))

    def body(dy_ref, u_ref, xs_hbm, bre_ref, bim_ref, cre_ref, cim_ref, a_ref, d_ref, *rest):
        du_ref, gb_ref, gc_ref, da_ref, gd_ref, lam, xs_ref, sem = rest[len(deps):]
        j = pl.program_id(0)
        fetch = pltpu.make_async_copy(xs_hbm.at[:, :, pl.ds(pl.multiple_of(j * sbn, sbn), sbn)], xs_ref, sem)
        fetch.start()

        def fill(r, carry):
            r0 = pl.multiple_of(r * rows, rows)
            dyb = dy_ref[pl.ds(r0, rows), :].astype(BF16)
            lam[0, pl.ds(r0, rows), :] = lax.dot_general(dyb, cre_ref[...].astype(BF16), nt, preferred_element_type=F32)
            lam[1, pl.ds(r0, rows), :] = lax.dot_general(dyb, cim_ref[...].astype(BF16), nt, preferred_element_type=F32)
            return carry
        lax.fori_loop(0, s // rows, fill, 0)
        fetch.wait()
        _scan_passes(a_ref, lam, lam, xs_ref, da_ref, s=s, tc=sbn, reverse=True)
        gb_ref[...] = jnp.zeros_like(gb_ref)
        gc_ref[...] = jnp.zeros_like(gc_ref)
        gd_ref[...] = jnp.zeros_like(gd_ref)

        def project(r, carry):
            r0 = pl.multiple_of(r * rows, rows)
            dyv, uv = dy_ref[pl.ds(r0, rows), :], u_ref[pl.ds(r0, rows), :]
            dyb, ub = dyv.astype(BF16), uv.astype(BF16)
            lr, li = lam[0, pl.ds(r0, rows), :].astype(BF16), lam[1, pl.ds(r0, rows), :].astype(BF16)
            du = lax.dot_general(lr, bre_ref[...].astype(BF16), nt, preferred_element_type=F32)
            du = du + lax.dot_general(li, bim_ref[...].astype(BF16), nt, preferred_element_type=F32)
            du_ref[pl.ds(r0, rows), :] = du + d_ref[...] * dyv
            gb_ref[:, 0:sbn] += lax.dot_general(ub, lr, tn, preferred_element_type=F32)
            gb_ref[:, sbn:2 * sbn] += lax.dot_general(ub, li, tn, preferred_element_type=F32)
            gc_ref[0] += lax.dot_general(xs_ref[0, pl.ds(r0, rows), :].astype(BF16), dyb, tn, preferred_element_type=F32)
            gc_ref[1] += lax.dot_general(xs_ref[1, pl.ds(r0, rows), :].astype(BF16), dyb, tn, preferred_element_type=F32)
            gd_ref[...] += jnp.sum(dyv * uv, axis=0, keepdims=True)
            return carry
        lax.fori_loop(0, s // rows, project, 0)

    col = lambda j: (0, j)
    return pl.pallas_call(
        body, name=name, grid=(nblk,),
        in_specs=[pl.BlockSpec((s, sb), col), pl.BlockSpec((s, sb), col), pl.BlockSpec(memory_space=pl.ANY),
                  pl.BlockSpec((sb, sbn), lambda j: (j, j)), pl.BlockSpec((sb, sbn), lambda j: (j, nblk + j)),
                  pl.BlockSpec((sbn, sb), lambda j: (j, j)), pl.BlockSpec((sbn, sb), lambda j: (nblk + j, j)),
                  pl.BlockSpec((2, 1, sbn), lambda j: (0, 0, j)), pl.BlockSpec((1, sb), col)]
        + [pl.BlockSpec(memory_space=pl.ANY)] * len(deps),
        out_specs=[pl.BlockSpec((s, sb), col), pl.BlockSpec((sb, 2 * sbn), lambda j: (j, 0)),
                   pl.BlockSpec((2, sbn, sb), lambda j: (0, j, 0)), pl.BlockSpec((2, 1, sbn), lambda j: (0, 0, j)),
                   pl.BlockSpec((1, sb), col)],
        out_shape=[jax.ShapeDtypeStruct((s, w), F32), jax.ShapeDtypeStruct((w, 2 * sbn), F32),
                   jax.ShapeDtypeStruct((2, nst, sb), F32), jax.ShapeDtypeStruct((2, 1, nst), F32),
                   jax.ShapeDtypeStruct((1, w), F32)],
        scratch_shapes=[pltpu.VMEM((2, s, sbn), F32), pltpu.VMEM((2, s, sbn), F32), pltpu.SemaphoreType.DMA],
        compiler_params=pltpu.CompilerParams(dimension_semantics=("arbitrary",), vmem_limit_bytes=SSM_VMEM_LIMIT_BYTES),
    )(dy, u, xs, bd, bd, cd, cd, a, d_row, *deps)


def _adamw_math(w, g, m, v):
    nm = ADAM_B1 * m + (1.0 - ADAM_B1) * g
    nv = ADAM_B2 * v + (1.0 - ADAM_B2) * (g * g)
    m_hat = nm / (1.0 - ADAM_B1 ** ADAM_STEP)
    v_hat = nv / (1.0 - ADAM_B2 ** ADAM_STEP)
    return -ADAM_LR * (m_hat / (jnp.sqrt(v_hat) + ADAM_EPS) + ADAM_WD * w), nm, nv


def _adamw_many(ws, gs, ms, vs, name, deps=()):
    n, nd = len(ws), len(deps)

    def body(*refs):
        outs = refs[4 * n + nd:]
        for i in range(n):
            d, nm, nv = _adamw_math(refs[i][...], refs[n + i][...], refs[2 * n + i][...], refs[3 * n + i][...])
            outs[i][...], outs[n + i][...], outs[2 * n + i][...] = d, nm, nv

    whole = pl.BlockSpec(memory_space=pltpu.VMEM)
    res = pl.pallas_call(
        body, name=name, in_specs=[whole] * (4 * n) + [pl.BlockSpec(memory_space=pl.ANY)] * nd,
        out_specs=[whole] * (3 * n), out_shape=[jax.ShapeDtypeStruct(w.shape, F32) for w in ws] * 3,
        compiler_params=pltpu.CompilerParams(vmem_limit_bytes=VMEM_LIMIT_BYTES),
    )(*ws, *gs, *ms, *vs, *deps)
    return res[:n], res[n:2 * n], res[2 * n:]


def _adamw(w, g, m, v, name, deps=()):
    nd = len(deps)
    r, c = w.shape
    tr = r
    for cand in (512, 256, 128, 64, 32, 16, 8):
        if r % cand == 0 and cand * c * 4 <= 2 * 1024 * 1024:
            tr = cand
            break

    def body(w_ref, g_ref, m_ref, v_ref, *rest):
        d_ref, nm_ref, nv_ref = rest[nd:]
        d_ref[...], nm_ref[...], nv_ref[...] = _adamw_math(w_ref[...], g_ref[...], m_ref[...], v_ref[...])

    spec = pl.BlockSpec((tr, c), lambda i: (i, 0))
    sds = jax.ShapeDtypeStruct((r, c), F32)
    return pl.pallas_call(body, name=name, grid=(r // tr,),
                          in_specs=[spec] * 4 + [pl.BlockSpec(memory_space=pl.ANY)] * nd, out_specs=[spec] * 3,
                          out_shape=[sds] * 3, compiler_params=_cparams(("parallel",)))(w, g, m, v, *deps)


def _sum_lead(x, name, out_dtype=F32):
    n, r, c = x.shape
    tr = r
    for cand in (512, 256, 128, 64, 32, 16, 8):
        if r % cand == 0 and n * cand * c * 4 <= 4 * 1024 * 1024:
            tr = cand
            break

    def body(x_ref, o_ref):
        acc = x_ref[0].astype(F32)
        for k in range(1, n):
            acc = acc + x_ref[k].astype(F32)
        o_ref[...] = acc.astype(o_ref.dtype)

    return pl.pallas_call(body, name=name, grid=(r // tr,),
                          in_specs=[pl.BlockSpec((n, tr, c), lambda i: (0, i, 0))],
                          out_specs=pl.BlockSpec((tr, c), lambda i: (i, 0)),
                          out_shape=jax.ShapeDtypeStruct((r, c), out_dtype),
                          compiler_params=_cparams(("parallel",)))(x)


def _row_tile(rows, row_bytes, budget, least=8):
    for cand in (1024, 512, 256, 128, 64, 32, 16, 8):
        if cand >= least and rows % cand == 0 and cand * row_bytes <= budget:
            return cand
    return rows


def _cast_into_slot(w, slot, name):
    r, c = w.shape
    tr = _row_tile(r, c * 4, 4 * 1024 * 1024, least=16)

    def body(slot_ref, w_ref, o_ref):
        o_ref[...] = w_ref[...].astype(o_ref.dtype)

    gs = pltpu.PrefetchScalarGridSpec(
        num_scalar_prefetch=1, grid=(r // tr,),
        in_specs=[pl.BlockSpec((tr, c), lambda i, s: (i, 0))],
        out_specs=pl.BlockSpec((None, tr, c), lambda i, s: (s[0], i, 0)))
    return pl.pallas_call(body, name=name, grid_spec=gs, out_shape=jax.ShapeDtypeStruct((N_CHIPS, r, c), BF16),
                          compiler_params=_cparams(("parallel",)))(slot, w)


def _sum_own(p, t, sel, name):
    _, h, c = p.shape
    tr = _row_tile(h, c * 4, 2 * 1024 * 1024, least=16)
    nblk = h // tr

    def body(sel_ref, p_ref, t_ref, o_ref):
        acc = p_ref[...].astype(F32)
        for k in range(3):
            acc = acc + t_ref[k].astype(F32)
        o_ref[...] = acc

    gs = pltpu.PrefetchScalarGridSpec(
        num_scalar_prefetch=1, grid=(nblk,),
        in_specs=[pl.BlockSpec((None, tr, c), lambda i, s: (s[0], i, 0)),
                  pl.BlockSpec((3, tr, c), lambda i, s: (0, i, 0))],
        out_specs=pl.BlockSpec((tr, c), lambda i, s: (s[1] * nblk + i, 0)))
    return pl.pallas_call(body, name=name, grid_spec=gs, out_shape=jax.ShapeDtypeStruct((2 * h, c), F32),
                          compiler_params=_cparams(("parallel",)))(sel, p, t)


def _add_half(g, t, half, name):
    n, r, c = g.shape
    h = r // 2
    tr = h
    for cand in (512, 256, 128, 64, 32, 16):
        if h % cand == 0 and cand * c * 2 <= 2 * 1024 * 1024:
            tr = cand
            break
    nblk = h // tr

    def body(half_ref, g_ref, t_ref, o_ref):
        o_ref[...] = (g_ref[...].astype(F32) + t_ref[...].astype(F32)).astype(o_ref.dtype)

    gs = pltpu.PrefetchScalarGridSpec(
        num_scalar_prefetch=1, grid=(n, nblk),
        in_specs=[pl.BlockSpec((None, tr, c), lambda j, i, hr: (j, hr[0] * nblk + i, 0)),
                  pl.BlockSpec((None, tr, c), lambda j, i, hr: (j, i, 0))],
        out_specs=pl.BlockSpec((None, tr, c), lambda j, i, hr: (j, i, 0)))
    return pl.pallas_call(body, name=name, grid_spec=gs, out_shape=jax.ShapeDtypeStruct((n, h, c), BF16),
                          compiler_params=_cparams(("parallel", "parallel")))(half, g, t)


def _position():
    x, y, c = lax.axis_index("x"), lax.axis_index("y"), lax.axis_index("c")
    return x, y, c


def _allgather8(xs, name):
    m_per, n = xs.shape

    def body(x_ref, out_ref, send_sems, recv_sems, local_sem):
        x, y, c = _position()
        me, sibling = (x, y, c), (x, y, 1 - c)
        chips = [(1 - x, y), (x, 1 - y), (1 - x, 1 - y)]

        def rows(px, py, pc):
            return out_ref.at[pl.ds((4 * px + 2 * py + pc) * m_per, m_per), :]

        def copy(k, block, to, src=None):
            return pltpu.make_async_remote_copy(
                src_ref=rows(*block) if src is None else src, dst_ref=rows(*block),
                send_sem=send_sems.at[k], recv_sem=recv_sems.at[k], device_id=to, device_id_type=MESH)

        mine = pltpu.make_async_copy(x_ref, rows(*me), local_sem)
        mine.start()
        first = [copy(0, me, sibling, src=x_ref)]
        first += [copy(1 + j, me, (*chip, c), src=x_ref) for j, chip in enumerate(chips)]
        for cp in first:
            cp.start()
        passed = [copy(4 + j, (*chip, c), sibling) for j, chip in enumerate(chips)]
        for j, chip in enumerate(chips):
            copy(1 + j, (*chip, c), me).wait_recv()
            passed[j].start()
        copy(0, sibling, me).wait_recv()
        for j, chip in enumerate(chips):
            copy(4 + j, (*chip, 1 - c), me).wait_recv()
        for cp in first + passed:
            cp.wait_send()
        mine.wait()

    return pl.pallas_call(
        body, name=name, out_shape=jax.ShapeDtypeStruct((N_DEV * m_per, n), xs.dtype),
        in_specs=[pl.BlockSpec(memory_space=pltpu.VMEM)], out_specs=pl.BlockSpec(memory_space=pltpu.VMEM),
        scratch_shapes=[pltpu.SemaphoreType.DMA((7,)), pltpu.SemaphoreType.DMA((7,)), pltpu.SemaphoreType.DMA],
        compiler_params=pltpu.CompilerParams(vmem_limit_bytes=VMEM_LIMIT_BYTES),
    )(xs)


_HBM = pl.BlockSpec(memory_space=pltpu.HBM)


_SEM = pl.BlockSpec(memory_space=pltpu.SEMAPHORE)
_ANY = pl.BlockSpec(memory_space=pl.ANY)
_EFFECT = pltpu.SideEffectType.DATAFLOW_SIDE_EFFECTING


def _in_hbm(a):
    return pltpu.with_memory_space_constraint(a, pltpu.HBM)


def _several(after):
    return list(after) if isinstance(after, (list, tuple)) else [after]


def _gather_start(ws, groups, after, name):
    n = len(ws)
    after = _several(after)

    def body(*refs):
        in_refs = refs[:n]
        sems, token = refs[2 * n + len(after):-1], refs[-1]
        x, y, c = _position()
        mychip = 2 * x + y
        chips = [(1 - x, y), (x, 1 - y), (1 - x, 1 - y)]
        for g, members in enumerate(groups):
            for k, i in enumerate(members):
                h = ws[i].shape[1] // 2
                mine = in_refs[i].at[mychip, pl.ds(c * h, h), :]
                for j, (px, py) in enumerate(chips):
                    pltpu.make_async_remote_copy(
                        src_ref=mine, dst_ref=mine, send_sem=sems[2 * g].at[3 * k + j],
                        recv_sem=sems[2 * g + 1].at[3 * k + j], device_id=(px, py, c), device_id_type=MESH).start()
        token[...] = jnp.zeros_like(token)

    sem_shapes = [pltpu.SemaphoreType.DMA((3 * len(m),)) for m in groups for _ in range(2)]
    res = pl.pallas_call(
        body, name=name,
        out_shape=[pltpu.HBM(w.shape, w.dtype) for w in ws] + sem_shapes + [jax.ShapeDtypeStruct((8, 128), F32)],
        in_specs=[_HBM] * n + [_ANY] * len(after),
        out_specs=[_HBM] * n + [_SEM] * len(sem_shapes) + [pl.BlockSpec(memory_space=pltpu.VMEM)],
        input_output_aliases={i: i for i in range(n)},
        compiler_params=pltpu.CompilerParams(has_side_effects=_EFFECT),
    )(*[_in_hbm(w) for w in ws], *after)
    bufs, sems, token = res[:n], res[n:-1], res[-1]
    return list(bufs), [(sems[2 * g], sems[2 * g + 1]) for g in range(len(groups))], token


def _gather_wait(bufs, send_sems, recv_sems, after, name):
    m = len(bufs)

    def body(*refs):
        in_refs = refs[:m]
        send, recv = refs[m], refs[m + 1]
        x, y, c = _position()
        mychip = 2 * x + y
        chips = [(1 - x, y), (x, 1 - y), (1 - x, 1 - y)]
        for k in range(m):
            h = bufs[k].shape[1] // 2
            mine = in_refs[k].at[mychip, pl.ds(c * h, h), :]
            for j, (px, py) in enumerate(chips):
                cp = pltpu.make_async_remote_copy(
                    src_ref=mine, dst_ref=in_refs[k].at[2 * px + py, pl.ds(c * h, h), :],
                    send_sem=send.at[3 * k + j], recv_sem=recv.at[3 * k + j],
                    device_id=(px, py, c), device_id_type=MESH)
                cp.wait_send()
                cp.wait_recv()

    res = pl.pallas_call(
        body, name=name, out_shape=[pltpu.HBM(b.shape, b.dtype) for b in bufs],
        in_specs=[_HBM] * m + [_SEM, _SEM] + [_ANY] * len(_several(after)), out_specs=[_HBM] * m,
        input_output_aliases={k: k for k in range(m)},
        compiler_params=pltpu.CompilerParams(has_side_effects=_EFFECT),
    )(*bufs, send_sems, recv_sems, *_several(after))
    return list(res)


def _forward_halves(ws, name):
    n = len(ws)

    def body(*refs):
        out_refs = refs[n:2 * n]
        send_sems, recv_sems = refs[2 * n:]
        x, y, c = _position()
        me, sibling = (x, y, c), (x, y, 1 - c)
        chips = [(1 - x, y), (x, 1 - y), (1 - x, 1 - y)]
        cps = []
        for i in range(n):
            h = ws[i].shape[1] // 2
            for j, (px, py) in enumerate(chips):
                got = out_refs[i].at[2 * px + py, pl.ds(c * h, h), :]
                cp = pltpu.make_async_remote_copy(
                    src_ref=got, dst_ref=got, send_sem=send_sems.at[3 * i + j], recv_sem=recv_sems.at[3 * i + j],
                    device_id=sibling, device_id_type=MESH)
                cp.start()
                cps.append(cp)
        for i in range(n):
            h = ws[i].shape[1] // 2
            for j, (px, py) in enumerate(chips):
                other = out_refs[i].at[2 * px + py, pl.ds((1 - c) * h, h), :]
                pltpu.make_async_remote_copy(
                    src_ref=other, dst_ref=other, send_sem=send_sems.at[3 * i + j], recv_sem=recv_sems.at[3 * i + j],
                    device_id=me, device_id_type=MESH).wait_recv()
        for cp in cps:
            cp.wait_send()

    return pl.pallas_call(
        body, name=name,
        out_shape=[jax.ShapeDtypeStruct(w.shape, w.dtype) for w in ws],
        in_specs=[_HBM] * n, out_specs=[_HBM] * n, input_output_aliases={i: i for i in range(n)},
        scratch_shapes=[pltpu.SemaphoreType.DMA((3 * n,)), pltpu.SemaphoreType.DMA((3 * n,))],
    )(*ws)


def _copies_start(arrays, copies, nsem, after, name):
    n = len(arrays)
    after = _several(after)
    first = 2 * n + len(after)

    def body(*refs):
        for cp in copies(refs[:n], refs[first], refs[first + 1]):
            cp.start()
        refs[first + 2][...] = jnp.zeros_like(refs[first + 2])

    res = pl.pallas_call(
        body, name=name,
        out_shape=[pltpu.HBM(a.shape, a.dtype) for a in arrays]
        + [pltpu.SemaphoreType.DMA((nsem,)), pltpu.SemaphoreType.DMA((nsem,)), jax.ShapeDtypeStruct((8, 128), F32)],
        in_specs=[_HBM] * n + [_ANY] * len(after),
        out_specs=[_HBM] * n + [_SEM, _SEM, pl.BlockSpec(memory_space=pltpu.VMEM)],
        input_output_aliases={i: i for i in range(n)},
        compiler_params=pltpu.CompilerParams(has_side_effects=_EFFECT),
    )(*[_in_hbm(a) for a in arrays], *after)
    return list(res[:n]), res[n], res[n + 1], res[n + 2]


def _copies_wait(arrays, copies, send_sems, recv_sems, after, name):
    n = len(arrays)

    def body(*refs):
        for cp in copies(refs[:n], refs[n], refs[n + 1]):
            cp.wait_send()
            cp.wait_recv()

    res = pl.pallas_call(
        body, name=name, out_shape=[pltpu.HBM(a.shape, a.dtype) for a in arrays],
        in_specs=[_HBM] * n + [_SEM, _SEM] + [_ANY] * len(_several(after)), out_specs=[_HBM] * n,
        input_output_aliases={i: i for i in range(n)},
        compiler_params=pltpu.CompilerParams(has_side_effects=_EFFECT),
    )(*arrays, send_sems, recv_sems, *_several(after))
    return list(res)


def _scatter_copies(refs, send, recv):
    n = len(refs) // 2
    x, y, c = _position()
    chips = [(1 - x, y), (x, 1 - y), (1 - x, 1 - y)]
    return [pltpu.make_async_remote_copy(
        src_ref=refs[i].at[2 * px + py], dst_ref=refs[n + i].at[j],
        send_sem=send.at[3 * i + j], recv_sem=recv.at[3 * i + j], device_id=(px, py, c), device_id_type=MESH)
        for i in range(n) for j, (px, py) in enumerate(chips)]


def _swap_copies(refs, send, recv):
    n = len(refs) // 2
    x, y, c = _position()
    cps = []
    for i in range(n):
        h = refs[i].shape[1] // 2
        cps.append(pltpu.make_async_remote_copy(
            src_ref=refs[i].at[:, pl.ds((1 - c) * h, h), :], dst_ref=refs[n + i],
            send_sem=send.at[i], recv_sem=recv.at[i], device_id=(x, y, 1 - c), device_id_type=MESH))
    return cps


def _join_copies(refs, send, recv):
    x, y, c = _position()
    cps = []
    for i, r in enumerate(refs):
        h = r.shape[0] // 2
        mine = r.at[pl.ds(c * h, h), :]
        cps.append(pltpu.make_async_remote_copy(
            src_ref=mine, dst_ref=mine, send_sem=send.at[i], recv_sem=recv.at[i],
            device_id=(x, y, 1 - c), device_id_type=MESH))
    return cps


def _forward_copies(refs, send, recv):
    x, y, c = _position()
    chips = [(1 - x, y), (x, 1 - y), (1 - x, 1 - y)]
    cps = []
    for i, r in enumerate(refs):
        h = r.shape[1] // 2
        for j, (px, py) in enumerate(chips):
            got = r.at[2 * px + py, pl.ds(c * h, h), :]
            cps.append(pltpu.make_async_remote_copy(
                src_ref=got, dst_ref=got, send_sem=send.at[3 * i + j], recv_sem=recv.at[3 * i + j],
                device_id=(x, y, 1 - c), device_id_type=MESH))
    return cps


def _t5_buckets_block():
    qi = np.arange(BLOCK)[:, None]
    ki = np.arange(2 * BLOCK)[None, :]
    n = np.maximum(qi + BLOCK - ki, 0)
    max_exact = NUM_BUCKETS // 2
    large = max_exact + (np.log(np.maximum(n, 1) / max_exact) / np.log(MAX_DISTANCE / max_exact)
                         * (NUM_BUCKETS - max_exact)).astype(np.int32)
    large = np.minimum(large, NUM_BUCKETS - 1)
    return np.where(n < max_exact, n, large).astype(np.int32)


def _discretise(lambda_re, lambda_im, log_step, b_re, b_im):
    lam_re = jnp.minimum(lambda_re, -1e-4)
    lam_im = lambda_im
    delta = jnp.exp(log_step)[:, None]
    mag = jnp.exp(lam_re * delta)
    ang = lam_im * delta
    abar_re, abar_im = mag * jnp.cos(ang), mag * jnp.sin(ang)
    num_re, num_im = abar_re - 1.0, abar_im
    den = lam_re * lam_re + lam_im * lam_im
    f_re = (num_re * lam_re + num_im * lam_im) / den
    f_im = (num_im * lam_re - num_re * lam_im) / den
    bbar_re = f_re[..., None] * b_re - f_im[..., None] * b_im
    bbar_im = f_re[..., None] * b_im + f_im[..., None] * b_re
    return abar_re, abar_im, bbar_re, bbar_im


def _interleave(v, nc):
    s, w = v.shape
    return v.reshape(nc, s // nc, w).transpose(1, 0, 2).reshape(s, w)


def _deinterleave(v, nc):
    s, w = v.shape
    return v.reshape(s // nc, nc, w).transpose(1, 0, 2).reshape(s, w)


_SMALL = ("norm1_g", "b_in", "attn_sinks", "rel_bias", "lambda_re", "lambda_im", "log_step", "ssm_b_re",
          "ssm_b_im", "ssm_c_re", "ssm_c_im", "ssm_d", "b_glu", "norm2_g", "final_g")


def _pack(parts):
    rows = []
    for p in parts:
        f = p.reshape(-1).astype(F32)
        pad = (-f.shape[0]) % 128
        rows.append(jnp.pad(f, (0, pad)).reshape(-1, 128))
    out = jnp.concatenate(rows, axis=0)
    pad = (-out.shape[0]) % 256
    return jnp.pad(out, ((0, pad), (0, 0)))


def _unpack(packed, shapes):
    res, r = [], 0
    for shp in shapes:
        size = int(np.prod(shp))
        nr = -(-size // 128)
        res.append(packed[r:r + nr].reshape(-1)[:size].reshape(shp))
        r += nr
    return res


def kernel(x, c, w_ada, b_ada, norm1_g, w_in, b_in, attn_sinks, rel_bias, lambda_re, lambda_im, log_step, ssm_b_re, ssm_b_im, ssm_c_re, ssm_c_im, ssm_d, w_glu, b_glu, w_attn_proj, w_ssm_proj, w_out, norm2_g, w_ff1, w_ff2, final_g, loss_target, m_w_ada, m_b_ada, m_norm1_g, m_w_in, m_b_in, m_attn_sinks, m_rel_bias, m_lambda_re, m_lambda_im, m_log_step, m_ssm_b_re, m_ssm_b_im, m_ssm_c_re, m_ssm_c_im, m_ssm_d, m_w_glu, m_b_glu, m_w_attn_proj, m_w_ssm_proj, m_w_out, m_norm2_g, m_w_ff1, m_w_ff2, m_final_g, v_w_ada, v_b_ada, v_norm1_g, v_w_in, v_b_in, v_attn_sinks, v_rel_bias, v_lambda_re, v_lambda_im, v_log_step, v_ssm_b_re, v_ssm_b_im, v_ssm_c_re, v_ssm_c_im, v_ssm_d, v_w_glu, v_b_glu, v_w_attn_proj, v_w_ssm_proj, v_w_out, v_norm2_g, v_w_ff1, v_w_ff2, v_final_g):
    given = dict(locals())
    S, D = x.shape[1], x.shape[2]
    SSM_W = w_glu.shape[2]
    G = SSM_W // SSM_GROUP_CH
    NST = G * SSM_STATE
    DFF = w_ff2.shape[1] * N_CHIPS
    INW = w_in.shape[2] * N_CHIPS
    o_q, o_k, o_v, o_u = 0, ATTN_WIDTH, ATTN_WIDTH + KV_WIDTH, ATTN_WIDTH + 2 * KV_WIDTH
    o_ga, o_gs = o_u + SSM_W, o_u + SSM_W + D
    mx, my, mc = _position()
    my_chip = 2 * mx + my
    my_b = 4 * mx + 2 * my + mc

    xv, tgt = x[0], loss_target[0]

    big = dict(w_in=w_in[0], w_glu=w_glu[0], w_attn_proj=w_attn_proj[0], w_ssm_proj=w_ssm_proj[0],
               w_out=w_out[0], w_ff1=w_ff1[0], w_ff2=w_ff2[0])
    big_names = list(big)
    colsharded = {"w_in", "w_attn_proj", "w_ssm_proj", "w_ff1"}
    chip_sel = my_chip.astype(jnp.int32).reshape(1)
    gather_groups = [["w_in"], ["w_attn_proj", "w_ssm_proj", "w_glu", "w_out"], ["w_ff1", "w_ff2"]]
    in_flight, gather_sems, gathered = {}, [], {}

    def finish_gather(g, after):
        bufs = [in_flight[k] for k in gather_groups[g]]
        bufs = _gather_wait(bufs, gather_sems[g][0], gather_sems[g][1], after, "gather_wait_%d" % g)
        gathered.update(zip(gather_groups[g], _forward_halves(bufs, "gather_forward_%d" % g)))

    def tied(v, token):
        return v + token[0:1, 0:1]

    def all_of(*arrays):
        return list(arrays)

    def wop(k):
        g = gathered[k]
        return _Op(g, N_CHIPS) if k in colsharded else _Op(g.reshape(g.shape[0] * g.shape[1], g.shape[2]))

    grads = {}
    nothing = jnp.zeros((8, 128), F32)
    half = mc.astype(jnp.int32).reshape(1)
    sel = jnp.stack([my_chip, mc]).astype(jnp.int32)

    def rs_swap(tag, named):
        keys, gl = list(named), []
        for k in keys:
            gk = named[k]
            if k not in colsharded:
                gk = gk.reshape(N_CHIPS, gk.shape[0] // N_CHIPS, gk.shape[1])
            gl.append(gk)
        lands = [lax.empty((g.shape[0], g.shape[1] // 2, g.shape[2]), g.dtype) for g in gl]
        arrays, ssem, rsem, token = _copies_start(gl + lands, _swap_copies, len(gl), nothing, "rs_swap_start_" + tag)
        return (keys, arrays, ssem, rsem), token

    def rs_scatter(tag, state, after):
        keys, arrays, ssem, rsem = state
        arrays = _copies_wait(arrays, _swap_copies, ssem, rsem, after, "rs_swap_wait_" + tag)
        n = len(keys)
        ps = [_add_half(g, t, half, "rs_add_" + k) for g, t, k in zip(arrays[:n], arrays[n:], keys)]
        lands = [lax.empty((3,) + p.shape[1:], p.dtype) for p in ps]
        arrays, ssem, rsem, token = _copies_start(ps + lands, _scatter_copies, 3 * n, nothing, "rs_start_" + tag)
        return (keys, arrays, ssem, rsem), token

    def rs_sum(tag, state, after):
        keys, arrays, ssem, rsem = state
        arrays = _copies_wait(arrays, _scatter_copies, ssem, rsem, after, "rs_wait_" + tag)
        n = len(keys)
        rs = [_sum_own(p, t, sel, "rs_sum_" + k) for p, t, k in zip(arrays[:n], arrays[n:], keys)]
        rs, ssem, rsem, token = _copies_start(rs, _join_copies, n, nothing, "rs_join_start_" + tag)
        return (keys, rs, ssem, rsem), token

    def rs_finish(tag, state, after):
        keys, rs, ssem, rsem = state
        for k, f in zip(keys, _copies_wait(rs, _join_copies, ssem, rsem, after, "rs_join_wait_" + tag)):
            grads[k] = f[None]

    c_all = _allgather8(jnp.pad(c, ((0, 7), (0, 0))), "gather_c").reshape(N_DEV, 8, D)[:, 0]
    c16 = jnp.pad(c_all, ((0, 8), (0, 0)))
    b_ada_mine = lax.dynamic_slice(b_ada.reshape(N_CHIPS, -1), (my_chip, 0), (1, w_ada.shape[2]))
    mod_sh = _mm(c16, w_ada[0], "NN", name="mod", M=16, N=w_ada.shape[2], K=D, a_fn=_silu,
                 epilogue=lambda acc, b: (acc + b,), extras=[(b_ada_mine, "row")])
    mod_all = _allgather8(mod_sh[:8], "gather_mod").reshape(N_DEV, 8, -1)
    mod_row = jnp.concatenate(
        [lax.dynamic_slice(mod_all, (2 * j, my_b, 0), (1, 1, mod_all.shape[2]))[0] for j in range(N_CHIPS)], axis=1)
    sh1, sc1, g1, sh2, sc2, g2 = [mod_row[:, i * D:(i + 1) * D] for i in range(6)]

    first = [_cast_into_slot(big["w_in"], chip_sel, "cast_w_in")]
    first, sems_first, token_first = _gather_start(first, [[0]], mod_all, "gather_start_in")
    rest_names = gather_groups[1] + gather_groups[2]
    rest = [_cast_into_slot(big[k], chip_sel, "cast_" + k) for k in rest_names]
    rest, sems_rest, token_rest = _gather_start(
        rest, [[rest_names.index(k) for k in grp] for grp in gather_groups[1:]], token_first, "gather_start_rest")
    in_flight.update(zip(["w_in"] + rest_names, first + rest))
    gather_sems.extend(sems_first + sems_rest)

    disc_in = (lambda_re[0], lambda_im[0], log_step[0], ssm_b_re[0], ssm_b_im[0])
    (abar_re, abar_im, bbar_re, bbar_im), disc_vjp = jax.vjp(_discretise, *disc_in)
    same_group = jnp.asarray(np.arange(SSM_W)[:, None] // SSM_GROUP_CH == np.arange(NST)[None, :] // SSM_STATE)

    def block_diag(t):
        return jnp.where(same_group, jnp.tile(t, (G, 1)), 0.0)

    bd = jnp.concatenate([block_diag(bb.transpose(2, 0, 1).reshape(SSM_GROUP_CH, NST)) for bb in (bbar_re, bbar_im)],
                         axis=1)
    cd = jnp.concatenate([block_diag(cc.transpose(1, 0, 2).reshape(SSM_GROUP_CH, NST)).T
                          for cc in (ssm_c_re[0], -ssm_c_im[0])], axis=0)
    a_fwd = jnp.stack([abar_re.reshape(1, NST), abar_im.reshape(1, NST)])
    a_bwd = jnp.stack([abar_re.reshape(1, NST), -abar_im.reshape(1, NST)])
    d_row = ssm_d

    buckets = _t5_buckets_block()
    onehot_t = (jnp.arange(128, dtype=jnp.int32)[:, None] == jnp.asarray(buckets.reshape(1, -1))).astype(BF16)
    rb_hi = rel_bias.astype(BF16)
    rb_lo = (rel_bias - rb_hi.astype(F32)).astype(BF16)
    rb_lo2 = (rel_bias - rb_hi.astype(F32) - rb_lo.astype(F32)).astype(BF16)
    rb3 = jnp.pad(jnp.concatenate([rb_hi.T, rb_lo.T, rb_lo2.T], axis=0), ((0, 0), (0, 128 - NUM_BUCKETS)))
    b3 = _mm(rb3, onehot_t, "NN", name="rel_bias_rows", M=3 * N_Q_HEADS, N=BLOCK * 2 * BLOCK, K=128, tj=4096)
    bias = (b3[:N_Q_HEADS] + b3[N_Q_HEADS:2 * N_Q_HEADS]) + b3[2 * N_Q_HEADS:]
    bias = bias.reshape(N_Q_HEADS, BLOCK, 2 * BLOCK)
    sinks_b = jnp.broadcast_to(attn_sinks[0][:, None, None], (N_Q_HEADS, BLOCK, 128)).reshape(N_Q_HEADS * BLOCK, 128)

    def two(fn):
        def both(*blocks):
            r = fn(*blocks)
            return r, r
        return both

    h1, h1_t = _rowwise(two(_norm_mod), [(xv, "tile", D), (tied(tied(norm1_g, token_first), token_rest), "row", D),
                                         (sh1, "row", D), (sc1, "row", D)],
                        [(D, BF16), (D, BF16, "T")], [], name="norm1", rows=S)
    finish_gather(0, all_of(h1, bd, cd, a_fwd, a_bwd, bias, sinks_b))
    proj = _mm(h1, wop("w_in"), "NN", name="proj", M=S, N=INW, K=D, out_dtypes=(BF16,),
               epilogue=lambda acc, b: (acc + b,), extras=[(b_in, "row")])

    def heads(v2d, nh):
        return v2d.reshape(S, nh, HEAD_DIM).transpose(1, 0, 2)

    def unheads(v3d):
        return v3d.transpose(1, 0, 2).reshape(S, -1)

    qh = heads(proj[:, o_q:o_k], N_Q_HEADS)
    kh = heads(proj[:, o_k:o_v], N_KV_HEADS)
    vh = heads(proj[:, o_v:o_u], N_KV_HEADS)
    attn = unheads(_attn_fwd(qh, kh, vh, sinks_b, bias, "attn_fwd"))
    finish_gather(1, attn)
    y_attn = _mm(attn, wop("w_attn_proj"), "NN", name="attn_proj", M=S, N=D, K=ATTN_WIDTH, out_dtypes=(BF16,))

    u = proj[:, o_u:o_ga]
    u_il = _interleave(u, SCAN_CHUNKS)
    SB = 128
    nsb, gpb = SSM_W // SB, SB // SSM_GROUP_CH
    SBN = gpb * SSM_STATE
    y_il, xs = _ssm_fwd(u_il, bd, cd, a_fwd, d_row, name="ssm_fwd", sb=SB, sbn=SBN)
    y = _deinterleave(y_il, SCAN_CHUNKS)
    z, t_glu = _mm(y, wop("w_glu"), "NN", name="glu", M=S, N=SSM_W, K=SSM_W, out_dtypes=(BF16, F32), a_fn=_gelu,
                   epilogue=lambda acc, b, yy: (_gelu(yy) * _sigmoid(acc + b), acc + b),
                   extras=[(b_glu, "row"), (y, "tile")])
    y_ssm = _mm(z, wop("w_ssm_proj"), "NN", name="ssm_proj", M=S, N=D, K=SSM_W, out_dtypes=(BF16,))

    ff_bufs = _gather_wait([in_flight[k] for k in gather_groups[2]], gather_sems[2][0], gather_sems[2][1], all_of(y_ssm),
                           "gather_wait_2")
    ff_bufs, ff_send, ff_recv, token = _copies_start(ff_bufs, _forward_copies, 3 * len(ff_bufs), nothing,
                                                    "gather_forward_2_start")
    merged, merged_t = _rowwise(two(_merge), [(_Op(proj, coff=o_ga), "tile", D), (_Op(proj, coff=o_gs), "tile", D),
                                              (y_attn, "tile", D), (y_ssm, "tile", D)],
                                [(D, BF16), (D, BF16, "T")], [], name="merge", rows=S, deps=[token])
    mo, x2 = _mm(merged, wop("w_out"), "NN", name="out_proj", M=S, N=D, K=D, out_dtypes=(BF16, F32),
                 epilogue=lambda acc, xx, gg: (acc, xx + gg * acc), extras=[(xv, "tile"), (g1, "row")])
    h2, h2_t = _rowwise(two(_norm_mod), [(x2, "tile", D), (norm2_g, "row", D), (sh2, "row", D), (sc2, "row", D)],
                        [(D, BF16), (D, BF16, "T")], [], name="norm2", rows=S)
    gathered.update(zip(gather_groups[2], _copies_wait(ff_bufs, _forward_copies, ff_send, ff_recv, h2,
                                                       "gather_forward_2_wait")))
    a_b, r_b = _mm(h2, wop("w_ff1"), "NN", name="ff1", M=S, N=DFF, K=D, out_dtypes=(BF16, BF16),
                   epilogue=lambda acc: (acc, jnp.square(jnp.maximum(acc, 0.0))))
    ff, x3 = _mm(r_b, wop("w_ff2"), "NN", name="ff2", M=S, N=D, K=DFF, out_dtypes=(BF16, F32),
                 epilogue=lambda acc, xx, gg: (acc, xx + gg * acc), extras=[(x2, "tile"), (g2, "row")],
                 tj=1024, tk=1024)

    def final_fn(x3b, gf, tb, ffb, g2b):
        def f(xx, gg):
            yv = xx * lax.rsqrt(jnp.mean(xx * xx, axis=-1, keepdims=True) + EPS) * gg
            err = jnp.square(yv - tb)
            return 0.5 * jnp.sum(jnp.mean(err, axis=-1, keepdims=True), axis=0, keepdims=True)
        lv, vjp = jax.vjp(f, x3b, gf)
        dx, dg = vjp(jnp.ones((1, 1), F32))
        return dx, dx * g2b, dg, jnp.broadcast_to(lv, (1, 128)), jnp.sum(dx * ffb, axis=0, keepdims=True)

    dx3, dff, g_final, loss_acc, d_g2 = _rowwise(
        final_fn, [(x3, "tile", D), (final_g.reshape(1, D), "row", D), (tgt, "tile", D), (ff, "tile", D), (g2, "row", D)],
        [(D, F32), (D, BF16)], [D, 128, D], name="final", rows=S)
    da = _mm(dff, wop("w_ff2"), "NT", name="ff2_dx", M=S, N=DFF, K=D, out_dtypes=(BF16,),
             epilogue=lambda acc, ab: (acc * (2.0 * jnp.maximum(ab.astype(F32), 0.0)),), extras=[(a_b, "tile")])
    g_w_ff2 = _mm(r_b, dff, "TN", name="ff2_dw", M=DFF, N=D, K=S, out_dtypes=(BF16,), tj=1024, tk=1024)
    g_w_ff1 = _mm(h2_t, da, "NN", name="ff1_dw", M=D, N=DFF, K=S, out_dtypes=(BF16,), out_nsh=N_CHIPS, tj=1024, tk=1024)
    rs_ff, token = rs_swap("ff", dict(w_ff2=g_w_ff2, w_ff1=g_w_ff1))
    dh2 = _mm(da, wop("w_ff1"), "NT", name="ff1_dx", M=S, N=D, K=DFF, tj=1024, tk=1024, deps=[token])
    rs_ff, token_ff = rs_scatter("ff", rs_ff, dh2)

    def norm2_bwd(x2b, dh2b, dx3b, mob, gn, shb, scb, g1b):
        _, vjp = jax.vjp(_norm_mod, x2b, gn, shb, scb)
        dx, dg, dsh, dsc = vjp(dh2b)
        dx2b = dx + dx3b
        return dx2b, dx2b * g1b, dg, dsh, dsc, jnp.sum(dx2b * mob, axis=0, keepdims=True)

    dx2, dmo, g_norm2, d_sh2, d_sc2, d_g1 = _rowwise(
        norm2_bwd, [(x2, "tile", D), (dh2, "tile", D), (dx3, "tile", D), (mo, "tile", D),
                    (tied(norm2_g, token_ff), "row", D), (sh2, "row", D), (sc2, "row", D), (g1, "row", D)],
        [(D, F32), (D, BF16)], [D, D, D, D], name="norm2_bwd", rows=S)
    dmerged = _mm(dmo, wop("w_out"), "NT", name="out_dx", M=S, N=D, K=D)
    g_w_out = _mm(merged_t, dmo, "NN", name="out_dw", M=D, N=D, K=S, out_dtypes=(BF16,), tj=1024, tk=1024)

    def merge_bwd(gab, gsb, yab, ysb, dmb):
        _, vjp = jax.vjp(_merge, gab, gsb, yab, ysb)
        return vjp(dmb)

    d_ga, d_gs, dy_attn, dy_ssm = _rowwise(
        merge_bwd, [(_Op(proj, coff=o_ga), "tile", D), (_Op(proj, coff=o_gs), "tile", D), (y_attn, "tile", D),
                    (y_ssm, "tile", D), (dmerged, "tile", D)],
        [(D, BF16), (D, BF16), (D, BF16), (D, BF16)], [], name="merge_bwd", rows=S)

    dattn = _mm(dy_attn, wop("w_attn_proj"), "NT", name="attn_proj_dx", M=S, N=ATTN_WIDTH, K=D, tj=1024, out_dtypes=(BF16,))
    g_w_attn_proj = _mm(attn, dy_attn, "TN", name="attn_proj_dw", M=ATTN_WIDTH, N=D, K=S, out_dtypes=(BF16,),
                        out_nsh=N_CHIPS, tk=1024)

    dz = _mm(dy_ssm, wop("w_ssm_proj"), "NT", name="ssm_proj_dx", M=S, N=SSM_W, K=D)
    g_w_ssm_proj = _mm(z, dy_ssm, "TN", name="ssm_proj_dw", M=SSM_W, N=D, K=S, out_dtypes=(BF16,),
                       out_nsh=N_CHIPS, tk=1024)

    def glu_bwd(dzb, yb, tb):
        z0 = _gelu(yb)
        sg = _sigmoid(tb)
        dt = dzb * z0 * sg * (1.0 - sg)
        return dt, dzb * sg, jnp.sum(dt, axis=0, keepdims=True)

    dt_b, dz0a, g_b_glu = _rowwise(glu_bwd, [(dz, "tile", SSM_W), (y, "tile", SSM_W), (t_glu, "tile", SSM_W)],
                                   [(SSM_W, BF16), (SSM_W, F32)], [SSM_W], name="glu_bwd", rows=S)

    def gelu_bwd(acc, dz0ab, yb):
        _, vjp = jax.vjp(_gelu, yb)
        return (vjp(acc + dz0ab)[0],)

    dy = _mm(dt_b, wop("w_glu"), "NT", name="glu_dx", M=S, N=SSM_W, K=SSM_W, epilogue=gelu_bwd,
             extras=[(dz0a, "tile"), (y, "tile")])
    g_w_glu = _mm(y, dt_b, "TN", name="glu_dw", M=SSM_W, N=SSM_W, K=S, out_dtypes=(BF16,), tk=1024, a_fn=_gelu)
    rs_mix, token = rs_swap("mix", dict(w_out=g_w_out, w_attn_proj=g_w_attn_proj, w_ssm_proj=g_w_ssm_proj,
                                        w_glu=g_w_glu))
    dy_il = _interleave(dy, SCAN_CHUNKS)
    du_il, g_bd, g_cd, d_abar, g_ssm_d = _ssm_bwd(dy_il, u_il, xs, bd, cd, a_bwd, d_row, name="ssm_bwd", sb=SB, sbn=SBN,
                                                  deps=[token])
    du = _deinterleave(du_il, SCAN_CHUNKS)
    rs_mix, token_mix = rs_scatter("mix", rs_mix, du_il)

    dqh, dkh, dvh, dsink_blk, dbias = _attn_bwd(qh, kh, vh, heads(dattn, N_Q_HEADS), tied(sinks_b, token_mix), bias,
                                                "attn_bwd")
    g_sinks = _sum_lead(dsink_blk.reshape(N_Q_HEADS, BLOCK, 128).transpose(1, 0, 2), "sinks_dw")[:, 0].reshape(1, N_Q_HEADS)
    g_rel = _mm(dbias.reshape(N_Q_HEADS, -1), onehot_t, "NT", name="rel_bias_dw", M=N_Q_HEADS, N=128,
                K=BLOCK * 2 * BLOCK, tk=4096)
    g_rel_bias = g_rel[:, :NUM_BUCKETS].T

    eye_b = jnp.eye(gpb, dtype=F32)
    g_cd6 = g_cd.reshape(2, nsb, gpb, SSM_STATE, gpb, SSM_GROUP_CH)
    g_c_re = jnp.einsum("bgnhp,gh->bgpn", g_cd6[0], eye_b).reshape(G, SSM_GROUP_CH, SSM_STATE)
    g_c_im = -jnp.einsum("bgnhp,gh->bgpn", g_cd6[1], eye_b).reshape(G, SSM_GROUP_CH, SSM_STATE)
    g_bd6 = g_bd.reshape(nsb, gpb, SSM_GROUP_CH, 2, gpb, SSM_STATE)
    g_bbar = jnp.einsum("bhprgn,hg->rbhnp", g_bd6, eye_b).reshape(2, G, SSM_STATE, SSM_GROUP_CH)
    g_bbar_re, g_bbar_im = g_bbar[0], g_bbar[1]
    g_lre, g_lim, g_lstep, g_bre, g_bim = disc_vjp(
        (d_abar[0].reshape(G, SSM_STATE), d_abar[1].reshape(G, SSM_STATE), g_bbar_re, g_bbar_im))

    dproj = jnp.concatenate([unheads(dqh).astype(BF16), unheads(dkh).astype(BF16), unheads(dvh).astype(BF16),
                             du.astype(BF16), d_ga, d_gs], axis=1)
    g_w_in = _mm(h1_t, dproj, "NN", name="proj_dw", M=D, N=INW, K=S, out_dtypes=(BF16,), out_nsh=N_CHIPS,
                 tj=INW // (2 * N_CHIPS), tk=1024)
    rs_in, token = rs_swap("in", dict(w_in=g_w_in))
    dh1 = _mm(dproj, wop("w_in"), "NT", name="proj_dx", M=S, N=D, K=INW, tj=1024, tk=INW // N_CHIPS, deps=[token])
    g_b_in = _rowwise(lambda d: (jnp.sum(d.astype(F32), axis=0, keepdims=True),), [(dproj, "tile", INW)], [], [INW],
                      name="proj_db", rows=S)[0]

    def norm1_bwd(xb, dhb, dresb, gn, shb, scb):
        _, vjp = jax.vjp(_norm_mod, xb, gn, shb, scb)
        dx, dg, dsh, dsc = vjp(dhb)
        return dx + dresb, dg, dsh, dsc

    grad_x, g_norm1, d_sh1, d_sc1 = _rowwise(
        norm1_bwd, [(xv, "tile", D), (dh1, "tile", D), (dx2, "tile", D), (norm1_g, "row", D),
                    (sh1, "row", D),
                    (sc1, "row", D)], [(D, F32)], [D, D, D], name="norm1_bwd", rows=S)

    dmod_row = jnp.concatenate([d_sh1, d_sc1, d_g1, d_sh2, d_sc2, d_g2], axis=1)
    dmod_all = _allgather8(jnp.pad(dmod_row, ((0, 7), (0, 0))), "gather_dmod").reshape(N_DEV, 8, -1)[:, 0]
    g_b_ada = _sum_lead(dmod_all.reshape(N_DEV, -1, 128), "b_ada_dw").reshape(1, -1)
    dmod_mine = lax.dynamic_slice(dmod_all.reshape(N_DEV, N_CHIPS, -1), (0, my_chip, 0), (N_DEV, 1, w_ada.shape[2]))[:, 0]
    g_w_ada = _mm(c16, jnp.pad(dmod_mine, ((0, 8), (0, 0))), "TN", name="ada_dw", M=D, N=w_ada.shape[2], K=16,
                  a_fn=_silu)

    small_g = dict(norm1_g=g_norm1, b_in=g_b_in, attn_sinks=g_sinks, rel_bias=g_rel_bias, lambda_re=g_lre[None],
                   lambda_im=g_lim[None], log_step=g_lstep[None], ssm_b_re=g_bre[None], ssm_b_im=g_bim[None],
                   ssm_c_re=g_c_re[None], ssm_c_im=g_c_im[None], ssm_d=g_ssm_d, b_glu=g_b_glu, norm2_g=g_norm2,
                   final_g=g_final.reshape(D))
    packed = _pack([loss_acc[:, :1]] + [small_g[k] for k in _SMALL])
    rows = packed.shape[0]
    summed = _sum_lead(_allgather8(packed, "gather_small").reshape(N_DEV, rows, 128), "small_sum")
    small_shapes = [(1,)] + [given[k].shape for k in _SMALL]
    parts = _unpack(summed, small_shapes)
    loss = parts[0].reshape(())
    grads.update(zip(_SMALL, parts[1:]))
    grads["b_ada"] = g_b_ada
    grads["w_ada"] = g_w_ada[None]

    deltas, new_m, new_v = {}, {}, {}

    def adamw_big(k, deps=()):
        d_, m_, v_ = _adamw(given[k][0], grads[k][0], given["m_" + k][0], given["v_" + k][0], "adamw_" + k, deps)
        deltas[k], new_m[k], new_v[k] = d_[None], m_[None], v_[None]
        return v_

    rs_in, token_in = rs_scatter("in", rs_in, all_of(summed, dmod_all))
    rs_ff, token = rs_sum("ff", rs_ff, all_of(summed, token_in))
    mark = adamw_big("w_ada", [token])
    rs_mix, token = rs_sum("mix", rs_mix, mark)
    small_all = list(_SMALL) + ["b_ada"]
    for k in small_all:
        grads[k] = grads[k].reshape(given[k].shape)

    def rows_of(a):
        return a.reshape(1, -1) if a.ndim == 1 else a

    d_, m_, v_ = _adamw_many(*[[rows_of(src[k]) for k in small_all] for src in (
        given, grads, {k: given["m_" + k] for k in small_all}, {k: given["v_" + k] for k in small_all})],
        "adamw_small", [token])
    for k, dd, mm, vv in zip(small_all, d_, m_, v_):
        deltas[k], new_m[k], new_v[k] = (t.reshape(given[k].shape) for t in (dd, mm, vv))
    v_ = v_[0]
    rs_finish("ff", rs_ff, v_)
    marks = [adamw_big(k) for k in ("w_ff2", "w_ff1")]
    rs_finish("mix", rs_mix, all_of(*marks))
    marks = [adamw_big(k) for k in ("w_out", "w_attn_proj", "w_ssm_proj", "w_glu")]
    rs_in, token = rs_sum("in", rs_in, all_of(*marks))
    rs_finish("in", rs_in, token)
    adamw_big("w_in")

    names = ["w_ada", "b_ada", "norm1_g", "w_in", "b_in", "attn_sinks", "rel_bias", "lambda_re", "lambda_im",
             "log_step", "ssm_b_re", "ssm_b_im", "ssm_c_re", "ssm_c_im", "ssm_d", "w_glu", "b_glu", "w_attn_proj",
             "w_ssm_proj", "w_out", "norm2_g", "w_ff1", "w_ff2", "final_g"]
    return (loss, grad_x[None], *[grads[n] for n in names], *[deltas[n] for n in names],
            *[new_m[n] for n in names], *[new_v[n] for n in names])
```

```python
import math

import numpy as np
import jax
import jax.numpy as jnp
from jax import lax
from jax.experimental import pallas as pl
from jax.experimental.pallas import tpu as pltpu

F32 = jnp.float32
BF16 = jnp.bfloat16
MESH = pl.DeviceIdType.MESH

HEAD_DIM = 64
N_Q_HEADS = 16
N_KV_HEADS = 4
GQA_GROUP = N_Q_HEADS // N_KV_HEADS
ATTN_WIDTH = N_Q_HEADS * HEAD_DIM
KV_WIDTH = N_KV_HEADS * HEAD_DIM
BLOCK = 128
NUM_BUCKETS = 32
MAX_DISTANCE = 128
NEG_INF = -1e30
SSM_GROUP_CH = 16
SSM_STATE = 64
EPS = 1e-6
ADAM_LR = 0.001
ADAM_B1 = 0.9
ADAM_B2 = 0.999
ADAM_EPS = 1e-08
ADAM_WD = 0.01
ADAM_STEP = 10

N_CHIPS = 4
N_DEV = 8
SCAN_CHUNKS = 8
VMEM_LIMIT_BYTES = 48 * 1024 * 1024
SSM_VMEM_LIMIT_BYTES = 56 * 1024 * 1024


def _cparams(sem=None):
    return pltpu.CompilerParams(dimension_semantics=sem, vmem_limit_bytes=VMEM_LIMIT_BYTES)


class _Op:
    def __init__(self, arr, nsh=None, coff=0):
        self.arr, self.nsh, self.coff = arr, nsh, coff
        if nsh is None:
            self.rows, self.cols = arr.shape
        else:
            assert arr.shape[0] == nsh
            self.rows, self.cols = arr.shape[1], arr.shape[2] * nsh

    def spec(self, br, bc, idx):
        assert self.coff % bc == 0
        off = self.coff // bc
        if self.nsh is None:
            return pl.BlockSpec((br, bc), lambda *g: (idx(*g)[0], idx(*g)[1] + off))
        per = (self.cols // self.nsh) // bc
        assert per * bc * self.nsh == self.cols

        def imap(*g):
            r, c = idx(*g)
            c = c + off
            return (c // per, r, c % per)
        return pl.BlockSpec((None, br, bc), imap)


def _as_op(a):
    return a if isinstance(a, _Op) else _Op(a)


def _mm(a, b, mode, *, name, M, N, K, out_dtypes=(F32,), out_nsh=None, epilogue=None, extras=(),
        a_fn=None, ti=1024, tj=512, tk=2048, deps=()):
    nd = len(deps)
    a, b = _as_op(a), _as_op(b)
    ti, tj, tk = min(ti, M), min(tj, N), min(tk, K)
    a_w = a.cols // a.nsh if a.nsh else None
    b_w = b.cols // b.nsh if b.nsh else None
    if a_w:
        ti, tk = (min(ti, a_w), tk) if mode == "TN" else (ti, min(tk, a_w))
    if b_w:
        tj, tk = (tj, min(tk, b_w)) if mode == "NT" else (min(tj, b_w), tk)
    if out_nsh:
        tj = min(tj, N // out_nsh)
    assert M % ti == 0 and N % tj == 0 and K % tk == 0, (name, M, N, K, ti, tj, tk)
    nk = K // tk
    if mode == "NN":
        a_spec = a.spec(ti, tk, lambda i, j, k: (i, k))
        b_spec = b.spec(tk, tj, lambda i, j, k: (k, j))
        dims = (((1,), (0,)), ((), ()))
    elif mode == "NT":
        a_spec = a.spec(ti, tk, lambda i, j, k: (i, k))
        b_spec = b.spec(tj, tk, lambda i, j, k: (j, k))
        dims = (((1,), (1,)), ((), ()))
    else:
        a_spec = a.spec(tk, ti, lambda i, j, k: (k, i))
        b_spec = b.spec(tk, tj, lambda i, j, k: (k, j))
        dims = (((0,), (0,)), ((), ()))
    ex_specs, ex_arrs = [], []
    for op, kind in extras:
        op = _as_op(op)
        if kind == "tile":
            ex_specs.append(op.spec(ti, tj, lambda i, j, k: (i, j)))
        else:
            ex_specs.append(op.spec(1, tj, lambda i, j, k: (0, j)))
        ex_arrs.append(op.arr)
    ne, no = len(ex_arrs), len(out_dtypes)
    if out_nsh is None:
        out_shapes = [jax.ShapeDtypeStruct((M, N), d) for d in out_dtypes]
        out_specs = [pl.BlockSpec((ti, tj), lambda i, j, k: (i, j)) for _ in out_dtypes]
    else:
        per = (N // out_nsh) // tj
        assert per * tj * out_nsh == N
        out_shapes = [jax.ShapeDtypeStruct((out_nsh, M, N // out_nsh), d) for d in out_dtypes]
        out_specs = [pl.BlockSpec((None, ti, tj), lambda i, j, k: (j // per, i, j % per)) for _ in out_dtypes]

    def body(a_ref, b_ref, *rest):
        ex_refs, out_refs, acc = rest[:ne], rest[ne + nd:ne + nd + no], rest[ne + nd + no]
        k = pl.program_id(2)

        @pl.when(k == 0)
        def _():
            acc[...] = jnp.zeros_like(acc)

        av = a_ref[...]
        if a_fn is not None:
            av = a_fn(av)
        acc[...] += lax.dot_general(av.astype(BF16), b_ref[...].astype(BF16), dims,
                                    preferred_element_type=F32)

        @pl.when(k == nk - 1)
        def _():
            res = acc[...]
            outs = epilogue(res, *[r[...] for r in ex_refs]) if epilogue is not None else (res,)
            for o_ref, o in zip(out_refs, outs):
                o_ref[...] = o.astype(o_ref.dtype)

    outs = pl.pallas_call(
        body, name=name, grid=(M // ti, N // tj, nk),
        in_specs=[a_spec, b_spec] + ex_specs + [pl.BlockSpec(memory_space=pl.ANY)] * nd,
        out_specs=out_specs, out_shape=out_shapes,
        scratch_shapes=[pltpu.VMEM((ti, tj), F32)],
        compiler_params=_cparams(("parallel", "parallel", "arbitrary")),
    )(a.arr, b.arr, *ex_arrs, *deps)
    return outs[0] if no == 1 else outs


def _rowwise(fn, ins, outs, accs, *, name, rows, tr=256, deps=()):
    tr = min(tr, rows)
    assert rows % tr == 0
    in_specs, arrs = [], []
    for op, kind, width in ins:
        op = _as_op(op)
        if kind == "tile":
            in_specs.append(op.spec(tr, width, lambda i: (i, 0)))
        else:
            in_specs.append(op.spec(op.rows, width, lambda i: (0, 0)))
        arrs.append(op.arr)
    ni, no, na = len(ins), len(outs), len(accs)
    flipped = [len(o) == 3 for o in outs]
    out_shapes = [jax.ShapeDtypeStruct((o[0], rows) if t else (rows, o[0]), o[1]) for o, t in zip(outs, flipped)]
    out_specs = [pl.BlockSpec((o[0], tr), lambda i: (0, i)) if t else pl.BlockSpec((tr, o[0]), lambda i: (i, 0))
                 for o, t in zip(outs, flipped)]
    out_shapes += [jax.ShapeDtypeStruct((1, w), F32) for w in accs]
    out_specs += [pl.BlockSpec((1, w), lambda i: (0, 0)) for w in accs]

    def body(*refs):
        nd = len(deps)
        in_refs, out_refs, acc_refs = refs[:ni], refs[ni + nd:ni + nd + no], refs[ni + nd + no:]
        res = fn(*[r[...] for r in in_refs])
        if not isinstance(res, (tuple, list)):
            res = (res,)
        for o_ref, r, t in zip(out_refs, res[:no], flipped):
            o_ref[...] = (r.astype(F32).T if t else r).astype(o_ref.dtype)
        if na:
            @pl.when(pl.program_id(0) == 0)
            def _():
                for a_ref in acc_refs:
                    a_ref[...] = jnp.zeros_like(a_ref)
            for a_ref, r in zip(acc_refs, res[no:]):
                a_ref[...] += r.astype(F32)

    res = pl.pallas_call(
        body, name=name, grid=(rows // tr,), in_specs=in_specs + [pl.BlockSpec(memory_space=pl.ANY)] * len(deps),
        out_specs=out_specs, out_shape=out_shapes, compiler_params=_cparams(("arbitrary",)),
    )(*arrs, *deps)
    return res


def _norm_mod(x, g, sh, sc):
    y = x * lax.rsqrt(jnp.mean(x * x, axis=-1, keepdims=True) + EPS) * g
    return y * (1.0 + sc) + sh


def _sigmoid(x):
    return 1.0 / (1.0 + jnp.exp(-x))


def _silu(x):
    return x * _sigmoid(x)


def _gelu(x):
    return 0.5 * x * (1.0 + jnp.tanh(math.sqrt(2.0 / math.pi) * (x + 0.044715 * (x * x * x))))


def _merge(ga, gs, ya, ys):
    ga, gs, ya, ys = (v.astype(F32) for v in (ga, gs, ya, ys))
    return _sigmoid(ga) * ya + _sigmoid(gs) * ys


def _attn_head(q, kp, kc, vp, vc, sink, bias_p, bias_c, not_first):
    nt = (((1,), (1,)), ((), ()))
    nn = (((1,), (0,)), ((), ()))
    qb = q.astype(BF16)
    scale = HEAD_DIM ** -0.5
    sp = lax.dot_general(qb, kp.astype(BF16), nt, preferred_element_type=F32) * scale + bias_p
    sc = lax.dot_general(qb, kc.astype(BF16), nt, preferred_element_type=F32) * scale + bias_c
    qi = lax.broadcasted_iota(jnp.int32, sp.shape, 0) & (BLOCK - 1)
    ki = lax.broadcasted_iota(jnp.int32, sp.shape, 1)
    sp = jnp.where(jnp.logical_and(ki > qi, not_first), sp, NEG_INF)
    sc = jnp.where(ki <= qi, sc, NEG_INF)
    m = jnp.maximum(jnp.maximum(jnp.max(sp, axis=-1, keepdims=True), jnp.max(sc, axis=-1, keepdims=True)), sink)
    m = lax.stop_gradient(m)
    pp = jnp.exp(sp - m)
    pc = jnp.exp(sc - m)
    denom = jnp.sum(pp, axis=-1, keepdims=True) + jnp.sum(pc, axis=-1, keepdims=True) + jnp.exp(sink - m)
    o = lax.dot_general((pp / denom).astype(BF16), vp.astype(BF16), nn, preferred_element_type=F32)
    o = o + lax.dot_general((pc / denom).astype(BF16), vc.astype(BF16), nn, preferred_element_type=F32)
    return o


def _attn_fwd(qh, kh, vh, sinks, bias, name):
    s = qh.shape[1]
    nb = s // BLOCK
    G = GQA_GROUP
    R = G * BLOCK

    def body(q_ref, kp_ref, kc_ref, vp_ref, vc_ref, sink_ref, bias_ref, o_ref):
        not_first = pl.program_id(0) > 0
        for kv in range(N_KV_HEADS):
            hs = slice(kv * G, (kv + 1) * G)
            o = _attn_head(q_ref[hs].reshape(R, HEAD_DIM), kp_ref[kv], kc_ref[kv], vp_ref[kv], vc_ref[kv],
                           sink_ref[kv * R:(kv + 1) * R, 0:1],
                           bias_ref[hs, :, 0:BLOCK].reshape(R, BLOCK), bias_ref[hs, :, BLOCK:2 * BLOCK].reshape(R, BLOCK),
                           not_first)
            o_ref[hs] = o.reshape(G, BLOCK, HEAD_DIM).astype(o_ref.dtype)

    cur = lambda i: (0, i, 0)
    prev = lambda i: (0, jnp.maximum(i - 1, 0), 0)
    return pl.pallas_call(
        body, name=name, grid=(nb,),
        in_specs=[pl.BlockSpec((N_Q_HEADS, BLOCK, HEAD_DIM), cur),
                  pl.BlockSpec((N_KV_HEADS, BLOCK, HEAD_DIM), prev), pl.BlockSpec((N_KV_HEADS, BLOCK, HEAD_DIM), cur),
                  pl.BlockSpec((N_KV_HEADS, BLOCK, HEAD_DIM), prev), pl.BlockSpec((N_KV_HEADS, BLOCK, HEAD_DIM), cur),
                  pl.BlockSpec((N_Q_HEADS * BLOCK, 128), lambda i: (0, 0)),
                  pl.BlockSpec((N_Q_HEADS, BLOCK, 2 * BLOCK), lambda i: (0, 0, 0))],
        out_specs=pl.BlockSpec((N_Q_HEADS, BLOCK, HEAD_DIM), cur),
        out_shape=jax.ShapeDtypeStruct((N_Q_HEADS, s, HEAD_DIM), BF16),
        compiler_params=_cparams(("arbitrary",)),
    )(qh, kh, kh, vh, vh, sinks, bias)


def _attn_bwd(qh, kh, vh, doh, sinks, bias, name):
    s = qh.shape[1]
    nb = s // BLOCK
    G = GQA_GROUP
    R = G * BLOCK

    def body(q_ref, kp_ref, kc_ref, vp_ref, vc_ref, do_ref, sink_ref, bias_ref,
             dq_ref, dk_ref, dv_ref, dsink_ref, dbias_ref, ck, cv):
        i = pl.program_id(1)

        @pl.when(i == 0)
        def _():
            dsink_ref[...] = jnp.zeros_like(dsink_ref)
            dbias_ref[...] = jnp.zeros_like(dbias_ref)
            ck[...] = jnp.zeros_like(ck)
            cv[...] = jnp.zeros_like(cv)

        @pl.when(i < nb)
        def _():
            not_first = i > 0
            _, vjp = jax.vjp(lambda q, a, b, c, d, sk, e, f: _attn_head(q, a, b, c, d, sk, e, f, not_first),
                             q_ref[...].astype(F32).reshape(R, HEAD_DIM), kp_ref[...].astype(F32),
                             kc_ref[...].astype(F32), vp_ref[...].astype(F32), vc_ref[...].astype(F32),
                             sink_ref[:, 0:1], bias_ref[:, :, 0:BLOCK].reshape(R, BLOCK),
                             bias_ref[:, :, BLOCK:2 * BLOCK].reshape(R, BLOCK))
            dq, dkp, dkc, dvp, dvc, dsk, dbp, dbc = vjp(do_ref[...].reshape(R, HEAD_DIM).astype(F32))
            dq_ref[...] = dq.reshape(G, BLOCK, HEAD_DIM).astype(dq_ref.dtype)
            dsink_ref[...] += jnp.broadcast_to(dsk, (R, 128))
            dbias_ref[:, :, 0:BLOCK] += dbp.reshape(G, BLOCK, BLOCK)
            dbias_ref[:, :, BLOCK:2 * BLOCK] += dbc.reshape(G, BLOCK, BLOCK)
            dk_ref[...] = (ck[...] + dkp).astype(dk_ref.dtype)
            dv_ref[...] = (cv[...] + dvp).astype(dv_ref.dtype)
            ck[...] = dkc
            cv[...] = dvc

        @pl.when(i == nb)
        def _():
            dk_ref[...] = ck[...].astype(dk_ref.dtype)
            dv_ref[...] = cv[...].astype(dv_ref.dtype)

    qcur = lambda kv, i: (kv, jnp.minimum(i, nb - 1), 0)
    kcur = lambda kv, i: (kv, jnp.minimum(i, nb - 1), 0)
    kprev = lambda kv, i: (kv, jnp.clip(i - 1, 0, nb - 1), 0)
    qspec = pl.BlockSpec((G, BLOCK, HEAD_DIM), qcur)
    kc_spec = pl.BlockSpec((None, BLOCK, HEAD_DIM), kcur)
    kp_spec = pl.BlockSpec((None, BLOCK, HEAD_DIM), kprev)
    return pl.pallas_call(
        body, name=name, grid=(N_KV_HEADS, nb + 1),
        in_specs=[qspec, kp_spec, kc_spec, kp_spec, kc_spec, qspec,
                  pl.BlockSpec((R, 128), lambda kv, i: (kv, 0)),
                  pl.BlockSpec((G, BLOCK, 2 * BLOCK), lambda kv, i: (kv, 0, 0))],
        out_specs=[qspec, kp_spec, kp_spec,
                   pl.BlockSpec((R, 128), lambda kv, i: (kv, 0)),
                   pl.BlockSpec((G, BLOCK, 2 * BLOCK), lambda kv, i: (kv, 0, 0))],
        out_shape=[jax.ShapeDtypeStruct((N_Q_HEADS, s, HEAD_DIM), BF16),
                   jax.ShapeDtypeStruct((N_KV_HEADS, s, HEAD_DIM), BF16),
                   jax.ShapeDtypeStruct((N_KV_HEADS, s, HEAD_DIM), BF16),
                   jax.ShapeDtypeStruct((N_Q_HEADS * BLOCK, 128), F32),
                   jax.ShapeDtypeStruct((N_Q_HEADS, BLOCK, 2 * BLOCK), F32)],
        scratch_shapes=[pltpu.VMEM((BLOCK, HEAD_DIM), F32), pltpu.VMEM((BLOCK, HEAD_DIM), F32)],
        compiler_params=_cparams(("arbitrary", "arbitrary")),
    )(qh, kh, kh, vh, vh, doh, sinks, bias)


def _cmul(ar, ai, br, bi):
    return ar * br - ai * bi, ar * bi + ai * br


def _scan_passes(a_ref, b_ref, x_ref, xp_ref, da_ref, *, s, tc, reverse):
    nc = SCAN_CHUNKS
    steps = s // nc
    with_da = xp_ref is not None
    unroll = 8 if steps % 8 == 0 else 1

    def shift(v, d):
        row = lax.broadcasted_iota(jnp.int32, v.shape, 0)
        if reverse:
            return jnp.where(row < nc - d, pltpu.roll(v, nc - d, 0), 0.0)
        return jnp.where(row >= d, pltpu.roll(v, d, 0), 0.0)

    def run():
        ar = jnp.broadcast_to(a_ref[0], (nc, tc))
        ai = jnp.broadcast_to(a_ref[1], (nc, tc))

        def row_of(step):
            j = (steps - 1 - step) if reverse else step
            return pl.multiple_of(j * nc, nc)

        def p1(step, st):
            sr, si = st
            r0 = row_of(step)
            mr, mi = _cmul(ar, ai, sr, si)
            sr = mr + b_ref[0, pl.ds(r0, nc), :]
            si = mi + b_ref[1, pl.ds(r0, nc), :]
            x_ref[0, pl.ds(r0, nc), :] = sr
            x_ref[1, pl.ds(r0, nc), :] = si
            return sr, si
        zero = jnp.zeros((nc, tc), F32)
        er, ei = lax.fori_loop(0, steps, p1, (zero, zero), unroll=unroll)

        pr, pi_ = jnp.ones((nc, tc), F32), zero
        br, bi, left = ar, ai, steps
        while left:
            if left & 1:
                pr, pi_ = _cmul(pr, pi_, br, bi)
            br, bi = _cmul(br, bi, br, bi)
            left >>= 1
        cr, ci = shift(er, 1), shift(ei, 1)
        d = 1
        while d < nc:
            mr, mi = _cmul(pr, pi_, shift(cr, d), shift(ci, d))
            cr, ci = cr + mr, ci + mi
            pr, pi_ = _cmul(pr, pi_, pr, pi_)
            d *= 2

        def p2(step, st):
            qr, qi, dar, dai = st
            r0 = row_of(step)
            qr, qi = _cmul(ar, ai, qr, qi)
            fr, fi = _cmul(qr, qi, cr, ci)
            xr = x_ref[0, pl.ds(r0, nc), :] + fr
            xi = x_ref[1, pl.ds(r0, nc), :] + fi
            x_ref[0, pl.ds(r0, nc), :] = xr
            x_ref[1, pl.ds(r0, nc), :] = xi
            if with_da:
                jm = jnp.where(step == steps - 1, steps - 1, steps - 2 - step)
                rp = pl.multiple_of(jm * nc, nc)
                vr, vi = xp_ref[0, pl.ds(rp, nc), :], xp_ref[1, pl.ds(rp, nc), :]
                row = lax.broadcasted_iota(jnp.int32, (nc, tc), 0)
                first = step == steps - 1
                sel = jnp.logical_and(first, row == 0)
                vr = jnp.where(sel, 0.0, jnp.where(first, pltpu.roll(vr, 1, 0), vr))
                vi = jnp.where(sel, 0.0, jnp.where(first, pltpu.roll(vi, 1, 0), vi))
                dar = dar + xr * vr + xi * vi
                dai = dai + xi * vr - xr * vi
            return qr, qi, dar, dai
        _, _, dar, dai = lax.fori_loop(0, steps, p2, (jnp.ones((nc, tc), F32), zero, zero, zero), unroll=unroll)
        if with_da:
            da_ref[0] = jnp.sum(dar, axis=0, keepdims=True)
            da_ref[1] = jnp.sum(dai, axis=0, keepdims=True)

    run()


def _ssm_fwd(u, bd, cd, a, d_row, *, name, sb, sbn):
    s, w = u.shape
    nst = a.shape[2]
    nblk = w // sb
    rows = min(512, s)
    nn = (((1,), (0,)), ((), ()))

    def body(u_ref, bre_ref, bim_ref, cre_ref, cim_ref, a_ref, d_ref, y_ref, x_ref):

        def fill(r, carry):
            r0 = pl.multiple_of(r * rows, rows)
            ub = u_ref[pl.ds(r0, rows), :].astype(BF16)
            x_ref[0, pl.ds(r0, rows), :] = lax.dot_general(ub, bre_ref[...].astype(BF16), nn, preferred_element_type=F32)
            x_ref[1, pl.ds(r0, rows), :] = lax.dot_general(ub, bim_ref[...].astype(BF16), nn, preferred_element_type=F32)
            return carry
        lax.fori_loop(0, s // rows, fill, 0)
        _scan_passes(a_ref, x_ref, x_ref, None, None, s=s, tc=sbn, reverse=False)

        def project(r, carry):
            r0 = pl.multiple_of(r * rows, rows)
            y = lax.dot_general(x_ref[0, pl.ds(r0, rows), :].astype(BF16), cre_ref[...].astype(BF16), nn, preferred_element_type=F32)
            y = y + lax.dot_general(x_ref[1, pl.ds(r0, rows), :].astype(BF16), cim_ref[...].astype(BF16), nn, preferred_element_type=F32)
            y_ref[pl.ds(r0, rows), :] = y + d_ref[...] * u_ref[pl.ds(r0, rows), :]
            return carry
        lax.fori_loop(0, s // rows, project, 0)

    return pl.pallas_call(
        body, name=name, grid=(nblk,),
        in_specs=[pl.BlockSpec((s, sb), lambda j: (0, j)),
                  pl.BlockSpec((sb, sbn), lambda j: (j, j)), pl.BlockSpec((sb, sbn), lambda j: (j, nblk + j)),
                  pl.BlockSpec((sbn, sb), lambda j: (j, j)), pl.BlockSpec((sbn, sb), lambda j: (nblk + j, j)),
                  pl.BlockSpec((2, 1, sbn), lambda j: (0, 0, j)), pl.BlockSpec((1, sb), lambda j: (0, j))],
        out_specs=[pl.BlockSpec((s, sb), lambda j: (0, j)), pl.BlockSpec((2, s, sbn), lambda j: (0, 0, j))],
        out_shape=[jax.ShapeDtypeStruct((s, w), F32), jax.ShapeDtypeStruct((2, s, nst), F32)],
        compiler_params=pltpu.CompilerParams(dimension_semantics=("arbitrary",), vmem_limit_bytes=SSM_VMEM_LIMIT_BYTES),
    )(u, bd, bd, cd, cd, a, d_row)


def _ssm_bwd(dy, u, xs, bd, cd, a, d_row, *, name, sb, sbn, deps=()):
    s, w = u.shape
    nst = a.shape[2]
    nblk = w // sb
    rows = min(512, s)
    nt = (((1,), (1,)), ((), ()))
    tn = (((0,), (0,)), ((), ()))

    def body(dy_ref, u_ref, xs_hbm, bre_ref, bim_ref, cre_ref, cim_ref, a_ref, d_ref, *rest):
        du_ref, gb_ref, gc_ref, da_ref, gd_ref, lam, xs_ref, sem = rest[len(deps):]
        j = pl.program_id(0)
        fetch = pltpu.make_async_copy(xs_hbm.at[:, :, pl.ds(pl.multiple_of(j * sbn, sbn), sbn)], xs_ref, sem)
        fetch.start()

        def fill(r, carry):
            r0 = pl.multiple_of(r * rows, rows)
            dyb = dy_ref[pl.ds(r0, rows), :].astype(BF16)
            lam[0, pl.ds(r0, rows), :] = lax.dot_general(dyb, cre_ref[...].astype(BF16), nt, preferred_element_type=F32)
            lam[1, pl.ds(r0, rows), :] = lax.dot_general(dyb, cim_ref[...].astype(BF16), nt, preferred_element_type=F32)
            return carry
        lax.fori_loop(0, s // rows, fill, 0)
        fetch.wait()
        _scan_passes(a_ref, lam, lam, xs_ref, da_ref, s=s, tc=sbn, reverse=True)
        gb_ref[...] = jnp.zeros_like(gb_ref)
        gc_ref[...] = jnp.zeros_like(gc_ref)
        gd_ref[...] = jnp.zeros_like(gd_ref)

        def project(r, carry):
            r0 = pl.multiple_of(r * rows, rows)
            dyv, uv = dy_ref[pl.ds(r0, rows), :], u_ref[pl.ds(r0, rows), :]
            dyb, ub = dyv.astype(BF16), uv.astype(BF16)
            lr, li = lam[0, pl.ds(r0, rows), :].astype(BF16), lam[1, pl.ds(r0, rows), :].astype(BF16)
            du = lax.dot_general(lr, bre_ref[...].astype(BF16), nt, preferred_element_type=F32)
            du = du + lax.dot_general(li, bim_ref[...].astype(BF16), nt, preferred_element_type=F32)
            du_ref[pl.ds(r0, rows), :] = du + d_ref[...] * dyv
            gb_ref[:, 0:sbn] += lax.dot_general(ub, lr, tn, preferred_element_type=F32)
            gb_ref[:, sbn:2 * sbn] += lax.dot_general(ub, li, tn, preferred_element_type=F32)
            gc_ref[0] += lax.dot_general(xs_ref[0, pl.ds(r0, rows), :].astype(BF16), dyb, tn, preferred_element_type=F32)
            gc_ref[1] += lax.dot_general(xs_ref[1, pl.ds(r0, rows), :].astype(BF16), dyb, tn, preferred_element_type=F32)
            gd_ref[...] += jnp.sum(dyv * uv, axis=0, keepdims=True)
            return carry
        lax.fori_loop(0, s // rows, project, 0)

    col = lambda j: (0, j)
    return pl.pallas_call(
        body, name=name, grid=(nblk,),
        in_specs=[pl.BlockSpec((s, sb), col), pl.BlockSpec((s, sb), col), pl.BlockSpec(memory_space=pl.ANY),
                  pl.BlockSpec((sb, sbn), lambda j: (j, j)), pl.BlockSpec((sb, sbn), lambda j: (j, nblk + j)),
                  pl.BlockSpec((sbn, sb), lambda j: (j, j)), pl.BlockSpec((sbn, sb), lambda j: (nblk + j, j)),
                  pl.BlockSpec((2, 1, sbn), lambda j: (0, 0, j)), pl.BlockSpec((1, sb), col)]
        + [pl.BlockSpec(memory_space=pl.ANY)] * len(deps),
        out_specs=[pl.BlockSpec((s, sb), col), pl.BlockSpec((sb, 2 * sbn), lambda j: (j, 0)),
                   pl.BlockSpec((2, sbn, sb), lambda j: (0, j, 0)), pl.BlockSpec((2, 1, sbn), lambda j: (0, 0, j)),
                   pl.BlockSpec((1, sb), col)],
        out_shape=[jax.ShapeDtypeStruct((s, w), F32), jax.ShapeDtypeStruct((w, 2 * sbn), F32),
                   jax.ShapeDtypeStruct((2, nst, sb), F32), jax.ShapeDtypeStruct((2, 1, nst), F32),
                   jax.ShapeDtypeStruct((1, w), F32)],
        scratch_shapes=[pltpu.VMEM((2, s, sbn), F32), pltpu.VMEM((2, s, sbn), F32), pltpu.SemaphoreType.DMA],
        compiler_params=pltpu.CompilerParams(dimension_semantics=("arbitrary",), vmem_limit_bytes=SSM_VMEM_LIMIT_BYTES),
    )(dy, u, xs, bd, bd, cd, cd, a, d_row, *deps)


def _adamw_math(w, g, m, v):
    nm = ADAM_B1 * m + (1.0 - ADAM_B1) * g
    nv = ADAM_B2 * v + (1.0 - ADAM_B2) * (g * g)
    m_hat = nm / (1.0 - ADAM_B1 ** ADAM_STEP)
    v_hat = nv / (1.0 - ADAM_B2 ** ADAM_STEP)
    return -ADAM_LR * (m_hat / (jnp.sqrt(v_hat) + ADAM_EPS) + ADAM_WD * w), nm, nv


def _adamw_many(ws, gs, ms, vs, name, deps=()):
    n, nd = len(ws), len(deps)

    def body(*refs):
        outs = refs[4 * n + nd:]
        for i in range(n):
            d, nm, nv = _adamw_math(refs[i][...], refs[n + i][...], refs[2 * n + i][...], refs[3 * n + i][...])
            outs[i][...], outs[n + i][...], outs[2 * n + i][...] = d, nm, nv

    whole = pl.BlockSpec(memory_space=pltpu.VMEM)
    res = pl.pallas_call(
        body, name=name, in_specs=[whole] * (4 * n) + [pl.BlockSpec(memory_space=pl.ANY)] * nd,
        out_specs=[whole] * (3 * n), out_shape=[jax.ShapeDtypeStruct(w.shape, F32) for w in ws] * 3,
        compiler_params=pltpu.CompilerParams(vmem_limit_bytes=VMEM_LIMIT_BYTES),
    )(*ws, *gs, *ms, *vs, *deps)
    return res[:n], res[n:2 * n], res[2 * n:]


def _adamw(w, g, m, v, name, deps=(), echo=False):
    nd = len(deps)
    r, c = w.shape
    tr = r
    for cand in (512, 256, 128, 64, 32, 16, 8):
        if r % cand == 0 and cand * c * 4 <= 2 * 1024 * 1024:
            tr = cand
            break

    def body(w_ref, g_ref, m_ref, v_ref, *rest):
        d_ref, nm_ref, nv_ref = rest[nd:nd + 3]
        gv = g_ref[...]
        d_ref[...], nm_ref[...], nv_ref[...] = _adamw_math(w_ref[...], gv, m_ref[...], v_ref[...])
        if echo:
            rest[nd + 3][...] = gv

    no = 4 if echo else 3
    spec = pl.BlockSpec((tr, c), lambda i: (i, 0))
    sds = jax.ShapeDtypeStruct((r, c), F32)
    return pl.pallas_call(body, name=name, grid=(r // tr,),
                          in_specs=[spec] * 4 + [pl.BlockSpec(memory_space=pl.ANY)] * nd, out_specs=[spec] * no,
                          out_shape=[sds] * no, compiler_params=_cparams(("parallel",)))(w, g, m, v, *deps)


def _sum_lead(x, name, out_dtype=F32):
    n, r, c = x.shape
    tr = r
    for cand in (512, 256, 128, 64, 32, 16, 8):
        if r % cand == 0 and n * cand * c * 4 <= 4 * 1024 * 1024:
            tr = cand
            break

    def body(x_ref, o_ref):
        acc = x_ref[0].astype(F32)
        for k in range(1, n):
            acc = acc + x_ref[k].astype(F32)
        o_ref[...] = acc.astype(o_ref.dtype)

    return pl.pallas_call(body, name=name, grid=(r // tr,),
                          in_specs=[pl.BlockSpec((n, tr, c), lambda i: (0, i, 0))],
                          out_specs=pl.BlockSpec((tr, c), lambda i: (i, 0)),
                          out_shape=jax.ShapeDtypeStruct((r, c), out_dtype),
                          compiler_params=_cparams(("parallel",)))(x)


def _row_tile(rows, row_bytes, budget, least=8):
    for cand in (1024, 512, 256, 128, 64, 32, 16, 8):
        if cand >= least and rows % cand == 0 and cand * row_bytes <= budget:
            return cand
    return rows


def _cast_into_slot(w, slot, name):
    r, c = w.shape
    tr = _row_tile(r, c * 4, 4 * 1024 * 1024, least=16)

    def body(slot_ref, w_ref, o_ref):
        o_ref[...] = w_ref[...].astype(o_ref.dtype)

    gs = pltpu.PrefetchScalarGridSpec(
        num_scalar_prefetch=1, grid=(r // tr,),
        in_specs=[pl.BlockSpec((tr, c), lambda i, s: (i, 0))],
        out_specs=pl.BlockSpec((None, tr, c), lambda i, s: (s[0], i, 0)))
    return pl.pallas_call(body, name=name, grid_spec=gs, out_shape=jax.ShapeDtypeStruct((N_CHIPS, r, c), BF16),
                          compiler_params=_cparams(("parallel",)))(slot, w)


def _sum_own(p, t, sel, name):
    _, h, c = p.shape
    tr = _row_tile(h, c * 4, 2 * 1024 * 1024, least=16)
    nblk = h // tr

    def body(sel_ref, p_ref, t_ref, o_ref):
        acc = p_ref[...].astype(F32)
        for k in range(3):
            acc = acc + t_ref[k].astype(F32)
        o_ref[...] = acc

    gs = pltpu.PrefetchScalarGridSpec(
        num_scalar_prefetch=1, grid=(nblk,),
        in_specs=[pl.BlockSpec((None, tr, c), lambda i, s: (s[0], i, 0)),
                  pl.BlockSpec((3, tr, c), lambda i, s: (0, i, 0))],
        out_specs=pl.BlockSpec((tr, c), lambda i, s: (s[1] * nblk + i, 0)))
    return pl.pallas_call(body, name=name, grid_spec=gs, out_shape=jax.ShapeDtypeStruct((2 * h, c), F32),
                          compiler_params=_cparams(("parallel",)))(sel, p, t)


def _add_half(g, t, half, name):
    n, r, c = g.shape
    h = r // 2
    tr = h
    for cand in (512, 256, 128, 64, 32, 16):
        if h % cand == 0 and cand * c * 2 <= 2 * 1024 * 1024:
            tr = cand
            break
    nblk = h // tr

    def body(half_ref, g_ref, t_ref, o_ref):
        o_ref[...] = (g_ref[...].astype(F32) + t_ref[...].astype(F32)).astype(o_ref.dtype)

    gs = pltpu.PrefetchScalarGridSpec(
        num_scalar_prefetch=1, grid=(n, nblk),
        in_specs=[pl.BlockSpec((None, tr, c), lambda j, i, hr: (j, hr[0] * nblk + i, 0)),
                  pl.BlockSpec((None, tr, c), lambda j, i, hr: (j, i, 0))],
        out_specs=pl.BlockSpec((None, tr, c), lambda j, i, hr: (j, i, 0)))
    return pl.pallas_call(body, name=name, grid_spec=gs, out_shape=jax.ShapeDtypeStruct((n, h, c), BF16),
                          compiler_params=_cparams(("parallel", "parallel")))(half, g, t)


def _position():
    x, y, c = lax.axis_index("x"), lax.axis_index("y"), lax.axis_index("c")
    return x, y, c


def _allgather8(xs, name):
    m_per, n = xs.shape

    def body(x_ref, out_ref, send_sems, recv_sems, local_sem):
        x, y, c = _position()
        me, sibling = (x, y, c), (x, y, 1 - c)
        chips = [(1 - x, y), (x, 1 - y), (1 - x, 1 - y)]

        def rows(px, py, pc):
            return out_ref.at[pl.ds((4 * px + 2 * py + pc) * m_per, m_per), :]

        def copy(k, block, to, src=None):
            return pltpu.make_async_remote_copy(
                src_ref=rows(*block) if src is None else src, dst_ref=rows(*block),
                send_sem=send_sems.at[k], recv_sem=recv_sems.at[k], device_id=to, device_id_type=MESH)

        mine = pltpu.make_async_copy(x_ref, rows(*me), local_sem)
        mine.start()
        first = [copy(0, me, sibling, src=x_ref)]
        first += [copy(1 + j, me, (*chip, c), src=x_ref) for j, chip in enumerate(chips)]
        for cp in first:
            cp.start()
        passed = [copy(4 + j, (*chip, c), sibling) for j, chip in enumerate(chips)]
        for j, chip in enumerate(chips):
            copy(1 + j, (*chip, c), me).wait_recv()
            passed[j].start()
        copy(0, sibling, me).wait_recv()
        for j, chip in enumerate(chips):
            copy(4 + j, (*chip, 1 - c), me).wait_recv()
        for cp in first + passed:
            cp.wait_send()
        mine.wait()

    return pl.pallas_call(
        body, name=name, out_shape=jax.ShapeDtypeStruct((N_DEV * m_per, n), xs.dtype),
        in_specs=[pl.BlockSpec(memory_space=pltpu.VMEM)], out_specs=pl.BlockSpec(memory_space=pltpu.VMEM),
        scratch_shapes=[pltpu.SemaphoreType.DMA((7,)), pltpu.SemaphoreType.DMA((7,)), pltpu.SemaphoreType.DMA],
        compiler_params=pltpu.CompilerParams(vmem_limit_bytes=VMEM_LIMIT_BYTES),
    )(xs)


_HBM = pl.BlockSpec(memory_space=pltpu.HBM)


_SEM = pl.BlockSpec(memory_space=pltpu.SEMAPHORE)
_ANY = pl.BlockSpec(memory_space=pl.ANY)
_EFFECT = pltpu.SideEffectType.DATAFLOW_SIDE_EFFECTING


def _in_hbm(a):
    return pltpu.with_memory_space_constraint(a, pltpu.HBM)


def _several(after):
    return list(after) if isinstance(after, (list, tuple)) else [after]


def _gather_start(ws, groups, after, name):
    n = len(ws)
    after = _several(after)

    def body(*refs):
        in_refs = refs[:n]
        sems, token = refs[2 * n + len(after):-1], refs[-1]
        x, y, c = _position()
        mychip = 2 * x + y
        chips = [(1 - x, y), (x, 1 - y), (1 - x, 1 - y)]
        for g, members in enumerate(groups):
            for k, i in enumerate(members):
                h = ws[i].shape[1] // 2
                mine = in_refs[i].at[mychip, pl.ds(c * h, h), :]
                for j, (px, py) in enumerate(chips):
                    pltpu.make_async_remote_copy(
                        src_ref=mine, dst_ref=mine, send_sem=sems[2 * g].at[3 * k + j],
                        recv_sem=sems[2 * g + 1].at[3 * k + j], device_id=(px, py, c), device_id_type=MESH).start()
        token[...] = jnp.zeros_like(token)

    sem_shapes = [pltpu.SemaphoreType.DMA((3 * len(m),)) for m in groups for _ in range(2)]
    res = pl.pallas_call(
        body, name=name,
        out_shape=[pltpu.HBM(w.shape, w.dtype) for w in ws] + sem_shapes + [jax.ShapeDtypeStruct((8, 128), F32)],
        in_specs=[_HBM] * n + [_ANY] * len(after),
        out_specs=[_HBM] * n + [_SEM] * len(sem_shapes) + [pl.BlockSpec(memory_space=pltpu.VMEM)],
        input_output_aliases={i: i for i in range(n)},
        compiler_params=pltpu.CompilerParams(has_side_effects=_EFFECT),
    )(*[_in_hbm(w) for w in ws], *after)
    bufs, sems, token = res[:n], res[n:-1], res[-1]
    return list(bufs), [(sems[2 * g], sems[2 * g + 1]) for g in range(len(groups))], token


def _gather_wait(bufs, send_sems, recv_sems, after, name):
    m = len(bufs)

    def body(*refs):
        in_refs = refs[:m]
        send, recv = refs[m], refs[m + 1]
        x, y, c = _position()
        mychip = 2 * x + y
        chips = [(1 - x, y), (x, 1 - y), (1 - x, 1 - y)]
        for k in range(m):
            h = bufs[k].shape[1] // 2
            mine = in_refs[k].at[mychip, pl.ds(c * h, h), :]
            for j, (px, py) in enumerate(chips):
                cp = pltpu.make_async_remote_copy(
                    src_ref=mine, dst_ref=in_refs[k].at[2 * px + py, pl.ds(c * h, h), :],
                    send_sem=send.at[3 * k + j], recv_sem=recv.at[3 * k + j],
                    device_id=(px, py, c), device_id_type=MESH)
                cp.wait_send()
                cp.wait_recv()

    res = pl.pallas_call(
        body, name=name, out_shape=[pltpu.HBM(b.shape, b.dtype) for b in bufs],
        in_specs=[_HBM] * m + [_SEM, _SEM] + [_ANY] * len(_several(after)), out_specs=[_HBM] * m,
        input_output_aliases={k: k for k in range(m)},
        compiler_params=pltpu.CompilerParams(has_side_effects=_EFFECT),
    )(*bufs, send_sems, recv_sems, *_several(after))
    return list(res)


def _forward_halves(ws, name):
    n = len(ws)

    def body(*refs):
        out_refs = refs[n:2 * n]
        send_sems, recv_sems = refs[2 * n:]
        x, y, c = _position()
        me, sibling = (x, y, c), (x, y, 1 - c)
        chips = [(1 - x, y), (x, 1 - y), (1 - x, 1 - y)]
        cps = []
        for i in range(n):
            h = ws[i].shape[1] // 2
            for j, (px, py) in enumerate(chips):
                got = out_refs[i].at[2 * px + py, pl.ds(c * h, h), :]
                cp = pltpu.make_async_remote_copy(
                    src_ref=got, dst_ref=got, send_sem=send_sems.at[3 * i + j], recv_sem=recv_sems.at[3 * i + j],
                    device_id=sibling, device_id_type=MESH)
                cp.start()
                cps.append(cp)
        for i in range(n):
            h = ws[i].shape[1] // 2
            for j, (px, py) in enumerate(chips):
                other = out_refs[i].at[2 * px + py, pl.ds((1 - c) * h, h), :]
                pltpu.make_async_remote_copy(
                    src_ref=other, dst_ref=other, send_sem=send_sems.at[3 * i + j], recv_sem=recv_sems.at[3 * i + j],
                    device_id=me, device_id_type=MESH).wait_recv()
        for cp in cps:
            cp.wait_send()

    return pl.pallas_call(
        body, name=name,
        out_shape=[jax.ShapeDtypeStruct(w.shape, w.dtype) for w in ws],
        in_specs=[_HBM] * n, out_specs=[_HBM] * n, input_output_aliases={i: i for i in range(n)},
        scratch_shapes=[pltpu.SemaphoreType.DMA((3 * n,)), pltpu.SemaphoreType.DMA((3 * n,))],
    )(*ws)


def _copies_start(arrays, copies, nsem, after, name):
    n = len(arrays)
    after = _several(after)
    first = 2 * n + len(after)

    def body(*refs):
        for cp in copies(refs[:n], refs[first], refs[first + 1]):
            cp.start()
        refs[first + 2][...] = jnp.zeros_like(refs[first + 2])

    res = pl.pallas_call(
        body, name=name,
        out_shape=[pltpu.HBM(a.shape, a.dtype) for a in arrays]
        + [pltpu.SemaphoreType.DMA((nsem,)), pltpu.SemaphoreType.DMA((nsem,)), jax.ShapeDtypeStruct((8, 128), F32)],
        in_specs=[_HBM] * n + [_ANY] * len(after),
        out_specs=[_HBM] * n + [_SEM, _SEM, pl.BlockSpec(memory_space=pltpu.VMEM)],
        input_output_aliases={i: i for i in range(n)},
        compiler_params=pltpu.CompilerParams(has_side_effects=_EFFECT),
    )(*[_in_hbm(a) for a in arrays], *after)
    return list(res[:n]), res[n], res[n + 1], res[n + 2]


def _copies_wait(arrays, copies, send_sems, recv_sems, after, name):
    n = len(arrays)

    def body(*refs):
        for cp in copies(refs[:n], refs[n], refs[n + 1]):
            cp.wait_send()
            cp.wait_recv()

    res = pl.pallas_call(
        body, name=name, out_shape=[pltpu.HBM(a.shape, a.dtype) for a in arrays],
        in_specs=[_HBM] * n + [_SEM, _SEM] + [_ANY] * len(_several(after)), out_specs=[_HBM] * n,
        input_output_aliases={i: i for i in range(n)},
        compiler_params=pltpu.CompilerParams(has_side_effects=_EFFECT),
    )(*arrays, send_sems, recv_sems, *_several(after))
    return list(res)


def _scatter_copies(refs, send, recv):
    n = len(refs) // 2
    x, y, c = _position()
    chips = [(1 - x, y), (x, 1 - y), (1 - x, 1 - y)]
    return [pltpu.make_async_remote_copy(
        src_ref=refs[i].at[2 * px + py], dst_ref=refs[n + i].at[j],
        send_sem=send.at[3 * i + j], recv_sem=recv.at[3 * i + j], device_id=(px, py, c), device_id_type=MESH)
        for i in range(n) for j, (px, py) in enumerate(chips)]


def _swap_copies(refs, send, recv):
    n = len(refs) // 2
    x, y, c = _position()
    cps = []
    for i in range(n):
        h = refs[i].shape[1] // 2
        cps.append(pltpu.make_async_remote_copy(
            src_ref=refs[i].at[:, pl.ds((1 - c) * h, h), :], dst_ref=refs[n + i],
            send_sem=send.at[i], recv_sem=recv.at[i], device_id=(x, y, 1 - c), device_id_type=MESH))
    return cps


def _join_copies(refs, send, recv):
    x, y, c = _position()
    cps = []
    for i, r in enumerate(refs):
        h = r.shape[0] // 2
        mine = r.at[pl.ds(c * h, h), :]
        cps.append(pltpu.make_async_remote_copy(
            src_ref=mine, dst_ref=mine, send_sem=send.at[i], recv_sem=recv.at[i],
            device_id=(x, y, 1 - c), device_id_type=MESH))
    return cps


def _forward_copies(refs, send, recv):
    x, y, c = _position()
    chips = [(1 - x, y), (x, 1 - y), (1 - x, 1 - y)]
    cps = []
    for i, r in enumerate(refs):
        h = r.shape[1] // 2
        for j, (px, py) in enumerate(chips):
            got = r.at[2 * px + py, pl.ds(c * h, h), :]
            cps.append(pltpu.make_async_remote_copy(
                src_ref=got, dst_ref=got, send_sem=send.at[3 * i + j], recv_sem=recv.at[3 * i + j],
                device_id=(x, y, 1 - c), device_id_type=MESH))
    return cps


def _t5_buckets_block():
    qi = np.arange(BLOCK)[:, None]
    ki = np.arange(2 * BLOCK)[None, :]
    n = np.maximum(qi + BLOCK - ki, 0)
    max_exact = NUM_BUCKETS // 2
    large = max_exact + (np.log(np.maximum(n, 1) / max_exact) / np.log(MAX_DISTANCE / max_exact)
                         * (NUM_BUCKETS - max_exact)).astype(np.int32)
    large = np.minimum(large, NUM_BUCKETS - 1)
    return np.where(n < max_exact, n, large).astype(np.int32)


def _discretise(lambda_re, lambda_im, log_step, b_re, b_im):
    lam_re = jnp.minimum(lambda_re, -1e-4)
    lam_im = lambda_im
    delta = jnp.exp(log_step)[:, None]
    mag = jnp.exp(lam_re * delta)
    ang = lam_im * delta
    abar_re, abar_im = mag * jnp.cos(ang), mag * jnp.sin(ang)
    num_re, num_im = abar_re - 1.0, abar_im
    den = lam_re * lam_re + lam_im * lam_im
    f_re = (num_re * lam_re + num_im * lam_im) / den
    f_im = (num_im * lam_re - num_re * lam_im) / den
    bbar_re = f_re[..., None] * b_re - f_im[..., None] * b_im
    bbar_im = f_re[..., None] * b_im + f_im[..., None] * b_re
    return abar_re, abar_im, bbar_re, bbar_im


def _interleave(v, nc):
    s, w = v.shape
    return v.reshape(nc, s // nc, w).transpose(1, 0, 2).reshape(s, w)


def _deinterleave(v, nc):
    s, w = v.shape
    return v.reshape(s // nc, nc, w).transpose(1, 0, 2).reshape(s, w)


_SMALL = ("norm1_g", "b_in", "attn_sinks", "rel_bias", "lambda_re", "lambda_im", "log_step", "ssm_b_re",
          "ssm_b_im", "ssm_c_re", "ssm_c_im", "ssm_d", "b_glu", "norm2_g", "final_g")


def _pack(parts):
    rows = []
    for p in parts:
        f = p.reshape(-1).astype(F32)
        pad = (-f.shape[0]) % 128
        rows.append(jnp.pad(f, (0, pad)).reshape(-1, 128))
    out = jnp.concatenate(rows, axis=0)
    pad = (-out.shape[0]) % 256
    return jnp.pad(out, ((0, pad), (0, 0)))


def _unpack(packed, shapes):
    res, r = [], 0
    for shp in shapes:
        size = int(np.prod(shp))
        nr = -(-size // 128)
        res.append(packed[r:r + nr].reshape(-1)[:size].reshape(shp))
        r += nr
    return res


def kernel(x, c, w_ada, b_ada, norm1_g, w_in, b_in, attn_sinks, rel_bias, lambda_re, lambda_im, log_step, ssm_b_re, ssm_b_im, ssm_c_re, ssm_c_im, ssm_d, w_glu, b_glu, w_attn_proj, w_ssm_proj, w_out, norm2_g, w_ff1, w_ff2, final_g, loss_target, m_w_ada, m_b_ada, m_norm1_g, m_w_in, m_b_in, m_attn_sinks, m_rel_bias, m_lambda_re, m_lambda_im, m_log_step, m_ssm_b_re, m_ssm_b_im, m_ssm_c_re, m_ssm_c_im, m_ssm_d, m_w_glu, m_b_glu, m_w_attn_proj, m_w_ssm_proj, m_w_out, m_norm2_g, m_w_ff1, m_w_ff2, m_final_g, v_w_ada, v_b_ada, v_norm1_g, v_w_in, v_b_in, v_attn_sinks, v_rel_bias, v_lambda_re, v_lambda_im, v_log_step, v_ssm_b_re, v_ssm_b_im, v_ssm_c_re, v_ssm_c_im, v_ssm_d, v_w_glu, v_b_glu, v_w_attn_proj, v_w_ssm_proj, v_w_out, v_norm2_g, v_w_ff1, v_w_ff2, v_final_g):
    given = dict(locals())
    S, D = x.shape[1], x.shape[2]
    SSM_W = w_glu.shape[2]
    G = SSM_W // SSM_GROUP_CH
    NST = G * SSM_STATE
    DFF = w_ff2.shape[1] * N_CHIPS
    INW = w_in.shape[2] * N_CHIPS
    o_q, o_k, o_v, o_u = 0, ATTN_WIDTH, ATTN_WIDTH + KV_WIDTH, ATTN_WIDTH + 2 * KV_WIDTH
    o_ga, o_gs = o_u + SSM_W, o_u + SSM_W + D
    mx, my, mc = _position()
    my_chip = 2 * mx + my
    my_b = 4 * mx + 2 * my + mc

    xv, tgt = x[0], loss_target[0]

    big = dict(w_in=w_in[0], w_glu=w_glu[0], w_attn_proj=w_attn_proj[0], w_ssm_proj=w_ssm_proj[0],
               w_out=w_out[0], w_ff1=w_ff1[0], w_ff2=w_ff2[0])
    big_names = list(big)
    colsharded = {"w_in", "w_attn_proj", "w_ssm_proj", "w_ff1"}
    chip_sel = my_chip.astype(jnp.int32).reshape(1)
    gather_groups = [["w_in"], ["w_attn_proj", "w_ssm_proj", "w_glu", "w_out"], ["w_ff1", "w_ff2"]]
    in_flight, gather_sems, gathered = {}, [], {}

    def finish_gather(g, after):
        bufs = [in_flight[k] for k in gather_groups[g]]
        bufs = _gather_wait(bufs, gather_sems[g][0], gather_sems[g][1], after, "gather_wait_%d" % g)
        gathered.update(zip(gather_groups[g], _forward_halves(bufs, "gather_forward_%d" % g)))

    def tied(v, token):
        return v + token[0:1, 0:1]

    def all_of(*arrays):
        return list(arrays)

    def wop(k):
        g = gathered[k]
        return _Op(g, N_CHIPS) if k in colsharded else _Op(g.reshape(g.shape[0] * g.shape[1], g.shape[2]))

    grads = {}
    nothing = jnp.zeros((8, 128), F32)
    half = mc.astype(jnp.int32).reshape(1)
    sel = jnp.stack([my_chip, mc]).astype(jnp.int32)

    def rs_swap(tag, named):
        keys, gl = list(named), []
        for k in keys:
            gk = named[k]
            if k not in colsharded:
                gk = gk.reshape(N_CHIPS, gk.shape[0] // N_CHIPS, gk.shape[1])
            gl.append(gk)
        lands = [lax.empty((g.shape[0], g.shape[1] // 2, g.shape[2]), g.dtype) for g in gl]
        arrays, ssem, rsem, token = _copies_start(gl + lands, _swap_copies, len(gl), nothing, "rs_swap_start_" + tag)
        return (keys, arrays, ssem, rsem), token

    def rs_scatter(tag, state, after):
        keys, arrays, ssem, rsem = state
        arrays = _copies_wait(arrays, _swap_copies, ssem, rsem, after, "rs_swap_wait_" + tag)
        n = len(keys)
        ps = [_add_half(g, t, half, "rs_add_" + k) for g, t, k in zip(arrays[:n], arrays[n:], keys)]
        lands = [lax.empty((3,) + p.shape[1:], p.dtype) for p in ps]
        arrays, ssem, rsem, token = _copies_start(ps + lands, _scatter_copies, 3 * n, nothing, "rs_start_" + tag)
        return (keys, arrays, ssem, rsem), token

    def rs_sum(tag, state, after):
        keys, arrays, ssem, rsem = state
        arrays = _copies_wait(arrays, _scatter_copies, ssem, rsem, after, "rs_wait_" + tag)
        n = len(keys)
        rs = [_sum_own(p, t, sel, "rs_sum_" + k) for p, t, k in zip(arrays[:n], arrays[n:], keys)]
        rs, ssem, rsem, token = _copies_start(rs, _join_copies, n, nothing, "rs_join_start_" + tag)
        return (keys, rs, ssem, rsem), token

    def rs_finish(tag, state, after):
        keys, rs, ssem, rsem = state
        for k, f in zip(keys, _copies_wait(rs, _join_copies, ssem, rsem, after, "rs_join_wait_" + tag)):
            grads[k] = f[None]

    c_all = _allgather8(jnp.pad(c, ((0, 7), (0, 0))), "gather_c").reshape(N_DEV, 8, D)[:, 0]
    c16 = jnp.pad(c_all, ((0, 8), (0, 0)))
    b_ada_mine = lax.dynamic_slice(b_ada.reshape(N_CHIPS, -1), (my_chip, 0), (1, w_ada.shape[2]))
    mod_sh = _mm(c16, w_ada[0], "NN", name="mod", M=16, N=w_ada.shape[2], K=D, a_fn=_silu,
                 epilogue=lambda acc, b: (acc + b,), extras=[(b_ada_mine, "row")])
    mod_all = _allgather8(mod_sh[:8], "gather_mod").reshape(N_DEV, 8, -1)
    mod_row = jnp.concatenate(
        [lax.dynamic_slice(mod_all, (2 * j, my_b, 0), (1, 1, mod_all.shape[2]))[0] for j in range(N_CHIPS)], axis=1)
    sh1, sc1, g1, sh2, sc2, g2 = [mod_row[:, i * D:(i + 1) * D] for i in range(6)]

    first = [_cast_into_slot(big["w_in"], chip_sel, "cast_w_in")]
    first, sems_first, token_first = _gather_start(first, [[0]], mod_all, "gather_start_in")
    rest_names = gather_groups[1] + gather_groups[2]
    rest = [_cast_into_slot(big[k], chip_sel, "cast_" + k) for k in rest_names]
    rest, sems_rest, token_rest = _gather_start(
        rest, [[rest_names.index(k) for k in grp] for grp in gather_groups[1:]], token_first, "gather_start_rest")
    in_flight.update(zip(["w_in"] + rest_names, first + rest))
    gather_sems.extend(sems_first + sems_rest)

    disc_in = (lambda_re[0], lambda_im[0], log_step[0], ssm_b_re[0], ssm_b_im[0])
    (abar_re, abar_im, bbar_re, bbar_im), disc_vjp = jax.vjp(_discretise, *disc_in)
    same_group = jnp.asarray(np.arange(SSM_W)[:, None] // SSM_GROUP_CH == np.arange(NST)[None, :] // SSM_STATE)

    def block_diag(t):
        return jnp.where(same_group, jnp.tile(t, (G, 1)), 0.0)

    bd = jnp.concatenate([block_diag(bb.transpose(2, 0, 1).reshape(SSM_GROUP_CH, NST)) for bb in (bbar_re, bbar_im)],
                         axis=1)
    cd = jnp.concatenate([block_diag(cc.transpose(1, 0, 2).reshape(SSM_GROUP_CH, NST)).T
                          for cc in (ssm_c_re[0], -ssm_c_im[0])], axis=0)
    a_fwd = jnp.stack([abar_re.reshape(1, NST), abar_im.reshape(1, NST)])
    a_bwd = jnp.stack([abar_re.reshape(1, NST), -abar_im.reshape(1, NST)])
    d_row = ssm_d

    buckets = _t5_buckets_block()
    onehot_t = (jnp.arange(128, dtype=jnp.int32)[:, None] == jnp.asarray(buckets.reshape(1, -1))).astype(BF16)
    rb_hi = rel_bias.astype(BF16)
    rb_lo = (rel_bias - rb_hi.astype(F32)).astype(BF16)
    rb_lo2 = (rel_bias - rb_hi.astype(F32) - rb_lo.astype(F32)).astype(BF16)
    rb3 = jnp.pad(jnp.concatenate([rb_hi.T, rb_lo.T, rb_lo2.T], axis=0), ((0, 0), (0, 128 - NUM_BUCKETS)))
    b3 = _mm(rb3, onehot_t, "NN", name="rel_bias_rows", M=3 * N_Q_HEADS, N=BLOCK * 2 * BLOCK, K=128, tj=4096)
    bias = (b3[:N_Q_HEADS] + b3[N_Q_HEADS:2 * N_Q_HEADS]) + b3[2 * N_Q_HEADS:]
    bias = bias.reshape(N_Q_HEADS, BLOCK, 2 * BLOCK)
    sinks_b = jnp.broadcast_to(attn_sinks[0][:, None, None], (N_Q_HEADS, BLOCK, 128)).reshape(N_Q_HEADS * BLOCK, 128)

    def two(fn):
        def both(*blocks):
            r = fn(*blocks)
            return r, r
        return both

    h1, h1_t = _rowwise(two(_norm_mod), [(xv, "tile", D), (tied(tied(norm1_g, token_first), token_rest), "row", D),
                                         (sh1, "row", D), (sc1, "row", D)],
                        [(D, BF16), (D, BF16, "T")], [], name="norm1", rows=S)
    finish_gather(0, all_of(h1, bd, cd, a_fwd, a_bwd, bias, sinks_b))
    proj = _mm(h1, wop("w_in"), "NN", name="proj", M=S, N=INW, K=D, out_dtypes=(BF16,),
               epilogue=lambda acc, b: (acc + b,), extras=[(b_in, "row")])

    def heads(v2d, nh):
        return v2d.reshape(S, nh, HEAD_DIM).transpose(1, 0, 2)

    def unheads(v3d):
        return v3d.transpose(1, 0, 2).reshape(S, -1)

    qh = heads(proj[:, o_q:o_k], N_Q_HEADS)
    kh = heads(proj[:, o_k:o_v], N_KV_HEADS)
    vh = heads(proj[:, o_v:o_u], N_KV_HEADS)
    attn = unheads(_attn_fwd(qh, kh, vh, sinks_b, bias, "attn_fwd"))
    finish_gather(1, attn)
    y_attn = _mm(attn, wop("w_attn_proj"), "NN", name="attn_proj", M=S, N=D, K=ATTN_WIDTH, out_dtypes=(BF16,))

    u = proj[:, o_u:o_ga]
    u_il = _interleave(u, SCAN_CHUNKS)
    SB = 128
    nsb, gpb = SSM_W // SB, SB // SSM_GROUP_CH
    SBN = gpb * SSM_STATE
    y_il, xs = _ssm_fwd(u_il, bd, cd, a_fwd, d_row, name="ssm_fwd", sb=SB, sbn=SBN)
    y = _deinterleave(y_il, SCAN_CHUNKS)
    z, t_glu = _mm(y, wop("w_glu"), "NN", name="glu", M=S, N=SSM_W, K=SSM_W, out_dtypes=(BF16, F32), a_fn=_gelu,
                   epilogue=lambda acc, b, yy: (_gelu(yy) * _sigmoid(acc + b), acc + b),
                   extras=[(b_glu, "row"), (y, "tile")])
    y_ssm = _mm(z, wop("w_ssm_proj"), "NN", name="ssm_proj", M=S, N=D, K=SSM_W, out_dtypes=(BF16,))

    ff_bufs = _gather_wait([in_flight[k] for k in gather_groups[2]], gather_sems[2][0], gather_sems[2][1], all_of(y_ssm),
                           "gather_wait_2")
    ff_bufs, ff_send, ff_recv, token = _copies_start(ff_bufs, _forward_copies, 3 * len(ff_bufs), nothing,
                                                    "gather_forward_2_start")
    merged, merged_t = _rowwise(two(_merge), [(_Op(proj, coff=o_ga), "tile", D), (_Op(proj, coff=o_gs), "tile", D),
                                              (y_attn, "tile", D), (y_ssm, "tile", D)],
                                [(D, BF16), (D, BF16, "T")], [], name="merge", rows=S)
    mo, x2 = _mm(merged, wop("w_out"), "NN", name="out_proj", M=S, N=D, K=D, out_dtypes=(BF16, F32),
                 epilogue=lambda acc, xx, gg: (acc, xx + gg * acc), extras=[(xv, "tile"), (g1, "row")], deps=[token])
    h2, h2_t = _rowwise(two(_norm_mod), [(x2, "tile", D), (norm2_g, "row", D), (sh2, "row", D), (sc2, "row", D)],
                        [(D, BF16), (D, BF16, "T")], [], name="norm2", rows=S)
    gathered.update(zip(gather_groups[2], _copies_wait(ff_bufs, _forward_copies, ff_send, ff_recv, h2,
                                                       "gather_forward_2_wait")))
    a_b, r_b = _mm(h2, wop("w_ff1"), "NN", name="ff1", M=S, N=DFF, K=D, out_dtypes=(BF16, BF16),
                   epilogue=lambda acc: (acc, jnp.square(jnp.maximum(acc, 0.0))))
    ff, x3 = _mm(r_b, wop("w_ff2"), "NN", name="ff2", M=S, N=D, K=DFF, out_dtypes=(BF16, F32),
                 epilogue=lambda acc, xx, gg: (acc, xx + gg * acc), extras=[(x2, "tile"), (g2, "row")],
                 tj=1024, tk=1024)

    def final_fn(x3b, gf, tb, ffb, g2b):
        def f(xx, gg):
            yv = xx * lax.rsqrt(jnp.mean(xx * xx, axis=-1, keepdims=True) + EPS) * gg
            err = jnp.square(yv - tb)
            return 0.5 * jnp.sum(jnp.mean(err, axis=-1, keepdims=True), axis=0, keepdims=True)
        lv, vjp = jax.vjp(f, x3b, gf)
        dx, dg = vjp(jnp.ones((1, 1), F32))
        return dx, dx * g2b, dg, jnp.broadcast_to(lv, (1, 128)), jnp.sum(dx * ffb, axis=0, keepdims=True)

    dx3, dff, g_final, loss_acc, d_g2 = _rowwise(
        final_fn, [(x3, "tile", D), (final_g.reshape(1, D), "row", D), (tgt, "tile", D), (ff, "tile", D), (g2, "row", D)],
        [(D, F32), (D, BF16)], [D, 128, D], name="final", rows=S)
    da = _mm(dff, wop("w_ff2"), "NT", name="ff2_dx", M=S, N=DFF, K=D, out_dtypes=(BF16,),
             epilogue=lambda acc, ab: (acc * (2.0 * jnp.maximum(ab.astype(F32), 0.0)),), extras=[(a_b, "tile")])
    g_w_ff2 = _mm(r_b, dff, "TN", name="ff2_dw", M=DFF, N=D, K=S, out_dtypes=(BF16,), tj=1024, tk=1024)
    g_w_ff1 = _mm(h2_t, da, "NN", name="ff1_dw", M=D, N=DFF, K=S, out_dtypes=(BF16,), out_nsh=N_CHIPS, tj=1024, tk=1024)
    rs_ff, token = rs_swap("ff", dict(w_ff2=g_w_ff2, w_ff1=g_w_ff1))
    dh2 = _mm(da, wop("w_ff1"), "NT", name="ff1_dx", M=S, N=D, K=DFF, tj=1024, tk=1024, deps=[token])
    rs_ff, token_ff = rs_scatter("ff", rs_ff, dh2)

    def norm2_bwd(x2b, dh2b, dx3b, mob, gn, shb, scb, g1b):
        _, vjp = jax.vjp(_norm_mod, x2b, gn, shb, scb)
        dx, dg, dsh, dsc = vjp(dh2b)
        dx2b = dx + dx3b
        return dx2b, dx2b * g1b, dg, dsh, dsc, jnp.sum(dx2b * mob, axis=0, keepdims=True)

    dx2, dmo, g_norm2, d_sh2, d_sc2, d_g1 = _rowwise(
        norm2_bwd, [(x2, "tile", D), (dh2, "tile", D), (dx3, "tile", D), (mo, "tile", D),
                    (tied(norm2_g, token_ff), "row", D), (sh2, "row", D), (sc2, "row", D), (g1, "row", D)],
        [(D, F32), (D, BF16)], [D, D, D, D], name="norm2_bwd", rows=S)
    dmerged = _mm(dmo, wop("w_out"), "NT", name="out_dx", M=S, N=D, K=D)
    g_w_out = _mm(merged_t, dmo, "NN", name="out_dw", M=D, N=D, K=S, out_dtypes=(BF16,), tj=1024, tk=1024)

    def merge_bwd(gab, gsb, yab, ysb, dmb):
        _, vjp = jax.vjp(_merge, gab, gsb, yab, ysb)
        return vjp(dmb)

    d_ga, d_gs, dy_attn, dy_ssm = _rowwise(
        merge_bwd, [(_Op(proj, coff=o_ga), "tile", D), (_Op(proj, coff=o_gs), "tile", D), (y_attn, "tile", D),
                    (y_ssm, "tile", D), (dmerged, "tile", D)],
        [(D, BF16), (D, BF16), (D, BF16), (D, BF16)], [], name="merge_bwd", rows=S)

    dattn = _mm(dy_attn, wop("w_attn_proj"), "NT", name="attn_proj_dx", M=S, N=ATTN_WIDTH, K=D, tj=1024, out_dtypes=(BF16,))
    g_w_attn_proj = _mm(attn, dy_attn, "TN", name="attn_proj_dw", M=ATTN_WIDTH, N=D, K=S, out_dtypes=(BF16,),
                        out_nsh=N_CHIPS, tk=1024)

    dz = _mm(dy_ssm, wop("w_ssm_proj"), "NT", name="ssm_proj_dx", M=S, N=SSM_W, K=D)
    g_w_ssm_proj = _mm(z, dy_ssm, "TN", name="ssm_proj_dw", M=SSM_W, N=D, K=S, out_dtypes=(BF16,),
                       out_nsh=N_CHIPS, tk=1024)

    def glu_bwd(dzb, yb, tb):
        z0 = _gelu(yb)
        sg = _sigmoid(tb)
        dt = dzb * z0 * sg * (1.0 - sg)
        return dt, dzb * sg, jnp.sum(dt, axis=0, keepdims=True)

    dt_b, dz0a, g_b_glu = _rowwise(glu_bwd, [(dz, "tile", SSM_W), (y, "tile", SSM_W), (t_glu, "tile", SSM_W)],
                                   [(SSM_W, BF16), (SSM_W, F32)], [SSM_W], name="glu_bwd", rows=S)

    def gelu_bwd(acc, dz0ab, yb):
        _, vjp = jax.vjp(_gelu, yb)
        return (vjp(acc + dz0ab)[0],)

    dy = _mm(dt_b, wop("w_glu"), "NT", name="glu_dx", M=S, N=SSM_W, K=SSM_W, epilogue=gelu_bwd,
             extras=[(dz0a, "tile"), (y, "tile")])
    g_w_glu = _mm(y, dt_b, "TN", name="glu_dw", M=SSM_W, N=SSM_W, K=S, out_dtypes=(BF16,), tk=1024, a_fn=_gelu)
    rs_mix, token = rs_swap("mix", dict(w_out=g_w_out, w_attn_proj=g_w_attn_proj, w_ssm_proj=g_w_ssm_proj,
                                        w_glu=g_w_glu))
    dy_il = _interleave(dy, SCAN_CHUNKS)
    du_il, g_bd, g_cd, d_abar, g_ssm_d = _ssm_bwd(dy_il, u_il, xs, bd, cd, a_bwd, d_row, name="ssm_bwd", sb=SB, sbn=SBN,
                                                  deps=[token])
    du = _deinterleave(du_il, SCAN_CHUNKS)
    rs_mix, token_mix = rs_scatter("mix", rs_mix, du_il)

    dqh, dkh, dvh, dsink_blk, dbias = _attn_bwd(qh, kh, vh, heads(dattn, N_Q_HEADS), tied(sinks_b, token_mix), bias,
                                                "attn_bwd")
    g_sinks = _sum_lead(dsink_blk.reshape(N_Q_HEADS, BLOCK, 128).transpose(1, 0, 2), "sinks_dw")[:, 0].reshape(1, N_Q_HEADS)
    g_rel = _mm(dbias.reshape(N_Q_HEADS, -1), onehot_t, "NT", name="rel_bias_dw", M=N_Q_HEADS, N=128,
                K=BLOCK * 2 * BLOCK, tk=4096)
    g_rel_bias = g_rel[:, :NUM_BUCKETS].T

    eye_b = jnp.eye(gpb, dtype=F32)
    g_cd6 = g_cd.reshape(2, nsb, gpb, SSM_STATE, gpb, SSM_GROUP_CH)
    g_c_re = jnp.einsum("bgnhp,gh->bgpn", g_cd6[0], eye_b).reshape(G, SSM_GROUP_CH, SSM_STATE)
    g_c_im = -jnp.einsum("bgnhp,gh->bgpn", g_cd6[1], eye_b).reshape(G, SSM_GROUP_CH, SSM_STATE)
    g_bd6 = g_bd.reshape(nsb, gpb, SSM_GROUP_CH, 2, gpb, SSM_STATE)
    g_bbar = jnp.einsum("bhprgn,hg->rbhnp", g_bd6, eye_b).reshape(2, G, SSM_STATE, SSM_GROUP_CH)
    g_bbar_re, g_bbar_im = g_bbar[0], g_bbar[1]
    g_lre, g_lim, g_lstep, g_bre, g_bim = disc_vjp(
        (d_abar[0].reshape(G, SSM_STATE), d_abar[1].reshape(G, SSM_STATE), g_bbar_re, g_bbar_im))

    dproj = jnp.concatenate([unheads(dqh).astype(BF16), unheads(dkh).astype(BF16), unheads(dvh).astype(BF16),
                             du.astype(BF16), d_ga, d_gs], axis=1)
    g_w_in = _mm(h1_t, dproj, "NN", name="proj_dw", M=D, N=INW, K=S, out_dtypes=(BF16,), out_nsh=N_CHIPS,
                 tj=INW // (2 * N_CHIPS), tk=1024)
    rs_in, token = rs_swap("in", dict(w_in=g_w_in))
    dh1 = _mm(dproj, wop("w_in"), "NT", name="proj_dx", M=S, N=D, K=INW, tj=1024, tk=INW // N_CHIPS, deps=[token])
    g_b_in = _rowwise(lambda d: (jnp.sum(d.astype(F32), axis=0, keepdims=True),), [(dproj, "tile", INW)], [], [INW],
                      name="proj_db", rows=S)[0]

    def norm1_bwd(xb, dhb, dresb, gn, shb, scb):
        _, vjp = jax.vjp(_norm_mod, xb, gn, shb, scb)
        dx, dg, dsh, dsc = vjp(dhb)
        return dx + dresb, dg, dsh, dsc

    grad_x, g_norm1, d_sh1, d_sc1 = _rowwise(
        norm1_bwd, [(xv, "tile", D), (dh1, "tile", D), (dx2, "tile", D), (norm1_g, "row", D),
                    (sh1, "row", D),
                    (sc1, "row", D)], [(D, F32)], [D, D, D], name="norm1_bwd", rows=S)

    dmod_row = jnp.concatenate([d_sh1, d_sc1, d_g1, d_sh2, d_sc2, d_g2], axis=1)
    dmod_all = _allgather8(jnp.pad(dmod_row, ((0, 7), (0, 0))), "gather_dmod").reshape(N_DEV, 8, -1)[:, 0]
    g_b_ada = _sum_lead(dmod_all.reshape(N_DEV, -1, 128), "b_ada_dw").reshape(1, -1)
    dmod_mine = lax.dynamic_slice(dmod_all.reshape(N_DEV, N_CHIPS, -1), (0, my_chip, 0), (N_DEV, 1, w_ada.shape[2]))[:, 0]
    g_w_ada = _mm(c16, jnp.pad(dmod_mine, ((0, 8), (0, 0))), "TN", name="ada_dw", M=D, N=w_ada.shape[2], K=16,
                  a_fn=_silu)

    small_g = dict(norm1_g=g_norm1, b_in=g_b_in, attn_sinks=g_sinks, rel_bias=g_rel_bias, lambda_re=g_lre[None],
                   lambda_im=g_lim[None], log_step=g_lstep[None], ssm_b_re=g_bre[None], ssm_b_im=g_bim[None],
                   ssm_c_re=g_c_re[None], ssm_c_im=g_c_im[None], ssm_d=g_ssm_d, b_glu=g_b_glu, norm2_g=g_norm2,
                   final_g=g_final.reshape(D))
    packed = _pack([loss_acc[:, :1]] + [small_g[k] for k in _SMALL])
    rows = packed.shape[0]
    summed = _sum_lead(_allgather8(packed, "gather_small").reshape(N_DEV, rows, 128), "small_sum")
    small_shapes = [(1,)] + [given[k].shape for k in _SMALL]
    parts = _unpack(summed, small_shapes)
    loss = parts[0].reshape(())
    grads.update(zip(_SMALL, parts[1:]))
    grads["b_ada"] = g_b_ada
    grads["w_ada"] = g_w_ada[None]

    deltas, new_m, new_v = {}, {}, {}

    def adamw_big(k, deps=()):
        echo = k in big_names
        res = _adamw(given[k][0], grads[k][0], given["m_" + k][0], given["v_" + k][0], "adamw_" + k, deps, echo)
        deltas[k], new_m[k], new_v[k] = res[0][None], res[1][None], res[2][None]
        if echo:
            grads[k] = res[3][None]
        return res[2]

    rs_in, token_in = rs_scatter("in", rs_in, all_of(summed, dmod_all))
    rs_ff, token = rs_sum("ff", rs_ff, all_of(summed, token_in))
    mark = adamw_big("w_ada", [token])
    rs_mix, token = rs_sum("mix", rs_mix, mark)
    small_all = list(_SMALL) + ["b_ada"]
    for k in small_all:
        grads[k] = grads[k].reshape(given[k].shape)

    def rows_of(a):
        return a.reshape(1, -1) if a.ndim == 1 else a

    d_, m_, v_ = _adamw_many(*[[rows_of(src[k]) for k in small_all] for src in (
        given, grads, {k: given["m_" + k] for k in small_all}, {k: given["v_" + k] for k in small_all})],
        "adamw_small", [token])
    for k, dd, mm, vv in zip(small_all, d_, m_, v_):
        deltas[k], new_m[k], new_v[k] = (t.reshape(given[k].shape) for t in (dd, mm, vv))
    v_ = v_[0]
    rs_finish("ff", rs_ff, v_)
    marks = [adamw_big(k) for k in ("w_ff2", "w_ff1")]
    rs_finish("mix", rs_mix, all_of(*marks))
    marks = [adamw_big(k) for k in ("w_out", "w_attn_proj", "w_ssm_proj", "w_glu")]
    rs_in, token = rs_sum("in", rs_in, all_of(*marks))
    rs_finish("in", rs_in, token)
    adamw_big("w_in")

    names = ["w_ada", "b_ada", "norm1_g", "w_in", "b_in", "attn_sinks", "rel_bias", "lambda_re", "lambda_im",
             "log_step", "ssm_b_re", "ssm_b_im", "ssm_c_re", "ssm_c_im", "ssm_d", "w_glu", "b_glu", "w_attn_proj",
             "w_ssm_proj", "w_out", "norm2_g", "w_ff1", "w_ff2", "final_g"]
    return (loss, grad_x[None], *[grads[n] for n in names], *[deltas[n] for n in names],
            *[new_m[n] for n in names], *[new_v[n] for n in names])
```

```python
import math

import numpy as np
import jax
import jax.numpy as jnp
from jax import lax
from jax.experimental import pallas as pl
from jax.experimental.pallas import tpu as pltpu

F32 = jnp.float32
BF16 = jnp.bfloat16
MESH = pl.DeviceIdType.MESH

HEAD_DIM = 64
N_Q_HEADS = 16
N_KV_HEADS = 4
GQA_GROUP = N_Q_HEADS // N_KV_HEADS
ATTN_WIDTH = N_Q_HEADS * HEAD_DIM
KV_WIDTH = N_KV_HEADS * HEAD_DIM
BLOCK = 128
NUM_BUCKETS = 32
MAX_DISTANCE = 128
NEG_INF = -1e30
SSM_GROUP_CH = 16
SSM_STATE = 64
EPS = 1e-6
ADAM_LR = 0.001
ADAM_B1 = 0.9
ADAM_B2 = 0.999
ADAM_EPS = 1e-08
ADAM_WD = 0.01
ADAM_STEP = 10

N_CHIPS = 4
N_DEV = 8
SCAN_CHUNKS = 8
VMEM_LIMIT_BYTES = 48 * 1024 * 1024
SSM_VMEM_LIMIT_BYTES = 56 * 1024 * 1024


def _cparams(sem=None):
    return pltpu.CompilerParams(dimension_semantics=sem, vmem_limit_bytes=VMEM_LIMIT_BYTES)


class _Op:
    def __init__(self, arr, nsh=None, coff=0):
        self.arr, self.nsh, self.coff = arr, nsh, coff
        if nsh is None:
            self.rows, self.cols = arr.shape
        else:
            assert arr.shape[0] == nsh
            self.rows, self.cols = arr.shape[1], arr.shape[2] * nsh

    def spec(self, br, bc, idx):
        assert self.coff % bc == 0
        off = self.coff // bc
        if self.nsh is None:
            return pl.BlockSpec((br, bc), lambda *g: (idx(*g)[0], idx(*g)[1] + off))
        per = (self.cols // self.nsh) // bc
        assert per * bc * self.nsh == self.cols

        def imap(*g):
            r, c = idx(*g)
            c = c + off
            return (c // per, r, c % per)
        return pl.BlockSpec((None, br, bc), imap)


def _as_op(a):
    return a if isinstance(a, _Op) else _Op(a)


def _mm(a, b, mode, *, name, M, N, K, out_dtypes=(F32,), out_nsh=None, epilogue=None, extras=(),
        a_fn=None, ti=1024, tj=512, tk=2048, deps=()):
    nd = len(deps)
    a, b = _as_op(a), _as_op(b)
    ti, tj, tk = min(ti, M), min(tj, N), min(tk, K)
    a_w = a.cols // a.nsh if a.nsh else None
    b_w = b.cols // b.nsh if b.nsh else None
    if a_w:
        ti, tk = (min(ti, a_w), tk) if mode == "TN" else (ti, min(tk, a_w))
    if b_w:
        tj, tk = (tj, min(tk, b_w)) if mode == "NT" else (min(tj, b_w), tk)
    if out_nsh:
        tj = min(tj, N // out_nsh)
    assert M % ti == 0 and N % tj == 0 and K % tk == 0, (name, M, N, K, ti, tj, tk)
    nk = K // tk
    if mode == "NN":
        a_spec = a.spec(ti, tk, lambda i, j, k: (i, k))
        b_spec = b.spec(tk, tj, lambda i, j, k: (k, j))
        dims = (((1,), (0,)), ((), ()))
    elif mode == "NT":
        a_spec = a.spec(ti, tk, lambda i, j, k: (i, k))
        b_spec = b.spec(tj, tk, lambda i, j, k: (j, k))
        dims = (((1,), (1,)), ((), ()))
    else:
        a_spec = a.spec(tk, ti, lambda i, j, k: (k, i))
        b_spec = b.spec(tk, tj, lambda i, j, k: (k, j))
        dims = (((0,), (0,)), ((), ()))
    ex_specs, ex_arrs = [], []
    for op, kind in extras:
        op = _as_op(op)
        if kind == "tile":
            ex_specs.append(op.spec(ti, tj, lambda i, j, k: (i, j)))
        else:
            ex_specs.append(op.spec(1, tj, lambda i, j, k: (0, j)))
        ex_arrs.append(op.arr)
    ne, no = len(ex_arrs), len(out_dtypes)
    if out_nsh is None:
        out_shapes = [jax.ShapeDtypeStruct((M, N), d) for d in out_dtypes]
        out_specs = [pl.BlockSpec((ti, tj), lambda i, j, k: (i, j)) for _ in out_dtypes]
    else:
        per = (N // out_nsh) // tj
        assert per * tj * out_nsh == N
        out_shapes = [jax.ShapeDtypeStruct((out_nsh, M, N // out_nsh), d) for d in out_dtypes]
        out_specs = [pl.BlockSpec((None, ti, tj), lambda i, j, k: (j // per, i, j % per)) for _ in out_dtypes]

    def body(a_ref, b_ref, *rest):
        ex_refs, out_refs, acc = rest[:ne], rest[ne + nd:ne + nd + no], rest[ne + nd + no]
        k = pl.program_id(2)

        @pl.when(k == 0)
        def _():
            acc[...] = jnp.zeros_like(acc)

        av = a_ref[...]
        if a_fn is not None:
            av = a_fn(av)
        acc[...] += lax.dot_general(av.astype(BF16), b_ref[...].astype(BF16), dims,
                                    preferred_element_type=F32)

        @pl.when(k == nk - 1)
        def _():
            res = acc[...]
            outs = epilogue(res, *[r[...] for r in ex_refs]) if epilogue is not None else (res,)
            for o_ref, o in zip(out_refs, outs):
                o_ref[...] = o.astype(o_ref.dtype)

    outs = pl.pallas_call(
        body, name=name, grid=(M // ti, N // tj, nk),
        in_specs=[a_spec, b_spec] + ex_specs + [pl.BlockSpec(memory_space=pl.ANY)] * nd,
        out_specs=out_specs, out_shape=out_shapes,
        scratch_shapes=[pltpu.VMEM((ti, tj), F32)],
        compiler_params=_cparams(("parallel", "parallel", "arbitrary")),
    )(a.arr, b.arr, *ex_arrs, *deps)
    return outs[0] if no == 1 else outs


def _rowwise(fn, ins, outs, accs, *, name, rows, tr=256, deps=()):
    tr = min(tr, rows)
    assert rows % tr == 0
    in_specs, arrs = [], []
    for op, kind, width in ins:
        op = _as_op(op)
        if kind == "tile":
            in_specs.append(op.spec(tr, width, lambda i: (i, 0)))
        else:
            in_specs.append(op.spec(op.rows, width, lambda i: (0, 0)))
        arrs.append(op.arr)
    ni, no, na = len(ins), len(outs), len(accs)
    flipped = [len(o) == 3 for o in outs]
    out_shapes = [jax.ShapeDtypeStruct((o[0], rows) if t else (rows, o[0]), o[1]) for o, t in zip(outs, flipped)]
    out_specs = [pl.BlockSpec((o[0], tr), lambda i: (0, i)) if t else pl.BlockSpec((tr, o[0]), lambda i: (i, 0))
                 for o, t in zip(outs, flipped)]
    out_shapes += [jax.ShapeDtypeStruct((1, w), F32) for w in accs]
    out_specs += [pl.BlockSpec((1, w), lambda i: (0, 0)) for w in accs]

    def body(*refs):
        nd = len(deps)
        in_refs, out_refs, acc_refs = refs[:ni], refs[ni + nd:ni + nd + no], refs[ni + nd + no:]
        res = fn(*[r[...] for r in in_refs])
        if not isinstance(res, (tuple, list)):
            res = (res,)
        for o_ref, r, t in zip(out_refs, res[:no], flipped):
            o_ref[...] = (r.astype(F32).T if t else r).astype(o_ref.dtype)
        if na:
            @pl.when(pl.program_id(0) == 0)
            def _():
                for a_ref in acc_refs:
                    a_ref[...] = jnp.zeros_like(a_ref)
            for a_ref, r in zip(acc_refs, res[no:]):
                a_ref[...] += r.astype(F32)

    res = pl.pallas_call(
        body, name=name, grid=(rows // tr,), in_specs=in_specs + [pl.BlockSpec(memory_space=pl.ANY)] * len(deps),
        out_specs=out_specs, out_shape=out_shapes, compiler_params=_cparams(("arbitrary",)),
    )(*arrs, *deps)
    return res


def _norm_mod(x, g, sh, sc):
    y = x * lax.rsqrt(jnp.mean(x * x, axis=-1, keepdims=True) + EPS) * g
    return y * (1.0 + sc) + sh


def _sigmoid(x):
    return 1.0 / (1.0 + jnp.exp(-x))


def _silu(x):
    return x * _sigmoid(x)


def _gelu(x):
    return 0.5 * x * (1.0 + jnp.tanh(math.sqrt(2.0 / math.pi) * (x + 0.044715 * (x * x * x))))


def _merge(ga, gs, ya, ys):
    ga, gs, ya, ys = (v.astype(F32) for v in (ga, gs, ya, ys))
    return _sigmoid(ga) * ya + _sigmoid(gs) * ys


def _attn_head(q, kp, kc, vp, vc, sink, bias_p, bias_c, not_first):
    nt = (((1,), (1,)), ((), ()))
    nn = (((1,), (0,)), ((), ()))
    qb = q.astype(BF16)
    scale = HEAD_DIM ** -0.5
    sp = lax.dot_general(qb, kp.astype(BF16), nt, preferred_element_type=F32) * scale + bias_p
    sc = lax.dot_general(qb, kc.astype(BF16), nt, preferred_element_type=F32) * scale + bias_c
    qi = lax.broadcasted_iota(jnp.int32, sp.shape, 0) & (BLOCK - 1)
    ki = lax.broadcasted_iota(jnp.int32, sp.shape, 1)
    sp = jnp.where(jnp.logical_and(ki > qi, not_first), sp, NEG_INF)
    sc = jnp.where(ki <= qi, sc, NEG_INF)
    m = jnp.maximum(jnp.maximum(jnp.max(sp, axis=-1, keepdims=True), jnp.max(sc, axis=-1, keepdims=True)), sink)
    m = lax.stop_gradient(m)
    pp = jnp.exp(sp - m)
    pc = jnp.exp(sc - m)
    denom = jnp.sum(pp, axis=-1, keepdims=True) + jnp.sum(pc, axis=-1, keepdims=True) + jnp.exp(sink - m)
    o = lax.dot_general((pp / denom).astype(BF16), vp.astype(BF16), nn, preferred_element_type=F32)
    o = o + lax.dot_general((pc / denom).astype(BF16), vc.astype(BF16), nn, preferred_element_type=F32)
    return o


def _attn_fwd(qh, kh, vh, sinks, bias, name):
    s = qh.shape[1]
    nb = s // BLOCK
    G = GQA_GROUP
    R = G * BLOCK

    def body(q_ref, kp_ref, kc_ref, vp_ref, vc_ref, sink_ref, bias_ref, o_ref):
        not_first = pl.program_id(0) > 0
        for kv in range(N_KV_HEADS):
            hs = slice(kv * G, (kv + 1) * G)
            o = _attn_head(q_ref[hs].reshape(R, HEAD_DIM), kp_ref[kv], kc_ref[kv], vp_ref[kv], vc_ref[kv],
                           sink_ref[kv * R:(kv + 1) * R, 0:1],
                           bias_ref[hs, :, 0:BLOCK].reshape(R, BLOCK), bias_ref[hs, :, BLOCK:2 * BLOCK].reshape(R, BLOCK),
                           not_first)
            o_ref[hs] = o.reshape(G, BLOCK, HEAD_DIM).astype(o_ref.dtype)

    cur = lambda i: (0, i, 0)
    prev = lambda i: (0, jnp.maximum(i - 1, 0), 0)
    return pl.pallas_call(
        body, name=name, grid=(nb,),
        in_specs=[pl.BlockSpec((N_Q_HEADS, BLOCK, HEAD_DIM), cur),
                  pl.BlockSpec((N_KV_HEADS, BLOCK, HEAD_DIM), prev), pl.BlockSpec((N_KV_HEADS, BLOCK, HEAD_DIM), cur),
                  pl.BlockSpec((N_KV_HEADS, BLOCK, HEAD_DIM), prev), pl.BlockSpec((N_KV_HEADS, BLOCK, HEAD_DIM), cur),
                  pl.BlockSpec((N_Q_HEADS * BLOCK, 128), lambda i: (0, 0)),
                  pl.BlockSpec((N_Q_HEADS, BLOCK, 2 * BLOCK), lambda i: (0, 0, 0))],
        out_specs=pl.BlockSpec((N_Q_HEADS, BLOCK, HEAD_DIM), cur),
        out_shape=jax.ShapeDtypeStruct((N_Q_HEADS, s, HEAD_DIM), BF16),
        compiler_params=_cparams(("arbitrary",)),
    )(qh, kh, kh, vh, vh, sinks, bias)


def _attn_bwd(qh, kh, vh, doh, sinks, bias, name):
    s = qh.shape[1]
    nb = s // BLOCK
    G = GQA_GROUP
    R = G * BLOCK

    def body(q_ref, kp_ref, kc_ref, vp_ref, vc_ref, do_ref, sink_ref, bias_ref,
             dq_ref, dk_ref, dv_ref, dsink_ref, dbias_ref, ck, cv):
        i = pl.program_id(1)

        @pl.when(i == 0)
        def _():
            dsink_ref[...] = jnp.zeros_like(dsink_ref)
            dbias_ref[...] = jnp.zeros_like(dbias_ref)
            ck[...] = jnp.zeros_like(ck)
            cv[...] = jnp.zeros_like(cv)

        @pl.when(i < nb)
        def _():
            not_first = i > 0
            _, vjp = jax.vjp(lambda q, a, b, c, d, sk, e, f: _attn_head(q, a, b, c, d, sk, e, f, not_first),
                             q_ref[...].astype(F32).reshape(R, HEAD_DIM), kp_ref[...].astype(F32),
                             kc_ref[...].astype(F32), vp_ref[...].astype(F32), vc_ref[...].astype(F32),
                             sink_ref[:, 0:1], bias_ref[:, :, 0:BLOCK].reshape(R, BLOCK),
                             bias_ref[:, :, BLOCK:2 * BLOCK].reshape(R, BLOCK))
            dq, dkp, dkc, dvp, dvc, dsk, dbp, dbc = vjp(do_ref[...].reshape(R, HEAD_DIM).astype(F32))
            dq_ref[...] = dq.reshape(G, BLOCK, HEAD_DIM).astype(dq_ref.dtype)
            dsink_ref[...] += jnp.broadcast_to(dsk, (R, 128))
            dbias_ref[:, :, 0:BLOCK] += dbp.reshape(G, BLOCK, BLOCK)
            dbias_ref[:, :, BLOCK:2 * BLOCK] += dbc.reshape(G, BLOCK, BLOCK)
            dk_ref[...] = (ck[...] + dkp).astype(dk_ref.dtype)
            dv_ref[...] = (cv[...] + dvp).astype(dv_ref.dtype)
            ck[...] = dkc
            cv[...] = dvc

        @pl.when(i == nb)
        def _():
            dk_ref[...] = ck[...].astype(dk_ref.dtype)
            dv_ref[...] = cv[...].astype(dv_ref.dtype)

    qcur = lambda kv, i: (kv, jnp.minimum(i, nb - 1), 0)
    kcur = lambda kv, i: (kv, jnp.minimum(i, nb - 1), 0)
    kprev = lambda kv, i: (kv, jnp.clip(i - 1, 0, nb - 1), 0)
    qspec = pl.BlockSpec((G, BLOCK, HEAD_DIM), qcur)
    kc_spec = pl.BlockSpec((None, BLOCK, HEAD_DIM), kcur)
    kp_spec = pl.BlockSpec((None, BLOCK, HEAD_DIM), kprev)
    return pl.pallas_call(
        body, name=name, grid=(N_KV_HEADS, nb + 1),
        in_specs=[qspec, kp_spec, kc_spec, kp_spec, kc_spec, qspec,
                  pl.BlockSpec((R, 128), lambda kv, i: (kv, 0)),
                  pl.BlockSpec((G, BLOCK, 2 * BLOCK), lambda kv, i: (kv, 0, 0))],
        out_specs=[qspec, kp_spec, kp_spec,
                   pl.BlockSpec((R, 128), lambda kv, i: (kv, 0)),
                   pl.BlockSpec((G, BLOCK, 2 * BLOCK), lambda kv, i: (kv, 0, 0))],
        out_shape=[jax.ShapeDtypeStruct((N_Q_HEADS, s, HEAD_DIM), BF16),
                   jax.ShapeDtypeStruct((N_KV_HEADS, s, HEAD_DIM), BF16),
                   jax.ShapeDtypeStruct((N_KV_HEADS, s, HEAD_DIM), BF16),
                   jax.ShapeDtypeStruct((N_Q_HEADS * BLOCK, 128), F32),
                   jax.ShapeDtypeStruct((N_Q_HEADS, BLOCK, 2 * BLOCK), F32)],
        scratch_shapes=[pltpu.VMEM((BLOCK, HEAD_DIM), F32), pltpu.VMEM((BLOCK, HEAD_DIM), F32)],
        compiler_params=_cparams(("arbitrary", "arbitrary")),
    )(qh, kh, kh, vh, vh, doh, sinks, bias)


def _cmul(ar, ai, br, bi):
    return ar * br - ai * bi, ar * bi + ai * br


def _scan_passes(a_ref, b_ref, x_ref, xp_ref, da_ref, *, s, tc, reverse):
    nc = SCAN_CHUNKS
    steps = s // nc
    with_da = xp_ref is not None
    unroll = 8 if steps % 8 == 0 else 1

    def shift(v, d):
        row = lax.broadcasted_iota(jnp.int32, v.shape, 0)
        if reverse:
            return jnp.where(row < nc - d, pltpu.roll(v, nc - d, 0), 0.0)
        return jnp.where(row >= d, pltpu.roll(v, d, 0), 0.0)

    def run():
        ar = jnp.broadcast_to(a_ref[0], (nc, tc))
        ai = jnp.broadcast_to(a_ref[1], (nc, tc))

        def row_of(step):
            j = (steps - 1 - step) if reverse else step
            return pl.multiple_of(j * nc, nc)

        def p1(step, st):
            sr, si = st
            r0 = row_of(step)
            mr, mi = _cmul(ar, ai, sr, si)
            sr = mr + b_ref[0, pl.ds(r0, nc), :]
            si = mi + b_ref[1, pl.ds(r0, nc), :]
            x_ref[0, pl.ds(r0, nc), :] = sr
            x_ref[1, pl.ds(r0, nc), :] = si
            return sr, si
        zero = jnp.zeros((nc, tc), F32)
        er, ei = lax.fori_loop(0, steps, p1, (zero, zero), unroll=unroll)

        pr, pi_ = jnp.ones((nc, tc), F32), zero
        br, bi, left = ar, ai, steps
        while left:
            if left & 1:
                pr, pi_ = _cmul(pr, pi_, br, bi)
            br, bi = _cmul(br, bi, br, bi)
            left >>= 1
        cr, ci = shift(er, 1), shift(ei, 1)
        d = 1
        while d < nc:
            mr, mi = _cmul(pr, pi_, shift(cr, d), shift(ci, d))
            cr, ci = cr + mr, ci + mi
            pr, pi_ = _cmul(pr, pi_, pr, pi_)
            d *= 2

        def p2(step, st):
            qr, qi, dar, dai = st
            r0 = row_of(step)
            qr, qi = _cmul(ar, ai, qr, qi)
            fr, fi = _cmul(qr, qi, cr, ci)
            xr = x_ref[0, pl.ds(r0, nc), :] + fr
            xi = x_ref[1, pl.ds(r0, nc), :] + fi
            x_ref[0, pl.ds(r0, nc), :] = xr
            x_ref[1, pl.ds(r0, nc), :] = xi
            if with_da:
                jm = jnp.where(step == steps - 1, steps - 1, steps - 2 - step)
                rp = pl.multiple_of(jm * nc, nc)
                vr, vi = xp_ref[0, pl.ds(rp, nc), :], xp_ref[1, pl.ds(rp, nc), :]
                row = lax.broadcasted_iota(jnp.int32, (nc, tc), 0)
                first = step == steps - 1
                sel = jnp.logical_and(first, row == 0)
                vr = jnp.where(sel, 0.0, jnp.where(first, pltpu.roll(vr, 1, 0), vr))
                vi = jnp.where(sel, 0.0, jnp.where(first, pltpu.roll(vi, 1, 0), vi))
                dar = dar + xr * vr + xi * vi
                dai = dai + xi * vr - xr * vi
            return qr, qi, dar, dai
        _, _, dar, dai = lax.fori_loop(0, steps, p2, (jnp.ones((nc, tc), F32), zero, zero, zero), unroll=unroll)
        if with_da:
            da_ref[0] = jnp.sum(dar, axis=0, keepdims=True)
            da_ref[1] = jnp.sum(dai, axis=0, keepdims=True)

    run()


def _ssm_fwd(u, bd, cd, a, d_row, *, name, sb, sbn):
    s, w = u.shape
    nst = a.shape[2]
    nblk = w // sb
    rows = min(512, s)
    nn = (((1,), (0,)), ((), ()))

    def body(u_ref, bre_ref, bim_ref, cre_ref, cim_ref, a_ref, d_ref, y_ref, x_ref):

        def fill(r, carry):
            r0 = pl.multiple_of(r * rows, rows)
            ub = u_ref[pl.ds(r0, rows), :].astype(BF16)
            x_ref[0, pl.ds(r0, rows), :] = lax.dot_general(ub, bre_ref[...].astype(BF16), nn, preferred_element_type=F32)
            x_ref[1, pl.ds(r0, rows), :] = lax.dot_general(ub, bim_ref[...].astype(BF16), nn, preferred_element_type=F32)
            return carry
        lax.fori_loop(0, s // rows, fill, 0)
        _scan_passes(a_ref, x_ref, x_ref, None, None, s=s, tc=sbn, reverse=False)

        def project(r, carry):
            r0 = pl.multiple_of(r * rows, rows)
            y = lax.dot_general(x_ref[0, pl.ds(r0, rows), :].astype(BF16), cre_ref[...].astype(BF16), nn, preferred_element_type=F32)
            y = y + lax.dot_general(x_ref[1, pl.ds(r0, rows), :].astype(BF16), cim_ref[...].astype(BF16), nn, preferred_element_type=F32)
            y_ref[pl.ds(r0, rows), :] = y + d_ref[...] * u_ref[pl.ds(r0, rows), :]
            return carry
        lax.fori_loop(0, s // rows, project, 0)

    return pl.pallas_call(
        body, name=name, grid=(nblk,),
        in_specs=[pl.BlockSpec((s, sb), lambda j: (0, j)),
                  pl.BlockSpec((sb, sbn), lambda j: (j, j)), pl.BlockSpec((sb, sbn), lambda j: (j, nblk + j)),
                  pl.BlockSpec((sbn, sb), lambda j: (j, j)), pl.BlockSpec((sbn, sb), lambda j: (nblk + j, j)),
                  pl.BlockSpec((2, 1, sbn), lambda j: (0, 0, j)), pl.BlockSpec((1, sb), lambda j: (0, j))],
        out_specs=[pl.BlockSpec((s, sb), lambda j: (0, j)), pl.BlockSpec((2, s, sbn), lambda j: (0, 0, j))],
        out_shape=[jax.ShapeDtypeStruct((s, w), F32), jax.ShapeDtypeStruct((2, s, nst), F32)],
        compiler_params=pltpu.CompilerParams(dimension_semantics=("arbitrary",), vmem_limit_bytes=SSM_VMEM_LIMIT_BYTES),
    )(u, bd, bd, cd, cd, a, d_row)


def _ssm_bwd(dy, u, xs, bd, cd, a, d_row, *, name, sb, sbn, deps=()):
    s, w = u.shape
    nst = a.shape[2]
    nblk = w // sb
    rows = min(512, s)
    nt = (((1,), (1,)), ((), ()))
    tn = (((0,), (0,)), ((), ()))

    def body(dy_ref, u_ref, xs_hbm, bre_ref, bim_ref, cre_ref, cim_ref, a_ref, d_ref, *rest):
        du_ref, gb_ref, gc_ref, da_ref, gd_ref, lam, xs_ref, sem = rest[len(deps):]
        j = pl.program_id(0)
        fetch = pltpu.make_async_copy(xs_hbm.at[:, :, pl.ds(pl.multiple_of(j * sbn, sbn), sbn)], xs_ref, sem)
        fetch.start()

        def fill(r, carry):
            r0 = pl.multiple_of(r * rows, rows)
            dyb = dy_ref[pl.ds(r0, rows), :].astype(BF16)
            lam[0, pl.ds(r0, rows), :] = lax.dot_general(dyb, cre_ref[...].astype(BF16), nt, preferred_element_type=F32)
            lam[1, pl.ds(r0, rows), :] = lax.dot_general(dyb, cim_ref[...].astype(BF16), nt, preferred_element_type=F32)
            return carry
        lax.fori_loop(0, s // rows, fill, 0)
        fetch.wait()
        _scan_passes(a_ref, lam, lam, xs_ref, da_ref, s=s, tc=sbn, reverse=True)
        gb_ref[...] = jnp.zeros_like(gb_ref)
        gc_ref[...] = jnp.zeros_like(gc_ref)
        gd_ref[...] = jnp.zeros_like(gd_ref)

        def project(r, carry):
            r0 = pl.multiple_of(r * rows, rows)
            dyv, uv = dy_ref[pl.ds(r0, rows), :], u_ref[pl.ds(r0, rows), :]
            dyb, ub = dyv.astype(BF16), uv.astype(BF16)
            lr, li = lam[0, pl.ds(r0, rows), :].astype(BF16), lam[1, pl.ds(r0, rows), :].astype(BF16)
            du = lax.dot_general(lr, bre_ref[...].astype(BF16), nt, preferred_element_type=F32)
            du = du + lax.dot_general(li, bim_ref[...].astype(BF16), nt, preferred_element_type=F32)
            du_ref[pl.ds(r0, rows), :] = du + d_ref[...] * dyv
            gb_ref[:, 0:sbn] += lax.dot_general(ub, lr, tn, preferred_element_type=F32)
            gb_ref[:, sbn:2 * sbn] += lax.dot_general(ub, li, tn, preferred_element_type=F32)
            gc_ref[0] += lax.dot_general(xs_ref[0, pl.ds(r0, rows), :].astype(BF16), dyb, tn, preferred_element_type=F32)
            gc_ref[1] += lax.dot_general(xs_ref[1, pl.ds(r0, rows), :].astype(BF16), dyb, tn, preferred_element_type=F32)
            gd_ref[...] += jnp.sum(dyv * uv, axis=0, keepdims=True)
            return carry
        lax.fori_loop(0, s // rows, project, 0)

    col = lambda j: (0, j)
    return pl.pallas_call(
        body, name=name, grid=(nblk,),
        in_specs=[pl.BlockSpec((s, sb), col), pl.BlockSpec((s, sb), col), pl.BlockSpec(memory_space=pl.ANY),
                  pl.BlockSpec((sb, sbn), lambda j: (j, j)), pl.BlockSpec((sb, sbn), lambda j: (j, nblk + j)),
                  pl.BlockSpec((sbn, sb), lambda j: (j, j)), pl.BlockSpec((sbn, sb), lambda j: (nblk + j, j)),
                  pl.BlockSpec((2, 1, sbn), lambda j: (0, 0, j)), pl.BlockSpec((1, sb), col)]
        + [pl.BlockSpec(memory_space=pl.ANY)] * len(deps),
        out_specs=[pl.BlockSpec((s, sb), col), pl.BlockSpec((sb, 2 * sbn), lambda j: (j, 0)),
                   pl.BlockSpec((2, sbn, sb), lambda j: (0, j, 0)), pl.BlockSpec((2, 1, sbn), lambda j: (0, 0, j)),
                   pl.BlockSpec((1, sb), col)],
        out_shape=[jax.ShapeDtypeStruct((s, w), F32), jax.ShapeDtypeStruct((w, 2 * sbn), F32),
                   jax.ShapeDtypeStruct((2, nst, sb), F32), jax.ShapeDtypeStruct((2, 1, nst), F32),
                   jax.ShapeDtypeStruct((1, w), F32)],
        scratch_shapes=[pltpu.VMEM((2, s, sbn), F32), pltpu.VMEM((2, s, sbn), F32), pltpu.SemaphoreType.DMA],
        compiler_params=pltpu.CompilerParams(dimension_semantics=("arbitrary",), vmem_limit_bytes=SSM_VMEM_LIMIT_BYTES),
    )(dy, u, xs, bd, bd, cd, cd, a, d_row, *deps)


def _adamw_math(w, g, m, v):
    nm = ADAM_B1 * m + (1.0 - ADAM_B1) * g
    nv = ADAM_B2 * v + (1.0 - ADAM_B2) * (g * g)
    m_hat = nm / (1.0 - ADAM_B1 ** ADAM_STEP)
    v_hat = nv / (1.0 - ADAM_B2 ** ADAM_STEP)
    return -ADAM_LR * (m_hat / (jnp.sqrt(v_hat) + ADAM_EPS) + ADAM_WD * w), nm, nv


def _adamw_many(ws, gs, ms, vs, name, deps=()):
    n, nd = len(ws), len(deps)

    def body(*refs):
        outs = refs[4 * n + nd:]
        for i in range(n):
            d, nm, nv = _adamw_math(refs[i][...], refs[n + i][...], refs[2 * n + i][...], refs[3 * n + i][...])
            outs[i][...], outs[n + i][...], outs[2 * n + i][...] = d, nm, nv

    whole = pl.BlockSpec(memory_space=pltpu.VMEM)
    res = pl.pallas_call(
        body, name=name, in_specs=[whole] * (4 * n) + [pl.BlockSpec(memory_space=pl.ANY)] * nd,
        out_specs=[whole] * (3 * n), out_shape=[jax.ShapeDtypeStruct(w.shape, F32) for w in ws] * 3,
        compiler_params=pltpu.CompilerParams(vmem_limit_bytes=VMEM_LIMIT_BYTES),
    )(*ws, *gs, *ms, *vs, *deps)
    return res[:n], res[n:2 * n], res[2 * n:]


def _adamw(w, g, m, v, name, deps=(), echo=False):
    nd = len(deps)
    r, c = w.shape
    tr = r
    for cand in (512, 256, 128, 64, 32, 16, 8):
        if r % cand == 0 and cand * c * 4 <= 2 * 1024 * 1024:
            tr = cand
            break

    def body(w_ref, g_ref, m_ref, v_ref, *rest):
        d_ref, nm_ref, nv_ref = rest[nd:nd + 3]
        gv = g_ref[...]
        d_ref[...], nm_ref[...], nv_ref[...] = _adamw_math(w_ref[...], gv, m_ref[...], v_ref[...])
        if echo:
            rest[nd + 3][...] = gv

    no = 4 if echo else 3
    spec = pl.BlockSpec((tr, c), lambda i: (i, 0))
    sds = jax.ShapeDtypeStruct((r, c), F32)
    return pl.pallas_call(body, name=name, grid=(r // tr,),
                          in_specs=[spec] * 4 + [pl.BlockSpec(memory_space=pl.ANY)] * nd, out_specs=[spec] * no,
                          out_shape=[sds] * no, compiler_params=_cparams(("parallel",)))(w, g, m, v, *deps)


def _sum_lead(x, name, out_dtype=F32):
    n, r, c = x.shape
    tr = r
    for cand in (512, 256, 128, 64, 32, 16, 8):
        if r % cand == 0 and n * cand * c * 4 <= 4 * 1024 * 1024:
            tr = cand
            break

    def body(x_ref, o_ref):
        acc = x_ref[0].astype(F32)
        for k in range(1, n):
            acc = acc + x_ref[k].astype(F32)
        o_ref[...] = acc.astype(o_ref.dtype)

    return pl.pallas_call(body, name=name, grid=(r // tr,),
                          in_specs=[pl.BlockSpec((n, tr, c), lambda i: (0, i, 0))],
                          out_specs=pl.BlockSpec((tr, c), lambda i: (i, 0)),
                          out_shape=jax.ShapeDtypeStruct((r, c), out_dtype),
                          compiler_params=_cparams(("parallel",)))(x)


def _row_tile(rows, row_bytes, budget, least=8):
    for cand in (1024, 512, 256, 128, 64, 32, 16, 8):
        if cand >= least and rows % cand == 0 and cand * row_bytes <= budget:
            return cand
    return rows


def _cast_into_slot(w, slot, name):
    r, c = w.shape
    tr = _row_tile(r, c * 4, 4 * 1024 * 1024, least=16)

    def body(slot_ref, w_ref, o_ref):
        o_ref[...] = w_ref[...].astype(o_ref.dtype)

    gs = pltpu.PrefetchScalarGridSpec(
        num_scalar_prefetch=1, grid=(r // tr,),
        in_specs=[pl.BlockSpec((tr, c), lambda i, s: (i, 0))],
        out_specs=pl.BlockSpec((None, tr, c), lambda i, s: (s[0], i, 0)))
    return pl.pallas_call(body, name=name, grid_spec=gs, out_shape=jax.ShapeDtypeStruct((N_CHIPS, r, c), BF16),
                          compiler_params=_cparams(("parallel",)))(slot, w)


def _sum_own(p, t, sel, name):
    _, h, c = p.shape
    tr = _row_tile(h, c * 4, 2 * 1024 * 1024, least=16)
    nblk = h // tr

    def body(sel_ref, p_ref, t_ref, o_ref):
        acc = p_ref[...].astype(F32)
        for k in range(3):
            acc = acc + t_ref[k].astype(F32)
        o_ref[...] = acc

    gs = pltpu.PrefetchScalarGridSpec(
        num_scalar_prefetch=1, grid=(nblk,),
        in_specs=[pl.BlockSpec((None, tr, c), lambda i, s: (s[0], i, 0)),
                  pl.BlockSpec((3, tr, c), lambda i, s: (0, i, 0))],
        out_specs=pl.BlockSpec((tr, c), lambda i, s: (s[1] * nblk + i, 0)))
    return pl.pallas_call(body, name=name, grid_spec=gs, out_shape=jax.ShapeDtypeStruct((2 * h, c), F32),
                          compiler_params=_cparams(("parallel",)))(sel, p, t)


def _add_half(g, t, half, name):
    n, r, c = g.shape
    h = r // 2
    tr = h
    for cand in (512, 256, 128, 64, 32, 16):
        if h % cand == 0 and cand * c * 2 <= 2 * 1024 * 1024:
            tr = cand
            break
    nblk = h // tr

    def body(half_ref, g_ref, t_ref, o_ref):
        o_ref[...] = (g_ref[...].astype(F32) + t_ref[...].astype(F32)).astype(o_ref.dtype)

    gs = pltpu.PrefetchScalarGridSpec(
        num_scalar_prefetch=1, grid=(n, nblk),
        in_specs=[pl.BlockSpec((None, tr, c), lambda j, i, hr: (j, hr[0] * nblk + i, 0)),
                  pl.BlockSpec((None, tr, c), lambda j, i, hr: (j, i, 0))],
        out_specs=pl.BlockSpec((None, tr, c), lambda j, i, hr: (j, i, 0)))
    return pl.pallas_call(body, name=name, grid_spec=gs, out_shape=jax.ShapeDtypeStruct((n, h, c), BF16),
                          compiler_params=_cparams(("parallel", "parallel")))(half, g, t)


def _position():
    x, y, c = lax.axis_index("x"), lax.axis_index("y"), lax.axis_index("c")
    return x, y, c


def _allgather8(xs, name):
    m_per, n = xs.shape

    def body(x_ref, out_ref, send_sems, recv_sems, local_sem):
        x, y, c = _position()
        me, sibling = (x, y, c), (x, y, 1 - c)
        chips = [(1 - x, y), (x, 1 - y), (1 - x, 1 - y)]

        def rows(px, py, pc):
            return out_ref.at[pl.ds((4 * px + 2 * py + pc) * m_per, m_per), :]

        def copy(k, block, to, src=None):
            return pltpu.make_async_remote_copy(
                src_ref=rows(*block) if src is None else src, dst_ref=rows(*block),
                send_sem=send_sems.at[k], recv_sem=recv_sems.at[k], device_id=to, device_id_type=MESH)

        mine = pltpu.make_async_copy(x_ref, rows(*me), local_sem)
        mine.start()
        first = [copy(0, me, sibling, src=x_ref)]
        first += [copy(1 + j, me, (*chip, c), src=x_ref) for j, chip in enumerate(chips)]
        for cp in first:
            cp.start()
        passed = [copy(4 + j, (*chip, c), sibling) for j, chip in enumerate(chips)]
        for j, chip in enumerate(chips):
            copy(1 + j, (*chip, c), me).wait_recv()
            passed[j].start()
        copy(0, sibling, me).wait_recv()
        for j, chip in enumerate(chips):
            copy(4 + j, (*chip, 1 - c), me).wait_recv()
        for cp in first + passed:
            cp.wait_send()
        mine.wait()

    return pl.pallas_call(
        body, name=name, out_shape=jax.ShapeDtypeStruct((N_DEV * m_per, n), xs.dtype),
        in_specs=[pl.BlockSpec(memory_space=pltpu.VMEM)], out_specs=pl.BlockSpec(memory_space=pltpu.VMEM),
        scratch_shapes=[pltpu.SemaphoreType.DMA((7,)), pltpu.SemaphoreType.DMA((7,)), pltpu.SemaphoreType.DMA],
        compiler_params=pltpu.CompilerParams(vmem_limit_bytes=VMEM_LIMIT_BYTES),
    )(xs)


_HBM = pl.BlockSpec(memory_space=pltpu.HBM)


_SEM = pl.BlockSpec(memory_space=pltpu.SEMAPHORE)
_ANY = pl.BlockSpec(memory_space=pl.ANY)
_EFFECT = pltpu.SideEffectType.DATAFLOW_SIDE_EFFECTING


def _in_hbm(a):
    return pltpu.with_memory_space_constraint(a, pltpu.HBM)


def _several(after):
    return list(after) if isinstance(after, (list, tuple)) else [after]


def _gather_start(ws, groups, after, name):
    n = len(ws)
    after = _several(after)

    def body(*refs):
        in_refs = refs[:n]
        sems, token = refs[2 * n + len(after):-1], refs[-1]
        x, y, c = _position()
        mychip = 2 * x + y
        chips = [(1 - x, y), (x, 1 - y), (1 - x, 1 - y)]
        for g, members in enumerate(groups):
            for k, i in enumerate(members):
                h = ws[i].shape[1] // 2
                mine = in_refs[i].at[mychip, pl.ds(c * h, h), :]
                for j, (px, py) in enumerate(chips):
                    pltpu.make_async_remote_copy(
                        src_ref=mine, dst_ref=mine, send_sem=sems[2 * g].at[3 * k + j],
                        recv_sem=sems[2 * g + 1].at[3 * k + j], device_id=(px, py, c), device_id_type=MESH).start()
        token[...] = jnp.zeros_like(token)

    sem_shapes = [pltpu.SemaphoreType.DMA((3 * len(m),)) for m in groups for _ in range(2)]
    res = pl.pallas_call(
        body, name=name,
        out_shape=[pltpu.HBM(w.shape, w.dtype) for w in ws] + sem_shapes + [jax.ShapeDtypeStruct((8, 128), F32)],
        in_specs=[_HBM] * n + [_ANY] * len(after),
        out_specs=[_HBM] * n + [_SEM] * len(sem_shapes) + [pl.BlockSpec(memory_space=pltpu.VMEM)],
        input_output_aliases={i: i for i in range(n)},
        compiler_params=pltpu.CompilerParams(has_side_effects=_EFFECT),
    )(*[_in_hbm(w) for w in ws], *after)
    bufs, sems, token = res[:n], res[n:-1], res[-1]
    return list(bufs), [(sems[2 * g], sems[2 * g + 1]) for g in range(len(groups))], token


def _gather_wait(bufs, send_sems, recv_sems, after, name):
    m = len(bufs)

    def body(*refs):
        in_refs = refs[:m]
        send, recv = refs[m], refs[m + 1]
        x, y, c = _position()
        mychip = 2 * x + y
        chips = [(1 - x, y), (x, 1 - y), (1 - x, 1 - y)]
        for k in range(m):
            h = bufs[k].shape[1] // 2
            mine = in_refs[k].at[mychip, pl.ds(c * h, h), :]
            for j, (px, py) in enumerate(chips):
                cp = pltpu.make_async_remote_copy(
                    src_ref=mine, dst_ref=in_refs[k].at[2 * px + py, pl.ds(c * h, h), :],
                    send_sem=send.at[3 * k + j], recv_sem=recv.at[3 * k + j],
                    device_id=(px, py, c), device_id_type=MESH)
                cp.wait_send()
                cp.wait_recv()

    res = pl.pallas_call(
        body, name=name, out_shape=[pltpu.HBM(b.shape, b.dtype) for b in bufs],
        in_specs=[_HBM] * m + [_SEM, _SEM] + [_ANY] * len(_several(after)), out_specs=[_HBM] * m,
        input_output_aliases={k: k for k in range(m)},
        compiler_params=pltpu.CompilerParams(has_side_effects=_EFFECT),
    )(*bufs, send_sems, recv_sems, *_several(after))
    return list(res)


def _forward_halves(ws, name):
    n = len(ws)

    def body(*refs):
        out_refs = refs[n:2 * n]
        send_sems, recv_sems = refs[2 * n:]
        x, y, c = _position()
        me, sibling = (x, y, c), (x, y, 1 - c)
        chips = [(1 - x, y), (x, 1 - y), (1 - x, 1 - y)]
        cps = []
        for i in range(n):
            h = ws[i].shape[1] // 2
            for j, (px, py) in enumerate(chips):
                got = out_refs[i].at[2 * px + py, pl.ds(c * h, h), :]
                cp = pltpu.make_async_remote_copy(
                    src_ref=got, dst_ref=got, send_sem=send_sems.at[3 * i + j], recv_sem=recv_sems.at[3 * i + j],
                    device_id=sibling, device_id_type=MESH)
                cp.start()
                cps.append(cp)
        for i in range(n):
            h = ws[i].shape[1] // 2
            for j, (px, py) in enumerate(chips):
                other = out_refs[i].at[2 * px + py, pl.ds((1 - c) * h, h), :]
                pltpu.make_async_remote_copy(
                    src_ref=other, dst_ref=other, send_sem=send_sems.at[3 * i + j], recv_sem=recv_sems.at[3 * i + j],
                    device_id=me, device_id_type=MESH).wait_recv()
        for cp in cps:
            cp.wait_send()

    return pl.pallas_call(
        body, name=name,
        out_shape=[jax.ShapeDtypeStruct(w.shape, w.dtype) for w in ws],
        in_specs=[_HBM] * n, out_specs=[_HBM] * n, input_output_aliases={i: i for i in range(n)},
        scratch_shapes=[pltpu.SemaphoreType.DMA((3 * n,)), pltpu.SemaphoreType.DMA((3 * n,))],
    )(*ws)


def _copies_start(arrays, copies, nsem, after, name):
    n = len(arrays)
    after = _several(after)
    first = 2 * n + len(after)

    def body(*refs):
        for cp in copies(refs[:n], refs[first], refs[first + 1]):
            cp.start()
        refs[first + 2][...] = jnp.zeros_like(refs[first + 2])

    res = pl.pallas_call(
        body, name=name,
        out_shape=[pltpu.HBM(a.shape, a.dtype) for a in arrays]
        + [pltpu.SemaphoreType.DMA((nsem,)), pltpu.SemaphoreType.DMA((nsem,)), jax.ShapeDtypeStruct((8, 128), F32)],
        in_specs=[_HBM] * n + [_ANY] * len(after),
        out_specs=[_HBM] * n + [_SEM, _SEM, pl.BlockSpec(memory_space=pltpu.VMEM)],
        input_output_aliases={i: i for i in range(n)},
        compiler_params=pltpu.CompilerParams(has_side_effects=_EFFECT),
    )(*[_in_hbm(a) for a in arrays], *after)
    return list(res[:n]), res[n], res[n + 1], res[n + 2]


def _copies_wait(arrays, copies, send_sems, recv_sems, after, name):
    n = len(arrays)

    def body(*refs):
        for cp in copies(refs[:n], refs[n], refs[n + 1]):
            cp.wait_send()
            cp.wait_recv()

    res = pl.pallas_call(
        body, name=name, out_shape=[pltpu.HBM(a.shape, a.dtype) for a in arrays],
        in_specs=[_HBM] * n + [_SEM, _SEM] + [_ANY] * len(_several(after)), out_specs=[_HBM] * n,
        input_output_aliases={i: i for i in range(n)},
        compiler_params=pltpu.CompilerParams(has_side_effects=_EFFECT),
    )(*arrays, send_sems, recv_sems, *_several(after))
    return list(res)


def _scatter_copies(refs, send, recv):
    n = len(refs) // 2
    x, y, c = _position()
    chips = [(1 - x, y), (x, 1 - y), (1 - x, 1 - y)]
    return [pltpu.make_async_remote_copy(
        src_ref=refs[i].at[2 * px + py], dst_ref=refs[n + i].at[j],
        send_sem=send.at[3 * i + j], recv_sem=recv.at[3 * i + j], device_id=(px, py, c), device_id_type=MESH)
        for i in range(n) for j, (px, py) in enumerate(chips)]


def _swap_copies(refs, send, recv):
    n = len(refs) // 2
    x, y, c = _position()
    cps = []
    for i in range(n):
        h = refs[i].shape[1] // 2
        cps.append(pltpu.make_async_remote_copy(
            src_ref=refs[i].at[:, pl.ds((1 - c) * h, h), :], dst_ref=refs[n + i],
            send_sem=send.at[i], recv_sem=recv.at[i], device_id=(x, y, 1 - c), device_id_type=MESH))
    return cps


def _join_copies(refs, send, recv):
    x, y, c = _position()
    cps = []
    for i, r in enumerate(refs):
        h = r.shape[0] // 2
        mine = r.at[pl.ds(c * h, h), :]
        cps.append(pltpu.make_async_remote_copy(
            src_ref=mine, dst_ref=mine, send_sem=send.at[i], recv_sem=recv.at[i],
            device_id=(x, y, 1 - c), device_id_type=MESH))
    return cps


def _forward_copies(refs, send, recv):
    x, y, c = _position()
    chips = [(1 - x, y), (x, 1 - y), (1 - x, 1 - y)]
    cps = []
    for i, r in enumerate(refs):
        h = r.shape[1] // 2
        for j, (px, py) in enumerate(chips):
            got = r.at[2 * px + py, pl.ds(c * h, h), :]
            cps.append(pltpu.make_async_remote_copy(
                src_ref=got, dst_ref=got, send_sem=send.at[3 * i + j], recv_sem=recv.at[3 * i + j],
                device_id=(x, y, 1 - c), device_id_type=MESH))
    return cps


def _t5_buckets_block():
    qi = np.arange(BLOCK)[:, None]
    ki = np.arange(2 * BLOCK)[None, :]
    n = np.maximum(qi + BLOCK - ki, 0)
    max_exact = NUM_BUCKETS // 2
    large = max_exact + (np.log(np.maximum(n, 1) / max_exact) / np.log(MAX_DISTANCE / max_exact)
                         * (NUM_BUCKETS - max_exact)).astype(np.int32)
    large = np.minimum(large, NUM_BUCKETS - 1)
    return np.where(n < max_exact, n, large).astype(np.int32)


def _discretise(lambda_re, lambda_im, log_step, b_re, b_im):
    lam_re = jnp.minimum(lambda_re, -1e-4)
    lam_im = lambda_im
    delta = jnp.exp(log_step)[:, None]
    mag = jnp.exp(lam_re * delta)
    ang = lam_im * delta
    abar_re, abar_im = mag * jnp.cos(ang), mag * jnp.sin(ang)
    num_re, num_im = abar_re - 1.0, abar_im
    den = lam_re * lam_re + lam_im * lam_im
    f_re = (num_re * lam_re + num_im * lam_im) / den
    f_im = (num_im * lam_re - num_re * lam_im) / den
    bbar_re = f_re[..., None] * b_re - f_im[..., None] * b_im
    bbar_im = f_re[..., None] * b_im + f_im[..., None] * b_re
    return abar_re, abar_im, bbar_re, bbar_im


def _interleave(v, nc):
    s, w = v.shape
    return v.reshape(nc, s // nc, w).transpose(1, 0, 2).reshape(s, w)


def _deinterleave(v, nc):
    s, w = v.shape
    return v.reshape(s // nc, nc, w).transpose(1, 0, 2).reshape(s, w)


_SMALL = ("norm1_g", "b_in", "attn_sinks", "rel_bias", "lambda_re", "lambda_im", "log_step", "ssm_b_re",
          "ssm_b_im", "ssm_c_re", "ssm_c_im", "ssm_d", "b_glu", "norm2_g", "final_g")


def _pack(parts):
    rows = []
    for p in parts:
        f = p.reshape(-1).astype(F32)
        pad = (-f.shape[0]) % 128
        rows.append(jnp.pad(f, (0, pad)).reshape(-1, 128))
    out = jnp.concatenate(rows, axis=0)
    pad = (-out.shape[0]) % 256
    return jnp.pad(out, ((0, pad), (0, 0)))


def _unpack(packed, shapes):
    res, r = [], 0
    for shp in shapes:
        size = int(np.prod(shp))
        nr = -(-size // 128)
        res.append(packed[r:r + nr].reshape(-1)[:size].reshape(shp))
        r += nr
    return res


def kernel(x, c, w_ada, b_ada, norm1_g, w_in, b_in, attn_sinks, rel_bias, lambda_re, lambda_im, log_step, ssm_b_re, ssm_b_im, ssm_c_re, ssm_c_im, ssm_d, w_glu, b_glu, w_attn_proj, w_ssm_proj, w_out, norm2_g, w_ff1, w_ff2, final_g, loss_target, m_w_ada, m_b_ada, m_norm1_g, m_w_in, m_b_in, m_attn_sinks, m_rel_bias, m_lambda_re, m_lambda_im, m_log_step, m_ssm_b_re, m_ssm_b_im, m_ssm_c_re, m_ssm_c_im, m_ssm_d, m_w_glu, m_b_glu, m_w_attn_proj, m_w_ssm_proj, m_w_out, m_norm2_g, m_w_ff1, m_w_ff2, m_final_g, v_w_ada, v_b_ada, v_norm1_g, v_w_in, v_b_in, v_attn_sinks, v_rel_bias, v_lambda_re, v_lambda_im, v_log_step, v_ssm_b_re, v_ssm_b_im, v_ssm_c_re, v_ssm_c_im, v_ssm_d, v_w_glu, v_b_glu, v_w_attn_proj, v_w_ssm_proj, v_w_out, v_norm2_g, v_w_ff1, v_w_ff2, v_final_g):
    given = dict(locals())
    S, D = x.shape[1], x.shape[2]
    SSM_W = w_glu.shape[2]
    G = SSM_W // SSM_GROUP_CH
    NST = G * SSM_STATE
    DFF = w_ff2.shape[1] * N_CHIPS
    INW = w_in.shape[2] * N_CHIPS
    o_q, o_k, o_v, o_u = 0, ATTN_WIDTH, ATTN_WIDTH + KV_WIDTH, ATTN_WIDTH + 2 * KV_WIDTH
    o_ga, o_gs = o_u + SSM_W, o_u + SSM_W + D
    mx, my, mc = _position()
    my_chip = 2 * mx + my
    my_b = 4 * mx + 2 * my + mc

    xv, tgt = x[0], loss_target[0]

    big = dict(w_in=w_in[0], w_glu=w_glu[0], w_attn_proj=w_attn_proj[0], w_ssm_proj=w_ssm_proj[0],
               w_out=w_out[0], w_ff1=w_ff1[0], w_ff2=w_ff2[0])
    big_names = list(big)
    colsharded = {"w_in", "w_attn_proj", "w_ssm_proj", "w_ff1"}
    chip_sel = my_chip.astype(jnp.int32).reshape(1)
    gather_groups = [["w_in"], ["w_attn_proj", "w_ssm_proj", "w_glu", "w_out"], ["w_ff1", "w_ff2"]]
    in_flight, gather_sems, gathered = {}, [], {}

    def finish_gather(g, after):
        bufs = [in_flight[k] for k in gather_groups[g]]
        bufs = _gather_wait(bufs, gather_sems[g][0], gather_sems[g][1], after, "gather_wait_%d" % g)
        gathered.update(zip(gather_groups[g], _forward_halves(bufs, "gather_forward_%d" % g)))

    def tied(v, token):
        return v + token[0:1, 0:1]

    def all_of(*arrays):
        return list(arrays)

    def wop(k):
        g = gathered[k]
        return _Op(g, N_CHIPS) if k in colsharded else _Op(g.reshape(g.shape[0] * g.shape[1], g.shape[2]))

    grads = {}
    nothing = jnp.zeros((8, 128), F32)
    half = mc.astype(jnp.int32).reshape(1)
    sel = jnp.stack([my_chip, mc]).astype(jnp.int32)

    def rs_swap(tag, named):
        keys, gl = list(named), []
        for k in keys:
            gk = named[k]
            if k not in colsharded:
                gk = gk.reshape(N_CHIPS, gk.shape[0] // N_CHIPS, gk.shape[1])
            gl.append(gk)
        lands = [lax.empty((g.shape[0], g.shape[1] // 2, g.shape[2]), g.dtype) for g in gl]
        arrays, ssem, rsem, token = _copies_start(gl + lands, _swap_copies, len(gl), nothing, "rs_swap_start_" + tag)
        return (keys, arrays, ssem, rsem), token

    def rs_scatter(tag, state, after):
        keys, arrays, ssem, rsem = state
        arrays = _copies_wait(arrays, _swap_copies, ssem, rsem, after, "rs_swap_wait_" + tag)
        n = len(keys)
        ps = [_add_half(g, t, half, "rs_add_" + k) for g, t, k in zip(arrays[:n], arrays[n:], keys)]
        lands = [lax.empty((3,) + p.shape[1:], p.dtype) for p in ps]
        arrays, ssem, rsem, token = _copies_start(ps + lands, _scatter_copies, 3 * n, nothing, "rs_start_" + tag)
        return (keys, arrays, ssem, rsem), token

    def rs_sum(tag, state, after):
        keys, arrays, ssem, rsem = state
        arrays = _copies_wait(arrays, _scatter_copies, ssem, rsem, after, "rs_wait_" + tag)
        n = len(keys)
        rs = [_sum_own(p, t, sel, "rs_sum_" + k) for p, t, k in zip(arrays[:n], arrays[n:], keys)]
        rs, ssem, rsem, token = _copies_start(rs, _join_copies, n, nothing, "rs_join_start_" + tag)
        return (keys, rs, ssem, rsem), token

    def rs_finish(tag, state, after):
        keys, rs, ssem, rsem = state
        for k, f in zip(keys, _copies_wait(rs, _join_copies, ssem, rsem, after, "rs_join_wait_" + tag)):
            grads[k] = f[None]

    c_all = _allgather8(jnp.pad(c, ((0, 7), (0, 0))), "gather_c").reshape(N_DEV, 8, D)[:, 0]
    c16 = jnp.pad(c_all, ((0, 8), (0, 0)))
    b_ada_mine = lax.dynamic_slice(b_ada.reshape(N_CHIPS, -1), (my_chip, 0), (1, w_ada.shape[2]))
    mod_sh = _mm(c16, w_ada[0], "NN", name="mod", M=16, N=w_ada.shape[2], K=D, a_fn=_silu,
                 epilogue=lambda acc, b: (acc + b,), extras=[(b_ada_mine, "row")])
    mod_all = _allgather8(mod_sh[:8], "gather_mod").reshape(N_DEV, 8, -1)
    mod_row = jnp.concatenate(
        [lax.dynamic_slice(mod_all, (2 * j, my_b, 0), (1, 1, mod_all.shape[2]))[0] for j in range(N_CHIPS)], axis=1)
    sh1, sc1, g1, sh2, sc2, g2 = [mod_row[:, i * D:(i + 1) * D] for i in range(6)]

    first = [_cast_into_slot(big["w_in"], chip_sel, "cast_w_in")]
    first, sems_first, token_first = _gather_start(first, [[0]], mod_all, "gather_start_in")
    rest_names = gather_groups[1] + gather_groups[2]
    rest = [_cast_into_slot(big[k], chip_sel, "cast_" + k) for k in rest_names]
    rest, sems_rest, token_rest = _gather_start(
        rest, [[rest_names.index(k) for k in grp] for grp in gather_groups[1:]], token_first, "gather_start_rest")
    in_flight.update(zip(["w_in"] + rest_names, first + rest))
    gather_sems.extend(sems_first + sems_rest)

    disc_in = (lambda_re[0], lambda_im[0], log_step[0], ssm_b_re[0], ssm_b_im[0])
    (abar_re, abar_im, bbar_re, bbar_im), disc_vjp = jax.vjp(_discretise, *disc_in)
    same_group = jnp.asarray(np.arange(SSM_W)[:, None] // SSM_GROUP_CH == np.arange(NST)[None, :] // SSM_STATE)

    def block_diag(t):
        return jnp.where(same_group, jnp.tile(t, (G, 1)), 0.0)

    bd = jnp.concatenate([block_diag(bb.transpose(2, 0, 1).reshape(SSM_GROUP_CH, NST)) for bb in (bbar_re, bbar_im)],
                         axis=1)
    cd = jnp.concatenate([block_diag(cc.transpose(1, 0, 2).reshape(SSM_GROUP_CH, NST)).T
                          for cc in (ssm_c_re[0], -ssm_c_im[0])], axis=0)
    a_fwd = jnp.stack([abar_re.reshape(1, NST), abar_im.reshape(1, NST)])
    a_bwd = jnp.stack([abar_re.reshape(1, NST), -abar_im.reshape(1, NST)])
    d_row = ssm_d

    buckets = _t5_buckets_block()
    onehot_t = (jnp.arange(128, dtype=jnp.int32)[:, None] == jnp.asarray(buckets.reshape(1, -1))).astype(BF16)
    rb_hi = rel_bias.astype(BF16)
    rb_lo = (rel_bias - rb_hi.astype(F32)).astype(BF16)
    rb_lo2 = (rel_bias - rb_hi.astype(F32) - rb_lo.astype(F32)).astype(BF16)
    rb3 = jnp.pad(jnp.concatenate([rb_hi.T, rb_lo.T, rb_lo2.T], axis=0), ((0, 0), (0, 128 - NUM_BUCKETS)))
    b3 = _mm(rb3, onehot_t, "NN", name="rel_bias_rows", M=3 * N_Q_HEADS, N=BLOCK * 2 * BLOCK, K=128, tj=4096)
    bias = (b3[:N_Q_HEADS] + b3[N_Q_HEADS:2 * N_Q_HEADS]) + b3[2 * N_Q_HEADS:]
    bias = bias.reshape(N_Q_HEADS, BLOCK, 2 * BLOCK)
    sinks_b = jnp.broadcast_to(attn_sinks[0][:, None, None], (N_Q_HEADS, BLOCK, 128)).reshape(N_Q_HEADS * BLOCK, 128)

    def two(fn):
        def both(*blocks):
            r = fn(*blocks)
            return r, r
        return both

    h1, h1_t = _rowwise(two(_norm_mod), [(xv, "tile", D), (tied(tied(norm1_g, token_first), token_rest), "row", D),
                                         (sh1, "row", D), (sc1, "row", D)],
                        [(D, BF16), (D, BF16, "T")], [], name="norm1", rows=S)
    finish_gather(0, all_of(h1, bd, cd, a_fwd, a_bwd, bias, sinks_b))
    proj = _mm(h1, wop("w_in"), "NN", name="proj", M=S, N=INW, K=D, out_dtypes=(BF16,),
               epilogue=lambda acc, b: (acc + b,), extras=[(b_in, "row")])

    def heads(v2d, nh):
        return v2d.reshape(S, nh, HEAD_DIM).transpose(1, 0, 2)

    def unheads(v3d):
        return v3d.transpose(1, 0, 2).reshape(S, -1)

    qh = heads(proj[:, o_q:o_k], N_Q_HEADS)
    kh = heads(proj[:, o_k:o_v], N_KV_HEADS)
    vh = heads(proj[:, o_v:o_u], N_KV_HEADS)
    attn = unheads(_attn_fwd(qh, kh, vh, sinks_b, bias, "attn_fwd"))
    finish_gather(1, attn)
    y_attn = _mm(attn, wop("w_attn_proj"), "NN", name="attn_proj", M=S, N=D, K=ATTN_WIDTH, out_dtypes=(BF16,))

    u = proj[:, o_u:o_ga]
    u_il = _interleave(u, SCAN_CHUNKS)
    SB = 128
    nsb, gpb = SSM_W // SB, SB // SSM_GROUP_CH
    SBN = gpb * SSM_STATE
    y_il, xs = _ssm_fwd(u_il, bd, cd, a_fwd, d_row, name="ssm_fwd", sb=SB, sbn=SBN)
    y = _deinterleave(y_il, SCAN_CHUNKS)
    z, t_glu = _mm(y, wop("w_glu"), "NN", name="glu", M=S, N=SSM_W, K=SSM_W, out_dtypes=(BF16, F32), a_fn=_gelu,
                   epilogue=lambda acc, b, yy: (_gelu(yy) * _sigmoid(acc + b), acc + b),
                   extras=[(b_glu, "row"), (y, "tile")])
    y_ssm = _mm(z, wop("w_ssm_proj"), "NN", name="ssm_proj", M=S, N=D, K=SSM_W, out_dtypes=(BF16,))

    ff_bufs = _gather_wait([in_flight[k] for k in gather_groups[2]], gather_sems[2][0], gather_sems[2][1], all_of(y_ssm),
                           "gather_wait_2")
    ff_bufs, ff_send, ff_recv, token = _copies_start(ff_bufs, _forward_copies, 3 * len(ff_bufs), nothing,
                                                    "gather_forward_2_start")
    merged, merged_t = _rowwise(two(_merge), [(_Op(proj, coff=o_ga), "tile", D), (_Op(proj, coff=o_gs), "tile", D),
                                              (y_attn, "tile", D), (y_ssm, "tile", D)],
                                [(D, BF16), (D, BF16, "T")], [], name="merge", rows=S)
    mo, x2 = _mm(merged, wop("w_out"), "NN", name="out_proj", M=S, N=D, K=D, out_dtypes=(BF16, F32),
                 epilogue=lambda acc, xx, gg: (acc, xx + gg * acc), extras=[(xv, "tile"), (g1, "row")], deps=[token])
    h2, h2_t = _rowwise(two(_norm_mod), [(x2, "tile", D), (norm2_g, "row", D), (sh2, "row", D), (sc2, "row", D)],
                        [(D, BF16), (D, BF16, "T")], [], name="norm2", rows=S)
    gathered.update(zip(gather_groups[2], _copies_wait(ff_bufs, _forward_copies, ff_send, ff_recv, h2,
                                                       "gather_forward_2_wait")))
    a_b, r_b = _mm(h2, wop("w_ff1"), "NN", name="ff1", M=S, N=DFF, K=D, out_dtypes=(BF16, BF16),
                   epilogue=lambda acc: (acc, jnp.square(jnp.maximum(acc, 0.0))))
    ff, x3 = _mm(r_b, wop("w_ff2"), "NN", name="ff2", M=S, N=D, K=DFF, out_dtypes=(BF16, F32),
                 epilogue=lambda acc, xx, gg: (acc, xx + gg * acc), extras=[(x2, "tile"), (g2, "row")],
                 tj=1024, tk=1024)

    def final_fn(x3b, gf, tb, ffb, g2b):
        def f(xx, gg):
            yv = xx * lax.rsqrt(jnp.mean(xx * xx, axis=-1, keepdims=True) + EPS) * gg
            err = jnp.square(yv - tb)
            return 0.5 * jnp.sum(jnp.mean(err, axis=-1, keepdims=True), axis=0, keepdims=True)
        lv, vjp = jax.vjp(f, x3b, gf)
        dx, dg = vjp(jnp.ones((1, 1), F32))
        return dx, dx * g2b, dg, jnp.broadcast_to(lv, (1, 128)), jnp.sum(dx * ffb, axis=0, keepdims=True)

    dx3, dff, g_final, loss_acc, d_g2 = _rowwise(
        final_fn, [(x3, "tile", D), (final_g.reshape(1, D), "row", D), (tgt, "tile", D), (ff, "tile", D), (g2, "row", D)],
        [(D, F32), (D, BF16)], [D, 128, D], name="final", rows=S)
    da = _mm(dff, wop("w_ff2"), "NT", name="ff2_dx", M=S, N=DFF, K=D, out_dtypes=(BF16,),
             epilogue=lambda acc, ab: (acc * (2.0 * jnp.maximum(ab.astype(F32), 0.0)),), extras=[(a_b, "tile")])
    g_w_ff2 = _mm(r_b, dff, "TN", name="ff2_dw", M=DFF, N=D, K=S, out_dtypes=(BF16,), tj=1024, tk=1024)
    g_w_ff1 = _mm(h2_t, da, "NN", name="ff1_dw", M=D, N=DFF, K=S, out_dtypes=(BF16,), out_nsh=N_CHIPS, tj=1024, tk=1024)
    rs_ff, token = rs_swap("ff", dict(w_ff2=g_w_ff2, w_ff1=g_w_ff1))
    dh2 = _mm(da, wop("w_ff1"), "NT", name="ff1_dx", M=S, N=D, K=DFF, tj=1024, tk=1024, deps=[token])
    rs_ff, token_ff = rs_scatter("ff", rs_ff, dh2)

    def norm2_bwd(x2b, dh2b, dx3b, mob, gn, shb, scb, g1b):
        _, vjp = jax.vjp(_norm_mod, x2b, gn, shb, scb)
        dx, dg, dsh, dsc = vjp(dh2b)
        dx2b = dx + dx3b
        return dx2b, dx2b * g1b, dg, dsh, dsc, jnp.sum(dx2b * mob, axis=0, keepdims=True)

    dx2, dmo, g_norm2, d_sh2, d_sc2, d_g1 = _rowwise(
        norm2_bwd, [(x2, "tile", D), (dh2, "tile", D), (dx3, "tile", D), (mo, "tile", D),
                    (tied(norm2_g, token_ff), "row", D), (sh2, "row", D), (sc2, "row", D), (g1, "row", D)],
        [(D, F32), (D, BF16)], [D, D, D, D], name="norm2_bwd", rows=S)
    dmerged = _mm(dmo, wop("w_out"), "NT", name="out_dx", M=S, N=D, K=D)
    g_w_out = _mm(merged_t, dmo, "NN", name="out_dw", M=D, N=D, K=S, out_dtypes=(BF16,), tj=1024, tk=1024)

    def merge_bwd(gab, gsb, yab, ysb, dmb):
        _, vjp = jax.vjp(_merge, gab, gsb, yab, ysb)
        return vjp(dmb)

    d_ga, d_gs, dy_attn, dy_ssm = _rowwise(
        merge_bwd, [(_Op(proj, coff=o_ga), "tile", D), (_Op(proj, coff=o_gs), "tile", D), (y_attn, "tile", D),
                    (y_ssm, "tile", D), (dmerged, "tile", D)],
        [(D, BF16), (D, BF16), (D, BF16), (D, BF16)], [], name="merge_bwd", rows=S)

    dattn = _mm(dy_attn, wop("w_attn_proj"), "NT", name="attn_proj_dx", M=S, N=ATTN_WIDTH, K=D, tj=1024, out_dtypes=(BF16,))
    g_w_attn_proj = _mm(attn, dy_attn, "TN", name="attn_proj_dw", M=ATTN_WIDTH, N=D, K=S, out_dtypes=(BF16,),
                        out_nsh=N_CHIPS, tk=1024)

    dz = _mm(dy_ssm, wop("w_ssm_proj"), "NT", name="ssm_proj_dx", M=S, N=SSM_W, K=D)
    g_w_ssm_proj = _mm(z, dy_ssm, "TN", name="ssm_proj_dw", M=SSM_W, N=D, K=S, out_dtypes=(BF16,),
                       out_nsh=N_CHIPS, tk=1024)

    def glu_bwd(dzb, yb, tb):
        z0 = _gelu(yb)
        sg = _sigmoid(tb)
        dt = dzb * z0 * sg * (1.0 - sg)
        return dt, dzb * sg, jnp.sum(dt, axis=0, keepdims=True)

    dt_b, dz0a, g_b_glu = _rowwise(glu_bwd, [(dz, "tile", SSM_W), (y, "tile", SSM_W), (t_glu, "tile", SSM_W)],
                                   [(SSM_W, BF16), (SSM_W, F32)], [SSM_W], name="glu_bwd", rows=S)

    def gelu_bwd(acc, dz0ab, yb):
        _, vjp = jax.vjp(_gelu, yb)
        return (vjp(acc + dz0ab)[0],)

    dy = _mm(dt_b, wop("w_glu"), "NT", name="glu_dx", M=S, N=SSM_W, K=SSM_W, epilogue=gelu_bwd,
             extras=[(dz0a, "tile"), (y, "tile")])
    g_w_glu = _mm(y, dt_b, "TN", name="glu_dw", M=SSM_W, N=SSM_W, K=S, out_dtypes=(BF16,), tk=1024, a_fn=_gelu)
    rs_mix, token = rs_swap("mix", dict(w_out=g_w_out, w_attn_proj=g_w_attn_proj, w_ssm_proj=g_w_ssm_proj,
                                        w_glu=g_w_glu))
    dy_il = _interleave(dy, SCAN_CHUNKS)
    du_il, g_bd, g_cd, d_abar, g_ssm_d = _ssm_bwd(dy_il, u_il, xs, bd, cd, a_bwd, d_row, name="ssm_bwd", sb=SB, sbn=SBN,
                                                  deps=[token])
    du = _deinterleave(du_il, SCAN_CHUNKS)
    rs_mix, token_mix = rs_scatter("mix", rs_mix, du_il)

    dqh, dkh, dvh, dsink_blk, dbias = _attn_bwd(qh, kh, vh, heads(dattn, N_Q_HEADS), tied(sinks_b, token_mix), bias,
                                                "attn_bwd")
    g_sinks = _sum_lead(dsink_blk.reshape(N_Q_HEADS, BLOCK, 128).transpose(1, 0, 2), "sinks_dw")[:, 0].reshape(1, N_Q_HEADS)
    g_rel = _mm(dbias.reshape(N_Q_HEADS, -1), onehot_t, "NT", name="rel_bias_dw", M=N_Q_HEADS, N=128,
                K=BLOCK * 2 * BLOCK, tk=4096)
    g_rel_bias = g_rel[:, :NUM_BUCKETS].T

    eye_b = jnp.eye(gpb, dtype=F32)
    g_cd6 = g_cd.reshape(2, nsb, gpb, SSM_STATE, gpb, SSM_GROUP_CH)
    g_c_re = jnp.einsum("bgnhp,gh->bgpn", g_cd6[0], eye_b).reshape(G, SSM_GROUP_CH, SSM_STATE)
    g_c_im = -jnp.einsum("bgnhp,gh->bgpn", g_cd6[1], eye_b).reshape(G, SSM_GROUP_CH, SSM_STATE)
    g_bd6 = g_bd.reshape(nsb, gpb, SSM_GROUP_CH, 2, gpb, SSM_STATE)
    g_bbar = jnp.einsum("bhprgn,hg->rbhnp", g_bd6, eye_b).reshape(2, G, SSM_STATE, SSM_GROUP_CH)
    g_bbar_re, g_bbar_im = g_bbar[0], g_bbar[1]
    g_lre, g_lim, g_lstep, g_bre, g_bim = disc_vjp(
        (d_abar[0].reshape(G, SSM_STATE), d_abar[1].reshape(G, SSM_STATE), g_bbar_re, g_bbar_im))

    dproj = jnp.concatenate([unheads(dqh).astype(BF16), unheads(dkh).astype(BF16), unheads(dvh).astype(BF16),
                             du.astype(BF16), d_ga, d_gs], axis=1)
    g_w_in = _mm(h1_t, dproj, "NN", name="proj_dw", M=D, N=INW, K=S, out_dtypes=(BF16,), out_nsh=N_CHIPS,
                 tj=INW // (2 * N_CHIPS), tk=1024)
    rs_in, token = rs_swap("in", dict(w_in=g_w_in))
    dh1 = _mm(dproj, wop("w_in"), "NT", name="proj_dx", M=S, N=D, K=INW, tj=1024, tk=INW // N_CHIPS, deps=[token])
    g_b_in = _rowwise(lambda d: (jnp.sum(d.astype(F32), axis=0, keepdims=True),), [(dproj, "tile", INW)], [], [INW],
                      name="proj_db", rows=S)[0]

    def norm1_bwd(xb, dhb, dresb, gn, shb, scb):
        _, vjp = jax.vjp(_norm_mod, xb, gn, shb, scb)
        dx, dg, dsh, dsc = vjp(dhb)
        return dx + dresb, dg, dsh, dsc

    grad_x, g_norm1, d_sh1, d_sc1 = _rowwise(
        norm1_bwd, [(xv, "tile", D), (dh1, "tile", D), (dx2, "tile", D), (norm1_g, "row", D),
                    (sh1, "row", D),
                    (sc1, "row", D)], [(D, F32)], [D, D, D], name="norm1_bwd", rows=S)

    dmod_row = jnp.concatenate([d_sh1, d_sc1, d_g1, d_sh2, d_sc2, d_g2], axis=1)
    small_g = dict(norm1_g=g_norm1, b_in=g_b_in, attn_sinks=g_sinks, rel_bias=g_rel_bias, lambda_re=g_lre[None],
                   lambda_im=g_lim[None], log_step=g_lstep[None], ssm_b_re=g_bre[None], ssm_b_im=g_bim[None],
                   ssm_c_re=g_c_re[None], ssm_c_im=g_c_im[None], ssm_d=g_ssm_d, b_glu=g_b_glu, norm2_g=g_norm2,
                   final_g=g_final.reshape(D))
    packed = _pack([dmod_row, loss_acc[:, :1]] + [small_g[k] for k in _SMALL])
    rows = packed.shape[0]
    gathered = _allgather8(packed, "gather_small").reshape(N_DEV, rows, 128)
    summed = _sum_lead(gathered, "small_sum")
    parts = _unpack(summed, [dmod_row.shape, (1,)] + [given[k].shape for k in _SMALL])
    loss = parts[1].reshape(())
    grads.update(zip(_SMALL, parts[2:]))
    grads["b_ada"] = parts[0]

    dmod_all = gathered[:, :dmod_row.shape[1] // 128].reshape(N_DEV, -1)
    dmod_mine = lax.dynamic_slice(dmod_all.reshape(N_DEV, N_CHIPS, -1), (0, my_chip, 0), (N_DEV, 1, w_ada.shape[2]))[:, 0]
    g_w_ada = _mm(c16, jnp.pad(dmod_mine, ((0, 8), (0, 0))), "TN", name="ada_dw", M=D, N=w_ada.shape[2], K=16,
                  a_fn=_silu)
    grads["w_ada"] = g_w_ada[None]

    deltas, new_m, new_v = {}, {}, {}

    def adamw_big(k, deps=()):
        echo = k in big_names
        res = _adamw(given[k][0], grads[k][0], given["m_" + k][0], given["v_" + k][0], "adamw_" + k, deps, echo)
        deltas[k], new_m[k], new_v[k] = res[0][None], res[1][None], res[2][None]
        if echo:
            grads[k] = res[3][None]
        return res[2]

    rs_in, token_in = rs_scatter("in", rs_in, all_of(summed, dmod_all))
    rs_ff, token = rs_sum("ff", rs_ff, all_of(summed, token_in))
    mark = adamw_big("w_ada", [token])
    rs_mix, token = rs_sum("mix", rs_mix, mark)
    small_all = list(_SMALL) + ["b_ada"]
    for k in small_all:
        grads[k] = grads[k].reshape(given[k].shape)

    def rows_of(a):
        return a.reshape(1, -1) if a.ndim == 1 else a

    d_, m_, v_ = _adamw_many(*[[rows_of(src[k]) for k in small_all] for src in (
        given, grads, {k: given["m_" + k] for k in small_all}, {k: given["v_" + k] for k in small_all})],
        "adamw_small", [token])
    for k, dd, mm, vv in zip(small_all, d_, m_, v_):
        deltas[k], new_m[k], new_v[k] = (t.reshape(given[k].shape) for t in (dd, mm, vv))
    v_ = v_[0]
    rs_finish("ff", rs_ff, v_)
    marks = [adamw_big(k) for k in ("w_ff2", "w_ff1")]
    rs_finish("mix", rs_mix, all_of(*marks))
    marks = [adamw_big(k) for k in ("w_out", "w_attn_proj", "w_ssm_proj", "w_glu")]
    rs_in, token = rs_sum("in", rs_in, all_of(*marks))
    rs_finish("in", rs_in, token)
    adamw_big("w_in")

    names = ["w_ada", "b_ada", "norm1_g", "w_in", "b_in", "attn_sinks", "rel_bias", "lambda_re", "lambda_im",
             "log_step", "ssm_b_re", "ssm_b_im", "ssm_c_re", "ssm_c_im", "ssm_d", "w_glu", "b_glu", "w_attn_proj",
             "w_ssm_proj", "w_out", "norm2_g", "w_ff1", "w_ff2", "final_g"]
    return (loss, grad_x[None], *[grads[n] for n in names], *[deltas[n] for n in names],
            *[new_m[n] for n in names], *[new_v[n] for n in names])
```

```python
import math

import numpy as np
import jax
import jax.numpy as jnp
from jax import lax
from jax.experimental import pallas as pl
from jax.experimental.pallas import tpu as pltpu

F32 = jnp.float32
BF16 = jnp.bfloat16
MESH = pl.DeviceIdType.MESH

HEAD_DIM = 64
N_Q_HEADS = 16
N_KV_HEADS = 4
GQA_GROUP = N_Q_HEADS // N_KV_HEADS
ATTN_WIDTH = N_Q_HEADS * HEAD_DIM
KV_WIDTH = N_KV_HEADS * HEAD_DIM
BLOCK = 128
NUM_BUCKETS = 32
MAX_DISTANCE = 128
NEG_INF = -1e30
SSM_GROUP_CH = 16
SSM_STATE = 64
EPS = 1e-6
ADAM_LR = 0.001
ADAM_B1 = 0.9
ADAM_B2 = 0.999
ADAM_EPS = 1e-08
ADAM_WD = 0.01
ADAM_STEP = 10

N_CHIPS = 4
N_DEV = 8
SCAN_CHUNKS = 8
VMEM_LIMIT_BYTES = 48 * 1024 * 1024
SSM_VMEM_LIMIT_BYTES = 56 * 1024 * 1024


def _cparams(sem=None):
    return pltpu.CompilerParams(dimension_semantics=sem, vmem_limit_bytes=VMEM_LIMIT_BYTES)


class _Op:
    def __init__(self, arr, nsh=None, coff=0):
        self.arr, self.nsh, self.coff = arr, nsh, coff
        if nsh is None:
            self.rows, self.cols = arr.shape
        else:
            assert arr.shape[0] == nsh
            self.rows, self.cols = arr.shape[1], arr.shape[2] * nsh

    def spec(self, br, bc, idx):
        assert self.coff % bc == 0
        off = self.coff // bc
        if self.nsh is None:
            return pl.BlockSpec((br, bc), lambda *g: (idx(*g)[0], idx(*g)[1] + off))
        per = (self.cols // self.nsh) // bc
        assert per * bc * self.nsh == self.cols

        def imap(*g):
            r, c = idx(*g)
            c = c + off
            return (c // per, r, c % per)
        return pl.BlockSpec((None, br, bc), imap)


def _as_op(a):
    return a if isinstance(a, _Op) else _Op(a)


def _mm(a, b, mode, *, name, M, N, K, out_dtypes=(F32,), out_nsh=None, epilogue=None, extras=(),
        a_fn=None, ti=1024, tj=512, tk=2048, deps=()):
    nd = len(deps)
    a, b = _as_op(a), _as_op(b)
    ti, tj, tk = min(ti, M), min(tj, N), min(tk, K)
    a_w = a.cols // a.nsh if a.nsh else None
    b_w = b.cols // b.nsh if b.nsh else None
    if a_w:
        ti, tk = (min(ti, a_w), tk) if mode == "TN" else (ti, min(tk, a_w))
    if b_w:
        tj, tk = (tj, min(tk, b_w)) if mode == "NT" else (min(tj, b_w), tk)
    if out_nsh:
        tj = min(tj, N // out_nsh)
    assert M % ti == 0 and N % tj == 0 and K % tk == 0, (name, M, N, K, ti, tj, tk)
    nk = K // tk
    if mode == "NN":
        a_spec = a.spec(ti, tk, lambda i, j, k: (i, k))
        b_spec = b.spec(tk, tj, lambda i, j, k: (k, j))
        dims = (((1,), (0,)), ((), ()))
    elif mode == "NT":
        a_spec = a.spec(ti, tk, lambda i, j, k: (i, k))
        b_spec = b.spec(tj, tk, lambda i, j, k: (j, k))
        dims = (((1,), (1,)), ((), ()))
    else:
        a_spec = a.spec(tk, ti, lambda i, j, k: (k, i))
        b_spec = b.spec(tk, tj, lambda i, j, k: (k, j))
        dims = (((0,), (0,)), ((), ()))
    ex_specs, ex_arrs = [], []
    for op, kind in extras:
        op = _as_op(op)
        if kind == "tile":
            ex_specs.append(op.spec(ti, tj, lambda i, j, k: (i, j)))
        else:
            ex_specs.append(op.spec(1, tj, lambda i, j, k: (0, j)))
        ex_arrs.append(op.arr)
    ne, no = len(ex_arrs), len(out_dtypes)
    if out_nsh is None:
        out_shapes = [jax.ShapeDtypeStruct((M, N), d) for d in out_dtypes]
        out_specs = [pl.BlockSpec((ti, tj), lambda i, j, k: (i, j)) for _ in out_dtypes]
    else:
        per = (N // out_nsh) // tj
        assert per * tj * out_nsh == N
        out_shapes = [jax.ShapeDtypeStruct((out_nsh, M, N // out_nsh), d) for d in out_dtypes]
        out_specs = [pl.BlockSpec((None, ti, tj), lambda i, j, k: (j // per, i, j % per)) for _ in out_dtypes]

    def body(a_ref, b_ref, *rest):
        ex_refs, out_refs, acc = rest[:ne], rest[ne + nd:ne + nd + no], rest[ne + nd + no]
        k = pl.program_id(2)

        @pl.when(k == 0)
        def _():
            acc[...] = jnp.zeros_like(acc)

        av = a_ref[...]
        if a_fn is not None:
            av = a_fn(av)
        acc[...] += lax.dot_general(av.astype(BF16), b_ref[...].astype(BF16), dims,
                                    preferred_element_type=F32)

        @pl.when(k == nk - 1)
        def _():
            res = acc[...]
            outs = epilogue(res, *[r[...] for r in ex_refs]) if epilogue is not None else (res,)
            for o_ref, o in zip(out_refs, outs):
                o_ref[...] = o.astype(o_ref.dtype)

    outs = pl.pallas_call(
        body, name=name, grid=(M // ti, N // tj, nk),
        in_specs=[a_spec, b_spec] + ex_specs + [pl.BlockSpec(memory_space=pl.ANY)] * nd,
        out_specs=out_specs, out_shape=out_shapes,
        scratch_shapes=[pltpu.VMEM((ti, tj), F32)],
        compiler_params=_cparams(("parallel", "parallel", "arbitrary")),
    )(a.arr, b.arr, *ex_arrs, *deps)
    return outs[0] if no == 1 else outs


def _rowwise(fn, ins, outs, accs, *, name, rows, tr=256, deps=()):
    tr = min(tr, rows)
    assert rows % tr == 0
    in_specs, arrs = [], []
    for op, kind, width in ins:
        op = _as_op(op)
        if kind == "tile":
            in_specs.append(op.spec(tr, width, lambda i: (i, 0)))
        else:
            in_specs.append(op.spec(op.rows, width, lambda i: (0, 0)))
        arrs.append(op.arr)
    ni, no, na = len(ins), len(outs), len(accs)
    flipped = [len(o) == 3 for o in outs]
    out_shapes = [jax.ShapeDtypeStruct((o[0], rows) if t else (rows, o[0]), o[1]) for o, t in zip(outs, flipped)]
    out_specs = [pl.BlockSpec((o[0], tr), lambda i: (0, i)) if t else pl.BlockSpec((tr, o[0]), lambda i: (i, 0))
                 for o, t in zip(outs, flipped)]
    out_shapes += [jax.ShapeDtypeStruct((1, w), F32) for w in accs]
    out_specs += [pl.BlockSpec((1, w), lambda i: (0, 0)) for w in accs]

    def body(*refs):
        nd = len(deps)
        in_refs, out_refs, acc_refs = refs[:ni], refs[ni + nd:ni + nd + no], refs[ni + nd + no:]
        res = fn(*[r[...] for r in in_refs])
        if not isinstance(res, (tuple, list)):
            res = (res,)
        for o_ref, r, t in zip(out_refs, res[:no], flipped):
            o_ref[...] = (r.astype(F32).T if t else r).astype(o_ref.dtype)
        if na:
            @pl.when(pl.program_id(0) == 0)
            def _():
                for a_ref in acc_refs:
                    a_ref[...] = jnp.zeros_like(a_ref)
            for a_ref, r in zip(acc_refs, res[no:]):
                a_ref[...] += r.astype(F32)

    res = pl.pallas_call(
        body, name=name, grid=(rows // tr,), in_specs=in_specs + [pl.BlockSpec(memory_space=pl.ANY)] * len(deps),
        out_specs=out_specs, out_shape=out_shapes, compiler_params=_cparams(("arbitrary",)),
    )(*arrs, *deps)
    return res


def _norm_mod(x, g, sh, sc):
    y = x * lax.rsqrt(jnp.mean(x * x, axis=-1, keepdims=True) + EPS) * g
    return y * (1.0 + sc) + sh


def _sigmoid(x):
    return 1.0 / (1.0 + jnp.exp(-x))


def _silu(x):
    return x * _sigmoid(x)


def _gelu(x):
    return 0.5 * x * (1.0 + jnp.tanh(math.sqrt(2.0 / math.pi) * (x + 0.044715 * (x * x * x))))


def _merge(ga, gs, ya, ys):
    ga, gs, ya, ys = (v.astype(F32) for v in (ga, gs, ya, ys))
    return _sigmoid(ga) * ya + _sigmoid(gs) * ys


def _attn_head(q, kp, kc, vp, vc, sink, bias_p, bias_c, not_first):
    nt = (((1,), (1,)), ((), ()))
    nn = (((1,), (0,)), ((), ()))
    qb = q.astype(BF16)
    scale = HEAD_DIM ** -0.5
    sp = lax.dot_general(qb, kp.astype(BF16), nt, preferred_element_type=F32) * scale + bias_p
    sc = lax.dot_general(qb, kc.astype(BF16), nt, preferred_element_type=F32) * scale + bias_c
    qi = lax.broadcasted_iota(jnp.int32, sp.shape, 0) & (BLOCK - 1)
    ki = lax.broadcasted_iota(jnp.int32, sp.shape, 1)
    sp = jnp.where(jnp.logical_and(ki > qi, not_first), sp, NEG_INF)
    sc = jnp.where(ki <= qi, sc, NEG_INF)
    m = jnp.maximum(jnp.maximum(jnp.max(sp, axis=-1, keepdims=True), jnp.max(sc, axis=-1, keepdims=True)), sink)
    m = lax.stop_gradient(m)
    pp = jnp.exp(sp - m)
    pc = jnp.exp(sc - m)
    denom = jnp.sum(pp, axis=-1, keepdims=True) + jnp.sum(pc, axis=-1, keepdims=True) + jnp.exp(sink - m)
    o = lax.dot_general((pp / denom).astype(BF16), vp.astype(BF16), nn, preferred_element_type=F32)
    o = o + lax.dot_general((pc / denom).astype(BF16), vc.astype(BF16), nn, preferred_element_type=F32)
    return o


def _attn_fwd(qh, kh, vh, sinks, bias, name):
    s = qh.shape[1]
    nb = s // BLOCK
    G = GQA_GROUP
    R = G * BLOCK

    def body(q_ref, kp_ref, kc_ref, vp_ref, vc_ref, sink_ref, bias_ref, o_ref):
        not_first = pl.program_id(0) > 0
        for kv in range(N_KV_HEADS):
            hs = slice(kv * G, (kv + 1) * G)
            o = _attn_head(q_ref[hs].reshape(R, HEAD_DIM), kp_ref[kv], kc_ref[kv], vp_ref[kv], vc_ref[kv],
                           sink_ref[kv * R:(kv + 1) * R, 0:1],
                           bias_ref[hs, :, 0:BLOCK].reshape(R, BLOCK), bias_ref[hs, :, BLOCK:2 * BLOCK].reshape(R, BLOCK),
                           not_first)
            o_ref[hs] = o.reshape(G, BLOCK, HEAD_DIM).astype(o_ref.dtype)

    cur = lambda i: (0, i, 0)
    prev = lambda i: (0, jnp.maximum(i - 1, 0), 0)
    return pl.pallas_call(
        body, name=name, grid=(nb,),
        in_specs=[pl.BlockSpec((N_Q_HEADS, BLOCK, HEAD_DIM), cur),
                  pl.BlockSpec((N_KV_HEADS, BLOCK, HEAD_DIM), prev), pl.BlockSpec((N_KV_HEADS, BLOCK, HEAD_DIM), cur),
                  pl.BlockSpec((N_KV_HEADS, BLOCK, HEAD_DIM), prev), pl.BlockSpec((N_KV_HEADS, BLOCK, HEAD_DIM), cur),
                  pl.BlockSpec((N_Q_HEADS * BLOCK, 128), lambda i: (0, 0)),
                  pl.BlockSpec((N_Q_HEADS, BLOCK, 2 * BLOCK), lambda i: (0, 0, 0))],
        out_specs=pl.BlockSpec((N_Q_HEADS, BLOCK, HEAD_DIM), cur),
        out_shape=jax.ShapeDtypeStruct((N_Q_HEADS, s, HEAD_DIM), BF16),
        compiler_params=_cparams(("arbitrary",)),
    )(qh, kh, kh, vh, vh, sinks, bias)


def _attn_bwd(qh, kh, vh, doh, sinks, bias, name):
    s = qh.shape[1]
    nb = s // BLOCK
    G = GQA_GROUP
    R = G * BLOCK

    def body(q_ref, kp_ref, kc_ref, vp_ref, vc_ref, do_ref, sink_ref, bias_ref,
             dq_ref, dk_ref, dv_ref, dsink_ref, dbias_ref, ck, cv):
        i = pl.program_id(1)

        @pl.when(i == 0)
        def _():
            dsink_ref[...] = jnp.zeros_like(dsink_ref)
            dbias_ref[...] = jnp.zeros_like(dbias_ref)
            ck[...] = jnp.zeros_like(ck)
            cv[...] = jnp.zeros_like(cv)

        @pl.when(i < nb)
        def _():
            not_first = i > 0
            _, vjp = jax.vjp(lambda q, a, b, c, d, sk, e, f: _attn_head(q, a, b, c, d, sk, e, f, not_first),
                             q_ref[...].astype(F32).reshape(R, HEAD_DIM), kp_ref[...].astype(F32),
                             kc_ref[...].astype(F32), vp_ref[...].astype(F32), vc_ref[...].astype(F32),
                             sink_ref[:, 0:1], bias_ref[:, :, 0:BLOCK].reshape(R, BLOCK),
                             bias_ref[:, :, BLOCK:2 * BLOCK].reshape(R, BLOCK))
            dq, dkp, dkc, dvp, dvc, dsk, dbp, dbc = vjp(do_ref[...].reshape(R, HEAD_DIM).astype(F32))
            dq_ref[...] = dq.reshape(G, BLOCK, HEAD_DIM).astype(dq_ref.dtype)
            dsink_ref[...] += jnp.broadcast_to(dsk, (R, 128))
            dbias_ref[:, :, 0:BLOCK] += dbp.reshape(G, BLOCK, BLOCK)
            dbias_ref[:, :, BLOCK:2 * BLOCK] += dbc.reshape(G, BLOCK, BLOCK)
            dk_ref[...] = (ck[...] + dkp).astype(dk_ref.dtype)
            dv_ref[...] = (cv[...] + dvp).astype(dv_ref.dtype)
            ck[...] = dkc
            cv[...] = dvc

        @pl.when(i == nb)
        def _():
            dk_ref[...] = ck[...].astype(dk_ref.dtype)
            dv_ref[...] = cv[...].astype(dv_ref.dtype)

    qcur = lambda kv, i: (kv, jnp.minimum(i, nb - 1), 0)
    kcur = lambda kv, i: (kv, jnp.minimum(i, nb - 1), 0)
    kprev = lambda kv, i: (kv, jnp.clip(i - 1, 0, nb - 1), 0)
    qspec = pl.BlockSpec((G, BLOCK, HEAD_DIM), qcur)
    kc_spec = pl.BlockSpec((None, BLOCK, HEAD_DIM), kcur)
    kp_spec = pl.BlockSpec((None, BLOCK, HEAD_DIM), kprev)
    return pl.pallas_call(
        body, name=name, grid=(N_KV_HEADS, nb + 1),
        in_specs=[qspec, kp_spec, kc_spec, kp_spec, kc_spec, qspec,
                  pl.BlockSpec((R, 128), lambda kv, i: (kv, 0)),
                  pl.BlockSpec((G, BLOCK, 2 * BLOCK), lambda kv, i: (kv, 0, 0))],
        out_specs=[qspec, kp_spec, kp_spec,
                   pl.BlockSpec((R, 128), lambda kv, i: (kv, 0)),
                   pl.BlockSpec((G, BLOCK, 2 * BLOCK), lambda kv, i: (kv, 0, 0))],
        out_shape=[jax.ShapeDtypeStruct((N_Q_HEADS, s, HEAD_DIM), BF16),
                   jax.ShapeDtypeStruct((N_KV_HEADS, s, HEAD_DIM), BF16),
                   jax.ShapeDtypeStruct((N_KV_HEADS, s, HEAD_DIM), BF16),
                   jax.ShapeDtypeStruct((N_Q_HEADS * BLOCK, 128), F32),
                   jax.ShapeDtypeStruct((N_Q_HEADS, BLOCK, 2 * BLOCK), F32)],
        scratch_shapes=[pltpu.VMEM((BLOCK, HEAD_DIM), F32), pltpu.VMEM((BLOCK, HEAD_DIM), F32)],
        compiler_params=_cparams(("arbitrary", "arbitrary")),
    )(qh, kh, kh, vh, vh, doh, sinks, bias)


def _cmul(ar, ai, br, bi):
    return ar * br - ai * bi, ar * bi + ai * br


def _scan_passes(a_ref, b_ref, x_ref, xp_ref, da_ref, *, s, tc, reverse):
    nc = SCAN_CHUNKS
    steps = s // nc
    with_da = xp_ref is not None
    unroll = 8 if steps % 8 == 0 else 1

    def shift(v, d):
        row = lax.broadcasted_iota(jnp.int32, v.shape, 0)
        if reverse:
            return jnp.where(row < nc - d, pltpu.roll(v, nc - d, 0), 0.0)
        return jnp.where(row >= d, pltpu.roll(v, d, 0), 0.0)

    def run():
        ar = jnp.broadcast_to(a_ref[0], (nc, tc))
        ai = jnp.broadcast_to(a_ref[1], (nc, tc))

        def row_of(step):
            j = (steps - 1 - step) if reverse else step
            return pl.multiple_of(j * nc, nc)

        def p1(step, st):
            sr, si = st
            r0 = row_of(step)
            mr, mi = _cmul(ar, ai, sr, si)
            sr = mr + b_ref[0, pl.ds(r0, nc), :]
            si = mi + b_ref[1, pl.ds(r0, nc), :]
            x_ref[0, pl.ds(r0, nc), :] = sr
            x_ref[1, pl.ds(r0, nc), :] = si
            return sr, si
        zero = jnp.zeros((nc, tc), F32)
        er, ei = lax.fori_loop(0, steps, p1, (zero, zero), unroll=unroll)

        pr, pi_ = jnp.ones((nc, tc), F32), zero
        br, bi, left = ar, ai, steps
        while left:
            if left & 1:
                pr, pi_ = _cmul(pr, pi_, br, bi)
            br, bi = _cmul(br, bi, br, bi)
            left >>= 1
        cr, ci = shift(er, 1), shift(ei, 1)
        d = 1
        while d < nc:
            mr, mi = _cmul(pr, pi_, shift(cr, d), shift(ci, d))
            cr, ci = cr + mr, ci + mi
            pr, pi_ = _cmul(pr, pi_, pr, pi_)
            d *= 2

        def p2(step, st):
            qr, qi, dar, dai = st
            r0 = row_of(step)
            qr, qi = _cmul(ar, ai, qr, qi)
            fr, fi = _cmul(qr, qi, cr, ci)
            xr = x_ref[0, pl.ds(r0, nc), :] + fr
            xi = x_ref[1, pl.ds(r0, nc), :] + fi
            x_ref[0, pl.ds(r0, nc), :] = xr
            x_ref[1, pl.ds(r0, nc), :] = xi
            if with_da:
                jm = jnp.where(step == steps - 1, steps - 1, steps - 2 - step)
                rp = pl.multiple_of(jm * nc, nc)
                vr, vi = xp_ref[0, pl.ds(rp, nc), :], xp_ref[1, pl.ds(rp, nc), :]
                row = lax.broadcasted_iota(jnp.int32, (nc, tc), 0)
                first = step == steps - 1
                sel = jnp.logical_and(first, row == 0)
                vr = jnp.where(sel, 0.0, jnp.where(first, pltpu.roll(vr, 1, 0), vr))
                vi = jnp.where(sel, 0.0, jnp.where(first, pltpu.roll(vi, 1, 0), vi))
                dar = dar + xr * vr + xi * vi
                dai = dai + xi * vr - xr * vi
            return qr, qi, dar, dai
        _, _, dar, dai = lax.fori_loop(0, steps, p2, (jnp.ones((nc, tc), F32), zero, zero, zero), unroll=unroll)
        if with_da:
            da_ref[0] = jnp.sum(dar, axis=0, keepdims=True)
            da_ref[1] = jnp.sum(dai, axis=0, keepdims=True)

    run()


def _ssm_fwd(u, bd, cd, a, d_row, *, name, sb, sbn):
    s, w = u.shape
    nst = a.shape[2]
    nblk = w // sb
    rows = min(512, s)
    nn = (((1,), (0,)), ((), ()))

    def body(u_ref, bre_ref, bim_ref, cre_ref, cim_ref, a_ref, d_ref, y_ref, x_ref):

        def fill(r, carry):
            r0 = pl.multiple_of(r * rows, rows)
            ub = u_ref[pl.ds(r0, rows), :].astype(BF16)
            x_ref[0, pl.ds(r0, rows), :] = lax.dot_general(ub, bre_ref[...].astype(BF16), nn, preferred_element_type=F32)
            x_ref[1, pl.ds(r0, rows), :] = lax.dot_general(ub, bim_ref[...].astype(BF16), nn, preferred_element_type=F32)
            return carry
        lax.fori_loop(0, s // rows, fill, 0)
        _scan_passes(a_ref, x_ref, x_ref, None, None, s=s, tc=sbn, reverse=False)

        def project(r, carry):
            r0 = pl.multiple_of(r * rows, rows)
            y = lax.dot_general(x_ref[0, pl.ds(r0, rows), :].astype(BF16), cre_ref[...].astype(BF16), nn, preferred_element_type=F32)
            y = y + lax.dot_general(x_ref[1, pl.ds(r0, rows), :].astype(BF16), cim_ref[...].astype(BF16), nn, preferred_element_type=F32)
            y_ref[pl.ds(r0, rows), :] = y + d_ref[...] * u_ref[pl.ds(r0, rows), :]
            return carry
        lax.fori_loop(0, s // rows, project, 0)

    return pl.pallas_call(
        body, name=name, grid=(nblk,),
        in_specs=[pl.BlockSpec((s, sb), lambda j: (0, j)),
                  pl.BlockSpec((sb, sbn), lambda j: (j, j)), pl.BlockSpec((sb, sbn), lambda j: (j, nblk + j)),
                  pl.BlockSpec((sbn, sb), lambda j: (j, j)), pl.BlockSpec((sbn, sb), lambda j: (nblk + j, j)),
                  pl.BlockSpec((2, 1, sbn), lambda j: (0, 0, j)), pl.BlockSpec((1, sb), lambda j: (0, j))],
        out_specs=[pl.BlockSpec((s, sb), lambda j: (0, j)), pl.BlockSpec((2, s, sbn), lambda j: (0, 0, j))],
        out_shape=[jax.ShapeDtypeStruct((s, w), F32), jax.ShapeDtypeStruct((2, s, nst), F32)],
        compiler_params=pltpu.CompilerParams(dimension_semantics=("arbitrary",), vmem_limit_bytes=SSM_VMEM_LIMIT_BYTES),
    )(u, bd, bd, cd, cd, a, d_row)


def _ssm_bwd(dy, u, xs, bd, cd, a, d_row, *, name, sb, sbn, deps=()):
    s, w = u.shape
    nst = a.shape[2]
    nblk = w // sb
    rows = min(512, s)
    nt = (((1,), (1,)), ((), ()))
    tn = (((0,), (0,)), ((), ()))

    def body(dy_ref, u_ref, xs_hbm, bre_ref, bim_ref, cre_ref, cim_ref, a_ref, d_ref, *rest):
        du_ref, gb_ref, gc_ref, da_ref, gd_ref, lam, xs_ref, sem = rest[len(deps):]
        j = pl.program_id(0)
        fetch = pltpu.make_async_copy(xs_hbm.at[:, :, pl.ds(pl.multiple_of(j * sbn, sbn), sbn)], xs_ref, sem)
        fetch.start()

        def fill(r, carry):
            r0 = pl.multiple_of(r * rows, rows)
            dyb = dy_ref[pl.ds(r0, rows), :].astype(BF16)
            lam[0, pl.ds(r0, rows), :] = lax.dot_general(dyb, cre_ref[...].astype(BF16), nt, preferred_element_type=F32)
            lam[1, pl.ds(r0, rows), :] = lax.dot_general(dyb, cim_ref[...].astype(BF16), nt, preferred_element_type=F32)
            return carry
        lax.fori_loop(0, s // rows, fill, 0)
        fetch.wait()
        _scan_passes(a_ref, lam, lam, xs_ref, da_ref, s=s, tc=sbn, reverse=True)
        gb_ref[...] = jnp.zeros_like(gb_ref)
        gc_ref[...] = jnp.zeros_like(gc_ref)
        gd_ref[...] = jnp.zeros_like(gd_ref)

        def project(r, carry):
            r0 = pl.multiple_of(r * rows, rows)
            dyv, uv = dy_ref[pl.ds(r0, rows), :], u_ref[pl.ds(r0, rows), :]
            dyb, ub = dyv.astype(BF16), uv.astype(BF16)
            lr, li = lam[0, pl.ds(r0, rows), :].astype(BF16), lam[1, pl.ds(r0, rows), :].astype(BF16)
            du = lax.dot_general(lr, bre_ref[...].astype(BF16), nt, preferred_element_type=F32)
            du = du + lax.dot_general(li, bim_ref[...].astype(BF16), nt, preferred_element_type=F32)
            du_ref[pl.ds(r0, rows), :] = du + d_ref[...] * dyv
            gb_ref[:, 0:sbn] += lax.dot_general(ub, lr, tn, preferred_element_type=F32)
            gb_ref[:, sbn:2 * sbn] += lax.dot_general(ub, li, tn, preferred_element_type=F32)
            gc_ref[0] += lax.dot_general(xs_ref[0, pl.ds(r0, rows), :].astype(BF16), dyb, tn, preferred_element_type=F32)
            gc_ref[1] += lax.dot_general(xs_ref[1, pl.ds(r0, rows), :].astype(BF16), dyb, tn, preferred_element_type=F32)
            gd_ref[...] += jnp.sum(dyv * uv, axis=0, keepdims=True)
            return carry
        lax.fori_loop(0, s // rows, project, 0)

    col = lambda j: (0, j)
    return pl.pallas_call(
        body, name=name, grid=(nblk,),
        in_specs=[pl.BlockSpec((s, sb), col), pl.BlockSpec((s, sb), col), pl.BlockSpec(memory_space=pl.ANY),
                  pl.BlockSpec((sb, sbn), lambda j: (j, j)), pl.BlockSpec((sb, sbn), lambda j: (j, nblk + j)),
                  pl.BlockSpec((sbn, sb), lambda j: (j, j)), pl.BlockSpec((sbn, sb), lambda j: (nblk + j, j)),
                  pl.BlockSpec((2, 1, sbn), lambda j: (0, 0, j)), pl.BlockSpec((1, sb), col)]
        + [pl.BlockSpec(memory_space=pl.ANY)] * len(deps),
        out_specs=[pl.BlockSpec((s, sb), col), pl.BlockSpec((sb, 2 * sbn), lambda j: (j, 0)),
                   pl.BlockSpec((2, sbn, sb), lambda j: (0, j, 0)), pl.BlockSpec((2, 1, sbn), lambda j: (0, 0, j)),
                   pl.BlockSpec((1, sb), col)],
        out_shape=[jax.ShapeDtypeStruct((s, w), F32), jax.ShapeDtypeStruct((w, 2 * sbn), F32),
                   jax.ShapeDtypeStruct((2, nst, sb), F32), jax.ShapeDtypeStruct((2, 1, nst), F32),
                   jax.ShapeDtypeStruct((1, w), F32)],
        scratch_shapes=[pltpu.VMEM((2, s, sbn), F32), pltpu.VMEM((2, s, sbn), F32), pltpu.SemaphoreType.DMA],
        compiler_params=pltpu.CompilerParams(dimension_semantics=("arbitrary",), vmem_limit_bytes=SSM_VMEM_LIMIT_BYTES),
    )(dy, u, xs, bd, bd, cd, cd, a, d_row, *deps)


def _adamw_math(w, g, m, v):
    nm = ADAM_B1 * m + (1.0 - ADAM_B1) * g
    nv = ADAM_B2 * v + (1.0 - ADAM_B2) * (g * g)
    m_hat = nm / (1.0 - ADAM_B1 ** ADAM_STEP)
    v_hat = nv / (1.0 - ADAM_B2 ** ADAM_STEP)
    return -ADAM_LR * (m_hat / (jnp.sqrt(v_hat) + ADAM_EPS) + ADAM_WD * w), nm, nv


def _adamw_many(ws, gs, ms, vs, name, deps=()):
    n, nd = len(ws), len(deps)

    def body(*refs):
        outs = refs[4 * n + nd:]
        for i in range(n):
            d, nm, nv = _adamw_math(refs[i][...], refs[n + i][...], refs[2 * n + i][...], refs[3 * n + i][...])
            outs[i][...], outs[n + i][...], outs[2 * n + i][...] = d, nm, nv

    whole = pl.BlockSpec(memory_space=pltpu.VMEM)
    res = pl.pallas_call(
        body, name=name, in_specs=[whole] * (4 * n) + [pl.BlockSpec(memory_space=pl.ANY)] * nd,
        out_specs=[whole] * (3 * n), out_shape=[jax.ShapeDtypeStruct(w.shape, F32) for w in ws] * 3,
        compiler_params=pltpu.CompilerParams(vmem_limit_bytes=VMEM_LIMIT_BYTES),
    )(*ws, *gs, *ms, *vs, *deps)
    return res[:n], res[n:2 * n], res[2 * n:]


def _adamw(w, g, m, v, name, deps=(), echo=False):
    nd = len(deps)
    r, c = w.shape
    tr = r
    for cand in (512, 256, 128, 64, 32, 16, 8):
        if r % cand == 0 and cand * c * 4 <= 2 * 1024 * 1024:
            tr = cand
            break

    def body(w_ref, g_ref, m_ref, v_ref, *rest):
        d_ref, nm_ref, nv_ref = rest[nd:nd + 3]
        gv = g_ref[...]
        d_ref[...], nm_ref[...], nv_ref[...] = _adamw_math(w_ref[...], gv, m_ref[...], v_ref[...])
        if echo:
            rest[nd + 3][...] = gv

    no = 4 if echo else 3
    spec = pl.BlockSpec((tr, c), lambda i: (i, 0))
    sds = jax.ShapeDtypeStruct((r, c), F32)
    return pl.pallas_call(body, name=name, grid=(r // tr,),
                          in_specs=[spec] * 4 + [pl.BlockSpec(memory_space=pl.ANY)] * nd, out_specs=[spec] * no,
                          out_shape=[sds] * no, compiler_params=_cparams(("parallel",)))(w, g, m, v, *deps)


def _sum_lead(x, name, out_dtype=F32):
    n, r, c = x.shape
    tr = r
    for cand in (512, 256, 128, 64, 32, 16, 8):
        if r % cand == 0 and n * cand * c * 4 <= 4 * 1024 * 1024:
            tr = cand
            break

    def body(x_ref, o_ref):
        acc = x_ref[0].astype(F32)
        for k in range(1, n):
            acc = acc + x_ref[k].astype(F32)
        o_ref[...] = acc.astype(o_ref.dtype)

    return pl.pallas_call(body, name=name, grid=(r // tr,),
                          in_specs=[pl.BlockSpec((n, tr, c), lambda i: (0, i, 0))],
                          out_specs=pl.BlockSpec((tr, c), lambda i: (i, 0)),
                          out_shape=jax.ShapeDtypeStruct((r, c), out_dtype),
                          compiler_params=_cparams(("parallel",)))(x)


def _row_tile(rows, row_bytes, budget, least=8):
    for cand in (1024, 512, 256, 128, 64, 32, 16, 8):
        if cand >= least and rows % cand == 0 and cand * row_bytes <= budget:
            return cand
    return rows


def _cast_into_slot(w, slot, name):
    r, c = w.shape
    tr = _row_tile(r, c * 4, 4 * 1024 * 1024, least=16)

    def body(slot_ref, w_ref, o_ref):
        o_ref[...] = w_ref[...].astype(o_ref.dtype)

    gs = pltpu.PrefetchScalarGridSpec(
        num_scalar_prefetch=1, grid=(r // tr,),
        in_specs=[pl.BlockSpec((tr, c), lambda i, s: (i, 0))],
        out_specs=pl.BlockSpec((None, tr, c), lambda i, s: (s[0], i, 0)))
    return pl.pallas_call(body, name=name, grid_spec=gs, out_shape=jax.ShapeDtypeStruct((N_CHIPS, r, c), BF16),
                          compiler_params=_cparams(("parallel",)))(slot, w)


def _sum_own(p, t, sel, name):
    _, h, c = p.shape
    tr = _row_tile(h, c * 4, 2 * 1024 * 1024, least=16)
    nblk = h // tr

    def body(sel_ref, p_ref, t_ref, o_ref):
        acc = p_ref[...].astype(F32)
        for k in range(3):
            acc = acc + t_ref[k].astype(F32)
        o_ref[...] = acc

    gs = pltpu.PrefetchScalarGridSpec(
        num_scalar_prefetch=1, grid=(nblk,),
        in_specs=[pl.BlockSpec((None, tr, c), lambda i, s: (s[0], i, 0)),
                  pl.BlockSpec((3, tr, c), lambda i, s: (0, i, 0))],
        out_specs=pl.BlockSpec((tr, c), lambda i, s: (s[1] * nblk + i, 0)))
    return pl.pallas_call(body, name=name, grid_spec=gs, out_shape=jax.ShapeDtypeStruct((2 * h, c), F32),
                          compiler_params=_cparams(("parallel",)))(sel, p, t)


def _add_half(g, t, half, name):
    n, r, c = g.shape
    h = r // 2
    tr = h
    for cand in (512, 256, 128, 64, 32, 16):
        if h % cand == 0 and cand * c * 2 <= 2 * 1024 * 1024:
            tr = cand
            break
    nblk = h // tr

    def body(half_ref, g_ref, t_ref, o_ref):
        o_ref[...] = (g_ref[...].astype(F32) + t_ref[...].astype(F32)).astype(o_ref.dtype)

    gs = pltpu.PrefetchScalarGridSpec(
        num_scalar_prefetch=1, grid=(n, nblk),
        in_specs=[pl.BlockSpec((None, tr, c), lambda j, i, hr: (j, hr[0] * nblk + i, 0)),
                  pl.BlockSpec((None, tr, c), lambda j, i, hr: (j, i, 0))],
        out_specs=pl.BlockSpec((None, tr, c), lambda j, i, hr: (j, i, 0)))
    return pl.pallas_call(body, name=name, grid_spec=gs, out_shape=jax.ShapeDtypeStruct((n, h, c), BF16),
                          compiler_params=_cparams(("parallel", "parallel")))(half, g, t)


def _position():
    x, y, c = lax.axis_index("x"), lax.axis_index("y"), lax.axis_index("c")
    return x, y, c


def _allgather8(xs, name):
    m_per, n = xs.shape

    def body(x_ref, out_ref, send_sems, recv_sems, local_sem):
        x, y, c = _position()
        me, sibling = (x, y, c), (x, y, 1 - c)
        chips = [(1 - x, y), (x, 1 - y), (1 - x, 1 - y)]

        def rows(px, py, pc):
            return out_ref.at[pl.ds((4 * px + 2 * py + pc) * m_per, m_per), :]

        def copy(k, block, to, src=None):
            return pltpu.make_async_remote_copy(
                src_ref=rows(*block) if src is None else src, dst_ref=rows(*block),
                send_sem=send_sems.at[k], recv_sem=recv_sems.at[k], device_id=to, device_id_type=MESH)

        mine = pltpu.make_async_copy(x_ref, rows(*me), local_sem)
        mine.start()
        first = [copy(0, me, sibling, src=x_ref)]
        first += [copy(1 + j, me, (*chip, c), src=x_ref) for j, chip in enumerate(chips)]
        for cp in first:
            cp.start()
        passed = [copy(4 + j, (*chip, c), sibling) for j, chip in enumerate(chips)]
        for j, chip in enumerate(chips):
            copy(1 + j, (*chip, c), me).wait_recv()
            passed[j].start()
        copy(0, sibling, me).wait_recv()
        for j, chip in enumerate(chips):
            copy(4 + j, (*chip, 1 - c), me).wait_recv()
        for cp in first + passed:
            cp.wait_send()
        mine.wait()

    return pl.pallas_call(
        body, name=name, out_shape=jax.ShapeDtypeStruct((N_DEV * m_per, n), xs.dtype),
        in_specs=[pl.BlockSpec(memory_space=pltpu.VMEM)], out_specs=pl.BlockSpec(memory_space=pltpu.VMEM),
        scratch_shapes=[pltpu.SemaphoreType.DMA((7,)), pltpu.SemaphoreType.DMA((7,)), pltpu.SemaphoreType.DMA],
        compiler_params=pltpu.CompilerParams(vmem_limit_bytes=VMEM_LIMIT_BYTES),
    )(xs)


_HBM = pl.BlockSpec(memory_space=pltpu.HBM)


_SEM = pl.BlockSpec(memory_space=pltpu.SEMAPHORE)
_ANY = pl.BlockSpec(memory_space=pl.ANY)
_EFFECT = pltpu.SideEffectType.DATAFLOW_SIDE_EFFECTING


def _in_hbm(a):
    return pltpu.with_memory_space_constraint(a, pltpu.HBM)


def _several(after):
    return list(after) if isinstance(after, (list, tuple)) else [after]


def _gather_start(ws, groups, after, name):
    n = len(ws)
    after = _several(after)

    def body(*refs):
        in_refs = refs[:n]
        sems, token = refs[2 * n + len(after):-1], refs[-1]
        x, y, c = _position()
        mychip = 2 * x + y
        chips = [(1 - x, y), (x, 1 - y), (1 - x, 1 - y)]
        for g, members in enumerate(groups):
            for k, i in enumerate(members):
                h = ws[i].shape[1] // 2
                mine = in_refs[i].at[mychip, pl.ds(c * h, h), :]
                for j, (px, py) in enumerate(chips):
                    pltpu.make_async_remote_copy(
                        src_ref=mine, dst_ref=mine, send_sem=sems[2 * g].at[3 * k + j],
                        recv_sem=sems[2 * g + 1].at[3 * k + j], device_id=(px, py, c), device_id_type=MESH).start()
        token[...] = jnp.zeros_like(token)

    sem_shapes = [pltpu.SemaphoreType.DMA((3 * len(m),)) for m in groups for _ in range(2)]
    res = pl.pallas_call(
        body, name=name,
        out_shape=[pltpu.HBM(w.shape, w.dtype) for w in ws] + sem_shapes + [jax.ShapeDtypeStruct((8, 128), F32)],
        in_specs=[_HBM] * n + [_ANY] * len(after),
        out_specs=[_HBM] * n + [_SEM] * len(sem_shapes) + [pl.BlockSpec(memory_space=pltpu.VMEM)],
        input_output_aliases={i: i for i in range(n)},
        compiler_params=pltpu.CompilerParams(has_side_effects=_EFFECT),
    )(*[_in_hbm(w) for w in ws], *after)
    bufs, sems, token = res[:n], res[n:-1], res[-1]
    return list(bufs), [(sems[2 * g], sems[2 * g + 1]) for g in range(len(groups))], token


def _gather_wait(bufs, send_sems, recv_sems, after, name):
    m = len(bufs)

    def body(*refs):
        in_refs = refs[:m]
        send, recv = refs[m], refs[m + 1]
        x, y, c = _position()
        mychip = 2 * x + y
        chips = [(1 - x, y), (x, 1 - y), (1 - x, 1 - y)]
        for k in range(m):
            h = bufs[k].shape[1] // 2
            mine = in_refs[k].at[mychip, pl.ds(c * h, h), :]
            for j, (px, py) in enumerate(chips):
                cp = pltpu.make_async_remote_copy(
                    src_ref=mine, dst_ref=in_refs[k].at[2 * px + py, pl.ds(c * h, h), :],
                    send_sem=send.at[3 * k + j], recv_sem=recv.at[3 * k + j],
                    device_id=(px, py, c), device_id_type=MESH)
                cp.wait_send()
                cp.wait_recv()

    res = pl.pallas_call(
        body, name=name, out_shape=[pltpu.HBM(b.shape, b.dtype) for b in bufs],
        in_specs=[_HBM] * m + [_SEM, _SEM] + [_ANY] * len(_several(after)), out_specs=[_HBM] * m,
        input_output_aliases={k: k for k in range(m)},
        compiler_params=pltpu.CompilerParams(has_side_effects=_EFFECT),
    )(*bufs, send_sems, recv_sems, *_several(after))
    return list(res)


def _forward_halves(ws, name):
    n = len(ws)

    def body(*refs):
        out_refs = refs[n:2 * n]
        send_sems, recv_sems = refs[2 * n:]
        x, y, c = _position()
        me, sibling = (x, y, c), (x, y, 1 - c)
        chips = [(1 - x, y), (x, 1 - y), (1 - x, 1 - y)]
        cps = []
        for i in range(n):
            h = ws[i].shape[1] // 2
            for j, (px, py) in enumerate(chips):
                got = out_refs[i].at[2 * px + py, pl.ds(c * h, h), :]
                cp = pltpu.make_async_remote_copy(
                    src_ref=got, dst_ref=got, send_sem=send_sems.at[3 * i + j], recv_sem=recv_sems.at[3 * i + j],
                    device_id=sibling, device_id_type=MESH)
                cp.start()
                cps.append(cp)
        for i in range(n):
            h = ws[i].shape[1] // 2
            for j, (px, py) in enumerate(chips):
                other = out_refs[i].at[2 * px + py, pl.ds((1 - c) * h, h), :]
                pltpu.make_async_remote_copy(
                    src_ref=other, dst_ref=other, send_sem=send_sems.at[3 * i + j], recv_sem=recv_sems.at[3 * i + j],
                    device_id=me, device_id_type=MESH).wait_recv()
        for cp in cps:
            cp.wait_send()

    return pl.pallas_call(
        body, name=name,
        out_shape=[jax.ShapeDtypeStruct(w.shape, w.dtype) for w in ws],
        in_specs=[_HBM] * n, out_specs=[_HBM] * n, input_output_aliases={i: i for i in range(n)},
        scratch_shapes=[pltpu.SemaphoreType.DMA((3 * n,)), pltpu.SemaphoreType.DMA((3 * n,))],
    )(*ws)


def _copies_start(arrays, copies, nsem, after, name):
    n = len(arrays)
    after = _several(after)
    first = 2 * n + len(after)

    def body(*refs):
        for cp in copies(refs[:n], refs[first], refs[first + 1]):
            cp.start()
        refs[first + 2][...] = jnp.zeros_like(refs[first + 2])

    res = pl.pallas_call(
        body, name=name,
        out_shape=[pltpu.HBM(a.shape, a.dtype) for a in arrays]
        + [pltpu.SemaphoreType.DMA((nsem,)), pltpu.SemaphoreType.DMA((nsem,)), jax.ShapeDtypeStruct((8, 128), F32)],
        in_specs=[_HBM] * n + [_ANY] * len(after),
        out_specs=[_HBM] * n + [_SEM, _SEM, pl.BlockSpec(memory_space=pltpu.VMEM)],
        input_output_aliases={i: i for i in range(n)},
        compiler_params=pltpu.CompilerParams(has_side_effects=_EFFECT),
    )(*[_in_hbm(a) for a in arrays], *after)
    return list(res[:n]), res[n], res[n + 1], res[n + 2]


def _copies_wait(arrays, copies, send_sems, recv_sems, after, name):
    n = len(arrays)

    def body(*refs):
        for cp in copies(refs[:n], refs[n], refs[n + 1]):
            cp.wait_send()
            cp.wait_recv()

    res = pl.pallas_call(
        body, name=name, out_shape=[pltpu.HBM(a.shape, a.dtype) for a in arrays],
        in_specs=[_HBM] * n + [_SEM, _SEM] + [_ANY] * len(_several(after)), out_specs=[_HBM] * n,
        input_output_aliases={i: i for i in range(n)},
        compiler_params=pltpu.CompilerParams(has_side_effects=_EFFECT),
    )(*arrays, send_sems, recv_sems, *_several(after))
    return list(res)


def _scatter_copies(refs, send, recv):
    n = len(refs) // 2
    x, y, c = _position()
    chips = [(1 - x, y), (x, 1 - y), (1 - x, 1 - y)]
    return [pltpu.make_async_remote_copy(
        src_ref=refs[i].at[2 * px + py], dst_ref=refs[n + i].at[j],
        send_sem=send.at[3 * i + j], recv_sem=recv.at[3 * i + j], device_id=(px, py, c), device_id_type=MESH)
        for i in range(n) for j, (px, py) in enumerate(chips)]


def _swap_copies(refs, send, recv):
    n = len(refs) // 2
    x, y, c = _position()
    cps = []
    for i in range(n):
        h = refs[i].shape[1] // 2
        cps.append(pltpu.make_async_remote_copy(
            src_ref=refs[i].at[:, pl.ds((1 - c) * h, h), :], dst_ref=refs[n + i],
            send_sem=send.at[i], recv_sem=recv.at[i], device_id=(x, y, 1 - c), device_id_type=MESH))
    return cps


def _join_copies(refs, send, recv):
    x, y, c = _position()
    cps = []
    for i, r in enumerate(refs):
        h = r.shape[0] // 2
        mine = r.at[pl.ds(c * h, h), :]
        cps.append(pltpu.make_async_remote_copy(
            src_ref=mine, dst_ref=mine, send_sem=send.at[i], recv_sem=recv.at[i],
            device_id=(x, y, 1 - c), device_id_type=MESH))
    return cps


def _forward_copies(refs, send, recv):
    x, y, c = _position()
    chips = [(1 - x, y), (x, 1 - y), (1 - x, 1 - y)]
    cps = []
    for i, r in enumerate(refs):
        h = r.shape[1] // 2
        for j, (px, py) in enumerate(chips):
            got = r.at[2 * px + py, pl.ds(c * h, h), :]
            cps.append(pltpu.make_async_remote_copy(
                src_ref=got, dst_ref=got, send_sem=send.at[3 * i + j], recv_sem=recv.at[3 * i + j],
                device_id=(x, y, 1 - c), device_id_type=MESH))
    return cps


def _t5_buckets_block():
    qi = np.arange(BLOCK)[:, None]
    ki = np.arange(2 * BLOCK)[None, :]
    n = np.maximum(qi + BLOCK - ki, 0)
    max_exact = NUM_BUCKETS // 2
    large = max_exact + (np.log(np.maximum(n, 1) / max_exact) / np.log(MAX_DISTANCE / max_exact)
                         * (NUM_BUCKETS - max_exact)).astype(np.int32)
    large = np.minimum(large, NUM_BUCKETS - 1)
    return np.where(n < max_exact, n, large).astype(np.int32)


def _discretise(lambda_re, lambda_im, log_step, b_re, b_im):
    lam_re = jnp.minimum(lambda_re, -1e-4)
    lam_im = lambda_im
    delta = jnp.exp(log_step)
    mag = jnp.exp(lam_re * delta)
    ang = lam_im * delta
    abar_re, abar_im = mag * jnp.cos(ang), mag * jnp.sin(ang)
    num_re, num_im = abar_re - 1.0, abar_im
    den = lam_re * lam_re + lam_im * lam_im
    f_re = (num_re * lam_re + num_im * lam_im) / den
    f_im = (num_im * lam_re - num_re * lam_im) / den
    bbar_re = f_re * b_re - f_im * b_im
    bbar_im = f_re * b_im + f_im * b_re
    return abar_re, abar_im, bbar_re, bbar_im


def _ssm_params(args, cotangents, name):
    whole = pl.BlockSpec(memory_space=pltpu.VMEM)
    n_in = len(args)

    def body(*refs):
        vals = [r[...] for r in refs[:n_in]]
        if cotangents is None:
            outs = _discretise(*vals)
        else:
            outs = jax.vjp(_discretise, *vals)[1](tuple(r[...] for r in refs[n_in:n_in + 4]))
        for o_ref, o in zip(refs[-len(outs):], outs):
            o_ref[...] = o

    if cotangents is None:
        like, operands = [args[0], args[0], args[3], args[3]], list(args)
    else:
        like, operands = list(args), list(args) + list(cotangents)
    return pl.pallas_call(body, name=name, in_specs=[whole] * len(operands), out_specs=[whole] * len(like),
                          out_shape=[jax.ShapeDtypeStruct(a.shape, F32) for a in like])(*operands)


def _interleave(v, nc):
    s, w = v.shape
    return v.reshape(nc, s // nc, w).transpose(1, 0, 2).reshape(s, w)


def _deinterleave(v, nc):
    s, w = v.shape
    return v.reshape(s // nc, nc, w).transpose(1, 0, 2).reshape(s, w)


_SMALL = ("norm1_g", "b_in", "attn_sinks", "rel_bias", "lambda_re", "lambda_im", "log_step", "ssm_b_re",
          "ssm_b_im", "ssm_c_re", "ssm_c_im", "ssm_d", "b_glu", "norm2_g", "final_g")


def _pack(parts):
    rows = []
    for p in parts:
        f = p.reshape(-1).astype(F32)
        pad = (-f.shape[0]) % 128
        rows.append(jnp.pad(f, (0, pad)).reshape(-1, 128))
    out = jnp.concatenate(rows, axis=0)
    pad = (-out.shape[0]) % 256
    return jnp.pad(out, ((0, pad), (0, 0)))


def _unpack(packed, shapes):
    res, r = [], 0
    for shp in shapes:
        size = int(np.prod(shp))
        nr = -(-size // 128)
        res.append(packed[r:r + nr].reshape(-1)[:size].reshape(shp))
        r += nr
    return res


def kernel(x, c, w_ada, b_ada, norm1_g, w_in, b_in, attn_sinks, rel_bias, lambda_re, lambda_im, log_step, ssm_b_re, ssm_b_im, ssm_c_re, ssm_c_im, ssm_d, w_glu, b_glu, w_attn_proj, w_ssm_proj, w_out, norm2_g, w_ff1, w_ff2, final_g, loss_target, m_w_ada, m_b_ada, m_norm1_g, m_w_in, m_b_in, m_attn_sinks, m_rel_bias, m_lambda_re, m_lambda_im, m_log_step, m_ssm_b_re, m_ssm_b_im, m_ssm_c_re, m_ssm_c_im, m_ssm_d, m_w_glu, m_b_glu, m_w_attn_proj, m_w_ssm_proj, m_w_out, m_norm2_g, m_w_ff1, m_w_ff2, m_final_g, v_w_ada, v_b_ada, v_norm1_g, v_w_in, v_b_in, v_attn_sinks, v_rel_bias, v_lambda_re, v_lambda_im, v_log_step, v_ssm_b_re, v_ssm_b_im, v_ssm_c_re, v_ssm_c_im, v_ssm_d, v_w_glu, v_b_glu, v_w_attn_proj, v_w_ssm_proj, v_w_out, v_norm2_g, v_w_ff1, v_w_ff2, v_final_g):
    given = dict(locals())
    S, D = x.shape[1], x.shape[2]
    SSM_W = w_glu.shape[2]
    G = SSM_W // SSM_GROUP_CH
    NST = G * SSM_STATE
    DFF = w_ff2.shape[1] * N_CHIPS
    INW = w_in.shape[2] * N_CHIPS
    o_q, o_k, o_v, o_u = 0, ATTN_WIDTH, ATTN_WIDTH + KV_WIDTH, ATTN_WIDTH + 2 * KV_WIDTH
    o_ga, o_gs = o_u + SSM_W, o_u + SSM_W + D
    mx, my, mc = _position()
    my_chip = 2 * mx + my
    my_b = 4 * mx + 2 * my + mc

    xv, tgt = x[0], loss_target[0]

    big = dict(w_in=w_in[0], w_glu=w_glu[0], w_attn_proj=w_attn_proj[0], w_ssm_proj=w_ssm_proj[0],
               w_out=w_out[0], w_ff1=w_ff1[0], w_ff2=w_ff2[0])
    big_names = list(big)
    colsharded = {"w_in", "w_attn_proj", "w_ssm_proj", "w_ff1"}
    chip_sel = my_chip.astype(jnp.int32).reshape(1)
    gather_groups = [["w_in"], ["w_attn_proj", "w_ssm_proj", "w_glu", "w_out"], ["w_ff1", "w_ff2"]]
    in_flight, gather_sems, gathered = {}, [], {}

    def finish_gather(g, after):
        bufs = [in_flight[k] for k in gather_groups[g]]
        bufs = _gather_wait(bufs, gather_sems[g][0], gather_sems[g][1], after, "gather_wait_%d" % g)
        gathered.update(zip(gather_groups[g], _forward_halves(bufs, "gather_forward_%d" % g)))

    def tied(v, token):
        return v + token[0:1, 0:1]

    def all_of(*arrays):
        return list(arrays)

    def wop(k):
        g = gathered[k]
        return _Op(g, N_CHIPS) if k in colsharded else _Op(g.reshape(g.shape[0] * g.shape[1], g.shape[2]))

    grads = {}
    nothing = jnp.zeros((8, 128), F32)
    half = mc.astype(jnp.int32).reshape(1)
    sel = jnp.stack([my_chip, mc]).astype(jnp.int32)

    def rs_swap(tag, named):
        keys, gl = list(named), []
        for k in keys:
            gk = named[k]
            if k not in colsharded:
                gk = gk.reshape(N_CHIPS, gk.shape[0] // N_CHIPS, gk.shape[1])
            gl.append(gk)
        lands = [lax.empty((g.shape[0], g.shape[1] // 2, g.shape[2]), g.dtype) for g in gl]
        arrays, ssem, rsem, token = _copies_start(gl + lands, _swap_copies, len(gl), nothing, "rs_swap_start_" + tag)
        return (keys, arrays, ssem, rsem), token

    def rs_scatter(tag, state, after):
        keys, arrays, ssem, rsem = state
        arrays = _copies_wait(arrays, _swap_copies, ssem, rsem, after, "rs_swap_wait_" + tag)
        n = len(keys)
        ps = [_add_half(g, t, half, "rs_add_" + k) for g, t, k in zip(arrays[:n], arrays[n:], keys)]
        lands = [lax.empty((3,) + p.shape[1:], p.dtype) for p in ps]
        arrays, ssem, rsem, token = _copies_start(ps + lands, _scatter_copies, 3 * n, nothing, "rs_start_" + tag)
        return (keys, arrays, ssem, rsem), token

    def rs_sum(tag, state, after):
        keys, arrays, ssem, rsem = state
        arrays = _copies_wait(arrays, _scatter_copies, ssem, rsem, after, "rs_wait_" + tag)
        n = len(keys)
        rs = [_sum_own(p, t, sel, "rs_sum_" + k) for p, t, k in zip(arrays[:n], arrays[n:], keys)]
        rs, ssem, rsem, token = _copies_start(rs, _join_copies, n, nothing, "rs_join_start_" + tag)
        return (keys, rs, ssem, rsem), token

    def rs_finish(tag, state, after):
        keys, rs, ssem, rsem = state
        for k, f in zip(keys, _copies_wait(rs, _join_copies, ssem, rsem, after, "rs_join_wait_" + tag)):
            grads[k] = f[None]

    c_all = _allgather8(jnp.pad(c, ((0, 7), (0, 0))), "gather_c").reshape(N_DEV, 8, D)[:, 0]
    c16 = jnp.pad(c_all, ((0, 8), (0, 0)))
    b_ada_mine = lax.dynamic_slice(b_ada.reshape(N_CHIPS, -1), (my_chip, 0), (1, w_ada.shape[2]))
    mod_sh = _mm(c16, w_ada[0], "NN", name="mod", M=16, N=w_ada.shape[2], K=D, a_fn=_silu,
                 epilogue=lambda acc, b: (acc + b,), extras=[(b_ada_mine, "row")])
    mod_all = _allgather8(mod_sh[:8], "gather_mod").reshape(N_DEV, 8, -1)
    mod_row = jnp.concatenate(
        [lax.dynamic_slice(mod_all, (2 * j, my_b, 0), (1, 1, mod_all.shape[2]))[0] for j in range(N_CHIPS)], axis=1)
    sh1, sc1, g1, sh2, sc2, g2 = [mod_row[:, i * D:(i + 1) * D] for i in range(6)]

    first = [_cast_into_slot(big["w_in"], chip_sel, "cast_w_in")]
    first, sems_first, token_first = _gather_start(first, [[0]], mod_all, "gather_start_in")
    rest_names = gather_groups[1] + gather_groups[2]
    rest = [_cast_into_slot(big[k], chip_sel, "cast_" + k) for k in rest_names]
    rest, sems_rest, token_rest = _gather_start(
        rest, [[rest_names.index(k) for k in grp] for grp in gather_groups[1:]], token_first, "gather_start_rest")
    in_flight.update(zip(["w_in"] + rest_names, first + rest))
    gather_sems.extend(sems_first + sems_rest)

    disc_in = (lambda_re[0].reshape(1, NST), lambda_im[0].reshape(1, NST),
               jnp.repeat(log_step[0], SSM_STATE).reshape(1, NST),
               ssm_b_re[0].transpose(2, 0, 1).reshape(SSM_GROUP_CH, NST),
               ssm_b_im[0].transpose(2, 0, 1).reshape(SSM_GROUP_CH, NST))
    abar_re, abar_im, bbar_re, bbar_im = _ssm_params(disc_in, None, "ssm_params")
    same_group = jnp.asarray(np.arange(SSM_W)[:, None] // SSM_GROUP_CH == np.arange(NST)[None, :] // SSM_STATE)

    def block_diag(t):
        return jnp.where(same_group, jnp.tile(t, (G, 1)), 0.0)

    bd = jnp.concatenate([block_diag(bbar_re), block_diag(bbar_im)], axis=1)
    cd = jnp.concatenate([block_diag(cc.transpose(1, 0, 2).reshape(SSM_GROUP_CH, NST)).T
                          for cc in (ssm_c_re[0], -ssm_c_im[0])], axis=0)
    a_fwd = jnp.stack([abar_re, abar_im])
    a_bwd = jnp.stack([abar_re, -abar_im])
    d_row = ssm_d

    buckets = _t5_buckets_block()
    onehot_t = (jnp.arange(128, dtype=jnp.int32)[:, None] == jnp.asarray(buckets.reshape(1, -1))).astype(BF16)
    rb_hi = rel_bias.astype(BF16)
    rb_lo = (rel_bias - rb_hi.astype(F32)).astype(BF16)
    rb_lo2 = (rel_bias - rb_hi.astype(F32) - rb_lo.astype(F32)).astype(BF16)
    rb3 = jnp.pad(jnp.concatenate([rb_hi.T, rb_lo.T, rb_lo2.T], axis=0), ((0, 0), (0, 128 - NUM_BUCKETS)))
    b3 = _mm(rb3, onehot_t, "NN", name="rel_bias_rows", M=3 * N_Q_HEADS, N=BLOCK * 2 * BLOCK, K=128, tj=4096)
    bias = (b3[:N_Q_HEADS] + b3[N_Q_HEADS:2 * N_Q_HEADS]) + b3[2 * N_Q_HEADS:]
    bias = bias.reshape(N_Q_HEADS, BLOCK, 2 * BLOCK)
    sinks_b = jnp.broadcast_to(attn_sinks[0][:, None, None], (N_Q_HEADS, BLOCK, 128)).reshape(N_Q_HEADS * BLOCK, 128)

    def two(fn):
        def both(*blocks):
            r = fn(*blocks)
            return r, r
        return both

    h1, h1_t = _rowwise(two(_norm_mod), [(xv, "tile", D), (tied(tied(norm1_g, token_first), token_rest), "row", D),
                                         (sh1, "row", D), (sc1, "row", D)],
                        [(D, BF16), (D, BF16, "T")], [], name="norm1", rows=S)
    finish_gather(0, all_of(h1, bd, cd, a_fwd, a_bwd, bias, sinks_b))
    proj = _mm(h1, wop("w_in"), "NN", name="proj", M=S, N=INW, K=D, out_dtypes=(BF16,),
               epilogue=lambda acc, b: (acc + b,), extras=[(b_in, "row")])

    def heads(v2d, nh):
        return v2d.reshape(S, nh, HEAD_DIM).transpose(1, 0, 2)

    def unheads(v3d):
        return v3d.transpose(1, 0, 2).reshape(S, -1)

    qh = heads(proj[:, o_q:o_k], N_Q_HEADS)
    kh = heads(proj[:, o_k:o_v], N_KV_HEADS)
    vh = heads(proj[:, o_v:o_u], N_KV_HEADS)
    attn = unheads(_attn_fwd(qh, kh, vh, sinks_b, bias, "attn_fwd"))
    finish_gather(1, attn)
    y_attn = _mm(attn, wop("w_attn_proj"), "NN", name="attn_proj", M=S, N=D, K=ATTN_WIDTH, out_dtypes=(BF16,))

    u = proj[:, o_u:o_ga]
    u_il = _interleave(u, SCAN_CHUNKS)
    SB = 128
    nsb, gpb = SSM_W // SB, SB // SSM_GROUP_CH
    SBN = gpb * SSM_STATE
    y_il, xs = _ssm_fwd(u_il, bd, cd, a_fwd, d_row, name="ssm_fwd", sb=SB, sbn=SBN)
    y = _deinterleave(y_il, SCAN_CHUNKS)
    z, t_glu = _mm(y, wop("w_glu"), "NN", name="glu", M=S, N=SSM_W, K=SSM_W, out_dtypes=(BF16, F32), a_fn=_gelu,
                   epilogue=lambda acc, b, yy: (_gelu(yy) * _sigmoid(acc + b), acc + b),
                   extras=[(b_glu, "row"), (y, "tile")])
    y_ssm = _mm(z, wop("w_ssm_proj"), "NN", name="ssm_proj", M=S, N=D, K=SSM_W, out_dtypes=(BF16,))

    ff_bufs = _gather_wait([in_flight[k] for k in gather_groups[2]], gather_sems[2][0], gather_sems[2][1], all_of(y_ssm),
                           "gather_wait_2")
    ff_bufs, ff_send, ff_recv, token = _copies_start(ff_bufs, _forward_copies, 3 * len(ff_bufs), nothing,
                                                    "gather_forward_2_start")
    merged, merged_t = _rowwise(two(_merge), [(_Op(proj, coff=o_ga), "tile", D), (_Op(proj, coff=o_gs), "tile", D),
                                              (y_attn, "tile", D), (y_ssm, "tile", D)],
                                [(D, BF16), (D, BF16, "T")], [], name="merge", rows=S)
    mo, x2 = _mm(merged, wop("w_out"), "NN", name="out_proj", M=S, N=D, K=D, out_dtypes=(BF16, F32),
                 epilogue=lambda acc, xx, gg: (acc, xx + gg * acc), extras=[(xv, "tile"), (g1, "row")], deps=[token])
    h2, h2_t = _rowwise(two(_norm_mod), [(x2, "tile", D), (norm2_g, "row", D), (sh2, "row", D), (sc2, "row", D)],
                        [(D, BF16), (D, BF16, "T")], [], name="norm2", rows=S)
    gathered.update(zip(gather_groups[2], _copies_wait(ff_bufs, _forward_copies, ff_send, ff_recv, h2,
                                                       "gather_forward_2_wait")))
    a_b, r_b = _mm(h2, wop("w_ff1"), "NN", name="ff1", M=S, N=DFF, K=D, out_dtypes=(BF16, BF16),
                   epilogue=lambda acc: (acc, jnp.square(jnp.maximum(acc, 0.0))))
    ff, x3 = _mm(r_b, wop("w_ff2"), "NN", name="ff2", M=S, N=D, K=DFF, out_dtypes=(BF16, F32),
                 epilogue=lambda acc, xx, gg: (acc, xx + gg * acc), extras=[(x2, "tile"), (g2, "row")],
                 tj=1024, tk=1024)

    def final_fn(x3b, gf, tb, ffb, g2b):
        def f(xx, gg):
            yv = xx * lax.rsqrt(jnp.mean(xx * xx, axis=-1, keepdims=True) + EPS) * gg
            err = jnp.square(yv - tb)
            return 0.5 * jnp.sum(jnp.mean(err, axis=-1, keepdims=True), axis=0, keepdims=True)
        lv, vjp = jax.vjp(f, x3b, gf)
        dx, dg = vjp(jnp.ones((1, 1), F32))
        return dx, dx * g2b, dg, jnp.broadcast_to(lv, (1, 128)), jnp.sum(dx * ffb, axis=0, keepdims=True)

    dx3, dff, g_final, loss_acc, d_g2 = _rowwise(
        final_fn, [(x3, "tile", D), (final_g.reshape(1, D), "row", D), (tgt, "tile", D), (ff, "tile", D), (g2, "row", D)],
        [(D, F32), (D, BF16)], [D, 128, D], name="final", rows=S)
    da = _mm(dff, wop("w_ff2"), "NT", name="ff2_dx", M=S, N=DFF, K=D, out_dtypes=(BF16,),
             epilogue=lambda acc, ab: (acc * (2.0 * jnp.maximum(ab.astype(F32), 0.0)),), extras=[(a_b, "tile")])
    g_w_ff2 = _mm(r_b, dff, "TN", name="ff2_dw", M=DFF, N=D, K=S, out_dtypes=(BF16,), tj=1024, tk=1024)
    g_w_ff1 = _mm(h2_t, da, "NN", name="ff1_dw", M=D, N=DFF, K=S, out_dtypes=(BF16,), out_nsh=N_CHIPS, tj=1024, tk=1024)
    rs_ff, token = rs_swap("ff", dict(w_ff2=g_w_ff2, w_ff1=g_w_ff1))
    dh2 = _mm(da, wop("w_ff1"), "NT", name="ff1_dx", M=S, N=D, K=DFF, tj=1024, tk=1024, deps=[token])
    rs_ff, token_ff = rs_scatter("ff", rs_ff, dh2)

    def norm2_bwd(x2b, dh2b, dx3b, mob, gn, shb, scb, g1b):
        _, vjp = jax.vjp(_norm_mod, x2b, gn, shb, scb)
        dx, dg, dsh, dsc = vjp(dh2b)
        dx2b = dx + dx3b
        return dx2b, dx2b * g1b, dg, dsh, dsc, jnp.sum(dx2b * mob, axis=0, keepdims=True)

    dx2, dmo, g_norm2, d_sh2, d_sc2, d_g1 = _rowwise(
        norm2_bwd, [(x2, "tile", D), (dh2, "tile", D), (dx3, "tile", D), (mo, "tile", D),
                    (tied(norm2_g, token_ff), "row", D), (sh2, "row", D), (sc2, "row", D), (g1, "row", D)],
        [(D, F32), (D, BF16)], [D, D, D, D], name="norm2_bwd", rows=S)
    dmerged = _mm(dmo, wop("w_out"), "NT", name="out_dx", M=S, N=D, K=D)
    g_w_out = _mm(merged_t, dmo, "NN", name="out_dw", M=D, N=D, K=S, out_dtypes=(BF16,), tj=1024, tk=1024)

    def merge_bwd(gab, gsb, yab, ysb, dmb):
        _, vjp = jax.vjp(_merge, gab, gsb, yab, ysb)
        return vjp(dmb)

    d_ga, d_gs, dy_attn, dy_ssm = _rowwise(
        merge_bwd, [(_Op(proj, coff=o_ga), "tile", D), (_Op(proj, coff=o_gs), "tile", D), (y_attn, "tile", D),
                    (y_ssm, "tile", D), (dmerged, "tile", D)],
        [(D, BF16), (D, BF16), (D, BF16), (D, BF16)], [], name="merge_bwd", rows=S)

    dattn = _mm(dy_attn, wop("w_attn_proj"), "NT", name="attn_proj_dx", M=S, N=ATTN_WIDTH, K=D, tj=1024, out_dtypes=(BF16,))
    g_w_attn_proj = _mm(attn, dy_attn, "TN", name="attn_proj_dw", M=ATTN_WIDTH, N=D, K=S, out_dtypes=(BF16,),
                        out_nsh=N_CHIPS, tk=1024)

    dz = _mm(dy_ssm, wop("w_ssm_proj"), "NT", name="ssm_proj_dx", M=S, N=SSM_W, K=D)
    g_w_ssm_proj = _mm(z, dy_ssm, "TN", name="ssm_proj_dw", M=SSM_W, N=D, K=S, out_dtypes=(BF16,),
                       out_nsh=N_CHIPS, tk=1024)

    def glu_bwd(dzb, yb, tb):
        z0 = _gelu(yb)
        sg = _sigmoid(tb)
        dt = dzb * z0 * sg * (1.0 - sg)
        return dt, dzb * sg, jnp.sum(dt, axis=0, keepdims=True)

    dt_b, dz0a, g_b_glu = _rowwise(glu_bwd, [(dz, "tile", SSM_W), (y, "tile", SSM_W), (t_glu, "tile", SSM_W)],
                                   [(SSM_W, BF16), (SSM_W, F32)], [SSM_W], name="glu_bwd", rows=S)

    def gelu_bwd(acc, dz0ab, yb):
        _, vjp = jax.vjp(_gelu, yb)
        return (vjp(acc + dz0ab)[0],)

    dy = _mm(dt_b, wop("w_glu"), "NT", name="glu_dx", M=S, N=SSM_W, K=SSM_W, epilogue=gelu_bwd,
             extras=[(dz0a, "tile"), (y, "tile")])
    g_w_glu = _mm(y, dt_b, "TN", name="glu_dw", M=SSM_W, N=SSM_W, K=S, out_dtypes=(BF16,), tk=1024, a_fn=_gelu)
    rs_mix, token = rs_swap("mix", dict(w_out=g_w_out, w_attn_proj=g_w_attn_proj, w_ssm_proj=g_w_ssm_proj,
                                        w_glu=g_w_glu))
    dy_il = _interleave(dy, SCAN_CHUNKS)
    du_il, g_bd, g_cd, d_abar, g_ssm_d = _ssm_bwd(dy_il, u_il, xs, bd, cd, a_bwd, d_row, name="ssm_bwd", sb=SB, sbn=SBN,
                                                  deps=[token])
    du = _deinterleave(du_il, SCAN_CHUNKS)
    rs_mix, token_mix = rs_scatter("mix", rs_mix, du_il)

    dqh, dkh, dvh, dsink_blk, dbias = _attn_bwd(qh, kh, vh, heads(dattn, N_Q_HEADS), tied(sinks_b, token_mix), bias,
                                                "attn_bwd")
    g_sinks = _sum_lead(dsink_blk.reshape(N_Q_HEADS, BLOCK, 128).transpose(1, 0, 2), "sinks_dw")[:, 0].reshape(1, N_Q_HEADS)
    g_rel = _mm(dbias.reshape(N_Q_HEADS, -1), onehot_t, "NT", name="rel_bias_dw", M=N_Q_HEADS, N=128,
                K=BLOCK * 2 * BLOCK, tk=4096)
    g_rel_bias = g_rel[:, :NUM_BUCKETS].T

    eye_b = jnp.eye(gpb, dtype=F32)
    g_cd6 = g_cd.reshape(2, nsb, gpb, SSM_STATE, gpb, SSM_GROUP_CH)
    g_c_re = jnp.einsum("bgnhp,gh->bgpn", g_cd6[0], eye_b).reshape(G, SSM_GROUP_CH, SSM_STATE)
    g_c_im = -jnp.einsum("bgnhp,gh->bgpn", g_cd6[1], eye_b).reshape(G, SSM_GROUP_CH, SSM_STATE)
    g_bd6 = g_bd.reshape(nsb, gpb, SSM_GROUP_CH, 2, gpb, SSM_STATE)
    g_bbar = jnp.einsum("bhprgn,hg->rpbhn", g_bd6, eye_b).reshape(2, SSM_GROUP_CH, NST)
    g_lre, g_lim, g_lstep, g_bre, g_bim = _ssm_params(disc_in, (d_abar[0], d_abar[1], g_bbar[0], g_bbar[1]),
                                                      "ssm_params_bwd")
    g_lre, g_lim = g_lre.reshape(G, SSM_STATE), g_lim.reshape(G, SSM_STATE)
    g_lstep = g_lstep.reshape(G, SSM_STATE).sum(axis=1)
    g_bre = g_bre.reshape(SSM_GROUP_CH, G, SSM_STATE).transpose(1, 2, 0)
    g_bim = g_bim.reshape(SSM_GROUP_CH, G, SSM_STATE).transpose(1, 2, 0)

    dproj = jnp.concatenate([unheads(dqh).astype(BF16), unheads(dkh).astype(BF16), unheads(dvh).astype(BF16),
                             du.astype(BF16), d_ga, d_gs], axis=1)
    g_w_in = _mm(h1_t, dproj, "NN", name="proj_dw", M=D, N=INW, K=S, out_dtypes=(BF16,), out_nsh=N_CHIPS,
                 tj=INW // (2 * N_CHIPS), tk=1024)
    rs_in, token = rs_swap("in", dict(w_in=g_w_in))
    dh1 = _mm(dproj, wop("w_in"), "NT", name="proj_dx", M=S, N=D, K=INW, tj=1024, tk=INW // N_CHIPS, deps=[token])
    g_b_in = _rowwise(lambda d: (jnp.sum(d.astype(F32), axis=0, keepdims=True),), [(dproj, "tile", INW)], [], [INW],
                      name="proj_db", rows=S)[0]

    def norm1_bwd(xb, dhb, dresb, gn, shb, scb):
        _, vjp = jax.vjp(_norm_mod, xb, gn, shb, scb)
        dx, dg, dsh, dsc = vjp(dhb)
        return dx + dresb, dg, dsh, dsc

    grad_x, g_norm1, d_sh1, d_sc1 = _rowwise(
        norm1_bwd, [(xv, "tile", D), (dh1, "tile", D), (dx2, "tile", D), (norm1_g, "row", D),
                    (sh1, "row", D),
                    (sc1, "row", D)], [(D, F32)], [D, D, D], name="norm1_bwd", rows=S)

    dmod_row = jnp.concatenate([d_sh1, d_sc1, d_g1, d_sh2, d_sc2, d_g2], axis=1)
    small_g = dict(norm1_g=g_norm1, b_in=g_b_in, attn_sinks=g_sinks, rel_bias=g_rel_bias, lambda_re=g_lre[None],
                   lambda_im=g_lim[None], log_step=g_lstep[None], ssm_b_re=g_bre[None], ssm_b_im=g_bim[None],
                   ssm_c_re=g_c_re[None], ssm_c_im=g_c_im[None], ssm_d=g_ssm_d, b_glu=g_b_glu, norm2_g=g_norm2,
                   final_g=g_final.reshape(D))
    packed = _pack([dmod_row, loss_acc[:, :1]] + [small_g[k] for k in _SMALL])
    rows = packed.shape[0]
    gathered = _allgather8(packed, "gather_small").reshape(N_DEV, rows, 128)
    summed = _sum_lead(gathered, "small_sum")
    parts = _unpack(summed, [dmod_row.shape, (1,)] + [given[k].shape for k in _SMALL])
    loss = parts[1].reshape(())
    grads.update(zip(_SMALL, parts[2:]))
    grads["b_ada"] = parts[0]

    dmod_all = gathered[:, :dmod_row.shape[1] // 128].reshape(N_DEV, -1)
    dmod_mine = lax.dynamic_slice(dmod_all.reshape(N_DEV, N_CHIPS, -1), (0, my_chip, 0), (N_DEV, 1, w_ada.shape[2]))[:, 0]
    g_w_ada = _mm(c16, jnp.pad(dmod_mine, ((0, 8), (0, 0))), "TN", name="ada_dw", M=D, N=w_ada.shape[2], K=16,
                  a_fn=_silu)
    grads["w_ada"] = g_w_ada[None]

    deltas, new_m, new_v = {}, {}, {}

    def adamw_big(k, deps=()):
        echo = k in big_names
        res = _adamw(given[k][0], grads[k][0], given["m_" + k][0], given["v_" + k][0], "adamw_" + k, deps, echo)
        deltas[k], new_m[k], new_v[k] = res[0][None], res[1][None], res[2][None]
        if echo:
            grads[k] = res[3][None]
        return res[2]

    rs_in, token_in = rs_scatter("in", rs_in, all_of(summed, dmod_all))
    rs_ff, token = rs_sum("ff", rs_ff, all_of(summed, token_in))
    mark = adamw_big("w_ada", [token])
    rs_mix, token = rs_sum("mix", rs_mix, mark)
    small_all = list(_SMALL) + ["b_ada"]
    for k in small_all:
        grads[k] = grads[k].reshape(given[k].shape)

    def rows_of(a):
        return a.reshape(1, -1) if a.ndim == 1 else a

    d_, m_, v_ = _adamw_many(*[[rows_of(src[k]) for k in small_all] for src in (
        given, grads, {k: given["m_" + k] for k in small_all}, {k: given["v_" + k] for k in small_all})],
        "adamw_small", [token])
    for k, dd, mm, vv in zip(small_all, d_, m_, v_):
        deltas[k], new_m[k], new_v[k] = (t.reshape(given[k].shape) for t in (dd, mm, vv))
    v_ = v_[0]
    rs_finish("ff", rs_ff, v_)
    marks = [adamw_big(k) for k in ("w_ff2", "w_ff1")]
    rs_finish("mix", rs_mix, all_of(*marks))
    marks = [adamw_big(k) for k in ("w_out", "w_attn_proj", "w_ssm_proj", "w_glu")]
    rs_in, token = rs_sum("in", rs_in, all_of(*marks))
    rs_finish("in", rs_in, token)
    adamw_big("w_in")

    names = ["w_ada", "b_ada", "norm1_g", "w_in", "b_in", "attn_sinks", "rel_bias", "lambda_re", "lambda_im",
             "log_step", "ssm_b_re", "ssm_b_im", "ssm_c_re", "ssm_c_im", "ssm_d", "w_glu", "b_glu", "w_attn_proj",
             "w_ssm_proj", "w_out", "norm2_g", "w_ff1", "w_ff2", "final_g"]
    return (loss, grad_x[None], *[grads[n] for n in names], *[deltas[n] for n in names],
            *[new_m[n] for n in names], *[new_v[n] for n in names])
```

```python
import math

import numpy as np
import jax
import jax.numpy as jnp
from jax import lax
from jax.experimental import pallas as pl
from jax.experimental.pallas import tpu as pltpu

F32 = jnp.float32
BF16 = jnp.bfloat16
MESH = pl.DeviceIdType.MESH

HEAD_DIM = 64
N_Q_HEADS = 16
N_KV_HEADS = 4
GQA_GROUP = N_Q_HEADS // N_KV_HEADS
ATTN_WIDTH = N_Q_HEADS * HEAD_DIM
KV_WIDTH = N_KV_HEADS * HEAD_DIM
BLOCK = 128
NUM_BUCKETS = 32
MAX_DISTANCE = 128
NEG_INF = -1e30
SSM_GROUP_CH = 16
SSM_STATE = 64
EPS = 1e-6
ADAM_LR = 0.001
ADAM_B1 = 0.9
ADAM_B2 = 0.999
ADAM_EPS = 1e-08
ADAM_WD = 0.01
ADAM_STEP = 10

N_CHIPS = 4
N_DEV = 8
SCAN_CHUNKS = 8
VMEM_LIMIT_BYTES = 48 * 1024 * 1024
SSM_VMEM_LIMIT_BYTES = 56 * 1024 * 1024


def _cparams(sem=None):
    return pltpu.CompilerParams(dimension_semantics=sem, vmem_limit_bytes=VMEM_LIMIT_BYTES)


class _Op:
    def __init__(self, arr, nsh=None, coff=0):
        self.arr, self.nsh, self.coff = arr, nsh, coff
        if nsh is None:
            self.rows, self.cols = arr.shape
        else:
            assert arr.shape[0] == nsh
            self.rows, self.cols = arr.shape[1], arr.shape[2] * nsh

    def spec(self, br, bc, idx):
        assert self.coff % bc == 0
        off = self.coff // bc
        if self.nsh is None:
            return pl.BlockSpec((br, bc), lambda *g: (idx(*g)[0], idx(*g)[1] + off))
        per = (self.cols // self.nsh) // bc
        assert per * bc * self.nsh == self.cols

        def imap(*g):
            r, c = idx(*g)
            c = c + off
            return (c // per, r, c % per)
        return pl.BlockSpec((None, br, bc), imap)


def _as_op(a):
    return a if isinstance(a, _Op) else _Op(a)


def _mm(a, b, mode, *, name, M, N, K, out_dtypes=(F32,), out_nsh=None, epilogue=None, extras=(),
        a_fn=None, ti=1024, tj=512, tk=2048, deps=()):
    nd = len(deps)
    a, b = _as_op(a), _as_op(b)
    ti, tj, tk = min(ti, M), min(tj, N), min(tk, K)
    a_w = a.cols // a.nsh if a.nsh else None
    b_w = b.cols // b.nsh if b.nsh else None
    if a_w:
        ti, tk = (min(ti, a_w), tk) if mode == "TN" else (ti, min(tk, a_w))
    if b_w:
        tj, tk = (tj, min(tk, b_w)) if mode == "NT" else (min(tj, b_w), tk)
    if out_nsh:
        tj = min(tj, N // out_nsh)
    assert M % ti == 0 and N % tj == 0 and K % tk == 0, (name, M, N, K, ti, tj, tk)
    nk = K // tk
    if mode == "NN":
        a_spec = a.spec(ti, tk, lambda i, j, k: (i, k))
        b_spec = b.spec(tk, tj, lambda i, j, k: (k, j))
        dims = (((1,), (0,)), ((), ()))
    elif mode == "NT":
        a_spec = a.spec(ti, tk, lambda i, j, k: (i, k))
        b_spec = b.spec(tj, tk, lambda i, j, k: (j, k))
        dims = (((1,), (1,)), ((), ()))
    else:
        a_spec = a.spec(tk, ti, lambda i, j, k: (k, i))
        b_spec = b.spec(tk, tj, lambda i, j, k: (k, j))
        dims = (((0,), (0,)), ((), ()))
    ex_specs, ex_arrs = [], []
    for op, kind in extras:
        op = _as_op(op)
        if kind == "tile":
            ex_specs.append(op.spec(ti, tj, lambda i, j, k: (i, j)))
        else:
            ex_specs.append(op.spec(1, tj, lambda i, j, k: (0, j)))
        ex_arrs.append(op.arr)
    ne, no = len(ex_arrs), len(out_dtypes)
    if out_nsh is None:
        out_shapes = [jax.ShapeDtypeStruct((M, N), d) for d in out_dtypes]
        out_specs = [pl.BlockSpec((ti, tj), lambda i, j, k: (i, j)) for _ in out_dtypes]
    else:
        per = (N // out_nsh) // tj
        assert per * tj * out_nsh == N
        out_shapes = [jax.ShapeDtypeStruct((out_nsh, M, N // out_nsh), d) for d in out_dtypes]
        out_specs = [pl.BlockSpec((None, ti, tj), lambda i, j, k: (j // per, i, j % per)) for _ in out_dtypes]

    def body(a_ref, b_ref, *rest):
        ex_refs, out_refs, acc = rest[:ne], rest[ne + nd:ne + nd + no], rest[ne + nd + no]
        k = pl.program_id(2)

        @pl.when(k == 0)
        def _():
            acc[...] = jnp.zeros_like(acc)

        av = a_ref[...]
        if a_fn is not None:
            av = a_fn(av)
        acc[...] += lax.dot_general(av.astype(BF16), b_ref[...].astype(BF16), dims,
                                    preferred_element_type=F32)

        @pl.when(k == nk - 1)
        def _():
            res = acc[...]
            outs = epilogue(res, *[r[...] for r in ex_refs]) if epilogue is not None else (res,)
            for o_ref, o in zip(out_refs, outs):
                o_ref[...] = o.astype(o_ref.dtype)

    outs = pl.pallas_call(
        body, name=name, grid=(M // ti, N // tj, nk),
        in_specs=[a_spec, b_spec] + ex_specs + [pl.BlockSpec(memory_space=pl.ANY)] * nd,
        out_specs=out_specs, out_shape=out_shapes,
        scratch_shapes=[pltpu.VMEM((ti, tj), F32)],
        compiler_params=_cparams(("parallel", "parallel", "arbitrary")),
    )(a.arr, b.arr, *ex_arrs, *deps)
    return outs[0] if no == 1 else outs


def _rowwise(fn, ins, outs, accs, *, name, rows, tr=256, deps=()):
    tr = min(tr, rows)
    assert rows % tr == 0
    in_specs, arrs = [], []
    for op, kind, width in ins:
        op = _as_op(op)
        if kind == "tile":
            in_specs.append(op.spec(tr, width, lambda i: (i, 0)))
        else:
            in_specs.append(op.spec(op.rows, width, lambda i: (0, 0)))
        arrs.append(op.arr)
    ni, no, na = len(ins), len(outs), len(accs)
    flipped = [len(o) == 3 for o in outs]
    out_shapes = [jax.ShapeDtypeStruct((o[0], rows) if t else (rows, o[0]), o[1]) for o, t in zip(outs, flipped)]
    out_specs = [pl.BlockSpec((o[0], tr), lambda i: (0, i)) if t else pl.BlockSpec((tr, o[0]), lambda i: (i, 0))
                 for o, t in zip(outs, flipped)]
    out_shapes += [jax.ShapeDtypeStruct((1, w), F32) for w in accs]
    out_specs += [pl.BlockSpec((1, w), lambda i: (0, 0)) for w in accs]

    def body(*refs):
        nd = len(deps)
        in_refs, out_refs, acc_refs = refs[:ni], refs[ni + nd:ni + nd + no], refs[ni + nd + no:]
        res = fn(*[r[...] for r in in_refs])
        if not isinstance(res, (tuple, list)):
            res = (res,)
        for o_ref, r, t in zip(out_refs, res[:no], flipped):
            o_ref[...] = (r.astype(F32).T if t else r).astype(o_ref.dtype)
        if na:
            @pl.when(pl.program_id(0) == 0)
            def _():
                for a_ref in acc_refs:
                    a_ref[...] = jnp.zeros_like(a_ref)
            for a_ref, r in zip(acc_refs, res[no:]):
                a_ref[...] += r.astype(F32)

    res = pl.pallas_call(
        body, name=name, grid=(rows // tr,), in_specs=in_specs + [pl.BlockSpec(memory_space=pl.ANY)] * len(deps),
        out_specs=out_specs, out_shape=out_shapes, compiler_params=_cparams(("arbitrary",)),
    )(*arrs, *deps)
    return res


def _norm_mod(x, g, sh, sc):
    y = x * lax.rsqrt(jnp.mean(x * x, axis=-1, keepdims=True) + EPS) * g
    return y * (1.0 + sc) + sh


def _sigmoid(x):
    return 1.0 / (1.0 + jnp.exp(-x))


def _silu(x):
    return x * _sigmoid(x)


def _gelu(x):
    return 0.5 * x * (1.0 + jnp.tanh(math.sqrt(2.0 / math.pi) * (x + 0.044715 * (x * x * x))))


def _merge(ga, gs, ya, ys):
    ga, gs, ya, ys = (v.astype(F32) for v in (ga, gs, ya, ys))
    return _sigmoid(ga) * ya + _sigmoid(gs) * ys


def _attn_head(q, kp, kc, vp, vc, sink, bias_p, bias_c, not_first):
    nt = (((1,), (1,)), ((), ()))
    nn = (((1,), (0,)), ((), ()))
    qb = q.astype(BF16)
    scale = HEAD_DIM ** -0.5
    sp = lax.dot_general(qb, kp.astype(BF16), nt, preferred_element_type=F32) * scale + bias_p
    sc = lax.dot_general(qb, kc.astype(BF16), nt, preferred_element_type=F32) * scale + bias_c
    qi = lax.broadcasted_iota(jnp.int32, sp.shape, 0) & (BLOCK - 1)
    ki = lax.broadcasted_iota(jnp.int32, sp.shape, 1)
    sp = jnp.where(jnp.logical_and(ki > qi, not_first), sp, NEG_INF)
    sc = jnp.where(ki <= qi, sc, NEG_INF)
    m = jnp.maximum(jnp.maximum(jnp.max(sp, axis=-1, keepdims=True), jnp.max(sc, axis=-1, keepdims=True)), sink)
    m = lax.stop_gradient(m)
    pp = jnp.exp(sp - m)
    pc = jnp.exp(sc - m)
    denom = jnp.sum(pp, axis=-1, keepdims=True) + jnp.sum(pc, axis=-1, keepdims=True) + jnp.exp(sink - m)
    o = lax.dot_general((pp / denom).astype(BF16), vp.astype(BF16), nn, preferred_element_type=F32)
    o = o + lax.dot_general((pc / denom).astype(BF16), vc.astype(BF16), nn, preferred_element_type=F32)
    return o


def _attn_fwd(qh, kh, vh, sinks, bias, name):
    s = qh.shape[1]
    nb = s // BLOCK
    G = GQA_GROUP
    R = G * BLOCK

    def body(q_ref, kp_ref, kc_ref, vp_ref, vc_ref, sink_ref, bias_ref, o_ref):
        not_first = pl.program_id(0) > 0
        for kv in range(N_KV_HEADS):
            hs = slice(kv * G, (kv + 1) * G)
            o = _attn_head(q_ref[hs].reshape(R, HEAD_DIM), kp_ref[kv], kc_ref[kv], vp_ref[kv], vc_ref[kv],
                           sink_ref[kv * R:(kv + 1) * R, 0:1],
                           bias_ref[hs, :, 0:BLOCK].reshape(R, BLOCK), bias_ref[hs, :, BLOCK:2 * BLOCK].reshape(R, BLOCK),
                           not_first)
            o_ref[hs] = o.reshape(G, BLOCK, HEAD_DIM).astype(o_ref.dtype)

    cur = lambda i: (0, i, 0)
    prev = lambda i: (0, jnp.maximum(i - 1, 0), 0)
    return pl.pallas_call(
        body, name=name, grid=(nb,),
        in_specs=[pl.BlockSpec((N_Q_HEADS, BLOCK, HEAD_DIM), cur),
                  pl.BlockSpec((N_KV_HEADS, BLOCK, HEAD_DIM), prev), pl.BlockSpec((N_KV_HEADS, BLOCK, HEAD_DIM), cur),
                  pl.BlockSpec((N_KV_HEADS, BLOCK, HEAD_DIM), prev), pl.BlockSpec((N_KV_HEADS, BLOCK, HEAD_DIM), cur),
                  pl.BlockSpec((N_Q_HEADS * BLOCK, 128), lambda i: (0, 0)),
                  pl.BlockSpec((N_Q_HEADS, BLOCK, 2 * BLOCK), lambda i: (0, 0, 0))],
        out_specs=pl.BlockSpec((N_Q_HEADS, BLOCK, HEAD_DIM), cur),
        out_shape=jax.ShapeDtypeStruct((N_Q_HEADS, s, HEAD_DIM), BF16),
        compiler_params=_cparams(("arbitrary",)),
    )(qh, kh, kh, vh, vh, sinks, bias)


def _attn_bwd(qh, kh, vh, doh, sinks, bias, name):
    s = qh.shape[1]
    nb = s // BLOCK
    G = GQA_GROUP
    R = G * BLOCK

    def body(q_ref, kp_ref, kc_ref, vp_ref, vc_ref, do_ref, sink_ref, bias_ref,
             dq_ref, dk_ref, dv_ref, dsink_ref, dbias_ref, ck, cv):
        i = pl.program_id(1)

        @pl.when(i == 0)
        def _():
            dsink_ref[...] = jnp.zeros_like(dsink_ref)
            dbias_ref[...] = jnp.zeros_like(dbias_ref)
            ck[...] = jnp.zeros_like(ck)
            cv[...] = jnp.zeros_like(cv)

        @pl.when(i < nb)
        def _():
            not_first = i > 0
            _, vjp = jax.vjp(lambda q, a, b, c, d, sk, e, f: _attn_head(q, a, b, c, d, sk, e, f, not_first),
                             q_ref[...].astype(F32).reshape(R, HEAD_DIM), kp_ref[...].astype(F32),
                             kc_ref[...].astype(F32), vp_ref[...].astype(F32), vc_ref[...].astype(F32),
                             sink_ref[:, 0:1], bias_ref[:, :, 0:BLOCK].reshape(R, BLOCK),
                             bias_ref[:, :, BLOCK:2 * BLOCK].reshape(R, BLOCK))
            dq, dkp, dkc, dvp, dvc, dsk, dbp, dbc = vjp(do_ref[...].reshape(R, HEAD_DIM).astype(F32))
            dq_ref[...] = dq.reshape(G, BLOCK, HEAD_DIM).astype(dq_ref.dtype)
            dsink_ref[...] += jnp.broadcast_to(dsk, (R, 128))
            dbias_ref[:, :, 0:BLOCK] += dbp.reshape(G, BLOCK, BLOCK)
            dbias_ref[:, :, BLOCK:2 * BLOCK] += dbc.reshape(G, BLOCK, BLOCK)
            dk_ref[...] = (ck[...] + dkp).astype(dk_ref.dtype)
            dv_ref[...] = (cv[...] + dvp).astype(dv_ref.dtype)
            ck[...] = dkc
            cv[...] = dvc

        @pl.when(i == nb)
        def _():
            dk_ref[...] = ck[...].astype(dk_ref.dtype)
            dv_ref[...] = cv[...].astype(dv_ref.dtype)

    qcur = lambda kv, i: (kv, jnp.minimum(i, nb - 1), 0)
    kcur = lambda kv, i: (kv, jnp.minimum(i, nb - 1), 0)
    kprev = lambda kv, i: (kv, jnp.clip(i - 1, 0, nb - 1), 0)
    qspec = pl.BlockSpec((G, BLOCK, HEAD_DIM), qcur)
    kc_spec = pl.BlockSpec((None, BLOCK, HEAD_DIM), kcur)
    kp_spec = pl.BlockSpec((None, BLOCK, HEAD_DIM), kprev)
    return pl.pallas_call(
        body, name=name, grid=(N_KV_HEADS, nb + 1),
        in_specs=[qspec, kp_spec, kc_spec, kp_spec, kc_spec, qspec,
                  pl.BlockSpec((R, 128), lambda kv, i: (kv, 0)),
                  pl.BlockSpec((G, BLOCK, 2 * BLOCK), lambda kv, i: (kv, 0, 0))],
        out_specs=[qspec, kp_spec, kp_spec,
                   pl.BlockSpec((R, 128), lambda kv, i: (kv, 0)),
                   pl.BlockSpec((G, BLOCK, 2 * BLOCK), lambda kv, i: (kv, 0, 0))],
        out_shape=[jax.ShapeDtypeStruct((N_Q_HEADS, s, HEAD_DIM), BF16),
                   jax.ShapeDtypeStruct((N_KV_HEADS, s, HEAD_DIM), BF16),
                   jax.ShapeDtypeStruct((N_KV_HEADS, s, HEAD_DIM), BF16),
                   jax.ShapeDtypeStruct((N_Q_HEADS * BLOCK, 128), F32),
                   jax.ShapeDtypeStruct((N_Q_HEADS, BLOCK, 2 * BLOCK), F32)],
        scratch_shapes=[pltpu.VMEM((BLOCK, HEAD_DIM), F32), pltpu.VMEM((BLOCK, HEAD_DIM), F32)],
        compiler_params=_cparams(("arbitrary", "arbitrary")),
    )(qh, kh, kh, vh, vh, doh, sinks, bias)


def _cmul(ar, ai, br, bi):
    return ar * br - ai * bi, ar * bi + ai * br


def _scan_passes(a_ref, b_ref, x_ref, xp_ref, da_ref, *, s, tc, reverse):
    nc = SCAN_CHUNKS
    steps = s // nc
    with_da = xp_ref is not None
    unroll = 8 if steps % 8 == 0 else 1

    def shift(v, d):
        row = lax.broadcasted_iota(jnp.int32, v.shape, 0)
        if reverse:
            return jnp.where(row < nc - d, pltpu.roll(v, nc - d, 0), 0.0)
        return jnp.where(row >= d, pltpu.roll(v, d, 0), 0.0)

    def run():
        ar = jnp.broadcast_to(a_ref[0], (nc, tc))
        ai = jnp.broadcast_to(a_ref[1], (nc, tc))

        def row_of(step):
            j = (steps - 1 - step) if reverse else step
            return pl.multiple_of(j * nc, nc)

        def p1(step, st):
            sr, si = st
            r0 = row_of(step)
            mr, mi = _cmul(ar, ai, sr, si)
            sr = mr + b_ref[0, pl.ds(r0, nc), :]
            si = mi + b_ref[1, pl.ds(r0, nc), :]
            x_ref[0, pl.ds(r0, nc), :] = sr
            x_ref[1, pl.ds(r0, nc), :] = si
            return sr, si
        zero = jnp.zeros((nc, tc), F32)
        er, ei = lax.fori_loop(0, steps, p1, (zero, zero), unroll=unroll)

        pr, pi_ = jnp.ones((nc, tc), F32), zero
        br, bi, left = ar, ai, steps
        while left:
            if left & 1:
                pr, pi_ = _cmul(pr, pi_, br, bi)
            br, bi = _cmul(br, bi, br, bi)
            left >>= 1
        cr, ci = shift(er, 1), shift(ei, 1)
        d = 1
        while d < nc:
            mr, mi = _cmul(pr, pi_, shift(cr, d), shift(ci, d))
            cr, ci = cr + mr, ci + mi
            pr, pi_ = _cmul(pr, pi_, pr, pi_)
            d *= 2

        def p2(step, st):
            qr, qi, dar, dai = st
            r0 = row_of(step)
            qr, qi = _cmul(ar, ai, qr, qi)
            fr, fi = _cmul(qr, qi, cr, ci)
            xr = x_ref[0, pl.ds(r0, nc), :] + fr
            xi = x_ref[1, pl.ds(r0, nc), :] + fi
            x_ref[0, pl.ds(r0, nc), :] = xr
            x_ref[1, pl.ds(r0, nc), :] = xi
            if with_da:
                jm = jnp.where(step == steps - 1, steps - 1, steps - 2 - step)
                rp = pl.multiple_of(jm * nc, nc)
                vr, vi = xp_ref[0, pl.ds(rp, nc), :], xp_ref[1, pl.ds(rp, nc), :]
                row = lax.broadcasted_iota(jnp.int32, (nc, tc), 0)
                first = step == steps - 1
                sel = jnp.logical_and(first, row == 0)
                vr = jnp.where(sel, 0.0, jnp.where(first, pltpu.roll(vr, 1, 0), vr))
                vi = jnp.where(sel, 0.0, jnp.where(first, pltpu.roll(vi, 1, 0), vi))
                dar = dar + xr * vr + xi * vi
                dai = dai + xi * vr - xr * vi
            return qr, qi, dar, dai
        _, _, dar, dai = lax.fori_loop(0, steps, p2, (jnp.ones((nc, tc), F32), zero, zero, zero), unroll=unroll)
        if with_da:
            da_ref[0] = jnp.sum(dar, axis=0, keepdims=True)
            da_ref[1] = jnp.sum(dai, axis=0, keepdims=True)

    run()


def _ssm_fwd(u, bd, cd, a, d_row, *, name, sb, sbn):
    s, w = u.shape
    nst = a.shape[2]
    nblk = w // sb
    rows = min(512, s)
    nn = (((1,), (0,)), ((), ()))

    def body(u_ref, bre_ref, bim_ref, cre_ref, cim_ref, a_ref, d_ref, y_ref, x_ref):

        def fill(r, carry):
            r0 = pl.multiple_of(r * rows, rows)
            ub = u_ref[pl.ds(r0, rows), :].astype(BF16)
            x_ref[0, pl.ds(r0, rows), :] = lax.dot_general(ub, bre_ref[...].astype(BF16), nn, preferred_element_type=F32)
            x_ref[1, pl.ds(r0, rows), :] = lax.dot_general(ub, bim_ref[...].astype(BF16), nn, preferred_element_type=F32)
            return carry
        lax.fori_loop(0, s // rows, fill, 0)
        _scan_passes(a_ref, x_ref, x_ref, None, None, s=s, tc=sbn, reverse=False)

        def project(r, carry):
            r0 = pl.multiple_of(r * rows, rows)
            y = lax.dot_general(x_ref[0, pl.ds(r0, rows), :].astype(BF16), cre_ref[...].astype(BF16), nn, preferred_element_type=F32)
            y = y + lax.dot_general(x_ref[1, pl.ds(r0, rows), :].astype(BF16), cim_ref[...].astype(BF16), nn, preferred_element_type=F32)
            y_ref[pl.ds(r0, rows), :] = y + d_ref[...] * u_ref[pl.ds(r0, rows), :]
            return carry
        lax.fori_loop(0, s // rows, project, 0)

    return pl.pallas_call(
        body, name=name, grid=(nblk,),
        in_specs=[pl.BlockSpec((s, sb), lambda j: (0, j)),
                  pl.BlockSpec((sb, sbn), lambda j: (j, j)), pl.BlockSpec((sb, sbn), lambda j: (j, nblk + j)),
                  pl.BlockSpec((sbn, sb), lambda j: (j, j)), pl.BlockSpec((sbn, sb), lambda j: (nblk + j, j)),
                  pl.BlockSpec((2, 1, sbn), lambda j: (0, 0, j)), pl.BlockSpec((1, sb), lambda j: (0, j))],
        out_specs=[pl.BlockSpec((s, sb), lambda j: (0, j)), pl.BlockSpec((2, s, sbn), lambda j: (0, 0, j))],
        out_shape=[jax.ShapeDtypeStruct((s, w), F32), jax.ShapeDtypeStruct((2, s, nst), F32)],
        compiler_params=pltpu.CompilerParams(dimension_semantics=("arbitrary",), vmem_limit_bytes=SSM_VMEM_LIMIT_BYTES),
    )(u, bd, bd, cd, cd, a, d_row)


def _ssm_bwd(dy, u, xs, bd, cd, a, d_row, *, name, sb, sbn, deps=()):
    s, w = u.shape
    nst = a.shape[2]
    nblk = w // sb
    rows = min(512, s)
    nt = (((1,), (1,)), ((), ()))
    tn = (((0,), (0,)), ((), ()))

    def body(dy_ref, u_ref, xs_hbm, bre_ref, bim_ref, cre_ref, cim_ref, a_ref, d_ref, *rest):
        du_ref, gb_ref, gc_ref, da_ref, gd_ref, lam, xs_ref, sem = rest[len(deps):]
        j = pl.program_id(0)
        fetch = pltpu.make_async_copy(xs_hbm.at[:, :, pl.ds(pl.multiple_of(j * sbn, sbn), sbn)], xs_ref, sem)
        fetch.start()

        def fill(r, carry):
            r0 = pl.multiple_of(r * rows, rows)
            dyb = dy_ref[pl.ds(r0, rows), :].astype(BF16)
            lam[0, pl.ds(r0, rows), :] = lax.dot_general(dyb, cre_ref[...].astype(BF16), nt, preferred_element_type=F32)
            lam[1, pl.ds(r0, rows), :] = lax.dot_general(dyb, cim_ref[...].astype(BF16), nt, preferred_element_type=F32)
            return carry
        lax.fori_loop(0, s // rows, fill, 0)
        fetch.wait()
        _scan_passes(a_ref, lam, lam, xs_ref, da_ref, s=s, tc=sbn, reverse=True)
        gb_ref[...] = jnp.zeros_like(gb_ref)
        gc_ref[...] = jnp.zeros_like(gc_ref)
        gd_ref[...] = jnp.zeros_like(gd_ref)

        def project(r, carry):
            r0 = pl.multiple_of(r * rows, rows)
            dyv, uv = dy_ref[pl.ds(r0, rows), :], u_ref[pl.ds(r0, rows), :]
            dyb, ub = dyv.astype(BF16), uv.astype(BF16)
            lr, li = lam[0, pl.ds(r0, rows), :].astype(BF16), lam[1, pl.ds(r0, rows), :].astype(BF16)
            du = lax.dot_general(lr, bre_ref[...].astype(BF16), nt, preferred_element_type=F32)
            du = du + lax.dot_general(li, bim_ref[...].astype(BF16), nt, preferred_element_type=F32)
            du_ref[pl.ds(r0, rows), :] = du + d_ref[...] * dyv
            gb_ref[:, 0:sbn] += lax.dot_general(ub, lr, tn, preferred_element_type=F32)
            gb_ref[:, sbn:2 * sbn] += lax.dot_general(ub, li, tn, preferred_element_type=F32)
            gc_ref[0] += lax.dot_general(xs_ref[0, pl.ds(r0, rows), :].astype(BF16), dyb, tn, preferred_element_type=F32)
            gc_ref[1] += lax.dot_general(xs_ref[1, pl.ds(r0, rows), :].astype(BF16), dyb, tn, preferred_element_type=F32)
            gd_ref[...] += jnp.sum(dyv * uv, axis=0, keepdims=True)
            return carry
        lax.fori_loop(0, s // rows, project, 0)

    col = lambda j: (0, j)
    return pl.pallas_call(
        body, name=name, grid=(nblk,),
        in_specs=[pl.BlockSpec((s, sb), col), pl.BlockSpec((s, sb), col), pl.BlockSpec(memory_space=pl.ANY),
                  pl.BlockSpec((sb, sbn), lambda j: (j, j)), pl.BlockSpec((sb, sbn), lambda j: (j, nblk + j)),
                  pl.BlockSpec((sbn, sb), lambda j: (j, j)), pl.BlockSpec((sbn, sb), lambda j: (nblk + j, j)),
                  pl.BlockSpec((2, 1, sbn), lambda j: (0, 0, j)), pl.BlockSpec((1, sb), col)]
        + [pl.BlockSpec(memory_space=pl.ANY)] * len(deps),
        out_specs=[pl.BlockSpec((s, sb), col), pl.BlockSpec((sb, 2 * sbn), lambda j: (j, 0)),
                   pl.BlockSpec((2, sbn, sb), lambda j: (0, j, 0)), pl.BlockSpec((2, 1, sbn), lambda j: (0, 0, j)),
                   pl.BlockSpec((1, sb), col)],
        out_shape=[jax.ShapeDtypeStruct((s, w), F32), jax.ShapeDtypeStruct((w, 2 * sbn), F32),
                   jax.ShapeDtypeStruct((2, nst, sb), F32), jax.ShapeDtypeStruct((2, 1, nst), F32),
                   jax.ShapeDtypeStruct((1, w), F32)],
        scratch_shapes=[pltpu.VMEM((2, s, sbn), F32), pltpu.VMEM((2, s, sbn), F32), pltpu.SemaphoreType.DMA],
        compiler_params=pltpu.CompilerParams(dimension_semantics=("arbitrary",), vmem_limit_bytes=SSM_VMEM_LIMIT_BYTES),
    )(dy, u, xs, bd, bd, cd, cd, a, d_row, *deps)


def _adamw_math(w, g, m, v):
    nm = ADAM_B1 * m + (1.0 - ADAM_B1) * g
    nv = ADAM_B2 * v + (1.0 - ADAM_B2) * (g * g)
    m_hat = nm / (1.0 - ADAM_B1 ** ADAM_STEP)
    v_hat = nv / (1.0 - ADAM_B2 ** ADAM_STEP)
    return -ADAM_LR * (m_hat / (jnp.sqrt(v_hat) + ADAM_EPS) + ADAM_WD * w), nm, nv


def _adamw_many(ws, gs, ms, vs, name, deps=()):
    n, nd = len(ws), len(deps)

    def body(*refs):
        outs = refs[4 * n + nd:]
        for i in range(n):
            d, nm, nv = _adamw_math(refs[i][...], refs[n + i][...], refs[2 * n + i][...], refs[3 * n + i][...])
            outs[i][...], outs[n + i][...], outs[2 * n + i][...] = d, nm, nv

    whole = pl.BlockSpec(memory_space=pltpu.VMEM)
    res = pl.pallas_call(
        body, name=name, in_specs=[whole] * (4 * n) + [pl.BlockSpec(memory_space=pl.ANY)] * nd,
        out_specs=[whole] * (3 * n), out_shape=[jax.ShapeDtypeStruct(w.shape, F32) for w in ws] * 3,
        compiler_params=pltpu.CompilerParams(vmem_limit_bytes=VMEM_LIMIT_BYTES),
    )(*ws, *gs, *ms, *vs, *deps)
    return res[:n], res[n:2 * n], res[2 * n:]


def _adamw(w, g, m, v, name, deps=(), echo=False):
    nd = len(deps)
    r, c = w.shape
    tr = r
    for cand in (512, 256, 128, 64, 32, 16, 8):
        if r % cand == 0 and cand * c * 4 <= 2 * 1024 * 1024:
            tr = cand
            break

    def body(w_ref, g_ref, m_ref, v_ref, *rest):
        d_ref, nm_ref, nv_ref = rest[nd:nd + 3]
        gv = g_ref[...]
        d_ref[...], nm_ref[...], nv_ref[...] = _adamw_math(w_ref[...], gv, m_ref[...], v_ref[...])
        if echo:
            rest[nd + 3][...] = gv

    no = 4 if echo else 3
    spec = pl.BlockSpec((tr, c), lambda i: (i, 0))
    sds = jax.ShapeDtypeStruct((r, c), F32)
    return pl.pallas_call(body, name=name, grid=(r // tr,),
                          in_specs=[spec] * 4 + [pl.BlockSpec(memory_space=pl.ANY)] * nd, out_specs=[spec] * no,
                          out_shape=[sds] * no, compiler_params=_cparams(("parallel",)))(w, g, m, v, *deps)


def _sum_lead(x, name, out_dtype=F32):
    n, r, c = x.shape
    tr = r
    for cand in (512, 256, 128, 64, 32, 16, 8):
        if r % cand == 0 and n * cand * c * 4 <= 4 * 1024 * 1024:
            tr = cand
            break

    def body(x_ref, o_ref):
        acc = x_ref[0].astype(F32)
        for k in range(1, n):
            acc = acc + x_ref[k].astype(F32)
        o_ref[...] = acc.astype(o_ref.dtype)

    return pl.pallas_call(body, name=name, grid=(r // tr,),
                          in_specs=[pl.BlockSpec((n, tr, c), lambda i: (0, i, 0))],
                          out_specs=pl.BlockSpec((tr, c), lambda i: (i, 0)),
                          out_shape=jax.ShapeDtypeStruct((r, c), out_dtype),
                          compiler_params=_cparams(("parallel",)))(x)


def _row_tile(rows, row_bytes, budget, least=8):
    for cand in (1024, 512, 256, 128, 64, 32, 16, 8):
        if cand >= least and rows % cand == 0 and cand * row_bytes <= budget:
            return cand
    return rows


def _cast_into_slot(w, slot, name):
    r, c = w.shape
    tr = _row_tile(r, c * 4, 4 * 1024 * 1024, least=16)

    def body(slot_ref, w_ref, o_ref):
        o_ref[...] = w_ref[...].astype(o_ref.dtype)

    gs = pltpu.PrefetchScalarGridSpec(
        num_scalar_prefetch=1, grid=(r // tr,),
        in_specs=[pl.BlockSpec((tr, c), lambda i, s: (i, 0))],
        out_specs=pl.BlockSpec((None, tr, c), lambda i, s: (s[0], i, 0)))
    return pl.pallas_call(body, name=name, grid_spec=gs, out_shape=jax.ShapeDtypeStruct((N_CHIPS, r, c), BF16),
                          compiler_params=_cparams(("parallel",)))(slot, w)


def _sum_own(p, t, sel, name):
    _, h, c = p.shape
    tr = _row_tile(h, c * 4, 2 * 1024 * 1024, least=16)
    nblk = h // tr

    def body(sel_ref, p_ref, t_ref, o_ref):
        acc = p_ref[...].astype(F32)
        for k in range(3):
            acc = acc + t_ref[k].astype(F32)
        o_ref[...] = acc

    gs = pltpu.PrefetchScalarGridSpec(
        num_scalar_prefetch=1, grid=(nblk,),
        in_specs=[pl.BlockSpec((None, tr, c), lambda i, s: (s[0], i, 0)),
                  pl.BlockSpec((3, tr, c), lambda i, s: (0, i, 0))],
        out_specs=pl.BlockSpec((tr, c), lambda i, s: (s[1] * nblk + i, 0)))
    return pl.pallas_call(body, name=name, grid_spec=gs, out_shape=jax.ShapeDtypeStruct((2 * h, c), F32),
                          compiler_params=_cparams(("parallel",)))(sel, p, t)


def _add_half(g, t, half, name):
    n, r, c = g.shape
    h = r // 2
    tr = h
    for cand in (512, 256, 128, 64, 32, 16):
        if h % cand == 0 and cand * c * 2 <= 2 * 1024 * 1024:
            tr = cand
            break
    nblk = h // tr

    def body(half_ref, g_ref, t_ref, o_ref):
        o_ref[...] = (g_ref[...].astype(F32) + t_ref[...].astype(F32)).astype(o_ref.dtype)

    gs = pltpu.PrefetchScalarGridSpec(
        num_scalar_prefetch=1, grid=(n, nblk),
        in_specs=[pl.BlockSpec((None, tr, c), lambda j, i, hr: (j, hr[0] * nblk + i, 0)),
                  pl.BlockSpec((None, tr, c), lambda j, i, hr: (j, i, 0))],
        out_specs=pl.BlockSpec((None, tr, c), lambda j, i, hr: (j, i, 0)))
    return pl.pallas_call(body, name=name, grid_spec=gs, out_shape=jax.ShapeDtypeStruct((n, h, c), BF16),
                          compiler_params=_cparams(("parallel", "parallel")))(half, g, t)


def _position():
    x, y, c = lax.axis_index("x"), lax.axis_index("y"), lax.axis_index("c")
    return x, y, c


def _allgather8(xs, name):
    m_per, n = xs.shape

    def body(x_ref, out_ref, send_sems, recv_sems, local_sem):
        x, y, c = _position()
        me, sibling = (x, y, c), (x, y, 1 - c)
        chips = [(1 - x, y), (x, 1 - y), (1 - x, 1 - y)]

        def rows(px, py, pc):
            return out_ref.at[pl.ds((4 * px + 2 * py + pc) * m_per, m_per), :]

        def copy(k, block, to, src=None):
            return pltpu.make_async_remote_copy(
                src_ref=rows(*block) if src is None else src, dst_ref=rows(*block),
                send_sem=send_sems.at[k], recv_sem=recv_sems.at[k], device_id=to, device_id_type=MESH)

        mine = pltpu.make_async_copy(x_ref, rows(*me), local_sem)
        mine.start()
        first = [copy(0, me, sibling, src=x_ref)]
        first += [copy(1 + j, me, (*chip, c), src=x_ref) for j, chip in enumerate(chips)]
        for cp in first:
            cp.start()
        passed = [copy(4 + j, (*chip, c), sibling) for j, chip in enumerate(chips)]
        for j, chip in enumerate(chips):
            copy(1 + j, (*chip, c), me).wait_recv()
            passed[j].start()
        copy(0, sibling, me).wait_recv()
        for j, chip in enumerate(chips):
            copy(4 + j, (*chip, 1 - c), me).wait_recv()
        for cp in first + passed:
            cp.wait_send()
        mine.wait()

    return pl.pallas_call(
        body, name=name, out_shape=jax.ShapeDtypeStruct((N_DEV * m_per, n), xs.dtype),
        in_specs=[pl.BlockSpec(memory_space=pltpu.VMEM)], out_specs=pl.BlockSpec(memory_space=pltpu.VMEM),
        scratch_shapes=[pltpu.SemaphoreType.DMA((7,)), pltpu.SemaphoreType.DMA((7,)), pltpu.SemaphoreType.DMA],
        compiler_params=pltpu.CompilerParams(vmem_limit_bytes=VMEM_LIMIT_BYTES),
    )(xs)


_HBM = pl.BlockSpec(memory_space=pltpu.HBM)


_SEM = pl.BlockSpec(memory_space=pltpu.SEMAPHORE)
_ANY = pl.BlockSpec(memory_space=pl.ANY)
_EFFECT = pltpu.SideEffectType.DATAFLOW_SIDE_EFFECTING


def _in_hbm(a):
    return pltpu.with_memory_space_constraint(a, pltpu.HBM)


def _several(after):
    return list(after) if isinstance(after, (list, tuple)) else [after]


def _gather_start(ws, groups, after, name):
    n = len(ws)
    after = _several(after)

    def body(*refs):
        in_refs = refs[:n]
        sems, token = refs[2 * n + len(after):-1], refs[-1]
        x, y, c = _position()
        mychip = 2 * x + y
        chips = [(1 - x, y), (x, 1 - y), (1 - x, 1 - y)]
        for g, members in enumerate(groups):
            for k, i in enumerate(members):
                h = ws[i].shape[1] // 2
                mine = in_refs[i].at[mychip, pl.ds(c * h, h), :]
                for j, (px, py) in enumerate(chips):
                    pltpu.make_async_remote_copy(
                        src_ref=mine, dst_ref=mine, send_sem=sems[2 * g].at[3 * k + j],
                        recv_sem=sems[2 * g + 1].at[3 * k + j], device_id=(px, py, c), device_id_type=MESH).start()
        token[...] = jnp.zeros_like(token)

    sem_shapes = [pltpu.SemaphoreType.DMA((3 * len(m),)) for m in groups for _ in range(2)]
    res = pl.pallas_call(
        body, name=name,
        out_shape=[pltpu.HBM(w.shape, w.dtype) for w in ws] + sem_shapes + [jax.ShapeDtypeStruct((8, 128), F32)],
        in_specs=[_HBM] * n + [_ANY] * len(after),
        out_specs=[_HBM] * n + [_SEM] * len(sem_shapes) + [pl.BlockSpec(memory_space=pltpu.VMEM)],
        input_output_aliases={i: i for i in range(n)},
        compiler_params=pltpu.CompilerParams(has_side_effects=_EFFECT),
    )(*[_in_hbm(w) for w in ws], *after)
    bufs, sems, token = res[:n], res[n:-1], res[-1]
    return list(bufs), [(sems[2 * g], sems[2 * g + 1]) for g in range(len(groups))], token


def _gather_wait(bufs, send_sems, recv_sems, after, name):
    m = len(bufs)

    def body(*refs):
        in_refs = refs[:m]
        send, recv = refs[m], refs[m + 1]
        x, y, c = _position()
        mychip = 2 * x + y
        chips = [(1 - x, y), (x, 1 - y), (1 - x, 1 - y)]
        for k in range(m):
            h = bufs[k].shape[1] // 2
            mine = in_refs[k].at[mychip, pl.ds(c * h, h), :]
            for j, (px, py) in enumerate(chips):
                cp = pltpu.make_async_remote_copy(
                    src_ref=mine, dst_ref=in_refs[k].at[2 * px + py, pl.ds(c * h, h), :],
                    send_sem=send.at[3 * k + j], recv_sem=recv.at[3 * k + j],
                    device_id=(px, py, c), device_id_type=MESH)
                cp.wait_send()
                cp.wait_recv()

    res = pl.pallas_call(
        body, name=name, out_shape=[pltpu.HBM(b.shape, b.dtype) for b in bufs],
        in_specs=[_HBM] * m + [_SEM, _SEM] + [_ANY] * len(_several(after)), out_specs=[_HBM] * m,
        input_output_aliases={k: k for k in range(m)},
        compiler_params=pltpu.CompilerParams(has_side_effects=_EFFECT),
    )(*bufs, send_sems, recv_sems, *_several(after))
    return list(res)


def _forward_halves(ws, name):
    n = len(ws)

    def body(*refs):
        out_refs = refs[n:2 * n]
        send_sems, recv_sems = refs[2 * n:]
        x, y, c = _position()
        me, sibling = (x, y, c), (x, y, 1 - c)
        chips = [(1 - x, y), (x, 1 - y), (1 - x, 1 - y)]
        cps = []
        for i in range(n):
            h = ws[i].shape[1] // 2
            for j, (px, py) in enumerate(chips):
                got = out_refs[i].at[2 * px + py, pl.ds(c * h, h), :]
                cp = pltpu.make_async_remote_copy(
                    src_ref=got, dst_ref=got, send_sem=send_sems.at[3 * i + j], recv_sem=recv_sems.at[3 * i + j],
                    device_id=sibling, device_id_type=MESH)
                cp.start()
                cps.append(cp)
        for i in range(n):
            h = ws[i].shape[1] // 2
            for j, (px, py) in enumerate(chips):
                other = out_refs[i].at[2 * px + py, pl.ds((1 - c) * h, h), :]
                pltpu.make_async_remote_copy(
                    src_ref=other, dst_ref=other, send_sem=send_sems.at[3 * i + j], recv_sem=recv_sems.at[3 * i + j],
                    device_id=me, device_id_type=MESH).wait_recv()
        for cp in cps:
            cp.wait_send()

    return pl.pallas_call(
        body, name=name,
        out_shape=[jax.ShapeDtypeStruct(w.shape, w.dtype) for w in ws],
        in_specs=[_HBM] * n, out_specs=[_HBM] * n, input_output_aliases={i: i for i in range(n)},
        scratch_shapes=[pltpu.SemaphoreType.DMA((3 * n,)), pltpu.SemaphoreType.DMA((3 * n,))],
    )(*ws)


def _copies_start(arrays, copies, nsem, after, name):
    n = len(arrays)
    after = _several(after)
    first = 2 * n + len(after)

    def body(*refs):
        for cp in copies(refs[:n], refs[first], refs[first + 1]):
            cp.start()
        refs[first + 2][...] = jnp.zeros_like(refs[first + 2])

    res = pl.pallas_call(
        body, name=name,
        out_shape=[pltpu.HBM(a.shape, a.dtype) for a in arrays]
        + [pltpu.SemaphoreType.DMA((nsem,)), pltpu.SemaphoreType.DMA((nsem,)), jax.ShapeDtypeStruct((8, 128), F32)],
        in_specs=[_HBM] * n + [_ANY] * len(after),
        out_specs=[_HBM] * n + [_SEM, _SEM, pl.BlockSpec(memory_space=pltpu.VMEM)],
        input_output_aliases={i: i for i in range(n)},
        compiler_params=pltpu.CompilerParams(has_side_effects=_EFFECT),
    )(*[_in_hbm(a) for a in arrays], *after)
    return list(res[:n]), res[n], res[n + 1], res[n + 2]


def _copies_wait(arrays, copies, send_sems, recv_sems, after, name):
    n = len(arrays)

    def body(*refs):
        for cp in copies(refs[:n], refs[n], refs[n + 1]):
            cp.wait_send()
            cp.wait_recv()

    res = pl.pallas_call(
        body, name=name, out_shape=[pltpu.HBM(a.shape, a.dtype) for a in arrays],
        in_specs=[_HBM] * n + [_SEM, _SEM] + [_ANY] * len(_several(after)), out_specs=[_HBM] * n,
        input_output_aliases={i: i for i in range(n)},
        compiler_params=pltpu.CompilerParams(has_side_effects=_EFFECT),
    )(*arrays, send_sems, recv_sems, *_several(after))
    return list(res)


def _scatter_copies(refs, send, recv):
    n = len(refs) // 2
    x, y, c = _position()
    chips = [(1 - x, y), (x, 1 - y), (1 - x, 1 - y)]
    return [pltpu.make_async_remote_copy(
        src_ref=refs[i].at[2 * px + py], dst_ref=refs[n + i].at[j],
        send_sem=send.at[3 * i + j], recv_sem=recv.at[3 * i + j], device_id=(px, py, c), device_id_type=MESH)
        for i in range(n) for j, (px, py) in enumerate(chips)]


def _swap_copies(refs, send, recv):
    n = len(refs) // 2
    x, y, c = _position()
    cps = []
    for i in range(n):
        h = refs[i].shape[1] // 2
        cps.append(pltpu.make_async_remote_copy(
            src_ref=refs[i].at[:, pl.ds((1 - c) * h, h), :], dst_ref=refs[n + i],
            send_sem=send.at[i], recv_sem=recv.at[i], device_id=(x, y, 1 - c), device_id_type=MESH))
    return cps


def _join_copies(refs, send, recv):
    x, y, c = _position()
    cps = []
    for i, r in enumerate(refs):
        h = r.shape[0] // 2
        mine = r.at[pl.ds(c * h, h), :]
        cps.append(pltpu.make_async_remote_copy(
            src_ref=mine, dst_ref=mine, send_sem=send.at[i], recv_sem=recv.at[i],
            device_id=(x, y, 1 - c), device_id_type=MESH))
    return cps


def _forward_copies(refs, send, recv):
    x, y, c = _position()
    chips = [(1 - x, y), (x, 1 - y), (1 - x, 1 - y)]
    cps = []
    for i, r in enumerate(refs):
        h = r.shape[1] // 2
        for j, (px, py) in enumerate(chips):
            got = r.at[2 * px + py, pl.ds(c * h, h), :]
            cps.append(pltpu.make_async_remote_copy(
                src_ref=got, dst_ref=got, send_sem=send.at[3 * i + j], recv_sem=recv.at[3 * i + j],
                device_id=(x, y, 1 - c), device_id_type=MESH))
    return cps


def _t5_buckets_block():
    qi = np.arange(BLOCK)[:, None]
    ki = np.arange(2 * BLOCK)[None, :]
    n = np.maximum(qi + BLOCK - ki, 0)
    max_exact = NUM_BUCKETS // 2
    large = max_exact + (np.log(np.maximum(n, 1) / max_exact) / np.log(MAX_DISTANCE / max_exact)
                         * (NUM_BUCKETS - max_exact)).astype(np.int32)
    large = np.minimum(large, NUM_BUCKETS - 1)
    return np.where(n < max_exact, n, large).astype(np.int32)


def _discretise(lambda_re, lambda_im, log_step, b_re, b_im):
    lam_re = jnp.minimum(lambda_re, -1e-4)
    lam_im = lambda_im
    delta = jnp.exp(log_step)
    mag = jnp.exp(lam_re * delta)
    ang = lam_im * delta
    abar_re, abar_im = mag * jnp.cos(ang), mag * jnp.sin(ang)
    num_re, num_im = abar_re - 1.0, abar_im
    den = lam_re * lam_re + lam_im * lam_im
    f_re = (num_re * lam_re + num_im * lam_im) / den
    f_im = (num_im * lam_re - num_re * lam_im) / den
    bbar_re = f_re * b_re - f_im * b_im
    bbar_im = f_re * b_im + f_im * b_re
    return abar_re, abar_im, bbar_re, bbar_im


def _ssm_params(args, cotangents, name):
    whole = pl.BlockSpec(memory_space=pltpu.VMEM)
    n_in = len(args)

    def body(*refs):
        vals = [r[...] for r in refs[:n_in]]
        if cotangents is None:
            outs = _discretise(*vals)
        else:
            outs = jax.vjp(_discretise, *vals)[1](tuple(r[...] for r in refs[n_in:n_in + 4]))
        for o_ref, o in zip(refs[-len(outs):], outs):
            o_ref[...] = o

    if cotangents is None:
        like, operands = [args[0], args[0], args[3], args[3]], list(args)
    else:
        like, operands = list(args), list(args) + list(cotangents)
    return pl.pallas_call(body, name=name, in_specs=[whole] * len(operands), out_specs=[whole] * len(like),
                          out_shape=[jax.ShapeDtypeStruct(a.shape, F32) for a in like])(*operands)


def _interleave(v, nc):
    s, w = v.shape
    return v.reshape(nc, s // nc, w).transpose(1, 0, 2).reshape(s, w)


def _deinterleave(v, nc):
    s, w = v.shape
    return v.reshape(s // nc, nc, w).transpose(1, 0, 2).reshape(s, w)


_SMALL = ("norm1_g", "b_in", "attn_sinks", "rel_bias", "lambda_re", "lambda_im", "log_step", "ssm_b_re",
          "ssm_b_im", "ssm_c_re", "ssm_c_im", "ssm_d", "b_glu", "norm2_g", "final_g")


def _pack(parts):
    rows = []
    for p in parts:
        f = p.reshape(-1).astype(F32)
        pad = (-f.shape[0]) % 128
        rows.append(jnp.pad(f, (0, pad)).reshape(-1, 128))
    out = jnp.concatenate(rows, axis=0)
    pad = (-out.shape[0]) % 256
    return jnp.pad(out, ((0, pad), (0, 0)))


def _unpack(packed, shapes):
    res, r = [], 0
    for shp in shapes:
        size = int(np.prod(shp))
        nr = -(-size // 128)
        res.append(packed[r:r + nr].reshape(-1)[:size].reshape(shp))
        r += nr
    return res


def kernel(x, c, w_ada, b_ada, norm1_g, w_in, b_in, attn_sinks, rel_bias, lambda_re, lambda_im, log_step, ssm_b_re, ssm_b_im, ssm_c_re, ssm_c_im, ssm_d, w_glu, b_glu, w_attn_proj, w_ssm_proj, w_out, norm2_g, w_ff1, w_ff2, final_g, loss_target, m_w_ada, m_b_ada, m_norm1_g, m_w_in, m_b_in, m_attn_sinks, m_rel_bias, m_lambda_re, m_lambda_im, m_log_step, m_ssm_b_re, m_ssm_b_im, m_ssm_c_re, m_ssm_c_im, m_ssm_d, m_w_glu, m_b_glu, m_w_attn_proj, m_w_ssm_proj, m_w_out, m_norm2_g, m_w_ff1, m_w_ff2, m_final_g, v_w_ada, v_b_ada, v_norm1_g, v_w_in, v_b_in, v_attn_sinks, v_rel_bias, v_lambda_re, v_lambda_im, v_log_step, v_ssm_b_re, v_ssm_b_im, v_ssm_c_re, v_ssm_c_im, v_ssm_d, v_w_glu, v_b_glu, v_w_attn_proj, v_w_ssm_proj, v_w_out, v_norm2_g, v_w_ff1, v_w_ff2, v_final_g):
    given = dict(locals())
    S, D = x.shape[1], x.shape[2]
    SSM_W = w_glu.shape[2]
    G = SSM_W // SSM_GROUP_CH
    NST = G * SSM_STATE
    DFF = w_ff2.shape[1] * N_CHIPS
    INW = w_in.shape[2] * N_CHIPS
    o_q, o_k, o_v, o_u = 0, ATTN_WIDTH, ATTN_WIDTH + KV_WIDTH, ATTN_WIDTH + 2 * KV_WIDTH
    o_ga, o_gs = o_u + SSM_W, o_u + SSM_W + D
    mx, my, mc = _position()
    my_chip = 2 * mx + my
    my_b = 4 * mx + 2 * my + mc

    xv, tgt = x[0], loss_target[0]

    big = dict(w_in=w_in[0], w_glu=w_glu[0], w_attn_proj=w_attn_proj[0], w_ssm_proj=w_ssm_proj[0],
               w_out=w_out[0], w_ff1=w_ff1[0], w_ff2=w_ff2[0])
    big_names = list(big)
    colsharded = {"w_in", "w_attn_proj", "w_ssm_proj", "w_ff1"}
    chip_sel = my_chip.astype(jnp.int32).reshape(1)
    gather_groups = [["w_in"], ["w_attn_proj", "w_ssm_proj", "w_glu", "w_out"], ["w_ff1", "w_ff2"]]
    in_flight, gather_sems, gathered = {}, [], {}

    def finish_gather(g, after, meanwhile=None):
        bufs = [in_flight[k] for k in gather_groups[g]]
        bufs = _gather_wait(bufs, gather_sems[g][0], gather_sems[g][1], after, "gather_wait_%d" % g)
        if meanwhile is None:
            bufs = _forward_halves(bufs, "gather_forward_%d" % g)
        else:
            bufs, send, recv, _ = _copies_start(bufs, _forward_copies, 3 * len(bufs), nothing,
                                                "gather_forward_%d_start" % g)
            bufs = _copies_wait(bufs, _forward_copies, send, recv, meanwhile, "gather_forward_%d_wait" % g)
        gathered.update(zip(gather_groups[g], bufs))

    def tied(v, token):
        return v + token[0:1, 0:1]

    def all_of(*arrays):
        return list(arrays)

    def wop(k):
        g = gathered[k]
        return _Op(g, N_CHIPS) if k in colsharded else _Op(g.reshape(g.shape[0] * g.shape[1], g.shape[2]))

    grads = {}
    nothing = jnp.zeros((8, 128), F32)
    half = mc.astype(jnp.int32).reshape(1)
    sel = jnp.stack([my_chip, mc]).astype(jnp.int32)

    def rs_swap(tag, named):
        keys, gl = list(named), []
        for k in keys:
            gk = named[k]
            if k not in colsharded:
                gk = gk.reshape(N_CHIPS, gk.shape[0] // N_CHIPS, gk.shape[1])
            gl.append(gk)
        lands = [lax.empty((g.shape[0], g.shape[1] // 2, g.shape[2]), g.dtype) for g in gl]
        arrays, ssem, rsem, token = _copies_start(gl + lands, _swap_copies, len(gl), nothing, "rs_swap_start_" + tag)
        return (keys, arrays, ssem, rsem), token

    def rs_scatter(tag, state, after):
        keys, arrays, ssem, rsem = state
        arrays = _copies_wait(arrays, _swap_copies, ssem, rsem, after, "rs_swap_wait_" + tag)
        n = len(keys)
        ps = [_add_half(g, t, half, "rs_add_" + k) for g, t, k in zip(arrays[:n], arrays[n:], keys)]
        lands = [lax.empty((3,) + p.shape[1:], p.dtype) for p in ps]
        arrays, ssem, rsem, token = _copies_start(ps + lands, _scatter_copies, 3 * n, nothing, "rs_start_" + tag)
        return (keys, arrays, ssem, rsem), token

    def rs_sum(tag, state, after):
        keys, arrays, ssem, rsem = state
        arrays = _copies_wait(arrays, _scatter_copies, ssem, rsem, after, "rs_wait_" + tag)
        n = len(keys)
        rs = [_sum_own(p, t, sel, "rs_sum_" + k) for p, t, k in zip(arrays[:n], arrays[n:], keys)]
        rs, ssem, rsem, token = _copies_start(rs, _join_copies, n, nothing, "rs_join_start_" + tag)
        return (keys, rs, ssem, rsem), token

    def rs_finish(tag, state, after):
        keys, rs, ssem, rsem = state
        for k, f in zip(keys, _copies_wait(rs, _join_copies, ssem, rsem, after, "rs_join_wait_" + tag)):
            grads[k] = f[None]

    c_all = _allgather8(jnp.pad(c, ((0, 7), (0, 0))), "gather_c").reshape(N_DEV, 8, D)[:, 0]
    c16 = jnp.pad(c_all, ((0, 8), (0, 0)))
    b_ada_mine = lax.dynamic_slice(b_ada.reshape(N_CHIPS, -1), (my_chip, 0), (1, w_ada.shape[2]))
    mod_sh = _mm(c16, w_ada[0], "NN", name="mod", M=16, N=w_ada.shape[2], K=D, a_fn=_silu,
                 epilogue=lambda acc, b: (acc + b,), extras=[(b_ada_mine, "row")])
    mod_all = _allgather8(mod_sh[:8], "gather_mod").reshape(N_DEV, 8, -1)
    mod_row = jnp.concatenate(
        [lax.dynamic_slice(mod_all, (2 * j, my_b, 0), (1, 1, mod_all.shape[2]))[0] for j in range(N_CHIPS)], axis=1)
    sh1, sc1, g1, sh2, sc2, g2 = [mod_row[:, i * D:(i + 1) * D] for i in range(6)]

    first = [_cast_into_slot(big["w_in"], chip_sel, "cast_w_in")]
    first, sems_first, token_first = _gather_start(first, [[0]], mod_all, "gather_start_in")
    rest_names = gather_groups[1] + gather_groups[2]
    rest = [_cast_into_slot(big[k], chip_sel, "cast_" + k) for k in rest_names]
    rest, sems_rest, token_rest = _gather_start(
        rest, [[rest_names.index(k) for k in grp] for grp in gather_groups[1:]], token_first, "gather_start_rest")
    in_flight.update(zip(["w_in"] + rest_names, first + rest))
    gather_sems.extend(sems_first + sems_rest)

    disc_in = (lambda_re[0].reshape(1, NST), lambda_im[0].reshape(1, NST),
               jnp.repeat(log_step[0], SSM_STATE).reshape(1, NST),
               ssm_b_re[0].transpose(2, 0, 1).reshape(SSM_GROUP_CH, NST),
               ssm_b_im[0].transpose(2, 0, 1).reshape(SSM_GROUP_CH, NST))
    abar_re, abar_im, bbar_re, bbar_im = _ssm_params(disc_in, None, "ssm_params")
    same_group = jnp.asarray(np.arange(SSM_W)[:, None] // SSM_GROUP_CH == np.arange(NST)[None, :] // SSM_STATE)

    def block_diag(t):
        return jnp.where(same_group, jnp.tile(t, (G, 1)), 0.0)

    bd = jnp.concatenate([block_diag(bbar_re), block_diag(bbar_im)], axis=1)
    cd = jnp.concatenate([block_diag(cc.transpose(1, 0, 2).reshape(SSM_GROUP_CH, NST)).T
                          for cc in (ssm_c_re[0], -ssm_c_im[0])], axis=0)
    a_fwd = jnp.stack([abar_re, abar_im])
    a_bwd = jnp.stack([abar_re, -abar_im])
    d_row = ssm_d

    buckets = _t5_buckets_block()
    onehot_t = (jnp.arange(128, dtype=jnp.int32)[:, None] == jnp.asarray(buckets.reshape(1, -1))).astype(BF16)
    rb_hi = rel_bias.astype(BF16)
    rb_lo = (rel_bias - rb_hi.astype(F32)).astype(BF16)
    rb_lo2 = (rel_bias - rb_hi.astype(F32) - rb_lo.astype(F32)).astype(BF16)
    rb3 = jnp.pad(jnp.concatenate([rb_hi.T, rb_lo.T, rb_lo2.T], axis=0), ((0, 0), (0, 128 - NUM_BUCKETS)))
    b3 = _mm(rb3, onehot_t, "NN", name="rel_bias_rows", M=3 * N_Q_HEADS, N=BLOCK * 2 * BLOCK, K=128, tj=4096)
    bias = (b3[:N_Q_HEADS] + b3[N_Q_HEADS:2 * N_Q_HEADS]) + b3[2 * N_Q_HEADS:]
    bias = bias.reshape(N_Q_HEADS, BLOCK, 2 * BLOCK)
    sinks_b = jnp.broadcast_to(attn_sinks[0][:, None, None], (N_Q_HEADS, BLOCK, 128)).reshape(N_Q_HEADS * BLOCK, 128)

    def two(fn):
        def both(*blocks):
            r = fn(*blocks)
            return r, r
        return both

    h1, h1_t = _rowwise(two(_norm_mod), [(xv, "tile", D), (tied(tied(norm1_g, token_first), token_rest), "row", D),
                                         (sh1, "row", D), (sc1, "row", D)],
                        [(D, BF16), (D, BF16, "T")], [], name="norm1", rows=S)
    finish_gather(0, all_of(h1, bd, cd, a_fwd, a_bwd), all_of(bias, sinks_b))
    proj = _mm(h1, wop("w_in"), "NN", name="proj", M=S, N=INW, K=D, out_dtypes=(BF16,),
               epilogue=lambda acc, b: (acc + b,), extras=[(b_in, "row")])

    def heads(v2d, nh):
        return v2d.reshape(S, nh, HEAD_DIM).transpose(1, 0, 2)

    def unheads(v3d):
        return v3d.transpose(1, 0, 2).reshape(S, -1)

    qh = heads(proj[:, o_q:o_k], N_Q_HEADS)
    kh = heads(proj[:, o_k:o_v], N_KV_HEADS)
    vh = heads(proj[:, o_v:o_u], N_KV_HEADS)
    attn = unheads(_attn_fwd(qh, kh, vh, sinks_b, bias, "attn_fwd"))
    finish_gather(1, attn)
    y_attn = _mm(attn, wop("w_attn_proj"), "NN", name="attn_proj", M=S, N=D, K=ATTN_WIDTH, out_dtypes=(BF16,))

    u = proj[:, o_u:o_ga]
    u_il = _interleave(u, SCAN_CHUNKS)
    SB = 128
    nsb, gpb = SSM_W // SB, SB // SSM_GROUP_CH
    SBN = gpb * SSM_STATE
    y_il, xs = _ssm_fwd(u_il, bd, cd, a_fwd, d_row, name="ssm_fwd", sb=SB, sbn=SBN)
    y = _deinterleave(y_il, SCAN_CHUNKS)
    z, t_glu = _mm(y, wop("w_glu"), "NN", name="glu", M=S, N=SSM_W, K=SSM_W, out_dtypes=(BF16, F32), a_fn=_gelu,
                   epilogue=lambda acc, b, yy: (_gelu(yy) * _sigmoid(acc + b), acc + b),
                   extras=[(b_glu, "row"), (y, "tile")])
    y_ssm = _mm(z, wop("w_ssm_proj"), "NN", name="ssm_proj", M=S, N=D, K=SSM_W, out_dtypes=(BF16,))

    ff_bufs = _gather_wait([in_flight[k] for k in gather_groups[2]], gather_sems[2][0], gather_sems[2][1], all_of(y_ssm),
                           "gather_wait_2")
    ff_bufs, ff_send, ff_recv, token = _copies_start(ff_bufs, _forward_copies, 3 * len(ff_bufs), nothing,
                                                    "gather_forward_2_start")
    merged, merged_t = _rowwise(two(_merge), [(_Op(proj, coff=o_ga), "tile", D), (_Op(proj, coff=o_gs), "tile", D),
                                              (y_attn, "tile", D), (y_ssm, "tile", D)],
                                [(D, BF16), (D, BF16, "T")], [], name="merge", rows=S)
    mo, x2 = _mm(merged, wop("w_out"), "NN", name="out_proj", M=S, N=D, K=D, out_dtypes=(BF16, F32),
                 epilogue=lambda acc, xx, gg: (acc, xx + gg * acc), extras=[(xv, "tile"), (g1, "row")], deps=[token])
    h2, h2_t = _rowwise(two(_norm_mod), [(x2, "tile", D), (norm2_g, "row", D), (sh2, "row", D), (sc2, "row", D)],
                        [(D, BF16), (D, BF16, "T")], [], name="norm2", rows=S)
    gathered.update(zip(gather_groups[2], _copies_wait(ff_bufs, _forward_copies, ff_send, ff_recv, h2,
                                                       "gather_forward_2_wait")))
    a_b, r_b = _mm(h2, wop("w_ff1"), "NN", name="ff1", M=S, N=DFF, K=D, out_dtypes=(BF16, BF16),
                   epilogue=lambda acc: (acc, jnp.square(jnp.maximum(acc, 0.0))))
    ff, x3 = _mm(r_b, wop("w_ff2"), "NN", name="ff2", M=S, N=D, K=DFF, out_dtypes=(BF16, F32),
                 epilogue=lambda acc, xx, gg: (acc, xx + gg * acc), extras=[(x2, "tile"), (g2, "row")],
                 tj=1024, tk=1024)

    def final_fn(x3b, gf, tb, ffb, g2b):
        def f(xx, gg):
            yv = xx * lax.rsqrt(jnp.mean(xx * xx, axis=-1, keepdims=True) + EPS) * gg
            err = jnp.square(yv - tb)
            return 0.5 * jnp.sum(jnp.mean(err, axis=-1, keepdims=True), axis=0, keepdims=True)
        lv, vjp = jax.vjp(f, x3b, gf)
        dx, dg = vjp(jnp.ones((1, 1), F32))
        return dx, dx * g2b, dg, jnp.broadcast_to(lv, (1, 128)), jnp.sum(dx * ffb, axis=0, keepdims=True)

    dx3, dff, g_final, loss_acc, d_g2 = _rowwise(
        final_fn, [(x3, "tile", D), (final_g.reshape(1, D), "row", D), (tgt, "tile", D), (ff, "tile", D), (g2, "row", D)],
        [(D, F32), (D, BF16)], [D, 128, D], name="final", rows=S)
    da = _mm(dff, wop("w_ff2"), "NT", name="ff2_dx", M=S, N=DFF, K=D, out_dtypes=(BF16,),
             epilogue=lambda acc, ab: (acc * (2.0 * jnp.maximum(ab.astype(F32), 0.0)),), extras=[(a_b, "tile")])
    g_w_ff2 = _mm(r_b, dff, "TN", name="ff2_dw", M=DFF, N=D, K=S, out_dtypes=(BF16,), tj=1024, tk=1024)
    g_w_ff1 = _mm(h2_t, da, "NN", name="ff1_dw", M=D, N=DFF, K=S, out_dtypes=(BF16,), out_nsh=N_CHIPS, tj=1024, tk=1024)
    rs_ff, token = rs_swap("ff", dict(w_ff2=g_w_ff2, w_ff1=g_w_ff1))
    dh2 = _mm(da, wop("w_ff1"), "NT", name="ff1_dx", M=S, N=D, K=DFF, tj=1024, tk=1024, deps=[token])
    rs_ff, token_ff = rs_scatter("ff", rs_ff, dh2)

    def norm2_bwd(x2b, dh2b, dx3b, mob, gn, shb, scb, g1b):
        _, vjp = jax.vjp(_norm_mod, x2b, gn, shb, scb)
        dx, dg, dsh, dsc = vjp(dh2b)
        dx2b = dx + dx3b
        return dx2b, dx2b * g1b, dg, dsh, dsc, jnp.sum(dx2b * mob, axis=0, keepdims=True)

    dx2, dmo, g_norm2, d_sh2, d_sc2, d_g1 = _rowwise(
        norm2_bwd, [(x2, "tile", D), (dh2, "tile", D), (dx3, "tile", D), (mo, "tile", D),
                    (tied(norm2_g, token_ff), "row", D), (sh2, "row", D), (sc2, "row", D), (g1, "row", D)],
        [(D, F32), (D, BF16)], [D, D, D, D], name="norm2_bwd", rows=S)
    dmerged = _mm(dmo, wop("w_out"), "NT", name="out_dx", M=S, N=D, K=D)
    g_w_out = _mm(merged_t, dmo, "NN", name="out_dw", M=D, N=D, K=S, out_dtypes=(BF16,), tj=1024, tk=1024)

    def merge_bwd(gab, gsb, yab, ysb, dmb):
        _, vjp = jax.vjp(_merge, gab, gsb, yab, ysb)
        return vjp(dmb)

    d_ga, d_gs, dy_attn, dy_ssm = _rowwise(
        merge_bwd, [(_Op(proj, coff=o_ga), "tile", D), (_Op(proj, coff=o_gs), "tile", D), (y_attn, "tile", D),
                    (y_ssm, "tile", D), (dmerged, "tile", D)],
        [(D, BF16), (D, BF16), (D, BF16), (D, BF16)], [], name="merge_bwd", rows=S)

    dattn = _mm(dy_attn, wop("w_attn_proj"), "NT", name="attn_proj_dx", M=S, N=ATTN_WIDTH, K=D, tj=1024, out_dtypes=(BF16,))
    g_w_attn_proj = _mm(attn, dy_attn, "TN", name="attn_proj_dw", M=ATTN_WIDTH, N=D, K=S, out_dtypes=(BF16,),
                        out_nsh=N_CHIPS, tk=1024)

    dz = _mm(dy_ssm, wop("w_ssm_proj"), "NT", name="ssm_proj_dx", M=S, N=SSM_W, K=D)
    g_w_ssm_proj = _mm(z, dy_ssm, "TN", name="ssm_proj_dw", M=SSM_W, N=D, K=S, out_dtypes=(BF16,),
                       out_nsh=N_CHIPS, tk=1024)

    def glu_bwd(dzb, yb, tb):
        z0 = _gelu(yb)
        sg = _sigmoid(tb)
        dt = dzb * z0 * sg * (1.0 - sg)
        return dt, dzb * sg, jnp.sum(dt, axis=0, keepdims=True)

    dt_b, dz0a, g_b_glu = _rowwise(glu_bwd, [(dz, "tile", SSM_W), (y, "tile", SSM_W), (t_glu, "tile", SSM_W)],
                                   [(SSM_W, BF16), (SSM_W, F32)], [SSM_W], name="glu_bwd", rows=S)

    def gelu_bwd(acc, dz0ab, yb):
        _, vjp = jax.vjp(_gelu, yb)
        return (vjp(acc + dz0ab)[0],)

    dy = _mm(dt_b, wop("w_glu"), "NT", name="glu_dx", M=S, N=SSM_W, K=SSM_W, epilogue=gelu_bwd,
             extras=[(dz0a, "tile"), (y, "tile")])
    g_w_glu = _mm(y, dt_b, "TN", name="glu_dw", M=SSM_W, N=SSM_W, K=S, out_dtypes=(BF16,), tk=1024, a_fn=_gelu)
    rs_mix, token = rs_swap("mix", dict(w_out=g_w_out, w_attn_proj=g_w_attn_proj, w_ssm_proj=g_w_ssm_proj,
                                        w_glu=g_w_glu))
    dy_il = _interleave(dy, SCAN_CHUNKS)
    du_il, g_bd, g_cd, d_abar, g_ssm_d = _ssm_bwd(dy_il, u_il, xs, bd, cd, a_bwd, d_row, name="ssm_bwd", sb=SB, sbn=SBN,
                                                  deps=[token])
    du = _deinterleave(du_il, SCAN_CHUNKS)
    rs_mix, token_mix = rs_scatter("mix", rs_mix, du_il)

    dqh, dkh, dvh, dsink_blk, dbias = _attn_bwd(qh, kh, vh, heads(dattn, N_Q_HEADS), tied(sinks_b, token_mix), bias,
                                                "attn_bwd")
    g_sinks = _sum_lead(dsink_blk.reshape(N_Q_HEADS, BLOCK, 128).transpose(1, 0, 2), "sinks_dw")[:, 0].reshape(1, N_Q_HEADS)
    g_rel = _mm(dbias.reshape(N_Q_HEADS, -1), onehot_t, "NT", name="rel_bias_dw", M=N_Q_HEADS, N=128,
                K=BLOCK * 2 * BLOCK, tk=4096)
    g_rel_bias = g_rel[:, :NUM_BUCKETS].T

    eye_b = jnp.eye(gpb, dtype=F32)
    g_cd6 = g_cd.reshape(2, nsb, gpb, SSM_STATE, gpb, SSM_GROUP_CH)
    g_c_re = jnp.einsum("bgnhp,gh->bgpn", g_cd6[0], eye_b).reshape(G, SSM_GROUP_CH, SSM_STATE)
    g_c_im = -jnp.einsum("bgnhp,gh->bgpn", g_cd6[1], eye_b).reshape(G, SSM_GROUP_CH, SSM_STATE)
    g_bd6 = g_bd.reshape(nsb, gpb, SSM_GROUP_CH, 2, gpb, SSM_STATE)
    g_bbar = jnp.einsum("bhprgn,hg->rpbhn", g_bd6, eye_b).reshape(2, SSM_GROUP_CH, NST)
    g_lre, g_lim, g_lstep, g_bre, g_bim = _ssm_params(disc_in, (d_abar[0], d_abar[1], g_bbar[0], g_bbar[1]),
                                                      "ssm_params_bwd")
    g_lre, g_lim = g_lre.reshape(G, SSM_STATE), g_lim.reshape(G, SSM_STATE)
    g_lstep = g_lstep.reshape(G, SSM_STATE).sum(axis=1)
    g_bre = g_bre.reshape(SSM_GROUP_CH, G, SSM_STATE).transpose(1, 2, 0)
    g_bim = g_bim.reshape(SSM_GROUP_CH, G, SSM_STATE).transpose(1, 2, 0)

    dproj = jnp.concatenate([unheads(dqh).astype(BF16), unheads(dkh).astype(BF16), unheads(dvh).astype(BF16),
                             du.astype(BF16), d_ga, d_gs], axis=1)
    g_w_in = _mm(h1_t, dproj, "NN", name="proj_dw", M=D, N=INW, K=S, out_dtypes=(BF16,), out_nsh=N_CHIPS,
                 tj=INW // (2 * N_CHIPS), tk=1024)
    rs_in, token = rs_swap("in", dict(w_in=g_w_in))
    dh1 = _mm(dproj, wop("w_in"), "NT", name="proj_dx", M=S, N=D, K=INW, tj=1024, tk=INW // N_CHIPS, deps=[token])
    g_b_in = _rowwise(lambda d: (jnp.sum(d.astype(F32), axis=0, keepdims=True),), [(dproj, "tile", INW)], [], [INW],
                      name="proj_db", rows=S)[0]

    def norm1_bwd(xb, dhb, dresb, gn, shb, scb):
        _, vjp = jax.vjp(_norm_mod, xb, gn, shb, scb)
        dx, dg, dsh, dsc = vjp(dhb)
        return dx + dresb, dg, dsh, dsc

    grad_x, g_norm1, d_sh1, d_sc1 = _rowwise(
        norm1_bwd, [(xv, "tile", D), (dh1, "tile", D), (dx2, "tile", D), (norm1_g, "row", D),
                    (sh1, "row", D),
                    (sc1, "row", D)], [(D, F32)], [D, D, D], name="norm1_bwd", rows=S)

    dmod_row = jnp.concatenate([d_sh1, d_sc1, d_g1, d_sh2, d_sc2, d_g2], axis=1)
    small_g = dict(norm1_g=g_norm1, b_in=g_b_in, attn_sinks=g_sinks, rel_bias=g_rel_bias, lambda_re=g_lre[None],
                   lambda_im=g_lim[None], log_step=g_lstep[None], ssm_b_re=g_bre[None], ssm_b_im=g_bim[None],
                   ssm_c_re=g_c_re[None], ssm_c_im=g_c_im[None], ssm_d=g_ssm_d, b_glu=g_b_glu, norm2_g=g_norm2,
                   final_g=g_final.reshape(D))
    packed = _pack([dmod_row, loss_acc[:, :1]] + [small_g[k] for k in _SMALL])
    rows = packed.shape[0]
    gathered = _allgather8(packed, "gather_small").reshape(N_DEV, rows, 128)
    summed = _sum_lead(gathered, "small_sum")
    parts = _unpack(summed, [dmod_row.shape, (1,)] + [given[k].shape for k in _SMALL])
    loss = parts[1].reshape(())
    grads.update(zip(_SMALL, parts[2:]))
    grads["b_ada"] = parts[0]

    dmod_all = gathered[:, :dmod_row.shape[1] // 128].reshape(N_DEV, -1)
    dmod_mine = lax.dynamic_slice(dmod_all.reshape(N_DEV, N_CHIPS, -1), (0, my_chip, 0), (N_DEV, 1, w_ada.shape[2]))[:, 0]
    g_w_ada = _mm(c16, jnp.pad(dmod_mine, ((0, 8), (0, 0))), "TN", name="ada_dw", M=D, N=w_ada.shape[2], K=16,
                  a_fn=_silu)
    grads["w_ada"] = g_w_ada[None]

    deltas, new_m, new_v = {}, {}, {}

    def adamw_big(k, deps=()):
        echo = k in big_names
        res = _adamw(given[k][0], grads[k][0], given["m_" + k][0], given["v_" + k][0], "adamw_" + k, deps, echo)
        deltas[k], new_m[k], new_v[k] = res[0][None], res[1][None], res[2][None]
        if echo:
            grads[k] = res[3][None]
        return res[2]

    rs_in, token_in = rs_scatter("in", rs_in, all_of(summed, dmod_all))
    rs_ff, token = rs_sum("ff", rs_ff, all_of(summed, token_in))
    mark = adamw_big("w_ada", [token])
    rs_mix, token = rs_sum("mix", rs_mix, mark)
    small_all = list(_SMALL) + ["b_ada"]
    for k in small_all:
        grads[k] = grads[k].reshape(given[k].shape)

    def rows_of(a):
        return a.reshape(1, -1) if a.ndim == 1 else a

    d_, m_, v_ = _adamw_many(*[[rows_of(src[k]) for k in small_all] for src in (
        given, grads, {k: given["m_" + k] for k in small_all}, {k: given["v_" + k] for k in small_all})],
        "adamw_small", [token])
    for k, dd, mm, vv in zip(small_all, d_, m_, v_):
        deltas[k], new_m[k], new_v[k] = (t.reshape(given[k].shape) for t in (dd, mm, vv))
    v_ = v_[0]
    rs_finish("ff", rs_ff, v_)
    marks = [adamw_big(k) for k in ("w_ff2", "w_ff1")]
    rs_finish("mix", rs_mix, all_of(*marks))
    marks = [adamw_big(k) for k in ("w_out", "w_attn_proj", "w_ssm_proj", "w_glu")]
    rs_in, token = rs_sum("in", rs_in, all_of(*marks))
    rs_finish("in", rs_in, token)
    adamw_big("w_in")

    names = ["w_ada", "b_ada", "norm1_g", "w_in", "b_in", "attn_sinks", "rel_bias", "lambda_re", "lambda_im",
             "log_step", "ssm_b_re", "ssm_b_im", "ssm_c_re", "ssm_c_im", "ssm_d", "w_glu", "b_glu", "w_attn_proj",
             "w_ssm_proj", "w_out", "norm2_g", "w_ff1", "w_ff2", "final_g"]
    return (loss, grad_x[None], *[grads[n] for n in names], *[deltas[n] for n in names],
            *[new_m[n] for n in names], *[new_v[n] for n in names])
```

```python
import math

import numpy as np
import jax
import jax.numpy as jnp
from jax import lax
from jax.experimental import pallas as pl
from jax.experimental.pallas import tpu as pltpu

F32 = jnp.float32
BF16 = jnp.bfloat16
MESH = pl.DeviceIdType.MESH

HEAD_DIM = 64
N_Q_HEADS = 16
N_KV_HEADS = 4
GQA_GROUP = N_Q_HEADS // N_KV_HEADS
ATTN_WIDTH = N_Q_HEADS * HEAD_DIM
KV_WIDTH = N_KV_HEADS * HEAD_DIM
BLOCK = 128
NUM_BUCKETS = 32
MAX_DISTANCE = 128
NEG_INF = -1e30
SSM_GROUP_CH = 16
SSM_STATE = 64
EPS = 1e-6
ADAM_LR = 0.001
ADAM_B1 = 0.9
ADAM_B2 = 0.999
ADAM_EPS = 1e-08
ADAM_WD = 0.01
ADAM_STEP = 10

N_CHIPS = 4
N_DEV = 8
SCAN_CHUNKS = 8
VMEM_LIMIT_BYTES = 48 * 1024 * 1024
SSM_VMEM_LIMIT_BYTES = 56 * 1024 * 1024


def _cparams(sem=None):
    return pltpu.CompilerParams(dimension_semantics=sem, vmem_limit_bytes=VMEM_LIMIT_BYTES)


class _Op:
    def __init__(self, arr, nsh=None, coff=0):
        self.arr, self.nsh, self.coff = arr, nsh, coff
        if nsh is None:
            self.rows, self.cols = arr.shape
        else:
            assert arr.shape[0] == nsh
            self.rows, self.cols = arr.shape[1], arr.shape[2] * nsh

    def spec(self, br, bc, idx):
        assert self.coff % bc == 0
        off = self.coff // bc
        if self.nsh is None:
            return pl.BlockSpec((br, bc), lambda *g: (idx(*g)[0], idx(*g)[1] + off))
        per = (self.cols // self.nsh) // bc
        assert per * bc * self.nsh == self.cols

        def imap(*g):
            r, c = idx(*g)
            c = c + off
            return (c // per, r, c % per)
        return pl.BlockSpec((None, br, bc), imap)


def _as_op(a):
    return a if isinstance(a, _Op) else _Op(a)


def _mm(a, b, mode, *, name, M, N, K, out_dtypes=(F32,), out_nsh=None, epilogue=None, extras=(),
        a_fn=None, ti=1024, tj=512, tk=2048, deps=()):
    nd = len(deps)
    a, b = _as_op(a), _as_op(b)
    ti, tj, tk = min(ti, M), min(tj, N), min(tk, K)
    a_w = a.cols // a.nsh if a.nsh else None
    b_w = b.cols // b.nsh if b.nsh else None
    if a_w:
        ti, tk = (min(ti, a_w), tk) if mode == "TN" else (ti, min(tk, a_w))
    if b_w:
        tj, tk = (tj, min(tk, b_w)) if mode == "NT" else (min(tj, b_w), tk)
    if out_nsh:
        tj = min(tj, N // out_nsh)
    assert M % ti == 0 and N % tj == 0 and K % tk == 0, (name, M, N, K, ti, tj, tk)
    nk = K // tk
    if mode == "NN":
        a_spec = a.spec(ti, tk, lambda i, j, k: (i, k))
        b_spec = b.spec(tk, tj, lambda i, j, k: (k, j))
        dims = (((1,), (0,)), ((), ()))
    elif mode == "NT":
        a_spec = a.spec(ti, tk, lambda i, j, k: (i, k))
        b_spec = b.spec(tj, tk, lambda i, j, k: (j, k))
        dims = (((1,), (1,)), ((), ()))
    else:
        a_spec = a.spec(tk, ti, lambda i, j, k: (k, i))
        b_spec = b.spec(tk, tj, lambda i, j, k: (k, j))
        dims = (((0,), (0,)), ((), ()))
    ex_specs, ex_arrs = [], []
    for op, kind in extras:
        op = _as_op(op)
        if kind == "tile":
            ex_specs.append(op.spec(ti, tj, lambda i, j, k: (i, j)))
        else:
            ex_specs.append(op.spec(1, tj, lambda i, j, k: (0, j)))
        ex_arrs.append(op.arr)
    ne, no = len(ex_arrs), len(out_dtypes)
    if out_nsh is None:
        out_shapes = [jax.ShapeDtypeStruct((M, N), d) for d in out_dtypes]
        out_specs = [pl.BlockSpec((ti, tj), lambda i, j, k: (i, j)) for _ in out_dtypes]
    else:
        per = (N // out_nsh) // tj
        assert per * tj * out_nsh == N
        out_shapes = [jax.ShapeDtypeStruct((out_nsh, M, N // out_nsh), d) for d in out_dtypes]
        out_specs = [pl.BlockSpec((None, ti, tj), lambda i, j, k: (j // per, i, j % per)) for _ in out_dtypes]

    def body(a_ref, b_ref, *rest):
        ex_refs, out_refs, acc = rest[:ne], rest[ne + nd:ne + nd + no], rest[ne + nd + no]
        k = pl.program_id(2)

        @pl.when(k == 0)
        def _():
            acc[...] = jnp.zeros_like(acc)

        av = a_ref[...]
        if a_fn is not None:
            av = a_fn(av)
        acc[...] += lax.dot_general(av.astype(BF16), b_ref[...].astype(BF16), dims,
                                    preferred_element_type=F32)

        @pl.when(k == nk - 1)
        def _():
            res = acc[...]
            outs = epilogue(res, *[r[...] for r in ex_refs]) if epilogue is not None else (res,)
            for o_ref, o in zip(out_refs, outs):
                o_ref[...] = o.astype(o_ref.dtype)

    outs = pl.pallas_call(
        body, name=name, grid=(M // ti, N // tj, nk),
        in_specs=[a_spec, b_spec] + ex_specs + [pl.BlockSpec(memory_space=pl.ANY)] * nd,
        out_specs=out_specs, out_shape=out_shapes,
        scratch_shapes=[pltpu.VMEM((ti, tj), F32)],
        compiler_params=_cparams(("parallel", "parallel", "arbitrary")),
    )(a.arr, b.arr, *ex_arrs, *deps)
    return outs[0] if no == 1 else outs


def _rowwise(fn, ins, outs, accs, *, name, rows, tr=256, deps=()):
    tr = min(tr, rows)
    assert rows % tr == 0
    in_specs, arrs = [], []
    for op, kind, width in ins:
        op = _as_op(op)
        if kind == "tile":
            in_specs.append(op.spec(tr, width, lambda i: (i, 0)))
        else:
            in_specs.append(op.spec(op.rows, width, lambda i: (0, 0)))
        arrs.append(op.arr)
    ni, no, na = len(ins), len(outs), len(accs)
    flipped = [len(o) == 3 for o in outs]
    out_shapes = [jax.ShapeDtypeStruct((o[0], rows) if t else (rows, o[0]), o[1]) for o, t in zip(outs, flipped)]
    out_specs = [pl.BlockSpec((o[0], tr), lambda i: (0, i)) if t else pl.BlockSpec((tr, o[0]), lambda i: (i, 0))
                 for o, t in zip(outs, flipped)]
    out_shapes += [jax.ShapeDtypeStruct((1, w), F32) for w in accs]
    out_specs += [pl.BlockSpec((1, w), lambda i: (0, 0)) for w in accs]

    def body(*refs):
        nd = len(deps)
        in_refs, out_refs, acc_refs = refs[:ni], refs[ni + nd:ni + nd + no], refs[ni + nd + no:]
        res = fn(*[r[...] for r in in_refs])
        if not isinstance(res, (tuple, list)):
            res = (res,)
        for o_ref, r, t in zip(out_refs, res[:no], flipped):
            o_ref[...] = (r.astype(F32).T if t else r).astype(o_ref.dtype)
        if na:
            @pl.when(pl.program_id(0) == 0)
            def _():
                for a_ref in acc_refs:
                    a_ref[...] = jnp.zeros_like(a_ref)
            for a_ref, r in zip(acc_refs, res[no:]):
                a_ref[...] += r.astype(F32)

    res = pl.pallas_call(
        body, name=name, grid=(rows // tr,), in_specs=in_specs + [pl.BlockSpec(memory_space=pl.ANY)] * len(deps),
        out_specs=out_specs, out_shape=out_shapes, compiler_params=_cparams(("arbitrary",)),
    )(*arrs, *deps)
    return res


def _norm_mod(x, g, sh, sc):
    y = x * lax.rsqrt(jnp.mean(x * x, axis=-1, keepdims=True) + EPS) * g
    return y * (1.0 + sc) + sh


def _sigmoid(x):
    return 1.0 / (1.0 + jnp.exp(-x))


def _silu(x):
    return x * _sigmoid(x)


def _gelu(x):
    return 0.5 * x * (1.0 + jnp.tanh(math.sqrt(2.0 / math.pi) * (x + 0.044715 * (x * x * x))))


def _merge(ga, gs, ya, ys):
    ga, gs, ya, ys = (v.astype(F32) for v in (ga, gs, ya, ys))
    return _sigmoid(ga) * ya + _sigmoid(gs) * ys


def _attn_head(q, kp, kc, vp, vc, sink, bias_p, bias_c, not_first):
    nt = (((1,), (1,)), ((), ()))
    nn = (((1,), (0,)), ((), ()))
    qb = q.astype(BF16)
    scale = HEAD_DIM ** -0.5
    sp = lax.dot_general(qb, kp.astype(BF16), nt, preferred_element_type=F32) * scale + bias_p
    sc = lax.dot_general(qb, kc.astype(BF16), nt, preferred_element_type=F32) * scale + bias_c
    qi = lax.broadcasted_iota(jnp.int32, sp.shape, 0) & (BLOCK - 1)
    ki = lax.broadcasted_iota(jnp.int32, sp.shape, 1)
    sp = jnp.where(jnp.logical_and(ki > qi, not_first), sp, NEG_INF)
    sc = jnp.where(ki <= qi, sc, NEG_INF)
    m = jnp.maximum(jnp.maximum(jnp.max(sp, axis=-1, keepdims=True), jnp.max(sc, axis=-1, keepdims=True)), sink)
    m = lax.stop_gradient(m)
    pp = jnp.exp(sp - m)
    pc = jnp.exp(sc - m)
    denom = jnp.sum(pp, axis=-1, keepdims=True) + jnp.sum(pc, axis=-1, keepdims=True) + jnp.exp(sink - m)
    o = lax.dot_general((pp / denom).astype(BF16), vp.astype(BF16), nn, preferred_element_type=F32)
    o = o + lax.dot_general((pc / denom).astype(BF16), vc.astype(BF16), nn, preferred_element_type=F32)
    return o


def _attn_fwd(qh, kh, vh, sinks, bias, name):
    s = qh.shape[1]
    nb = s // BLOCK
    G = GQA_GROUP
    R = G * BLOCK

    def body(q_ref, kp_ref, kc_ref, vp_ref, vc_ref, sink_ref, bias_ref, o_ref):
        not_first = pl.program_id(0) > 0
        for kv in range(N_KV_HEADS):
            hs = slice(kv * G, (kv + 1) * G)
            o = _attn_head(q_ref[hs].reshape(R, HEAD_DIM), kp_ref[kv], kc_ref[kv], vp_ref[kv], vc_ref[kv],
                           sink_ref[kv * R:(kv + 1) * R, 0:1],
                           bias_ref[hs, :, 0:BLOCK].reshape(R, BLOCK), bias_ref[hs, :, BLOCK:2 * BLOCK].reshape(R, BLOCK),
                           not_first)
            o_ref[hs] = o.reshape(G, BLOCK, HEAD_DIM).astype(o_ref.dtype)

    cur = lambda i: (0, i, 0)
    prev = lambda i: (0, jnp.maximum(i - 1, 0), 0)
    return pl.pallas_call(
        body, name=name, grid=(nb,),
        in_specs=[pl.BlockSpec((N_Q_HEADS, BLOCK, HEAD_DIM), cur),
                  pl.BlockSpec((N_KV_HEADS, BLOCK, HEAD_DIM), prev), pl.BlockSpec((N_KV_HEADS, BLOCK, HEAD_DIM), cur),
                  pl.BlockSpec((N_KV_HEADS, BLOCK, HEAD_DIM), prev), pl.BlockSpec((N_KV_HEADS, BLOCK, HEAD_DIM), cur),
                  pl.BlockSpec((N_Q_HEADS * BLOCK, 128), lambda i: (0, 0)),
                  pl.BlockSpec((N_Q_HEADS, BLOCK, 2 * BLOCK), lambda i: (0, 0, 0))],
        out_specs=pl.BlockSpec((N_Q_HEADS, BLOCK, HEAD_DIM), cur),
        out_shape=jax.ShapeDtypeStruct((N_Q_HEADS, s, HEAD_DIM), BF16),
        compiler_params=_cparams(("arbitrary",)),
    )(qh, kh, kh, vh, vh, sinks, bias)


def _attn_bwd(qh, kh, vh, doh, sinks, bias, name):
    s = qh.shape[1]
    nb = s // BLOCK
    G = GQA_GROUP
    R = G * BLOCK

    def body(q_ref, kp_ref, kc_ref, vp_ref, vc_ref, do_ref, sink_ref, bias_ref,
             dq_ref, dk_ref, dv_ref, dsink_ref, dbias_ref, ck, cv):
        i = pl.program_id(1)

        @pl.when(i == 0)
        def _():
            dsink_ref[...] = jnp.zeros_like(dsink_ref)
            dbias_ref[...] = jnp.zeros_like(dbias_ref)
            ck[...] = jnp.zeros_like(ck)
            cv[...] = jnp.zeros_like(cv)

        @pl.when(i < nb)
        def _():
            not_first = i > 0
            _, vjp = jax.vjp(lambda q, a, b, c, d, sk, e, f: _attn_head(q, a, b, c, d, sk, e, f, not_first),
                             q_ref[...].astype(F32).reshape(R, HEAD_DIM), kp_ref[...].astype(F32),
                             kc_ref[...].astype(F32), vp_ref[...].astype(F32), vc_ref[...].astype(F32),
                             sink_ref[:, 0:1], bias_ref[:, :, 0:BLOCK].reshape(R, BLOCK),
                             bias_ref[:, :, BLOCK:2 * BLOCK].reshape(R, BLOCK))
            dq, dkp, dkc, dvp, dvc, dsk, dbp, dbc = vjp(do_ref[...].reshape(R, HEAD_DIM).astype(F32))
            dq_ref[...] = dq.reshape(G, BLOCK, HEAD_DIM).astype(dq_ref.dtype)
            dsink_ref[...] += jnp.broadcast_to(dsk, (R, 128))
            dbias_ref[:, :, 0:BLOCK] += dbp.reshape(G, BLOCK, BLOCK)
            dbias_ref[:, :, BLOCK:2 * BLOCK] += dbc.reshape(G, BLOCK, BLOCK)
            dk_ref[...] = (ck[...] + dkp).astype(dk_ref.dtype)
            dv_ref[...] = (cv[...] + dvp).astype(dv_ref.dtype)
            ck[...] = dkc
            cv[...] = dvc

        @pl.when(i == nb)
        def _():
            dk_ref[...] = ck[...].astype(dk_ref.dtype)
            dv_ref[...] = cv[...].astype(dv_ref.dtype)

    qcur = lambda kv, i: (kv, jnp.minimum(i, nb - 1), 0)
    kcur = lambda kv, i: (kv, jnp.minimum(i, nb - 1), 0)
    kprev = lambda kv, i: (kv, jnp.clip(i - 1, 0, nb - 1), 0)
    qspec = pl.BlockSpec((G, BLOCK, HEAD_DIM), qcur)
    kc_spec = pl.BlockSpec((None, BLOCK, HEAD_DIM), kcur)
    kp_spec = pl.BlockSpec((None, BLOCK, HEAD_DIM), kprev)
    return pl.pallas_call(
        body, name=name, grid=(N_KV_HEADS, nb + 1),
        in_specs=[qspec, kp_spec, kc_spec, kp_spec, kc_spec, qspec,
                  pl.BlockSpec((R, 128), lambda kv, i: (kv, 0)),
                  pl.BlockSpec((G, BLOCK, 2 * BLOCK), lambda kv, i: (kv, 0, 0))],
        out_specs=[qspec, kp_spec, kp_spec,
                   pl.BlockSpec((R, 128), lambda kv, i: (kv, 0)),
                   pl.BlockSpec((G, BLOCK, 2 * BLOCK), lambda kv, i: (kv, 0, 0))],
        out_shape=[jax.ShapeDtypeStruct((N_Q_HEADS, s, HEAD_DIM), BF16),
                   jax.ShapeDtypeStruct((N_KV_HEADS, s, HEAD_DIM), BF16),
                   jax.ShapeDtypeStruct((N_KV_HEADS, s, HEAD_DIM), BF16),
                   jax.ShapeDtypeStruct((N_Q_HEADS * BLOCK, 128), F32),
                   jax.ShapeDtypeStruct((N_Q_HEADS, BLOCK, 2 * BLOCK), F32)],
        scratch_shapes=[pltpu.VMEM((BLOCK, HEAD_DIM), F32), pltpu.VMEM((BLOCK, HEAD_DIM), F32)],
        compiler_params=_cparams(("arbitrary", "arbitrary")),
    )(qh, kh, kh, vh, vh, doh, sinks, bias)


def _cmul(ar, ai, br, bi):
    return ar * br - ai * bi, ar * bi + ai * br


def _scan_passes(a_ref, b_ref, x_ref, xp_ref, da_ref, *, s, tc, reverse):
    nc = SCAN_CHUNKS
    steps = s // nc
    with_da = xp_ref is not None
    unroll = 8 if steps % 8 == 0 else 1

    def shift(v, d):
        row = lax.broadcasted_iota(jnp.int32, v.shape, 0)
        if reverse:
            return jnp.where(row < nc - d, pltpu.roll(v, nc - d, 0), 0.0)
        return jnp.where(row >= d, pltpu.roll(v, d, 0), 0.0)

    def run():
        ar = jnp.broadcast_to(a_ref[0], (nc, tc))
        ai = jnp.broadcast_to(a_ref[1], (nc, tc))

        def row_of(step):
            j = (steps - 1 - step) if reverse else step
            return pl.multiple_of(j * nc, nc)

        def p1(step, st):
            sr, si = st
            r0 = row_of(step)
            mr, mi = _cmul(ar, ai, sr, si)
            sr = mr + b_ref[0, pl.ds(r0, nc), :]
            si = mi + b_ref[1, pl.ds(r0, nc), :]
            x_ref[0, pl.ds(r0, nc), :] = sr
            x_ref[1, pl.ds(r0, nc), :] = si
            return sr, si
        zero = jnp.zeros((nc, tc), F32)
        er, ei = lax.fori_loop(0, steps, p1, (zero, zero), unroll=unroll)

        pr, pi_ = jnp.ones((nc, tc), F32), zero
        br, bi, left = ar, ai, steps
        while left:
            if left & 1:
                pr, pi_ = _cmul(pr, pi_, br, bi)
            br, bi = _cmul(br, bi, br, bi)
            left >>= 1
        cr, ci = shift(er, 1), shift(ei, 1)
        d = 1
        while d < nc:
            mr, mi = _cmul(pr, pi_, shift(cr, d), shift(ci, d))
            cr, ci = cr + mr, ci + mi
            pr, pi_ = _cmul(pr, pi_, pr, pi_)
            d *= 2

        def p2(step, st):
            qr, qi, dar, dai = st
            r0 = row_of(step)
            qr, qi = _cmul(ar, ai, qr, qi)
            fr, fi = _cmul(qr, qi, cr, ci)
            xr = x_ref[0, pl.ds(r0, nc), :] + fr
            xi = x_ref[1, pl.ds(r0, nc), :] + fi
            x_ref[0, pl.ds(r0, nc), :] = xr
            x_ref[1, pl.ds(r0, nc), :] = xi
            if with_da:
                jm = jnp.where(step == steps - 1, steps - 1, steps - 2 - step)
                rp = pl.multiple_of(jm * nc, nc)
                vr, vi = xp_ref[0, pl.ds(rp, nc), :], xp_ref[1, pl.ds(rp, nc), :]
                row = lax.broadcasted_iota(jnp.int32, (nc, tc), 0)
                first = step == steps - 1
                sel = jnp.logical_and(first, row == 0)
                vr = jnp.where(sel, 0.0, jnp.where(first, pltpu.roll(vr, 1, 0), vr))
                vi = jnp.where(sel, 0.0, jnp.where(first, pltpu.roll(vi, 1, 0), vi))
                dar = dar + xr * vr + xi * vi
                dai = dai + xi * vr - xr * vi
            return qr, qi, dar, dai
        _, _, dar, dai = lax.fori_loop(0, steps, p2, (jnp.ones((nc, tc), F32), zero, zero, zero), unroll=unroll)
        if with_da:
            da_ref[0] = jnp.sum(dar, axis=0, keepdims=True)
            da_ref[1] = jnp.sum(dai, axis=0, keepdims=True)

    run()


def _ssm_fwd(u, bd, cd, a, d_row, *, name, sb, sbn):
    s, w = u.shape
    nst = a.shape[2]
    nblk = w // sb
    rows = min(512, s)
    nn = (((1,), (0,)), ((), ()))

    def body(u_ref, bre_ref, bim_ref, cre_ref, cim_ref, a_ref, d_ref, y_ref, x_ref):

        def fill(r, carry):
            r0 = pl.multiple_of(r * rows, rows)
            ub = u_ref[pl.ds(r0, rows), :].astype(BF16)
            x_ref[0, pl.ds(r0, rows), :] = lax.dot_general(ub, bre_ref[...].astype(BF16), nn, preferred_element_type=F32)
            x_ref[1, pl.ds(r0, rows), :] = lax.dot_general(ub, bim_ref[...].astype(BF16), nn, preferred_element_type=F32)
            return carry
        lax.fori_loop(0, s // rows, fill, 0)
        _scan_passes(a_ref, x_ref, x_ref, None, None, s=s, tc=sbn, reverse=False)

        def project(r, carry):
            r0 = pl.multiple_of(r * rows, rows)
            y = lax.dot_general(x_ref[0, pl.ds(r0, rows), :].astype(BF16), cre_ref[...].astype(BF16), nn, preferred_element_type=F32)
            y = y + lax.dot_general(x_ref[1, pl.ds(r0, rows), :].astype(BF16), cim_ref[...].astype(BF16), nn, preferred_element_type=F32)
            y_ref[pl.ds(r0, rows), :] = y + d_ref[...] * u_ref[pl.ds(r0, rows), :]
            return carry
        lax.fori_loop(0, s // rows, project, 0)

    return pl.pallas_call(
        body, name=name, grid=(nblk,),
        in_specs=[pl.BlockSpec((s, sb), lambda j: (0, j)),
                  pl.BlockSpec((sb, sbn), lambda j: (j, j)), pl.BlockSpec((sb, sbn), lambda j: (j, nblk + j)),
                  pl.BlockSpec((sbn, sb), lambda j: (j, j)), pl.BlockSpec((sbn, sb), lambda j: (nblk + j, j)),
                  pl.BlockSpec((2, 1, sbn), lambda j: (0, 0, j)), pl.BlockSpec((1, sb), lambda j: (0, j))],
        out_specs=[pl.BlockSpec((s, sb), lambda j: (0, j)), pl.BlockSpec((2, s, sbn), lambda j: (0, 0, j))],
        out_shape=[jax.ShapeDtypeStruct((s, w), F32), jax.ShapeDtypeStruct((2, s, nst), F32)],
        compiler_params=pltpu.CompilerParams(dimension_semantics=("arbitrary",), vmem_limit_bytes=SSM_VMEM_LIMIT_BYTES),
    )(u, bd, bd, cd, cd, a, d_row)


def _ssm_bwd(dy, u, xs, bd, cd, a, d_row, *, name, sb, sbn, deps=()):
    s, w = u.shape
    nst = a.shape[2]
    nblk = w // sb
    rows = min(512, s)
    nt = (((1,), (1,)), ((), ()))
    tn = (((0,), (0,)), ((), ()))

    def body(dy_ref, u_ref, xs_hbm, bre_ref, bim_ref, cre_ref, cim_ref, a_ref, d_ref, *rest):
        du_ref, gb_ref, gc_ref, da_ref, gd_ref, lam, xs_ref, sem = rest[len(deps):]
        j = pl.program_id(0)
        fetch = pltpu.make_async_copy(xs_hbm.at[:, :, pl.ds(pl.multiple_of(j * sbn, sbn), sbn)], xs_ref, sem)
        fetch.start()

        def fill(r, carry):
            r0 = pl.multiple_of(r * rows, rows)
            dyb = dy_ref[pl.ds(r0, rows), :].astype(BF16)
            lam[0, pl.ds(r0, rows), :] = lax.dot_general(dyb, cre_ref[...].astype(BF16), nt, preferred_element_type=F32)
            lam[1, pl.ds(r0, rows), :] = lax.dot_general(dyb, cim_ref[...].astype(BF16), nt, preferred_element_type=F32)
            return carry
        lax.fori_loop(0, s // rows, fill, 0)
        fetch.wait()
        _scan_passes(a_ref, lam, lam, xs_ref, da_ref, s=s, tc=sbn, reverse=True)
        gb_ref[...] = jnp.zeros_like(gb_ref)
        gc_ref[...] = jnp.zeros_like(gc_ref)
        gd_ref[...] = jnp.zeros_like(gd_ref)

        def project(r, carry):
            r0 = pl.multiple_of(r * rows, rows)
            dyv, uv = dy_ref[pl.ds(r0, rows), :], u_ref[pl.ds(r0, rows), :]
            dyb, ub = dyv.astype(BF16), uv.astype(BF16)
            lr, li = lam[0, pl.ds(r0, rows), :].astype(BF16), lam[1, pl.ds(r0, rows), :].astype(BF16)
            du = lax.dot_general(lr, bre_ref[...].astype(BF16), nt, preferred_element_type=F32)
            du = du + lax.dot_general(li, bim_ref[...].astype(BF16), nt, preferred_element_type=F32)
            du_ref[pl.ds(r0, rows), :] = du + d_ref[...] * dyv
            gb_ref[:, 0:sbn] += lax.dot_general(ub, lr, tn, preferred_element_type=F32)
            gb_ref[:, sbn:2 * sbn] += lax.dot_general(ub, li, tn, preferred_element_type=F32)
            gc_ref[0] += lax.dot_general(xs_ref[0, pl.ds(r0, rows), :].astype(BF16), dyb, tn, preferred_element_type=F32)
            gc_ref[1] += lax.dot_general(xs_ref[1, pl.ds(r0, rows), :].astype(BF16), dyb, tn, preferred_element_type=F32)
            gd_ref[...] += jnp.sum(dyv * uv, axis=0, keepdims=True)
            return carry
        lax.fori_loop(0, s // rows, project, 0)

    col = lambda j: (0, j)
    return pl.pallas_call(
        body, name=name, grid=(nblk,),
        in_specs=[pl.BlockSpec((s, sb), col), pl.BlockSpec((s, sb), col), pl.BlockSpec(memory_space=pl.ANY),
                  pl.BlockSpec((sb, sbn), lambda j: (j, j)), pl.BlockSpec((sb, sbn), lambda j: (j, nblk + j)),
                  pl.BlockSpec((sbn, sb), lambda j: (j, j)), pl.BlockSpec((sbn, sb), lambda j: (nblk + j, j)),
                  pl.BlockSpec((2, 1, sbn), lambda j: (0, 0, j)), pl.BlockSpec((1, sb), col)]
        + [pl.BlockSpec(memory_space=pl.ANY)] * len(deps),
        out_specs=[pl.BlockSpec((s, sb), col), pl.BlockSpec((sb, 2 * sbn), lambda j: (j, 0)),
                   pl.BlockSpec((2, sbn, sb), lambda j: (0, j, 0)), pl.BlockSpec((2, 1, sbn), lambda j: (0, 0, j)),
                   pl.BlockSpec((1, sb), col)],
        out_shape=[jax.ShapeDtypeStruct((s, w), F32), jax.ShapeDtypeStruct((w, 2 * sbn), F32),
                   jax.ShapeDtypeStruct((2, nst, sb), F32), jax.ShapeDtypeStruct((2, 1, nst), F32),
                   jax.ShapeDtypeStruct((1, w), F32)],
        scratch_shapes=[pltpu.VMEM((2, s, sbn), F32), pltpu.VMEM((2, s, sbn), F32), pltpu.SemaphoreType.DMA],
        compiler_params=pltpu.CompilerParams(dimension_semantics=("arbitrary",), vmem_limit_bytes=SSM_VMEM_LIMIT_BYTES),
    )(dy, u, xs, bd, bd, cd, cd, a, d_row, *deps)


def _adamw_math(w, g, m, v):
    nm = ADAM_B1 * m + (1.0 - ADAM_B1) * g
    nv = ADAM_B2 * v + (1.0 - ADAM_B2) * (g * g)
    m_hat = nm / (1.0 - ADAM_B1 ** ADAM_STEP)
    v_hat = nv / (1.0 - ADAM_B2 ** ADAM_STEP)
    return -ADAM_LR * (m_hat / (jnp.sqrt(v_hat) + ADAM_EPS) + ADAM_WD * w), nm, nv


def _adamw_many(ws, gs, ms, vs, name, deps=()):
    n, nd = len(ws), len(deps)

    def body(*refs):
        outs = refs[4 * n + nd:]
        for i in range(n):
            d, nm, nv = _adamw_math(refs[i][...], refs[n + i][...], refs[2 * n + i][...], refs[3 * n + i][...])
            outs[i][...], outs[n + i][...], outs[2 * n + i][...] = d, nm, nv

    whole = pl.BlockSpec(memory_space=pltpu.VMEM)
    res = pl.pallas_call(
        body, name=name, in_specs=[whole] * (4 * n) + [pl.BlockSpec(memory_space=pl.ANY)] * nd,
        out_specs=[whole] * (3 * n), out_shape=[jax.ShapeDtypeStruct(w.shape, F32) for w in ws] * 3,
        compiler_params=pltpu.CompilerParams(vmem_limit_bytes=VMEM_LIMIT_BYTES),
    )(*ws, *gs, *ms, *vs, *deps)
    return res[:n], res[n:2 * n], res[2 * n:]


def _adamw(w, g, m, v, name, deps=(), echo=False):
    nd = len(deps)
    r, c = w.shape
    tr = r
    for cand in (512, 256, 128, 64, 32, 16, 8):
        if r % cand == 0 and cand * c * 4 <= 2 * 1024 * 1024:
            tr = cand
            break

    def body(w_ref, g_ref, m_ref, v_ref, *rest):
        d_ref, nm_ref, nv_ref = rest[nd:nd + 3]
        gv = g_ref[...]
        d_ref[...], nm_ref[...], nv_ref[...] = _adamw_math(w_ref[...], gv, m_ref[...], v_ref[...])
        if echo:
            rest[nd + 3][...] = gv

    no = 4 if echo else 3
    spec = pl.BlockSpec((tr, c), lambda i: (i, 0))
    sds = jax.ShapeDtypeStruct((r, c), F32)
    return pl.pallas_call(body, name=name, grid=(r // tr,),
                          in_specs=[spec] * 4 + [pl.BlockSpec(memory_space=pl.ANY)] * nd, out_specs=[spec] * no,
                          out_shape=[sds] * no, compiler_params=_cparams(("parallel",)))(w, g, m, v, *deps)


def _sum_lead(x, name, out_dtype=F32):
    n, r, c = x.shape
    tr = r
    for cand in (512, 256, 128, 64, 32, 16, 8):
        if r % cand == 0 and n * cand * c * 4 <= 4 * 1024 * 1024:
            tr = cand
            break

    def body(x_ref, o_ref):
        acc = x_ref[0].astype(F32)
        for k in range(1, n):
            acc = acc + x_ref[k].astype(F32)
        o_ref[...] = acc.astype(o_ref.dtype)

    return pl.pallas_call(body, name=name, grid=(r // tr,),
                          in_specs=[pl.BlockSpec((n, tr, c), lambda i: (0, i, 0))],
                          out_specs=pl.BlockSpec((tr, c), lambda i: (i, 0)),
                          out_shape=jax.ShapeDtypeStruct((r, c), out_dtype),
                          compiler_params=_cparams(("parallel",)))(x)


def _row_tile(rows, row_bytes, budget, least=8):
    for cand in (1024, 512, 256, 128, 64, 32, 16, 8):
        if cand >= least and rows % cand == 0 and cand * row_bytes <= budget:
            return cand
    return rows


def _cast_into_slot(w, slot, name):
    r, c = w.shape
    tr = _row_tile(r, c * 4, 4 * 1024 * 1024, least=16)

    def body(slot_ref, w_ref, o_ref):
        o_ref[...] = w_ref[...].astype(o_ref.dtype)

    gs = pltpu.PrefetchScalarGridSpec(
        num_scalar_prefetch=1, grid=(r // tr,),
        in_specs=[pl.BlockSpec((tr, c), lambda i, s: (i, 0))],
        out_specs=pl.BlockSpec((None, tr, c), lambda i, s: (s[0], i, 0)))
    return pl.pallas_call(body, name=name, grid_spec=gs, out_shape=jax.ShapeDtypeStruct((N_CHIPS, r, c), BF16),
                          compiler_params=_cparams(("parallel",)))(slot, w)


def _sum_own(p, t, sel, name):
    _, h, c = p.shape
    tr = _row_tile(h, c * 4, 2 * 1024 * 1024, least=16)
    nblk = h // tr

    def body(sel_ref, p_ref, t_ref, o_ref):
        acc = p_ref[...].astype(F32)
        for k in range(3):
            acc = acc + t_ref[k].astype(F32)
        o_ref[...] = acc

    gs = pltpu.PrefetchScalarGridSpec(
        num_scalar_prefetch=1, grid=(nblk,),
        in_specs=[pl.BlockSpec((None, tr, c), lambda i, s: (s[0], i, 0)),
                  pl.BlockSpec((3, tr, c), lambda i, s: (0, i, 0))],
        out_specs=pl.BlockSpec((tr, c), lambda i, s: (s[1] * nblk + i, 0)))
    return pl.pallas_call(body, name=name, grid_spec=gs, out_shape=jax.ShapeDtypeStruct((2 * h, c), F32),
                          compiler_params=_cparams(("parallel",)))(sel, p, t)


def _add_half(g, t, half, name):
    n, r, c = g.shape
    h = r // 2
    tr = h
    for cand in (512, 256, 128, 64, 32, 16):
        if h % cand == 0 and cand * c * 2 <= 2 * 1024 * 1024:
            tr = cand
            break
    nblk = h // tr

    def body(half_ref, g_ref, t_ref, o_ref):
        o_ref[...] = (g_ref[...].astype(F32) + t_ref[...].astype(F32)).astype(o_ref.dtype)

    gs = pltpu.PrefetchScalarGridSpec(
        num_scalar_prefetch=1, grid=(n, nblk),
        in_specs=[pl.BlockSpec((None, tr, c), lambda j, i, hr: (j, hr[0] * nblk + i, 0)),
                  pl.BlockSpec((None, tr, c), lambda j, i, hr: (j, i, 0))],
        out_specs=pl.BlockSpec((None, tr, c), lambda j, i, hr: (j, i, 0)))
    return pl.pallas_call(body, name=name, grid_spec=gs, out_shape=jax.ShapeDtypeStruct((n, h, c), BF16),
                          compiler_params=_cparams(("parallel", "parallel")))(half, g, t)


def _position():
    x, y, c = lax.axis_index("x"), lax.axis_index("y"), lax.axis_index("c")
    return x, y, c


def _allgather8(xs, name):
    m_per, n = xs.shape

    def body(x_ref, out_ref, send_sems, recv_sems, local_sem):
        x, y, c = _position()
        me, sibling = (x, y, c), (x, y, 1 - c)
        chips = [(1 - x, y), (x, 1 - y), (1 - x, 1 - y)]

        def rows(px, py, pc):
            return out_ref.at[pl.ds((4 * px + 2 * py + pc) * m_per, m_per), :]

        def copy(k, block, to, src=None):
            return pltpu.make_async_remote_copy(
                src_ref=rows(*block) if src is None else src, dst_ref=rows(*block),
                send_sem=send_sems.at[k], recv_sem=recv_sems.at[k], device_id=to, device_id_type=MESH)

        mine = pltpu.make_async_copy(x_ref, rows(*me), local_sem)
        mine.start()
        first = [copy(0, me, sibling, src=x_ref)]
        first += [copy(1 + j, me, (*chip, c), src=x_ref) for j, chip in enumerate(chips)]
        for cp in first:
            cp.start()
        passed = [copy(4 + j, (*chip, c), sibling) for j, chip in enumerate(chips)]
        for j, chip in enumerate(chips):
            copy(1 + j, (*chip, c), me).wait_recv()
            passed[j].start()
        copy(0, sibling, me).wait_recv()
        for j, chip in enumerate(chips):
            copy(4 + j, (*chip, 1 - c), me).wait_recv()
        for cp in first + passed:
            cp.wait_send()
        mine.wait()

    return pl.pallas_call(
        body, name=name, out_shape=jax.ShapeDtypeStruct((N_DEV * m_per, n), xs.dtype),
        in_specs=[pl.BlockSpec(memory_space=pltpu.VMEM)], out_specs=pl.BlockSpec(memory_space=pltpu.VMEM),
        scratch_shapes=[pltpu.SemaphoreType.DMA((7,)), pltpu.SemaphoreType.DMA((7,)), pltpu.SemaphoreType.DMA],
        compiler_params=pltpu.CompilerParams(vmem_limit_bytes=VMEM_LIMIT_BYTES),
    )(xs)


_HBM = pl.BlockSpec(memory_space=pltpu.HBM)


_SEM = pl.BlockSpec(memory_space=pltpu.SEMAPHORE)
_ANY = pl.BlockSpec(memory_space=pl.ANY)
_EFFECT = pltpu.SideEffectType.DATAFLOW_SIDE_EFFECTING


def _in_hbm(a):
    return pltpu.with_memory_space_constraint(a, pltpu.HBM)


def _several(after):
    return list(after) if isinstance(after, (list, tuple)) else [after]


def _gather_start(ws, groups, after, name):
    n = len(ws)
    after = _several(after)

    def body(*refs):
        in_refs = refs[:n]
        sems, token = refs[2 * n + len(after):-1], refs[-1]
        x, y, c = _position()
        mychip = 2 * x + y
        chips = [(1 - x, y), (x, 1 - y), (1 - x, 1 - y)]
        for g, members in enumerate(groups):
            for k, i in enumerate(members):
                h = ws[i].shape[1] // 2
                mine = in_refs[i].at[mychip, pl.ds(c * h, h), :]
                for j, (px, py) in enumerate(chips):
                    pltpu.make_async_remote_copy(
                        src_ref=mine, dst_ref=mine, send_sem=sems[2 * g].at[3 * k + j],
                        recv_sem=sems[2 * g + 1].at[3 * k + j], device_id=(px, py, c), device_id_type=MESH).start()
        token[...] = jnp.zeros_like(token)

    sem_shapes = [pltpu.SemaphoreType.DMA((3 * len(m),)) for m in groups for _ in range(2)]
    res = pl.pallas_call(
        body, name=name,
        out_shape=[pltpu.HBM(w.shape, w.dtype) for w in ws] + sem_shapes + [jax.ShapeDtypeStruct((8, 128), F32)],
        in_specs=[_HBM] * n + [_ANY] * len(after),
        out_specs=[_HBM] * n + [_SEM] * len(sem_shapes) + [pl.BlockSpec(memory_space=pltpu.VMEM)],
        input_output_aliases={i: i for i in range(n)},
        compiler_params=pltpu.CompilerParams(has_side_effects=_EFFECT),
    )(*[_in_hbm(w) for w in ws], *after)
    bufs, sems, token = res[:n], res[n:-1], res[-1]
    return list(bufs), [(sems[2 * g], sems[2 * g + 1]) for g in range(len(groups))], token


def _gather_wait(bufs, send_sems, recv_sems, after, name):
    m = len(bufs)

    def body(*refs):
        in_refs = refs[:m]
        send, recv = refs[m], refs[m + 1]
        x, y, c = _position()
        mychip = 2 * x + y
        chips = [(1 - x, y), (x, 1 - y), (1 - x, 1 - y)]
        for k in range(m):
            h = bufs[k].shape[1] // 2
            mine = in_refs[k].at[mychip, pl.ds(c * h, h), :]
            for j, (px, py) in enumerate(chips):
                cp = pltpu.make_async_remote_copy(
                    src_ref=mine, dst_ref=in_refs[k].at[2 * px + py, pl.ds(c * h, h), :],
                    send_sem=send.at[3 * k + j], recv_sem=recv.at[3 * k + j],
                    device_id=(px, py, c), device_id_type=MESH)
                cp.wait_send()
                cp.wait_recv()

    res = pl.pallas_call(
        body, name=name, out_shape=[pltpu.HBM(b.shape, b.dtype) for b in bufs],
        in_specs=[_HBM] * m + [_SEM, _SEM] + [_ANY] * len(_several(after)), out_specs=[_HBM] * m,
        input_output_aliases={k: k for k in range(m)},
        compiler_params=pltpu.CompilerParams(has_side_effects=_EFFECT),
    )(*bufs, send_sems, recv_sems, *_several(after))
    return list(res)


def _forward_halves(ws, name):
    n = len(ws)

    def body(*refs):
        out_refs = refs[n:2 * n]
        send_sems, recv_sems = refs[2 * n:]
        x, y, c = _position()
        me, sibling = (x, y, c), (x, y, 1 - c)
        chips = [(1 - x, y), (x, 1 - y), (1 - x, 1 - y)]
        cps = []
        for i in range(n):
            h = ws[i].shape[1] // 2
            for j, (px, py) in enumerate(chips):
                got = out_refs[i].at[2 * px + py, pl.ds(c * h, h), :]
                cp = pltpu.make_async_remote_copy(
                    src_ref=got, dst_ref=got, send_sem=send_sems.at[3 * i + j], recv_sem=recv_sems.at[3 * i + j],
                    device_id=sibling, device_id_type=MESH)
                cp.start()
                cps.append(cp)
        for i in range(n):
            h = ws[i].shape[1] // 2
            for j, (px, py) in enumerate(chips):
                other = out_refs[i].at[2 * px + py, pl.ds((1 - c) * h, h), :]
                pltpu.make_async_remote_copy(
                    src_ref=other, dst_ref=other, send_sem=send_sems.at[3 * i + j], recv_sem=recv_sems.at[3 * i + j],
                    device_id=me, device_id_type=MESH).wait_recv()
        for cp in cps:
            cp.wait_send()

    return pl.pallas_call(
        body, name=name,
        out_shape=[jax.ShapeDtypeStruct(w.shape, w.dtype) for w in ws],
        in_specs=[_HBM] * n, out_specs=[_HBM] * n, input_output_aliases={i: i for i in range(n)},
        scratch_shapes=[pltpu.SemaphoreType.DMA((3 * n,)), pltpu.SemaphoreType.DMA((3 * n,))],
    )(*ws)


def _copies_start(arrays, copies, nsem, after, name):
    n = len(arrays)
    after = _several(after)
    first = 2 * n + len(after)

    def body(*refs):
        for cp in copies(refs[:n], refs[first], refs[first + 1]):
            cp.start()
        refs[first + 2][...] = jnp.zeros_like(refs[first + 2])

    res = pl.pallas_call(
        body, name=name,
        out_shape=[pltpu.HBM(a.shape, a.dtype) for a in arrays]
        + [pltpu.SemaphoreType.DMA((nsem,)), pltpu.SemaphoreType.DMA((nsem,)), jax.ShapeDtypeStruct((8, 128), F32)],
        in_specs=[_HBM] * n + [_ANY] * len(after),
        out_specs=[_HBM] * n + [_SEM, _SEM, pl.BlockSpec(memory_space=pltpu.VMEM)],
        input_output_aliases={i: i for i in range(n)},
        compiler_params=pltpu.CompilerParams(has_side_effects=_EFFECT),
    )(*[_in_hbm(a) for a in arrays], *after)
    return list(res[:n]), res[n], res[n + 1], res[n + 2]


def _copies_wait(arrays, copies, send_sems, recv_sems, after, name):
    n = len(arrays)

    def body(*refs):
        for cp in copies(refs[:n], refs[n], refs[n + 1]):
            cp.wait_send()
            cp.wait_recv()

    res = pl.pallas_call(
        body, name=name, out_shape=[pltpu.HBM(a.shape, a.dtype) for a in arrays],
        in_specs=[_HBM] * n + [_SEM, _SEM] + [_ANY] * len(_several(after)), out_specs=[_HBM] * n,
        input_output_aliases={i: i for i in range(n)},
        compiler_params=pltpu.CompilerParams(has_side_effects=_EFFECT),
    )(*arrays, send_sems, recv_sems, *_several(after))
    return list(res)


def _scatter_copies(refs, send, recv):
    n = len(refs) // 2
    x, y, c = _position()
    chips = [(1 - x, y), (x, 1 - y), (1 - x, 1 - y)]
    return [pltpu.make_async_remote_copy(
        src_ref=refs[i].at[2 * px + py], dst_ref=refs[n + i].at[j],
        send_sem=send.at[3 * i + j], recv_sem=recv.at[3 * i + j], device_id=(px, py, c), device_id_type=MESH)
        for i in range(n) for j, (px, py) in enumerate(chips)]


def _swap_copies(refs, send, recv):
    n = len(refs) // 2
    x, y, c = _position()
    cps = []
    for i in range(n):
        h = refs[i].shape[1] // 2
        cps.append(pltpu.make_async_remote_copy(
            src_ref=refs[i].at[:, pl.ds((1 - c) * h, h), :], dst_ref=refs[n + i],
            send_sem=send.at[i], recv_sem=recv.at[i], device_id=(x, y, 1 - c), device_id_type=MESH))
    return cps


def _join_copies(refs, send, recv):
    x, y, c = _position()
    cps = []
    for i, r in enumerate(refs):
        h = r.shape[0] // 2
        mine = r.at[pl.ds(c * h, h), :]
        cps.append(pltpu.make_async_remote_copy(
            src_ref=mine, dst_ref=mine, send_sem=send.at[i], recv_sem=recv.at[i],
            device_id=(x, y, 1 - c), device_id_type=MESH))
    return cps


def _forward_copies(refs, send, recv):
    x, y, c = _position()
    chips = [(1 - x, y), (x, 1 - y), (1 - x, 1 - y)]
    cps = []
    for i, r in enumerate(refs):
        h = r.shape[1] // 2
        for j, (px, py) in enumerate(chips):
            got = r.at[2 * px + py, pl.ds(c * h, h), :]
            cps.append(pltpu.make_async_remote_copy(
                src_ref=got, dst_ref=got, send_sem=send.at[3 * i + j], recv_sem=recv.at[3 * i + j],
                device_id=(x, y, 1 - c), device_id_type=MESH))
    return cps


def _t5_buckets_block():
    qi = np.arange(BLOCK)[:, None]
    ki = np.arange(2 * BLOCK)[None, :]
    n = np.maximum(qi + BLOCK - ki, 0)
    max_exact = NUM_BUCKETS // 2
    large = max_exact + (np.log(np.maximum(n, 1) / max_exact) / np.log(MAX_DISTANCE / max_exact)
                         * (NUM_BUCKETS - max_exact)).astype(np.int32)
    large = np.minimum(large, NUM_BUCKETS - 1)
    return np.where(n < max_exact, n, large).astype(np.int32)


def _discretise(lambda_re, lambda_im, log_step, b_re, b_im):
    lam_re = jnp.minimum(lambda_re, -1e-4)
    lam_im = lambda_im
    delta = jnp.exp(log_step)
    mag = jnp.exp(lam_re * delta)
    ang = lam_im * delta
    abar_re, abar_im = mag * jnp.cos(ang), mag * jnp.sin(ang)
    num_re, num_im = abar_re - 1.0, abar_im
    den = lam_re * lam_re + lam_im * lam_im
    f_re = (num_re * lam_re + num_im * lam_im) / den
    f_im = (num_im * lam_re - num_re * lam_im) / den
    bbar_re = f_re * b_re - f_im * b_im
    bbar_im = f_re * b_im + f_im * b_re
    return abar_re, abar_im, bbar_re, bbar_im


def _ssm_params(args, cotangents, name):
    whole = pl.BlockSpec(memory_space=pltpu.VMEM)
    n_in = len(args)

    def body(*refs):
        vals = [r[...] for r in refs[:n_in]]
        if cotangents is None:
            outs = _discretise(*vals)
        else:
            outs = jax.vjp(_discretise, *vals)[1](tuple(r[...] for r in refs[n_in:n_in + 4]))
        for o_ref, o in zip(refs[-len(outs):], outs):
            o_ref[...] = o

    if cotangents is None:
        like, operands = [args[0], args[0], args[3], args[3]], list(args)
    else:
        like, operands = list(args), list(args) + list(cotangents)
    return pl.pallas_call(body, name=name, in_specs=[whole] * len(operands), out_specs=[whole] * len(like),
                          out_shape=[jax.ShapeDtypeStruct(a.shape, F32) for a in like])(*operands)


def _interleave(v, nc):
    s, w = v.shape
    return v.reshape(nc, s // nc, w).transpose(1, 0, 2).reshape(s, w)


def _deinterleave(v, nc):
    s, w = v.shape
    return v.reshape(s // nc, nc, w).transpose(1, 0, 2).reshape(s, w)


_SMALL = ("norm1_g", "b_in", "attn_sinks", "rel_bias", "lambda_re", "lambda_im", "log_step", "ssm_b_re",
          "ssm_b_im", "ssm_c_re", "ssm_c_im", "ssm_d", "b_glu", "norm2_g", "final_g")


def _pack(parts):
    rows = []
    for p in parts:
        f = p.reshape(-1).astype(F32)
        pad = (-f.shape[0]) % 128
        rows.append(jnp.pad(f, (0, pad)).reshape(-1, 128))
    out = jnp.concatenate(rows, axis=0)
    pad = (-out.shape[0]) % 256
    return jnp.pad(out, ((0, pad), (0, 0)))


def _unpack(packed, shapes):
    res, r = [], 0
    for shp in shapes:
        size = int(np.prod(shp))
        nr = -(-size // 128)
        res.append(packed[r:r + nr].reshape(-1)[:size].reshape(shp))
        r += nr
    return res


def kernel(x, c, w_ada, b_ada, norm1_g, w_in, b_in, attn_sinks, rel_bias, lambda_re, lambda_im, log_step, ssm_b_re, ssm_b_im, ssm_c_re, ssm_c_im, ssm_d, w_glu, b_glu, w_attn_proj, w_ssm_proj, w_out, norm2_g, w_ff1, w_ff2, final_g, loss_target, m_w_ada, m_b_ada, m_norm1_g, m_w_in, m_b_in, m_attn_sinks, m_rel_bias, m_lambda_re, m_lambda_im, m_log_step, m_ssm_b_re, m_ssm_b_im, m_ssm_c_re, m_ssm_c_im, m_ssm_d, m_w_glu, m_b_glu, m_w_attn_proj, m_w_ssm_proj, m_w_out, m_norm2_g, m_w_ff1, m_w_ff2, m_final_g, v_w_ada, v_b_ada, v_norm1_g, v_w_in, v_b_in, v_attn_sinks, v_rel_bias, v_lambda_re, v_lambda_im, v_log_step, v_ssm_b_re, v_ssm_b_im, v_ssm_c_re, v_ssm_c_im, v_ssm_d, v_w_glu, v_b_glu, v_w_attn_proj, v_w_ssm_proj, v_w_out, v_norm2_g, v_w_ff1, v_w_ff2, v_final_g):
    given = dict(locals())
    S, D = x.shape[1], x.shape[2]
    SSM_W = w_glu.shape[2]
    G = SSM_W // SSM_GROUP_CH
    NST = G * SSM_STATE
    DFF = w_ff2.shape[1] * N_CHIPS
    INW = w_in.shape[2] * N_CHIPS
    o_q, o_k, o_v, o_u = 0, ATTN_WIDTH, ATTN_WIDTH + KV_WIDTH, ATTN_WIDTH + 2 * KV_WIDTH
    o_ga, o_gs = o_u + SSM_W, o_u + SSM_W + D
    mx, my, mc = _position()
    my_chip = 2 * mx + my
    my_b = 4 * mx + 2 * my + mc

    xv, tgt = x[0], loss_target[0]

    big = dict(w_in=w_in[0], w_glu=w_glu[0], w_attn_proj=w_attn_proj[0], w_ssm_proj=w_ssm_proj[0],
               w_out=w_out[0], w_ff1=w_ff1[0], w_ff2=w_ff2[0])
    big_names = list(big)
    colsharded = {"w_in", "w_attn_proj", "w_ssm_proj", "w_ff1"}
    chip_sel = my_chip.astype(jnp.int32).reshape(1)
    gather_groups = [["w_in"], ["w_attn_proj", "w_ssm_proj", "w_glu", "w_out"], ["w_ff1", "w_ff2"]]
    in_flight, gather_sems, gathered = {}, [], {}

    def finish_gather(g, after):
        bufs = [in_flight[k] for k in gather_groups[g]]
        bufs = _gather_wait(bufs, gather_sems[g][0], gather_sems[g][1], after, "gather_wait_%d" % g)
        gathered.update(zip(gather_groups[g], _forward_halves(bufs, "gather_forward_%d" % g)))

    def tied(v, token):
        return v + token[0:1, 0:1]

    def all_of(*arrays):
        return list(arrays)

    def wop(k):
        g = gathered[k]
        return _Op(g, N_CHIPS) if k in colsharded else _Op(g.reshape(g.shape[0] * g.shape[1], g.shape[2]))

    grads = {}
    nothing = jnp.zeros((8, 128), F32)
    half = mc.astype(jnp.int32).reshape(1)
    sel = jnp.stack([my_chip, mc]).astype(jnp.int32)

    def rs_swap(tag, named):
        keys, gl = list(named), []
        for k in keys:
            gk = named[k]
            if k not in colsharded:
                gk = gk.reshape(N_CHIPS, gk.shape[0] // N_CHIPS, gk.shape[1])
            gl.append(gk)
        lands = [lax.empty((g.shape[0], g.shape[1] // 2, g.shape[2]), g.dtype) for g in gl]
        arrays, ssem, rsem, token = _copies_start(gl + lands, _swap_copies, len(gl), nothing, "rs_swap_start_" + tag)
        return (keys, arrays, ssem, rsem), token

    def rs_scatter(tag, state, after):
        keys, arrays, ssem, rsem = state
        arrays = _copies_wait(arrays, _swap_copies, ssem, rsem, after, "rs_swap_wait_" + tag)
        n = len(keys)
        ps = [_add_half(g, t, half, "rs_add_" + k) for g, t, k in zip(arrays[:n], arrays[n:], keys)]
        lands = [lax.empty((3,) + p.shape[1:], p.dtype) for p in ps]
        arrays, ssem, rsem, token = _copies_start(ps + lands, _scatter_copies, 3 * n, nothing, "rs_start_" + tag)
        return (keys, arrays, ssem, rsem), token

    def rs_sum(tag, state, after):
        keys, arrays, ssem, rsem = state
        arrays = _copies_wait(arrays, _scatter_copies, ssem, rsem, after, "rs_wait_" + tag)
        n = len(keys)
        rs = [_sum_own(p, t, sel, "rs_sum_" + k) for p, t, k in zip(arrays[:n], arrays[n:], keys)]
        rs, ssem, rsem, token = _copies_start(rs, _join_copies, n, nothing, "rs_join_start_" + tag)
        return (keys, rs, ssem, rsem), token

    def rs_finish(tag, state, after):
        keys, rs, ssem, rsem = state
        for k, f in zip(keys, _copies_wait(rs, _join_copies, ssem, rsem, after, "rs_join_wait_" + tag)):
            grads[k] = f[None]

    c_all = _allgather8(jnp.pad(c, ((0, 7), (0, 0))), "gather_c").reshape(N_DEV, 8, D)[:, 0]
    c16 = jnp.pad(c_all, ((0, 8), (0, 0)))
    b_ada_mine = lax.dynamic_slice(b_ada.reshape(N_CHIPS, -1), (my_chip, 0), (1, w_ada.shape[2]))
    mod_sh = _mm(c16, w_ada[0], "NN", name="mod", M=16, N=w_ada.shape[2], K=D, a_fn=_silu,
                 epilogue=lambda acc, b: (acc + b,), extras=[(b_ada_mine, "row")])
    mod_all = _allgather8(mod_sh[:8], "gather_mod").reshape(N_DEV, 8, -1)
    mod_row = jnp.concatenate(
        [lax.dynamic_slice(mod_all, (2 * j, my_b, 0), (1, 1, mod_all.shape[2]))[0] for j in range(N_CHIPS)], axis=1)
    sh1, sc1, g1, sh2, sc2, g2 = [mod_row[:, i * D:(i + 1) * D] for i in range(6)]

    first = [_cast_into_slot(big["w_in"], chip_sel, "cast_w_in")]
    first, sems_first, token_first = _gather_start(first, [[0]], mod_all, "gather_start_in")
    rest_names = gather_groups[1] + gather_groups[2]
    rest = [_cast_into_slot(big[k], chip_sel, "cast_" + k) for k in rest_names]
    rest, sems_rest, token_rest = _gather_start(
        rest, [[rest_names.index(k) for k in grp] for grp in gather_groups[1:]], token_first, "gather_start_rest")
    in_flight.update(zip(["w_in"] + rest_names, first + rest))
    gather_sems.extend(sems_first + sems_rest)

    disc_in = (lambda_re[0].reshape(1, NST), lambda_im[0].reshape(1, NST),
               jnp.repeat(log_step[0], SSM_STATE).reshape(1, NST),
               ssm_b_re[0].transpose(2, 0, 1).reshape(SSM_GROUP_CH, NST),
               ssm_b_im[0].transpose(2, 0, 1).reshape(SSM_GROUP_CH, NST))
    abar_re, abar_im, bbar_re, bbar_im = _ssm_params(disc_in, None, "ssm_params")
    same_group = jnp.asarray(np.arange(SSM_W)[:, None] // SSM_GROUP_CH == np.arange(NST)[None, :] // SSM_STATE)

    def block_diag(t):
        return jnp.where(same_group, jnp.tile(t, (G, 1)), 0.0)

    bd = jnp.concatenate([block_diag(bbar_re), block_diag(bbar_im)], axis=1)
    cd = jnp.concatenate([block_diag(cc.transpose(1, 0, 2).reshape(SSM_GROUP_CH, NST)).T
                          for cc in (ssm_c_re[0], -ssm_c_im[0])], axis=0)
    a_fwd = jnp.stack([abar_re, abar_im])
    a_bwd = jnp.stack([abar_re, -abar_im])
    d_row = ssm_d

    buckets = _t5_buckets_block()
    onehot_t = (jnp.arange(128, dtype=jnp.int32)[:, None] == jnp.asarray(buckets.reshape(1, -1))).astype(BF16)
    rb_hi = rel_bias.astype(BF16)
    rb_lo = (rel_bias - rb_hi.astype(F32)).astype(BF16)
    rb_lo2 = (rel_bias - rb_hi.astype(F32) - rb_lo.astype(F32)).astype(BF16)
    rb3 = jnp.pad(jnp.concatenate([rb_hi.T, rb_lo.T, rb_lo2.T], axis=0), ((0, 0), (0, 128 - NUM_BUCKETS)))
    b3 = _mm(rb3, onehot_t, "NN", name="rel_bias_rows", M=3 * N_Q_HEADS, N=BLOCK * 2 * BLOCK, K=128, tj=4096)
    bias = (b3[:N_Q_HEADS] + b3[N_Q_HEADS:2 * N_Q_HEADS]) + b3[2 * N_Q_HEADS:]
    bias = bias.reshape(N_Q_HEADS, BLOCK, 2 * BLOCK)
    sinks_b = jnp.broadcast_to(attn_sinks[0][:, None, None], (N_Q_HEADS, BLOCK, 128)).reshape(N_Q_HEADS * BLOCK, 128)

    def two(fn):
        def both(*blocks):
            r = fn(*blocks)
            return r, r
        return both

    h1, h1_t = _rowwise(two(_norm_mod), [(xv, "tile", D), (tied(tied(norm1_g, token_first), token_rest), "row", D),
                                         (sh1, "row", D), (sc1, "row", D)],
                        [(D, BF16), (D, BF16, "T")], [], name="norm1", rows=S)
    finish_gather(0, all_of(h1, bd, cd, a_fwd, a_bwd, bias, sinks_b))
    proj = _mm(h1, wop("w_in"), "NN", name="proj", M=S, N=INW, K=D, out_dtypes=(BF16,),
               epilogue=lambda acc, b: (acc + b,), extras=[(b_in, "row")])

    def heads(v2d, nh):
        return v2d.reshape(S, nh, HEAD_DIM).transpose(1, 0, 2)

    def unheads(v3d):
        return v3d.transpose(1, 0, 2).reshape(S, -1)

    qh = heads(proj[:, o_q:o_k], N_Q_HEADS)
    kh = heads(proj[:, o_k:o_v], N_KV_HEADS)
    vh = heads(proj[:, o_v:o_u], N_KV_HEADS)
    attn = unheads(_attn_fwd(qh, kh, vh, sinks_b, bias, "attn_fwd"))
    finish_gather(1, attn)
    y_attn = _mm(attn, wop("w_attn_proj"), "NN", name="attn_proj", M=S, N=D, K=ATTN_WIDTH, out_dtypes=(BF16,))

    u = proj[:, o_u:o_ga]
    u_il = _interleave(u, SCAN_CHUNKS)
    SB = 128
    nsb, gpb = SSM_W // SB, SB // SSM_GROUP_CH
    SBN = gpb * SSM_STATE
    y_il, xs = _ssm_fwd(u_il, bd, cd, a_fwd, d_row, name="ssm_fwd", sb=SB, sbn=SBN)
    y = _deinterleave(y_il, SCAN_CHUNKS)
    z, t_glu = _mm(y, wop("w_glu"), "NN", name="glu", M=S, N=SSM_W, K=SSM_W, out_dtypes=(BF16, F32), a_fn=_gelu,
                   epilogue=lambda acc, b, yy: (_gelu(yy) * _sigmoid(acc + b), acc + b),
                   extras=[(b_glu, "row"), (y, "tile")])
    y_ssm = _mm(z, wop("w_ssm_proj"), "NN", name="ssm_proj", M=S, N=D, K=SSM_W, out_dtypes=(BF16,))

    ff_bufs = _gather_wait([in_flight[k] for k in gather_groups[2]], gather_sems[2][0], gather_sems[2][1], all_of(y_ssm),
                           "gather_wait_2")
    ff_bufs, ff_send, ff_recv, token = _copies_start(ff_bufs, _forward_copies, 3 * len(ff_bufs), nothing,
                                                    "gather_forward_2_start")
    merged, merged_t = _rowwise(two(_merge), [(_Op(proj, coff=o_ga), "tile", D), (_Op(proj, coff=o_gs), "tile", D),
                                              (y_attn, "tile", D), (y_ssm, "tile", D)],
                                [(D, BF16), (D, BF16, "T")], [], name="merge", rows=S)
    mo, x2 = _mm(merged, wop("w_out"), "NN", name="out_proj", M=S, N=D, K=D, out_dtypes=(BF16, F32),
                 epilogue=lambda acc, xx, gg: (acc, xx + gg * acc), extras=[(xv, "tile"), (g1, "row")], deps=[token])
    h2, h2_t = _rowwise(two(_norm_mod), [(x2, "tile", D), (norm2_g, "row", D), (sh2, "row", D), (sc2, "row", D)],
                        [(D, BF16), (D, BF16, "T")], [], name="norm2", rows=S)
    gathered.update(zip(gather_groups[2], _copies_wait(ff_bufs, _forward_copies, ff_send, ff_recv, h2,
                                                       "gather_forward_2_wait")))
    a_b, r_b = _mm(h2, wop("w_ff1"), "NN", name="ff1", M=S, N=DFF, K=D, out_dtypes=(BF16, BF16),
                   epilogue=lambda acc: (acc, jnp.square(jnp.maximum(acc, 0.0))))
    ff, x3 = _mm(r_b, wop("w_ff2"), "NN", name="ff2", M=S, N=D, K=DFF, out_dtypes=(BF16, F32),
                 epilogue=lambda acc, xx, gg: (acc, xx + gg * acc), extras=[(x2, "tile"), (g2, "row")],
                 tj=1024, tk=1024)

    def final_fn(x3b, gf, tb, ffb, g2b):
        def f(xx, gg):
            yv = xx * lax.rsqrt(jnp.mean(xx * xx, axis=-1, keepdims=True) + EPS) * gg
            err = jnp.square(yv - tb)
            return 0.5 * jnp.sum(jnp.mean(err, axis=-1, keepdims=True), axis=0, keepdims=True)
        lv, vjp = jax.vjp(f, x3b, gf)
        dx, dg = vjp(jnp.ones((1, 1), F32))
        return dx, dx * g2b, dg, jnp.broadcast_to(lv, (1, 128)), jnp.sum(dx * ffb, axis=0, keepdims=True)

    dx3, dff, g_final, loss_acc, d_g2 = _rowwise(
        final_fn, [(x3, "tile", D), (final_g.reshape(1, D), "row", D), (tgt, "tile", D), (ff, "tile", D), (g2, "row", D)],
        [(D, F32), (D, BF16)], [D, 128, D], name="final", rows=S)
    da = _mm(dff, wop("w_ff2"), "NT", name="ff2_dx", M=S, N=DFF, K=D, out_dtypes=(BF16,),
             epilogue=lambda acc, ab: (acc * (2.0 * jnp.maximum(ab.astype(F32), 0.0)),), extras=[(a_b, "tile")])
    g_w_ff2 = _mm(r_b, dff, "TN", name="ff2_dw", M=DFF, N=D, K=S, out_dtypes=(BF16,), tj=1024, tk=2048)
    g_w_ff1 = _mm(h2_t, da, "NN", name="ff1_dw", M=D, N=DFF, K=S, out_dtypes=(BF16,), out_nsh=N_CHIPS, tj=1024, tk=2048)
    rs_ff, token = rs_swap("ff", dict(w_ff2=g_w_ff2, w_ff1=g_w_ff1))
    dh2 = _mm(da, wop("w_ff1"), "NT", name="ff1_dx", M=S, N=D, K=DFF, tj=1024, tk=2048, deps=[token])
    rs_ff, token_ff = rs_scatter("ff", rs_ff, dh2)

    def norm2_bwd(x2b, dh2b, dx3b, mob, gn, shb, scb, g1b):
        _, vjp = jax.vjp(_norm_mod, x2b, gn, shb, scb)
        dx, dg, dsh, dsc = vjp(dh2b)
        dx2b = dx + dx3b
        return dx2b, dx2b * g1b, dg, dsh, dsc, jnp.sum(dx2b * mob, axis=0, keepdims=True)

    dx2, dmo, g_norm2, d_sh2, d_sc2, d_g1 = _rowwise(
        norm2_bwd, [(x2, "tile", D), (dh2, "tile", D), (dx3, "tile", D), (mo, "tile", D),
                    (tied(norm2_g, token_ff), "row", D), (sh2, "row", D), (sc2, "row", D), (g1, "row", D)],
        [(D, F32), (D, BF16)], [D, D, D, D], name="norm2_bwd", rows=S)
    dmerged = _mm(dmo, wop("w_out"), "NT", name="out_dx", M=S, N=D, K=D)
    g_w_out = _mm(merged_t, dmo, "NN", name="out_dw", M=D, N=D, K=S, out_dtypes=(BF16,), tj=1024, tk=1024)

    def merge_bwd(gab, gsb, yab, ysb, dmb):
        _, vjp = jax.vjp(_merge, gab, gsb, yab, ysb)
        return vjp(dmb)

    d_ga, d_gs, dy_attn, dy_ssm = _rowwise(
        merge_bwd, [(_Op(proj, coff=o_ga), "tile", D), (_Op(proj, coff=o_gs), "tile", D), (y_attn, "tile", D),
                    (y_ssm, "tile", D), (dmerged, "tile", D)],
        [(D, BF16), (D, BF16), (D, BF16), (D, BF16)], [], name="merge_bwd", rows=S)

    dattn = _mm(dy_attn, wop("w_attn_proj"), "NT", name="attn_proj_dx", M=S, N=ATTN_WIDTH, K=D, tj=1024, out_dtypes=(BF16,))
    g_w_attn_proj = _mm(attn, dy_attn, "TN", name="attn_proj_dw", M=ATTN_WIDTH, N=D, K=S, out_dtypes=(BF16,),
                        out_nsh=N_CHIPS, tk=1024)

    dz = _mm(dy_ssm, wop("w_ssm_proj"), "NT", name="ssm_proj_dx", M=S, N=SSM_W, K=D)
    g_w_ssm_proj = _mm(z, dy_ssm, "TN", name="ssm_proj_dw", M=SSM_W, N=D, K=S, out_dtypes=(BF16,),
                       out_nsh=N_CHIPS, tk=1024)

    def glu_bwd(dzb, yb, tb):
        z0 = _gelu(yb)
        sg = _sigmoid(tb)
        dt = dzb * z0 * sg * (1.0 - sg)
        return dt, dzb * sg, jnp.sum(dt, axis=0, keepdims=True)

    dt_b, dz0a, g_b_glu = _rowwise(glu_bwd, [(dz, "tile", SSM_W), (y, "tile", SSM_W), (t_glu, "tile", SSM_W)],
                                   [(SSM_W, BF16), (SSM_W, F32)], [SSM_W], name="glu_bwd", rows=S)

    def gelu_bwd(acc, dz0ab, yb):
        _, vjp = jax.vjp(_gelu, yb)
        return (vjp(acc + dz0ab)[0],)

    dy = _mm(dt_b, wop("w_glu"), "NT", name="glu_dx", M=S, N=SSM_W, K=SSM_W, epilogue=gelu_bwd,
             extras=[(dz0a, "tile"), (y, "tile")])
    g_w_glu = _mm(y, dt_b, "TN", name="glu_dw", M=SSM_W, N=SSM_W, K=S, out_dtypes=(BF16,), tk=1024, a_fn=_gelu)
    rs_mix, token = rs_swap("mix", dict(w_out=g_w_out, w_attn_proj=g_w_attn_proj, w_ssm_proj=g_w_ssm_proj,
                                        w_glu=g_w_glu))
    dy_il = _interleave(dy, SCAN_CHUNKS)
    du_il, g_bd, g_cd, d_abar, g_ssm_d = _ssm_bwd(dy_il, u_il, xs, bd, cd, a_bwd, d_row, name="ssm_bwd", sb=SB, sbn=SBN,
                                                  deps=[token])
    du = _deinterleave(du_il, SCAN_CHUNKS)
    rs_mix, token_mix = rs_scatter("mix", rs_mix, du_il)

    dqh, dkh, dvh, dsink_blk, dbias = _attn_bwd(qh, kh, vh, heads(dattn, N_Q_HEADS), tied(sinks_b, token_mix), bias,
                                                "attn_bwd")
    g_sinks = _sum_lead(dsink_blk.reshape(N_Q_HEADS, BLOCK, 128).transpose(1, 0, 2), "sinks_dw")[:, 0].reshape(1, N_Q_HEADS)
    g_rel = _mm(dbias.reshape(N_Q_HEADS, -1), onehot_t, "NT", name="rel_bias_dw", M=N_Q_HEADS, N=128,
                K=BLOCK * 2 * BLOCK, tk=4096)
    g_rel_bias = g_rel[:, :NUM_BUCKETS].T

    eye_b = jnp.eye(gpb, dtype=F32)
    g_cd6 = g_cd.reshape(2, nsb, gpb, SSM_STATE, gpb, SSM_GROUP_CH)
    g_c_re = jnp.einsum("bgnhp,gh->bgpn", g_cd6[0], eye_b).reshape(G, SSM_GROUP_CH, SSM_STATE)
    g_c_im = -jnp.einsum("bgnhp,gh->bgpn", g_cd6[1], eye_b).reshape(G, SSM_GROUP_CH, SSM_STATE)
    g_bd6 = g_bd.reshape(nsb, gpb, SSM_GROUP_CH, 2, gpb, SSM_STATE)
    g_bbar = jnp.einsum("bhprgn,hg->rpbhn", g_bd6, eye_b).reshape(2, SSM_GROUP_CH, NST)
    g_lre, g_lim, g_lstep, g_bre, g_bim = _ssm_params(disc_in, (d_abar[0], d_abar[1], g_bbar[0], g_bbar[1]),
                                                      "ssm_params_bwd")
    g_lre, g_lim = g_lre.reshape(G, SSM_STATE), g_lim.reshape(G, SSM_STATE)
    g_lstep = g_lstep.reshape(G, SSM_STATE).sum(axis=1)
    g_bre = g_bre.reshape(SSM_GROUP_CH, G, SSM_STATE).transpose(1, 2, 0)
    g_bim = g_bim.reshape(SSM_GROUP_CH, G, SSM_STATE).transpose(1, 2, 0)

    dproj = jnp.concatenate([unheads(dqh).astype(BF16), unheads(dkh).astype(BF16), unheads(dvh).astype(BF16),
                             du.astype(BF16), d_ga, d_gs], axis=1)
    g_w_in = _mm(h1_t, dproj, "NN", name="proj_dw", M=D, N=INW, K=S, out_dtypes=(BF16,), out_nsh=N_CHIPS,
                 tj=INW // (2 * N_CHIPS), tk=1024)
    rs_in, token = rs_swap("in", dict(w_in=g_w_in))
    dh1 = _mm(dproj, wop("w_in"), "NT", name="proj_dx", M=S, N=D, K=INW, tj=1024, tk=INW // N_CHIPS, deps=[token])
    g_b_in = _rowwise(lambda d: (jnp.sum(d.astype(F32), axis=0, keepdims=True),), [(dproj, "tile", INW)], [], [INW],
                      name="proj_db", rows=S)[0]

    def norm1_bwd(xb, dhb, dresb, gn, shb, scb):
        _, vjp = jax.vjp(_norm_mod, xb, gn, shb, scb)
        dx, dg, dsh, dsc = vjp(dhb)
        return dx + dresb, dg, dsh, dsc

    grad_x, g_norm1, d_sh1, d_sc1 = _rowwise(
        norm1_bwd, [(xv, "tile", D), (dh1, "tile", D), (dx2, "tile", D), (norm1_g, "row", D),
                    (sh1, "row", D),
                    (sc1, "row", D)], [(D, F32)], [D, D, D], name="norm1_bwd", rows=S)

    dmod_row = jnp.concatenate([d_sh1, d_sc1, d_g1, d_sh2, d_sc2, d_g2], axis=1)
    small_g = dict(norm1_g=g_norm1, b_in=g_b_in, attn_sinks=g_sinks, rel_bias=g_rel_bias, lambda_re=g_lre[None],
                   lambda_im=g_lim[None], log_step=g_lstep[None], ssm_b_re=g_bre[None], ssm_b_im=g_bim[None],
                   ssm_c_re=g_c_re[None], ssm_c_im=g_c_im[None], ssm_d=g_ssm_d, b_glu=g_b_glu, norm2_g=g_norm2,
                   final_g=g_final.reshape(D))
    packed = _pack([dmod_row, loss_acc[:, :1]] + [small_g[k] for k in _SMALL])
    rows = packed.shape[0]
    gathered = _allgather8(packed, "gather_small").reshape(N_DEV, rows, 128)
    summed = _sum_lead(gathered, "small_sum")
    parts = _unpack(summed, [dmod_row.shape, (1,)] + [given[k].shape for k in _SMALL])
    loss = parts[1].reshape(())
    grads.update(zip(_SMALL, parts[2:]))
    grads["b_ada"] = parts[0]

    dmod_all = gathered[:, :dmod_row.shape[1] // 128].reshape(N_DEV, -1)
    dmod_mine = lax.dynamic_slice(dmod_all.reshape(N_DEV, N_CHIPS, -1), (0, my_chip, 0), (N_DEV, 1, w_ada.shape[2]))[:, 0]
    g_w_ada = _mm(c16, jnp.pad(dmod_mine, ((0, 8), (0, 0))), "TN", name="ada_dw", M=D, N=w_ada.shape[2], K=16,
                  a_fn=_silu)
    grads["w_ada"] = g_w_ada[None]

    deltas, new_m, new_v = {}, {}, {}

    def adamw_big(k, deps=()):
        echo = k in big_names
        res = _adamw(given[k][0], grads[k][0], given["m_" + k][0], given["v_" + k][0], "adamw_" + k, deps, echo)
        deltas[k], new_m[k], new_v[k] = res[0][None], res[1][None], res[2][None]
        if echo:
            grads[k] = res[3][None]
        return res[2]

    rs_in, token_in = rs_scatter("in", rs_in, all_of(summed, dmod_all))
    rs_ff, token = rs_sum("ff", rs_ff, all_of(summed, token_in))
    mark = adamw_big("w_ada", [token])
    rs_mix, token = rs_sum("mix", rs_mix, mark)
    small_all = list(_SMALL) + ["b_ada"]
    for k in small_all:
        grads[k] = grads[k].reshape(given[k].shape)

    def rows_of(a):
        return a.reshape(1, -1) if a.ndim == 1 else a

    d_, m_, v_ = _adamw_many(*[[rows_of(src[k]) for k in small_all] for src in (
        given, grads, {k: given["m_" + k] for k in small_all}, {k: given["v_" + k] for k in small_all})],
        "adamw_small", [token])
    for k, dd, mm, vv in zip(small_all, d_, m_, v_):
        deltas[k], new_m[k], new_v[k] = (t.reshape(given[k].shape) for t in (dd, mm, vv))
    v_ = v_[0]
    rs_finish("ff", rs_ff, v_)
    marks = [adamw_big(k) for k in ("w_ff2", "w_ff1")]
    rs_finish("mix", rs_mix, all_of(*marks))
    marks = [adamw_big(k) for k in ("w_out", "w_attn_proj", "w_ssm_proj", "w_glu")]
    rs_in, token = rs_sum("in", rs_in, all_of(*marks))
    rs_finish("in", rs_in, token)
    adamw_big("w_in")

    names = ["w_ada", "b_ada", "norm1_g", "w_in", "b_in", "attn_sinks", "rel_bias", "lambda_re", "lambda_im",
             "log_step", "ssm_b_re", "ssm_b_im", "ssm_c_re", "ssm_c_im", "ssm_d", "w_glu", "b_glu", "w_attn_proj",
             "w_ssm_proj", "w_out", "norm2_g", "w_ff1", "w_ff2", "final_g"]
    return (loss, grad_x[None], *[grads[n] for n in names], *[deltas[n] for n in names],
            *[new_m[n] for n in names], *[new_v[n] for n in names])
```

```python
import math

import numpy as np
import jax
import jax.numpy as jnp
from jax import lax
from jax.experimental import pallas as pl
from jax.experimental.pallas import tpu as pltpu

F32 = jnp.float32
BF16 = jnp.bfloat16
MESH = pl.DeviceIdType.MESH

HEAD_DIM = 64
N_Q_HEADS = 16
N_KV_HEADS = 4
GQA_GROUP = N_Q_HEADS // N_KV_HEADS
ATTN_WIDTH = N_Q_HEADS * HEAD_DIM
KV_WIDTH = N_KV_HEADS * HEAD_DIM
BLOCK = 128
NUM_BUCKETS = 32
MAX_DISTANCE = 128
NEG_INF = -1e30
SSM_GROUP_CH = 16
SSM_STATE = 64
EPS = 1e-6
ADAM_LR = 0.001
ADAM_B1 = 0.9
ADAM_B2 = 0.999
ADAM_EPS = 1e-08
ADAM_WD = 0.01
ADAM_STEP = 10

N_CHIPS = 4
N_DEV = 8
SCAN_CHUNKS = 8
VMEM_LIMIT_BYTES = 48 * 1024 * 1024
SSM_VMEM_LIMIT_BYTES = 56 * 1024 * 1024


def _cparams(sem=None):
    return pltpu.CompilerParams(dimension_semantics=sem, vmem_limit_bytes=VMEM_LIMIT_BYTES)


class _Op:
    def __init__(self, arr, nsh=None, coff=0):
        self.arr, self.nsh, self.coff = arr, nsh, coff
        if nsh is None:
            self.rows, self.cols = arr.shape
        else:
            assert arr.shape[0] == nsh
            self.rows, self.cols = arr.shape[1], arr.shape[2] * nsh

    def spec(self, br, bc, idx):
        assert self.coff % bc == 0
        off = self.coff // bc
        if self.nsh is None:
            return pl.BlockSpec((br, bc), lambda *g: (idx(*g)[0], idx(*g)[1] + off))
        per = (self.cols // self.nsh) // bc
        assert per * bc * self.nsh == self.cols

        def imap(*g):
            r, c = idx(*g)
            c = c + off
            return (c // per, r, c % per)
        return pl.BlockSpec((None, br, bc), imap)


def _as_op(a):
    return a if isinstance(a, _Op) else _Op(a)


def _mm(a, b, mode, *, name, M, N, K, out_dtypes=(F32,), out_nsh=None, epilogue=None, extras=(),
        a_fn=None, ti=1024, tj=512, tk=2048, deps=()):
    nd = len(deps)
    a, b = _as_op(a), _as_op(b)
    ti, tj, tk = min(ti, M), min(tj, N), min(tk, K)
    a_w = a.cols // a.nsh if a.nsh else None
    b_w = b.cols // b.nsh if b.nsh else None
    if a_w:
        ti, tk = (min(ti, a_w), tk) if mode == "TN" else (ti, min(tk, a_w))
    if b_w:
        tj, tk = (tj, min(tk, b_w)) if mode == "NT" else (min(tj, b_w), tk)
    if out_nsh:
        tj = min(tj, N // out_nsh)
    assert M % ti == 0 and N % tj == 0 and K % tk == 0, (name, M, N, K, ti, tj, tk)
    nk = K // tk
    if mode == "NN":
        a_spec = a.spec(ti, tk, lambda i, j, k: (i, k))
        b_spec = b.spec(tk, tj, lambda i, j, k: (k, j))
        dims = (((1,), (0,)), ((), ()))
    elif mode == "NT":
        a_spec = a.spec(ti, tk, lambda i, j, k: (i, k))
        b_spec = b.spec(tj, tk, lambda i, j, k: (j, k))
        dims = (((1,), (1,)), ((), ()))
    else:
        a_spec = a.spec(tk, ti, lambda i, j, k: (k, i))
        b_spec = b.spec(tk, tj, lambda i, j, k: (k, j))
        dims = (((0,), (0,)), ((), ()))
    ex_specs, ex_arrs = [], []
    for op, kind in extras:
        op = _as_op(op)
        if kind == "tile":
            ex_specs.append(op.spec(ti, tj, lambda i, j, k: (i, j)))
        else:
            ex_specs.append(op.spec(1, tj, lambda i, j, k: (0, j)))
        ex_arrs.append(op.arr)
    ne, no = len(ex_arrs), len(out_dtypes)
    if out_nsh is None:
        out_shapes = [jax.ShapeDtypeStruct((M, N), d) for d in out_dtypes]
        out_specs = [pl.BlockSpec((ti, tj), lambda i, j, k: (i, j)) for _ in out_dtypes]
    else:
        per = (N // out_nsh) // tj
        assert per * tj * out_nsh == N
        out_shapes = [jax.ShapeDtypeStruct((out_nsh, M, N // out_nsh), d) for d in out_dtypes]
        out_specs = [pl.BlockSpec((None, ti, tj), lambda i, j, k: (j // per, i, j % per)) for _ in out_dtypes]

    def body(a_ref, b_ref, *rest):
        ex_refs, out_refs, acc = rest[:ne], rest[ne + nd:ne + nd + no], rest[ne + nd + no]
        k = pl.program_id(2)

        @pl.when(k == 0)
        def _():
            acc[...] = jnp.zeros_like(acc)

        av = a_ref[...]
        if a_fn is not None:
            av = a_fn(av)
        acc[...] += lax.dot_general(av.astype(BF16), b_ref[...].astype(BF16), dims,
                                    preferred_element_type=F32)

        @pl.when(k == nk - 1)
        def _():
            res = acc[...]
            outs = epilogue(res, *[r[...] for r in ex_refs]) if epilogue is not None else (res,)
            for o_ref, o in zip(out_refs, outs):
                o_ref[...] = o.astype(o_ref.dtype)

    outs = pl.pallas_call(
        body, name=name, grid=(M // ti, N // tj, nk),
        in_specs=[a_spec, b_spec] + ex_specs + [pl.BlockSpec(memory_space=pl.ANY)] * nd,
        out_specs=out_specs, out_shape=out_shapes,
        scratch_shapes=[pltpu.VMEM((ti, tj), F32)],
        compiler_params=_cparams(("parallel", "parallel", "arbitrary")),
    )(a.arr, b.arr, *ex_arrs, *deps)
    return outs[0] if no == 1 else outs


def _rowwise(fn, ins, outs, accs, *, name, rows, tr=256, deps=()):
    tr = min(tr, rows)
    assert rows % tr == 0
    in_specs, arrs = [], []
    for op, kind, width in ins:
        op = _as_op(op)
        if kind == "tile":
            in_specs.append(op.spec(tr, width, lambda i: (i, 0)))
        else:
            in_specs.append(op.spec(op.rows, width, lambda i: (0, 0)))
        arrs.append(op.arr)
    ni, no, na = len(ins), len(outs), len(accs)
    flipped = [len(o) == 3 for o in outs]
    out_shapes = [jax.ShapeDtypeStruct((o[0], rows) if t else (rows, o[0]), o[1]) for o, t in zip(outs, flipped)]
    out_specs = [pl.BlockSpec((o[0], tr), lambda i: (0, i)) if t else pl.BlockSpec((tr, o[0]), lambda i: (i, 0))
                 for o, t in zip(outs, flipped)]
    out_shapes += [jax.ShapeDtypeStruct((1, w), F32) for w in accs]
    out_specs += [pl.BlockSpec((1, w), lambda i: (0, 0)) for w in accs]

    def body(*refs):
        nd = len(deps)
        in_refs, out_refs, acc_refs = refs[:ni], refs[ni + nd:ni + nd + no], refs[ni + nd + no:]
        res = fn(*[r[...] for r in in_refs])
        if not isinstance(res, (tuple, list)):
            res = (res,)
        for o_ref, r, t in zip(out_refs, res[:no], flipped):
            o_ref[...] = (r.astype(F32).T if t else r).astype(o_ref.dtype)
        if na:
            @pl.when(pl.program_id(0) == 0)
            def _():
                for a_ref in acc_refs:
                    a_ref[...] = jnp.zeros_like(a_ref)
            for a_ref, r in zip(acc_refs, res[no:]):
                a_ref[...] += r.astype(F32)

    res = pl.pallas_call(
        body, name=name, grid=(rows // tr,), in_specs=in_specs + [pl.BlockSpec(memory_space=pl.ANY)] * len(deps),
        out_specs=out_specs, out_shape=out_shapes, compiler_params=_cparams(("arbitrary",)),
    )(*arrs, *deps)
    return res


def _norm_mod(x, g, sh, sc):
    y = x * lax.rsqrt(jnp.mean(x * x, axis=-1, keepdims=True) + EPS) * g
    return y * (1.0 + sc) + sh


def _sigmoid(x):
    return 1.0 / (1.0 + jnp.exp(-x))


def _silu(x):
    return x * _sigmoid(x)


def _gelu(x):
    return 0.5 * x * (1.0 + jnp.tanh(math.sqrt(2.0 / math.pi) * (x + 0.044715 * (x * x * x))))


def _merge(ga, gs, ya, ys):
    ga, gs, ya, ys = (v.astype(F32) for v in (ga, gs, ya, ys))
    return _sigmoid(ga) * ya + _sigmoid(gs) * ys


def _attn_head(q, kp, kc, vp, vc, sink, bias_p, bias_c, not_first):
    nt = (((1,), (1,)), ((), ()))
    nn = (((1,), (0,)), ((), ()))
    qb = q.astype(BF16)
    scale = HEAD_DIM ** -0.5
    sp = lax.dot_general(qb, kp.astype(BF16), nt, preferred_element_type=F32) * scale + bias_p
    sc = lax.dot_general(qb, kc.astype(BF16), nt, preferred_element_type=F32) * scale + bias_c
    qi = lax.broadcasted_iota(jnp.int32, sp.shape, 0) & (BLOCK - 1)
    ki = lax.broadcasted_iota(jnp.int32, sp.shape, 1)
    sp = jnp.where(jnp.logical_and(ki > qi, not_first), sp, NEG_INF)
    sc = jnp.where(ki <= qi, sc, NEG_INF)
    m = jnp.maximum(jnp.maximum(jnp.max(sp, axis=-1, keepdims=True), jnp.max(sc, axis=-1, keepdims=True)), sink)
    m = lax.stop_gradient(m)
    pp = jnp.exp(sp - m)
    pc = jnp.exp(sc - m)
    denom = jnp.sum(pp, axis=-1, keepdims=True) + jnp.sum(pc, axis=-1, keepdims=True) + jnp.exp(sink - m)
    o = lax.dot_general((pp / denom).astype(BF16), vp.astype(BF16), nn, preferred_element_type=F32)
    o = o + lax.dot_general((pc / denom).astype(BF16), vc.astype(BF16), nn, preferred_element_type=F32)
    return o


def _attn_fwd(qh, kh, vh, sinks, bias, name):
    s = qh.shape[1]
    nb = s // BLOCK
    G = GQA_GROUP
    R = G * BLOCK

    def body(q_ref, kp_ref, kc_ref, vp_ref, vc_ref, sink_ref, bias_ref, o_ref):
        not_first = pl.program_id(0) > 0
        for kv in range(N_KV_HEADS):
            hs = slice(kv * G, (kv + 1) * G)
            o = _attn_head(q_ref[hs].reshape(R, HEAD_DIM), kp_ref[kv], kc_ref[kv], vp_ref[kv], vc_ref[kv],
                           sink_ref[kv * R:(kv + 1) * R, 0:1],
                           bias_ref[hs, :, 0:BLOCK].reshape(R, BLOCK), bias_ref[hs, :, BLOCK:2 * BLOCK].reshape(R, BLOCK),
                           not_first)
            o_ref[hs] = o.reshape(G, BLOCK, HEAD_DIM).astype(o_ref.dtype)

    cur = lambda i: (0, i, 0)
    prev = lambda i: (0, jnp.maximum(i - 1, 0), 0)
    return pl.pallas_call(
        body, name=name, grid=(nb,),
        in_specs=[pl.BlockSpec((N_Q_HEADS, BLOCK, HEAD_DIM), cur),
                  pl.BlockSpec((N_KV_HEADS, BLOCK, HEAD_DIM), prev), pl.BlockSpec((N_KV_HEADS, BLOCK, HEAD_DIM), cur),
                  pl.BlockSpec((N_KV_HEADS, BLOCK, HEAD_DIM), prev), pl.BlockSpec((N_KV_HEADS, BLOCK, HEAD_DIM), cur),
                  pl.BlockSpec((N_Q_HEADS * BLOCK, 128), lambda i: (0, 0)),
                  pl.BlockSpec((N_Q_HEADS, BLOCK, 2 * BLOCK), lambda i: (0, 0, 0))],
        out_specs=pl.BlockSpec((N_Q_HEADS, BLOCK, HEAD_DIM), cur),
        out_shape=jax.ShapeDtypeStruct((N_Q_HEADS, s, HEAD_DIM), BF16),
        compiler_params=_cparams(("arbitrary",)),
    )(qh, kh, kh, vh, vh, sinks, bias)


def _attn_bwd(qh, kh, vh, doh, sinks, bias, name):
    s = qh.shape[1]
    nb = s // BLOCK
    G = GQA_GROUP
    R = G * BLOCK

    def body(q_ref, kp_ref, kc_ref, vp_ref, vc_ref, do_ref, sink_ref, bias_ref,
             dq_ref, dk_ref, dv_ref, dsink_ref, dbias_ref, ck, cv):
        i = pl.program_id(1)

        @pl.when(i == 0)
        def _():
            dsink_ref[...] = jnp.zeros_like(dsink_ref)
            dbias_ref[...] = jnp.zeros_like(dbias_ref)
            ck[...] = jnp.zeros_like(ck)
            cv[...] = jnp.zeros_like(cv)

        @pl.when(i < nb)
        def _():
            not_first = i > 0
            _, vjp = jax.vjp(lambda q, a, b, c, d, sk, e, f: _attn_head(q, a, b, c, d, sk, e, f, not_first),
                             q_ref[...].astype(F32).reshape(R, HEAD_DIM), kp_ref[...].astype(F32),
                             kc_ref[...].astype(F32), vp_ref[...].astype(F32), vc_ref[...].astype(F32),
                             sink_ref[:, 0:1], bias_ref[:, :, 0:BLOCK].reshape(R, BLOCK),
                             bias_ref[:, :, BLOCK:2 * BLOCK].reshape(R, BLOCK))
            dq, dkp, dkc, dvp, dvc, dsk, dbp, dbc = vjp(do_ref[...].reshape(R, HEAD_DIM).astype(F32))
            dq_ref[...] = dq.reshape(G, BLOCK, HEAD_DIM).astype(dq_ref.dtype)
            dsink_ref[...] += jnp.broadcast_to(dsk, (R, 128))
            dbias_ref[:, :, 0:BLOCK] += dbp.reshape(G, BLOCK, BLOCK)
            dbias_ref[:, :, BLOCK:2 * BLOCK] += dbc.reshape(G, BLOCK, BLOCK)
            dk_ref[...] = (ck[...] + dkp).astype(dk_ref.dtype)
            dv_ref[...] = (cv[...] + dvp).astype(dv_ref.dtype)
            ck[...] = dkc
            cv[...] = dvc

        @pl.when(i == nb)
        def _():
            dk_ref[...] = ck[...].astype(dk_ref.dtype)
            dv_ref[...] = cv[...].astype(dv_ref.dtype)

    qcur = lambda kv, i: (kv, jnp.minimum(i, nb - 1), 0)
    kcur = lambda kv, i: (kv, jnp.minimum(i, nb - 1), 0)
    kprev = lambda kv, i: (kv, jnp.clip(i - 1, 0, nb - 1), 0)
    qspec = pl.BlockSpec((G, BLOCK, HEAD_DIM), qcur)
    kc_spec = pl.BlockSpec((None, BLOCK, HEAD_DIM), kcur)
    kp_spec = pl.BlockSpec((None, BLOCK, HEAD_DIM), kprev)
    return pl.pallas_call(
        body, name=name, grid=(N_KV_HEADS, nb + 1),
        in_specs=[qspec, kp_spec, kc_spec, kp_spec, kc_spec, qspec,
                  pl.BlockSpec((R, 128), lambda kv, i: (kv, 0)),
                  pl.BlockSpec((G, BLOCK, 2 * BLOCK), lambda kv, i: (kv, 0, 0))],
        out_specs=[qspec, kp_spec, kp_spec,
                   pl.BlockSpec((R, 128), lambda kv, i: (kv, 0)),
                   pl.BlockSpec((G, BLOCK, 2 * BLOCK), lambda kv, i: (kv, 0, 0))],
        out_shape=[jax.ShapeDtypeStruct((N_Q_HEADS, s, HEAD_DIM), BF16),
                   jax.ShapeDtypeStruct((N_KV_HEADS, s, HEAD_DIM), BF16),
                   jax.ShapeDtypeStruct((N_KV_HEADS, s, HEAD_DIM), BF16),
                   jax.ShapeDtypeStruct((N_Q_HEADS * BLOCK, 128), F32),
                   jax.ShapeDtypeStruct((N_Q_HEADS, BLOCK, 2 * BLOCK), F32)],
        scratch_shapes=[pltpu.VMEM((BLOCK, HEAD_DIM), F32), pltpu.VMEM((BLOCK, HEAD_DIM), F32)],
        compiler_params=_cparams(("arbitrary", "arbitrary")),
    )(qh, kh, kh, vh, vh, doh, sinks, bias)


def _cmul(ar, ai, br, bi):
    return ar * br - ai * bi, ar * bi + ai * br


def _scan_passes(a_ref, b_ref, x_ref, xp_ref, da_ref, *, s, tc, reverse):
    nc = SCAN_CHUNKS
    steps = s // nc
    with_da = xp_ref is not None
    unroll = 8 if steps % 8 == 0 else 1

    def shift(v, d):
        row = lax.broadcasted_iota(jnp.int32, v.shape, 0)
        if reverse:
            return jnp.where(row < nc - d, pltpu.roll(v, nc - d, 0), 0.0)
        return jnp.where(row >= d, pltpu.roll(v, d, 0), 0.0)

    def run():
        ar = jnp.broadcast_to(a_ref[0], (nc, tc))
        ai = jnp.broadcast_to(a_ref[1], (nc, tc))

        def row_of(step):
            j = (steps - 1 - step) if reverse else step
            return pl.multiple_of(j * nc, nc)

        def p1(step, st):
            sr, si = st
            r0 = row_of(step)
            mr, mi = _cmul(ar, ai, sr, si)
            sr = mr + b_ref[0, pl.ds(r0, nc), :]
            si = mi + b_ref[1, pl.ds(r0, nc), :]
            x_ref[0, pl.ds(r0, nc), :] = sr
            x_ref[1, pl.ds(r0, nc), :] = si
            return sr, si
        zero = jnp.zeros((nc, tc), F32)
        er, ei = lax.fori_loop(0, steps, p1, (zero, zero), unroll=unroll)

        pr, pi_ = jnp.ones((nc, tc), F32), zero
        br, bi, left = ar, ai, steps
        while left:
            if left & 1:
                pr, pi_ = _cmul(pr, pi_, br, bi)
            br, bi = _cmul(br, bi, br, bi)
            left >>= 1
        cr, ci = shift(er, 1), shift(ei, 1)
        d = 1
        while d < nc:
            mr, mi = _cmul(pr, pi_, shift(cr, d), shift(ci, d))
            cr, ci = cr + mr, ci + mi
            pr, pi_ = _cmul(pr, pi_, pr, pi_)
            d *= 2

        def p2(step, st):
            qr, qi, dar, dai = st
            r0 = row_of(step)
            qr, qi = _cmul(ar, ai, qr, qi)
            fr, fi = _cmul(qr, qi, cr, ci)
            xr = x_ref[0, pl.ds(r0, nc), :] + fr
            xi = x_ref[1, pl.ds(r0, nc), :] + fi
            x_ref[0, pl.ds(r0, nc), :] = xr
            x_ref[1, pl.ds(r0, nc), :] = xi
            if with_da:
                jm = jnp.where(step == steps - 1, steps - 1, steps - 2 - step)
                rp = pl.multiple_of(jm * nc, nc)
                vr, vi = xp_ref[0, pl.ds(rp, nc), :], xp_ref[1, pl.ds(rp, nc), :]
                row = lax.broadcasted_iota(jnp.int32, (nc, tc), 0)
                first = step == steps - 1
                sel = jnp.logical_and(first, row == 0)
                vr = jnp.where(sel, 0.0, jnp.where(first, pltpu.roll(vr, 1, 0), vr))
                vi = jnp.where(sel, 0.0, jnp.where(first, pltpu.roll(vi, 1, 0), vi))
                dar = dar + xr * vr + xi * vi
                dai = dai + xi * vr - xr * vi
            return qr, qi, dar, dai
        _, _, dar, dai = lax.fori_loop(0, steps, p2, (jnp.ones((nc, tc), F32), zero, zero, zero), unroll=unroll)
        if with_da:
            da_ref[0] = jnp.sum(dar, axis=0, keepdims=True)
            da_ref[1] = jnp.sum(dai, axis=0, keepdims=True)

    run()


def _ssm_fwd(u, bd, cd, a, d_row, *, name, sb, sbn):
    s, w = u.shape
    nst = a.shape[2]
    nblk = w // sb
    rows = min(512, s)
    nn = (((1,), (0,)), ((), ()))

    def body(u_ref, bre_ref, bim_ref, cre_ref, cim_ref, a_ref, d_ref, y_ref, x_ref):

        def fill(r, carry):
            r0 = pl.multiple_of(r * rows, rows)
            ub = u_ref[pl.ds(r0, rows), :].astype(BF16)
            x_ref[0, pl.ds(r0, rows), :] = lax.dot_general(ub, bre_ref[...].astype(BF16), nn, preferred_element_type=F32)
            x_ref[1, pl.ds(r0, rows), :] = lax.dot_general(ub, bim_ref[...].astype(BF16), nn, preferred_element_type=F32)
            return carry
        lax.fori_loop(0, s // rows, fill, 0)
        _scan_passes(a_ref, x_ref, x_ref, None, None, s=s, tc=sbn, reverse=False)

        def project(r, carry):
            r0 = pl.multiple_of(r * rows, rows)
            y = lax.dot_general(x_ref[0, pl.ds(r0, rows), :].astype(BF16), cre_ref[...].astype(BF16), nn, preferred_element_type=F32)
            y = y + lax.dot_general(x_ref[1, pl.ds(r0, rows), :].astype(BF16), cim_ref[...].astype(BF16), nn, preferred_element_type=F32)
            y_ref[pl.ds(r0, rows), :] = y + d_ref[...] * u_ref[pl.ds(r0, rows), :]
            return carry
        lax.fori_loop(0, s // rows, project, 0)

    return pl.pallas_call(
        body, name=name, grid=(nblk,),
        in_specs=[pl.BlockSpec((s, sb), lambda j: (0, j)),
                  pl.BlockSpec((sb, sbn), lambda j: (j, j)), pl.BlockSpec((sb, sbn), lambda j: (j, nblk + j)),
                  pl.BlockSpec((sbn, sb), lambda j: (j, j)), pl.BlockSpec((sbn, sb), lambda j: (nblk + j, j)),
                  pl.BlockSpec((2, 1, sbn), lambda j: (0, 0, j)), pl.BlockSpec((1, sb), lambda j: (0, j))],
        out_specs=[pl.BlockSpec((s, sb), lambda j: (0, j)), pl.BlockSpec((2, s, sbn), lambda j: (0, 0, j))],
        out_shape=[jax.ShapeDtypeStruct((s, w), F32), jax.ShapeDtypeStruct((2, s, nst), F32)],
        compiler_params=pltpu.CompilerParams(dimension_semantics=("arbitrary",), vmem_limit_bytes=SSM_VMEM_LIMIT_BYTES),
    )(u, bd, bd, cd, cd, a, d_row)


def _ssm_bwd(dy, u, xs, bd, cd, a, d_row, *, name, sb, sbn, deps=()):
    s, w = u.shape
    nst = a.shape[2]
    nblk = w // sb
    rows = min(512, s)
    nt = (((1,), (1,)), ((), ()))
    tn = (((0,), (0,)), ((), ()))

    def body(dy_ref, u_ref, xs_hbm, bre_ref, bim_ref, cre_ref, cim_ref, a_ref, d_ref, *rest):
        du_ref, gb_ref, gc_ref, da_ref, gd_ref, lam, xs_ref, sem = rest[len(deps):]
        j = pl.program_id(0)
        fetch = pltpu.make_async_copy(xs_hbm.at[:, :, pl.ds(pl.multiple_of(j * sbn, sbn), sbn)], xs_ref, sem)
        fetch.start()

        def fill(r, carry):
            r0 = pl.multiple_of(r * rows, rows)
            dyb = dy_ref[pl.ds(r0, rows), :].astype(BF16)
            lam[0, pl.ds(r0, rows), :] = lax.dot_general(dyb, cre_ref[...].astype(BF16), nt, preferred_element_type=F32)
            lam[1, pl.ds(r0, rows), :] = lax.dot_general(dyb, cim_ref[...].astype(BF16), nt, preferred_element_type=F32)
            return carry
        lax.fori_loop(0, s // rows, fill, 0)
        fetch.wait()
        _scan_passes(a_ref, lam, lam, xs_ref, da_ref, s=s, tc=sbn, reverse=True)
        gb_ref[...] = jnp.zeros_like(gb_ref)
        gc_ref[...] = jnp.zeros_like(gc_ref)
        gd_ref[...] = jnp.zeros_like(gd_ref)

        def project(r, carry):
            r0 = pl.multiple_of(r * rows, rows)
            dyv, uv = dy_ref[pl.ds(r0, rows), :], u_ref[pl.ds(r0, rows), :]
            dyb, ub = dyv.astype(BF16), uv.astype(BF16)
            lr, li = lam[0, pl.ds(r0, rows), :].astype(BF16), lam[1, pl.ds(r0, rows), :].astype(BF16)
            du = lax.dot_general(lr, bre_ref[...].astype(BF16), nt, preferred_element_type=F32)
            du = du + lax.dot_general(li, bim_ref[...].astype(BF16), nt, preferred_element_type=F32)
            du_ref[pl.ds(r0, rows), :] = du + d_ref[...] * dyv
            gb_ref[:, 0:sbn] += lax.dot_general(ub, lr, tn, preferred_element_type=F32)
            gb_ref[:, sbn:2 * sbn] += lax.dot_general(ub, li, tn, preferred_element_type=F32)
            gc_ref[0] += lax.dot_general(xs_ref[0, pl.ds(r0, rows), :].astype(BF16), dyb, tn, preferred_element_type=F32)
            gc_ref[1] += lax.dot_general(xs_ref[1, pl.ds(r0, rows), :].astype(BF16), dyb, tn, preferred_element_type=F32)
            gd_ref[...] += jnp.sum(dyv * uv, axis=0, keepdims=True)
            return carry
        lax.fori_loop(0, s // rows, project, 0)

    col = lambda j: (0, j)
    return pl.pallas_call(
        body, name=name, grid=(nblk,),
        in_specs=[pl.BlockSpec((s, sb), col), pl.BlockSpec((s, sb), col), pl.BlockSpec(memory_space=pl.ANY),
                  pl.BlockSpec((sb, sbn), lambda j: (j, j)), pl.BlockSpec((sb, sbn), lambda j: (j, nblk + j)),
                  pl.BlockSpec((sbn, sb), lambda j: (j, j)), pl.BlockSpec((sbn, sb), lambda j: (nblk + j, j)),
                  pl.BlockSpec((2, 1, sbn), lambda j: (0, 0, j)), pl.BlockSpec((1, sb), col)]
        + [pl.BlockSpec(memory_space=pl.ANY)] * len(deps),
        out_specs=[pl.BlockSpec((s, sb), col), pl.BlockSpec((sb, 2 * sbn), lambda j: (j, 0)),
                   pl.BlockSpec((2, sbn, sb), lambda j: (0, j, 0)), pl.BlockSpec((2, 1, sbn), lambda j: (0, 0, j)),
                   pl.BlockSpec((1, sb), col)],
        out_shape=[jax.ShapeDtypeStruct((s, w), F32), jax.ShapeDtypeStruct((w, 2 * sbn), F32),
                   jax.ShapeDtypeStruct((2, nst, sb), F32), jax.ShapeDtypeStruct((2, 1, nst), F32),
                   jax.ShapeDtypeStruct((1, w), F32)],
        scratch_shapes=[pltpu.VMEM((2, s, sbn), F32), pltpu.VMEM((2, s, sbn), F32), pltpu.SemaphoreType.DMA],
        compiler_params=pltpu.CompilerParams(dimension_semantics=("arbitrary",), vmem_limit_bytes=SSM_VMEM_LIMIT_BYTES),
    )(dy, u, xs, bd, bd, cd, cd, a, d_row, *deps)


def _adamw_math(w, g, m, v):
    nm = ADAM_B1 * m + (1.0 - ADAM_B1) * g
    nv = ADAM_B2 * v + (1.0 - ADAM_B2) * (g * g)
    m_hat = nm / (1.0 - ADAM_B1 ** ADAM_STEP)
    v_hat = nv / (1.0 - ADAM_B2 ** ADAM_STEP)
    return -ADAM_LR * (m_hat / (jnp.sqrt(v_hat) + ADAM_EPS) + ADAM_WD * w), nm, nv


def _adamw_many(ws, gs, ms, vs, name, deps=()):
    n, nd = len(ws), len(deps)

    def body(*refs):
        outs = refs[4 * n + nd:]
        for i in range(n):
            d, nm, nv = _adamw_math(refs[i][...], refs[n + i][...], refs[2 * n + i][...], refs[3 * n + i][...])
            outs[i][...], outs[n + i][...], outs[2 * n + i][...] = d, nm, nv

    whole = pl.BlockSpec(memory_space=pltpu.VMEM)
    res = pl.pallas_call(
        body, name=name, in_specs=[whole] * (4 * n) + [pl.BlockSpec(memory_space=pl.ANY)] * nd,
        out_specs=[whole] * (3 * n), out_shape=[jax.ShapeDtypeStruct(w.shape, F32) for w in ws] * 3,
        compiler_params=pltpu.CompilerParams(vmem_limit_bytes=VMEM_LIMIT_BYTES),
    )(*ws, *gs, *ms, *vs, *deps)
    return res[:n], res[n:2 * n], res[2 * n:]


def _adamw(w, g, m, v, name, deps=(), echo=False):
    nd = len(deps)
    r, c = w.shape
    tr = r
    for cand in (512, 256, 128, 64, 32, 16, 8):
        if r % cand == 0 and cand * c * 4 <= 2 * 1024 * 1024:
            tr = cand
            break

    def body(w_ref, g_ref, m_ref, v_ref, *rest):
        d_ref, nm_ref, nv_ref = rest[nd:nd + 3]
        gv = g_ref[...]
        d_ref[...], nm_ref[...], nv_ref[...] = _adamw_math(w_ref[...], gv, m_ref[...], v_ref[...])
        if echo:
            rest[nd + 3][...] = gv

    no = 4 if echo else 3
    spec = pl.BlockSpec((tr, c), lambda i: (i, 0))
    sds = jax.ShapeDtypeStruct((r, c), F32)
    return pl.pallas_call(body, name=name, grid=(r // tr,),
                          in_specs=[spec] * 4 + [pl.BlockSpec(memory_space=pl.ANY)] * nd, out_specs=[spec] * no,
                          out_shape=[sds] * no, compiler_params=_cparams(("parallel",)))(w, g, m, v, *deps)


def _sum_lead(x, name, out_dtype=F32):
    n, r, c = x.shape
    tr = r
    for cand in (512, 256, 128, 64, 32, 16, 8):
        if r % cand == 0 and n * cand * c * 4 <= 4 * 1024 * 1024:
            tr = cand
            break

    def body(x_ref, o_ref):
        acc = x_ref[0].astype(F32)
        for k in range(1, n):
            acc = acc + x_ref[k].astype(F32)
        o_ref[...] = acc.astype(o_ref.dtype)

    return pl.pallas_call(body, name=name, grid=(r // tr,),
                          in_specs=[pl.BlockSpec((n, tr, c), lambda i: (0, i, 0))],
                          out_specs=pl.BlockSpec((tr, c), lambda i: (i, 0)),
                          out_shape=jax.ShapeDtypeStruct((r, c), out_dtype),
                          compiler_params=_cparams(("parallel",)))(x)


def _row_tile(rows, row_bytes, budget, least=8):
    for cand in (1024, 512, 256, 128, 64, 32, 16, 8):
        if cand >= least and rows % cand == 0 and cand * row_bytes <= budget:
            return cand
    return rows


def _cast_into_slot(w, slot, name):
    r, c = w.shape
    tr = _row_tile(r, c * 4, 4 * 1024 * 1024, least=16)

    def body(slot_ref, w_ref, o_ref):
        o_ref[...] = w_ref[...].astype(o_ref.dtype)

    gs = pltpu.PrefetchScalarGridSpec(
        num_scalar_prefetch=1, grid=(r // tr,),
        in_specs=[pl.BlockSpec((tr, c), lambda i, s: (i, 0))],
        out_specs=pl.BlockSpec((None, tr, c), lambda i, s: (s[0], i, 0)))
    return pl.pallas_call(body, name=name, grid_spec=gs, out_shape=jax.ShapeDtypeStruct((N_CHIPS, r, c), BF16),
                          compiler_params=_cparams(("parallel",)))(slot, w)


def _sum_own(p, t, sel, name):
    _, h, c = p.shape
    tr = _row_tile(h, c * 4, 2 * 1024 * 1024, least=16)
    nblk = h // tr

    def body(sel_ref, p_ref, t_ref, o_ref):
        acc = p_ref[...].astype(F32)
        for k in range(3):
            acc = acc + t_ref[k].astype(F32)
        o_ref[...] = acc

    gs = pltpu.PrefetchScalarGridSpec(
        num_scalar_prefetch=1, grid=(nblk,),
        in_specs=[pl.BlockSpec((None, tr, c), lambda i, s: (s[0], i, 0)),
                  pl.BlockSpec((3, tr, c), lambda i, s: (0, i, 0))],
        out_specs=pl.BlockSpec((tr, c), lambda i, s: (s[1] * nblk + i, 0)))
    return pl.pallas_call(body, name=name, grid_spec=gs, out_shape=jax.ShapeDtypeStruct((2 * h, c), F32),
                          compiler_params=_cparams(("parallel",)))(sel, p, t)


def _add_half(g, t, half, name):
    n, r, c = g.shape
    h = r // 2
    tr = h
    for cand in (512, 256, 128, 64, 32, 16):
        if h % cand == 0 and cand * c * 2 <= 2 * 1024 * 1024:
            tr = cand
            break
    nblk = h // tr

    def body(half_ref, g_ref, t_ref, o_ref):
        o_ref[...] = (g_ref[...].astype(F32) + t_ref[...].astype(F32)).astype(o_ref.dtype)

    gs = pltpu.PrefetchScalarGridSpec(
        num_scalar_prefetch=1, grid=(n, nblk),
        in_specs=[pl.BlockSpec((None, tr, c), lambda j, i, hr: (j, hr[0] * nblk + i, 0)),
                  pl.BlockSpec((None, tr, c), lambda j, i, hr: (j, i, 0))],
        out_specs=pl.BlockSpec((None, tr, c), lambda j, i, hr: (j, i, 0)))
    return pl.pallas_call(body, name=name, grid_spec=gs, out_shape=jax.ShapeDtypeStruct((n, h, c), BF16),
                          compiler_params=_cparams(("parallel", "parallel")))(half, g, t)


def _position():
    x, y, c = lax.axis_index("x"), lax.axis_index("y"), lax.axis_index("c")
    return x, y, c


def _allgather8(xs, name):
    m_per, n = xs.shape

    def body(x_ref, out_ref, send_sems, recv_sems, local_sem):
        x, y, c = _position()
        me, sibling = (x, y, c), (x, y, 1 - c)
        chips = [(1 - x, y), (x, 1 - y), (1 - x, 1 - y)]

        def rows(px, py, pc):
            return out_ref.at[pl.ds((4 * px + 2 * py + pc) * m_per, m_per), :]

        def copy(k, block, to, src=None):
            return pltpu.make_async_remote_copy(
                src_ref=rows(*block) if src is None else src, dst_ref=rows(*block),
                send_sem=send_sems.at[k], recv_sem=recv_sems.at[k], device_id=to, device_id_type=MESH)

        mine = pltpu.make_async_copy(x_ref, rows(*me), local_sem)
        mine.start()
        first = [copy(0, me, sibling, src=x_ref)]
        first += [copy(1 + j, me, (*chip, c), src=x_ref) for j, chip in enumerate(chips)]
        for cp in first:
            cp.start()
        passed = [copy(4 + j, (*chip, c), sibling) for j, chip in enumerate(chips)]
        for j, chip in enumerate(chips):
            copy(1 + j, (*chip, c), me).wait_recv()
            passed[j].start()
        copy(0, sibling, me).wait_recv()
        for j, chip in enumerate(chips):
            copy(4 + j, (*chip, 1 - c), me).wait_recv()
        for cp in first + passed:
            cp.wait_send()
        mine.wait()

    return pl.pallas_call(
        body, name=name, out_shape=jax.ShapeDtypeStruct((N_DEV * m_per, n), xs.dtype),
        in_specs=[pl.BlockSpec(memory_space=pltpu.VMEM)], out_specs=pl.BlockSpec(memory_space=pltpu.VMEM),
        scratch_shapes=[pltpu.SemaphoreType.DMA((7,)), pltpu.SemaphoreType.DMA((7,)), pltpu.SemaphoreType.DMA],
        compiler_params=pltpu.CompilerParams(vmem_limit_bytes=VMEM_LIMIT_BYTES),
    )(xs)


_HBM = pl.BlockSpec(memory_space=pltpu.HBM)


_SEM = pl.BlockSpec(memory_space=pltpu.SEMAPHORE)
_ANY = pl.BlockSpec(memory_space=pl.ANY)
_EFFECT = pltpu.SideEffectType.DATAFLOW_SIDE_EFFECTING


def _in_hbm(a):
    return pltpu.with_memory_space_constraint(a, pltpu.HBM)


def _several(after):
    return list(after) if isinstance(after, (list, tuple)) else [after]


def _gather_start(ws, groups, after, name):
    n = len(ws)
    after = _several(after)

    def body(*refs):
        in_refs = refs[:n]
        sems, token = refs[2 * n + len(after):-1], refs[-1]
        x, y, c = _position()
        mychip = 2 * x + y
        chips = [(1 - x, y), (x, 1 - y), (1 - x, 1 - y)]
        for g, members in enumerate(groups):
            for k, i in enumerate(members):
                h = ws[i].shape[1] // 2
                mine = in_refs[i].at[mychip, pl.ds(c * h, h), :]
                for j, (px, py) in enumerate(chips):
                    pltpu.make_async_remote_copy(
                        src_ref=mine, dst_ref=mine, send_sem=sems[2 * g].at[3 * k + j],
                        recv_sem=sems[2 * g + 1].at[3 * k + j], device_id=(px, py, c), device_id_type=MESH).start()
        token[...] = jnp.zeros_like(token)

    sem_shapes = [pltpu.SemaphoreType.DMA((3 * len(m),)) for m in groups for _ in range(2)]
    res = pl.pallas_call(
        body, name=name,
        out_shape=[pltpu.HBM(w.shape, w.dtype) for w in ws] + sem_shapes + [jax.ShapeDtypeStruct((8, 128), F32)],
        in_specs=[_HBM] * n + [_ANY] * len(after),
        out_specs=[_HBM] * n + [_SEM] * len(sem_shapes) + [pl.BlockSpec(memory_space=pltpu.VMEM)],
        input_output_aliases={i: i for i in range(n)},
        compiler_params=pltpu.CompilerParams(has_side_effects=_EFFECT),
    )(*[_in_hbm(w) for w in ws], *after)
    bufs, sems, token = res[:n], res[n:-1], res[-1]
    return list(bufs), [(sems[2 * g], sems[2 * g + 1]) for g in range(len(groups))], token


def _gather_wait(bufs, send_sems, recv_sems, after, name):
    m = len(bufs)

    def body(*refs):
        in_refs = refs[:m]
        send, recv = refs[m], refs[m + 1]
        x, y, c = _position()
        mychip = 2 * x + y
        chips = [(1 - x, y), (x, 1 - y), (1 - x, 1 - y)]
        for k in range(m):
            h = bufs[k].shape[1] // 2
            mine = in_refs[k].at[mychip, pl.ds(c * h, h), :]
            for j, (px, py) in enumerate(chips):
                cp = pltpu.make_async_remote_copy(
                    src_ref=mine, dst_ref=in_refs[k].at[2 * px + py, pl.ds(c * h, h), :],
                    send_sem=send.at[3 * k + j], recv_sem=recv.at[3 * k + j],
                    device_id=(px, py, c), device_id_type=MESH)
                cp.wait_send()
                cp.wait_recv()

    res = pl.pallas_call(
        body, name=name, out_shape=[pltpu.HBM(b.shape, b.dtype) for b in bufs],
        in_specs=[_HBM] * m + [_SEM, _SEM] + [_ANY] * len(_several(after)), out_specs=[_HBM] * m,
        input_output_aliases={k: k for k in range(m)},
        compiler_params=pltpu.CompilerParams(has_side_effects=_EFFECT),
    )(*bufs, send_sems, recv_sems, *_several(after))
    return list(res)


def _forward_halves(ws, name):
    n = len(ws)

    def body(*refs):
        out_refs = refs[n:2 * n]
        send_sems, recv_sems = refs[2 * n:]
        x, y, c = _position()
        me, sibling = (x, y, c), (x, y, 1 - c)
        chips = [(1 - x, y), (x, 1 - y), (1 - x, 1 - y)]
        cps = []
        for i in range(n):
            h = ws[i].shape[1] // 2
            for j, (px, py) in enumerate(chips):
                got = out_refs[i].at[2 * px + py, pl.ds(c * h, h), :]
                cp = pltpu.make_async_remote_copy(
                    src_ref=got, dst_ref=got, send_sem=send_sems.at[3 * i + j], recv_sem=recv_sems.at[3 * i + j],
                    device_id=sibling, device_id_type=MESH)
                cp.start()
                cps.append(cp)
        for i in range(n):
            h = ws[i].shape[1] // 2
            for j, (px, py) in enumerate(chips):
                other = out_refs[i].at[2 * px + py, pl.ds((1 - c) * h, h), :]
                pltpu.make_async_remote_copy(
                    src_ref=other, dst_ref=other, send_sem=send_sems.at[3 * i + j], recv_sem=recv_sems.at[3 * i + j],
                    device_id=me, device_id_type=MESH).wait_recv()
        for cp in cps:
            cp.wait_send()

    return pl.pallas_call(
        body, name=name,
        out_shape=[jax.ShapeDtypeStruct(w.shape, w.dtype) for w in ws],
        in_specs=[_HBM] * n, out_specs=[_HBM] * n, input_output_aliases={i: i for i in range(n)},
        scratch_shapes=[pltpu.SemaphoreType.DMA((3 * n,)), pltpu.SemaphoreType.DMA((3 * n,))],
    )(*ws)


def _copies_start(arrays, copies, nsem, after, name):
    n = len(arrays)
    after = _several(after)
    first = 2 * n + len(after)

    def body(*refs):
        for cp in copies(refs[:n], refs[first], refs[first + 1]):
            cp.start()
        refs[first + 2][...] = jnp.zeros_like(refs[first + 2])

    res = pl.pallas_call(
        body, name=name,
        out_shape=[pltpu.HBM(a.shape, a.dtype) for a in arrays]
        + [pltpu.SemaphoreType.DMA((nsem,)), pltpu.SemaphoreType.DMA((nsem,)), jax.ShapeDtypeStruct((8, 128), F32)],
        in_specs=[_HBM] * n + [_ANY] * len(after),
        out_specs=[_HBM] * n + [_SEM, _SEM, pl.BlockSpec(memory_space=pltpu.VMEM)],
        input_output_aliases={i: i for i in range(n)},
        compiler_params=pltpu.CompilerParams(has_side_effects=_EFFECT),
    )(*[_in_hbm(a) for a in arrays], *after)
    return list(res[:n]), res[n], res[n + 1], res[n + 2]


def _copies_wait(arrays, copies, send_sems, recv_sems, after, name):
    n = len(arrays)

    def body(*refs):
        for cp in copies(refs[:n], refs[n], refs[n + 1]):
            cp.wait_send()
            cp.wait_recv()

    res = pl.pallas_call(
        body, name=name, out_shape=[pltpu.HBM(a.shape, a.dtype) for a in arrays],
        in_specs=[_HBM] * n + [_SEM, _SEM] + [_ANY] * len(_several(after)), out_specs=[_HBM] * n,
        input_output_aliases={i: i for i in range(n)},
        compiler_params=pltpu.CompilerParams(has_side_effects=_EFFECT),
    )(*arrays, send_sems, recv_sems, *_several(after))
    return list(res)


def _scatter_copies(refs, send, recv):
    n = len(refs) // 2
    x, y, c = _position()
    chips = [(1 - x, y), (x, 1 - y), (1 - x, 1 - y)]
    return [pltpu.make_async_remote_copy(
        src_ref=refs[i].at[2 * px + py], dst_ref=refs[n + i].at[j],
        send_sem=send.at[3 * i + j], recv_sem=recv.at[3 * i + j], device_id=(px, py, c), device_id_type=MESH)
        for i in range(n) for j, (px, py) in enumerate(chips)]


def _swap_copies(refs, send, recv):
    n = len(refs) // 2
    x, y, c = _position()
    cps = []
    for i in range(n):
        h = refs[i].shape[1] // 2
        cps.append(pltpu.make_async_remote_copy(
            src_ref=refs[i].at[:, pl.ds((1 - c) * h, h), :], dst_ref=refs[n + i],
            send_sem=send.at[i], recv_sem=recv.at[i], device_id=(x, y, 1 - c), device_id_type=MESH))
    return cps


def _join_copies(refs, send, recv):
    x, y, c = _position()
    cps = []
    for i, r in enumerate(refs):
        h = r.shape[0] // 2
        mine = r.at[pl.ds(c * h, h), :]
        cps.append(pltpu.make_async_remote_copy(
            src_ref=mine, dst_ref=mine, send_sem=send.at[i], recv_sem=recv.at[i],
            device_id=(x, y, 1 - c), device_id_type=MESH))
    return cps


def _forward_copies(refs, send, recv):
    x, y, c = _position()
    chips = [(1 - x, y), (x, 1 - y), (1 - x, 1 - y)]
    cps = []
    for i, r in enumerate(refs):
        h = r.shape[1] // 2
        for j, (px, py) in enumerate(chips):
            got = r.at[2 * px + py, pl.ds(c * h, h), :]
            cps.append(pltpu.make_async_remote_copy(
                src_ref=got, dst_ref=got, send_sem=send.at[3 * i + j], recv_sem=recv.at[3 * i + j],
                device_id=(x, y, 1 - c), device_id_type=MESH))
    return cps


def _t5_buckets_block():
    qi = np.arange(BLOCK)[:, None]
    ki = np.arange(2 * BLOCK)[None, :]
    n = np.maximum(qi + BLOCK - ki, 0)
    max_exact = NUM_BUCKETS // 2
    large = max_exact + (np.log(np.maximum(n, 1) / max_exact) / np.log(MAX_DISTANCE / max_exact)
                         * (NUM_BUCKETS - max_exact)).astype(np.int32)
    large = np.minimum(large, NUM_BUCKETS - 1)
    return np.where(n < max_exact, n, large).astype(np.int32)


def _discretise(lambda_re, lambda_im, log_step, b_re, b_im):
    lam_re = jnp.minimum(lambda_re, -1e-4)
    lam_im = lambda_im
    delta = jnp.exp(log_step)
    mag = jnp.exp(lam_re * delta)
    ang = lam_im * delta
    abar_re, abar_im = mag * jnp.cos(ang), mag * jnp.sin(ang)
    num_re, num_im = abar_re - 1.0, abar_im
    den = lam_re * lam_re + lam_im * lam_im
    f_re = (num_re * lam_re + num_im * lam_im) / den
    f_im = (num_im * lam_re - num_re * lam_im) / den
    bbar_re = f_re * b_re - f_im * b_im
    bbar_im = f_re * b_im + f_im * b_re
    return abar_re, abar_im, bbar_re, bbar_im


def _ssm_params(args, cotangents, name):
    whole = pl.BlockSpec(memory_space=pltpu.VMEM)
    n_in = len(args)

    def body(*refs):
        vals = [r[...] for r in refs[:n_in]]
        if cotangents is None:
            outs = _discretise(*vals)
        else:
            outs = jax.vjp(_discretise, *vals)[1](tuple(r[...] for r in refs[n_in:n_in + 4]))
        for o_ref, o in zip(refs[-len(outs):], outs):
            o_ref[...] = o

    if cotangents is None:
        like, operands = [args[0], args[0], args[3], args[3]], list(args)
    else:
        like, operands = list(args), list(args) + list(cotangents)
    return pl.pallas_call(body, name=name, in_specs=[whole] * len(operands), out_specs=[whole] * len(like),
                          out_shape=[jax.ShapeDtypeStruct(a.shape, F32) for a in like])(*operands)


def _interleave(v, nc):
    s, w = v.shape
    return v.reshape(nc, s // nc, w).transpose(1, 0, 2).reshape(s, w)


def _deinterleave(v, nc):
    s, w = v.shape
    return v.reshape(s // nc, nc, w).transpose(1, 0, 2).reshape(s, w)


_SMALL = ("norm1_g", "b_in", "attn_sinks", "rel_bias", "lambda_re", "lambda_im", "log_step", "ssm_b_re",
          "ssm_b_im", "ssm_c_re", "ssm_c_im", "ssm_d", "b_glu", "norm2_g", "final_g")


def _pack(parts):
    rows = []
    for p in parts:
        f = p.reshape(-1).astype(F32)
        pad = (-f.shape[0]) % 128
        rows.append(jnp.pad(f, (0, pad)).reshape(-1, 128))
    out = jnp.concatenate(rows, axis=0)
    pad = (-out.shape[0]) % 256
    return jnp.pad(out, ((0, pad), (0, 0)))


def _unpack(packed, shapes):
    res, r = [], 0
    for shp in shapes:
        size = int(np.prod(shp))
        nr = -(-size // 128)
        res.append(packed[r:r + nr].reshape(-1)[:size].reshape(shp))
        r += nr
    return res


def kernel(x, c, w_ada, b_ada, norm1_g, w_in, b_in, attn_sinks, rel_bias, lambda_re, lambda_im, log_step, ssm_b_re, ssm_b_im, ssm_c_re, ssm_c_im, ssm_d, w_glu, b_glu, w_attn_proj, w_ssm_proj, w_out, norm2_g, w_ff1, w_ff2, final_g, loss_target, m_w_ada, m_b_ada, m_norm1_g, m_w_in, m_b_in, m_attn_sinks, m_rel_bias, m_lambda_re, m_lambda_im, m_log_step, m_ssm_b_re, m_ssm_b_im, m_ssm_c_re, m_ssm_c_im, m_ssm_d, m_w_glu, m_b_glu, m_w_attn_proj, m_w_ssm_proj, m_w_out, m_norm2_g, m_w_ff1, m_w_ff2, m_final_g, v_w_ada, v_b_ada, v_norm1_g, v_w_in, v_b_in, v_attn_sinks, v_rel_bias, v_lambda_re, v_lambda_im, v_log_step, v_ssm_b_re, v_ssm_b_im, v_ssm_c_re, v_ssm_c_im, v_ssm_d, v_w_glu, v_b_glu, v_w_attn_proj, v_w_ssm_proj, v_w_out, v_norm2_g, v_w_ff1, v_w_ff2, v_final_g):
    given = dict(locals())
    S, D = x.shape[1], x.shape[2]
    SSM_W = w_glu.shape[2]
    G = SSM_W // SSM_GROUP_CH
    NST = G * SSM_STATE
    DFF = w_ff2.shape[1] * N_CHIPS
    INW = w_in.shape[2] * N_CHIPS
    o_q, o_k, o_v, o_u = 0, ATTN_WIDTH, ATTN_WIDTH + KV_WIDTH, ATTN_WIDTH + 2 * KV_WIDTH
    o_ga, o_gs = o_u + SSM_W, o_u + SSM_W + D
    mx, my, mc = _position()
    my_chip = 2 * mx + my
    my_b = 4 * mx + 2 * my + mc

    xv, tgt = x[0], loss_target[0]

    big = dict(w_in=w_in[0], w_glu=w_glu[0], w_attn_proj=w_attn_proj[0], w_ssm_proj=w_ssm_proj[0],
               w_out=w_out[0], w_ff1=w_ff1[0], w_ff2=w_ff2[0])
    big_names = list(big)
    colsharded = {"w_in", "w_attn_proj", "w_ssm_proj", "w_ff1"}
    chip_sel = my_chip.astype(jnp.int32).reshape(1)
    gather_groups = [["w_in"], ["w_attn_proj", "w_ssm_proj", "w_glu", "w_out"], ["w_ff1", "w_ff2"]]
    in_flight, gather_sems, gathered = {}, [], {}

    def finish_gather(g, after):
        bufs = [in_flight[k] for k in gather_groups[g]]
        bufs = _gather_wait(bufs, gather_sems[g][0], gather_sems[g][1], after, "gather_wait_%d" % g)
        gathered.update(zip(gather_groups[g], _forward_halves(bufs, "gather_forward_%d" % g)))

    def tied(v, token):
        return v + token[0:1, 0:1]

    def all_of(*arrays):
        return list(arrays)

    def wop(k):
        g = gathered[k]
        return _Op(g, N_CHIPS) if k in colsharded else _Op(g.reshape(g.shape[0] * g.shape[1], g.shape[2]))

    grads = {}
    nothing = jnp.zeros((8, 128), F32)
    half = mc.astype(jnp.int32).reshape(1)
    sel = jnp.stack([my_chip, mc]).astype(jnp.int32)

    def rs_swap(tag, named):
        keys, gl = list(named), []
        for k in keys:
            gk = named[k]
            if k not in colsharded:
                gk = gk.reshape(N_CHIPS, gk.shape[0] // N_CHIPS, gk.shape[1])
            gl.append(gk)
        lands = [lax.empty((g.shape[0], g.shape[1] // 2, g.shape[2]), g.dtype) for g in gl]
        arrays, ssem, rsem, token = _copies_start(gl + lands, _swap_copies, len(gl), nothing, "rs_swap_start_" + tag)
        return (keys, arrays, ssem, rsem), token

    def rs_scatter(tag, state, after):
        keys, arrays, ssem, rsem = state
        arrays = _copies_wait(arrays, _swap_copies, ssem, rsem, after, "rs_swap_wait_" + tag)
        n = len(keys)
        ps = [_add_half(g, t, half, "rs_add_" + k) for g, t, k in zip(arrays[:n], arrays[n:], keys)]
        lands = [lax.empty((3,) + p.shape[1:], p.dtype) for p in ps]
        arrays, ssem, rsem, token = _copies_start(ps + lands, _scatter_copies, 3 * n, nothing, "rs_start_" + tag)
        return (keys, arrays, ssem, rsem), token

    def rs_sum(tag, state, after):
        keys, arrays, ssem, rsem = state
        arrays = _copies_wait(arrays, _scatter_copies, ssem, rsem, after, "rs_wait_" + tag)
        n = len(keys)
        rs = [_sum_own(p, t, sel, "rs_sum_" + k) for p, t, k in zip(arrays[:n], arrays[n:], keys)]
        rs, ssem, rsem, token = _copies_start(rs, _join_copies, n, nothing, "rs_join_start_" + tag)
        return (keys, rs, ssem, rsem), token

    def rs_finish(tag, state, after):
        keys, rs, ssem, rsem = state
        for k, f in zip(keys, _copies_wait(rs, _join_copies, ssem, rsem, after, "rs_join_wait_" + tag)):
            grads[k] = f[None]

    c_all = _allgather8(jnp.pad(c, ((0, 7), (0, 0))), "gather_c").reshape(N_DEV, 8, D)[:, 0]
    c16 = jnp.pad(c_all, ((0, 8), (0, 0)))
    b_ada_mine = lax.dynamic_slice(b_ada.reshape(N_CHIPS, -1), (my_chip, 0), (1, w_ada.shape[2]))
    mod_sh = _mm(c16, w_ada[0], "NN", name="mod", M=16, N=w_ada.shape[2], K=D, a_fn=_silu,
                 epilogue=lambda acc, b: (acc + b,), extras=[(b_ada_mine, "row")])
    mod_all = _allgather8(mod_sh[:8], "gather_mod").reshape(N_DEV, 8, -1)
    mod_row = jnp.concatenate(
        [lax.dynamic_slice(mod_all, (2 * j, my_b, 0), (1, 1, mod_all.shape[2]))[0] for j in range(N_CHIPS)], axis=1)
    sh1, sc1, g1, sh2, sc2, g2 = [mod_row[:, i * D:(i + 1) * D] for i in range(6)]

    first = [_cast_into_slot(big["w_in"], chip_sel, "cast_w_in")]
    first, sems_first, token_first = _gather_start(first, [[0]], mod_all, "gather_start_in")
    rest_names = gather_groups[1] + gather_groups[2]
    rest = [_cast_into_slot(big[k], chip_sel, "cast_" + k) for k in rest_names]
    rest, sems_rest, token_rest = _gather_start(
        rest, [[rest_names.index(k) for k in grp] for grp in gather_groups[1:]], token_first, "gather_start_rest")
    in_flight.update(zip(["w_in"] + rest_names, first + rest))
    gather_sems.extend(sems_first + sems_rest)

    disc_in = (lambda_re[0].reshape(1, NST), lambda_im[0].reshape(1, NST),
               jnp.repeat(log_step[0], SSM_STATE).reshape(1, NST),
               ssm_b_re[0].transpose(2, 0, 1).reshape(SSM_GROUP_CH, NST),
               ssm_b_im[0].transpose(2, 0, 1).reshape(SSM_GROUP_CH, NST))
    abar_re, abar_im, bbar_re, bbar_im = _ssm_params(disc_in, None, "ssm_params")
    same_group = jnp.asarray(np.arange(SSM_W)[:, None] // SSM_GROUP_CH == np.arange(NST)[None, :] // SSM_STATE)

    def block_diag(t):
        return jnp.where(same_group, jnp.tile(t, (G, 1)), 0.0)

    bd = jnp.concatenate([block_diag(bbar_re), block_diag(bbar_im)], axis=1)
    cd = jnp.concatenate([block_diag(cc.transpose(1, 0, 2).reshape(SSM_GROUP_CH, NST)).T
                          for cc in (ssm_c_re[0], -ssm_c_im[0])], axis=0)
    a_fwd = jnp.stack([abar_re, abar_im])
    a_bwd = jnp.stack([abar_re, -abar_im])
    d_row = ssm_d

    buckets = _t5_buckets_block()
    onehot_t = (jnp.arange(128, dtype=jnp.int32)[:, None] == jnp.asarray(buckets.reshape(1, -1))).astype(BF16)
    rb_hi = rel_bias.astype(BF16)
    rb_lo = (rel_bias - rb_hi.astype(F32)).astype(BF16)
    rb_lo2 = (rel_bias - rb_hi.astype(F32) - rb_lo.astype(F32)).astype(BF16)
    rb3 = jnp.pad(jnp.concatenate([rb_hi.T, rb_lo.T, rb_lo2.T], axis=0), ((0, 0), (0, 128 - NUM_BUCKETS)))
    b3 = _mm(rb3, onehot_t, "NN", name="rel_bias_rows", M=3 * N_Q_HEADS, N=BLOCK * 2 * BLOCK, K=128, tj=4096)
    bias = (b3[:N_Q_HEADS] + b3[N_Q_HEADS:2 * N_Q_HEADS]) + b3[2 * N_Q_HEADS:]
    bias = bias.reshape(N_Q_HEADS, BLOCK, 2 * BLOCK)
    sinks_b = jnp.broadcast_to(attn_sinks[0][:, None, None], (N_Q_HEADS, BLOCK, 128)).reshape(N_Q_HEADS * BLOCK, 128)

    def two(fn):
        def both(*blocks):
            r = fn(*blocks)
            return r, r
        return both

    h1, h1_t = _rowwise(two(_norm_mod), [(xv, "tile", D), (tied(tied(norm1_g, token_first), token_rest), "row", D),
                                         (sh1, "row", D), (sc1, "row", D)],
                        [(D, BF16), (D, BF16, "T")], [], name="norm1", rows=S)
    finish_gather(0, all_of(h1, bd, cd, a_fwd, a_bwd, bias, sinks_b))
    proj = _mm(h1, wop("w_in"), "NN", name="proj", M=S, N=INW, K=D, out_dtypes=(BF16,),
               epilogue=lambda acc, b: (acc + b,), extras=[(b_in, "row")])

    def heads(v2d, nh):
        return v2d.reshape(S, nh, HEAD_DIM).transpose(1, 0, 2)

    def unheads(v3d):
        return v3d.transpose(1, 0, 2).reshape(S, -1)

    qh = heads(proj[:, o_q:o_k], N_Q_HEADS)
    kh = heads(proj[:, o_k:o_v], N_KV_HEADS)
    vh = heads(proj[:, o_v:o_u], N_KV_HEADS)
    attn = unheads(_attn_fwd(qh, kh, vh, sinks_b, bias, "attn_fwd"))
    finish_gather(1, attn)
    y_attn = _mm(attn, wop("w_attn_proj"), "NN", name="attn_proj", M=S, N=D, K=ATTN_WIDTH, out_dtypes=(BF16,))

    u = proj[:, o_u:o_ga]
    u_il = _interleave(u, SCAN_CHUNKS)
    SB = 128
    nsb, gpb = SSM_W // SB, SB // SSM_GROUP_CH
    SBN = gpb * SSM_STATE
    y_il, xs = _ssm_fwd(u_il, bd, cd, a_fwd, d_row, name="ssm_fwd", sb=SB, sbn=SBN)
    y = _deinterleave(y_il, SCAN_CHUNKS)
    z, t_glu = _mm(y, wop("w_glu"), "NN", name="glu", M=S, N=SSM_W, K=SSM_W, out_dtypes=(BF16, F32), a_fn=_gelu,
                   epilogue=lambda acc, b, yy: (_gelu(yy) * _sigmoid(acc + b), acc + b),
                   extras=[(b_glu, "row"), (y, "tile")])
    y_ssm = _mm(z, wop("w_ssm_proj"), "NN", name="ssm_proj", M=S, N=D, K=SSM_W, out_dtypes=(BF16,))

    ff_bufs = _gather_wait([in_flight[k] for k in gather_groups[2]], gather_sems[2][0], gather_sems[2][1], all_of(y_ssm),
                           "gather_wait_2")
    ff_bufs, ff_send, ff_recv, token = _copies_start(ff_bufs, _forward_copies, 3 * len(ff_bufs), nothing,
                                                    "gather_forward_2_start")
    merged, merged_t = _rowwise(two(_merge), [(_Op(proj, coff=o_ga), "tile", D), (_Op(proj, coff=o_gs), "tile", D),
                                              (y_attn, "tile", D), (y_ssm, "tile", D)],
                                [(D, BF16), (D, BF16, "T")], [], name="merge", rows=S)
    mo, x2 = _mm(merged, wop("w_out"), "NN", name="out_proj", M=S, N=D, K=D, out_dtypes=(BF16, F32),
                 epilogue=lambda acc, xx, gg: (acc, xx + gg * acc), extras=[(xv, "tile"), (g1, "row")], deps=[token])
    h2, h2_t = _rowwise(two(_norm_mod), [(x2, "tile", D), (norm2_g, "row", D), (sh2, "row", D), (sc2, "row", D)],
                        [(D, BF16), (D, BF16, "T")], [], name="norm2", rows=S)
    gathered.update(zip(gather_groups[2], _copies_wait(ff_bufs, _forward_copies, ff_send, ff_recv, h2,
                                                       "gather_forward_2_wait")))
    a_b, r_b = _mm(h2, wop("w_ff1"), "NN", name="ff1", M=S, N=DFF, K=D, out_dtypes=(BF16, BF16),
                   epilogue=lambda acc: (acc, jnp.square(jnp.maximum(acc, 0.0))))
    ff, x3 = _mm(r_b, wop("w_ff2"), "NN", name="ff2", M=S, N=D, K=DFF, out_dtypes=(BF16, F32),
                 epilogue=lambda acc, xx, gg: (acc, xx + gg * acc), extras=[(x2, "tile"), (g2, "row")],
                 tj=1024, tk=1024)

    def final_fn(x3b, gf, tb, ffb, g2b):
        def f(xx, gg):
            yv = xx * lax.rsqrt(jnp.mean(xx * xx, axis=-1, keepdims=True) + EPS) * gg
            err = jnp.square(yv - tb)
            return 0.5 * jnp.sum(jnp.mean(err, axis=-1, keepdims=True), axis=0, keepdims=True)
        lv, vjp = jax.vjp(f, x3b, gf)
        dx, dg = vjp(jnp.ones((1, 1), F32))
        return dx, dx * g2b, dg, jnp.broadcast_to(lv, (1, 128)), jnp.sum(dx * ffb, axis=0, keepdims=True)

    dx3, dff, g_final, loss_acc, d_g2 = _rowwise(
        final_fn, [(x3, "tile", D), (final_g.reshape(1, D), "row", D), (tgt, "tile", D), (ff, "tile", D), (g2, "row", D)],
        [(D, F32), (D, BF16)], [D, 128, D], name="final", rows=S)
    da = _mm(dff, wop("w_ff2"), "NT", name="ff2_dx", M=S, N=DFF, K=D, out_dtypes=(BF16,),
             epilogue=lambda acc, ab: (acc * (2.0 * jnp.maximum(ab.astype(F32), 0.0)),), extras=[(a_b, "tile")])
    g_w_ff2 = _mm(r_b, dff, "TN", name="ff2_dw", M=DFF, N=D, K=S, out_dtypes=(BF16,), tj=1024, tk=2048)
    g_w_ff1 = _mm(h2_t, da, "NN", name="ff1_dw", M=D, N=DFF, K=S, out_dtypes=(BF16,), out_nsh=N_CHIPS, tj=1024, tk=2048)
    rs_ff, token = rs_swap("ff", dict(w_ff2=g_w_ff2, w_ff1=g_w_ff1))
    dh2 = _mm(da, wop("w_ff1"), "NT", name="ff1_dx", M=S, N=D, K=DFF, tj=1024, tk=2048, deps=[token])
    rs_ff, token_ff = rs_scatter("ff", rs_ff, dh2)

    def norm2_bwd(x2b, dh2b, dx3b, mob, gn, shb, scb, g1b):
        _, vjp = jax.vjp(_norm_mod, x2b, gn, shb, scb)
        dx, dg, dsh, dsc = vjp(dh2b)
        dx2b = dx + dx3b
        return dx2b, dx2b * g1b, dg, dsh, dsc, jnp.sum(dx2b * mob, axis=0, keepdims=True)

    dx2, dmo, g_norm2, d_sh2, d_sc2, d_g1 = _rowwise(
        norm2_bwd, [(x2, "tile", D), (dh2, "tile", D), (dx3, "tile", D), (mo, "tile", D),
                    (tied(norm2_g, token_ff), "row", D), (sh2, "row", D), (sc2, "row", D), (g1, "row", D)],
        [(D, F32), (D, BF16)], [D, D, D, D], name="norm2_bwd", rows=S)
    dmerged = _mm(dmo, wop("w_out"), "NT", name="out_dx", M=S, N=D, K=D)
    g_w_out = _mm(merged_t, dmo, "NN", name="out_dw", M=D, N=D, K=S, out_dtypes=(BF16,), tj=1024, tk=2048)

    def merge_bwd(gab, gsb, yab, ysb, dmb):
        _, vjp = jax.vjp(_merge, gab, gsb, yab, ysb)
        return vjp(dmb)

    d_ga, d_gs, dy_attn, dy_ssm = _rowwise(
        merge_bwd, [(_Op(proj, coff=o_ga), "tile", D), (_Op(proj, coff=o_gs), "tile", D), (y_attn, "tile", D),
                    (y_ssm, "tile", D), (dmerged, "tile", D)],
        [(D, BF16), (D, BF16), (D, BF16), (D, BF16)], [], name="merge_bwd", rows=S)

    dattn = _mm(dy_attn, wop("w_attn_proj"), "NT", name="attn_proj_dx", M=S, N=ATTN_WIDTH, K=D, tj=1024, out_dtypes=(BF16,))
    g_w_attn_proj = _mm(attn, dy_attn, "TN", name="attn_proj_dw", M=ATTN_WIDTH, N=D, K=S, out_dtypes=(BF16,),
                        out_nsh=N_CHIPS, tk=1024)

    dz = _mm(dy_ssm, wop("w_ssm_proj"), "NT", name="ssm_proj_dx", M=S, N=SSM_W, K=D)
    g_w_ssm_proj = _mm(z, dy_ssm, "TN", name="ssm_proj_dw", M=SSM_W, N=D, K=S, out_dtypes=(BF16,),
                       out_nsh=N_CHIPS, tk=1024)

    def glu_bwd(dzb, yb, tb):
        z0 = _gelu(yb)
        sg = _sigmoid(tb)
        dt = dzb * z0 * sg * (1.0 - sg)
        return dt, dzb * sg, jnp.sum(dt, axis=0, keepdims=True)

    dt_b, dz0a, g_b_glu = _rowwise(glu_bwd, [(dz, "tile", SSM_W), (y, "tile", SSM_W), (t_glu, "tile", SSM_W)],
                                   [(SSM_W, BF16), (SSM_W, F32)], [SSM_W], name="glu_bwd", rows=S)

    def gelu_bwd(acc, dz0ab, yb):
        _, vjp = jax.vjp(_gelu, yb)
        return (vjp(acc + dz0ab)[0],)

    dy = _mm(dt_b, wop("w_glu"), "NT", name="glu_dx", M=S, N=SSM_W, K=SSM_W, epilogue=gelu_bwd,
             extras=[(dz0a, "tile"), (y, "tile")])
    g_w_glu = _mm(y, dt_b, "TN", name="glu_dw", M=SSM_W, N=SSM_W, K=S, out_dtypes=(BF16,), tk=1024, a_fn=_gelu)
    rs_mix, token = rs_swap("mix", dict(w_out=g_w_out, w_attn_proj=g_w_attn_proj, w_ssm_proj=g_w_ssm_proj,
                                        w_glu=g_w_glu))
    dy_il = _interleave(dy, SCAN_CHUNKS)
    du_il, g_bd, g_cd, d_abar, g_ssm_d = _ssm_bwd(dy_il, u_il, xs, bd, cd, a_bwd, d_row, name="ssm_bwd", sb=SB, sbn=SBN,
                                                  deps=[token])
    du = _deinterleave(du_il, SCAN_CHUNKS)
    rs_mix, token_mix = rs_scatter("mix", rs_mix, du_il)

    dqh, dkh, dvh, dsink_blk, dbias = _attn_bwd(qh, kh, vh, heads(dattn, N_Q_HEADS), tied(sinks_b, token_mix), bias,
                                                "attn_bwd")
    g_sinks = _sum_lead(dsink_blk.reshape(N_Q_HEADS, BLOCK, 128).transpose(1, 0, 2), "sinks_dw")[:, 0].reshape(1, N_Q_HEADS)
    g_rel = _mm(dbias.reshape(N_Q_HEADS, -1), onehot_t, "NT", name="rel_bias_dw", M=N_Q_HEADS, N=128,
                K=BLOCK * 2 * BLOCK, tk=4096)
    g_rel_bias = g_rel[:, :NUM_BUCKETS].T

    eye_b = jnp.eye(gpb, dtype=F32)
    g_cd6 = g_cd.reshape(2, nsb, gpb, SSM_STATE, gpb, SSM_GROUP_CH)
    g_c_re = jnp.einsum("bgnhp,gh->bgpn", g_cd6[0], eye_b).reshape(G, SSM_GROUP_CH, SSM_STATE)
    g_c_im = -jnp.einsum("bgnhp,gh->bgpn", g_cd6[1], eye_b).reshape(G, SSM_GROUP_CH, SSM_STATE)
    g_bd6 = g_bd.reshape(nsb, gpb, SSM_GROUP_CH, 2, gpb, SSM_STATE)
    g_bbar = jnp.einsum("bhprgn,hg->rpbhn", g_bd6, eye_b).reshape(2, SSM_GROUP_CH, NST)
    g_lre, g_lim, g_lstep, g_bre, g_bim = _ssm_params(disc_in, (d_abar[0], d_abar[1], g_bbar[0], g_bbar[1]),
                                                      "ssm_params_bwd")
    g_lre, g_lim = g_lre.reshape(G, SSM_STATE), g_lim.reshape(G, SSM_STATE)
    g_lstep = g_lstep.reshape(G, SSM_STATE).sum(axis=1)
    g_bre = g_bre.reshape(SSM_GROUP_CH, G, SSM_STATE).transpose(1, 2, 0)
    g_bim = g_bim.reshape(SSM_GROUP_CH, G, SSM_STATE).transpose(1, 2, 0)

    dproj = jnp.concatenate([unheads(dqh).astype(BF16), unheads(dkh).astype(BF16), unheads(dvh).astype(BF16),
                             du.astype(BF16), d_ga, d_gs], axis=1)
    g_w_in = _mm(h1_t, dproj, "NN", name="proj_dw", M=D, N=INW, K=S, out_dtypes=(BF16,), out_nsh=N_CHIPS,
                 tj=INW // (2 * N_CHIPS), tk=2048)
    rs_in, token = rs_swap("in", dict(w_in=g_w_in))
    dh1 = _mm(dproj, wop("w_in"), "NT", name="proj_dx", M=S, N=D, K=INW, tj=1024, tk=INW // N_CHIPS, deps=[token])
    g_b_in = _rowwise(lambda d: (jnp.sum(d.astype(F32), axis=0, keepdims=True),), [(dproj, "tile", INW)], [], [INW],
                      name="proj_db", rows=S)[0]

    def norm1_bwd(xb, dhb, dresb, gn, shb, scb):
        _, vjp = jax.vjp(_norm_mod, xb, gn, shb, scb)
        dx, dg, dsh, dsc = vjp(dhb)
        return dx + dresb, dg, dsh, dsc

    grad_x, g_norm1, d_sh1, d_sc1 = _rowwise(
        norm1_bwd, [(xv, "tile", D), (dh1, "tile", D), (dx2, "tile", D), (norm1_g, "row", D),
                    (sh1, "row", D),
                    (sc1, "row", D)], [(D, F32)], [D, D, D], name="norm1_bwd", rows=S)

    dmod_row = jnp.concatenate([d_sh1, d_sc1, d_g1, d_sh2, d_sc2, d_g2], axis=1)
    small_g = dict(norm1_g=g_norm1, b_in=g_b_in, attn_sinks=g_sinks, rel_bias=g_rel_bias, lambda_re=g_lre[None],
                   lambda_im=g_lim[None], log_step=g_lstep[None], ssm_b_re=g_bre[None], ssm_b_im=g_bim[None],
                   ssm_c_re=g_c_re[None], ssm_c_im=g_c_im[None], ssm_d=g_ssm_d, b_glu=g_b_glu, norm2_g=g_norm2,
                   final_g=g_final.reshape(D))
    packed = _pack([dmod_row, loss_acc[:, :1]] + [small_g[k] for k in _SMALL])
    rows = packed.shape[0]
    gathered = _allgather8(packed, "gather_small").reshape(N_DEV, rows, 128)
    summed = _sum_lead(gathered, "small_sum")
    parts = _unpack(summed, [dmod_row.shape, (1,)] + [given[k].shape for k in _SMALL])
    loss = parts[1].reshape(())
    grads.update(zip(_SMALL, parts[2:]))
    grads["b_ada"] = parts[0]

    dmod_all = gathered[:, :dmod_row.shape[1] // 128].reshape(N_DEV, -1)
    dmod_mine = lax.dynamic_slice(dmod_all.reshape(N_DEV, N_CHIPS, -1), (0, my_chip, 0), (N_DEV, 1, w_ada.shape[2]))[:, 0]
    g_w_ada = _mm(c16, jnp.pad(dmod_mine, ((0, 8), (0, 0))), "TN", name="ada_dw", M=D, N=w_ada.shape[2], K=16,
                  a_fn=_silu)
    grads["w_ada"] = g_w_ada[None]

    deltas, new_m, new_v = {}, {}, {}

    def adamw_big(k, deps=()):
        echo = k in big_names
        res = _adamw(given[k][0], grads[k][0], given["m_" + k][0], given["v_" + k][0], "adamw_" + k, deps, echo)
        deltas[k], new_m[k], new_v[k] = res[0][None], res[1][None], res[2][None]
        if echo:
            grads[k] = res[3][None]
        return res[2]

    rs_in, token_in = rs_scatter("in", rs_in, all_of(summed, dmod_all))
    rs_ff, token = rs_sum("ff", rs_ff, all_of(summed, token_in))
    mark = adamw_big("w_ada", [token])
    rs_mix, token = rs_sum("mix", rs_mix, mark)
    small_all = list(_SMALL) + ["b_ada"]
    for k in small_all:
        grads[k] = grads[k].reshape(given[k].shape)

    def rows_of(a):
        return a.reshape(1, -1) if a.ndim == 1 else a

    d_, m_, v_ = _adamw_many(*[[rows_of(src[k]) for k in small_all] for src in (
        given, grads, {k: given["m_" + k] for k in small_all}, {k: given["v_" + k] for k in small_all})],
        "adamw_small", [token])
    for k, dd, mm, vv in zip(small_all, d_, m_, v_):
        deltas[k], new_m[k], new_v[k] = (t.reshape(given[k].shape) for t in (dd, mm, vv))
    v_ = v_[0]
    rs_finish("ff", rs_ff, v_)
    marks = [adamw_big(k) for k in ("w_ff2", "w_ff1")]
    rs_finish("mix", rs_mix, all_of(*marks))
    marks = [adamw_big(k) for k in ("w_out", "w_attn_proj", "w_ssm_proj", "w_glu")]
    rs_in, token = rs_sum("in", rs_in, all_of(*marks))
    rs_finish("in", rs_in, token)
    adamw_big("w_in")

    names = ["w_ada", "b_ada", "norm1_g", "w_in", "b_in", "attn_sinks", "rel_bias", "lambda_re", "lambda_im",
             "log_step", "ssm_b_re", "ssm_b_im", "ssm_c_re", "ssm_c_im", "ssm_d", "w_glu", "b_glu", "w_attn_proj",
             "w_ssm_proj", "w_out", "norm2_g", "w_ff1", "w_ff2", "final_g"]
    return (loss, grad_x[None], *[grads[n] for n in names], *[deltas[n] for n in names],
            *[new_m[n] for n in names], *[new_v[n] for n in names])
```
